```python
import jax, jax.numpy as jnp
from jax import lax
import numpy as np

D_MODEL = 2048
BATCH = 8
SEQ = 8192
DEPTH = 1

D_MIX = 2 * D_MODEL
D_A = D_MIX // 2
CHUNK_A = 128
N_GROUPS_A = 16
D_HEAD_A = D_A // N_GROUPS_A
D_SSM = D_MIX - D_A
SSM_HEAD_DIM = 64
N_SSM_HEADS = D_SSM // SSM_HEAD_DIM
N_SSM_GROUPS = 8
HEADS_PER_GROUP = N_SSM_HEADS // N_SSM_GROUPS
D_STATE = 128
SSM_CONV = 4
SSD_CHUNK = 128
D_XBC = D_SSM + 2 * N_SSM_GROUPS * D_STATE
D_IN = 2 * D_A + D_SSM + D_XBC + N_SSM_HEADS
D_FF = 5632
FFN_CONV = 3
D_PLE = 256
EPS = 1e-6

kernel_name = "hybrid_gmlp_ssd_convffn_ple"


def rms_norm(x, g):
    xf = x.astype(jnp.float32)
    y = xf * lax.rsqrt(jnp.mean(xf * xf, axis=-1, keepdims=True) + EPS)
    return (y * g.astype(jnp.float32)).astype(x.dtype)


def causal_dwconv(x, w, b):
    k_w = w.shape[0]
    s = x.shape[1]
    xp = jnp.pad(x, ((0, 0), (k_w - 1, 0), (0, 0)))
    y = b
    for k in range(k_w):
        y = y + xp[:, k:k + s] * w[k]
    return y


def gmlp_chunk_mixer(uv, ln_g, ln_b, w_s, b_s):
    bn, s, _ = uv.shape
    uv = jax.nn.gelu(uv)
    u, v = jnp.split(uv, 2, axis=-1)
    v = v.reshape(bn, s // CHUNK_A, CHUNK_A, N_GROUPS_A, D_HEAD_A)
    vf = v.astype(jnp.float32)
    mu = jnp.mean(vf, axis=-1, keepdims=True)
    var = jnp.mean(jnp.square(vf - mu), axis=-1, keepdims=True)
    g = ln_g.reshape(N_GROUPS_A, D_HEAD_A).astype(jnp.float32)
    bb = ln_b.reshape(N_GROUPS_A, D_HEAD_A).astype(jnp.float32)
    vn = ((vf - mu) * lax.rsqrt(var + EPS) * g + bb).astype(uv.dtype)
    mask = jnp.tril(jnp.ones((CHUNK_A, CHUNK_A), dtype=bool))
    ws = jnp.where(mask[None], w_s, jnp.zeros_like(w_s))
    sg = jnp.einsum('gts,bcsgd->bctgd', ws, vn) + b_s.T[:, :, None]
    return u * sg.reshape(bn, s, D_A)


def ssd_chunked(xs, dt, a, bm, cm, d_skip):
    bn, s, _ = xs.shape
    nc = s // SSD_CHUNK
    shp = (bn, nc, SSD_CHUNK, N_SSM_GROUPS, HEADS_PER_GROUP)
    x5 = xs.reshape(shp + (SSM_HEAD_DIM,))
    dt5 = dt.reshape(shp)
    b5 = bm.reshape(bn, nc, SSD_CHUNK, N_SSM_GROUPS, D_STATE)
    c5 = cm.reshape(bn, nc, SSD_CHUNK, N_SSM_GROUPS, D_STATE)
    a_dt = dt5 * a.reshape(N_SSM_GROUPS, HEADS_PER_GROUP)
    x_dt = x5 * dt5[..., None]
    a_cs = jnp.cumsum(a_dt, axis=2)
    a_t = jnp.moveaxis(a_cs, 2, -1)
    diff = a_t[..., :, None] - a_t[..., None, :]
    mask = jnp.tril(jnp.ones((SSD_CHUNK, SSD_CHUNK), dtype=bool))
    l_mat = jnp.exp(jnp.where(mask, diff, -jnp.inf))
    scores = jnp.einsum('bclgn,bcsgn->bcgls', c5, b5)
    y_diag = jnp.einsum('bcgrls,bcsgrp->bclgrp', scores[:, :, :, None] * l_mat, x_dt)
    decay_states = jnp.exp(a_cs[:, :, -1:] - a_cs)
    states = jnp.einsum('bclgn,bclgrp->bcgrpn', b5, x_dt * decay_states[..., None])
    chunk_decay = jnp.exp(a_cs[:, :, -1])

    def step(h, inp):
        dec, st = inp
        return dec[..., None, None] * h + st, h

    h0 = jnp.zeros((bn, N_SSM_GROUPS, HEADS_PER_GROUP, SSM_HEAD_DIM, D_STATE), jnp.float32)
    _, h_in = lax.scan(step, h0, (jnp.moveaxis(chunk_decay, 1, 0), jnp.moveaxis(states, 1, 0)))
    h_in = jnp.moveaxis(h_in, 0, 1)
    y_off = jnp.einsum('bclgn,bcgrpn->bclgrp', c5, h_in) * jnp.exp(a_cs)[..., None]
    y = y_diag + y_off + d_skip.reshape(N_SSM_GROUPS, HEADS_PER_GROUP, 1) * x5
    return y.reshape(bn, s, D_SSM)


def mamba2_mixer(z, xbc, dt_raw, conv_w, conv_b, dt_bias, a_log, d_skip, norm_g):
    bn, s, _ = z.shape
    xbc = jax.nn.silu(causal_dwconv(xbc, conv_w, conv_b))
    n_bc = N_SSM_GROUPS * D_STATE
    xs = xbc[..., :D_SSM].astype(jnp.float32)
    bm = xbc[..., D_SSM:D_SSM + n_bc].astype(jnp.float32)
    cm = xbc[..., D_SSM + n_bc:].astype(jnp.float32)
    dt = jax.nn.softplus(dt_raw.astype(jnp.float32) + dt_bias.astype(jnp.float32))
    a = -jnp.exp(a_log.astype(jnp.float32))
    y = ssd_chunked(xs, dt, a, bm, cm, d_skip.astype(jnp.float32))
    y = y * jax.nn.silu(z.astype(jnp.float32))
    yg = y.reshape(bn, s, N_SSM_GROUPS, D_SSM // N_SSM_GROUPS)
    yg = yg * lax.rsqrt(jnp.mean(yg * yg, axis=-1, keepdims=True) + EPS)
    y = yg.reshape(bn, s, D_SSM) * norm_g.astype(jnp.float32)
    return y.astype(z.dtype)


def _fwd_setup_inputs(seed: int = 0) -> dict:
    key = jax.random.key(seed)
    ks = jax.random.split(key, 32)
    f32 = jnp.float32
    nrm = lambda k, shape, scale: jax.random.normal(k, shape, f32) * scale
    gain = lambda k, n: 1.0 + 0.05 * jax.random.normal(k, (DEPTH, n), f32)
    dt0 = jnp.exp(jax.random.uniform(ks[5], (DEPTH, N_SSM_HEADS), f32,
                                     np.log(1e-3).astype(np.float32), np.log(1e-1).astype(np.float32)))
    dt_bias = dt0 + jnp.log(-jnp.expm1(-dt0))
    return {
        "x": jax.random.normal(ks[0], (BATCH, SEQ, D_MODEL), f32),
        "p": jax.random.normal(ks[1], (DEPTH, BATCH, SEQ, D_PLE), f32),
        "norm_mix_g": gain(ks[2], D_MODEL),
        "w_in": nrm(ks[3], (DEPTH, D_MODEL, D_IN), D_MODEL ** -0.5),
        "ln_a_g": gain(ks[4], D_A),
        "ln_a_b": nrm(ks[6], (DEPTH, D_A), 0.02),
        "w_s": nrm(ks[7], (DEPTH, N_GROUPS_A, CHUNK_A, CHUNK_A), CHUNK_A ** -0.5),
        "b_s": 1.0 + nrm(ks[8], (DEPTH, N_GROUPS_A, CHUNK_A), 0.1),
        "norm_a_g": gain(ks[9], D_A),
        "conv_ssm_w": nrm(ks[10], (DEPTH, SSM_CONV, D_XBC), SSM_CONV ** -0.5),
        "conv_ssm_b": nrm(ks[11], (DEPTH, D_XBC), 0.02),
        "dt_bias": dt_bias,
        "a_log": jnp.log(jax.random.uniform(ks[12], (DEPTH, N_SSM_HEADS), f32, 1.0, 16.0)),
        "d_skip": 1.0 + nrm(ks[13], (DEPTH, N_SSM_HEADS), 0.1),
        "ssm_norm_g": gain(ks[14], D_SSM),
        "w_out": nrm(ks[15], (DEPTH, D_MIX, D_MODEL), D_MIX ** -0.5),
        "norm_ffn_g": gain(ks[16], D_MODEL),
        "w_up": nrm(ks[17], (DEPTH, D_MODEL, 2 * D_FF), D_MODEL ** -0.5),
        "conv_ffn_w": nrm(ks[18], (DEPTH, FFN_CONV, 2 * D_FF), FFN_CONV ** -0.5),
        "conv_ffn_b": nrm(ks[19], (DEPTH, 2 * D_FF), 0.02),
        "w_down": nrm(ks[20], (DEPTH, D_FF, D_MODEL), D_FF ** -0.5),
        "norm_ple_g": gain(ks[21], D_MODEL),
        "w_ple_gate": nrm(ks[22], (DEPTH, D_MODEL, D_MODEL), D_MODEL ** -0.5),
        "w_ple": nrm(ks[23], (DEPTH, D_PLE, D_MODEL), D_PLE ** -0.5),
        "norm_final_g": 1.0 + 0.05 * jax.random.normal(ks[24], (D_MODEL,), f32),
    }


def _fwd_reference(x, p, norm_mix_g, w_in, ln_a_g, ln_a_b, w_s, b_s, norm_a_g,
              conv_ssm_w, conv_ssm_b, dt_bias, a_log, d_skip, ssm_norm_g, w_out,
              norm_ffn_g, w_up, conv_ffn_w, conv_ffn_b, w_down,
              norm_ple_g, w_ple_gate, w_ple, norm_final_g):
    h = x
    o_z = 2 * D_A
    o_xbc = o_z + D_SSM
    o_dt = o_xbc + D_XBC
    for i in range(DEPTH):
        a = rms_norm(h, norm_mix_g[i])
        proj = a @ w_in[i]
        y_a = gmlp_chunk_mixer(proj[..., :o_z], ln_a_g[i], ln_a_b[i], w_s[i], b_s[i])
        y_a = rms_norm(y_a, norm_a_g[i])
        y_b = mamba2_mixer(proj[..., o_z:o_xbc], proj[..., o_xbc:o_dt], proj[..., o_dt:],
                           conv_ssm_w[i], conv_ssm_b[i], dt_bias[i], a_log[i], d_skip[i],
                           ssm_norm_g[i])
        h = h + jnp.concatenate([y_a, y_b], axis=-1) @ w_out[i]
        f = rms_norm(h, norm_ffn_g[i])
        hid = causal_dwconv(f @ w_up[i], conv_ffn_w[i], conv_ffn_b[i])
        gate, up = jnp.split(hid, 2, axis=-1)
        h = h + (jax.nn.silu(gate) * up) @ w_down[i]
        g_ple = jax.nn.sigmoid(rms_norm(h, norm_ple_g[i]) @ w_ple_gate[i])
        h = h + g_ple * (p[i] @ w_ple[i])
    return rms_norm(h, norm_final_g)


import jax as _jax
import jax.numpy as _jnp

TWIN_FORMAT = 'train_step'
FWD_PARAMS = ['x', 'p', 'norm_mix_g', 'w_in', 'ln_a_g', 'ln_a_b', 'w_s', 'b_s', 'norm_a_g', 'conv_ssm_w', 'conv_ssm_b', 'dt_bias', 'a_log', 'd_skip', 'ssm_norm_g', 'w_out', 'norm_ffn_g', 'w_up', 'conv_ffn_w', 'conv_ffn_b', 'w_down', 'norm_ple_g', 'w_ple_gate', 'w_ple', 'norm_final_g']
TWIN_WEIGHTS = ['norm_mix_g', 'w_in', 'ln_a_g', 'ln_a_b', 'w_s', 'b_s', 'norm_a_g', 'conv_ssm_w', 'conv_ssm_b', 'dt_bias', 'a_log', 'd_skip', 'ssm_norm_g', 'w_out', 'norm_ffn_g', 'w_up', 'conv_ffn_w', 'conv_ffn_b', 'w_down', 'norm_ple_g', 'w_ple_gate', 'w_ple', 'norm_final_g']
TWIN_DIFF_INPUT = 'x'
TWIN_INPUTS = ['x', 'p', 'norm_mix_g', 'w_in', 'ln_a_g', 'ln_a_b', 'w_s', 'b_s', 'norm_a_g', 'conv_ssm_w', 'conv_ssm_b', 'dt_bias', 'a_log', 'd_skip', 'ssm_norm_g', 'w_out', 'norm_ffn_g', 'w_up', 'conv_ffn_w', 'conv_ffn_b', 'w_down', 'norm_ple_g', 'w_ple_gate', 'w_ple', 'norm_final_g', 'loss_target', 'm_norm_mix_g', 'm_w_in', 'm_ln_a_g', 'm_ln_a_b', 'm_w_s', 'm_b_s', 'm_norm_a_g', 'm_conv_ssm_w', 'm_conv_ssm_b', 'm_dt_bias', 'm_a_log', 'm_d_skip', 'm_ssm_norm_g', 'm_w_out', 'm_norm_ffn_g', 'm_w_up', 'm_conv_ffn_w', 'm_conv_ffn_b', 'm_w_down', 'm_norm_ple_g', 'm_w_ple_gate', 'm_w_ple', 'm_norm_final_g', 'v_norm_mix_g', 'v_w_in', 'v_ln_a_g', 'v_ln_a_b', 'v_w_s', 'v_b_s', 'v_norm_a_g', 'v_conv_ssm_w', 'v_conv_ssm_b', 'v_dt_bias', 'v_a_log', 'v_d_skip', 'v_ssm_norm_g', 'v_w_out', 'v_norm_ffn_g', 'v_w_up', 'v_conv_ffn_w', 'v_conv_ffn_b', 'v_w_down', 'v_norm_ple_g', 'v_w_ple_gate', 'v_w_ple', 'v_norm_final_g']
TWIN_OUTPUTS = ['loss', 'grad_x', 'grad_norm_mix_g', 'grad_w_in', 'grad_ln_a_g', 'grad_ln_a_b', 'grad_w_s', 'grad_b_s', 'grad_norm_a_g', 'grad_conv_ssm_w', 'grad_conv_ssm_b', 'grad_dt_bias', 'grad_a_log', 'grad_d_skip', 'grad_ssm_norm_g', 'grad_w_out', 'grad_norm_ffn_g', 'grad_w_up', 'grad_conv_ffn_w', 'grad_conv_ffn_b', 'grad_w_down', 'grad_norm_ple_g', 'grad_w_ple_gate', 'grad_w_ple', 'grad_norm_final_g', 'delta_norm_mix_g', 'delta_w_in', 'delta_ln_a_g', 'delta_ln_a_b', 'delta_w_s', 'delta_b_s', 'delta_norm_a_g', 'delta_conv_ssm_w', 'delta_conv_ssm_b', 'delta_dt_bias', 'delta_a_log', 'delta_d_skip', 'delta_ssm_norm_g', 'delta_w_out', 'delta_norm_ffn_g', 'delta_w_up', 'delta_conv_ffn_w', 'delta_conv_ffn_b', 'delta_w_down', 'delta_norm_ple_g', 'delta_w_ple_gate', 'delta_w_ple', 'delta_norm_final_g', 'new_m_norm_mix_g', 'new_m_w_in', 'new_m_ln_a_g', 'new_m_ln_a_b', 'new_m_w_s', 'new_m_b_s', 'new_m_norm_a_g', 'new_m_conv_ssm_w', 'new_m_conv_ssm_b', 'new_m_dt_bias', 'new_m_a_log', 'new_m_d_skip', 'new_m_ssm_norm_g', 'new_m_w_out', 'new_m_norm_ffn_g', 'new_m_w_up', 'new_m_conv_ffn_w', 'new_m_conv_ffn_b', 'new_m_w_down', 'new_m_norm_ple_g', 'new_m_w_ple_gate', 'new_m_w_ple', 'new_m_norm_final_g', 'new_v_norm_mix_g', 'new_v_w_in', 'new_v_ln_a_g', 'new_v_ln_a_b', 'new_v_w_s', 'new_v_b_s', 'new_v_norm_a_g', 'new_v_conv_ssm_w', 'new_v_conv_ssm_b', 'new_v_dt_bias', 'new_v_a_log', 'new_v_d_skip', 'new_v_ssm_norm_g', 'new_v_w_out', 'new_v_norm_ffn_g', 'new_v_w_up', 'new_v_conv_ffn_w', 'new_v_conv_ffn_b', 'new_v_w_down', 'new_v_norm_ple_g', 'new_v_w_ple_gate', 'new_v_w_ple', 'new_v_norm_final_g']
TWIN_LEAF_KINDS = {'loss': 'loss', 'grad_x': 'grad_x', 'grad_norm_mix_g': 'grad_w', 'grad_w_in': 'grad_w', 'grad_ln_a_g': 'grad_w', 'grad_ln_a_b': 'grad_w', 'grad_w_s': 'grad_w', 'grad_b_s': 'grad_w', 'grad_norm_a_g': 'grad_w', 'grad_conv_ssm_w': 'grad_w', 'grad_conv_ssm_b': 'grad_w', 'grad_dt_bias': 'grad_w', 'grad_a_log': 'grad_w', 'grad_d_skip': 'grad_w', 'grad_ssm_norm_g': 'grad_w', 'grad_w_out': 'grad_w', 'grad_norm_ffn_g': 'grad_w', 'grad_w_up': 'grad_w', 'grad_conv_ffn_w': 'grad_w', 'grad_conv_ffn_b': 'grad_w', 'grad_w_down': 'grad_w', 'grad_norm_ple_g': 'grad_w', 'grad_w_ple_gate': 'grad_w', 'grad_w_ple': 'grad_w', 'grad_norm_final_g': 'grad_w', 'delta_norm_mix_g': 'delta_w', 'delta_w_in': 'delta_w', 'delta_ln_a_g': 'delta_w', 'delta_ln_a_b': 'delta_w', 'delta_w_s': 'delta_w', 'delta_b_s': 'delta_w', 'delta_norm_a_g': 'delta_w', 'delta_conv_ssm_w': 'delta_w', 'delta_conv_ssm_b': 'delta_w', 'delta_dt_bias': 'delta_w', 'delta_a_log': 'delta_w', 'delta_d_skip': 'delta_w', 'delta_ssm_norm_g': 'delta_w', 'delta_w_out': 'delta_w', 'delta_norm_ffn_g': 'delta_w', 'delta_w_up': 'delta_w', 'delta_conv_ffn_w': 'delta_w', 'delta_conv_ffn_b': 'delta_w', 'delta_w_down': 'delta_w', 'delta_norm_ple_g': 'delta_w', 'delta_w_ple_gate': 'delta_w', 'delta_w_ple': 'delta_w', 'delta_norm_final_g': 'delta_w', 'new_m_norm_mix_g': 'new_m', 'new_m_w_in': 'new_m', 'new_m_ln_a_g': 'new_m', 'new_m_ln_a_b': 'new_m', 'new_m_w_s': 'new_m', 'new_m_b_s': 'new_m', 'new_m_norm_a_g': 'new_m', 'new_m_conv_ssm_w': 'new_m', 'new_m_conv_ssm_b': 'new_m', 'new_m_dt_bias': 'new_m', 'new_m_a_log': 'new_m', 'new_m_d_skip': 'new_m', 'new_m_ssm_norm_g': 'new_m', 'new_m_w_out': 'new_m', 'new_m_norm_ffn_g': 'new_m', 'new_m_w_up': 'new_m', 'new_m_conv_ffn_w': 'new_m', 'new_m_conv_ffn_b': 'new_m', 'new_m_w_down': 'new_m', 'new_m_norm_ple_g': 'new_m', 'new_m_w_ple_gate': 'new_m', 'new_m_w_ple': 'new_m', 'new_m_norm_final_g': 'new_m', 'new_v_norm_mix_g': 'new_v', 'new_v_w_in': 'new_v', 'new_v_ln_a_g': 'new_v', 'new_v_ln_a_b': 'new_v', 'new_v_w_s': 'new_v', 'new_v_b_s': 'new_v', 'new_v_norm_a_g': 'new_v', 'new_v_conv_ssm_w': 'new_v', 'new_v_conv_ssm_b': 'new_v', 'new_v_dt_bias': 'new_v', 'new_v_a_log': 'new_v', 'new_v_d_skip': 'new_v', 'new_v_ssm_norm_g': 'new_v', 'new_v_w_out': 'new_v', 'new_v_norm_ffn_g': 'new_v', 'new_v_w_up': 'new_v', 'new_v_conv_ffn_w': 'new_v', 'new_v_conv_ffn_b': 'new_v', 'new_v_w_down': 'new_v', 'new_v_norm_ple_g': 'new_v', 'new_v_w_ple_gate': 'new_v', 'new_v_w_ple': 'new_v', 'new_v_norm_final_g': 'new_v'}


def _forward(args):
    return _fwd_reference(*[args[k] for k in FWD_PARAMS])


def _output_shape():
    def fwd():
        inp = _fwd_setup_inputs(0)
        return _fwd_reference(*[inp[k] for k in FWD_PARAMS])
    out = _jax.eval_shape(fwd)
    return out.shape, out.dtype

N_MICROBATCH = 1
ADAM_LR = 0.001
ADAM_B1 = 0.9
ADAM_B2 = 0.999
ADAM_EPS = 1e-08
ADAM_WD = 0.01
ADAM_STEP = 10
PER_EXAMPLE_BATCH_AXIS = {'x': 0, 'p': 1, 'loss_target': 0}
SHARED_INPUTS = []
_WEIGHT_DTYPES = {'norm_mix_g': _jnp.float32, 'w_in': _jnp.float32, 'ln_a_g': _jnp.float32, 'ln_a_b': _jnp.float32, 'w_s': _jnp.float32, 'b_s': _jnp.float32, 'norm_a_g': _jnp.float32, 'conv_ssm_w': _jnp.float32, 'conv_ssm_b': _jnp.float32, 'dt_bias': _jnp.float32, 'a_log': _jnp.float32, 'd_skip': _jnp.float32, 'ssm_norm_g': _jnp.float32, 'w_out': _jnp.float32, 'norm_ffn_g': _jnp.float32, 'w_up': _jnp.float32, 'conv_ffn_w': _jnp.float32, 'conv_ffn_b': _jnp.float32, 'w_down': _jnp.float32, 'norm_ple_g': _jnp.float32, 'w_ple_gate': _jnp.float32, 'w_ple': _jnp.float32, 'norm_final_g': _jnp.float32}
MOMENT_SCALE = {'norm_mix_g': 1.320927e-01, 'w_in': 5.650447e-02, 'ln_a_g': 3.582107e-02, 'ln_a_b': 3.659116e-02, 'w_s': 3.581477e-02, 'b_s': 5.352051e-02, 'norm_a_g': 8.084470e-02, 'conv_ssm_w': 5.038235e-02, 'conv_ssm_b': 7.532674e-02, 'dt_bias': 1.534798e-01, 'a_log': 2.824001e-01, 'd_skip': 3.009827e-01, 'ssm_norm_g': 6.972693e-02, 'w_out': 1.058361e-01, 'norm_ffn_g': 7.359969e-02, 'w_up': 2.908115e-02, 'conv_ffn_w': 3.005914e-02, 'conv_ffn_b': 3.578089e-02, 'w_down': 4.833548e-02, 'norm_ple_g': 2.197069e-02, 'w_ple_gate': 2.084309e-02, 'w_ple': 4.346622e-02, 'norm_final_g': 3.207786e+01}


def _to_microbatches(a, axis):
    t = _jnp.moveaxis(a, axis, 0)
    t = t.reshape((N_MICROBATCH, t.shape[0] // N_MICROBATCH) + t.shape[1:])
    return _jnp.moveaxis(t, 1, axis + 1)


def setup_inputs(seed: int = 0) -> dict:
    inp = _fwd_setup_inputs(seed)
    key = _jax.random.fold_in(_jax.random.key(seed), 7919)
    shape, _ = _output_shape()
    out = dict(inp)
    out["loss_target"] = _jax.random.normal(_jax.random.fold_in(key, 0), shape, _jnp.float32)
    for i, name in enumerate(TWIN_WEIGHTS):
        w = inp[name].astype(_jnp.float32)
        if MOMENT_SCALE is None:
            s = _jnp.sqrt(_jnp.mean(_jnp.square(w)) + 1e-30)
        else:
            s = MOMENT_SCALE[name]
        km, kv = _jax.random.split(_jax.random.fold_in(key, i + 1))
        out[name] = w
        out["m_" + name] = s * _jax.random.normal(km, w.shape, _jnp.float32)
        out["v_" + name] = (s * s) * _jax.random.uniform(kv, w.shape, _jnp.float32, 0.5, 1.5)
    if N_MICROBATCH > 1:
        for name, axis in PER_EXAMPLE_BATCH_AXIS.items():
            out[name] = _to_microbatches(out[name], axis)
    return {'x': out['x'], 'p': out['p'], 'norm_mix_g': out['norm_mix_g'], 'w_in': out['w_in'], 'ln_a_g': out['ln_a_g'], 'ln_a_b': out['ln_a_b'], 'w_s': out['w_s'], 'b_s': out['b_s'], 'norm_a_g': out['norm_a_g'], 'conv_ssm_w': out['conv_ssm_w'], 'conv_ssm_b': out['conv_ssm_b'], 'dt_bias': out['dt_bias'], 'a_log': out['a_log'], 'd_skip': out['d_skip'], 'ssm_norm_g': out['ssm_norm_g'], 'w_out': out['w_out'], 'norm_ffn_g': out['norm_ffn_g'], 'w_up': out['w_up'], 'conv_ffn_w': out['conv_ffn_w'], 'conv_ffn_b': out['conv_ffn_b'], 'w_down': out['w_down'], 'norm_ple_g': out['norm_ple_g'], 'w_ple_gate': out['w_ple_gate'], 'w_ple': out['w_ple'], 'norm_final_g': out['norm_final_g'], 'loss_target': out['loss_target'], 'm_norm_mix_g': out['m_norm_mix_g'], 'm_w_in': out['m_w_in'], 'm_ln_a_g': out['m_ln_a_g'], 'm_ln_a_b': out['m_ln_a_b'], 'm_w_s': out['m_w_s'], 'm_b_s': out['m_b_s'], 'm_norm_a_g': out['m_norm_a_g'], 'm_conv_ssm_w': out['m_conv_ssm_w'], 'm_conv_ssm_b': out['m_conv_ssm_b'], 'm_dt_bias': out['m_dt_bias'], 'm_a_log': out['m_a_log'], 'm_d_skip': out['m_d_skip'], 'm_ssm_norm_g': out['m_ssm_norm_g'], 'm_w_out': out['m_w_out'], 'm_norm_ffn_g': out['m_norm_ffn_g'], 'm_w_up': out['m_w_up'], 'm_conv_ffn_w': out['m_conv_ffn_w'], 'm_conv_ffn_b': out['m_conv_ffn_b'], 'm_w_down': out['m_w_down'], 'm_norm_ple_g': out['m_norm_ple_g'], 'm_w_ple_gate': out['m_w_ple_gate'], 'm_w_ple': out['m_w_ple'], 'm_norm_final_g': out['m_norm_final_g'], 'v_norm_mix_g': out['v_norm_mix_g'], 'v_w_in': out['v_w_in'], 'v_ln_a_g': out['v_ln_a_g'], 'v_ln_a_b': out['v_ln_a_b'], 'v_w_s': out['v_w_s'], 'v_b_s': out['v_b_s'], 'v_norm_a_g': out['v_norm_a_g'], 'v_conv_ssm_w': out['v_conv_ssm_w'], 'v_conv_ssm_b': out['v_conv_ssm_b'], 'v_dt_bias': out['v_dt_bias'], 'v_a_log': out['v_a_log'], 'v_d_skip': out['v_d_skip'], 'v_ssm_norm_g': out['v_ssm_norm_g'], 'v_w_out': out['v_w_out'], 'v_norm_ffn_g': out['v_norm_ffn_g'], 'v_w_up': out['v_w_up'], 'v_conv_ffn_w': out['v_conv_ffn_w'], 'v_conv_ffn_b': out['v_conv_ffn_b'], 'v_w_down': out['v_w_down'], 'v_norm_ple_g': out['v_norm_ple_g'], 'v_w_ple_gate': out['v_w_ple_gate'], 'v_w_ple': out['v_w_ple'], 'v_norm_final_g': out['v_norm_final_g']}


def _loss(weights, diff, rest, loss_target):
    with _jax.named_scope("forward"):
        args = {**rest, TWIN_DIFF_INPUT: diff, **{k: w.astype(_WEIGHT_DTYPES[k]) for k, w in weights.items()}}
        y = _forward(args)
    with _jax.named_scope("loss_head"):
        err = _jnp.square(y.astype(_jnp.float32) - loss_target)
        return 0.5 * _jnp.sum(_jnp.mean(err, axis=-1)) if err.ndim else 0.5 * err


def _adamw(w, g, m, v):
    m = ADAM_B1 * m + (1.0 - ADAM_B1) * g
    v = ADAM_B2 * v + (1.0 - ADAM_B2) * _jnp.square(g)
    m_hat = m / (1.0 - ADAM_B1 ** ADAM_STEP)
    v_hat = v / (1.0 - ADAM_B2 ** ADAM_STEP)
    delta = -ADAM_LR * (m_hat / (_jnp.sqrt(v_hat) + ADAM_EPS) + ADAM_WD * w)
    return delta, m, v


def reference(x, p, norm_mix_g, w_in, ln_a_g, ln_a_b, w_s, b_s, norm_a_g, conv_ssm_w, conv_ssm_b, dt_bias, a_log, d_skip, ssm_norm_g, w_out, norm_ffn_g, w_up, conv_ffn_w, conv_ffn_b, w_down, norm_ple_g, w_ple_gate, w_ple, norm_final_g, loss_target, m_norm_mix_g, m_w_in, m_ln_a_g, m_ln_a_b, m_w_s, m_b_s, m_norm_a_g, m_conv_ssm_w, m_conv_ssm_b, m_dt_bias, m_a_log, m_d_skip, m_ssm_norm_g, m_w_out, m_norm_ffn_g, m_w_up, m_conv_ffn_w, m_conv_ffn_b, m_w_down, m_norm_ple_g, m_w_ple_gate, m_w_ple, m_norm_final_g, v_norm_mix_g, v_w_in, v_ln_a_g, v_ln_a_b, v_w_s, v_b_s, v_norm_a_g, v_conv_ssm_w, v_conv_ssm_b, v_dt_bias, v_a_log, v_d_skip, v_ssm_norm_g, v_w_out, v_norm_ffn_g, v_w_up, v_conv_ffn_w, v_conv_ffn_b, v_w_down, v_norm_ple_g, v_w_ple_gate, v_w_ple, v_norm_final_g):
    given = dict(x=x, p=p, norm_mix_g=norm_mix_g, w_in=w_in, ln_a_g=ln_a_g, ln_a_b=ln_a_b, w_s=w_s, b_s=b_s, norm_a_g=norm_a_g, conv_ssm_w=conv_ssm_w, conv_ssm_b=conv_ssm_b, dt_bias=dt_bias, a_log=a_log, d_skip=d_skip, ssm_norm_g=ssm_norm_g, w_out=w_out, norm_ffn_g=norm_ffn_g, w_up=w_up, conv_ffn_w=conv_ffn_w, conv_ffn_b=conv_ffn_b, w_down=w_down, norm_ple_g=norm_ple_g, w_ple_gate=w_ple_gate, w_ple=w_ple, norm_final_g=norm_final_g, loss_target=loss_target, m_norm_mix_g=m_norm_mix_g, m_w_in=m_w_in, m_ln_a_g=m_ln_a_g, m_ln_a_b=m_ln_a_b, m_w_s=m_w_s, m_b_s=m_b_s, m_norm_a_g=m_norm_a_g, m_conv_ssm_w=m_conv_ssm_w, m_conv_ssm_b=m_conv_ssm_b, m_dt_bias=m_dt_bias, m_a_log=m_a_log, m_d_skip=m_d_skip, m_ssm_norm_g=m_ssm_norm_g, m_w_out=m_w_out, m_norm_ffn_g=m_norm_ffn_g, m_w_up=m_w_up, m_conv_ffn_w=m_conv_ffn_w, m_conv_ffn_b=m_conv_ffn_b, m_w_down=m_w_down, m_norm_ple_g=m_norm_ple_g, m_w_ple_gate=m_w_ple_gate, m_w_ple=m_w_ple, m_norm_final_g=m_norm_final_g, v_norm_mix_g=v_norm_mix_g, v_w_in=v_w_in, v_ln_a_g=v_ln_a_g, v_ln_a_b=v_ln_a_b, v_w_s=v_w_s, v_b_s=v_b_s, v_norm_a_g=v_norm_a_g, v_conv_ssm_w=v_conv_ssm_w, v_conv_ssm_b=v_conv_ssm_b, v_dt_bias=v_dt_bias, v_a_log=v_a_log, v_d_skip=v_d_skip, v_ssm_norm_g=v_ssm_norm_g, v_w_out=v_w_out, v_norm_ffn_g=v_norm_ffn_g, v_w_up=v_w_up, v_conv_ffn_w=v_conv_ffn_w, v_conv_ffn_b=v_conv_ffn_b, v_w_down=v_w_down, v_norm_ple_g=v_norm_ple_g, v_w_ple_gate=v_w_ple_gate, v_w_ple=v_w_ple, v_norm_final_g=v_norm_final_g)
    weights = {n: given[n] for n in TWIN_WEIGHTS}
    shared = {n: given[n] for n in SHARED_INPUTS}
    per_example = {n: given[n] for n in ['x', 'p']}
    grad_fn = _jax.value_and_grad(_loss, argnums=(0, 1))

    def one_microbatch(ex, loss_target):
        ex = dict(ex)
        diff = ex.pop(TWIN_DIFF_INPUT)
        return grad_fn(weights, diff, {**shared, **ex}, loss_target)

    if N_MICROBATCH == 1:
        loss, (grad_w, grad_x) = one_microbatch(per_example, given["loss_target"])
    else:
        def body(carry, xs):
            loss_sum, grad_sum = carry
            l_k, (gw_k, gx_k) = one_microbatch(xs[0], xs[1])
            with _jax.named_scope("update"):
                return (loss_sum + l_k, _jax.tree.map(_jnp.add, grad_sum, gw_k)), gx_k

        init = (_jnp.zeros((), _jnp.float32), _jax.tree.map(_jnp.zeros_like, weights))
        (loss, grad_w), grad_x = _jax.lax.scan(body, init, (per_example, given["loss_target"]))
    with _jax.named_scope("update"):
        delta_w, new_m, new_v = {}, {}, {}
        for n in TWIN_WEIGHTS:
            delta_w[n], new_m[n], new_v[n] = _adamw(weights[n], grad_w[n], given["m_" + n], given["v_" + n])
    return (loss, grad_x, *[grad_w[n] for n in TWIN_WEIGHTS], *[delta_w[n] for n in TWIN_WEIGHTS],
            *[new_m[n] for n in TWIN_WEIGHTS], *[new_v[n] for n in TWIN_WEIGHTS])
```

```python
import functools
import math

import jax
import jax.numpy as jnp
from jax import lax
from jax.experimental import pallas as pl
from jax.experimental.pallas import tpu as pltpu

D_MODEL = 2048
SEQ = 8192
D_MIX = 2 * D_MODEL
D_A = D_MIX // 2
CHUNK = 128
N_GROUPS_A = D_A // 128
D_SSM = D_MIX - D_A
HEAD_DIM = 64
N_HEADS = D_SSM // HEAD_DIM
HEADS_PER_GROUP = 4
N_SSM_GROUPS = N_HEADS // HEADS_PER_GROUP
GW = HEADS_PER_GROUP * HEAD_DIM
D_STATE = 128
SSM_CONV = 4
D_BC = N_SSM_GROUPS * D_STATE
D_XBC = D_SSM + 2 * D_BC
D_MAIN = 2 * D_A + D_SSM + D_XBC
D_IN = D_MAIN + N_HEADS
D_FF = (D_MODEL * 11) // 4
FFN_CONV = 3
D_PLE = 256
EPS = 1e-6
HPAD = 128
N_CHIPS = 4

ADAM_LR = 0.001
ADAM_B1 = 0.9
ADAM_B2 = 0.999
ADAM_EPS = 1e-08
ADAM_WD = 0.01
ADAM_STEP = 10

F32 = jnp.float32
BF16 = jnp.bfloat16
MESH = pl.DeviceIdType.MESH
VMEM_LIMIT = 56 * 1024 * 1024


def _cparams(sem):
    return pltpu.CompilerParams(dimension_semantics=sem, vmem_limit_bytes=VMEM_LIMIT)


def _tile(n, pref, mult):
    t = min(pref, n)
    t -= t % mult
    while n % t:
        t -= mult
    return t


def _dot(a, b):
    return jnp.dot(a, b, preferred_element_type=F32)


def _dot_nt(a, b):
    return lax.dot_general(a, b, (((1,), (1,)), ((), ())), preferred_element_type=F32)


def _dot_tn(a, b):
    return lax.dot_general(a, b, (((0,), (0,)), ((), ())), preferred_element_type=F32)


def _split3(x):
    hi = x.astype(BF16)
    r = x - hi.astype(F32)
    mid = r.astype(BF16)
    lo = (r - mid.astype(F32)).astype(BF16)
    return hi, mid, lo


def _x01(x, e):
    h, m, l = _split3(x)
    return _dot(h, e) + _dot(m, e) + _dot(l, e)


def _x01_nt(x, e):
    h, m, l = _split3(x)
    return _dot_nt(h, e) + _dot_nt(m, e) + _dot_nt(l, e)


def _e01x(e, x):
    h, m, l = _split3(x)
    return _dot(e, h) + _dot(e, m) + _dot(e, l)


def _e01x_tn(e, x):
    h, m, l = _split3(x)
    return _dot_tn(e, h) + _dot_tn(e, m) + _dot_tn(e, l)


def _sigmoid(x):
    return 1.0 / (1.0 + jnp.exp(-x))


_GELU_C = math.sqrt(2.0 / math.pi)


def _gelu_and_grad(x):
    x2 = x * x
    th = jnp.tanh(_GELU_C * (x + 0.044715 * x * x2))
    y = 0.5 * x * (1.0 + th)
    dy = 0.5 * (1.0 + th) + 0.5 * x * (1.0 - th * th) * (_GELU_C * (1.0 + 3.0 * 0.044715 * x2))
    return y, dy


def _silu_and_grad(x):
    s = _sigmoid(x)
    return x * s, s * (1.0 + x * (1.0 - s))


def _softplus(x):
    u = jnp.exp(-jnp.abs(x))
    w = 1.0 + u
    l1p = jnp.where(w == 1.0, u, jnp.log(w) * (u / (w - 1.0)))
    return jnp.maximum(x, 0.0) + l1p


def _matmul(a, b, *, mode, name, out_dtype=F32, res=None, tm=512, tn=512, tk=2048,
            b_sharded=False, out_shards=0):
    if mode == 'tn':
        kdim, m = a.shape
        n = b.shape[1]
    else:
        m, kdim = a.shape
        if b_sharded:
            s_b, d1, d2 = b.shape
            n = s_b * d2 if mode == 'nn' else d1
        else:
            n = b.shape[1] if mode == 'nn' else b.shape[0]
    per = None
    if b_sharded:
        per = b.shape[2]
    if out_shards:
        per = n // out_shards
    tm = _tile(m, tm, 128 if mode == 'tn' else 8)
    if mode == 'nt' and b_sharded:
        tn = _tile(n, tn, 128)
        tk = _tile(per, tk, 128)
    elif per is not None:
        tn = _tile(per, tn, 128)
        tk = _tile(kdim, tk, 128 if mode != 'tn' else 8)
    else:
        tn = _tile(n, tn, 128)
        tk = _tile(kdim, tk, 128 if mode != 'tn' else 8)
    nm, nn_, nk = m // tm, n // tn, kdim // tk
    has_res = res is not None

    def body(*refs):
        a_ref, b_ref = refs[0], refs[1]
        res_ref = refs[2] if has_res else None
        o_ref = refs[2 + has_res]
        av = a_ref[...].astype(BF16)
        bv = b_ref[...].astype(BF16)
        if mode == 'nn':
            p = _dot(av, bv)
        elif mode == 'nt':
            p = _dot_nt(av, bv)
        else:
            p = _dot_tn(av, bv)

        def fin(v):
            if has_res:
                v = v + res_ref[...]
            o_ref[...] = v.astype(o_ref.dtype)

        if nk == 1:
            fin(p)
        else:
            acc_ref = refs[-1]
            k = pl.program_id(2)

            @pl.when(k == 0)
            def _():
                acc_ref[...] = p

            @pl.when(k > 0)
            def _():
                acc_ref[...] += p

            @pl.when(k == nk - 1)
            def _():
                fin(acc_ref[...])

    if mode == 'nn':
        a_spec = pl.BlockSpec((tm, tk), lambda j, i, k: (i, k))
        if b_sharded:
            nps = per // tn
            b_spec = pl.BlockSpec((None, tk, tn), lambda j, i, k: (j // nps, k, j % nps))
        else:
            b_spec = pl.BlockSpec((tk, tn), lambda j, i, k: (k, j))
    elif mode == 'nt':
        a_spec = pl.BlockSpec((tm, tk), lambda j, i, k: (i, k))
        if b_sharded:
            kps = per // tk
            b_spec = pl.BlockSpec((None, tn, tk), lambda j, i, k: (k // kps, j, k % kps))
        else:
            b_spec = pl.BlockSpec((tn, tk), lambda j, i, k: (j, k))
    else:
        a_spec = pl.BlockSpec((tk, tm), lambda j, i, k: (k, i))
        b_spec = pl.BlockSpec((tk, tn), lambda j, i, k: (k, j))
    in_specs = [a_spec, b_spec]
    args = [a, b]
    if has_res:
        in_specs.append(pl.BlockSpec((tm, tn), lambda j, i, k: (i, j)))
        args.append(res)
    if out_shards:
        nps_o = per // tn
        out_shape = jax.ShapeDtypeStruct((out_shards, m, per), out_dtype)
        out_spec = pl.BlockSpec((None, tm, tn), lambda j, i, k: (j // nps_o, i, j % nps_o))
    else:
        out_shape = jax.ShapeDtypeStruct((m, n), out_dtype)
        out_spec = pl.BlockSpec((tm, tn), lambda j, i, k: (i, j))
    scratch = [pltpu.VMEM((tm, tn), F32)] if nk > 1 else []
    return pl.pallas_call(
        body, name=name, grid=(nn_, nm, nk), in_specs=in_specs, out_specs=out_spec,
        out_shape=out_shape, scratch_shapes=scratch,
        compiler_params=_cparams(("parallel", "parallel", "arbitrary")),
    )(*args)


def _rms_fwd(x, g, name):
    t, d = x.shape
    tt = _tile(t, 512, 8)

    def body(x_ref, g_ref, o_ref):
        xv = x_ref[...]
        r = lax.rsqrt(jnp.mean(xv * xv, axis=-1, keepdims=True) + EPS)
        o_ref[...] = (xv * r * g_ref[...]).astype(o_ref.dtype)

    return pl.pallas_call(
        body, name=name, grid=(t // tt,),
        in_specs=[pl.BlockSpec((tt, d), lambda i: (i, 0)), pl.BlockSpec((1, d), lambda i: (0, 0))],
        out_specs=pl.BlockSpec((tt, d), lambda i: (i, 0)),
        out_shape=jax.ShapeDtypeStruct((t, d), BF16),
        compiler_params=_cparams(("parallel",)),
    )(x, g)


def _rms_bwd(x, g, dy, dres, name):
    t, d = x.shape
    tt = _tile(t, 256, 8)

    def body(x_ref, g_ref, dy_ref, dres_ref, dx_ref, dg_ref):
        i = pl.program_id(0)
        xv = x_ref[...]
        r = lax.rsqrt(jnp.mean(xv * xv, axis=-1, keepdims=True) + EPS)
        xh = xv * r
        dyv = dy_ref[...].astype(F32)
        dxh = dyv * g_ref[...]
        c = jnp.mean(dxh * xh, axis=-1, keepdims=True)
        dx_ref[...] = dres_ref[...] + r * (dxh - xh * c)
        part = jnp.sum(dyv * xh, axis=0, keepdims=True)

        @pl.when(i == 0)
        def _():
            dg_ref[...] = part

        @pl.when(i > 0)
        def _():
            dg_ref[...] += part

    row = pl.BlockSpec((tt, d), lambda i: (i, 0))
    vec = pl.BlockSpec((1, d), lambda i: (0, 0))
    return pl.pallas_call(
        body, name=name, grid=(t // tt,),
        in_specs=[row, vec, row, row], out_specs=[row, vec],
        out_shape=[jax.ShapeDtypeStruct((t, d), F32), jax.ShapeDtypeStruct((1, d), F32)],
        compiler_params=_cparams(("arbitrary",)),
    )(x, g, dy, dres)


def _tail(h2, gl, pe, target, gfin):
    t, d = h2.shape
    tt = _tile(t, 256, 8)

    def body(h2_ref, gl_ref, pe_ref, tg_ref, gf_ref, dh3_ref, dgl_ref, dpe_ref, loss_ref, dgf_ref):
        i = pl.program_id(0)
        sig = _sigmoid(gl_ref[...])
        pev = pe_ref[...]
        h3 = h2_ref[...] + sig * pev
        r = lax.rsqrt(jnp.mean(h3 * h3, axis=-1, keepdims=True) + EPS)
        xh = h3 * r
        gf = gf_ref[...]
        e = xh * gf - tg_ref[...]
        dy = e * (1.0 / d)
        dxh = dy * gf
        c = jnp.mean(dxh * xh, axis=-1, keepdims=True)
        dh3 = r * (dxh - xh * c)
        dh3_ref[...] = dh3
        dgl_ref[...] = (dh3 * pev * sig * (1.0 - sig)).astype(BF16)
        dpe_ref[...] = (dh3 * sig).astype(BF16)
        lpart = jnp.sum(e * e, axis=0, keepdims=True) * (0.5 / d)
        gpart = jnp.sum(dy * xh, axis=0, keepdims=True)

        @pl.when(i == 0)
        def _():
            loss_ref[...] = lpart
            dgf_ref[...] = gpart

        @pl.when(i > 0)
        def _():
            loss_ref[...] += lpart
            dgf_ref[...] += gpart

        @pl.when(i == t // tt - 1)
        def _():
            loss_ref[...] = jnp.broadcast_to(jnp.sum(loss_ref[...], axis=-1, keepdims=True), (1, d))

    row = pl.BlockSpec((tt, d), lambda i: (i, 0))
    vec = pl.BlockSpec((1, d), lambda i: (0, 0))
    return pl.pallas_call(
        body, name="tail", grid=(t // tt,),
        in_specs=[row, row, row, row, vec], out_specs=[row, row, row, vec, vec],
        out_shape=[jax.ShapeDtypeStruct((t, d), F32), jax.ShapeDtypeStruct((t, d), BF16),
                   jax.ShapeDtypeStruct((t, d), BF16), jax.ShapeDtypeStruct((1, d), F32),
                   jax.ShapeDtypeStruct((1, d), F32)],
        compiler_params=_cparams(("arbitrary",)),
    )(h2, gl, pe, target, gfin)


def _conv(cur_ref, prev_ref, w_ref, b_ref, ext_ref, first, width):
    rows = cur_ref.shape[0]
    ext_ref[0:8, :] = jnp.where(first, 0.0, prev_ref[...])
    ext_ref[8:8 + rows, :] = cur_ref[...]
    acc = b_ref[...]
    for k in range(width):
        acc = acc + w_ref[k:k + 1, :] * ext_ref[pl.ds(9 - width + k, rows), :]
    return acc


def _ffn_specs(t, tt, tc, nf):
    hb = tt // 8
    cur_g = pl.BlockSpec((tt, tc), lambda j, i: (i, j))
    cur_u = pl.BlockSpec((tt, tc), lambda j, i: (i, j + nf))
    prev_g = pl.BlockSpec((8, tc), lambda j, i: (jnp.maximum(i * hb - 1, 0), j))
    prev_u = pl.BlockSpec((8, tc), lambda j, i: (jnp.maximum(i * hb - 1, 0), j + nf))
    w_g = pl.BlockSpec((FFN_CONV, tc), lambda j, i: (0, j))
    w_u = pl.BlockSpec((FFN_CONV, tc), lambda j, i: (0, j + nf))
    b_g = pl.BlockSpec((1, tc), lambda j, i: (0, j))
    b_u = pl.BlockSpec((1, tc), lambda j, i: (0, j + nf))
    return [cur_g, prev_g, cur_u, prev_u, w_g, w_u, b_g, b_u]


def _ffn_act_fwd(hid, cw, cb):
    t = hid.shape[0]
    tt = _tile(t, 512, 8)
    tc = _tile(D_FF, 1408, 128)
    nf = D_FF // tc

    def body(g_ref, gp_ref, u_ref, up_ref, wg_ref, wu_ref, bg_ref, bu_ref, o_ref, eg_ref, eu_ref):
        first = pl.program_id(1) == 0
        gate = _conv(g_ref, gp_ref, wg_ref, bg_ref, eg_ref, first, FFN_CONV)
        up = _conv(u_ref, up_ref, wu_ref, bu_ref, eu_ref, first, FFN_CONV)
        o_ref[...] = (gate * _sigmoid(gate) * up).astype(BF16)

    return pl.pallas_call(
        body, name="ffn_act_fwd", grid=(nf, t // tt), in_specs=_ffn_specs(t, tt, tc, nf),
        out_specs=pl.BlockSpec((tt, tc), lambda j, i: (i, j)),
        out_shape=jax.ShapeDtypeStruct((t, D_FF), BF16),
        scratch_shapes=[pltpu.VMEM((tt + 8, tc), F32), pltpu.VMEM((tt + 8, tc), F32)],
        compiler_params=_cparams(("parallel", "arbitrary")),
    )(hid, hid, hid, hid, cw, cw, cb, cb)


def _ffn_act_bwd_a(hid, cw, cb, dact):
    t = hid.shape[0]
    tt = _tile(t, 256, 8)
    tc = _tile(D_FF, 1408, 128)
    nf = D_FF // tc

    def body(g_ref, gp_ref, u_ref, up_ref, wg_ref, wu_ref, bg_ref, bu_ref, da_ref,
             dg_ref, du_ref, wg_acc, wu_acc, eg_ref, eu_ref):
        i = pl.program_id(1)
        first = i == 0
        gate = _conv(g_ref, gp_ref, wg_ref, bg_ref, eg_ref, first, FFN_CONV)
        up = _conv(u_ref, up_ref, wu_ref, bu_ref, eu_ref, first, FFN_CONV)
        sv, sg = _silu_and_grad(gate)
        da = da_ref[...].astype(F32)
        dgate = da * up * sg
        dup = da * sv
        dg_ref[...] = dgate
        du_ref[...] = dup

        @pl.when(first)
        def _():
            wg_acc[...] = jnp.zeros_like(wg_acc)
            wu_acc[...] = jnp.zeros_like(wu_acc)

        for acc, ext, dv in ((wg_acc, eg_ref, dgate), (wu_acc, eu_ref, dup)):
            for k in range(FFN_CONV):
                acc[k:k + 1, :] += jnp.sum(ext[pl.ds(6 + k, tt), :] * dv, axis=0, keepdims=True)
            acc[3:4, :] += jnp.sum(dv, axis=0, keepdims=True)

    acc_spec = pl.BlockSpec((8, tc), lambda j, i: (0, j))
    out_blk = pl.BlockSpec((tt, tc), lambda j, i: (i, j))
    return pl.pallas_call(
        body, name="ffn_act_bwd_a", grid=(nf, t // tt),
        in_specs=_ffn_specs(t, tt, tc, nf) + [out_blk],
        out_specs=[out_blk, out_blk, acc_spec, acc_spec],
        out_shape=[jax.ShapeDtypeStruct((t, D_FF), F32), jax.ShapeDtypeStruct((t, D_FF), F32),
                   jax.ShapeDtypeStruct((8, D_FF), F32), jax.ShapeDtypeStruct((8, D_FF), F32)],
        scratch_shapes=[pltpu.VMEM((tt + 8, tc), F32), pltpu.VMEM((tt + 8, tc), F32)],
        compiler_params=_cparams(("parallel", "arbitrary")),
    )(hid, hid, hid, hid, cw, cw, cb, cb, dact)


def _ffn_act_bwd_b(dg, du, cw):
    t = dg.shape[0]
    tt = _tile(t, 512, 8)
    tc = _tile(D_FF, 1408, 128)
    nf = D_FF // tc
    hb = tt // 8
    nrow8 = t // 8
    nt = t // tt

    def body(g_ref, gn_ref, u_ref, un_ref, w_ref, o_ref, ext_ref):
        jj, i = pl.program_id(0), pl.program_id(1)
        is_g = jj < nf
        last = i == nt - 1
        cur = jnp.where(is_g, g_ref[...], u_ref[...])
        nxt = jnp.where(is_g, gn_ref[...], un_ref[...])
        ext_ref[0:tt, :] = cur
        ext_ref[tt:tt + 8, :] = jnp.where(last, 0.0, nxt)
        acc = w_ref[2:3, :] * ext_ref[pl.ds(0, tt), :]
        acc = acc + w_ref[1:2, :] * ext_ref[pl.ds(1, tt), :]
        acc = acc + w_ref[0:1, :] * ext_ref[pl.ds(2, tt), :]
        o_ref[...] = acc.astype(BF16)

    def nrow(i):
        return jnp.minimum((i + 1) * hb, nrow8 - 1)

    in_specs = [
        pl.BlockSpec((tt, tc), lambda jj, i: (jnp.where(jj < nf, i, 0), jnp.minimum(jj, nf - 1))),
        pl.BlockSpec((8, tc), lambda jj, i: (jnp.where(jj < nf, nrow(i), 0), jnp.minimum(jj, nf - 1))),
        pl.BlockSpec((tt, tc), lambda jj, i: (jnp.where(jj >= nf, i, 0), jnp.maximum(jj - nf, 0))),
        pl.BlockSpec((8, tc), lambda jj, i: (jnp.where(jj >= nf, nrow(i), 0), jnp.maximum(jj - nf, 0))),
        pl.BlockSpec((FFN_CONV, tc), lambda jj, i: (0, jj)),
    ]
    return pl.pallas_call(
        body, name="ffn_act_bwd_b", grid=(2 * nf, nt), in_specs=in_specs,
        out_specs=pl.BlockSpec((tt, tc), lambda jj, i: (i, jj)),
        out_shape=jax.ShapeDtypeStruct((t, 2 * D_FF), BF16),
        scratch_shapes=[pltpu.VMEM((tt + 8, tc), F32)],
        compiler_params=_cparams(("parallel", "arbitrary")),
    )(dg, dg, du, du, cw)


def _tri_mask():
    r = lax.broadcasted_iota(jnp.int32, (CHUNK, CHUNK), 0)
    c = lax.broadcasted_iota(jnp.int32, (CHUNK, CHUNK), 1)
    return r >= c


def _gmlp_group_fwd(uv_ref, lng_ref, lnb_ref, ws_ref, bexp_ref, tri, g, want_grad):
    lo, hi = g * 128, (g + 1) * 128
    u_pre = uv_ref[:, lo:hi]
    v_pre = uv_ref[:, D_A + lo:D_A + hi]
    u, du = _gelu_and_grad(u_pre)
    v, dv = _gelu_and_grad(v_pre)
    mu = jnp.mean(v, axis=-1, keepdims=True)
    dc = v - mu
    rs = lax.rsqrt(jnp.mean(dc * dc, axis=-1, keepdims=True) + EPS)
    xh = dc * rs
    vn = (xh * lng_ref[:, lo:hi] + lnb_ref[:, lo:hi]).astype(BF16)
    w = jnp.where(tri, ws_ref[g], 0.0).astype(BF16)
    sg = _dot(w, vn) + bexp_ref[g]
    if want_grad:
        return u, du, dv, rs, xh, vn, w, sg
    return u * sg


def _gmlp_fwd(proj, ln_g, ln_b, w_s, b_exp, na_g):
    t = proj.shape[0]
    ng = N_GROUPS_A

    def body(uv_ref, lng_ref, lnb_ref, ws_ref, bexp_ref, nag_ref, o_ref):
        tri = _tri_mask()
        ys = [_gmlp_group_fwd(uv_ref, lng_ref, lnb_ref, ws_ref, bexp_ref, tri, g, False) for g in range(ng)]
        ssq = ys[0] * 0.0
        for y in ys:
            ssq = ssq + y * y
        r = lax.rsqrt(jnp.sum(ssq, axis=-1, keepdims=True) * (1.0 / D_A) + EPS)
        for g, y in enumerate(ys):
            o_ref[:, g * 128:(g + 1) * 128] = (y * r * nag_ref[:, g * 128:(g + 1) * 128]).astype(BF16)

    vec = pl.BlockSpec((1, D_A), lambda i: (0, 0))
    cube = pl.BlockSpec((ng, CHUNK, CHUNK), lambda i: (0, 0, 0))
    return pl.pallas_call(
        body, name="gmlp_fwd", grid=(t // CHUNK,),
        in_specs=[pl.BlockSpec((CHUNK, 2 * D_A), lambda i: (i, 0)), vec, vec, cube, cube, vec],
        out_specs=pl.BlockSpec((CHUNK, D_A), lambda i: (i, 0)),
        out_shape=jax.ShapeDtypeStruct((t, D_MIX), BF16),
        compiler_params=_cparams(("parallel",)),
    )(proj, ln_g, ln_b, w_s, b_exp, na_g)


def _gmlp_bwd(proj, dyab, ln_g, ln_b, w_s, b_exp, na_g):
    t = proj.shape[0]
    ng = N_GROUPS_A
    nsteps = t // CHUNK

    def body(uv_ref, dy_ref, lng_ref, lnb_ref, ws_ref, bexp_ref, nag_ref,
             duv_ref, dws_ref, dbs_ref, dlng_ref, dlnb_ref, dnag_ref, dbacc):
        i = pl.program_id(0)
        tri = _tri_mask()

        @pl.when(i == 0)
        def _():
            dws_ref[...] = jnp.zeros_like(dws_ref)
            dbacc[...] = jnp.zeros_like(dbacc)
            dlng_ref[...] = jnp.zeros_like(dlng_ref)
            dlnb_ref[...] = jnp.zeros_like(dlnb_ref)
            dnag_ref[...] = jnp.zeros_like(dnag_ref)

        st = [_gmlp_group_fwd(uv_ref, lng_ref, lnb_ref, ws_ref, bexp_ref, tri, g, True) for g in range(ng)]
        ssq = st[0][0] * 0.0
        for s in st:
            y = s[0] * s[7]
            ssq = ssq + y * y
        r = lax.rsqrt(jnp.sum(ssq, axis=-1, keepdims=True) * (1.0 / D_A) + EPS)
        csum = st[0][0] * 0.0
        for g, s in enumerate(st):
            sl = slice(g * 128, (g + 1) * 128)
            xhy = s[0] * s[7] * r
            dya = dy_ref[:, sl]
            dnag_ref[:, sl] += jnp.sum(dya * xhy, axis=0, keepdims=True)
            csum = csum + dya * nag_ref[:, sl] * xhy
        c1 = jnp.sum(csum, axis=-1, keepdims=True) * (1.0 / D_A)
        for g, s in enumerate(st):
            u, du, dv, rs, xh, vn, w, sg = s
            sl = slice(g * 128, (g + 1) * 128)
            dy = r * (dy_ref[:, sl] * nag_ref[:, sl] - u * sg * r * c1)
            dsg = dy * u
            dsg_b = dsg.astype(BF16)
            dws_ref[g] += _dot_nt(dsg_b, vn)
            dbacc[g] += dsg
            dvn = _dot_tn(w, dsg_b)
            dlnb_ref[:, sl] += jnp.sum(dvn, axis=0, keepdims=True)
            dlng_ref[:, sl] += jnp.sum(dvn * xh, axis=0, keepdims=True)
            dxh = dvn * lng_ref[:, sl]
            dvv = rs * (dxh - jnp.mean(dxh, axis=-1, keepdims=True)
                        - xh * jnp.mean(dxh * xh, axis=-1, keepdims=True))
            duv_ref[:, sl] = (dy * sg * du).astype(BF16)
            duv_ref[:, D_A + g * 128:D_A + (g + 1) * 128] = (dvv * dv).astype(BF16)

        @pl.when(i == nsteps - 1)
        def _():
            for g in range(ng):
                dws_ref[g] = jnp.where(tri, dws_ref[g], 0.0)
                dbs_ref[g] = jnp.sum(dbacc[g], axis=-1, keepdims=True)

    vec = pl.BlockSpec((1, D_A), lambda i: (0, 0))
    cube = pl.BlockSpec((ng, CHUNK, CHUNK), lambda i: (0, 0, 0))
    return pl.pallas_call(
        body, name="gmlp_bwd", grid=(nsteps,),
        in_specs=[pl.BlockSpec((CHUNK, 2 * D_A), lambda i: (i, 0)),
                  pl.BlockSpec((CHUNK, D_A), lambda i: (i, 0)), vec, vec, cube, cube, vec],
        out_specs=[pl.BlockSpec((CHUNK, 2 * D_A), lambda i: (i, 0)), cube,
                   pl.BlockSpec((ng, CHUNK, 1), lambda i: (0, 0, 0)), vec, vec, vec],
        out_shape=[jax.ShapeDtypeStruct((t, 2 * D_A), BF16), jax.ShapeDtypeStruct((ng, CHUNK, CHUNK), F32),
                   jax.ShapeDtypeStruct((ng, CHUNK, 1), F32), jax.ShapeDtypeStruct((1, D_A), F32),
                   jax.ShapeDtypeStruct((1, D_A), F32), jax.ShapeDtypeStruct((1, D_A), F32)],
        scratch_shapes=[pltpu.VMEM((ng, CHUNK, CHUNK), F32)],
        compiler_params=_cparams(("arbitrary",)),
    )(proj, dyab, ln_g, ln_b, w_s, b_exp, na_g)


OFF_Z = 2 * D_A
OFF_XS = OFF_Z + D_SSM
OFF_B = OFF_XS + D_SSM
OFF_C = OFF_B + D_BC


def _ssd_consts():
    tri = jnp.tril(jnp.ones((CHUNK, CHUNK), F32)).astype(BF16)
    h = jnp.arange(HPAD)[None, :, None]
    g = jnp.arange(N_SSM_GROUPS)[:, None, None]
    j1 = jnp.arange(GW)[None, None, :]
    eh = (h == g * HEADS_PER_GROUP + j1 // HEAD_DIM).astype(BF16)
    j2 = jnp.arange(HEADS_PER_GROUP * 128)[None, None, :]
    e128 = (h == g * HEADS_PER_GROUP + j2 // 128).astype(BF16)
    return tri, eh, e128


def _ssd_in_specs(cmap):
    def rows(i):
        return cmap(i)

    def prev8(i):
        return jnp.maximum(cmap(i) * (CHUNK // 8) - 1, 0)

    gw, ns = GW, D_STATE
    specs = [
        pl.BlockSpec((CHUNK, gw), lambda i, g: (rows(i), OFF_Z // gw + g)),
        pl.BlockSpec((CHUNK, gw), lambda i, g: (rows(i), OFF_XS // gw + g)),
        pl.BlockSpec((8, gw), lambda i, g: (prev8(i), OFF_XS // gw + g)),
        pl.BlockSpec((CHUNK, ns), lambda i, g: (rows(i), OFF_B // ns + g)),
        pl.BlockSpec((8, ns), lambda i, g: (prev8(i), OFF_B // ns + g)),
        pl.BlockSpec((CHUNK, ns), lambda i, g: (rows(i), OFF_C // ns + g)),
        pl.BlockSpec((8, ns), lambda i, g: (prev8(i), OFF_C // ns + g)),
        pl.BlockSpec((CHUNK, HPAD), lambda i, g: (rows(i), 0)),
        pl.BlockSpec((SSM_CONV, gw), lambda i, g: (0, g)),
        pl.BlockSpec((SSM_CONV, ns), lambda i, g: (0, D_SSM // ns + g)),
        pl.BlockSpec((SSM_CONV, ns), lambda i, g: (0, (D_SSM + D_BC) // ns + g)),
        pl.BlockSpec((1, gw), lambda i, g: (0, g)),
        pl.BlockSpec((1, ns), lambda i, g: (0, D_SSM // ns + g)),
        pl.BlockSpec((1, ns), lambda i, g: (0, (D_SSM + D_BC) // ns + g)),
        pl.BlockSpec((1, HPAD), lambda i, g: (0, 0)),
        pl.BlockSpec((1, HPAD), lambda i, g: (0, 0)),
        pl.BlockSpec((1, gw), lambda i, g: (0, g)),
        pl.BlockSpec((1, gw), lambda i, g: (0, g)),
        pl.BlockSpec((CHUNK, CHUNK), lambda i, g: (0, 0)),
        pl.BlockSpec((None, HPAD, gw), lambda i, g: (g, 0, 0)),
        pl.BlockSpec((None, HPAD, HEADS_PER_GROUP * 128), lambda i, g: (g, 0, 0)),
    ]
    return specs


def _ssd_scratch():
    return [pltpu.VMEM((CHUNK + 8, GW), F32), pltpu.VMEM((CHUNK + 8, D_STATE), F32),
            pltpu.VMEM((CHUNK + 8, D_STATE), F32), pltpu.VMEM((HPAD, CHUNK), F32),
            pltpu.VMEM((CHUNK, GW), F32)]


def _ssd_pre(first, g, refs, scr):
    (z_ref, xs_ref, xsp_ref, b_ref, bp_ref, c_ref, cp_ref, dt_ref, cwx, cwb, cwc, cbx, cbb, cbc,
     dtb_ref, alog_ref, de_ref, gain_ref, tri_ref, eh_ref, e128_ref) = refs
    ext_x, ext_b, ext_c, acst_sc, acse_sc = scr
    p = {}
    px = _conv(xs_ref, xsp_ref, cwx, cbx, ext_x, first, SSM_CONV)
    pb = _conv(b_ref, bp_ref, cwb, cbb, ext_b, first, SSM_CONV)
    pc = _conv(c_ref, cp_ref, cwc, cbc, ext_c, first, SSM_CONV)
    p['xs'], p['dsx'] = _silu_and_grad(px)
    p['bm'], p['dsb'] = _silu_and_grad(pb)
    p['cm'], p['dsc'] = _silu_and_grad(pc)
    dt_in = dt_ref[...] + dtb_ref[...]
    p['dt_in'] = dt_in
    dt = _softplus(dt_in)
    p['dt'] = dt
    a = -jnp.exp(alog_ref[...])
    p['a'] = a
    tri_b = tri_ref[...]
    acs = _e01x(tri_b, dt * a)
    acst_sc[...] = acs.T
    eh = eh_ref[...]
    p['eh'] = eh
    p['dt_e'] = _x01(dt, eh)
    acs_e = _x01(acs, eh)
    acse_sc[...] = acs_e
    p['acs_e'] = acs_e
    p['acs_c'] = _x01(acs, e128_ref[...])
    p['acs_last_e'] = acse_sc[pl.ds(CHUNK - 1, 1), :]
    p['xdt'] = p['xs'] * p['dt_e']
    p['decay_e'] = jnp.exp(p['acs_last_e'] - acs_e)
    p['cm_b'] = p['cm'].astype(BF16)
    p['bm_b'] = p['bm'].astype(BF16)
    p['scores'] = _dot_nt(p['cm_b'], p['bm_b'])
    p['tri_b'] = tri_b
    return p


def _ssd_l(p, g, r, acst_sc, tri):
    col = p['acs_c'][:, r * 128:(r + 1) * 128]
    row = acst_sc[pl.ds(g * HEADS_PER_GROUP + r, 1), :]
    return jnp.exp(jnp.where(tri, col - row, -1e30))


def _ssd_fwd(proj, dt_raw, yab, cw, cb, dtb, alog, de, gain, consts):
    t = proj.shape[0]
    nc = t // CHUNK
    ng = N_SSM_GROUPS
    tri_c, eh_c, e128_c = consts
    n_in = 21

    def body(*refs):
        ins = refs[:n_in]
        yb_ref, ys_ref, hs_ref = refs[n_in + 1:n_in + 4]
        h_sc = refs[n_in + 4]
        scr = refs[n_in + 5:]
        c, g = pl.program_id(0), pl.program_id(1)
        z_ref, de_ref, gain_ref = ins[0], ins[16], ins[17]
        slab = pl.ds(pl.multiple_of(g * D_STATE, D_STATE), D_STATE)

        @pl.when(c == 0)
        def _():
            h_sc[slab, :] = jnp.zeros((D_STATE, GW), F32)

        p = _ssd_pre(c == 0, g, ins, scr)
        tri = _tri_mask()
        h_in = h_sc[slab, :]
        hs_ref[...] = h_in
        yoff = _dot(p['cm_b'], h_in.astype(BF16)) * jnp.exp(p['acs_e'])
        states = _dot_tn(p['bm_b'], (p['xdt'] * p['decay_e']).astype(BF16))
        lane = lax.broadcasted_iota(jnp.int32, (CHUNK, 128), 1)
        slabs = []
        for r2 in range(HEADS_PER_GROUP // 2):
            xb = p['xdt'][:, r2 * 128:(r2 + 1) * 128].astype(BF16)
            ya = _dot((p['scores'] * _ssd_l(p, g, 2 * r2, scr[3], tri)).astype(BF16), xb)
            yb = _dot((p['scores'] * _ssd_l(p, g, 2 * r2 + 1, scr[3], tri)).astype(BF16), xb)
            slabs.append(jnp.where(lane < HEAD_DIM, ya, yb))
        y = jnp.concatenate(slabs, axis=1) + yoff + de_ref[...] * p['xs']
        ys_ref[...] = y
        h_sc[slab, :] = jnp.exp(p['acs_last_e']) * h_in + states
        zv = z_ref[...]
        yg = y * zv * _sigmoid(zv)
        r = lax.rsqrt(jnp.mean(yg * yg, axis=-1, keepdims=True) + EPS)
        yb_ref[...] = (yg * r * gain_ref[...]).astype(BF16)

    in_specs = _ssd_in_specs(lambda i: i) + [pl.BlockSpec(memory_space=pl.ANY)]
    out_specs = [pl.BlockSpec((CHUNK, GW), lambda i, g: (i, D_A // GW + g)),
                 pl.BlockSpec((CHUNK, GW), lambda i, g: (i, g)),
                 pl.BlockSpec((None, D_STATE, GW), lambda i, g: (i, g, 0))]
    out_shape = [jax.ShapeDtypeStruct((t, D_MIX), BF16), jax.ShapeDtypeStruct((t, D_SSM), F32),
                 jax.ShapeDtypeStruct((nc, ng * D_STATE, GW), F32)]
    return pl.pallas_call(
        body, name="ssd_fwd", grid=(nc, ng), in_specs=in_specs, out_specs=out_specs, out_shape=out_shape,
        scratch_shapes=[pltpu.VMEM((ng * D_STATE, GW), F32)] + _ssd_scratch(),
        input_output_aliases={n_in: 0},
        compiler_params=_cparams(("arbitrary", "arbitrary")),
    )(proj, proj, proj, proj, proj, proj, proj, dt_raw, cw, cw, cw, cb, cb, cb, dtb, alog, de, gain,
      tri_c, eh_c, e128_c, yab)


def _rows8(vals, width):
    rid = lax.broadcasted_iota(jnp.int32, (8, width), 0)
    out = jnp.zeros((8, width), F32)
    for k, v in enumerate(vals):
        if v is not None:
            out = out + jnp.where(rid == k, v, 0.0)
    return out


def _ssd_bwd(proj, dt_raw, dyab, ysave, hs, cw, cb, dtb, alog, de, gain, consts):
    t = proj.shape[0]
    nc = t // CHUNK
    ng = N_SSM_GROUPS
    tri_c, eh_c, e128_c = consts
    n_in = 21

    def body(*refs):
        ins = refs[:n_in]
        dy_ref, ys_ref, hs_ref = refs[n_in:n_in + 3]
        (dz_ref, dxs_ref, db_ref, dc_ref, ddt_ref, acc_x, acc_b, acc_c, acc_gain,
         acc_head) = refs[n_in + 3:n_in + 13]
        dh_sc, car_x, car_b, car_c, dext_x, dext_b, dext_c = refs[n_in + 13:n_in + 20]
        scr = refs[n_in + 20:]
        ext_x, ext_b, ext_c, acst_sc, _ = scr
        cc, g = pl.program_id(0), pl.program_id(1)
        c = nc - 1 - cc
        z_ref, cwx, cwb, cwc, de_ref, gain_ref = ins[0], ins[8], ins[9], ins[10], ins[16], ins[17]
        slab = pl.ds(pl.multiple_of(g * D_STATE, D_STATE), D_STATE)

        @pl.when(cc == 0)
        def _():
            dh_sc[slab, :] = jnp.zeros((D_STATE, GW), F32)
            car_x[g] = jnp.zeros((8, GW), F32)
            car_b[g] = jnp.zeros((8, D_STATE), F32)
            car_c[g] = jnp.zeros((8, D_STATE), F32)

        @pl.when(jnp.logical_and(cc == 0, g == 0))
        def _():
            for a in (acc_x, acc_b, acc_c, acc_gain, acc_head):
                a[...] = jnp.zeros_like(a)

        p = _ssd_pre(c == 0, g, ins, scr)
        tri = _tri_mask()
        xs, dt_e, acs_e, xdt, decay_e = p['xs'], p['dt_e'], p['acs_e'], p['xdt'], p['decay_e']
        cm_b, bm_b, scores, eh = p['cm_b'], p['bm_b'], p['scores'], p['eh']
        h_in = hs_ref[...]
        h_in_b = h_in.astype(BF16)
        e_a = jnp.exp(acs_e)
        raw = _dot(cm_b, h_in_b)

        y = ys_ref[...]
        zv = z_ref[...]
        sz, dsz = _silu_and_grad(zv)
        yg = y * sz
        r = lax.rsqrt(jnp.mean(yg * yg, axis=-1, keepdims=True) + EPS)
        xh = yg * r
        dout = dy_ref[...]
        gain = gain_ref[...]
        dxh = dout * gain
        dyg = r * (dxh - xh * jnp.mean(dxh * xh, axis=-1, keepdims=True))
        dy = dyg * sz
        dz_ref[...] = (dyg * y * dsz).astype(BF16)
        acc_gain[g] += _rows8([jnp.sum(dout * xh, axis=0, keepdims=True)], GW)
        d_skip8 = _x01_nt(_rows8([None, None, jnp.sum(dy * xs, axis=0, keepdims=True)], GW), eh)
        dxs = de_ref[...] * dy

        q = dy * raw * e_a
        draw = (dy * e_a).astype(BF16)
        d_c = _dot_nt(draw, h_in_b)
        dh_in = _dot_tn(cm_b, draw)

        lane = lax.broadcasted_iota(jnp.int32, (CHUNK, 128), 1)
        ones_b = jnp.ones((CHUNK, 128), BF16)
        dscores = jnp.zeros((CHUNK, CHUNK), F32)
        dxdt_slabs, q_slabs = [], []
        for r2 in range(HEADS_PER_GROUP // 2):
            sl = slice(r2 * 128, (r2 + 1) * 128)
            xb = xdt[:, sl].astype(BF16)
            dys = dy[:, sl]
            dys_b = dys.astype(BF16)
            dxh_pair, qv_pair = [], []
            for half in range(2):
                lmat = _ssd_l(p, g, 2 * r2 + half, acst_sc, tri)
                m = scores * lmat
                mine = (lane < HEAD_DIM) if half == 0 else (lane >= HEAD_DIM)
                dm = _dot_nt(jnp.where(mine, dys, 0.0).astype(BF16), xb)
                dscores = dscores + dm * lmat
                gm = dm * m
                dxh_pair.append(_dot_tn(m.astype(BF16), dys_b))
                h3 = _split3(gm)
                colsum = _dot_tn(h3[0], ones_b) + _dot_tn(h3[1], ones_b) + _dot_tn(h3[2], ones_b)
                qv_pair.append(jnp.sum(gm, axis=-1, keepdims=True) - colsum)
            dxdt_slabs.append(jnp.where(lane < HEAD_DIM, dxh_pair[0], dxh_pair[1]))
            q_slabs.append(jnp.where(lane == 0, qv_pair[0], 0.0) + jnp.where(lane == HEAD_DIM, qv_pair[1], 0.0))
        dxdt = jnp.concatenate(dxdt_slabs, axis=1)
        q = q + jnp.concatenate(q_slabs, axis=1)

        dh_out = dh_sc[slab, :]
        dh_out_b = dh_out.astype(BF16)
        e_l = jnp.exp(p['acs_last_e'])
        dh_sc[slab, :] = dh_in + e_l * dh_out
        dlast = jnp.sum(dh_out * h_in, axis=0, keepdims=True) * e_l
        dxd = _dot(bm_b, dh_out_b)
        xd = xdt * decay_e
        dxdt = dxdt + dxd * decay_e
        dd = dxd * xd
        q = q - dd
        dlast = dlast + jnp.sum(dd, axis=0, keepdims=True)
        d_b = _dot_nt(xd.astype(BF16), dh_out_b)
        dsc_b = dscores.astype(BF16)
        d_c = d_c + _dot(dsc_b, bm_b)
        d_b = d_b + _dot_tn(dsc_b, cm_b)

        dxs = dxs + dxdt * dt_e
        rid = lax.broadcasted_iota(jnp.int32, (CHUNK, GW), 0)
        q = q + jnp.where(rid == CHUNK - 1, dlast, 0.0)
        dacs = _x01_nt(q, eh)
        ddt = _x01_nt(dxdt * xs, eh)
        dadt = _e01x_tn(p['tri_b'], dacs)
        ddt = ddt + dadt * p['a']
        d_a = jnp.sum(dadt * p['dt'], axis=0, keepdims=True)
        ddt_raw = ddt * _sigmoid(p['dt_in'])
        acc_head[...] += _rows8([jnp.sum(ddt_raw, axis=0, keepdims=True), d_a * p['a']], HPAD) + d_skip8

        @pl.when(g == 0)
        def _():
            ddt_ref[...] = ddt_raw

        @pl.when(g > 0)
        def _():
            ddt_ref[...] += ddt_raw

        for dv, dsil, ext, dext, car, acc, w_ref, o_ref in (
                (dxs, p['dsx'], ext_x, dext_x, car_x, acc_x, cwx, dxs_ref),
                (d_b, p['dsb'], ext_b, dext_b, car_b, acc_b, cwb, db_ref),
                (d_c, p['dsc'], ext_c, dext_c, car_c, acc_c, cwc, dc_ref)):
            dp = dv * dsil
            width = dp.shape[1]
            rows = [jnp.sum(ext[pl.ds(5 + k, CHUNK), :] * dp, axis=0, keepdims=True) for k in range(SSM_CONV)]
            rows.append(jnp.sum(dp, axis=0, keepdims=True))
            acc[g] += _rows8(rows, width)
            dext[0:CHUNK, :] = dp
            dext[CHUNK:CHUNK + 8, :] = car[g]
            car[g] = dext[0:8, :]
            dx = w_ref[SSM_CONV - 1:SSM_CONV, :] * dext[pl.ds(0, CHUNK), :]
            for k in range(SSM_CONV - 1):
                dx = dx + w_ref[k:k + 1, :] * dext[pl.ds(SSM_CONV - 1 - k, CHUNK), :]
            o_ref[...] = dx.astype(BF16)

    def cmap(i):
        return nc - 1 - i

    in_specs = _ssd_in_specs(cmap) + [
        pl.BlockSpec((CHUNK, GW), lambda i, g: (cmap(i), D_A // GW + g)),
        pl.BlockSpec((CHUNK, GW), lambda i, g: (cmap(i), g)),
        pl.BlockSpec((None, D_STATE, GW), lambda i, g: (cmap(i), g, 0)),
    ]

    def full(shape):
        return pl.BlockSpec(shape, lambda i, g: (0,) * len(shape))

    out_specs = [
        pl.BlockSpec((CHUNK, GW), lambda i, g: (cmap(i), g)),
        pl.BlockSpec((CHUNK, GW), lambda i, g: (cmap(i), g)),
        pl.BlockSpec((CHUNK, D_STATE), lambda i, g: (cmap(i), g)),
        pl.BlockSpec((CHUNK, D_STATE), lambda i, g: (cmap(i), g)),
        pl.BlockSpec((CHUNK, HPAD), lambda i, g: (cmap(i), 0)),
        full((ng, 8, GW)), full((ng, 8, D_STATE)), full((ng, 8, D_STATE)),
        full((ng, 8, GW)), full((8, HPAD)),
    ]
    out_shape = [
        jax.ShapeDtypeStruct((t, D_SSM), BF16), jax.ShapeDtypeStruct((t, D_SSM), BF16),
        jax.ShapeDtypeStruct((t, D_BC), BF16), jax.ShapeDtypeStruct((t, D_BC), BF16),
        jax.ShapeDtypeStruct((t, HPAD), F32),
        jax.ShapeDtypeStruct((ng, 8, GW), F32), jax.ShapeDtypeStruct((ng, 8, D_STATE), F32),
        jax.ShapeDtypeStruct((ng, 8, D_STATE), F32), jax.ShapeDtypeStruct((ng, 8, GW), F32),
        jax.ShapeDtypeStruct((8, HPAD), F32),
    ]
    scratch = [pltpu.VMEM((ng * D_STATE, GW), F32),
               pltpu.VMEM((ng, 8, GW), F32), pltpu.VMEM((ng, 8, D_STATE), F32), pltpu.VMEM((ng, 8, D_STATE), F32),
               pltpu.VMEM((CHUNK + 8, GW), F32), pltpu.VMEM((CHUNK + 8, D_STATE), F32),
               pltpu.VMEM((CHUNK + 8, D_STATE), F32)] + _ssd_scratch()
    return pl.pallas_call(
        body, name="ssd_bwd", grid=(nc, ng), in_specs=in_specs, out_specs=out_specs, out_shape=out_shape,
        scratch_shapes=scratch, compiler_params=_cparams(("arbitrary", "arbitrary")),
    )(proj, proj, proj, proj, proj, proj, proj, dt_raw, cw, cw, cw, cb, cb, cb, dtb, alog, de, gain,
      tri_c, eh_c, e128_c, dyab, ysave, hs)


ANY = pl.BlockSpec(memory_space=pl.ANY)


def _place():
    x, y, c = lax.axis_index("x"), lax.axis_index("y"), lax.axis_index("c")
    chips = [(1 - x, y), (x, 1 - y), (1 - x, 1 - y)]
    return x, y, c, chips


def _rcopy(src, dst, send_sems, recv_sems, k, dev):
    return pltpu.make_async_remote_copy(src_ref=src, dst_ref=dst, send_sem=send_sems.at[k],
                                        recv_sem=recv_sems.at[k], device_id=dev, device_id_type=MESH)


def _gather_weights(bigs, smalls):
    nb, ns = len(bigs), len(smalls)
    na = nb + ns
    nsem = 6 * nb + 3 * ns

    def body(*refs):
        ins, outs = refs[:na], refs[na:2 * na]
        send_sems, recv_sems, loc_sems = refs[2 * na:]
        x, y, c, chips = _place()
        q = 2 * x + y
        sib = (x, y, 1 - c)
        locs, sends = [], []
        for a in range(na):
            cp = pltpu.make_async_copy(ins[a], outs[a].at[q], loc_sems.at[a])
            cp.start()
            locs.append(cp)
        for a in range(nb):
            half = bigs[a].shape[0] // 2
            mine = pl.ds(c * half, half)
            for k, chip in enumerate(chips):
                cp = _rcopy(ins[a].at[mine], outs[a].at[q, mine], send_sems, recv_sems, 6 * a + k,
                            (chip[0], chip[1], c))
                cp.start()
                sends.append(cp)
        for s in range(ns):
            a = nb + s
            for k, chip in enumerate(chips):
                cp = _rcopy(ins[a], outs[a].at[q], send_sems, recv_sems, 6 * nb + 3 * s + k,
                            (chip[0], chip[1], c))
                cp.start()
                sends.append(cp)
        for a in range(nb):
            half = bigs[a].shape[0] // 2
            mine = pl.ds(c * half, half)
            for k, chip in enumerate(chips):
                qk = 2 * chip[0] + chip[1]
                _rcopy(ins[a].at[mine], outs[a].at[qk, mine], send_sems, recv_sems, 6 * a + k, sib).wait_recv()
                cp = _rcopy(outs[a].at[qk, mine], outs[a].at[qk, mine], send_sems, recv_sems, 6 * a + 3 + k, sib)
                cp.start()
                sends.append(cp)
        for a in range(nb):
            half = bigs[a].shape[0] // 2
            other = pl.ds((1 - c) * half, half)
            for k, chip in enumerate(chips):
                qk = 2 * chip[0] + chip[1]
                _rcopy(ins[a].at[other], outs[a].at[qk, other], send_sems, recv_sems, 6 * a + 3 + k,
                       sib).wait_recv()
        for s in range(ns):
            a = nb + s
            for k, chip in enumerate(chips):
                qk = 2 * chip[0] + chip[1]
                _rcopy(ins[a], outs[a].at[qk], send_sems, recv_sems, 6 * nb + 3 * s + k, sib).wait_recv()
        for cp in sends:
            cp.wait_send()
        for cp in locs:
            cp.wait()

    arrs = list(bigs) + list(smalls)
    return pl.pallas_call(
        body, name="gather_weights", in_specs=[ANY] * na, out_specs=[ANY] * na,
        out_shape=[jax.ShapeDtypeStruct((N_CHIPS,) + a.shape, a.dtype) for a in arrs],
        scratch_shapes=[pltpu.SemaphoreType.DMA((nsem,)), pltpu.SemaphoreType.DMA((nsem,)),
                        pltpu.SemaphoreType.DMA((na,))],
    )(*arrs)


def _rs_sibling(gs):
    na = len(gs)

    def body(*refs):
        ins, mine_o, recv_o = refs[:na], refs[na:2 * na], refs[2 * na:3 * na]
        send_sems, recv_sems, loc_sems = refs[3 * na:]
        x, y, c, _ = _place()
        cps = []
        for a in range(na):
            half = gs[a].shape[1] // 2
            loc = pltpu.make_async_copy(ins[a].at[:, pl.ds(c * half, half), :], mine_o[a], loc_sems.at[a])
            snd = _rcopy(ins[a].at[:, pl.ds((1 - c) * half, half), :], recv_o[a], send_sems, recv_sems, a,
                         (x, y, 1 - c))
            loc.start()
            snd.start()
            cps += [loc, snd]
        for cp in cps:
            cp.wait()

    halves = [jax.ShapeDtypeStruct((N_CHIPS, g.shape[1] // 2, g.shape[2]), g.dtype) for g in gs]
    outs = pl.pallas_call(
        body, name="rs_sibling", in_specs=[ANY] * na, out_specs=[ANY] * (2 * na), out_shape=halves + halves,
        scratch_shapes=[pltpu.SemaphoreType.DMA((na,)), pltpu.SemaphoreType.DMA((na,)),
                        pltpu.SemaphoreType.DMA((na,))],
    )(*gs)
    return outs[:na], outs[na:]


def _rs_chips(ps):
    na = len(ps)

    def body(*refs):
        ins, own_o, recv_o = refs[:na], refs[na:2 * na], refs[2 * na:3 * na]
        send_sems, recv_sems, loc_sems = refs[3 * na:]
        x, y, c, chips = _place()
        q = 2 * x + y
        cps = []
        for a in range(na):
            loc = pltpu.make_async_copy(ins[a].at[q], own_o[a], loc_sems.at[a])
            loc.start()
            cps.append(loc)
            for k, chip in enumerate(chips):
                qk = 2 * chip[0] + chip[1]
                snd = _rcopy(ins[a].at[qk], recv_o[a].at[k], send_sems, recv_sems, 3 * a + k,
                             (chip[0], chip[1], c))
                snd.start()
                cps.append(snd)
        for cp in cps:
            cp.wait()

    own = [jax.ShapeDtypeStruct(p.shape[1:], p.dtype) for p in ps]
    rcv = [jax.ShapeDtypeStruct((3,) + p.shape[1:], p.dtype) for p in ps]
    outs = pl.pallas_call(
        body, name="rs_chips", in_specs=[ANY] * na, out_specs=[ANY] * (2 * na), out_shape=own + rcv,
        scratch_shapes=[pltpu.SemaphoreType.DMA((3 * na,)), pltpu.SemaphoreType.DMA((3 * na,)),
                        pltpu.SemaphoreType.DMA((na,))],
    )(*ps)
    return outs[:na], outs[na:]


def _share_sibling(fs):
    na = len(fs)

    def body(*refs):
        ins, outs = refs[:na], refs[na:2 * na]
        send_sems, recv_sems, loc_sems = refs[2 * na:]
        x, y, c, _ = _place()
        cps = []
        for a in range(na):
            half = fs[a].shape[0]
            rows = pl.ds(c * half, half)
            loc = pltpu.make_async_copy(ins[a], outs[a].at[rows], loc_sems.at[a])
            snd = _rcopy(ins[a], outs[a].at[rows], send_sems, recv_sems, a, (x, y, 1 - c))
            loc.start()
            snd.start()
            cps += [loc, snd]
        for cp in cps:
            cp.wait()

    return pl.pallas_call(
        body, name="share_sibling", in_specs=[ANY] * na, out_specs=[ANY] * na,
        out_shape=[jax.ShapeDtypeStruct((2 * f.shape[0], f.shape[1]), f.dtype) for f in fs],
        scratch_shapes=[pltpu.SemaphoreType.DMA((na,)), pltpu.SemaphoreType.DMA((na,)),
                        pltpu.SemaphoreType.DMA((na,))],
    )(*fs)


def _allgather_small(buf):
    def body(in_ref, out_ref, send_sems, recv_sems, loc_sem):
        x, y, c, _ = _place()
        me = 4 * x + 2 * y + c
        loc = pltpu.make_async_copy(in_ref, out_ref.at[me], loc_sem)
        loc.start()
        cps = [loc]
        for k in range(1, 8):
            dev = (1 - x if k & 4 else x, 1 - y if k & 2 else y, 1 - c if k & 1 else c)
            snd = _rcopy(in_ref, out_ref.at[me], send_sems, recv_sems, k - 1, dev)
            snd.start()
            cps.append(snd)
        for cp in cps:
            cp.wait()

    return pl.pallas_call(
        body, name="allgather_small", in_specs=[ANY], out_specs=ANY,
        out_shape=jax.ShapeDtypeStruct((8,) + buf.shape, buf.dtype),
        scratch_shapes=[pltpu.SemaphoreType.DMA((7,)), pltpu.SemaphoreType.DMA((7,)), pltpu.SemaphoreType.DMA],
    )(buf)


def _row_tile(rows, cols, nbuf):
    budget = 24 * 1024 * 1024 // (nbuf * cols * 4 * 2)
    return _tile(rows, max(16, budget - budget % 16), 16) if rows % 16 == 0 else rows


def _add_pairs(a, b, name):
    s, r, c = a.shape
    tr = _row_tile(r, c, 3)

    def body(a_ref, b_ref, o_ref):
        o_ref[...] = (a_ref[...].astype(F32) + b_ref[...].astype(F32)).astype(o_ref.dtype)

    blk = pl.BlockSpec((None, tr, c), lambda j, i: (j, i, 0))
    return pl.pallas_call(
        body, name=name, grid=(s, r // tr), in_specs=[blk, blk], out_specs=blk,
        out_shape=jax.ShapeDtypeStruct(a.shape, a.dtype), compiler_params=_cparams(("parallel", "parallel")),
    )(a, b)


def _sum_chips(own, rcv, name):
    r, c = own.shape
    tr = _row_tile(r, c, 5)

    def body(o_ref, r_ref, out_ref):
        acc = o_ref[...].astype(F32)
        for k in range(3):
            acc = acc + r_ref[k].astype(F32)
        out_ref[...] = acc

    return pl.pallas_call(
        body, name=name, grid=(r // tr,),
        in_specs=[pl.BlockSpec((tr, c), lambda i: (i, 0)), pl.BlockSpec((3, tr, c), lambda i: (0, i, 0))],
        out_specs=pl.BlockSpec((tr, c), lambda i: (i, 0)),
        out_shape=jax.ShapeDtypeStruct((r, c), F32), compiler_params=_cparams(("parallel",)),
    )(own, rcv)


def _sum_devices(parts):
    _, n, _ = parts.shape
    tr = _tile(n, 512, 8)

    def body(p_ref, o_ref):
        acc = p_ref[0]
        for k in range(1, 8):
            acc = acc + p_ref[k]
        o_ref[...] = acc

    return pl.pallas_call(
        body, name="sum_devices", grid=(n // tr,),
        in_specs=[pl.BlockSpec((8, tr, 128), lambda i: (0, i, 0))],
        out_specs=pl.BlockSpec((tr, 128), lambda i: (i, 0)),
        out_shape=jax.ShapeDtypeStruct((n, 128), F32), compiler_params=_cparams(("parallel",)),
    )(parts)


def _adamw(w, g, m, v, name):
    r, c = w.shape
    tr = _row_tile(r, c, 7)
    c1 = 1.0 - ADAM_B1 ** ADAM_STEP
    c2 = 1.0 - ADAM_B2 ** ADAM_STEP

    def body(w_ref, g_ref, m_ref, v_ref, d_ref, mo_ref, vo_ref):
        gv = g_ref[...]
        mn = ADAM_B1 * m_ref[...] + (1.0 - ADAM_B1) * gv
        vn = ADAM_B2 * v_ref[...] + (1.0 - ADAM_B2) * (gv * gv)
        mo_ref[...] = mn
        vo_ref[...] = vn
        m_hat = mn / c1
        v_hat = vn / c2
        d_ref[...] = -ADAM_LR * (m_hat / (jnp.sqrt(v_hat) + ADAM_EPS) + ADAM_WD * w_ref[...])

    blk = pl.BlockSpec((tr, c), lambda i: (i, 0))
    sh = jax.ShapeDtypeStruct((r, c), F32)
    return pl.pallas_call(
        body, name=name, grid=(r // tr,), in_specs=[blk] * 4, out_specs=[blk] * 3, out_shape=[sh] * 3,
        compiler_params=_cparams(("parallel",)),
    )(w, g, m, v)


WEIGHTS = ['norm_mix_g', 'w_in', 'ln_a_g', 'ln_a_b', 'w_s', 'b_s', 'norm_a_g', 'conv_ssm_w', 'conv_ssm_b',
           'dt_bias', 'a_log', 'd_skip', 'ssm_norm_g', 'w_out', 'norm_ffn_g', 'w_up', 'conv_ffn_w',
           'conv_ffn_b', 'w_down', 'norm_ple_g', 'w_ple_gate', 'w_ple', 'norm_final_g']
BIG = ['w_in', 'w_out', 'w_up', 'w_down', 'w_ple_gate', 'w_ple']
SMALL = [n for n in WEIGHTS if n not in BIG]
PACK_ALIGN = 2048


def _pack(arrs):
    parts = []
    for a in arrs:
        f = a.reshape(-1).astype(F32)
        parts.append(jnp.pad(f, (0, (-f.shape[0]) % PACK_ALIGN)))
    return jnp.concatenate(parts).reshape(-1, 128)


def _unpack(buf, shapes):
    flat = buf.reshape(-1)
    out, off = [], 0
    for s in shapes:
        n = math.prod(s)
        out.append(flat[off:off + n].reshape(s))
        off += n + (-n) % PACK_ALIGN
    return out


def _pad_heads(v):
    return jnp.pad(v, ((0, 0), (0, HPAD - v.shape[1])))


def _col_sharded(full):
    r, c4 = full.shape
    return jnp.transpose(full.reshape(r, N_CHIPS, c4 // N_CHIPS), (1, 0, 2))


def _from_col_sharded(g):
    s, r, c = g.shape
    return jnp.transpose(g, (1, 0, 2)).reshape(r, s * c)


def kernel(x, p, norm_mix_g, w_in, ln_a_g, ln_a_b, w_s, b_s, norm_a_g, conv_ssm_w, conv_ssm_b, dt_bias, a_log, d_skip, ssm_norm_g, w_out, norm_ffn_g, w_up, conv_ffn_w, conv_ffn_b, w_down, norm_ple_g, w_ple_gate, w_ple, norm_final_g, loss_target, m_norm_mix_g, m_w_in, m_ln_a_g, m_ln_a_b, m_w_s, m_b_s, m_norm_a_g, m_conv_ssm_w, m_conv_ssm_b, m_dt_bias, m_a_log, m_d_skip, m_ssm_norm_g, m_w_out, m_norm_ffn_g, m_w_up, m_conv_ffn_w, m_conv_ffn_b, m_w_down, m_norm_ple_g, m_w_ple_gate, m_w_ple, m_norm_final_g, v_norm_mix_g, v_w_in, v_ln_a_g, v_ln_a_b, v_w_s, v_b_s, v_norm_a_g, v_conv_ssm_w, v_conv_ssm_b, v_dt_bias, v_a_log, v_d_skip, v_ssm_norm_g, v_w_out, v_norm_ffn_g, v_w_up, v_conv_ffn_w, v_conv_ffn_b, v_w_down, v_norm_ple_g, v_w_ple_gate, v_w_ple, v_norm_final_g):
    given = dict(locals())
    wts = {n: given[n] for n in WEIGHTS}
    mom = {n: given['m_' + n] for n in WEIGHTS}
    var = {n: given['v_' + n] for n in WEIGHTS}
    d = D_MODEL
    xt, pt, tgt = x[0], p[0, 0], loss_target[0]
    chip = 2 * lax.axis_index("x") + lax.axis_index("y")

    g_in, g_out, g_up, g_down, g_pg, g_ple, g_cs, g_cf = _gather_weights(
        [wts[n][0].astype(BF16) for n in BIG], [conv_ssm_w[0], conv_ffn_w[0]])
    w_in_full = _from_col_sharded(g_in)
    w_main = w_in_full[:, :D_MAIN]
    w_dt = _pad_heads(w_in_full[:, D_MAIN:])
    w_out_f = g_out.reshape(D_MIX, d)
    w_down_f = g_down.reshape(D_FF, d)
    w_pg_f = g_pg.reshape(d, d)
    cs_w = _from_col_sharded(g_cs)
    cf_w = _from_col_sharded(g_cf)
    consts = _ssd_consts()
    dtb, alog = _pad_heads(dt_bias), _pad_heads(a_log)
    de = jnp.repeat(d_skip[0], HEAD_DIM)[None, :]
    b_exp = jnp.broadcast_to(b_s[0][:, :, None], (N_GROUPS_A, CHUNK, CHUNK))

    a1 = _rms_fwd(xt, norm_mix_g, "rms_mix")
    proj = _matmul(a1, w_main, mode='nn', name="mm_proj", tm=512, tn=1024)
    dt_raw = _matmul(a1, w_dt, mode='nn', name="mm_dt", tm=1024, tn=128)
    yab = _gmlp_fwd(proj, ln_a_g, ln_a_b, w_s[0], b_exp, norm_a_g)
    yab, ysave, hs = _ssd_fwd(proj, dt_raw, yab, cs_w, conv_ssm_b, dtb, alog, de, ssm_norm_g, consts)
    h1 = _matmul(yab, w_out_f, mode='nn', name="mm_out", res=xt, tm=512, tn=1024, tk=4096)
    f = _rms_fwd(h1, norm_ffn_g, "rms_ffn")
    hid = _matmul(f, g_up, mode='nn', name="mm_up", b_sharded=True, tm=512, tn=1408)
    act = _ffn_act_fwd(hid, cf_w, conv_ffn_b)
    h2 = _matmul(act, w_down_f, mode='nn', name="mm_down", res=h1, tm=512, tn=1024, tk=2816)
    n3 = _rms_fwd(h2, norm_ple_g, "rms_ple")
    gl = _matmul(n3, w_pg_f, mode='nn', name="mm_pg", tm=512, tn=1024)
    pe = _matmul(pt, g_ple, mode='nn', name="mm_ple", b_sharded=True, tm=1024, tn=512)
    dh3, dgl, dpe, lossv, dgf = _tail(h2, gl, pe, tgt, norm_final_g[None, :])

    gs_ple = _matmul(pt, dpe, mode='tn', name="mm_dw_ple", out_dtype=BF16, out_shards=N_CHIPS,
                     tm=256, tn=512, tk=1024)
    gs_pg = _matmul(n3, dgl, mode='tn', name="mm_dw_pg", out_dtype=BF16, tm=1024, tn=1024, tk=512)
    dn3 = _matmul(dgl, w_pg_f, mode='nt', name="mm_dn3", tm=512, tn=1024)
    dh2, dg_ple = _rms_bwd(h2, norm_ple_g, dn3, dh3, "rms_ple_bwd")
    dact = _matmul(dh2, w_down_f, mode='nt', name="mm_dact", out_dtype=BF16, tm=512, tn=1408)
    gs_down = _matmul(act, dh2, mode='tn', name="mm_dw_down", out_dtype=BF16, tm=1408, tn=1024, tk=512)
    dgc, duc, wg_acc, wu_acc = _ffn_act_bwd_a(hid, cf_w, conv_ffn_b, dact)
    dpre = _ffn_act_bwd_b(dgc, duc, cf_w)
    gs_up = _matmul(f, dpre, mode='tn', name="mm_dw_up", out_dtype=BF16, out_shards=N_CHIPS,
                    tm=1024, tn=1408, tk=512)
    df = _matmul(dpre, g_up, mode='nt', name="mm_df", b_sharded=True, tm=1024, tn=1024, tk=1408)
    dh1, dg_ffn = _rms_bwd(h1, norm_ffn_g, df, dh2, "rms_ffn_bwd")
    dyab = _matmul(dh1, w_out_f, mode='nt', name="mm_dyab", tm=512, tn=1024)
    gs_out = _matmul(yab, dh1, mode='tn', name="mm_dw_out", out_dtype=BF16, tm=1024, tn=1024, tk=512)
    duv, dws, dbs, dlng, dlnb, dnag = _gmlp_bwd(proj, dyab, ln_a_g, ln_a_b, w_s[0], b_exp, norm_a_g)
    dz, dxs, dbm, dcm, ddt_raw, acc_x, acc_b, acc_c, acc_gain, acc_head = _ssd_bwd(
        proj, dt_raw, dyab, ysave, hs, cs_w, conv_ssm_b, dtb, alog, de, ssm_norm_g, consts)
    dproj = jnp.concatenate([duv, dz, dxs, dbm, dcm], axis=1)
    dw_main = _matmul(a1, dproj, mode='tn', name="mm_dw_main", out_dtype=BF16, tm=1024, tn=1024, tk=512)
    dw_dt = _matmul(a1, ddt_raw, mode='tn', name="mm_dw_dt", out_dtype=BF16, tm=1024, tn=128, tk=512)
    gs_in = _col_sharded(jnp.concatenate([dw_main, dw_dt[:, :N_HEADS]], axis=1))
    da_dt = _matmul(ddt_raw, w_dt, mode='nt', name="mm_da_dt", tm=1024, tn=1024)
    da = _matmul(dproj, w_main, mode='nt', name="mm_da", res=da_dt, tm=1024, tn=1024, tk=2048)
    dx, dg_mix = _rms_bwd(xt, norm_mix_g, da, dh1, "rms_mix_bwd")

    gs = [gs_in, gs_out.reshape(N_CHIPS, D_MIX // N_CHIPS, d), gs_up,
          gs_down.reshape(N_CHIPS, D_FF // N_CHIPS, d), gs_pg.reshape(N_CHIPS, d // N_CHIPS, d), gs_ple]
    mine, sib = _rs_sibling(gs)
    chip_part = [_add_pairs(a, b, "rs_add_%d" % i) for i, (a, b) in enumerate(zip(mine, sib))]
    own, rcv = _rs_chips(chip_part)
    halves = [_sum_chips(a, b, "rs_sum_%d" % i) for i, (a, b) in enumerate(zip(own, rcv))]
    g_big = dict(zip(BIG, _share_sibling(halves)))

    def conv_rows(acc, k):
        return acc[:, k, :].reshape(1, -1)

    dcw = jnp.concatenate([jnp.concatenate([conv_rows(acc_x, k), conv_rows(acc_b, k), conv_rows(acc_c, k)], axis=1)
                           for k in range(SSM_CONV)], axis=0)
    dcb = jnp.concatenate([conv_rows(acc_x, SSM_CONV), conv_rows(acc_b, SSM_CONV), conv_rows(acc_c, SSM_CONV)], axis=1)
    part = {
        'norm_mix_g': dg_mix, 'ln_a_g': dlng, 'ln_a_b': dlnb, 'w_s': dws, 'b_s': dbs, 'norm_a_g': dnag,
        'conv_ssm_w': dcw, 'conv_ssm_b': dcb,
        'dt_bias': acc_head[0:1, :N_HEADS], 'a_log': acc_head[1:2, :N_HEADS], 'd_skip': acc_head[2:3, :N_HEADS],
        'ssm_norm_g': acc_gain[:, 0, :], 'norm_ffn_g': dg_ffn,
        'conv_ffn_w': jnp.concatenate([wg_acc[:FFN_CONV], wu_acc[:FFN_CONV]], axis=1),
        'conv_ffn_b': jnp.concatenate([wg_acc[FFN_CONV:FFN_CONV + 1], wu_acc[FFN_CONV:FFN_CONV + 1]], axis=1),
        'norm_ple_g': dg_ple, 'norm_final_g': dgf,
    }
    full_shapes = {n: wts[n].shape for n in SMALL}
    full_shapes['conv_ssm_w'] = (1, SSM_CONV, D_XBC)
    full_shapes['conv_ffn_w'] = (1, FFN_CONV, 2 * D_FF)
    packed = _pack([part[n] for n in SMALL] + [lossv[:, 0:1]])
    total = _sum_devices(_allgather_small(packed))
    pieces = _unpack(total, [full_shapes[n] for n in SMALL] + [(1,)])
    g_small = dict(zip(SMALL, pieces[:-1]))
    loss = pieces[-1][0]
    for n in ('conv_ssm_w', 'conv_ffn_w'):
        width = wts[n].shape[2]
        g_small[n] = lax.dynamic_slice_in_dim(g_small[n], chip * width, width, axis=2)

    grads, delta, new_m, new_v = {}, {}, {}, {}
    for n in BIG:
        shp = wts[n].shape
        dl, mn, vn = _adamw(wts[n][0], g_big[n], mom[n][0], var[n][0], "adamw_" + n)
        grads[n], delta[n], new_m[n], new_v[n] = (g_big[n].reshape(shp), dl.reshape(shp), mn.reshape(shp),
                                                  vn.reshape(shp))
    shapes = [wts[n].shape for n in SMALL]
    dl, mn, vn = _adamw(_pack([wts[n] for n in SMALL]), _pack([g_small[n] for n in SMALL]),
                        _pack([mom[n] for n in SMALL]), _pack([var[n] for n in SMALL]), "adamw_small")
    for n, a, b, c in zip(SMALL, _unpack(dl, shapes), _unpack(mn, shapes), _unpack(vn, shapes)):
        grads[n], delta[n], new_m[n], new_v[n] = g_small[n], a, b, c

    return (loss, dx[None], *[grads[n] for n in WEIGHTS], *[delta[n] for n in WEIGHTS],
            *[new_m[n] for n in WEIGHTS], *[new_v[n] for n in WEIGHTS])
```

```python
import functools
import math

import jax
import jax.numpy as jnp
from jax import lax
from jax.experimental import pallas as pl
from jax.experimental.pallas import tpu as pltpu

D_MODEL = 2048
SEQ = 8192
D_MIX = 2 * D_MODEL
D_A = D_MIX // 2
CHUNK = 128
N_GROUPS_A = D_A // 128
D_SSM = D_MIX - D_A
HEAD_DIM = 64
N_HEADS = D_SSM // HEAD_DIM
HEADS_PER_GROUP = 4
N_SSM_GROUPS = N_HEADS // HEADS_PER_GROUP
GW = HEADS_PER_GROUP * HEAD_DIM
D_STATE = 128
SSM_CONV = 4
D_BC = N_SSM_GROUPS * D_STATE
D_XBC = D_SSM + 2 * D_BC
D_MAIN = 2 * D_A + D_SSM + D_XBC
D_IN = D_MAIN + N_HEADS
D_FF = (D_MODEL * 11) // 4
FFN_CONV = 3
D_PLE = 256
EPS = 1e-6
HPAD = 128
N_CHIPS = 4

ADAM_LR = 0.001
ADAM_B1 = 0.9
ADAM_B2 = 0.999
ADAM_EPS = 1e-08
ADAM_WD = 0.01
ADAM_STEP = 10

F32 = jnp.float32
BF16 = jnp.bfloat16
MESH = pl.DeviceIdType.MESH
VMEM_LIMIT = 56 * 1024 * 1024


def _cparams(sem):
    return pltpu.CompilerParams(dimension_semantics=sem, vmem_limit_bytes=VMEM_LIMIT)


def _tile(n, pref, mult):
    t = min(pref, n)
    t -= t % mult
    while n % t:
        t -= mult
    return t


def _dot(a, b):
    return jnp.dot(a, b, preferred_element_type=F32)


def _dot_nt(a, b):
    return lax.dot_general(a, b, (((1,), (1,)), ((), ())), preferred_element_type=F32)


def _dot_tn(a, b):
    return lax.dot_general(a, b, (((0,), (0,)), ((), ())), preferred_element_type=F32)


def _split3(x):
    hi = x.astype(BF16)
    r = x - hi.astype(F32)
    mid = r.astype(BF16)
    lo = (r - mid.astype(F32)).astype(BF16)
    return hi, mid, lo


def _x01(x, e):
    h, m, l = _split3(x)
    return _dot(h, e) + _dot(m, e) + _dot(l, e)


def _x01_nt(x, e):
    h, m, l = _split3(x)
    return _dot_nt(h, e) + _dot_nt(m, e) + _dot_nt(l, e)


def _e01x(e, x):
    h, m, l = _split3(x)
    return _dot(e, h) + _dot(e, m) + _dot(e, l)


def _e01x_tn(e, x):
    h, m, l = _split3(x)
    return _dot_tn(e, h) + _dot_tn(e, m) + _dot_tn(e, l)


def _sigmoid(x):
    return 1.0 / (1.0 + jnp.exp(-x))


_GELU_C = math.sqrt(2.0 / math.pi)


def _gelu_and_grad(x):
    x2 = x * x
    th = jnp.tanh(_GELU_C * (x + 0.044715 * x * x2))
    y = 0.5 * x * (1.0 + th)
    dy = 0.5 * (1.0 + th) + 0.5 * x * (1.0 - th * th) * (_GELU_C * (1.0 + 3.0 * 0.044715 * x2))
    return y, dy


def _silu_and_grad(x):
    s = _sigmoid(x)
    return x * s, s * (1.0 + x * (1.0 - s))


def _softplus(x):
    u = jnp.exp(-jnp.abs(x))
    w = 1.0 + u
    l1p = jnp.where(w == 1.0, u, jnp.log(w) * (u / (w - 1.0)))
    return jnp.maximum(x, 0.0) + l1p


def _matmul(a, b, *, mode, name, out_dtype=F32, res=None, tm=512, tn=512, tk=2048,
            b_sharded=False, out_shards=0):
    if mode == 'tn':
        kdim, m = a.shape
        n = b.shape[1]
    else:
        m, kdim = a.shape
        if b_sharded:
            s_b, d1, d2 = b.shape
            n = s_b * d2 if mode == 'nn' else d1
        else:
            n = b.shape[1] if mode == 'nn' else b.shape[0]
    per = None
    if b_sharded:
        per = b.shape[2]
    if out_shards:
        per = n // out_shards
    tm = _tile(m, tm, 128 if mode == 'tn' else 8)
    if mode == 'nt' and b_sharded:
        tn = _tile(n, tn, 128)
        tk = _tile(per, tk, 128)
    elif per is not None:
        tn = _tile(per, tn, 128)
        tk = _tile(kdim, tk, 128 if mode != 'tn' else 8)
    else:
        tn = _tile(n, tn, 128)
        tk = _tile(kdim, tk, 128 if mode != 'tn' else 8)
    nm, nn_, nk = m // tm, n // tn, kdim // tk
    has_res = res is not None

    def body(*refs):
        a_ref, b_ref = refs[0], refs[1]
        res_ref = refs[2] if has_res else None
        o_ref = refs[2 + has_res]
        av = a_ref[...].astype(BF16)
        bv = b_ref[...].astype(BF16)
        if mode == 'nn':
            p = _dot(av, bv)
        elif mode == 'nt':
            p = _dot_nt(av, bv)
        else:
            p = _dot_tn(av, bv)

        def fin(v):
            if has_res:
                v = v + res_ref[...]
            o_ref[...] = v.astype(o_ref.dtype)

        if nk == 1:
            fin(p)
        else:
            acc_ref = refs[-1]
            k = pl.program_id(2)

            @pl.when(k == 0)
            def _():
                acc_ref[...] = p

            @pl.when(k > 0)
            def _():
                acc_ref[...] += p

            @pl.when(k == nk - 1)
            def _():
                fin(acc_ref[...])

    if mode == 'nn':
        a_spec = pl.BlockSpec((tm, tk), lambda j, i, k: (i, k))
        if b_sharded:
            nps = per // tn
            b_spec = pl.BlockSpec((None, tk, tn), lambda j, i, k: (j // nps, k, j % nps))
        else:
            b_spec = pl.BlockSpec((tk, tn), lambda j, i, k: (k, j))
    elif mode == 'nt':
        a_spec = pl.BlockSpec((tm, tk), lambda j, i, k: (i, k))
        if b_sharded:
            kps = per // tk
            b_spec = pl.BlockSpec((None, tn, tk), lambda j, i, k: (k // kps, j, k % kps))
        else:
            b_spec = pl.BlockSpec((tn, tk), lambda j, i, k: (j, k))
    else:
        a_spec = pl.BlockSpec((tk, tm), lambda j, i, k: (k, i))
        b_spec = pl.BlockSpec((tk, tn), lambda j, i, k: (k, j))
    in_specs = [a_spec, b_spec]
    args = [a, b]
    if has_res:
        in_specs.append(pl.BlockSpec((tm, tn), lambda j, i, k: (i, j)))
        args.append(res)
    if out_shards:
        nps_o = per // tn
        out_shape = jax.ShapeDtypeStruct((out_shards, m, per), out_dtype)
        out_spec = pl.BlockSpec((None, tm, tn), lambda j, i, k: (j // nps_o, i, j % nps_o))
    else:
        out_shape = jax.ShapeDtypeStruct((m, n), out_dtype)
        out_spec = pl.BlockSpec((tm, tn), lambda j, i, k: (i, j))
    scratch = [pltpu.VMEM((tm, tn), F32)] if nk > 1 else []
    return pl.pallas_call(
        body, name=name, grid=(nn_, nm, nk), in_specs=in_specs, out_specs=out_spec,
        out_shape=out_shape, scratch_shapes=scratch,
        compiler_params=_cparams(("parallel", "parallel", "arbitrary")),
    )(*args)


def _rms_fwd(x, g, name):
    t, d = x.shape
    tt = _tile(t, 512, 8)

    def body(x_ref, g_ref, o_ref):
        xv = x_ref[...]
        r = lax.rsqrt(jnp.mean(xv * xv, axis=-1, keepdims=True) + EPS)
        o_ref[...] = (xv * r * g_ref[...]).astype(o_ref.dtype)

    return pl.pallas_call(
        body, name=name, grid=(t // tt,),
        in_specs=[pl.BlockSpec((tt, d), lambda i: (i, 0)), pl.BlockSpec((1, d), lambda i: (0, 0))],
        out_specs=pl.BlockSpec((tt, d), lambda i: (i, 0)),
        out_shape=jax.ShapeDtypeStruct((t, d), BF16),
        compiler_params=_cparams(("parallel",)),
    )(x, g)


def _rms_bwd(x, g, dy, dres, name):
    t, d = x.shape
    tt = _tile(t, 256, 8)

    def body(x_ref, g_ref, dy_ref, dres_ref, dx_ref, dg_ref):
        i = pl.program_id(0)
        xv = x_ref[...]
        r = lax.rsqrt(jnp.mean(xv * xv, axis=-1, keepdims=True) + EPS)
        xh = xv * r
        dyv = dy_ref[...].astype(F32)
        dxh = dyv * g_ref[...]
        c = jnp.mean(dxh * xh, axis=-1, keepdims=True)
        dx_ref[...] = dres_ref[...] + r * (dxh - xh * c)
        part = jnp.sum(dyv * xh, axis=0, keepdims=True)

        @pl.when(i == 0)
        def _():
            dg_ref[...] = part

        @pl.when(i > 0)
        def _():
            dg_ref[...] += part

    row = pl.BlockSpec((tt, d), lambda i: (i, 0))
    vec = pl.BlockSpec((1, d), lambda i: (0, 0))
    return pl.pallas_call(
        body, name=name, grid=(t // tt,),
        in_specs=[row, vec, row, row], out_specs=[row, vec],
        out_shape=[jax.ShapeDtypeStruct((t, d), F32), jax.ShapeDtypeStruct((1, d), F32)],
        compiler_params=_cparams(("arbitrary",)),
    )(x, g, dy, dres)


def _tail(h2, gl, pe, target, gfin):
    t, d = h2.shape
    tt = _tile(t, 256, 8)

    def body(h2_ref, gl_ref, pe_ref, tg_ref, gf_ref, dh3_ref, dgl_ref, dpe_ref, loss_ref, dgf_ref):
        i = pl.program_id(0)
        sig = _sigmoid(gl_ref[...])
        pev = pe_ref[...]
        h3 = h2_ref[...] + sig * pev
        r = lax.rsqrt(jnp.mean(h3 * h3, axis=-1, keepdims=True) + EPS)
        xh = h3 * r
        gf = gf_ref[...]
        e = xh * gf - tg_ref[...]
        dy = e * (1.0 / d)
        dxh = dy * gf
        c = jnp.mean(dxh * xh, axis=-1, keepdims=True)
        dh3 = r * (dxh - xh * c)
        dh3_ref[...] = dh3
        dgl_ref[...] = (dh3 * pev * sig * (1.0 - sig)).astype(BF16)
        dpe_ref[...] = (dh3 * sig).astype(BF16)
        lpart = jnp.sum(e * e, axis=0, keepdims=True) * (0.5 / d)
        gpart = jnp.sum(dy * xh, axis=0, keepdims=True)

        @pl.when(i == 0)
        def _():
            loss_ref[...] = lpart
            dgf_ref[...] = gpart

        @pl.when(i > 0)
        def _():
            loss_ref[...] += lpart
            dgf_ref[...] += gpart

        @pl.when(i == t // tt - 1)
        def _():
            loss_ref[...] = jnp.broadcast_to(jnp.sum(loss_ref[...], axis=-1, keepdims=True), (1, d))

    row = pl.BlockSpec((tt, d), lambda i: (i, 0))
    vec = pl.BlockSpec((1, d), lambda i: (0, 0))
    return pl.pallas_call(
        body, name="tail", grid=(t // tt,),
        in_specs=[row, row, row, row, vec], out_specs=[row, row, row, vec, vec],
        out_shape=[jax.ShapeDtypeStruct((t, d), F32), jax.ShapeDtypeStruct((t, d), BF16),
                   jax.ShapeDtypeStruct((t, d), BF16), jax.ShapeDtypeStruct((1, d), F32),
                   jax.ShapeDtypeStruct((1, d), F32)],
        compiler_params=_cparams(("arbitrary",)),
    )(h2, gl, pe, target, gfin)


def _conv(cur_ref, prev_ref, w_ref, b_ref, ext_ref, first, width):
    rows = cur_ref.shape[0]
    ext_ref[0:8, :] = jnp.where(first, 0.0, prev_ref[...])
    ext_ref[8:8 + rows, :] = cur_ref[...]
    acc = b_ref[...]
    for k in range(width):
        acc = acc + w_ref[k:k + 1, :] * ext_ref[pl.ds(9 - width + k, rows), :]
    return acc


def _ffn_specs(t, tt, tc, nf):
    hb = tt // 8
    cur_g = pl.BlockSpec((tt, tc), lambda j, i: (i, j))
    cur_u = pl.BlockSpec((tt, tc), lambda j, i: (i, j + nf))
    prev_g = pl.BlockSpec((8, tc), lambda j, i: (jnp.maximum(i * hb - 1, 0), j))
    prev_u = pl.BlockSpec((8, tc), lambda j, i: (jnp.maximum(i * hb - 1, 0), j + nf))
    w_g = pl.BlockSpec((FFN_CONV, tc), lambda j, i: (0, j))
    w_u = pl.BlockSpec((FFN_CONV, tc), lambda j, i: (0, j + nf))
    b_g = pl.BlockSpec((1, tc), lambda j, i: (0, j))
    b_u = pl.BlockSpec((1, tc), lambda j, i: (0, j + nf))
    return [cur_g, prev_g, cur_u, prev_u, w_g, w_u, b_g, b_u]


FFN_TC = 512


def _shift_down(prev, cur, n, rid):
    return jnp.where(rid < n, pltpu.roll(prev, n, 0), pltpu.roll(cur, n, 0))


def _shift_up(cur, nxt, n, rid):
    return jnp.where(rid < 8 - n, pltpu.roll(cur, 8 - n, 0), pltpu.roll(nxt, 8 - n, 0))


def _conv3_group(prev, cur, w_ref, b_ref, rid):
    x1 = _shift_down(prev, cur, 1, rid)
    x2 = _shift_down(prev, cur, 2, rid)
    return b_ref[...] + w_ref[2:3, :] * cur + w_ref[1:2, :] * x1 + w_ref[0:1, :] * x2


def _ffn_act_fwd(hid, cw, cb):
    t = hid.shape[0]
    tt = _tile(t, 512, 16)
    tc = _tile(D_FF, FFN_TC, 128)
    nf = D_FF // tc

    def body(g_ref, gp_ref, u_ref, up_ref, wg_ref, wu_ref, bg_ref, bu_ref, o_ref):
        first = pl.program_id(1) == 0
        rid = lax.broadcasted_iota(jnp.int32, (8, tc), 0)

        def act(pg, cg, pu, cu):
            gate = _conv3_group(pg, cg, wg_ref, bg_ref, rid)
            up = _conv3_group(pu, cu, wu_ref, bu_ref, rid)
            return gate * _sigmoid(gate) * up

        def step(s, carry):
            pg, pu = carry
            r0 = pl.multiple_of(s * 16, 16)
            g0, g1 = g_ref[pl.ds(r0, 8), :], g_ref[pl.ds(r0 + 8, 8), :]
            u0, u1 = u_ref[pl.ds(r0, 8), :], u_ref[pl.ds(r0 + 8, 8), :]
            out = jnp.concatenate([act(pg, g0, pu, u0), act(g0, g1, u0, u1)], axis=0)
            o_ref[pl.ds(r0, 16), :] = out.astype(BF16)
            return g1, u1

        init = (jnp.where(first, 0.0, gp_ref[...]), jnp.where(first, 0.0, up_ref[...]))
        lax.fori_loop(0, tt // 16, step, init)

    return pl.pallas_call(
        body, name="ffn_act_fwd", grid=(nf, t // tt), in_specs=_ffn_specs(t, tt, tc, nf),
        out_specs=pl.BlockSpec((tt, tc), lambda j, i: (i, j)),
        out_shape=jax.ShapeDtypeStruct((t, D_FF), BF16),
        compiler_params=_cparams(("parallel", "arbitrary")),
    )(hid, hid, hid, hid, cw, cw, cb, cb)


def _ffn_act_bwd(hid, cw, cb, dact):
    t = hid.shape[0]
    tt = _tile(t, 512, 16)
    tc = _tile(D_FF, FFN_TC, 128)
    nf = D_FF // tc
    nt = t // tt
    n16 = tt // 16
    hb = tt // 8

    def body(g_ref, gp_ref, u_ref, up_ref, wg_ref, wu_ref, bg_ref, bu_ref, gn_ref, un_ref, da_ref, dan_ref,
             og_ref, ou_ref, ag_ref, au_ref, accs):
        i = pl.program_id(1)
        first, last = i == 0, i == nt - 1
        rid = lax.broadcasted_iota(jnp.int32, (8, tc), 0)
        accs[...] = jnp.zeros_like(accs)

        def dgroup(pg, cg, pu, cu, da):
            gate = _conv3_group(pg, cg, wg_ref, bg_ref, rid)
            up = _conv3_group(pu, cu, wu_ref, bu_ref, rid)
            sv, sgr = _silu_and_grad(gate)
            return da * up * sgr, da * sv

        def finish(x, d0, d1, w_ref):
            s1 = _shift_up(d0, d1, 1, rid)
            s2 = _shift_up(d0, d1, 2, rid)
            dpre = w_ref[2:3, :] * d0 + w_ref[1:2, :] * s1 + w_ref[0:1, :] * s2
            return dpre, (x * s2, x * s1, x * d0, d0)

        def two_groups(it, carry, x2g, x2u, da_blk, zero_ahead):
            x0g, x0u, d0g, d0u, da1 = carry
            r0 = it * 16 if isinstance(it, int) else pl.multiple_of(it * 16, 16)
            x1g, x1u = g_ref[pl.ds(r0 + 8, 8), :], u_ref[pl.ds(r0 + 8, 8), :]
            d1g, d1u = dgroup(x0g, x1g, x0u, x1u, da1)
            d2g, d2u = dgroup(x1g, x2g, x1u, x2u, da_blk[0:8])
            d2g = jnp.where(zero_ahead, 0.0, d2g)
            d2u = jnp.where(zero_ahead, 0.0, d2u)
            outs = []
            for half, (xa, xb, da_, db_, dc_, w_ref, o_ref) in enumerate((
                    (x0g, x1g, d0g, d1g, d2g, wg_ref, og_ref), (x0u, x1u, d0u, d1u, d2u, wu_ref, ou_ref))):
                pa, prods_a = finish(xa, da_, db_, w_ref)
                pb, prods_b = finish(xb, db_, dc_, w_ref)
                o_ref[pl.ds(r0, 16), :] = jnp.concatenate([pa, pb], axis=0).astype(BF16)
                for k in range(4):
                    accs[4 * half + k] += prods_a[k] + prods_b[k]
            return x2g, x2u, d2g, d2u, da_blk[8:16]

        def step(it, carry):
            r1 = pl.multiple_of(it * 16 + 16, 16)
            return two_groups(it, carry, g_ref[pl.ds(r1, 8), :], u_ref[pl.ds(r1, 8), :],
                              da_ref[pl.ds(r1, 16), :].astype(F32), False)

        da0 = da_ref[pl.ds(0, 16), :].astype(F32)
        x0g, x0u = g_ref[pl.ds(0, 8), :], u_ref[pl.ds(0, 8), :]
        d0g, d0u = dgroup(jnp.where(first, 0.0, gp_ref[...]), x0g, jnp.where(first, 0.0, up_ref[...]), x0u, da0[0:8])
        carry = lax.fori_loop(0, n16 - 1, step, (x0g, x0u, d0g, d0u, da0[8:16]))
        two_groups(n16 - 1, carry, gn_ref[...], un_ref[...], dan_ref[...].astype(F32), last)

        @pl.when(first)
        def _():
            ag_ref[...] = jnp.zeros_like(ag_ref)
            au_ref[...] = jnp.zeros_like(au_ref)

        for half, a_ref in enumerate((ag_ref, au_ref)):
            for k in range(4):
                a_ref[k:k + 1, :] += jnp.sum(accs[4 * half + k], axis=0, keepdims=True)

    def nxt8(i):
        return jnp.minimum((i + 1) * hb, t // 8 - 1)

    def nxt16(i):
        return jnp.minimum((i + 1) * n16, t // 16 - 1)

    out_blk = pl.BlockSpec((tt, tc), lambda j, i: (i, j))
    acc_spec = pl.BlockSpec((8, tc), lambda j, i: (0, j))
    in_specs = _ffn_specs(t, tt, tc, nf) + [
        pl.BlockSpec((8, tc), lambda j, i: (nxt8(i), j)), pl.BlockSpec((8, tc), lambda j, i: (nxt8(i), j + nf)),
        out_blk, pl.BlockSpec((16, tc), lambda j, i: (nxt16(i), j))]
    return pl.pallas_call(
        body, name="ffn_act_bwd", grid=(nf, nt), in_specs=in_specs,
        out_specs=[out_blk, out_blk, acc_spec, acc_spec],
        out_shape=[jax.ShapeDtypeStruct((t, D_FF), BF16), jax.ShapeDtypeStruct((t, D_FF), BF16),
                   jax.ShapeDtypeStruct((8, D_FF), F32), jax.ShapeDtypeStruct((8, D_FF), F32)],
        scratch_shapes=[pltpu.VMEM((8, 8, tc), F32)],
        compiler_params=_cparams(("parallel", "arbitrary")),
    )(hid, hid, hid, hid, cw, cw, cb, cb, hid, hid, dact, dact)


def _tri_mask():
    r = lax.broadcasted_iota(jnp.int32, (CHUNK, CHUNK), 0)
    c = lax.broadcasted_iota(jnp.int32, (CHUNK, CHUNK), 1)
    return r >= c


def _gmlp_group_fwd(uv_ref, lng_ref, lnb_ref, ws_ref, bexp_ref, tri, g, want_grad):
    lo, hi = g * 128, (g + 1) * 128
    u_pre = uv_ref[:, lo:hi]
    v_pre = uv_ref[:, D_A + lo:D_A + hi]
    u, du = _gelu_and_grad(u_pre)
    v, dv = _gelu_and_grad(v_pre)
    mu = jnp.mean(v, axis=-1, keepdims=True)
    dc = v - mu
    rs = lax.rsqrt(jnp.mean(dc * dc, axis=-1, keepdims=True) + EPS)
    xh = dc * rs
    vn = (xh * lng_ref[:, lo:hi] + lnb_ref[:, lo:hi]).astype(BF16)
    w = jnp.where(tri, ws_ref[g], 0.0).astype(BF16)
    sg = _dot(w, vn) + bexp_ref[g]
    if want_grad:
        return u, du, dv, rs, xh, vn, w, sg
    return u * sg


def _gmlp_fwd(proj, ln_g, ln_b, w_s, b_exp, na_g):
    t = proj.shape[0]
    ng = N_GROUPS_A

    def body(uv_ref, lng_ref, lnb_ref, ws_ref, bexp_ref, nag_ref, o_ref):
        tri = _tri_mask()
        ys = [_gmlp_group_fwd(uv_ref, lng_ref, lnb_ref, ws_ref, bexp_ref, tri, g, False) for g in range(ng)]
        ssq = ys[0] * 0.0
        for y in ys:
            ssq = ssq + y * y
        r = lax.rsqrt(jnp.sum(ssq, axis=-1, keepdims=True) * (1.0 / D_A) + EPS)
        for g, y in enumerate(ys):
            o_ref[:, g * 128:(g + 1) * 128] = (y * r * nag_ref[:, g * 128:(g + 1) * 128]).astype(BF16)

    vec = pl.BlockSpec((1, D_A), lambda i: (0, 0))
    cube = pl.BlockSpec((ng, CHUNK, CHUNK), lambda i: (0, 0, 0))
    return pl.pallas_call(
        body, name="gmlp_fwd", grid=(t // CHUNK,),
        in_specs=[pl.BlockSpec((CHUNK, 2 * D_A), lambda i: (i, 0)), vec, vec, cube, cube, vec],
        out_specs=pl.BlockSpec((CHUNK, D_A), lambda i: (i, 0)),
        out_shape=jax.ShapeDtypeStruct((t, D_MIX), BF16),
        compiler_params=_cparams(("parallel",)),
    )(proj, ln_g, ln_b, w_s, b_exp, na_g)


def _gmlp_bwd(proj, dyab, ln_g, ln_b, w_s, b_exp, na_g):
    t = proj.shape[0]
    ng = N_GROUPS_A
    nsteps = t // CHUNK

    def body(uv_ref, dy_ref, lng_ref, lnb_ref, ws_ref, bexp_ref, nag_ref,
             duv_ref, dws_ref, dbs_ref, dlng_ref, dlnb_ref, dnag_ref, dbacc):
        i = pl.program_id(0)
        tri = _tri_mask()

        @pl.when(i == 0)
        def _():
            dws_ref[...] = jnp.zeros_like(dws_ref)
            dbacc[...] = jnp.zeros_like(dbacc)
            dlng_ref[...] = jnp.zeros_like(dlng_ref)
            dlnb_ref[...] = jnp.zeros_like(dlnb_ref)
            dnag_ref[...] = jnp.zeros_like(dnag_ref)

        st = [_gmlp_group_fwd(uv_ref, lng_ref, lnb_ref, ws_ref, bexp_ref, tri, g, True) for g in range(ng)]
        ssq = st[0][0] * 0.0
        for s in st:
            y = s[0] * s[7]
            ssq = ssq + y * y
        r = lax.rsqrt(jnp.sum(ssq, axis=-1, keepdims=True) * (1.0 / D_A) + EPS)
        csum = st[0][0] * 0.0
        for g, s in enumerate(st):
            sl = slice(g * 128, (g + 1) * 128)
            xhy = s[0] * s[7] * r
            dya = dy_ref[:, sl]
            dnag_ref[:, sl] += jnp.sum(dya * xhy, axis=0, keepdims=True)
            csum = csum + dya * nag_ref[:, sl] * xhy
        c1 = jnp.sum(csum, axis=-1, keepdims=True) * (1.0 / D_A)
        for g, s in enumerate(st):
            u, du, dv, rs, xh, vn, w, sg = s
            sl = slice(g * 128, (g + 1) * 128)
            dy = r * (dy_ref[:, sl] * nag_ref[:, sl] - u * sg * r * c1)
            dsg = dy * u
            dsg_b = dsg.astype(BF16)
            dws_ref[g] += _dot_nt(dsg_b, vn)
            dbacc[g] += dsg
            dvn = _dot_tn(w, dsg_b)
            dlnb_ref[:, sl] += jnp.sum(dvn, axis=0, keepdims=True)
            dlng_ref[:, sl] += jnp.sum(dvn * xh, axis=0, keepdims=True)
            dxh = dvn * lng_ref[:, sl]
            dvv = rs * (dxh - jnp.mean(dxh, axis=-1, keepdims=True)
                        - xh * jnp.mean(dxh * xh, axis=-1, keepdims=True))
            duv_ref[:, sl] = (dy * sg * du).astype(BF16)
            duv_ref[:, D_A + g * 128:D_A + (g + 1) * 128] = (dvv * dv).astype(BF16)

        @pl.when(i == nsteps - 1)
        def _():
            for g in range(ng):
                dws_ref[g] = jnp.where(tri, dws_ref[g], 0.0)
                dbs_ref[g] = jnp.sum(dbacc[g], axis=-1, keepdims=True)

    vec = pl.BlockSpec((1, D_A), lambda i: (0, 0))
    cube = pl.BlockSpec((ng, CHUNK, CHUNK), lambda i: (0, 0, 0))
    return pl.pallas_call(
        body, name="gmlp_bwd", grid=(nsteps,),
        in_specs=[pl.BlockSpec((CHUNK, 2 * D_A), lambda i: (i, 0)),
                  pl.BlockSpec((CHUNK, D_A), lambda i: (i, 0)), vec, vec, cube, cube, vec],
        out_specs=[pl.BlockSpec((CHUNK, 2 * D_A), lambda i: (i, 0)), cube,
                   pl.BlockSpec((ng, CHUNK, 1), lambda i: (0, 0, 0)), vec, vec, vec],
        out_shape=[jax.ShapeDtypeStruct((t, 2 * D_A), BF16), jax.ShapeDtypeStruct((ng, CHUNK, CHUNK), F32),
                   jax.ShapeDtypeStruct((ng, CHUNK, 1), F32), jax.ShapeDtypeStruct((1, D_A), F32),
                   jax.ShapeDtypeStruct((1, D_A), F32), jax.ShapeDtypeStruct((1, D_A), F32)],
        scratch_shapes=[pltpu.VMEM((ng, CHUNK, CHUNK), F32)],
        compiler_params=_cparams(("arbitrary",)),
    )(proj, dyab, ln_g, ln_b, w_s, b_exp, na_g)


OFF_Z = 2 * D_A
OFF_XS = OFF_Z + D_SSM
OFF_B = OFF_XS + D_SSM
OFF_C = OFF_B + D_BC


def _ssd_consts():
    tri = jnp.tril(jnp.ones((CHUNK, CHUNK), F32)).astype(BF16)
    h = jnp.arange(HPAD)[None, :, None]
    g = jnp.arange(N_SSM_GROUPS)[:, None, None]
    j1 = jnp.arange(GW)[None, None, :]
    eh = (h == g * HEADS_PER_GROUP + j1 // HEAD_DIM).astype(BF16)
    j2 = jnp.arange(HEADS_PER_GROUP * 128)[None, None, :]
    e128 = (h == g * HEADS_PER_GROUP + j2 // 128).astype(BF16)
    return tri, eh, e128


def _ssd_in_specs(cmap):
    def rows(i):
        return cmap(i)

    def prev8(i):
        return jnp.maximum(cmap(i) * (CHUNK // 8) - 1, 0)

    gw, ns = GW, D_STATE
    specs = [
        pl.BlockSpec((CHUNK, gw), lambda i, g: (rows(i), OFF_Z // gw + g)),
        pl.BlockSpec((CHUNK, gw), lambda i, g: (rows(i), OFF_XS // gw + g)),
        pl.BlockSpec((8, gw), lambda i, g: (prev8(i), OFF_XS // gw + g)),
        pl.BlockSpec((CHUNK, ns), lambda i, g: (rows(i), OFF_B // ns + g)),
        pl.BlockSpec((8, ns), lambda i, g: (prev8(i), OFF_B // ns + g)),
        pl.BlockSpec((CHUNK, ns), lambda i, g: (rows(i), OFF_C // ns + g)),
        pl.BlockSpec((8, ns), lambda i, g: (prev8(i), OFF_C // ns + g)),
        pl.BlockSpec((CHUNK, HPAD), lambda i, g: (rows(i), 0)),
        pl.BlockSpec((SSM_CONV, gw), lambda i, g: (0, g)),
        pl.BlockSpec((SSM_CONV, ns), lambda i, g: (0, D_SSM // ns + g)),
        pl.BlockSpec((SSM_CONV, ns), lambda i, g: (0, (D_SSM + D_BC) // ns + g)),
        pl.BlockSpec((1, gw), lambda i, g: (0, g)),
        pl.BlockSpec((1, ns), lambda i, g: (0, D_SSM // ns + g)),
        pl.BlockSpec((1, ns), lambda i, g: (0, (D_SSM + D_BC) // ns + g)),
        pl.BlockSpec((1, HPAD), lambda i, g: (0, 0)),
        pl.BlockSpec((1, HPAD), lambda i, g: (0, 0)),
        pl.BlockSpec((1, gw), lambda i, g: (0, g)),
        pl.BlockSpec((1, gw), lambda i, g: (0, g)),
        pl.BlockSpec((CHUNK, CHUNK), lambda i, g: (0, 0)),
        pl.BlockSpec((None, HPAD, gw), lambda i, g: (g, 0, 0)),
        pl.BlockSpec((None, HPAD, HEADS_PER_GROUP * 128), lambda i, g: (g, 0, 0)),
    ]
    return specs


def _ssd_scratch():
    return [pltpu.VMEM((CHUNK + 8, GW), F32), pltpu.VMEM((CHUNK + 8, D_STATE), F32),
            pltpu.VMEM((CHUNK + 8, D_STATE), F32), pltpu.VMEM((HPAD, CHUNK), F32),
            pltpu.VMEM((CHUNK, GW), F32)]


def _ssd_pre(first, g, refs, scr):
    (z_ref, xs_ref, xsp_ref, b_ref, bp_ref, c_ref, cp_ref, dt_ref, cwx, cwb, cwc, cbx, cbb, cbc,
     dtb_ref, alog_ref, de_ref, gain_ref, tri_ref, eh_ref, e128_ref) = refs
    ext_x, ext_b, ext_c, acst_sc, acse_sc = scr
    p = {}
    px = _conv(xs_ref, xsp_ref, cwx, cbx, ext_x, first, SSM_CONV)
    pb = _conv(b_ref, bp_ref, cwb, cbb, ext_b, first, SSM_CONV)
    pc = _conv(c_ref, cp_ref, cwc, cbc, ext_c, first, SSM_CONV)
    p['xs'], p['dsx'] = _silu_and_grad(px)
    p['bm'], p['dsb'] = _silu_and_grad(pb)
    p['cm'], p['dsc'] = _silu_and_grad(pc)
    dt_in = dt_ref[...] + dtb_ref[...]
    p['dt_in'] = dt_in
    dt = _softplus(dt_in)
    p['dt'] = dt
    a = -jnp.exp(alog_ref[...])
    p['a'] = a
    tri_b = tri_ref[...]
    acs = _e01x(tri_b, dt * a)
    acst_sc[...] = acs.T
    eh = eh_ref[...]
    p['eh'] = eh
    p['dt_e'] = _x01(dt, eh)
    acs_e = _x01(acs, eh)
    acse_sc[...] = acs_e
    p['acs_e'] = acs_e
    p['acs_c'] = _x01(acs, e128_ref[...])
    p['acs_last_e'] = acse_sc[pl.ds(CHUNK - 1, 1), :]
    p['xdt'] = p['xs'] * p['dt_e']
    p['decay_e'] = jnp.exp(p['acs_last_e'] - acs_e)
    p['cm_b'] = p['cm'].astype(BF16)
    p['bm_b'] = p['bm'].astype(BF16)
    p['scores'] = _dot_nt(p['cm_b'], p['bm_b'])
    p['tri_b'] = tri_b
    return p


def _ssd_l(p, g, r, acst_sc, tri):
    col = p['acs_c'][:, r * 128:(r + 1) * 128]
    row = acst_sc[pl.ds(g * HEADS_PER_GROUP + r, 1), :]
    return jnp.exp(jnp.where(tri, col - row, -1e30))


def _ssd_fwd(proj, dt_raw, yab, cw, cb, dtb, alog, de, gain, consts):
    t = proj.shape[0]
    nc = t // CHUNK
    ng = N_SSM_GROUPS
    tri_c, eh_c, e128_c = consts
    n_in = 21

    def body(*refs):
        ins = refs[:n_in]
        yb_ref, ys_ref, hs_ref = refs[n_in + 1:n_in + 4]
        h_sc = refs[n_in + 4]
        scr = refs[n_in + 5:]
        c, g = pl.program_id(0), pl.program_id(1)
        z_ref, de_ref, gain_ref = ins[0], ins[16], ins[17]
        slab = pl.ds(pl.multiple_of(g * D_STATE, D_STATE), D_STATE)

        @pl.when(c == 0)
        def _():
            h_sc[slab, :] = jnp.zeros((D_STATE, GW), F32)

        p = _ssd_pre(c == 0, g, ins, scr)
        tri = _tri_mask()
        h_in = h_sc[slab, :]
        hs_ref[...] = h_in
        yoff = _dot(p['cm_b'], h_in.astype(BF16)) * jnp.exp(p['acs_e'])
        states = _dot_tn(p['bm_b'], (p['xdt'] * p['decay_e']).astype(BF16))
        lane = lax.broadcasted_iota(jnp.int32, (CHUNK, 128), 1)
        slabs = []
        for r2 in range(HEADS_PER_GROUP // 2):
            xb = p['xdt'][:, r2 * 128:(r2 + 1) * 128].astype(BF16)
            ya = _dot((p['scores'] * _ssd_l(p, g, 2 * r2, scr[3], tri)).astype(BF16), xb)
            yb = _dot((p['scores'] * _ssd_l(p, g, 2 * r2 + 1, scr[3], tri)).astype(BF16), xb)
            slabs.append(jnp.where(lane < HEAD_DIM, ya, yb))
        y = jnp.concatenate(slabs, axis=1) + yoff + de_ref[...] * p['xs']
        ys_ref[...] = y
        h_sc[slab, :] = jnp.exp(p['acs_last_e']) * h_in + states
        zv = z_ref[...]
        yg = y * zv * _sigmoid(zv)
        r = lax.rsqrt(jnp.mean(yg * yg, axis=-1, keepdims=True) + EPS)
        yb_ref[...] = (yg * r * gain_ref[...]).astype(BF16)

    in_specs = _ssd_in_specs(lambda i: i) + [pl.BlockSpec(memory_space=pl.ANY)]
    out_specs = [pl.BlockSpec((CHUNK, GW), lambda i, g: (i, D_A // GW + g)),
                 pl.BlockSpec((CHUNK, GW), lambda i, g: (i, g)),
                 pl.BlockSpec((None, D_STATE, GW), lambda i, g: (i, g, 0))]
    out_shape = [jax.ShapeDtypeStruct((t, D_MIX), BF16), jax.ShapeDtypeStruct((t, D_SSM), F32),
                 jax.ShapeDtypeStruct((nc, ng * D_STATE, GW), F32)]
    return pl.pallas_call(
        body, name="ssd_fwd", grid=(nc, ng), in_specs=in_specs, out_specs=out_specs, out_shape=out_shape,
        scratch_shapes=[pltpu.VMEM((ng * D_STATE, GW), F32)] + _ssd_scratch(),
        input_output_aliases={n_in: 0},
        compiler_params=_cparams(("arbitrary", "arbitrary")),
    )(proj, proj, proj, proj, proj, proj, proj, dt_raw, cw, cw, cw, cb, cb, cb, dtb, alog, de, gain,
      tri_c, eh_c, e128_c, yab)


def _rows8(vals, width):
    rid = lax.broadcasted_iota(jnp.int32, (8, width), 0)
    out = jnp.zeros((8, width), F32)
    for k, v in enumerate(vals):
        if v is not None:
            out = out + jnp.where(rid == k, v, 0.0)
    return out


def _ssd_bwd(proj, dt_raw, dyab, ysave, hs, cw, cb, dtb, alog, de, gain, consts):
    t = proj.shape[0]
    nc = t // CHUNK
    ng = N_SSM_GROUPS
    tri_c, eh_c, e128_c = consts
    n_in = 21

    def body(*refs):
        ins = refs[:n_in]
        dy_ref, ys_ref, hs_ref = refs[n_in:n_in + 3]
        (dz_ref, dxs_ref, db_ref, dc_ref, ddt_ref, acc_x, acc_b, acc_c, acc_gain,
         acc_head) = refs[n_in + 3:n_in + 13]
        dh_sc, car_x, car_b, car_c, dext_x, dext_b, dext_c = refs[n_in + 13:n_in + 20]
        scr = refs[n_in + 20:]
        ext_x, ext_b, ext_c, acst_sc, _ = scr
        cc, g = pl.program_id(0), pl.program_id(1)
        c = nc - 1 - cc
        z_ref, cwx, cwb, cwc, de_ref, gain_ref = ins[0], ins[8], ins[9], ins[10], ins[16], ins[17]
        slab = pl.ds(pl.multiple_of(g * D_STATE, D_STATE), D_STATE)

        @pl.when(cc == 0)
        def _():
            dh_sc[slab, :] = jnp.zeros((D_STATE, GW), F32)
            car_x[g] = jnp.zeros((8, GW), F32)
            car_b[g] = jnp.zeros((8, D_STATE), F32)
            car_c[g] = jnp.zeros((8, D_STATE), F32)

        @pl.when(jnp.logical_and(cc == 0, g == 0))
        def _():
            for a in (acc_x, acc_b, acc_c, acc_gain, acc_head):
                a[...] = jnp.zeros_like(a)

        p = _ssd_pre(c == 0, g, ins, scr)
        tri = _tri_mask()
        xs, dt_e, acs_e, xdt, decay_e = p['xs'], p['dt_e'], p['acs_e'], p['xdt'], p['decay_e']
        cm_b, bm_b, scores, eh = p['cm_b'], p['bm_b'], p['scores'], p['eh']
        h_in = hs_ref[...]
        h_in_b = h_in.astype(BF16)
        e_a = jnp.exp(acs_e)
        raw = _dot(cm_b, h_in_b)

        y = ys_ref[...]
        zv = z_ref[...]
        sz, dsz = _silu_and_grad(zv)
        yg = y * sz
        r = lax.rsqrt(jnp.mean(yg * yg, axis=-1, keepdims=True) + EPS)
        xh = yg * r
        dout = dy_ref[...]
        gain = gain_ref[...]
        dxh = dout * gain
        dyg = r * (dxh - xh * jnp.mean(dxh * xh, axis=-1, keepdims=True))
        dy = dyg * sz
        dz_ref[...] = (dyg * y * dsz).astype(BF16)
        acc_gain[g] += _rows8([jnp.sum(dout * xh, axis=0, keepdims=True)], GW)
        d_skip8 = _x01_nt(_rows8([None, None, jnp.sum(dy * xs, axis=0, keepdims=True)], GW), eh)
        dxs = de_ref[...] * dy

        q = dy * raw * e_a
        draw = (dy * e_a).astype(BF16)
        d_c = _dot_nt(draw, h_in_b)
        dh_in = _dot_tn(cm_b, draw)

        lane = lax.broadcasted_iota(jnp.int32, (CHUNK, 128), 1)
        ones_b = jnp.ones((CHUNK, 128), BF16)
        dscores = jnp.zeros((CHUNK, CHUNK), F32)
        dxdt_slabs, q_slabs = [], []
        for r2 in range(HEADS_PER_GROUP // 2):
            sl = slice(r2 * 128, (r2 + 1) * 128)
            xb = xdt[:, sl].astype(BF16)
            dys = dy[:, sl]
            dys_b = dys.astype(BF16)
            dxh_pair, qv_pair = [], []
            for half in range(2):
                lmat = _ssd_l(p, g, 2 * r2 + half, acst_sc, tri)
                m = scores * lmat
                mine = (lane < HEAD_DIM) if half == 0 else (lane >= HEAD_DIM)
                dm = _dot_nt(jnp.where(mine, dys, 0.0).astype(BF16), xb)
                dscores = dscores + dm * lmat
                gm = dm * m
                dxh_pair.append(_dot_tn(m.astype(BF16), dys_b))
                h3 = _split3(gm)
                colsum = _dot_tn(h3[0], ones_b) + _dot_tn(h3[1], ones_b) + _dot_tn(h3[2], ones_b)
                qv_pair.append(jnp.sum(gm, axis=-1, keepdims=True) - colsum)
            dxdt_slabs.append(jnp.where(lane < HEAD_DIM, dxh_pair[0], dxh_pair[1]))
            q_slabs.append(jnp.where(lane == 0, qv_pair[0], 0.0) + jnp.where(lane == HEAD_DIM, qv_pair[1], 0.0))
        dxdt = jnp.concatenate(dxdt_slabs, axis=1)
        q = q + jnp.concatenate(q_slabs, axis=1)

        dh_out = dh_sc[slab, :]
        dh_out_b = dh_out.astype(BF16)
        e_l = jnp.exp(p['acs_last_e'])
        dh_sc[slab, :] = dh_in + e_l * dh_out
        dlast = jnp.sum(dh_out * h_in, axis=0, keepdims=True) * e_l
        dxd = _dot(bm_b, dh_out_b)
        xd = xdt * decay_e
        dxdt = dxdt + dxd * decay_e
        dd = dxd * xd
        q = q - dd
        dlast = dlast + jnp.sum(dd, axis=0, keepdims=True)
        d_b = _dot_nt(xd.astype(BF16), dh_out_b)
        dsc_b = dscores.astype(BF16)
        d_c = d_c + _dot(dsc_b, bm_b)
        d_b = d_b + _dot_tn(dsc_b, cm_b)

        dxs = dxs + dxdt * dt_e
        rid = lax.broadcasted_iota(jnp.int32, (CHUNK, GW), 0)
        q = q + jnp.where(rid == CHUNK - 1, dlast, 0.0)
        dacs = _x01_nt(q, eh)
        ddt = _x01_nt(dxdt * xs, eh)
        dadt = _e01x_tn(p['tri_b'], dacs)
        ddt = ddt + dadt * p['a']
        d_a = jnp.sum(dadt * p['dt'], axis=0, keepdims=True)
        ddt_raw = ddt * _sigmoid(p['dt_in'])
        acc_head[...] += _rows8([jnp.sum(ddt_raw, axis=0, keepdims=True), d_a * p['a']], HPAD) + d_skip8

        @pl.when(g == 0)
        def _():
            ddt_ref[...] = ddt_raw

        @pl.when(g > 0)
        def _():
            ddt_ref[...] += ddt_raw

        for dv, dsil, ext, dext, car, acc, w_ref, o_ref in (
                (dxs, p['dsx'], ext_x, dext_x, car_x, acc_x, cwx, dxs_ref),
                (d_b, p['dsb'], ext_b, dext_b, car_b, acc_b, cwb, db_ref),
                (d_c, p['dsc'], ext_c, dext_c, car_c, acc_c, cwc, dc_ref)):
            dp = dv * dsil
            width = dp.shape[1]
            rows = [jnp.sum(ext[pl.ds(5 + k, CHUNK), :] * dp, axis=0, keepdims=True) for k in range(SSM_CONV)]
            rows.append(jnp.sum(dp, axis=0, keepdims=True))
            acc[g] += _rows8(rows, width)
            dext[0:CHUNK, :] = dp
            dext[CHUNK:CHUNK + 8, :] = car[g]
            car[g] = dext[0:8, :]
            dx = w_ref[SSM_CONV - 1:SSM_CONV, :] * dext[pl.ds(0, CHUNK), :]
            for k in range(SSM_CONV - 1):
                dx = dx + w_ref[k:k + 1, :] * dext[pl.ds(SSM_CONV - 1 - k, CHUNK), :]
            o_ref[...] = dx.astype(BF16)

    def cmap(i):
        return nc - 1 - i

    in_specs = _ssd_in_specs(cmap) + [
        pl.BlockSpec((CHUNK, GW), lambda i, g: (cmap(i), D_A // GW + g)),
        pl.BlockSpec((CHUNK, GW), lambda i, g: (cmap(i), g)),
        pl.BlockSpec((None, D_STATE, GW), lambda i, g: (cmap(i), g, 0)),
    ]

    def full(shape):
        return pl.BlockSpec(shape, lambda i, g: (0,) * len(shape))

    out_specs = [
        pl.BlockSpec((CHUNK, GW), lambda i, g: (cmap(i), g)),
        pl.BlockSpec((CHUNK, GW), lambda i, g: (cmap(i), g)),
        pl.BlockSpec((CHUNK, D_STATE), lambda i, g: (cmap(i), g)),
        pl.BlockSpec((CHUNK, D_STATE), lambda i, g: (cmap(i), g)),
        pl.BlockSpec((CHUNK, HPAD), lambda i, g: (cmap(i), 0)),
        full((ng, 8, GW)), full((ng, 8, D_STATE)), full((ng, 8, D_STATE)),
        full((ng, 8, GW)), full((8, HPAD)),
    ]
    out_shape = [
        jax.ShapeDtypeStruct((t, D_SSM), BF16), jax.ShapeDtypeStruct((t, D_SSM), BF16),
        jax.ShapeDtypeStruct((t, D_BC), BF16), jax.ShapeDtypeStruct((t, D_BC), BF16),
        jax.ShapeDtypeStruct((t, HPAD), F32),
        jax.ShapeDtypeStruct((ng, 8, GW), F32), jax.ShapeDtypeStruct((ng, 8, D_STATE), F32),
        jax.ShapeDtypeStruct((ng, 8, D_STATE), F32), jax.ShapeDtypeStruct((ng, 8, GW), F32),
        jax.ShapeDtypeStruct((8, HPAD), F32),
    ]
    scratch = [pltpu.VMEM((ng * D_STATE, GW), F32),
               pltpu.VMEM((ng, 8, GW), F32), pltpu.VMEM((ng, 8, D_STATE), F32), pltpu.VMEM((ng, 8, D_STATE), F32),
               pltpu.VMEM((CHUNK + 8, GW), F32), pltpu.VMEM((CHUNK + 8, D_STATE), F32),
               pltpu.VMEM((CHUNK + 8, D_STATE), F32)] + _ssd_scratch()
    return pl.pallas_call(
        body, name="ssd_bwd", grid=(nc, ng), in_specs=in_specs, out_specs=out_specs, out_shape=out_shape,
        scratch_shapes=scratch, compiler_params=_cparams(("arbitrary", "arbitrary")),
    )(proj, proj, proj, proj, proj, proj, proj, dt_raw, cw, cw, cw, cb, cb, cb, dtb, alog, de, gain,
      tri_c, eh_c, e128_c, dyab, ysave, hs)


ANY = pl.BlockSpec(memory_space=pl.ANY)


def _place():
    x, y, c = lax.axis_index("x"), lax.axis_index("y"), lax.axis_index("c")
    chips = [(1 - x, y), (x, 1 - y), (1 - x, 1 - y)]
    return x, y, c, chips


def _rcopy(src, dst, send_sems, recv_sems, k, dev):
    return pltpu.make_async_remote_copy(src_ref=src, dst_ref=dst, send_sem=send_sems.at[k],
                                        recv_sem=recv_sems.at[k], device_id=dev, device_id_type=MESH)


def _my_chip():
    return 2 * lax.axis_index("x") + lax.axis_index("y")


def _cast_into_slot(w, name):
    r, c = w.shape
    tr = _row_tile(r, c, 2)

    def body(w_ref, o_ref):
        o_ref[...] = w_ref[...].astype(BF16)

    return pl.pallas_call(
        body, name=name, grid=(r // tr,), in_specs=[pl.BlockSpec((tr, c), lambda i: (i, 0))],
        out_specs=pl.BlockSpec((None, tr, c), lambda i: (_my_chip(), i, 0)),
        out_shape=jax.ShapeDtypeStruct((N_CHIPS, r, c), BF16), compiler_params=_cparams(("parallel",)),
    )(w)


def _gather_weights(bigs, smalls):
    nb, ns = len(bigs), len(smalls)
    na = nb + ns
    nsem = 6 * nb + 3 * ns

    def body(*refs):
        ins, outs = refs[:na], refs[na:2 * na]
        send_sems, recv_sems, loc_sems = refs[2 * na:]
        x, y, c, chips = _place()
        q = 2 * x + y
        sib = (x, y, 1 - c)
        locs, sends = [], []
        for s in range(ns):
            cp = pltpu.make_async_copy(ins[nb + s], outs[nb + s].at[q], loc_sems.at[s])
            cp.start()
            locs.append(cp)
        for a in range(nb):
            half = bigs[a].shape[1] // 2
            mine = pl.ds(c * half, half)
            for k, chip in enumerate(chips):
                cp = _rcopy(outs[a].at[q, mine], outs[a].at[q, mine], send_sems, recv_sems, 6 * a + k,
                            (chip[0], chip[1], c))
                cp.start()
                sends.append(cp)
        for s in range(ns):
            a = nb + s
            for k, chip in enumerate(chips):
                cp = _rcopy(ins[a], outs[a].at[q], send_sems, recv_sems, 6 * nb + 3 * s + k,
                            (chip[0], chip[1], c))
                cp.start()
                sends.append(cp)
        for a in range(nb):
            half = bigs[a].shape[1] // 2
            mine = pl.ds(c * half, half)
            for k, chip in enumerate(chips):
                qk = 2 * chip[0] + chip[1]
                _rcopy(outs[a].at[qk, mine], outs[a].at[qk, mine], send_sems, recv_sems, 6 * a + k, sib).wait_recv()
                cp = _rcopy(outs[a].at[qk, mine], outs[a].at[qk, mine], send_sems, recv_sems, 6 * a + 3 + k, sib)
                cp.start()
                sends.append(cp)
        for a in range(nb):
            half = bigs[a].shape[1] // 2
            other = pl.ds((1 - c) * half, half)
            for k, chip in enumerate(chips):
                qk = 2 * chip[0] + chip[1]
                _rcopy(outs[a].at[qk, other], outs[a].at[qk, other], send_sems, recv_sems, 6 * a + 3 + k,
                       sib).wait_recv()
        for s in range(ns):
            a = nb + s
            for k, chip in enumerate(chips):
                qk = 2 * chip[0] + chip[1]
                _rcopy(ins[a], outs[a].at[qk], send_sems, recv_sems, 6 * nb + 3 * s + k, sib).wait_recv()
        for cp in sends:
            cp.wait_send()
        for cp in locs:
            cp.wait()

    arrs = list(bigs) + list(smalls)
    out_shape = ([jax.ShapeDtypeStruct(a.shape, a.dtype) for a in bigs]
                 + [jax.ShapeDtypeStruct((N_CHIPS,) + a.shape, a.dtype) for a in smalls])
    return pl.pallas_call(
        body, name="gather_weights", in_specs=[ANY] * na, out_specs=[ANY] * na, out_shape=out_shape,
        input_output_aliases={a: a for a in range(nb)},
        scratch_shapes=[pltpu.SemaphoreType.DMA((nsem,)), pltpu.SemaphoreType.DMA((nsem,)),
                        pltpu.SemaphoreType.DMA((max(ns, 1),))],
    )(*arrs)


def _rs_sibling(gs):
    na = len(gs)

    def body(*refs):
        ins, recv_o = refs[:na], refs[na:2 * na]
        send_sems, recv_sems = refs[2 * na:]
        x, y, c, _ = _place()
        cps = []
        for a in range(na):
            half = gs[a].shape[1] // 2
            snd = _rcopy(ins[a].at[:, pl.ds((1 - c) * half, half), :], recv_o[a], send_sems, recv_sems, a,
                         (x, y, 1 - c))
            snd.start()
            cps.append(snd)
        for cp in cps:
            cp.wait()

    halves = [jax.ShapeDtypeStruct((N_CHIPS, g.shape[1] // 2, g.shape[2]), g.dtype) for g in gs]
    return pl.pallas_call(
        body, name="rs_sibling", in_specs=[ANY] * na, out_specs=[ANY] * na, out_shape=halves,
        scratch_shapes=[pltpu.SemaphoreType.DMA((na,)), pltpu.SemaphoreType.DMA((na,))],
    )(*gs)


def _rs_chips(ps):
    na = len(ps)

    def body(*refs):
        ins, recv_o = refs[:na], refs[na:2 * na]
        send_sems, recv_sems = refs[2 * na:]
        x, y, c, chips = _place()
        cps = []
        for a in range(na):
            for k, chip in enumerate(chips):
                qk = 2 * chip[0] + chip[1]
                snd = _rcopy(ins[a].at[qk], recv_o[a].at[k], send_sems, recv_sems, 3 * a + k,
                             (chip[0], chip[1], c))
                snd.start()
                cps.append(snd)
        for cp in cps:
            cp.wait()

    rcv = [jax.ShapeDtypeStruct((3,) + p.shape[1:], p.dtype) for p in ps]
    return pl.pallas_call(
        body, name="rs_chips", in_specs=[ANY] * na, out_specs=[ANY] * na, out_shape=rcv,
        scratch_shapes=[pltpu.SemaphoreType.DMA((3 * na,)), pltpu.SemaphoreType.DMA((3 * na,))],
    )(*ps)


def _share_sibling(fs):
    na = len(fs)

    def body(*refs):
        outs = refs[na:2 * na]
        send_sems, recv_sems = refs[2 * na:]
        x, y, c, _ = _place()
        cps = []
        for a in range(na):
            half = fs[a].shape[0] // 2
            rows = pl.ds(c * half, half)
            snd = _rcopy(outs[a].at[rows], outs[a].at[rows], send_sems, recv_sems, a, (x, y, 1 - c))
            snd.start()
            cps.append(snd)
        for a, cp in enumerate(cps):
            half = fs[a].shape[0] // 2
            other = pl.ds((1 - c) * half, half)
            cp.wait_send()
            _rcopy(outs[a].at[other], outs[a].at[other], send_sems, recv_sems, a, (x, y, 1 - c)).wait_recv()

    return pl.pallas_call(
        body, name="share_sibling", in_specs=[ANY] * na, out_specs=[ANY] * na,
        out_shape=[jax.ShapeDtypeStruct(f.shape, f.dtype) for f in fs],
        input_output_aliases={a: a for a in range(na)},
        scratch_shapes=[pltpu.SemaphoreType.DMA((na,)), pltpu.SemaphoreType.DMA((na,))],
    )(*fs)


def _allgather_small(buf):
    def body(in_ref, out_ref, send_sems, recv_sems, loc_sem):
        x, y, c, _ = _place()
        me = 4 * x + 2 * y + c
        loc = pltpu.make_async_copy(in_ref, out_ref.at[me], loc_sem)
        loc.start()
        cps = [loc]
        for k in range(1, 8):
            dev = (1 - x if k & 4 else x, 1 - y if k & 2 else y, 1 - c if k & 1 else c)
            snd = _rcopy(in_ref, out_ref.at[me], send_sems, recv_sems, k - 1, dev)
            snd.start()
            cps.append(snd)
        for cp in cps:
            cp.wait()

    return pl.pallas_call(
        body, name="allgather_small", in_specs=[ANY], out_specs=ANY,
        out_shape=jax.ShapeDtypeStruct((8,) + buf.shape, buf.dtype),
        scratch_shapes=[pltpu.SemaphoreType.DMA((7,)), pltpu.SemaphoreType.DMA((7,)), pltpu.SemaphoreType.DMA],
    )(buf)


def _row_tile(rows, cols, nbuf):
    budget = 24 * 1024 * 1024 // (nbuf * cols * 4 * 2)
    return _tile(rows, max(16, budget - budget % 16), 16) if rows % 16 == 0 else rows


def _add_pairs(g, rcv, name):
    s, half, c = rcv.shape
    tr = _row_tile(half, c, 3)
    nh = half // tr

    def body(a_ref, b_ref, o_ref):
        o_ref[...] = (a_ref[...].astype(F32) + b_ref[...].astype(F32)).astype(o_ref.dtype)

    blk = pl.BlockSpec((None, tr, c), lambda j, i: (j, i, 0))
    mine = pl.BlockSpec((None, tr, c), lambda j, i: (j, lax.axis_index("c") * nh + i, 0))
    return pl.pallas_call(
        body, name=name, grid=(s, nh), in_specs=[mine, blk], out_specs=blk,
        out_shape=jax.ShapeDtypeStruct(rcv.shape, rcv.dtype), compiler_params=_cparams(("parallel", "parallel")),
    )(g, rcv)


def _sum_chips(part, rcv, name):
    _, half, c = part.shape
    tr = _row_tile(half, c, 5)
    nh = half // tr

    def body(o_ref, r_ref, out_ref):
        acc = o_ref[...].astype(F32)
        for k in range(3):
            acc = acc + r_ref[k].astype(F32)
        out_ref[...] = acc

    return pl.pallas_call(
        body, name=name, grid=(nh,),
        in_specs=[pl.BlockSpec((None, tr, c), lambda i: (_my_chip(), i, 0)),
                  pl.BlockSpec((3, tr, c), lambda i: (0, i, 0))],
        out_specs=pl.BlockSpec((tr, c), lambda i: (lax.axis_index("c") * nh + i, 0)),
        out_shape=jax.ShapeDtypeStruct((2 * half, c), F32), compiler_params=_cparams(("parallel",)),
    )(part, rcv)


def _sum_devices(parts):
    _, n, _ = parts.shape
    tr = _tile(n, 512, 8)

    def body(p_ref, o_ref):
        acc = p_ref[0]
        for k in range(1, 8):
            acc = acc + p_ref[k]
        o_ref[...] = acc

    return pl.pallas_call(
        body, name="sum_devices", grid=(n // tr,),
        in_specs=[pl.BlockSpec((8, tr, 128), lambda i: (0, i, 0))],
        out_specs=pl.BlockSpec((tr, 128), lambda i: (i, 0)),
        out_shape=jax.ShapeDtypeStruct((n, 128), F32), compiler_params=_cparams(("parallel",)),
    )(parts)


def _adamw(w, g, m, v, name):
    r, c = w.shape
    tr = _row_tile(r, c, 7)
    c1 = 1.0 - ADAM_B1 ** ADAM_STEP
    c2 = 1.0 - ADAM_B2 ** ADAM_STEP

    def body(w_ref, g_ref, m_ref, v_ref, d_ref, mo_ref, vo_ref):
        gv = g_ref[...]
        mn = ADAM_B1 * m_ref[...] + (1.0 - ADAM_B1) * gv
        vn = ADAM_B2 * v_ref[...] + (1.0 - ADAM_B2) * (gv * gv)
        mo_ref[...] = mn
        vo_ref[...] = vn
        m_hat = mn / c1
        v_hat = vn / c2
        d_ref[...] = -ADAM_LR * (m_hat / (jnp.sqrt(v_hat) + ADAM_EPS) + ADAM_WD * w_ref[...])

    blk = pl.BlockSpec((tr, c), lambda i: (i, 0))
    sh = jax.ShapeDtypeStruct((r, c), F32)
    return pl.pallas_call(
        body, name=name, grid=(r // tr,), in_specs=[blk] * 4, out_specs=[blk] * 3, out_shape=[sh] * 3,
        compiler_params=_cparams(("parallel",)),
    )(w, g, m, v)


WEIGHTS = ['norm_mix_g', 'w_in', 'ln_a_g', 'ln_a_b', 'w_s', 'b_s', 'norm_a_g', 'conv_ssm_w', 'conv_ssm_b',
           'dt_bias', 'a_log', 'd_skip', 'ssm_norm_g', 'w_out', 'norm_ffn_g', 'w_up', 'conv_ffn_w',
           'conv_ffn_b', 'w_down', 'norm_ple_g', 'w_ple_gate', 'w_ple', 'norm_final_g']
BIG = ['w_in', 'w_out', 'w_up', 'w_down', 'w_ple_gate', 'w_ple']
SMALL = [n for n in WEIGHTS if n not in BIG]
PACK_ALIGN = 2048


def _pack(arrs):
    parts = []
    for a in arrs:
        f = a.reshape(-1).astype(F32)
        parts.append(jnp.pad(f, (0, (-f.shape[0]) % PACK_ALIGN)))
    return jnp.concatenate(parts).reshape(-1, 128)


def _unpack(buf, shapes):
    flat = buf.reshape(-1)
    out, off = [], 0
    for s in shapes:
        n = math.prod(s)
        out.append(flat[off:off + n].reshape(s))
        off += n + (-n) % PACK_ALIGN
    return out


def _pad_heads(v):
    return jnp.pad(v, ((0, 0), (0, HPAD - v.shape[1])))


def _col_sharded(full):
    r, c4 = full.shape
    return jnp.transpose(full.reshape(r, N_CHIPS, c4 // N_CHIPS), (1, 0, 2))


def _from_col_sharded(g):
    s, r, c = g.shape
    return jnp.transpose(g, (1, 0, 2)).reshape(r, s * c)


def kernel(x, p, norm_mix_g, w_in, ln_a_g, ln_a_b, w_s, b_s, norm_a_g, conv_ssm_w, conv_ssm_b, dt_bias, a_log, d_skip, ssm_norm_g, w_out, norm_ffn_g, w_up, conv_ffn_w, conv_ffn_b, w_down, norm_ple_g, w_ple_gate, w_ple, norm_final_g, loss_target, m_norm_mix_g, m_w_in, m_ln_a_g, m_ln_a_b, m_w_s, m_b_s, m_norm_a_g, m_conv_ssm_w, m_conv_ssm_b, m_dt_bias, m_a_log, m_d_skip, m_ssm_norm_g, m_w_out, m_norm_ffn_g, m_w_up, m_conv_ffn_w, m_conv_ffn_b, m_w_down, m_norm_ple_g, m_w_ple_gate, m_w_ple, m_norm_final_g, v_norm_mix_g, v_w_in, v_ln_a_g, v_ln_a_b, v_w_s, v_b_s, v_norm_a_g, v_conv_ssm_w, v_conv_ssm_b, v_dt_bias, v_a_log, v_d_skip, v_ssm_norm_g, v_w_out, v_norm_ffn_g, v_w_up, v_conv_ffn_w, v_conv_ffn_b, v_w_down, v_norm_ple_g, v_w_ple_gate, v_w_ple, v_norm_final_g):
    given = dict(locals())
    wts = {n: given[n] for n in WEIGHTS}
    mom = {n: given['m_' + n] for n in WEIGHTS}
    var = {n: given['v_' + n] for n in WEIGHTS}
    d = D_MODEL
    xt, pt, tgt = x[0], p[0, 0], loss_target[0]
    chip = 2 * lax.axis_index("x") + lax.axis_index("y")

    g_in, g_out, g_up, g_down, g_pg, g_ple, g_cs, g_cf = _gather_weights(
        [_cast_into_slot(wts[n][0], "cast_" + n) for n in BIG], [conv_ssm_w[0], conv_ffn_w[0]])
    w_in_full = _from_col_sharded(g_in)
    w_main = w_in_full[:, :D_MAIN]
    w_dt = _pad_heads(w_in_full[:, D_MAIN:])
    w_out_f = g_out.reshape(D_MIX, d)
    w_down_f = g_down.reshape(D_FF, d)
    w_pg_f = g_pg.reshape(d, d)
    cs_w = _from_col_sharded(g_cs)
    cf_w = _from_col_sharded(g_cf)
    consts = _ssd_consts()
    dtb, alog = _pad_heads(dt_bias), _pad_heads(a_log)
    de = jnp.repeat(d_skip[0], HEAD_DIM)[None, :]
    b_exp = jnp.broadcast_to(b_s[0][:, :, None], (N_GROUPS_A, CHUNK, CHUNK))

    a1 = _rms_fwd(xt, norm_mix_g, "rms_mix")
    proj = _matmul(a1, w_main, mode='nn', name="mm_proj", tm=512, tn=1024)
    dt_raw = _matmul(a1, w_dt, mode='nn', name="mm_dt", tm=1024, tn=128)
    yab = _gmlp_fwd(proj, ln_a_g, ln_a_b, w_s[0], b_exp, norm_a_g)
    yab, ysave, hs = _ssd_fwd(proj, dt_raw, yab, cs_w, conv_ssm_b, dtb, alog, de, ssm_norm_g, consts)
    h1 = _matmul(yab, w_out_f, mode='nn', name="mm_out", res=xt, tm=512, tn=1024, tk=4096)
    f = _rms_fwd(h1, norm_ffn_g, "rms_ffn")
    hid = _matmul(f, g_up, mode='nn', name="mm_up", b_sharded=True, tm=512, tn=1408)
    act = _ffn_act_fwd(hid, cf_w, conv_ffn_b)
    h2 = _matmul(act, w_down_f, mode='nn', name="mm_down", res=h1, tm=512, tn=1024, tk=2816)
    n3 = _rms_fwd(h2, norm_ple_g, "rms_ple")
    gl = _matmul(n3, w_pg_f, mode='nn', name="mm_pg", tm=512, tn=1024)
    pe = _matmul(pt, g_ple, mode='nn', name="mm_ple", b_sharded=True, tm=1024, tn=512)
    dh3, dgl, dpe, lossv, dgf = _tail(h2, gl, pe, tgt, norm_final_g[None, :])

    gs_ple = _matmul(pt, dpe, mode='tn', name="mm_dw_ple", out_dtype=BF16, out_shards=N_CHIPS,
                     tm=256, tn=512, tk=2048)
    gs_pg = _matmul(n3, dgl, mode='tn', name="mm_dw_pg", out_dtype=BF16, tm=1024, tn=1024, tk=2048)
    dn3 = _matmul(dgl, w_pg_f, mode='nt', name="mm_dn3", tm=512, tn=1024)
    dh2, dg_ple = _rms_bwd(h2, norm_ple_g, dn3, dh3, "rms_ple_bwd")
    dact = _matmul(dh2, w_down_f, mode='nt', name="mm_dact", out_dtype=BF16, tm=512, tn=1408)
    gs_down = _matmul(act, dh2, mode='tn', name="mm_dw_down", out_dtype=BF16, tm=1408, tn=1024, tk=2048)
    dpg, dpu, wg_acc, wu_acc = _ffn_act_bwd(hid, cf_w, conv_ffn_b, dact)
    hc = N_CHIPS // 2
    gs_up = jnp.concatenate(
        [_matmul(f, dpg, mode='tn', name="mm_dw_up_g", out_dtype=BF16, out_shards=hc, tm=1024, tn=1408, tk=2048),
         _matmul(f, dpu, mode='tn', name="mm_dw_up_u", out_dtype=BF16, out_shards=hc, tm=1024, tn=1408, tk=2048)],
        axis=0)
    df = _matmul(dpg, g_up[:hc], mode='nt', name="mm_df_g", b_sharded=True, tm=1024, tn=1024, tk=1408)
    df = _matmul(dpu, g_up[hc:], mode='nt', name="mm_df_u", b_sharded=True, res=df, tm=1024, tn=1024, tk=1408)
    dh1, dg_ffn = _rms_bwd(h1, norm_ffn_g, df, dh2, "rms_ffn_bwd")
    dyab = _matmul(dh1, w_out_f, mode='nt', name="mm_dyab", tm=512, tn=1024)
    gs_out = _matmul(yab, dh1, mode='tn', name="mm_dw_out", out_dtype=BF16, tm=1024, tn=1024, tk=2048)
    duv, dws, dbs, dlng, dlnb, dnag = _gmlp_bwd(proj, dyab, ln_a_g, ln_a_b, w_s[0], b_exp, norm_a_g)
    dz, dxs, dbm, dcm, ddt_raw, acc_x, acc_b, acc_c, acc_gain, acc_head = _ssd_bwd(
        proj, dt_raw, dyab, ysave, hs, cs_w, conv_ssm_b, dtb, alog, de, ssm_norm_g, consts)
    dproj = jnp.concatenate([duv, dz, dxs, dbm, dcm], axis=1)
    dw_main = _matmul(a1, dproj, mode='tn', name="mm_dw_main", out_dtype=BF16, tm=1024, tn=1024, tk=2048)
    dw_dt = _matmul(a1, ddt_raw, mode='tn', name="mm_dw_dt", out_dtype=BF16, tm=1024, tn=128, tk=2048)
    gs_in = _col_sharded(jnp.concatenate([dw_main, dw_dt[:, :N_HEADS]], axis=1))
    da_dt = _matmul(ddt_raw, w_dt, mode='nt', name="mm_da_dt", tm=1024, tn=1024)
    da = _matmul(dproj, w_main, mode='nt', name="mm_da", res=da_dt, tm=1024, tn=1024, tk=2048)
    dx, dg_mix = _rms_bwd(xt, norm_mix_g, da, dh1, "rms_mix_bwd")

    gs = [gs_in, gs_out.reshape(N_CHIPS, D_MIX // N_CHIPS, d), gs_up,
          gs_down.reshape(N_CHIPS, D_FF // N_CHIPS, d), gs_pg.reshape(N_CHIPS, d // N_CHIPS, d), gs_ple]
    sib = _rs_sibling(gs)
    chip_part = [_add_pairs(a, b, "rs_add_%d" % i) for i, (a, b) in enumerate(zip(gs, sib))]
    rcv = _rs_chips(chip_part)
    halves = [_sum_chips(a, b, "rs_sum_%d" % i) for i, (a, b) in enumerate(zip(chip_part, rcv))]
    g_big = dict(zip(BIG, _share_sibling(halves)))

    def conv_rows(acc, k):
        return acc[:, k, :].reshape(1, -1)

    dcw = jnp.concatenate([jnp.concatenate([conv_rows(acc_x, k), conv_rows(acc_b, k), conv_rows(acc_c, k)], axis=1)
                           for k in range(SSM_CONV)], axis=0)
    dcb = jnp.concatenate([conv_rows(acc_x, SSM_CONV), conv_rows(acc_b, SSM_CONV), conv_rows(acc_c, SSM_CONV)], axis=1)
    part = {
        'norm_mix_g': dg_mix, 'ln_a_g': dlng, 'ln_a_b': dlnb, 'w_s': dws, 'b_s': dbs, 'norm_a_g': dnag,
        'conv_ssm_w': dcw, 'conv_ssm_b': dcb,
        'dt_bias': acc_head[0:1, :N_HEADS], 'a_log': acc_head[1:2, :N_HEADS], 'd_skip': acc_head[2:3, :N_HEADS],
        'ssm_norm_g': acc_gain[:, 0, :], 'norm_ffn_g': dg_ffn,
        'conv_ffn_w': jnp.concatenate([wg_acc[:FFN_CONV], wu_acc[:FFN_CONV]], axis=1),
        'conv_ffn_b': jnp.concatenate([wg_acc[FFN_CONV:FFN_CONV + 1], wu_acc[FFN_CONV:FFN_CONV + 1]], axis=1),
        'norm_ple_g': dg_ple, 'norm_final_g': dgf,
    }
    full_shapes = {n: wts[n].shape for n in SMALL}
    full_shapes['conv_ssm_w'] = (1, SSM_CONV, D_XBC)
    full_shapes['conv_ffn_w'] = (1, FFN_CONV, 2 * D_FF)
    packed = _pack([part[n] for n in SMALL] + [lossv[:, 0:1]])
    total = _sum_devices(_allgather_small(packed))
    pieces = _unpack(total, [full_shapes[n] for n in SMALL] + [(1,)])
    g_small = dict(zip(SMALL, pieces[:-1]))
    loss = pieces[-1][0]
    for n in ('conv_ssm_w', 'conv_ffn_w'):
        width = wts[n].shape[2]
        g_small[n] = lax.dynamic_slice_in_dim(g_small[n], chip * width, width, axis=2)

    grads, delta, new_m, new_v = {}, {}, {}, {}
    for n in BIG:
        shp = wts[n].shape
        dl, mn, vn = _adamw(wts[n][0], g_big[n], mom[n][0], var[n][0], "adamw_" + n)
        grads[n], delta[n], new_m[n], new_v[n] = (g_big[n].reshape(shp), dl.reshape(shp), mn.reshape(shp),
                                                  vn.reshape(shp))
    shapes = [wts[n].shape for n in SMALL]
    dl, mn, vn = _adamw(_pack([wts[n] for n in SMALL]), _pack([g_small[n] for n in SMALL]),
                        _pack([mom[n] for n in SMALL]), _pack([var[n] for n in SMALL]), "adamw_small")
    for n, a, b, c in zip(SMALL, _unpack(dl, shapes), _unpack(mn, shapes), _unpack(vn, shapes)):
        grads[n], delta[n], new_m[n], new_v[n] = g_small[n], a, b, c

    return (loss, dx[None], *[grads[n] for n in WEIGHTS], *[delta[n] for n in WEIGHTS],
            *[new_m[n] for n in WEIGHTS], *[new_v[n] for n in WEIGHTS])
```

```python
import functools
import math

import jax
import jax.numpy as jnp
from jax import lax
from jax.experimental import pallas as pl
from jax.experimental.pallas import tpu as pltpu

D_MODEL = 2048
SEQ = 8192
D_MIX = 2 * D_MODEL
D_A = D_MIX // 2
CHUNK = 128
N_GROUPS_A = D_A // 128
D_SSM = D_MIX - D_A
HEAD_DIM = 64
N_HEADS = D_SSM // HEAD_DIM
HEADS_PER_GROUP = 4
N_SSM_GROUPS = N_HEADS // HEADS_PER_GROUP
GW = HEADS_PER_GROUP * HEAD_DIM
D_STATE = 128
SSM_CONV = 4
D_BC = N_SSM_GROUPS * D_STATE
D_XBC = D_SSM + 2 * D_BC
D_MAIN = 2 * D_A + D_SSM + D_XBC
D_IN = D_MAIN + N_HEADS
D_FF = (D_MODEL * 11) // 4
FFN_CONV = 3
D_PLE = 256
EPS = 1e-6
HPAD = 128
N_CHIPS = 4

ADAM_LR = 0.001
ADAM_B1 = 0.9
ADAM_B2 = 0.999
ADAM_EPS = 1e-08
ADAM_WD = 0.01
ADAM_STEP = 10

F32 = jnp.float32
BF16 = jnp.bfloat16
MESH = pl.DeviceIdType.MESH
VMEM_LIMIT = 56 * 1024 * 1024


def _cparams(sem):
    return pltpu.CompilerParams(dimension_semantics=sem, vmem_limit_bytes=VMEM_LIMIT)


def _tile(n, pref, mult):
    t = min(pref, n)
    t -= t % mult
    while n % t:
        t -= mult
    return t


def _dot(a, b):
    return jnp.dot(a, b, preferred_element_type=F32)


def _dot_nt(a, b):
    return lax.dot_general(a, b, (((1,), (1,)), ((), ())), preferred_element_type=F32)


def _dot_tn(a, b):
    return lax.dot_general(a, b, (((0,), (0,)), ((), ())), preferred_element_type=F32)


def _split3(x):
    hi = x.astype(BF16)
    r = x - hi.astype(F32)
    mid = r.astype(BF16)
    lo = (r - mid.astype(F32)).astype(BF16)
    return hi, mid, lo


def _x01(x, e):
    h, m, l = _split3(x)
    return _dot(h, e) + _dot(m, e) + _dot(l, e)


def _x01_nt(x, e):
    h, m, l = _split3(x)
    return _dot_nt(h, e) + _dot_nt(m, e) + _dot_nt(l, e)


def _e01x(e, x):
    h, m, l = _split3(x)
    return _dot(e, h) + _dot(e, m) + _dot(e, l)


def _e01x_tn(e, x):
    h, m, l = _split3(x)
    return _dot_tn(e, h) + _dot_tn(e, m) + _dot_tn(e, l)


def _sigmoid(x):
    return 1.0 / (1.0 + jnp.exp(-x))


_GELU_C = math.sqrt(2.0 / math.pi)


def _gelu_and_grad(x):
    x2 = x * x
    th = jnp.tanh(_GELU_C * (x + 0.044715 * x * x2))
    y = 0.5 * x * (1.0 + th)
    dy = 0.5 * (1.0 + th) + 0.5 * x * (1.0 - th * th) * (_GELU_C * (1.0 + 3.0 * 0.044715 * x2))
    return y, dy


def _silu_and_grad(x):
    s = _sigmoid(x)
    return x * s, s * (1.0 + x * (1.0 - s))


def _softplus(x):
    u = jnp.exp(-jnp.abs(x))
    w = 1.0 + u
    l1p = jnp.where(w == 1.0, u, jnp.log(w) * (u / (w - 1.0)))
    return jnp.maximum(x, 0.0) + l1p


def _matmul(a, b, *, mode, name, out_dtype=F32, res=None, tm=512, tn=512, tk=2048,
            b_sharded=False, out_shards=0, hosted=None):
    if mode == 'tn':
        kdim, m = a.shape
        n = b.shape[1]
    else:
        m, kdim = a.shape
        if b_sharded:
            s_b, d1, d2 = b.shape
            n = s_b * d2 if mode == 'nn' else d1
        else:
            n = b.shape[1] if mode == 'nn' else b.shape[0]
    per = None
    if b_sharded:
        per = b.shape[2]
    if out_shards:
        per = n // out_shards
    tm = _tile(m, tm, 128 if mode == 'tn' else 8)
    if mode == 'nt' and b_sharded:
        tn = _tile(n, tn, 128)
        tk = _tile(per, tk, 128)
    elif per is not None:
        tn = _tile(per, tn, 128)
        tk = _tile(kdim, tk, 128 if mode != 'tn' else 8)
    else:
        tn = _tile(n, tn, 128)
        tk = _tile(kdim, tk, 128 if mode != 'tn' else 8)
    nm, nn_, nk = m // tm, n // tn, kdim // tk
    has_res = res is not None
    n_in = 2 + has_res
    nh_in = len(hosted['arrays']) if hosted else 0
    nh_out = len(hosted['out_shape']) if hosted else 0

    def body(*refs):
        a_ref, b_ref = refs[0], refs[1]
        res_ref = refs[2] if has_res else None
        o_ref = refs[n_in + nh_in]
        if hosted:
            hins = refs[n_in:n_in + nh_in]
            houts = refs[n_in + nh_in + 1:n_in + nh_in + 1 + nh_out]
            sems = refs[-2:]
            ids = [pl.program_id(d) for d in range(3)]
            at_first = jnp.logical_and(jnp.logical_and(ids[0] == 0, ids[1] == 0), ids[2] == 0)
            at_last = jnp.logical_and(jnp.logical_and(ids[0] == nn_ - 1, ids[1] == nm - 1), ids[2] == nk - 1)
            _host_phase(hosted, 'start', at_first, hins, houts, sems)
        av = a_ref[...].astype(BF16)
        bv = b_ref[...].astype(BF16)
        if mode == 'nn':
            p = _dot(av, bv)
        elif mode == 'nt':
            p = _dot_nt(av, bv)
        else:
            p = _dot_tn(av, bv)

        def fin(v):
            if has_res:
                v = v + res_ref[...]
            o_ref[...] = v.astype(o_ref.dtype)

        if nk == 1:
            fin(p)
        else:
            acc_ref = refs[n_in + nh_in + 1 + nh_out]
            k = pl.program_id(2)

            @pl.when(k == 0)
            def _():
                acc_ref[...] = p

            @pl.when(k > 0)
            def _():
                acc_ref[...] += p

            @pl.when(k == nk - 1)
            def _():
                fin(acc_ref[...])
        if hosted:
            _host_phase(hosted, 'mid', at_last, hins, houts, sems)
            _host_phase(hosted, 'finish', at_last, hins, houts, sems)

    if mode == 'nn':
        a_spec = pl.BlockSpec((tm, tk), lambda j, i, k: (i, k))
        if b_sharded:
            nps = per // tn
            b_spec = pl.BlockSpec((None, tk, tn), lambda j, i, k: (j // nps, k, j % nps))
        else:
            b_spec = pl.BlockSpec((tk, tn), lambda j, i, k: (k, j))
    elif mode == 'nt':
        a_spec = pl.BlockSpec((tm, tk), lambda j, i, k: (i, k))
        if b_sharded:
            kps = per // tk
            b_spec = pl.BlockSpec((None, tn, tk), lambda j, i, k: (k // kps, j, k % kps))
        else:
            b_spec = pl.BlockSpec((tn, tk), lambda j, i, k: (j, k))
    else:
        a_spec = pl.BlockSpec((tk, tm), lambda j, i, k: (k, i))
        b_spec = pl.BlockSpec((tk, tn), lambda j, i, k: (k, j))
    in_specs = [a_spec, b_spec]
    args = [a, b]
    if has_res:
        in_specs.append(pl.BlockSpec((tm, tn), lambda j, i, k: (i, j)))
        args.append(res)
    if out_shards:
        nps_o = per // tn
        out_shape = jax.ShapeDtypeStruct((out_shards, m, per), out_dtype)
        out_spec = pl.BlockSpec((None, tm, tn), lambda j, i, k: (j // nps_o, i, j % nps_o))
    else:
        out_shape = jax.ShapeDtypeStruct((m, n), out_dtype)
        out_spec = pl.BlockSpec((tm, tn), lambda j, i, k: (i, j))
    scratch = [pltpu.VMEM((tm, tn), F32)] if nk > 1 else []
    if not hosted:
        return pl.pallas_call(
            body, name=name, grid=(nn_, nm, nk), in_specs=in_specs, out_specs=out_spec,
            out_shape=out_shape, scratch_shapes=scratch,
            compiler_params=_cparams(("parallel", "parallel", "arbitrary")),
        )(*args)
    h_in, h_out, h_shape, h_scratch, h_alias = _host_plumbing(hosted, n_in, 1)
    return pl.pallas_call(
        body, name=name, grid=(nn_, nm, nk), in_specs=in_specs + h_in, out_specs=[out_spec] + h_out,
        out_shape=[out_shape] + h_shape, scratch_shapes=scratch + h_scratch, input_output_aliases=h_alias,
        compiler_params=_cparams(("arbitrary", "arbitrary", "arbitrary")),
    )(*args, *hosted['arrays'])


def _rms_fwd(x, g, name):
    t, d = x.shape
    tt = _tile(t, 512, 8)

    def body(x_ref, g_ref, o_ref):
        xv = x_ref[...]
        r = lax.rsqrt(jnp.mean(xv * xv, axis=-1, keepdims=True) + EPS)
        o_ref[...] = (xv * r * g_ref[...]).astype(o_ref.dtype)

    return pl.pallas_call(
        body, name=name, grid=(t // tt,),
        in_specs=[pl.BlockSpec((tt, d), lambda i: (i, 0)), pl.BlockSpec((1, d), lambda i: (0, 0))],
        out_specs=pl.BlockSpec((tt, d), lambda i: (i, 0)),
        out_shape=jax.ShapeDtypeStruct((t, d), BF16),
        compiler_params=_cparams(("parallel",)),
    )(x, g)


def _rms_bwd(x, g, dy, dres, name):
    t, d = x.shape
    tt = _tile(t, 256, 8)

    def body(x_ref, g_ref, dy_ref, dres_ref, dx_ref, dg_ref):
        i = pl.program_id(0)
        xv = x_ref[...]
        r = lax.rsqrt(jnp.mean(xv * xv, axis=-1, keepdims=True) + EPS)
        xh = xv * r
        dyv = dy_ref[...].astype(F32)
        dxh = dyv * g_ref[...]
        c = jnp.mean(dxh * xh, axis=-1, keepdims=True)
        dx_ref[...] = dres_ref[...] + r * (dxh - xh * c)
        part = jnp.sum(dyv * xh, axis=0, keepdims=True)

        @pl.when(i == 0)
        def _():
            dg_ref[...] = part

        @pl.when(i > 0)
        def _():
            dg_ref[...] += part

    row = pl.BlockSpec((tt, d), lambda i: (i, 0))
    vec = pl.BlockSpec((1, d), lambda i: (0, 0))
    return pl.pallas_call(
        body, name=name, grid=(t // tt,),
        in_specs=[row, vec, row, row], out_specs=[row, vec],
        out_shape=[jax.ShapeDtypeStruct((t, d), F32), jax.ShapeDtypeStruct((1, d), F32)],
        compiler_params=_cparams(("arbitrary",)),
    )(x, g, dy, dres)


def _tail(h2, gl, pe, target, gfin):
    t, d = h2.shape
    tt = _tile(t, 256, 8)

    def body(h2_ref, gl_ref, pe_ref, tg_ref, gf_ref, dh3_ref, dgl_ref, dpe_ref, loss_ref, dgf_ref):
        i = pl.program_id(0)
        sig = _sigmoid(gl_ref[...])
        pev = pe_ref[...]
        h3 = h2_ref[...] + sig * pev
        r = lax.rsqrt(jnp.mean(h3 * h3, axis=-1, keepdims=True) + EPS)
        xh = h3 * r
        gf = gf_ref[...]
        e = xh * gf - tg_ref[...]
        dy = e * (1.0 / d)
        dxh = dy * gf
        c = jnp.mean(dxh * xh, axis=-1, keepdims=True)
        dh3 = r * (dxh - xh * c)
        dh3_ref[...] = dh3
        dgl_ref[...] = (dh3 * pev * sig * (1.0 - sig)).astype(BF16)
        dpe_ref[...] = (dh3 * sig).astype(BF16)
        lpart = jnp.sum(e * e, axis=0, keepdims=True) * (0.5 / d)
        gpart = jnp.sum(dy * xh, axis=0, keepdims=True)

        @pl.when(i == 0)
        def _():
            loss_ref[...] = lpart
            dgf_ref[...] = gpart

        @pl.when(i > 0)
        def _():
            loss_ref[...] += lpart
            dgf_ref[...] += gpart

        @pl.when(i == t // tt - 1)
        def _():
            loss_ref[...] = jnp.broadcast_to(jnp.sum(loss_ref[...], axis=-1, keepdims=True), (1, d))

    row = pl.BlockSpec((tt, d), lambda i: (i, 0))
    vec = pl.BlockSpec((1, d), lambda i: (0, 0))
    return pl.pallas_call(
        body, name="tail", grid=(t // tt,),
        in_specs=[row, row, row, row, vec], out_specs=[row, row, row, vec, vec],
        out_shape=[jax.ShapeDtypeStruct((t, d), F32), jax.ShapeDtypeStruct((t, d), BF16),
                   jax.ShapeDtypeStruct((t, d), BF16), jax.ShapeDtypeStruct((1, d), F32),
                   jax.ShapeDtypeStruct((1, d), F32)],
        compiler_params=_cparams(("arbitrary",)),
    )(h2, gl, pe, target, gfin)


def _conv(cur_ref, prev_ref, w_ref, b_ref, ext_ref, first, width):
    rows = cur_ref.shape[0]
    ext_ref[0:8, :] = jnp.where(first, 0.0, prev_ref[...])
    ext_ref[8:8 + rows, :] = cur_ref[...]
    acc = b_ref[...]
    for k in range(width):
        acc = acc + w_ref[k:k + 1, :] * ext_ref[pl.ds(9 - width + k, rows), :]
    return acc


def _ffn_specs(t, tt, tc, nf):
    hb = tt // 8
    cur_g = pl.BlockSpec((tt, tc), lambda j, i: (i, j))
    cur_u = pl.BlockSpec((tt, tc), lambda j, i: (i, j + nf))
    prev_g = pl.BlockSpec((8, tc), lambda j, i: (jnp.maximum(i * hb - 1, 0), j))
    prev_u = pl.BlockSpec((8, tc), lambda j, i: (jnp.maximum(i * hb - 1, 0), j + nf))
    w_g = pl.BlockSpec((FFN_CONV, tc), lambda j, i: (0, j))
    w_u = pl.BlockSpec((FFN_CONV, tc), lambda j, i: (0, j + nf))
    b_g = pl.BlockSpec((1, tc), lambda j, i: (0, j))
    b_u = pl.BlockSpec((1, tc), lambda j, i: (0, j + nf))
    return [cur_g, prev_g, cur_u, prev_u, w_g, w_u, b_g, b_u]


FFN_TC = 512


def _shift_down(prev, cur, n, rid):
    return jnp.where(rid < n, pltpu.roll(prev, n, 0), pltpu.roll(cur, n, 0))


def _shift_up(cur, nxt, n, rid):
    return jnp.where(rid < 8 - n, pltpu.roll(cur, 8 - n, 0), pltpu.roll(nxt, 8 - n, 0))


def _conv3_group(prev, cur, w_ref, b_ref, rid):
    x1 = _shift_down(prev, cur, 1, rid)
    x2 = _shift_down(prev, cur, 2, rid)
    return b_ref[...] + w_ref[2:3, :] * cur + w_ref[1:2, :] * x1 + w_ref[0:1, :] * x2


def _ffn_act_fwd(hid, cw, cb):
    t = hid.shape[0]
    tt = _tile(t, 512, 16)
    tc = _tile(D_FF, FFN_TC, 128)
    nf = D_FF // tc

    def body(g_ref, gp_ref, u_ref, up_ref, wg_ref, wu_ref, bg_ref, bu_ref, o_ref):
        first = pl.program_id(1) == 0
        rid = lax.broadcasted_iota(jnp.int32, (8, tc), 0)

        def act(pg, cg, pu, cu):
            gate = _conv3_group(pg, cg, wg_ref, bg_ref, rid)
            up = _conv3_group(pu, cu, wu_ref, bu_ref, rid)
            return gate * _sigmoid(gate) * up

        def step(s, carry):
            pg, pu = carry
            r0 = pl.multiple_of(s * 16, 16)
            g0, g1 = g_ref[pl.ds(r0, 8), :], g_ref[pl.ds(r0 + 8, 8), :]
            u0, u1 = u_ref[pl.ds(r0, 8), :], u_ref[pl.ds(r0 + 8, 8), :]
            out = jnp.concatenate([act(pg, g0, pu, u0), act(g0, g1, u0, u1)], axis=0)
            o_ref[pl.ds(r0, 16), :] = out.astype(BF16)
            return g1, u1

        init = (jnp.where(first, 0.0, gp_ref[...]), jnp.where(first, 0.0, up_ref[...]))
        lax.fori_loop(0, tt // 16, step, init)

    return pl.pallas_call(
        body, name="ffn_act_fwd", grid=(nf, t // tt), in_specs=_ffn_specs(t, tt, tc, nf),
        out_specs=pl.BlockSpec((tt, tc), lambda j, i: (i, j)),
        out_shape=jax.ShapeDtypeStruct((t, D_FF), BF16),
        compiler_params=_cparams(("parallel", "arbitrary")),
    )(hid, hid, hid, hid, cw, cw, cb, cb)


def _ffn_act_bwd(hid, cw, cb, dact):
    t = hid.shape[0]
    tt = _tile(t, 512, 16)
    tc = _tile(D_FF, FFN_TC, 128)
    nf = D_FF // tc
    nt = t // tt
    n16 = tt // 16
    hb = tt // 8

    def body(g_ref, gp_ref, u_ref, up_ref, wg_ref, wu_ref, bg_ref, bu_ref, gn_ref, un_ref, da_ref, dan_ref,
             og_ref, ou_ref, ag_ref, au_ref, accs):
        i = pl.program_id(1)
        first, last = i == 0, i == nt - 1
        rid = lax.broadcasted_iota(jnp.int32, (8, tc), 0)
        accs[...] = jnp.zeros_like(accs)

        def dgroup(pg, cg, pu, cu, da):
            gate = _conv3_group(pg, cg, wg_ref, bg_ref, rid)
            up = _conv3_group(pu, cu, wu_ref, bu_ref, rid)
            sv, sgr = _silu_and_grad(gate)
            return da * up * sgr, da * sv

        def finish(x, d0, d1, w_ref):
            s1 = _shift_up(d0, d1, 1, rid)
            s2 = _shift_up(d0, d1, 2, rid)
            dpre = w_ref[2:3, :] * d0 + w_ref[1:2, :] * s1 + w_ref[0:1, :] * s2
            return dpre, (x * s2, x * s1, x * d0, d0)

        def two_groups(it, carry, x2g, x2u, da_blk, zero_ahead):
            x0g, x0u, d0g, d0u, da1 = carry
            r0 = it * 16 if isinstance(it, int) else pl.multiple_of(it * 16, 16)
            x1g, x1u = g_ref[pl.ds(r0 + 8, 8), :], u_ref[pl.ds(r0 + 8, 8), :]
            d1g, d1u = dgroup(x0g, x1g, x0u, x1u, da1)
            d2g, d2u = dgroup(x1g, x2g, x1u, x2u, da_blk[0:8])
            d2g = jnp.where(zero_ahead, 0.0, d2g)
            d2u = jnp.where(zero_ahead, 0.0, d2u)
            outs = []
            for half, (xa, xb, da_, db_, dc_, w_ref, o_ref) in enumerate((
                    (x0g, x1g, d0g, d1g, d2g, wg_ref, og_ref), (x0u, x1u, d0u, d1u, d2u, wu_ref, ou_ref))):
                pa, prods_a = finish(xa, da_, db_, w_ref)
                pb, prods_b = finish(xb, db_, dc_, w_ref)
                o_ref[pl.ds(r0, 16), :] = jnp.concatenate([pa, pb], axis=0).astype(BF16)
                for k in range(4):
                    accs[4 * half + k] += prods_a[k] + prods_b[k]
            return x2g, x2u, d2g, d2u, da_blk[8:16]

        def step(it, carry):
            r1 = pl.multiple_of(it * 16 + 16, 16)
            return two_groups(it, carry, g_ref[pl.ds(r1, 8), :], u_ref[pl.ds(r1, 8), :],
                              da_ref[pl.ds(r1, 16), :].astype(F32), False)

        da0 = da_ref[pl.ds(0, 16), :].astype(F32)
        x0g, x0u = g_ref[pl.ds(0, 8), :], u_ref[pl.ds(0, 8), :]
        d0g, d0u = dgroup(jnp.where(first, 0.0, gp_ref[...]), x0g, jnp.where(first, 0.0, up_ref[...]), x0u, da0[0:8])
        carry = lax.fori_loop(0, n16 - 1, step, (x0g, x0u, d0g, d0u, da0[8:16]))
        two_groups(n16 - 1, carry, gn_ref[...], un_ref[...], dan_ref[...].astype(F32), last)

        @pl.when(first)
        def _():
            ag_ref[...] = jnp.zeros_like(ag_ref)
            au_ref[...] = jnp.zeros_like(au_ref)

        for half, a_ref in enumerate((ag_ref, au_ref)):
            for k in range(4):
                a_ref[k:k + 1, :] += jnp.sum(accs[4 * half + k], axis=0, keepdims=True)

    def nxt8(i):
        return jnp.minimum((i + 1) * hb, t // 8 - 1)

    def nxt16(i):
        return jnp.minimum((i + 1) * n16, t // 16 - 1)

    out_blk = pl.BlockSpec((tt, tc), lambda j, i: (i, j))
    acc_spec = pl.BlockSpec((8, tc), lambda j, i: (0, j))
    in_specs = _ffn_specs(t, tt, tc, nf) + [
        pl.BlockSpec((8, tc), lambda j, i: (nxt8(i), j)), pl.BlockSpec((8, tc), lambda j, i: (nxt8(i), j + nf)),
        out_blk, pl.BlockSpec((16, tc), lambda j, i: (nxt16(i), j))]
    return pl.pallas_call(
        body, name="ffn_act_bwd", grid=(nf, nt), in_specs=in_specs,
        out_specs=[out_blk, out_blk, acc_spec, acc_spec],
        out_shape=[jax.ShapeDtypeStruct((t, D_FF), BF16), jax.ShapeDtypeStruct((t, D_FF), BF16),
                   jax.ShapeDtypeStruct((8, D_FF), F32), jax.ShapeDtypeStruct((8, D_FF), F32)],
        scratch_shapes=[pltpu.VMEM((8, 8, tc), F32)],
        compiler_params=_cparams(("parallel", "arbitrary")),
    )(hid, hid, hid, hid, cw, cw, cb, cb, hid, hid, dact, dact)


def _tri_mask():
    r = lax.broadcasted_iota(jnp.int32, (CHUNK, CHUNK), 0)
    c = lax.broadcasted_iota(jnp.int32, (CHUNK, CHUNK), 1)
    return r >= c


def _gmlp_group_fwd(uv_ref, lng_ref, lnb_ref, ws_ref, bexp_ref, tri, g, want_grad):
    lo, hi = g * 128, (g + 1) * 128
    u_pre = uv_ref[:, lo:hi]
    v_pre = uv_ref[:, D_A + lo:D_A + hi]
    u, du = _gelu_and_grad(u_pre)
    v, dv = _gelu_and_grad(v_pre)
    mu = jnp.mean(v, axis=-1, keepdims=True)
    dc = v - mu
    rs = lax.rsqrt(jnp.mean(dc * dc, axis=-1, keepdims=True) + EPS)
    xh = dc * rs
    vn = (xh * lng_ref[:, lo:hi] + lnb_ref[:, lo:hi]).astype(BF16)
    w = jnp.where(tri, ws_ref[g], 0.0).astype(BF16)
    sg = _dot(w, vn) + bexp_ref[g]
    if want_grad:
        return u, du, dv, rs, xh, vn, w, sg
    return u * sg


def _gmlp_fwd(proj, ln_g, ln_b, w_s, b_exp, na_g):
    t = proj.shape[0]
    ng = N_GROUPS_A

    def body(uv_ref, lng_ref, lnb_ref, ws_ref, bexp_ref, nag_ref, o_ref):
        tri = _tri_mask()
        ys = [_gmlp_group_fwd(uv_ref, lng_ref, lnb_ref, ws_ref, bexp_ref, tri, g, False) for g in range(ng)]
        ssq = ys[0] * 0.0
        for y in ys:
            ssq = ssq + y * y
        r = lax.rsqrt(jnp.sum(ssq, axis=-1, keepdims=True) * (1.0 / D_A) + EPS)
        for g, y in enumerate(ys):
            o_ref[:, g * 128:(g + 1) * 128] = (y * r * nag_ref[:, g * 128:(g + 1) * 128]).astype(BF16)

    vec = pl.BlockSpec((1, D_A), lambda i: (0, 0))
    cube = pl.BlockSpec((ng, CHUNK, CHUNK), lambda i: (0, 0, 0))
    return pl.pallas_call(
        body, name="gmlp_fwd", grid=(t // CHUNK,),
        in_specs=[pl.BlockSpec((CHUNK, 2 * D_A), lambda i: (i, 0)), vec, vec, cube, cube, vec],
        out_specs=pl.BlockSpec((CHUNK, D_A), lambda i: (i, 0)),
        out_shape=jax.ShapeDtypeStruct((t, D_MIX), BF16),
        compiler_params=_cparams(("parallel",)),
    )(proj, ln_g, ln_b, w_s, b_exp, na_g)


def _gmlp_bwd(proj, dyab, ln_g, ln_b, w_s, b_exp, na_g):
    t = proj.shape[0]
    ng = N_GROUPS_A
    nsteps = t // CHUNK

    def body(uv_ref, dy_ref, lng_ref, lnb_ref, ws_ref, bexp_ref, nag_ref,
             duv_ref, dws_ref, dbs_ref, dlng_ref, dlnb_ref, dnag_ref, dbacc):
        i = pl.program_id(0)
        tri = _tri_mask()

        @pl.when(i == 0)
        def _():
            dws_ref[...] = jnp.zeros_like(dws_ref)
            dbacc[...] = jnp.zeros_like(dbacc)
            dlng_ref[...] = jnp.zeros_like(dlng_ref)
            dlnb_ref[...] = jnp.zeros_like(dlnb_ref)
            dnag_ref[...] = jnp.zeros_like(dnag_ref)

        st = [_gmlp_group_fwd(uv_ref, lng_ref, lnb_ref, ws_ref, bexp_ref, tri, g, True) for g in range(ng)]
        ssq = st[0][0] * 0.0
        for s in st:
            y = s[0] * s[7]
            ssq = ssq + y * y
        r = lax.rsqrt(jnp.sum(ssq, axis=-1, keepdims=True) * (1.0 / D_A) + EPS)
        csum = st[0][0] * 0.0
        for g, s in enumerate(st):
            sl = slice(g * 128, (g + 1) * 128)
            xhy = s[0] * s[7] * r
            dya = dy_ref[:, sl]
            dnag_ref[:, sl] += jnp.sum(dya * xhy, axis=0, keepdims=True)
            csum = csum + dya * nag_ref[:, sl] * xhy
        c1 = jnp.sum(csum, axis=-1, keepdims=True) * (1.0 / D_A)
        for g, s in enumerate(st):
            u, du, dv, rs, xh, vn, w, sg = s
            sl = slice(g * 128, (g + 1) * 128)
            dy = r * (dy_ref[:, sl] * nag_ref[:, sl] - u * sg * r * c1)
            dsg = dy * u
            dsg_b = dsg.astype(BF16)
            dws_ref[g] += _dot_nt(dsg_b, vn)
            dbacc[g] += dsg
            dvn = _dot_tn(w, dsg_b)
            dlnb_ref[:, sl] += jnp.sum(dvn, axis=0, keepdims=True)
            dlng_ref[:, sl] += jnp.sum(dvn * xh, axis=0, keepdims=True)
            dxh = dvn * lng_ref[:, sl]
            dvv = rs * (dxh - jnp.mean(dxh, axis=-1, keepdims=True)
                        - xh * jnp.mean(dxh * xh, axis=-1, keepdims=True))
            duv_ref[:, sl] = (dy * sg * du).astype(BF16)
            duv_ref[:, D_A + g * 128:D_A + (g + 1) * 128] = (dvv * dv).astype(BF16)

        @pl.when(i == nsteps - 1)
        def _():
            for g in range(ng):
                dws_ref[g] = jnp.where(tri, dws_ref[g], 0.0)
                dbs_ref[g] = jnp.sum(dbacc[g], axis=-1, keepdims=True)

    vec = pl.BlockSpec((1, D_A), lambda i: (0, 0))
    cube = pl.BlockSpec((ng, CHUNK, CHUNK), lambda i: (0, 0, 0))
    return pl.pallas_call(
        body, name="gmlp_bwd", grid=(nsteps,),
        in_specs=[pl.BlockSpec((CHUNK, 2 * D_A), lambda i: (i, 0)),
                  pl.BlockSpec((CHUNK, D_A), lambda i: (i, 0)), vec, vec, cube, cube, vec],
        out_specs=[pl.BlockSpec((CHUNK, 2 * D_A), lambda i: (i, 0)), cube,
                   pl.BlockSpec((ng, CHUNK, 1), lambda i: (0, 0, 0)), vec, vec, vec],
        out_shape=[jax.ShapeDtypeStruct((t, 2 * D_A), BF16), jax.ShapeDtypeStruct((ng, CHUNK, CHUNK), F32),
                   jax.ShapeDtypeStruct((ng, CHUNK, 1), F32), jax.ShapeDtypeStruct((1, D_A), F32),
                   jax.ShapeDtypeStruct((1, D_A), F32), jax.ShapeDtypeStruct((1, D_A), F32)],
        scratch_shapes=[pltpu.VMEM((ng, CHUNK, CHUNK), F32)],
        compiler_params=_cparams(("arbitrary",)),
    )(proj, dyab, ln_g, ln_b, w_s, b_exp, na_g)


OFF_Z = 2 * D_A
OFF_XS = OFF_Z + D_SSM
OFF_B = OFF_XS + D_SSM
OFF_C = OFF_B + D_BC


def _ssd_consts():
    tri = jnp.tril(jnp.ones((CHUNK, CHUNK), F32)).astype(BF16)
    h = jnp.arange(HPAD)[None, :, None]
    g = jnp.arange(N_SSM_GROUPS)[:, None, None]
    j1 = jnp.arange(GW)[None, None, :]
    eh = (h == g * HEADS_PER_GROUP + j1 // HEAD_DIM).astype(BF16)
    j2 = jnp.arange(HEADS_PER_GROUP * 128)[None, None, :]
    e128 = (h == g * HEADS_PER_GROUP + j2 // 128).astype(BF16)
    return tri, eh, e128


def _ssd_in_specs(cmap):
    def rows(i):
        return cmap(i)

    def prev8(i):
        return jnp.maximum(cmap(i) * (CHUNK // 8) - 1, 0)

    gw, ns = GW, D_STATE
    specs = [
        pl.BlockSpec((CHUNK, gw), lambda i, g: (rows(i), OFF_Z // gw + g)),
        pl.BlockSpec((CHUNK, gw), lambda i, g: (rows(i), OFF_XS // gw + g)),
        pl.BlockSpec((8, gw), lambda i, g: (prev8(i), OFF_XS // gw + g)),
        pl.BlockSpec((CHUNK, ns), lambda i, g: (rows(i), OFF_B // ns + g)),
        pl.BlockSpec((8, ns), lambda i, g: (prev8(i), OFF_B // ns + g)),
        pl.BlockSpec((CHUNK, ns), lambda i, g: (rows(i), OFF_C // ns + g)),
        pl.BlockSpec((8, ns), lambda i, g: (prev8(i), OFF_C // ns + g)),
        pl.BlockSpec((CHUNK, HPAD), lambda i, g: (rows(i), 0)),
        pl.BlockSpec((SSM_CONV, gw), lambda i, g: (0, g)),
        pl.BlockSpec((SSM_CONV, ns), lambda i, g: (0, D_SSM // ns + g)),
        pl.BlockSpec((SSM_CONV, ns), lambda i, g: (0, (D_SSM + D_BC) // ns + g)),
        pl.BlockSpec((1, gw), lambda i, g: (0, g)),
        pl.BlockSpec((1, ns), lambda i, g: (0, D_SSM // ns + g)),
        pl.BlockSpec((1, ns), lambda i, g: (0, (D_SSM + D_BC) // ns + g)),
        pl.BlockSpec((1, HPAD), lambda i, g: (0, 0)),
        pl.BlockSpec((1, HPAD), lambda i, g: (0, 0)),
        pl.BlockSpec((1, gw), lambda i, g: (0, g)),
        pl.BlockSpec((1, gw), lambda i, g: (0, g)),
        pl.BlockSpec((CHUNK, CHUNK), lambda i, g: (0, 0)),
        pl.BlockSpec((None, HPAD, gw), lambda i, g: (g, 0, 0)),
        pl.BlockSpec((None, HPAD, HEADS_PER_GROUP * 128), lambda i, g: (g, 0, 0)),
    ]
    return specs


def _ssd_scratch():
    return [pltpu.VMEM((CHUNK + 8, GW), F32), pltpu.VMEM((CHUNK + 8, D_STATE), F32),
            pltpu.VMEM((CHUNK + 8, D_STATE), F32), pltpu.VMEM((HPAD, CHUNK), F32),
            pltpu.VMEM((CHUNK, GW), F32)]


def _ssd_pre(first, g, refs, scr):
    (z_ref, xs_ref, xsp_ref, b_ref, bp_ref, c_ref, cp_ref, dt_ref, cwx, cwb, cwc, cbx, cbb, cbc,
     dtb_ref, alog_ref, de_ref, gain_ref, tri_ref, eh_ref, e128_ref) = refs
    ext_x, ext_b, ext_c, acst_sc, acse_sc = scr
    p = {}
    px = _conv(xs_ref, xsp_ref, cwx, cbx, ext_x, first, SSM_CONV)
    pb = _conv(b_ref, bp_ref, cwb, cbb, ext_b, first, SSM_CONV)
    pc = _conv(c_ref, cp_ref, cwc, cbc, ext_c, first, SSM_CONV)
    p['xs'], p['dsx'] = _silu_and_grad(px)
    p['bm'], p['dsb'] = _silu_and_grad(pb)
    p['cm'], p['dsc'] = _silu_and_grad(pc)
    dt_in = dt_ref[...] + dtb_ref[...]
    p['dt_in'] = dt_in
    dt = _softplus(dt_in)
    p['dt'] = dt
    a = -jnp.exp(alog_ref[...])
    p['a'] = a
    tri_b = tri_ref[...]
    acs = _e01x(tri_b, dt * a)
    acst_sc[...] = acs.T
    eh = eh_ref[...]
    p['eh'] = eh
    p['dt_e'] = _x01(dt, eh)
    acs_e = _x01(acs, eh)
    acse_sc[...] = acs_e
    p['acs_e'] = acs_e
    p['acs_c'] = _x01(acs, e128_ref[...])
    p['acs_last_e'] = acse_sc[pl.ds(CHUNK - 1, 1), :]
    p['xdt'] = p['xs'] * p['dt_e']
    p['decay_e'] = jnp.exp(p['acs_last_e'] - acs_e)
    p['cm_b'] = p['cm'].astype(BF16)
    p['bm_b'] = p['bm'].astype(BF16)
    p['scores'] = _dot_nt(p['cm_b'], p['bm_b'])
    p['tri_b'] = tri_b
    return p


def _ssd_l(p, g, r, acst_sc, tri):
    col = p['acs_c'][:, r * 128:(r + 1) * 128]
    row = acst_sc[pl.ds(g * HEADS_PER_GROUP + r, 1), :]
    return jnp.exp(jnp.where(tri, col - row, -1e30))


def _ssd_fwd(proj, dt_raw, yab, cw, cb, dtb, alog, de, gain, consts, hosted):
    t = proj.shape[0]
    nc = t // CHUNK
    ng = N_SSM_GROUPS
    tri_c, eh_c, e128_c = consts
    n_in = 21
    nh_in, nh_out = len(hosted['arrays']), len(hosted['out_shape'])
    mid_chunk = (7 * nc) // 8

    def body(*refs):
        ins = refs[:n_in]
        hins = refs[n_in + 1:n_in + 1 + nh_in]
        o0 = n_in + 1 + nh_in
        yb_ref, ys_ref, hs_ref = refs[o0:o0 + 3]
        houts = refs[o0 + 3:o0 + 3 + nh_out]
        h_sc = refs[o0 + 3 + nh_out]
        scr = refs[o0 + 4 + nh_out:-2]
        sems = refs[-2:]
        c, g = pl.program_id(0), pl.program_id(1)
        _host_phase(hosted, 'start', jnp.logical_and(c == 0, g == 0), hins, houts, sems)
        _host_phase(hosted, 'mid', jnp.logical_and(c == mid_chunk, g == 0), hins, houts, sems)
        z_ref, de_ref, gain_ref = ins[0], ins[16], ins[17]
        slab = pl.ds(pl.multiple_of(g * D_STATE, D_STATE), D_STATE)

        @pl.when(c == 0)
        def _():
            h_sc[slab, :] = jnp.zeros((D_STATE, GW), F32)

        p = _ssd_pre(c == 0, g, ins, scr)
        tri = _tri_mask()
        h_in = h_sc[slab, :]
        hs_ref[...] = h_in
        yoff = _dot(p['cm_b'], h_in.astype(BF16)) * jnp.exp(p['acs_e'])
        states = _dot_tn(p['bm_b'], (p['xdt'] * p['decay_e']).astype(BF16))
        lane = lax.broadcasted_iota(jnp.int32, (CHUNK, 128), 1)
        slabs = []
        for r2 in range(HEADS_PER_GROUP // 2):
            xb = p['xdt'][:, r2 * 128:(r2 + 1) * 128].astype(BF16)
            ya = _dot((p['scores'] * _ssd_l(p, g, 2 * r2, scr[3], tri)).astype(BF16), xb)
            yb = _dot((p['scores'] * _ssd_l(p, g, 2 * r2 + 1, scr[3], tri)).astype(BF16), xb)
            slabs.append(jnp.where(lane < HEAD_DIM, ya, yb))
        y = jnp.concatenate(slabs, axis=1) + yoff + de_ref[...] * p['xs']
        ys_ref[...] = y
        h_sc[slab, :] = jnp.exp(p['acs_last_e']) * h_in + states
        zv = z_ref[...]
        yg = y * zv * _sigmoid(zv)
        r = lax.rsqrt(jnp.mean(yg * yg, axis=-1, keepdims=True) + EPS)
        yb_ref[...] = (yg * r * gain_ref[...]).astype(BF16)
        _host_phase(hosted, 'finish', jnp.logical_and(c == nc - 1, g == ng - 1), hins, houts, sems)

    h_in, h_out, h_shape, h_scratch, h_alias = _host_plumbing(hosted, n_in + 1, 3)
    in_specs = _ssd_in_specs(lambda i: i) + [pl.BlockSpec(memory_space=pl.ANY)] + h_in
    out_specs = [pl.BlockSpec((CHUNK, GW), lambda i, g: (i, D_A // GW + g)),
                 pl.BlockSpec((CHUNK, GW), lambda i, g: (i, g)),
                 pl.BlockSpec((None, D_STATE, GW), lambda i, g: (i, g, 0))] + h_out
    out_shape = [jax.ShapeDtypeStruct((t, D_MIX), BF16), jax.ShapeDtypeStruct((t, D_SSM), F32),
                 jax.ShapeDtypeStruct((nc, ng * D_STATE, GW), F32)] + h_shape
    return pl.pallas_call(
        body, name="ssd_fwd", grid=(nc, ng), in_specs=in_specs, out_specs=out_specs, out_shape=out_shape,
        scratch_shapes=[pltpu.VMEM((ng * D_STATE, GW), F32)] + _ssd_scratch() + h_scratch,
        input_output_aliases={n_in: 0, **h_alias},
        compiler_params=_cparams(("arbitrary", "arbitrary")),
    )(proj, proj, proj, proj, proj, proj, proj, dt_raw, cw, cw, cw, cb, cb, cb, dtb, alog, de, gain,
      tri_c, eh_c, e128_c, yab, *hosted['arrays'])


def _rows8(vals, width):
    rid = lax.broadcasted_iota(jnp.int32, (8, width), 0)
    out = jnp.zeros((8, width), F32)
    for k, v in enumerate(vals):
        if v is not None:
            out = out + jnp.where(rid == k, v, 0.0)
    return out


def _ssd_bwd(proj, dt_raw, dyab, ysave, hs, cw, cb, dtb, alog, de, gain, consts, hosted):
    t = proj.shape[0]
    nc = t // CHUNK
    ng = N_SSM_GROUPS
    tri_c, eh_c, e128_c = consts
    n_in = 21
    nh_in, nh_out = len(hosted['arrays']), len(hosted['out_shape'])

    def body(*refs):
        ins = refs[:n_in]
        dy_ref, ys_ref, hs_ref = refs[n_in:n_in + 3]
        hins = refs[n_in + 3:n_in + 3 + nh_in]
        o0 = n_in + 3 + nh_in
        (dz_ref, dxs_ref, db_ref, dc_ref, ddt_ref, acc_x, acc_b, acc_c, acc_gain,
         acc_head) = refs[o0:o0 + 10]
        houts = refs[o0 + 10:o0 + 10 + nh_out]
        s0 = o0 + 10 + nh_out
        dh_sc, car_x, car_b, car_c, dext_x, dext_b, dext_c = refs[s0:s0 + 7]
        scr = refs[s0 + 7:-2]
        sems = refs[-2:]
        ext_x, ext_b, ext_c, acst_sc, _ = scr
        cc, g = pl.program_id(0), pl.program_id(1)
        _host_phase(hosted, 'start', jnp.logical_and(cc == 0, g == 0), hins, houts, sems)
        c = nc - 1 - cc
        z_ref, cwx, cwb, cwc, de_ref, gain_ref = ins[0], ins[8], ins[9], ins[10], ins[16], ins[17]
        slab = pl.ds(pl.multiple_of(g * D_STATE, D_STATE), D_STATE)

        @pl.when(cc == 0)
        def _():
            dh_sc[slab, :] = jnp.zeros((D_STATE, GW), F32)
            car_x[g] = jnp.zeros((8, GW), F32)
            car_b[g] = jnp.zeros((8, D_STATE), F32)
            car_c[g] = jnp.zeros((8, D_STATE), F32)

        @pl.when(jnp.logical_and(cc == 0, g == 0))
        def _():
            for a in (acc_x, acc_b, acc_c, acc_gain, acc_head):
                a[...] = jnp.zeros_like(a)

        p = _ssd_pre(c == 0, g, ins, scr)
        tri = _tri_mask()
        xs, dt_e, acs_e, xdt, decay_e = p['xs'], p['dt_e'], p['acs_e'], p['xdt'], p['decay_e']
        cm_b, bm_b, scores, eh = p['cm_b'], p['bm_b'], p['scores'], p['eh']
        h_in = hs_ref[...]
        h_in_b = h_in.astype(BF16)
        e_a = jnp.exp(acs_e)
        raw = _dot(cm_b, h_in_b)

        y = ys_ref[...]
        zv = z_ref[...]
        sz, dsz = _silu_and_grad(zv)
        yg = y * sz
        r = lax.rsqrt(jnp.mean(yg * yg, axis=-1, keepdims=True) + EPS)
        xh = yg * r
        dout = dy_ref[...]
        gain = gain_ref[...]
        dxh = dout * gain
        dyg = r * (dxh - xh * jnp.mean(dxh * xh, axis=-1, keepdims=True))
        dy = dyg * sz
        dz_ref[...] = (dyg * y * dsz).astype(BF16)
        acc_gain[g] += _rows8([jnp.sum(dout * xh, axis=0, keepdims=True)], GW)
        d_skip8 = _x01_nt(_rows8([None, None, jnp.sum(dy * xs, axis=0, keepdims=True)], GW), eh)
        dxs = de_ref[...] * dy

        q = dy * raw * e_a
        draw = (dy * e_a).astype(BF16)
        d_c = _dot_nt(draw, h_in_b)
        dh_in = _dot_tn(cm_b, draw)

        lane = lax.broadcasted_iota(jnp.int32, (CHUNK, 128), 1)
        ones_b = jnp.ones((CHUNK, 128), BF16)
        dscores = jnp.zeros((CHUNK, CHUNK), F32)
        dxdt_slabs, q_slabs = [], []
        for r2 in range(HEADS_PER_GROUP // 2):
            sl = slice(r2 * 128, (r2 + 1) * 128)
            xb = xdt[:, sl].astype(BF16)
            dys = dy[:, sl]
            dys_b = dys.astype(BF16)
            dxh_pair, qv_pair = [], []
            for half in range(2):
                lmat = _ssd_l(p, g, 2 * r2 + half, acst_sc, tri)
                m = scores * lmat
                mine = (lane < HEAD_DIM) if half == 0 else (lane >= HEAD_DIM)
                dm = _dot_nt(jnp.where(mine, dys, 0.0).astype(BF16), xb)
                dscores = dscores + dm * lmat
                gm = dm * m
                dxh_pair.append(_dot_tn(m.astype(BF16), dys_b))
                h3 = _split3(gm)
                colsum = _dot_tn(h3[0], ones_b) + _dot_tn(h3[1], ones_b) + _dot_tn(h3[2], ones_b)
                qv_pair.append(jnp.sum(gm, axis=-1, keepdims=True) - colsum)
            dxdt_slabs.append(jnp.where(lane < HEAD_DIM, dxh_pair[0], dxh_pair[1]))
            q_slabs.append(jnp.where(lane == 0, qv_pair[0], 0.0) + jnp.where(lane == HEAD_DIM, qv_pair[1], 0.0))
        dxdt = jnp.concatenate(dxdt_slabs, axis=1)
        q = q + jnp.concatenate(q_slabs, axis=1)

        dh_out = dh_sc[slab, :]
        dh_out_b = dh_out.astype(BF16)
        e_l = jnp.exp(p['acs_last_e'])
        dh_sc[slab, :] = dh_in + e_l * dh_out
        dlast = jnp.sum(dh_out * h_in, axis=0, keepdims=True) * e_l
        dxd = _dot(bm_b, dh_out_b)
        xd = xdt * decay_e
        dxdt = dxdt + dxd * decay_e
        dd = dxd * xd
        q = q - dd
        dlast = dlast + jnp.sum(dd, axis=0, keepdims=True)
        d_b = _dot_nt(xd.astype(BF16), dh_out_b)
        dsc_b = dscores.astype(BF16)
        d_c = d_c + _dot(dsc_b, bm_b)
        d_b = d_b + _dot_tn(dsc_b, cm_b)

        dxs = dxs + dxdt * dt_e
        rid = lax.broadcasted_iota(jnp.int32, (CHUNK, GW), 0)
        q = q + jnp.where(rid == CHUNK - 1, dlast, 0.0)
        dacs = _x01_nt(q, eh)
        ddt = _x01_nt(dxdt * xs, eh)
        dadt = _e01x_tn(p['tri_b'], dacs)
        ddt = ddt + dadt * p['a']
        d_a = jnp.sum(dadt * p['dt'], axis=0, keepdims=True)
        ddt_raw = ddt * _sigmoid(p['dt_in'])
        acc_head[...] += _rows8([jnp.sum(ddt_raw, axis=0, keepdims=True), d_a * p['a']], HPAD) + d_skip8

        @pl.when(g == 0)
        def _():
            ddt_ref[...] = ddt_raw

        @pl.when(g > 0)
        def _():
            ddt_ref[...] += ddt_raw

        for dv, dsil, ext, dext, car, acc, w_ref, o_ref in (
                (dxs, p['dsx'], ext_x, dext_x, car_x, acc_x, cwx, dxs_ref),
                (d_b, p['dsb'], ext_b, dext_b, car_b, acc_b, cwb, db_ref),
                (d_c, p['dsc'], ext_c, dext_c, car_c, acc_c, cwc, dc_ref)):
            dp = dv * dsil
            width = dp.shape[1]
            rows = [jnp.sum(ext[pl.ds(5 + k, CHUNK), :] * dp, axis=0, keepdims=True) for k in range(SSM_CONV)]
            rows.append(jnp.sum(dp, axis=0, keepdims=True))
            acc[g] += _rows8(rows, width)
            dext[0:CHUNK, :] = dp
            dext[CHUNK:CHUNK + 8, :] = car[g]
            car[g] = dext[0:8, :]
            dx = w_ref[SSM_CONV - 1:SSM_CONV, :] * dext[pl.ds(0, CHUNK), :]
            for k in range(SSM_CONV - 1):
                dx = dx + w_ref[k:k + 1, :] * dext[pl.ds(SSM_CONV - 1 - k, CHUNK), :]
            o_ref[...] = dx.astype(BF16)
        _host_phase(hosted, 'finish', jnp.logical_and(cc == nc - 1, g == ng - 1), hins, houts, sems)

    def cmap(i):
        return nc - 1 - i

    in_specs = _ssd_in_specs(cmap) + [
        pl.BlockSpec((CHUNK, GW), lambda i, g: (cmap(i), D_A // GW + g)),
        pl.BlockSpec((CHUNK, GW), lambda i, g: (cmap(i), g)),
        pl.BlockSpec((None, D_STATE, GW), lambda i, g: (cmap(i), g, 0)),
    ]

    def full(shape):
        return pl.BlockSpec(shape, lambda i, g: (0,) * len(shape))

    out_specs = [
        pl.BlockSpec((CHUNK, GW), lambda i, g: (cmap(i), g)),
        pl.BlockSpec((CHUNK, GW), lambda i, g: (cmap(i), g)),
        pl.BlockSpec((CHUNK, D_STATE), lambda i, g: (cmap(i), g)),
        pl.BlockSpec((CHUNK, D_STATE), lambda i, g: (cmap(i), g)),
        pl.BlockSpec((CHUNK, HPAD), lambda i, g: (cmap(i), 0)),
        full((ng, 8, GW)), full((ng, 8, D_STATE)), full((ng, 8, D_STATE)),
        full((ng, 8, GW)), full((8, HPAD)),
    ]
    out_shape = [
        jax.ShapeDtypeStruct((t, D_SSM), BF16), jax.ShapeDtypeStruct((t, D_SSM), BF16),
        jax.ShapeDtypeStruct((t, D_BC), BF16), jax.ShapeDtypeStruct((t, D_BC), BF16),
        jax.ShapeDtypeStruct((t, HPAD), F32),
        jax.ShapeDtypeStruct((ng, 8, GW), F32), jax.ShapeDtypeStruct((ng, 8, D_STATE), F32),
        jax.ShapeDtypeStruct((ng, 8, D_STATE), F32), jax.ShapeDtypeStruct((ng, 8, GW), F32),
        jax.ShapeDtypeStruct((8, HPAD), F32),
    ]
    scratch = [pltpu.VMEM((ng * D_STATE, GW), F32),
               pltpu.VMEM((ng, 8, GW), F32), pltpu.VMEM((ng, 8, D_STATE), F32), pltpu.VMEM((ng, 8, D_STATE), F32),
               pltpu.VMEM((CHUNK + 8, GW), F32), pltpu.VMEM((CHUNK + 8, D_STATE), F32),
               pltpu.VMEM((CHUNK + 8, D_STATE), F32)] + _ssd_scratch()
    h_in, h_out, h_shape, h_scratch, h_alias = _host_plumbing(hosted, n_in + 3, len(out_shape))
    return pl.pallas_call(
        body, name="ssd_bwd", grid=(nc, ng), in_specs=in_specs + h_in, out_specs=out_specs + h_out,
        out_shape=out_shape + h_shape, scratch_shapes=scratch + h_scratch, input_output_aliases=h_alias,
        compiler_params=_cparams(("arbitrary", "arbitrary")),
    )(proj, proj, proj, proj, proj, proj, proj, dt_raw, cw, cw, cw, cb, cb, cb, dtb, alog, de, gain,
      tri_c, eh_c, e128_c, dyab, ysave, hs, *hosted['arrays'])


ANY = pl.BlockSpec(memory_space=pl.ANY)


def _place():
    x, y, c = lax.axis_index("x"), lax.axis_index("y"), lax.axis_index("c")
    chips = [(1 - x, y), (x, 1 - y), (1 - x, 1 - y)]
    return x, y, c, chips


def _rcopy(src, dst, send_sems, recv_sems, k, dev):
    return pltpu.make_async_remote_copy(src_ref=src, dst_ref=dst, send_sem=send_sems.at[k],
                                        recv_sem=recv_sems.at[k], device_id=dev, device_id_type=MESH)


def _my_chip():
    return 2 * lax.axis_index("x") + lax.axis_index("y")


def _cast_into_slot(w, name):
    r, c = w.shape
    tr = _row_tile(r, c, 2)

    def body(w_ref, o_ref):
        o_ref[...] = w_ref[...].astype(BF16)

    return pl.pallas_call(
        body, name=name, grid=(r // tr,), in_specs=[pl.BlockSpec((tr, c), lambda i: (i, 0))],
        out_specs=pl.BlockSpec((None, tr, c), lambda i: (_my_chip(), i, 0)),
        out_shape=jax.ShapeDtypeStruct((N_CHIPS, r, c), BF16), compiler_params=_cparams(("parallel",)),
    )(w)


def _hosted_gather(bigs):
    nb = len(bigs)

    def rows(a, c):
        half = bigs[a].shape[1] // 2
        return pl.ds(c * half, half)

    def start(ins, outs, send_sems, recv_sems):
        x, y, c, chips = _place()
        q = 2 * x + y
        for a in range(nb):
            for k, chip in enumerate(chips):
                _rcopy(outs[a].at[q, rows(a, c)], outs[a].at[q, rows(a, c)], send_sems, recv_sems, 6 * a + k,
                       (chip[0], chip[1], c)).start()

    def mid(ins, outs, send_sems, recv_sems):
        x, y, c, chips = _place()
        sib = (x, y, 1 - c)
        for a in range(nb):
            for k, chip in enumerate(chips):
                slab = outs[a].at[2 * chip[0] + chip[1], rows(a, c)]
                _rcopy(slab, slab, send_sems, recv_sems, 6 * a + k, sib).wait_recv()
                _rcopy(slab, slab, send_sems, recv_sems, 6 * a + 3 + k, sib).start()

    def finish(ins, outs, send_sems, recv_sems):
        x, y, c, chips = _place()
        q = 2 * x + y
        sib = (x, y, 1 - c)
        for a in range(nb):
            for k, chip in enumerate(chips):
                qk = 2 * chip[0] + chip[1]
                other = outs[a].at[qk, rows(a, 1 - c)]
                _rcopy(other, other, send_sems, recv_sems, 6 * a + 3 + k, sib).wait_recv()
                mine = outs[a].at[q, rows(a, c)]
                _rcopy(mine, mine, send_sems, recv_sems, 6 * a + k, sib).wait_send()
                fwd = outs[a].at[qk, rows(a, c)]
                _rcopy(fwd, fwd, send_sems, recv_sems, 6 * a + 3 + k, sib).wait_send()

    return dict(arrays=list(bigs), out_shape=[jax.ShapeDtypeStruct(b.shape, b.dtype) for b in bigs],
                aliases={a: a for a in range(nb)}, nsem=6 * nb, start=start, mid=mid, finish=finish)


def _hosted_rs_chips(ps):
    na = len(ps)

    def copies(ins, outs, send_sems, recv_sems):
        x, y, c, chips = _place()
        return [_rcopy(ins[a].at[2 * chip[0] + chip[1]], outs[a].at[k], send_sems, recv_sems, 3 * a + k,
                       (chip[0], chip[1], c)) for a in range(na) for k, chip in enumerate(chips)]

    def start(ins, outs, send_sems, recv_sems):
        for cp in copies(ins, outs, send_sems, recv_sems):
            cp.start()

    def finish(ins, outs, send_sems, recv_sems):
        for cp in copies(ins, outs, send_sems, recv_sems):
            cp.wait()

    return dict(arrays=list(ps), out_shape=[jax.ShapeDtypeStruct((3,) + p.shape[1:], p.dtype) for p in ps],
                aliases={}, nsem=3 * na, start=start, mid=None, finish=finish)


def _host_plumbing(hosted, n_in, n_out):
    nh = len(hosted['arrays'])
    return ([ANY] * nh, [ANY] * len(hosted['out_shape']), list(hosted['out_shape']),
            [pltpu.SemaphoreType.DMA((hosted['nsem'],)), pltpu.SemaphoreType.DMA((hosted['nsem'],))],
            {n_in + a: n_out + b for a, b in hosted['aliases'].items()})


def _host_phase(hosted, phase, when, hins, houts, sems):
    fn = hosted[phase]
    if fn is None:
        return

    @pl.when(when)
    def _():
        fn(hins, houts, sems[0], sems[1])


def _gather_weights(bigs, smalls):
    nb, ns = len(bigs), len(smalls)
    na = nb + ns
    nsem = 6 * nb + 3 * ns
    big = _hosted_gather(bigs)

    def body(*refs):
        ins, outs = refs[:na], refs[na:2 * na]
        send_sems, recv_sems, loc_sems = refs[2 * na:]
        x, y, c, chips = _place()
        q = 2 * x + y
        sib = (x, y, 1 - c)
        locs, sends = [], []
        for s in range(ns):
            cp = pltpu.make_async_copy(ins[nb + s], outs[nb + s].at[q], loc_sems.at[s])
            cp.start()
            locs.append(cp)
        big['start'](ins[:nb], outs[:nb], send_sems, recv_sems)
        for s in range(ns):
            a = nb + s
            for k, chip in enumerate(chips):
                cp = _rcopy(ins[a], outs[a].at[q], send_sems, recv_sems, 6 * nb + 3 * s + k,
                            (chip[0], chip[1], c))
                cp.start()
                sends.append(cp)
        big['mid'](ins[:nb], outs[:nb], send_sems, recv_sems)
        big['finish'](ins[:nb], outs[:nb], send_sems, recv_sems)
        for s in range(ns):
            a = nb + s
            for k, chip in enumerate(chips):
                qk = 2 * chip[0] + chip[1]
                _rcopy(ins[a], outs[a].at[qk], send_sems, recv_sems, 6 * nb + 3 * s + k, sib).wait_recv()
        for cp in sends:
            cp.wait_send()
        for cp in locs:
            cp.wait()

    arrs = list(bigs) + list(smalls)
    out_shape = ([jax.ShapeDtypeStruct(a.shape, a.dtype) for a in bigs]
                 + [jax.ShapeDtypeStruct((N_CHIPS,) + a.shape, a.dtype) for a in smalls])
    return pl.pallas_call(
        body, name="gather_weights", in_specs=[ANY] * na, out_specs=[ANY] * na, out_shape=out_shape,
        input_output_aliases={a: a for a in range(nb)},
        scratch_shapes=[pltpu.SemaphoreType.DMA((nsem,)), pltpu.SemaphoreType.DMA((nsem,)),
                        pltpu.SemaphoreType.DMA((max(ns, 1),))],
    )(*arrs)


def _rs_sibling(gs, name):
    na = len(gs)

    def body(*refs):
        ins, recv_o = refs[:na], refs[na:2 * na]
        send_sems, recv_sems = refs[2 * na:]
        x, y, c, _ = _place()
        cps = []
        for a in range(na):
            half = gs[a].shape[1] // 2
            snd = _rcopy(ins[a].at[:, pl.ds((1 - c) * half, half), :], recv_o[a], send_sems, recv_sems, a,
                         (x, y, 1 - c))
            snd.start()
            cps.append(snd)
        for cp in cps:
            cp.wait()

    halves = [jax.ShapeDtypeStruct((N_CHIPS, g.shape[1] // 2, g.shape[2]), g.dtype) for g in gs]
    return pl.pallas_call(
        body, name=name, in_specs=[ANY] * na, out_specs=[ANY] * na, out_shape=halves,
        scratch_shapes=[pltpu.SemaphoreType.DMA((na,)), pltpu.SemaphoreType.DMA((na,))],
    )(*gs)


def _share_sibling(fs):
    na = len(fs)

    def body(*refs):
        outs = refs[na:2 * na]
        send_sems, recv_sems = refs[2 * na:]
        x, y, c, _ = _place()
        cps = []
        for a in range(na):
            half = fs[a].shape[0] // 2
            rows = pl.ds(c * half, half)
            snd = _rcopy(outs[a].at[rows], outs[a].at[rows], send_sems, recv_sems, a, (x, y, 1 - c))
            snd.start()
            cps.append(snd)
        for a, cp in enumerate(cps):
            half = fs[a].shape[0] // 2
            other = pl.ds((1 - c) * half, half)
            cp.wait_send()
            _rcopy(outs[a].at[other], outs[a].at[other], send_sems, recv_sems, a, (x, y, 1 - c)).wait_recv()

    return pl.pallas_call(
        body, name="share_sibling", in_specs=[ANY] * na, out_specs=[ANY] * na,
        out_shape=[jax.ShapeDtypeStruct(f.shape, f.dtype) for f in fs],
        input_output_aliases={a: a for a in range(na)},
        scratch_shapes=[pltpu.SemaphoreType.DMA((na,)), pltpu.SemaphoreType.DMA((na,))],
    )(*fs)


def _allgather_small(buf):
    def body(in_ref, out_ref, send_sems, recv_sems, loc_sem):
        x, y, c, _ = _place()
        me = 4 * x + 2 * y + c
        loc = pltpu.make_async_copy(in_ref, out_ref.at[me], loc_sem)
        loc.start()
        cps = [loc]
        for k in range(1, 8):
            dev = (1 - x if k & 4 else x, 1 - y if k & 2 else y, 1 - c if k & 1 else c)
            snd = _rcopy(in_ref, out_ref.at[me], send_sems, recv_sems, k - 1, dev)
            snd.start()
            cps.append(snd)
        for cp in cps:
            cp.wait()

    return pl.pallas_call(
        body, name="allgather_small", in_specs=[ANY], out_specs=ANY,
        out_shape=jax.ShapeDtypeStruct((8,) + buf.shape, buf.dtype),
        scratch_shapes=[pltpu.SemaphoreType.DMA((7,)), pltpu.SemaphoreType.DMA((7,)), pltpu.SemaphoreType.DMA],
    )(buf)


def _row_tile(rows, cols, nbuf):
    budget = 24 * 1024 * 1024 // (nbuf * cols * 4 * 2)
    return _tile(rows, max(16, budget - budget % 16), 16) if rows % 16 == 0 else rows


def _add_pairs(g, rcv, name):
    s, half, c = rcv.shape
    tr = _row_tile(half, c, 3)
    nh = half // tr

    def body(a_ref, b_ref, o_ref):
        o_ref[...] = (a_ref[...].astype(F32) + b_ref[...].astype(F32)).astype(o_ref.dtype)

    blk = pl.BlockSpec((None, tr, c), lambda j, i: (j, i, 0))
    mine = pl.BlockSpec((None, tr, c), lambda j, i: (j, lax.axis_index("c") * nh + i, 0))
    return pl.pallas_call(
        body, name=name, grid=(s, nh), in_specs=[mine, blk], out_specs=blk,
        out_shape=jax.ShapeDtypeStruct(rcv.shape, rcv.dtype), compiler_params=_cparams(("parallel", "parallel")),
    )(g, rcv)


def _sum_chips(part, rcv, name):
    _, half, c = part.shape
    tr = _row_tile(half, c, 5)
    nh = half // tr

    def body(o_ref, r_ref, out_ref):
        acc = o_ref[...].astype(F32)
        for k in range(3):
            acc = acc + r_ref[k].astype(F32)
        out_ref[...] = acc

    return pl.pallas_call(
        body, name=name, grid=(nh,),
        in_specs=[pl.BlockSpec((None, tr, c), lambda i: (_my_chip(), i, 0)),
                  pl.BlockSpec((3, tr, c), lambda i: (0, i, 0))],
        out_specs=pl.BlockSpec((tr, c), lambda i: (lax.axis_index("c") * nh + i, 0)),
        out_shape=jax.ShapeDtypeStruct((2 * half, c), F32), compiler_params=_cparams(("parallel",)),
    )(part, rcv)


def _sum_devices(parts):
    _, n, _ = parts.shape
    tr = _tile(n, 512, 8)

    def body(p_ref, o_ref):
        acc = p_ref[0]
        for k in range(1, 8):
            acc = acc + p_ref[k]
        o_ref[...] = acc

    return pl.pallas_call(
        body, name="sum_devices", grid=(n // tr,),
        in_specs=[pl.BlockSpec((8, tr, 128), lambda i: (0, i, 0))],
        out_specs=pl.BlockSpec((tr, 128), lambda i: (i, 0)),
        out_shape=jax.ShapeDtypeStruct((n, 128), F32), compiler_params=_cparams(("parallel",)),
    )(parts)


def _adamw(w, g, m, v, name):
    r, c = w.shape
    tr = _row_tile(r, c, 7)
    c1 = 1.0 - ADAM_B1 ** ADAM_STEP
    c2 = 1.0 - ADAM_B2 ** ADAM_STEP

    def body(w_ref, g_ref, m_ref, v_ref, d_ref, mo_ref, vo_ref):
        gv = g_ref[...]
        mn = ADAM_B1 * m_ref[...] + (1.0 - ADAM_B1) * gv
        vn = ADAM_B2 * v_ref[...] + (1.0 - ADAM_B2) * (gv * gv)
        mo_ref[...] = mn
        vo_ref[...] = vn
        m_hat = mn / c1
        v_hat = vn / c2
        d_ref[...] = -ADAM_LR * (m_hat / (jnp.sqrt(v_hat) + ADAM_EPS) + ADAM_WD * w_ref[...])

    blk = pl.BlockSpec((tr, c), lambda i: (i, 0))
    sh = jax.ShapeDtypeStruct((r, c), F32)
    return pl.pallas_call(
        body, name=name, grid=(r // tr,), in_specs=[blk] * 4, out_specs=[blk] * 3, out_shape=[sh] * 3,
        compiler_params=_cparams(("parallel",)),
    )(w, g, m, v)


WEIGHTS = ['norm_mix_g', 'w_in', 'ln_a_g', 'ln_a_b', 'w_s', 'b_s', 'norm_a_g', 'conv_ssm_w', 'conv_ssm_b',
           'dt_bias', 'a_log', 'd_skip', 'ssm_norm_g', 'w_out', 'norm_ffn_g', 'w_up', 'conv_ffn_w',
           'conv_ffn_b', 'w_down', 'norm_ple_g', 'w_ple_gate', 'w_ple', 'norm_final_g']
BIG = ['w_in', 'w_out', 'w_up', 'w_down', 'w_ple_gate', 'w_ple']
SMALL = [n for n in WEIGHTS if n not in BIG]
PACK_ALIGN = 2048


def _pack(arrs):
    parts = []
    for a in arrs:
        f = a.reshape(-1).astype(F32)
        parts.append(jnp.pad(f, (0, (-f.shape[0]) % PACK_ALIGN)))
    return jnp.concatenate(parts).reshape(-1, 128)


def _unpack(buf, shapes):
    flat = buf.reshape(-1)
    out, off = [], 0
    for s in shapes:
        n = math.prod(s)
        out.append(flat[off:off + n].reshape(s))
        off += n + (-n) % PACK_ALIGN
    return out


def _pad_heads(v):
    return jnp.pad(v, ((0, 0), (0, HPAD - v.shape[1])))


def _col_sharded(full):
    r, c4 = full.shape
    return jnp.transpose(full.reshape(r, N_CHIPS, c4 // N_CHIPS), (1, 0, 2))


def _from_col_sharded(g):
    s, r, c = g.shape
    return jnp.transpose(g, (1, 0, 2)).reshape(r, s * c)


def kernel(x, p, norm_mix_g, w_in, ln_a_g, ln_a_b, w_s, b_s, norm_a_g, conv_ssm_w, conv_ssm_b, dt_bias, a_log, d_skip, ssm_norm_g, w_out, norm_ffn_g, w_up, conv_ffn_w, conv_ffn_b, w_down, norm_ple_g, w_ple_gate, w_ple, norm_final_g, loss_target, m_norm_mix_g, m_w_in, m_ln_a_g, m_ln_a_b, m_w_s, m_b_s, m_norm_a_g, m_conv_ssm_w, m_conv_ssm_b, m_dt_bias, m_a_log, m_d_skip, m_ssm_norm_g, m_w_out, m_norm_ffn_g, m_w_up, m_conv_ffn_w, m_conv_ffn_b, m_w_down, m_norm_ple_g, m_w_ple_gate, m_w_ple, m_norm_final_g, v_norm_mix_g, v_w_in, v_ln_a_g, v_ln_a_b, v_w_s, v_b_s, v_norm_a_g, v_conv_ssm_w, v_conv_ssm_b, v_dt_bias, v_a_log, v_d_skip, v_ssm_norm_g, v_w_out, v_norm_ffn_g, v_w_up, v_conv_ffn_w, v_conv_ffn_b, v_w_down, v_norm_ple_g, v_w_ple_gate, v_w_ple, v_norm_final_g):
    given = dict(locals())
    wts = {n: given[n] for n in WEIGHTS}
    mom = {n: given['m_' + n] for n in WEIGHTS}
    var = {n: given['v_' + n] for n in WEIGHTS}
    d = D_MODEL
    xt, pt, tgt = x[0], p[0, 0], loss_target[0]
    chip = 2 * lax.axis_index("x") + lax.axis_index("y")

    slots = {n: _cast_into_slot(wts[n][0], "cast_" + n) for n in BIG}
    g_in, g_cs, g_cf = _gather_weights([slots['w_in']], [conv_ssm_w[0], conv_ffn_w[0]])
    later = [n for n in BIG if n != 'w_in']
    w_in_full = _from_col_sharded(g_in)
    w_main = w_in_full[:, :D_MAIN]
    w_dt = _pad_heads(w_in_full[:, D_MAIN:])
    cs_w = _from_col_sharded(g_cs)
    cf_w = _from_col_sharded(g_cf)
    consts = _ssd_consts()
    dtb, alog = _pad_heads(dt_bias), _pad_heads(a_log)
    de = jnp.repeat(d_skip[0], HEAD_DIM)[None, :]
    b_exp = jnp.broadcast_to(b_s[0][:, :, None], (N_GROUPS_A, CHUNK, CHUNK))

    a1 = _rms_fwd(xt, norm_mix_g, "rms_mix")
    proj = _matmul(a1, w_main, mode='nn', name="mm_proj", tm=512, tn=1024)
    dt_raw = _matmul(a1, w_dt, mode='nn', name="mm_dt", tm=1024, tn=128)
    yab = _gmlp_fwd(proj, ln_a_g, ln_a_b, w_s[0], b_exp, norm_a_g)
    yab, ysave, hs, *gathered = _ssd_fwd(proj, dt_raw, yab, cs_w, conv_ssm_b, dtb, alog, de, ssm_norm_g, consts,
                                         _hosted_gather([slots[n] for n in later]))
    g_out, g_up, g_down, g_pg, g_ple = gathered
    w_out_f = g_out.reshape(D_MIX, d)
    w_down_f = g_down.reshape(D_FF, d)
    w_pg_f = g_pg.reshape(d, d)
    h1 = _matmul(yab, w_out_f, mode='nn', name="mm_out", res=xt, tm=512, tn=1024, tk=4096)
    f = _rms_fwd(h1, norm_ffn_g, "rms_ffn")
    hid = _matmul(f, g_up, mode='nn', name="mm_up", b_sharded=True, tm=512, tn=1408)
    act = _ffn_act_fwd(hid, cf_w, conv_ffn_b)
    h2 = _matmul(act, w_down_f, mode='nn', name="mm_down", res=h1, tm=512, tn=1024, tk=2816)
    n3 = _rms_fwd(h2, norm_ple_g, "rms_ple")
    gl = _matmul(n3, w_pg_f, mode='nn', name="mm_pg", tm=512, tn=1024)
    pe = _matmul(pt, g_ple, mode='nn', name="mm_ple", b_sharded=True, tm=1024, tn=512)
    dh3, dgl, dpe, lossv, dgf = _tail(h2, gl, pe, tgt, norm_final_g[None, :])

    gs_ple = _matmul(pt, dpe, mode='tn', name="mm_dw_ple", out_dtype=BF16, out_shards=N_CHIPS,
                     tm=256, tn=512, tk=2048)
    gs_pg = _matmul(n3, dgl, mode='tn', name="mm_dw_pg", out_dtype=BF16, tm=1024, tn=1024, tk=2048)
    dn3 = _matmul(dgl, w_pg_f, mode='nt', name="mm_dn3", tm=512, tn=1024)
    dh2, dg_ple = _rms_bwd(h2, norm_ple_g, dn3, dh3, "rms_ple_bwd")
    dact = _matmul(dh2, w_down_f, mode='nt', name="mm_dact", out_dtype=BF16, tm=512, tn=1408)
    gs_down = _matmul(act, dh2, mode='tn', name="mm_dw_down", out_dtype=BF16, tm=1408, tn=1024, tk=2048)
    dpg, dpu, wg_acc, wu_acc = _ffn_act_bwd(hid, cf_w, conv_ffn_b, dact)
    hc = N_CHIPS // 2
    gs_up = jnp.concatenate(
        [_matmul(f, dpg, mode='tn', name="mm_dw_up_g", out_dtype=BF16, out_shards=hc, tm=1024, tn=1408, tk=2048),
         _matmul(f, dpu, mode='tn', name="mm_dw_up_u", out_dtype=BF16, out_shards=hc, tm=1024, tn=1408, tk=2048)],
        axis=0)
    early = [gs_up, gs_down.reshape(N_CHIPS, D_FF // N_CHIPS, d), gs_pg.reshape(N_CHIPS, d // N_CHIPS, d), gs_ple]
    part_e = [_add_pairs(a, b, "rs_add_e%d" % i)
              for i, (a, b) in enumerate(zip(early, _rs_sibling(early, "rs_sibling_early")))]
    df = _matmul(dpg, g_up[:hc], mode='nt', name="mm_df_g", b_sharded=True, tm=1024, tn=1024, tk=1408)
    df = _matmul(dpu, g_up[hc:], mode='nt', name="mm_df_u", b_sharded=True, res=df, tm=1024, tn=1024, tk=1408)
    dh1, dg_ffn = _rms_bwd(h1, norm_ffn_g, df, dh2, "rms_ffn_bwd")
    dyab = _matmul(dh1, w_out_f, mode='nt', name="mm_dyab", tm=512, tn=1024)
    gs_out = _matmul(yab, dh1, mode='tn', name="mm_dw_out", out_dtype=BF16, tm=1024, tn=1024, tk=2048)
    duv, dws, dbs, dlng, dlnb, dnag = _gmlp_bwd(proj, dyab, ln_a_g, ln_a_b, w_s[0], b_exp, norm_a_g)
    dz, dxs, dbm, dcm, ddt_raw, acc_x, acc_b, acc_c, acc_gain, acc_head, *rcv_e = _ssd_bwd(
        proj, dt_raw, dyab, ysave, hs, cs_w, conv_ssm_b, dtb, alog, de, ssm_norm_g, consts,
        _hosted_rs_chips(part_e))
    dproj = jnp.concatenate([duv, dz, dxs, dbm, dcm], axis=1)
    dw_main = _matmul(a1, dproj, mode='tn', name="mm_dw_main", out_dtype=BF16, tm=1024, tn=1024, tk=2048)
    dw_dt = _matmul(a1, ddt_raw, mode='tn', name="mm_dw_dt", out_dtype=BF16, tm=1024, tn=128, tk=2048)
    gs_in = _col_sharded(jnp.concatenate([dw_main, dw_dt[:, :N_HEADS]], axis=1))
    late = [gs_in, gs_out.reshape(N_CHIPS, D_MIX // N_CHIPS, d)]
    part_l = [_add_pairs(a, b, "rs_add_l%d" % i)
              for i, (a, b) in enumerate(zip(late, _rs_sibling(late, "rs_sibling_late")))]
    da_dt = _matmul(ddt_raw, w_dt, mode='nt', name="mm_da_dt", tm=1024, tn=1024)
    da, *rcv_l = _matmul(dproj, w_main, mode='nt', name="mm_da", res=da_dt, tm=1024, tn=1024, tk=2048,
                         hosted=_hosted_rs_chips(part_l))
    dx, dg_mix = _rms_bwd(xt, norm_mix_g, da, dh1, "rms_mix_bwd")

    order = ['w_up', 'w_down', 'w_ple_gate', 'w_ple', 'w_in', 'w_out']
    halves = [_sum_chips(a, b, "rs_sum_%d" % i)
              for i, (a, b) in enumerate(zip(part_e + part_l, list(rcv_e) + list(rcv_l)))]
    g_big = dict(zip(order, _share_sibling(halves)))

    def conv_rows(acc, k):
        return acc[:, k, :].reshape(1, -1)

    dcw = jnp.concatenate([jnp.concatenate([conv_rows(acc_x, k), conv_rows(acc_b, k), conv_rows(acc_c, k)], axis=1)
                           for k in range(SSM_CONV)], axis=0)
    dcb = jnp.concatenate([conv_rows(acc_x, SSM_CONV), conv_rows(acc_b, SSM_CONV), conv_rows(acc_c, SSM_CONV)], axis=1)
    part = {
        'norm_mix_g': dg_mix, 'ln_a_g': dlng, 'ln_a_b': dlnb, 'w_s': dws, 'b_s': dbs, 'norm_a_g': dnag,
        'conv_ssm_w': dcw, 'conv_ssm_b': dcb,
        'dt_bias': acc_head[0:1, :N_HEADS], 'a_log': acc_head[1:2, :N_HEADS], 'd_skip': acc_head[2:3, :N_HEADS],
        'ssm_norm_g': acc_gain[:, 0, :], 'norm_ffn_g': dg_ffn,
        'conv_ffn_w': jnp.concatenate([wg_acc[:FFN_CONV], wu_acc[:FFN_CONV]], axis=1),
        'conv_ffn_b': jnp.concatenate([wg_acc[FFN_CONV:FFN_CONV + 1], wu_acc[FFN_CONV:FFN_CONV + 1]], axis=1),
        'norm_ple_g': dg_ple, 'norm_final_g': dgf,
    }
    full_shapes = {n: wts[n].shape for n in SMALL}
    full_shapes['conv_ssm_w'] = (1, SSM_CONV, D_XBC)
    full_shapes['conv_ffn_w'] = (1, FFN_CONV, 2 * D_FF)
    packed = _pack([part[n] for n in SMALL] + [lossv[:, 0:1]])
    total = _sum_devices(_allgather_small(packed))
    pieces = _unpack(total, [full_shapes[n] for n in SMALL] + [(1,)])
    g_small = dict(zip(SMALL, pieces[:-1]))
    loss = pieces[-1][0]
    for n in ('conv_ssm_w', 'conv_ffn_w'):
        width = wts[n].shape[2]
        g_small[n] = lax.dynamic_slice_in_dim(g_small[n], chip * width, width, axis=2)

    grads, delta, new_m, new_v = {}, {}, {}, {}
    for n in BIG:
        shp = wts[n].shape
        dl, mn, vn = _adamw(wts[n][0], g_big[n], mom[n][0], var[n][0], "adamw_" + n)
        grads[n], delta[n], new_m[n], new_v[n] = (g_big[n].reshape(shp), dl.reshape(shp), mn.reshape(shp),
                                                  vn.reshape(shp))
    shapes = [wts[n].shape for n in SMALL]
    dl, mn, vn = _adamw(_pack([wts[n] for n in SMALL]), _pack([g_small[n] for n in SMALL]),
                        _pack([mom[n] for n in SMALL]), _pack([var[n] for n in SMALL]), "adamw_small")
    for n, a, b, c in zip(SMALL, _unpack(dl, shapes), _unpack(mn, shapes), _unpack(vn, shapes)):
        grads[n], delta[n], new_m[n], new_v[n] = g_small[n], a, b, c

    return (loss, dx[None], *[grads[n] for n in WEIGHTS], *[delta[n] for n in WEIGHTS],
            *[new_m[n] for n in WEIGHTS], *[new_v[n] for n in WEIGHTS])
```

```python
import functools
import math

import jax
import jax.numpy as jnp
from jax import lax
from jax.experimental import pallas as pl
from jax.experimental.pallas import tpu as pltpu

D_MODEL = 2048
SEQ = 8192
D_MIX = 2 * D_MODEL
D_A = D_MIX // 2
CHUNK = 128
N_GROUPS_A = D_A // 128
D_SSM = D_MIX - D_A
HEAD_DIM = 64
N_HEADS = D_SSM // HEAD_DIM
HEADS_PER_GROUP = 4
N_SSM_GROUPS = N_HEADS // HEADS_PER_GROUP
GW = HEADS_PER_GROUP * HEAD_DIM
D_STATE = 128
SSM_CONV = 4
D_BC = N_SSM_GROUPS * D_STATE
D_XBC = D_SSM + 2 * D_BC
D_MAIN = 2 * D_A + D_SSM + D_XBC
D_IN = D_MAIN + N_HEADS
D_FF = (D_MODEL * 11) // 4
FFN_CONV = 3
D_PLE = 256
EPS = 1e-6
HPAD = 128
N_CHIPS = 4

ADAM_LR = 0.001
ADAM_B1 = 0.9
ADAM_B2 = 0.999
ADAM_EPS = 1e-08
ADAM_WD = 0.01
ADAM_STEP = 10

F32 = jnp.float32
BF16 = jnp.bfloat16
MESH = pl.DeviceIdType.MESH
VMEM_LIMIT = 56 * 1024 * 1024


def _cparams(sem):
    return pltpu.CompilerParams(dimension_semantics=sem, vmem_limit_bytes=VMEM_LIMIT)


def _tile(n, pref, mult):
    t = min(pref, n)
    t -= t % mult
    while n % t:
        t -= mult
    return t


def _dot(a, b):
    return jnp.dot(a, b, preferred_element_type=F32)


def _dot_nt(a, b):
    return lax.dot_general(a, b, (((1,), (1,)), ((), ())), preferred_element_type=F32)


def _dot_tn(a, b):
    return lax.dot_general(a, b, (((0,), (0,)), ((), ())), preferred_element_type=F32)


def _split3(x):
    hi = x.astype(BF16)
    r = x - hi.astype(F32)
    mid = r.astype(BF16)
    lo = (r - mid.astype(F32)).astype(BF16)
    return hi, mid, lo


def _x01(x, e):
    h, m, l = _split3(x)
    return _dot(h, e) + _dot(m, e) + _dot(l, e)


def _x01_nt(x, e):
    h, m, l = _split3(x)
    return _dot_nt(h, e) + _dot_nt(m, e) + _dot_nt(l, e)


def _e01x(e, x):
    h, m, l = _split3(x)
    return _dot(e, h) + _dot(e, m) + _dot(e, l)


def _e01x_tn(e, x):
    h, m, l = _split3(x)
    return _dot_tn(e, h) + _dot_tn(e, m) + _dot_tn(e, l)


def _sigmoid(x):
    return 1.0 / (1.0 + jnp.exp(-x))


_GELU_C = math.sqrt(2.0 / math.pi)


def _gelu_and_grad(x):
    x2 = x * x
    th = jnp.tanh(_GELU_C * (x + 0.044715 * x * x2))
    y = 0.5 * x * (1.0 + th)
    dy = 0.5 * (1.0 + th) + 0.5 * x * (1.0 - th * th) * (_GELU_C * (1.0 + 3.0 * 0.044715 * x2))
    return y, dy


def _silu_and_grad(x):
    s = _sigmoid(x)
    return x * s, s * (1.0 + x * (1.0 - s))


def _softplus(x):
    u = jnp.exp(-jnp.abs(x))
    w = 1.0 + u
    l1p = jnp.where(w == 1.0, u, jnp.log(w) * (u / (w - 1.0)))
    return jnp.maximum(x, 0.0) + l1p


def _matmul(a, b, *, mode, name, out_dtype=F32, res=None, tm=512, tn=512, tk=2048,
            b_sharded=False, b_shard_off=0, out_shards=0, hosted=None):
    if mode == 'tn':
        kdim, m = a.shape
        n = b.shape[1]
    else:
        m, kdim = a.shape
        if b_sharded:
            s_b, d1, d2 = b.shape
            n = s_b * d2 if mode == 'nn' else d1
        else:
            n = b.shape[1] if mode == 'nn' else b.shape[0]
    per = None
    if b_sharded:
        per = b.shape[2]
    if out_shards:
        per = n // out_shards
    tm = _tile(m, tm, 128 if mode == 'tn' else 8)
    if mode == 'nt' and b_sharded:
        tn = _tile(n, tn, 128)
        tk = _tile(per, tk, 128)
    elif per is not None:
        tn = _tile(per, tn, 128)
        tk = _tile(kdim, tk, 128 if mode != 'tn' else 8)
    else:
        tn = _tile(n, tn, 128)
        tk = _tile(kdim, tk, 128 if mode != 'tn' else 8)
    nm, nn_, nk = m // tm, n // tn, kdim // tk
    has_res = res is not None
    n_in = 2 + has_res
    nh_in = len(hosted['arrays']) if hosted else 0
    nh_out = len(hosted['out_shape']) if hosted else 0

    def body(*refs):
        a_ref, b_ref = refs[0], refs[1]
        res_ref = refs[2] if has_res else None
        o_ref = refs[n_in + nh_in]
        if hosted:
            hins = refs[n_in:n_in + nh_in]
            houts = refs[n_in + nh_in + 1:n_in + nh_in + 1 + nh_out]
            sems = refs[-2:]
            ids = [pl.program_id(d) for d in range(3)]
            at_first = jnp.logical_and(jnp.logical_and(ids[0] == 0, ids[1] == 0), ids[2] == 0)
            at_last = jnp.logical_and(jnp.logical_and(ids[0] == nn_ - 1, ids[1] == nm - 1), ids[2] == nk - 1)
            _host_phase(hosted, 'start', at_first, hins, houts, sems)
        av = a_ref[...].astype(BF16)
        bv = b_ref[...].astype(BF16)
        if mode == 'nn':
            p = _dot(av, bv)
        elif mode == 'nt':
            p = _dot_nt(av, bv)
        else:
            p = _dot_tn(av, bv)

        def fin(v):
            if has_res:
                v = v + res_ref[...]
            o_ref[...] = v.astype(o_ref.dtype)

        if nk == 1:
            fin(p)
        else:
            acc_ref = refs[n_in + nh_in + 1 + nh_out]
            k = pl.program_id(2)

            @pl.when(k == 0)
            def _():
                acc_ref[...] = p

            @pl.when(k > 0)
            def _():
                acc_ref[...] += p

            @pl.when(k == nk - 1)
            def _():
                fin(acc_ref[...])
        if hosted:
            _host_phase(hosted, 'mid', at_last, hins, houts, sems)
            _host_phase(hosted, 'finish', at_last, hins, houts, sems)

    if mode == 'nn':
        a_spec = pl.BlockSpec((tm, tk), lambda j, i, k: (i, k))
        if b_sharded:
            nps = per // tn
            b_spec = pl.BlockSpec((None, tk, tn), lambda j, i, k: (j // nps, k, j % nps))
        else:
            b_spec = pl.BlockSpec((tk, tn), lambda j, i, k: (k, j))
    elif mode == 'nt':
        a_spec = pl.BlockSpec((tm, tk), lambda j, i, k: (i, k))
        if b_sharded:
            kps = per // tk
            b_spec = pl.BlockSpec((None, tn, tk), lambda j, i, k: (k // kps + b_shard_off, j, k % kps))
        else:
            b_spec = pl.BlockSpec((tn, tk), lambda j, i, k: (j, k))
    else:
        a_spec = pl.BlockSpec((tk, tm), lambda j, i, k: (k, i))
        b_spec = pl.BlockSpec((tk, tn), lambda j, i, k: (k, j))
    in_specs = [a_spec, b_spec]
    args = [a, b]
    if has_res:
        in_specs.append(pl.BlockSpec((tm, tn), lambda j, i, k: (i, j)))
        args.append(res)
    if out_shards:
        nps_o = per // tn
        out_shape = jax.ShapeDtypeStruct((out_shards, m, per), out_dtype)
        out_spec = pl.BlockSpec((None, tm, tn), lambda j, i, k: (j // nps_o, i, j % nps_o))
    else:
        out_shape = jax.ShapeDtypeStruct((m, n), out_dtype)
        out_spec = pl.BlockSpec((tm, tn), lambda j, i, k: (i, j))
    scratch = [pltpu.VMEM((tm, tn), F32)] if nk > 1 else []
    if not hosted:
        return pl.pallas_call(
            body, name=name, grid=(nn_, nm, nk), in_specs=in_specs, out_specs=out_spec,
            out_shape=out_shape, scratch_shapes=scratch,
            compiler_params=_cparams(("parallel", "parallel", "arbitrary")),
        )(*args)
    h_in, h_out, h_shape, h_scratch, h_alias = _host_plumbing(hosted, n_in, 1)
    return pl.pallas_call(
        body, name=name, grid=(nn_, nm, nk), in_specs=in_specs + h_in, out_specs=[out_spec] + h_out,
        out_shape=[out_shape] + h_shape, scratch_shapes=scratch + h_scratch, input_output_aliases=h_alias,
        compiler_params=_cparams(("arbitrary", "arbitrary", "arbitrary")),
    )(*args, *hosted['arrays'])


def _rms_fwd(x, g, name):
    t, d = x.shape
    tt = _tile(t, 512, 8)

    def body(x_ref, g_ref, o_ref):
        xv = x_ref[...]
        r = lax.rsqrt(jnp.mean(xv * xv, axis=-1, keepdims=True) + EPS)
        o_ref[...] = (xv * r * g_ref[...]).astype(o_ref.dtype)

    return pl.pallas_call(
        body, name=name, grid=(t // tt,),
        in_specs=[pl.BlockSpec((tt, d), lambda i: (i, 0)), pl.BlockSpec((1, d), lambda i: (0, 0))],
        out_specs=pl.BlockSpec((tt, d), lambda i: (i, 0)),
        out_shape=jax.ShapeDtypeStruct((t, d), BF16),
        compiler_params=_cparams(("parallel",)),
    )(x, g)


def _rms_bwd(x, g, dy, dres, name, also_bf16):
    t, d = x.shape
    tt = _tile(t, 256, 16)

    def body(x_ref, g_ref, dy_ref, dres_ref, dx_ref, dg_ref, *dxb):
        i = pl.program_id(0)
        xv = x_ref[...]
        r = lax.rsqrt(jnp.mean(xv * xv, axis=-1, keepdims=True) + EPS)
        xh = xv * r
        dyv = dy_ref[...].astype(F32)
        dxh = dyv * g_ref[...]
        c = jnp.mean(dxh * xh, axis=-1, keepdims=True)
        dx = dres_ref[...] + r * (dxh - xh * c)
        dx_ref[...] = dx
        for dxb_ref in dxb:
            dxb_ref[...] = dx.astype(BF16)
        part = jnp.sum(dyv * xh, axis=0, keepdims=True)

        @pl.when(i == 0)
        def _():
            dg_ref[...] = part

        @pl.when(i > 0)
        def _():
            dg_ref[...] += part

    row = pl.BlockSpec((tt, d), lambda i: (i, 0))
    vec = pl.BlockSpec((1, d), lambda i: (0, 0))
    return pl.pallas_call(
        body, name=name, grid=(t // tt,),
        in_specs=[row, vec, row, row], out_specs=[row, vec] + [row] * also_bf16,
        out_shape=[jax.ShapeDtypeStruct((t, d), F32), jax.ShapeDtypeStruct((1, d), F32)]
        + [jax.ShapeDtypeStruct((t, d), BF16)] * also_bf16,
        compiler_params=_cparams(("arbitrary",)),
    )(x, g, dy, dres)


def _tail(h2, gl, pe, target, gfin):
    t, d = h2.shape
    tt = _tile(t, 256, 8)

    def body(h2_ref, gl_ref, pe_ref, tg_ref, gf_ref, dh3_ref, dgl_ref, dpe_ref, loss_ref, dgf_ref):
        i = pl.program_id(0)
        sig = _sigmoid(gl_ref[...])
        pev = pe_ref[...]
        h3 = h2_ref[...] + sig * pev
        r = lax.rsqrt(jnp.mean(h3 * h3, axis=-1, keepdims=True) + EPS)
        xh = h3 * r
        gf = gf_ref[...]
        e = xh * gf - tg_ref[...]
        dy = e * (1.0 / d)
        dxh = dy * gf
        c = jnp.mean(dxh * xh, axis=-1, keepdims=True)
        dh3 = r * (dxh - xh * c)
        dh3_ref[...] = dh3
        dgl_ref[...] = (dh3 * pev * sig * (1.0 - sig)).astype(BF16)
        dpe_ref[...] = (dh3 * sig).astype(BF16)
        lpart = jnp.sum(e * e, axis=0, keepdims=True) * (0.5 / d)
        gpart = jnp.sum(dy * xh, axis=0, keepdims=True)

        @pl.when(i == 0)
        def _():
            loss_ref[...] = lpart
            dgf_ref[...] = gpart

        @pl.when(i > 0)
        def _():
            loss_ref[...] += lpart
            dgf_ref[...] += gpart

        @pl.when(i == t // tt - 1)
        def _():
            loss_ref[...] = jnp.broadcast_to(jnp.sum(loss_ref[...], axis=-1, keepdims=True), (1, d))

    row = pl.BlockSpec((tt, d), lambda i: (i, 0))
    vec = pl.BlockSpec((1, d), lambda i: (0, 0))
    return pl.pallas_call(
        body, name="tail", grid=(t // tt,),
        in_specs=[row, row, row, row, vec], out_specs=[row, row, row, vec, vec],
        out_shape=[jax.ShapeDtypeStruct((t, d), F32), jax.ShapeDtypeStruct((t, d), BF16),
                   jax.ShapeDtypeStruct((t, d), BF16), jax.ShapeDtypeStruct((1, d), F32),
                   jax.ShapeDtypeStruct((1, d), F32)],
        compiler_params=_cparams(("arbitrary",)),
    )(h2, gl, pe, target, gfin)


def _conv(cur_ref, prev_ref, w_ref, b_ref, ext_ref, first, width):
    rows = cur_ref.shape[0]
    ext_ref[0:8, :] = jnp.where(first, 0.0, prev_ref[...])
    ext_ref[8:8 + rows, :] = cur_ref[...]
    acc = b_ref[...]
    for k in range(width):
        acc = acc + w_ref[k:k + 1, :] * ext_ref[pl.ds(9 - width + k, rows), :]
    return acc


def _ffn_specs(t, tt, tc, nf):
    hb = tt // 8
    cur_g = pl.BlockSpec((tt, tc), lambda j, i: (i, j))
    cur_u = pl.BlockSpec((tt, tc), lambda j, i: (i, j + nf))
    prev_g = pl.BlockSpec((8, tc), lambda j, i: (jnp.maximum(i * hb - 1, 0), j))
    prev_u = pl.BlockSpec((8, tc), lambda j, i: (jnp.maximum(i * hb - 1, 0), j + nf))
    w_g = pl.BlockSpec((FFN_CONV, tc), lambda j, i: (0, j))
    w_u = pl.BlockSpec((FFN_CONV, tc), lambda j, i: (0, j + nf))
    b_g = pl.BlockSpec((1, tc), lambda j, i: (0, j))
    b_u = pl.BlockSpec((1, tc), lambda j, i: (0, j + nf))
    return [cur_g, prev_g, cur_u, prev_u, w_g, w_u, b_g, b_u]


FFN_TC = 512


def _shift_down(prev, cur, n, rid):
    return jnp.where(rid < n, pltpu.roll(prev, n, 0), pltpu.roll(cur, n, 0))


def _shift_up(cur, nxt, n, rid):
    return jnp.where(rid < 8 - n, pltpu.roll(cur, 8 - n, 0), pltpu.roll(nxt, 8 - n, 0))


def _conv3_group(prev, cur, w_ref, b_ref, rid):
    x1 = _shift_down(prev, cur, 1, rid)
    x2 = _shift_down(prev, cur, 2, rid)
    return b_ref[...] + w_ref[2:3, :] * cur + w_ref[1:2, :] * x1 + w_ref[0:1, :] * x2


def _ffn_act_fwd(hid, cw, cb):
    t = hid.shape[0]
    tt = _tile(t, 512, 16)
    tc = _tile(D_FF, FFN_TC, 128)
    nf = D_FF // tc

    def body(g_ref, gp_ref, u_ref, up_ref, wg_ref, wu_ref, bg_ref, bu_ref, o_ref):
        first = pl.program_id(1) == 0
        rid = lax.broadcasted_iota(jnp.int32, (8, tc), 0)

        def act(pg, cg, pu, cu):
            gate = _conv3_group(pg, cg, wg_ref, bg_ref, rid)
            up = _conv3_group(pu, cu, wu_ref, bu_ref, rid)
            return gate * _sigmoid(gate) * up

        def step(s, carry):
            pg, pu = carry
            r0 = pl.multiple_of(s * 16, 16)
            g0, g1 = g_ref[pl.ds(r0, 8), :], g_ref[pl.ds(r0 + 8, 8), :]
            u0, u1 = u_ref[pl.ds(r0, 8), :], u_ref[pl.ds(r0 + 8, 8), :]
            out = jnp.concatenate([act(pg, g0, pu, u0), act(g0, g1, u0, u1)], axis=0)
            o_ref[pl.ds(r0, 16), :] = out.astype(BF16)
            return g1, u1

        init = (jnp.where(first, 0.0, gp_ref[...]), jnp.where(first, 0.0, up_ref[...]))
        lax.fori_loop(0, tt // 16, step, init)

    return pl.pallas_call(
        body, name="ffn_act_fwd", grid=(nf, t // tt), in_specs=_ffn_specs(t, tt, tc, nf),
        out_specs=pl.BlockSpec((tt, tc), lambda j, i: (i, j)),
        out_shape=jax.ShapeDtypeStruct((t, D_FF), BF16),
        compiler_params=_cparams(("parallel", "arbitrary")),
    )(hid, hid, hid, hid, cw, cw, cb, cb)


def _ffn_act_bwd(hid, cw, cb, dact):
    t = hid.shape[0]
    tt = _tile(t, 512, 16)
    tc = _tile(D_FF, FFN_TC, 128)
    nf = D_FF // tc
    nt = t // tt
    n16 = tt // 16
    hb = tt // 8

    def body(g_ref, gp_ref, u_ref, up_ref, wg_ref, wu_ref, bg_ref, bu_ref, gn_ref, un_ref, da_ref, dan_ref,
             og_ref, ou_ref, ag_ref, au_ref, accs):
        i = pl.program_id(1)
        first, last = i == 0, i == nt - 1
        rid = lax.broadcasted_iota(jnp.int32, (8, tc), 0)
        accs[...] = jnp.zeros_like(accs)

        def dgroup(pg, cg, pu, cu, da):
            gate = _conv3_group(pg, cg, wg_ref, bg_ref, rid)
            up = _conv3_group(pu, cu, wu_ref, bu_ref, rid)
            sv, sgr = _silu_and_grad(gate)
            return da * up * sgr, da * sv

        def finish(x, d0, d1, w_ref):
            s1 = _shift_up(d0, d1, 1, rid)
            s2 = _shift_up(d0, d1, 2, rid)
            dpre = w_ref[2:3, :] * d0 + w_ref[1:2, :] * s1 + w_ref[0:1, :] * s2
            return dpre, (x * s2, x * s1, x * d0, d0)

        def two_groups(it, carry, x2g, x2u, da_blk, zero_ahead):
            x0g, x0u, d0g, d0u, da1 = carry
            r0 = it * 16 if isinstance(it, int) else pl.multiple_of(it * 16, 16)
            x1g, x1u = g_ref[pl.ds(r0 + 8, 8), :], u_ref[pl.ds(r0 + 8, 8), :]
            d1g, d1u = dgroup(x0g, x1g, x0u, x1u, da1)
            d2g, d2u = dgroup(x1g, x2g, x1u, x2u, da_blk[0:8])
            d2g = jnp.where(zero_ahead, 0.0, d2g)
            d2u = jnp.where(zero_ahead, 0.0, d2u)
            outs = []
            for half, (xa, xb, da_, db_, dc_, w_ref, o_ref) in enumerate((
                    (x0g, x1g, d0g, d1g, d2g, wg_ref, og_ref), (x0u, x1u, d0u, d1u, d2u, wu_ref, ou_ref))):
                pa, prods_a = finish(xa, da_, db_, w_ref)
                pb, prods_b = finish(xb, db_, dc_, w_ref)
                o_ref[pl.ds(r0, 16), :] = jnp.concatenate([pa, pb], axis=0).astype(BF16)
                for k in range(4):
                    accs[4 * half + k] += prods_a[k] + prods_b[k]
            return x2g, x2u, d2g, d2u, da_blk[8:16]

        def step(it, carry):
            r1 = pl.multiple_of(it * 16 + 16, 16)
            return two_groups(it, carry, g_ref[pl.ds(r1, 8), :], u_ref[pl.ds(r1, 8), :],
                              da_ref[pl.ds(r1, 16), :].astype(F32), False)

        da0 = da_ref[pl.ds(0, 16), :].astype(F32)
        x0g, x0u = g_ref[pl.ds(0, 8), :], u_ref[pl.ds(0, 8), :]
        d0g, d0u = dgroup(jnp.where(first, 0.0, gp_ref[...]), x0g, jnp.where(first, 0.0, up_ref[...]), x0u, da0[0:8])
        carry = lax.fori_loop(0, n16 - 1, step, (x0g, x0u, d0g, d0u, da0[8:16]))
        two_groups(n16 - 1, carry, gn_ref[...], un_ref[...], dan_ref[...].astype(F32), last)

        @pl.when(first)
        def _():
            ag_ref[...] = jnp.zeros_like(ag_ref)
            au_ref[...] = jnp.zeros_like(au_ref)

        for half, a_ref in enumerate((ag_ref, au_ref)):
            for k in range(4):
                a_ref[k:k + 1, :] += jnp.sum(accs[4 * half + k], axis=0, keepdims=True)

    def nxt8(i):
        return jnp.minimum((i + 1) * hb, t // 8 - 1)

    def nxt16(i):
        return jnp.minimum((i + 1) * n16, t // 16 - 1)

    out_blk = pl.BlockSpec((tt, tc), lambda j, i: (i, j))
    acc_spec = pl.BlockSpec((8, tc), lambda j, i: (0, j))
    in_specs = _ffn_specs(t, tt, tc, nf) + [
        pl.BlockSpec((8, tc), lambda j, i: (nxt8(i), j)), pl.BlockSpec((8, tc), lambda j, i: (nxt8(i), j + nf)),
        out_blk, pl.BlockSpec((16, tc), lambda j, i: (nxt16(i), j))]
    return pl.pallas_call(
        body, name="ffn_act_bwd", grid=(nf, nt), in_specs=in_specs,
        out_specs=[out_blk, out_blk, acc_spec, acc_spec],
        out_shape=[jax.ShapeDtypeStruct((t, D_FF), BF16), jax.ShapeDtypeStruct((t, D_FF), BF16),
                   jax.ShapeDtypeStruct((8, D_FF), F32), jax.ShapeDtypeStruct((8, D_FF), F32)],
        scratch_shapes=[pltpu.VMEM((8, 8, tc), F32)],
        compiler_params=_cparams(("parallel", "arbitrary")),
    )(hid, hid, hid, hid, cw, cw, cb, cb, hid, hid, dact, dact)


def _tri_mask():
    r = lax.broadcasted_iota(jnp.int32, (CHUNK, CHUNK), 0)
    c = lax.broadcasted_iota(jnp.int32, (CHUNK, CHUNK), 1)
    return r >= c


def _gmlp_group_fwd(uv_ref, lng_ref, lnb_ref, ws_ref, bexp_ref, tri, g, want_grad):
    lo, hi = g * 128, (g + 1) * 128
    u_pre = uv_ref[:, lo:hi]
    v_pre = uv_ref[:, D_A + lo:D_A + hi]
    u, du = _gelu_and_grad(u_pre)
    v, dv = _gelu_and_grad(v_pre)
    mu = jnp.mean(v, axis=-1, keepdims=True)
    dc = v - mu
    rs = lax.rsqrt(jnp.mean(dc * dc, axis=-1, keepdims=True) + EPS)
    xh = dc * rs
    vn = (xh * lng_ref[:, lo:hi] + lnb_ref[:, lo:hi]).astype(BF16)
    w = jnp.where(tri, ws_ref[g], 0.0).astype(BF16)
    sg = _dot(w, vn) + bexp_ref[g]
    if want_grad:
        return u, du, dv, rs, xh, vn, w, sg
    return u * sg


def _gmlp_fwd(proj, ln_g, ln_b, w_s, b_exp, na_g):
    t = proj.shape[0]
    ng = N_GROUPS_A

    def body(uv_ref, lng_ref, lnb_ref, ws_ref, bexp_ref, nag_ref, o_ref):
        tri = _tri_mask()
        ys = [_gmlp_group_fwd(uv_ref, lng_ref, lnb_ref, ws_ref, bexp_ref, tri, g, False) for g in range(ng)]
        ssq = ys[0] * 0.0
        for y in ys:
            ssq = ssq + y * y
        r = lax.rsqrt(jnp.sum(ssq, axis=-1, keepdims=True) * (1.0 / D_A) + EPS)
        for g, y in enumerate(ys):
            o_ref[:, g * 128:(g + 1) * 128] = (y * r * nag_ref[:, g * 128:(g + 1) * 128]).astype(BF16)

    vec = pl.BlockSpec((1, D_A), lambda i: (0, 0))
    cube = pl.BlockSpec((ng, CHUNK, CHUNK), lambda i: (0, 0, 0))
    return pl.pallas_call(
        body, name="gmlp_fwd", grid=(t // CHUNK,),
        in_specs=[pl.BlockSpec((CHUNK, 2 * D_A), lambda i: (i, 0)), vec, vec, cube, cube, vec],
        out_specs=pl.BlockSpec((CHUNK, D_A), lambda i: (i, 0)),
        out_shape=jax.ShapeDtypeStruct((t, D_MIX), BF16),
        compiler_params=_cparams(("parallel",)),
    )(proj, ln_g, ln_b, w_s, b_exp, na_g)


def _gmlp_bwd(proj, dyab, ln_g, ln_b, w_s, b_exp, na_g):
    t = proj.shape[0]
    ng = N_GROUPS_A
    nsteps = t // CHUNK

    def body(uv_ref, dy_ref, lng_ref, lnb_ref, ws_ref, bexp_ref, nag_ref,
             duv_ref, dws_ref, dbs_ref, dlng_ref, dlnb_ref, dnag_ref, dbacc):
        i = pl.program_id(0)
        tri = _tri_mask()

        @pl.when(i == 0)
        def _():
            dws_ref[...] = jnp.zeros_like(dws_ref)
            dbacc[...] = jnp.zeros_like(dbacc)
            dlng_ref[...] = jnp.zeros_like(dlng_ref)
            dlnb_ref[...] = jnp.zeros_like(dlnb_ref)
            dnag_ref[...] = jnp.zeros_like(dnag_ref)

        st = [_gmlp_group_fwd(uv_ref, lng_ref, lnb_ref, ws_ref, bexp_ref, tri, g, True) for g in range(ng)]
        ssq = st[0][0] * 0.0
        for s in st:
            y = s[0] * s[7]
            ssq = ssq + y * y
        r = lax.rsqrt(jnp.sum(ssq, axis=-1, keepdims=True) * (1.0 / D_A) + EPS)
        csum = st[0][0] * 0.0
        for g, s in enumerate(st):
            sl = slice(g * 128, (g + 1) * 128)
            xhy = s[0] * s[7] * r
            dya = dy_ref[:, sl]
            dnag_ref[:, sl] += jnp.sum(dya * xhy, axis=0, keepdims=True)
            csum = csum + dya * nag_ref[:, sl] * xhy
        c1 = jnp.sum(csum, axis=-1, keepdims=True) * (1.0 / D_A)
        for g, s in enumerate(st):
            u, du, dv, rs, xh, vn, w, sg = s
            sl = slice(g * 128, (g + 1) * 128)
            dy = r * (dy_ref[:, sl] * nag_ref[:, sl] - u * sg * r * c1)
            dsg = dy * u
            dsg_b = dsg.astype(BF16)
            dws_ref[g] += _dot_nt(dsg_b, vn)
            dbacc[g] += dsg
            dvn = _dot_tn(w, dsg_b)
            dlnb_ref[:, sl] += jnp.sum(dvn, axis=0, keepdims=True)
            dlng_ref[:, sl] += jnp.sum(dvn * xh, axis=0, keepdims=True)
            dxh = dvn * lng_ref[:, sl]
            dvv = rs * (dxh - jnp.mean(dxh, axis=-1, keepdims=True)
                        - xh * jnp.mean(dxh * xh, axis=-1, keepdims=True))
            duv_ref[:, sl] = (dy * sg * du).astype(BF16)
            duv_ref[:, D_A + g * 128:D_A + (g + 1) * 128] = (dvv * dv).astype(BF16)

        @pl.when(i == nsteps - 1)
        def _():
            for g in range(ng):
                dws_ref[g] = jnp.where(tri, dws_ref[g], 0.0)
                dbs_ref[g] = jnp.sum(dbacc[g], axis=-1, keepdims=True)

    vec = pl.BlockSpec((1, D_A), lambda i: (0, 0))
    cube = pl.BlockSpec((ng, CHUNK, CHUNK), lambda i: (0, 0, 0))
    return pl.pallas_call(
        body, name="gmlp_bwd", grid=(nsteps,),
        in_specs=[pl.BlockSpec((CHUNK, 2 * D_A), lambda i: (i, 0)),
                  pl.BlockSpec((CHUNK, D_A), lambda i: (i, 0)), vec, vec, cube, cube, vec],
        out_specs=[pl.BlockSpec((CHUNK, 2 * D_A), lambda i: (i, 0)), cube,
                   pl.BlockSpec((ng, CHUNK, 1), lambda i: (0, 0, 0)), vec, vec, vec],
        out_shape=[jax.ShapeDtypeStruct((t, 2 * D_A), BF16), jax.ShapeDtypeStruct((ng, CHUNK, CHUNK), F32),
                   jax.ShapeDtypeStruct((ng, CHUNK, 1), F32), jax.ShapeDtypeStruct((1, D_A), F32),
                   jax.ShapeDtypeStruct((1, D_A), F32), jax.ShapeDtypeStruct((1, D_A), F32)],
        scratch_shapes=[pltpu.VMEM((ng, CHUNK, CHUNK), F32)],
        compiler_params=_cparams(("arbitrary",)),
    )(proj, dyab, ln_g, ln_b, w_s, b_exp, na_g)


OFF_Z = 2 * D_A
OFF_XS = OFF_Z + D_SSM
OFF_B = OFF_XS + D_SSM
OFF_C = OFF_B + D_BC


def _ssd_consts():
    tri = jnp.tril(jnp.ones((CHUNK, CHUNK), F32)).astype(BF16)
    h = jnp.arange(HPAD)[None, :, None]
    g = jnp.arange(N_SSM_GROUPS)[:, None, None]
    j1 = jnp.arange(GW)[None, None, :]
    eh = (h == g * HEADS_PER_GROUP + j1 // HEAD_DIM).astype(BF16)
    j2 = jnp.arange(HEADS_PER_GROUP * 128)[None, None, :]
    e128 = (h == g * HEADS_PER_GROUP + j2 // 128).astype(BF16)
    return tri, eh, e128


def _ssd_in_specs(cmap):
    def rows(i):
        return cmap(i)

    def prev8(i):
        return jnp.maximum(cmap(i) * (CHUNK // 8) - 1, 0)

    def whole(*shape):
        return pl.BlockSpec(shape, lambda i: (0,) * len(shape))

    bcw = 2 * D_BC
    specs = [
        pl.BlockSpec((CHUNK, D_SSM), lambda i: (rows(i), OFF_Z // D_SSM)),
        pl.BlockSpec((CHUNK, D_SSM), lambda i: (rows(i), OFF_XS // D_SSM)),
        pl.BlockSpec((8, D_SSM), lambda i: (prev8(i), OFF_XS // D_SSM)),
        pl.BlockSpec((CHUNK, bcw), lambda i: (rows(i), OFF_B // bcw)),
        pl.BlockSpec((8, bcw), lambda i: (prev8(i), OFF_B // bcw)),
        pl.BlockSpec((CHUNK, HPAD), lambda i: (rows(i), 0)),
        whole(SSM_CONV, D_XBC), whole(1, D_XBC),
        whole(1, HPAD), whole(1, HPAD),
        whole(1, D_SSM), whole(1, D_SSM),
        whole(CHUNK, CHUNK),
        whole(N_SSM_GROUPS, HPAD, GW), whole(N_SSM_GROUPS, HPAD, HEADS_PER_GROUP * 128),
    ]
    return specs


N_SSD_IN = 15


def _lanes(ref, start, width):
    return ref.at[:, pl.ds(pl.multiple_of(start, 128), width)]


def _ssd_group_refs(ins, g):
    (z_ref, xs_ref, xsp_ref, bc_ref, bcp_ref, dt_ref, cw_ref, cb_ref, dtb_ref, alog_ref, de_ref, gain_ref,
     tri_ref, eh_ref, e128_ref) = ins
    ox, ob, oc = g * GW, g * D_STATE, D_BC + g * D_STATE
    return (_lanes(z_ref, ox, GW), _lanes(xs_ref, ox, GW), _lanes(xsp_ref, ox, GW),
            _lanes(bc_ref, ob, D_STATE), _lanes(bcp_ref, ob, D_STATE),
            _lanes(bc_ref, oc, D_STATE), _lanes(bcp_ref, oc, D_STATE), dt_ref,
            _lanes(cw_ref, ox, GW), _lanes(cw_ref, D_SSM + ob, D_STATE), _lanes(cw_ref, D_SSM + oc, D_STATE),
            _lanes(cb_ref, ox, GW), _lanes(cb_ref, D_SSM + ob, D_STATE), _lanes(cb_ref, D_SSM + oc, D_STATE),
            dtb_ref, alog_ref, _lanes(de_ref, ox, GW), _lanes(gain_ref, ox, GW), tri_ref,
            eh_ref.at[g], e128_ref.at[g])


def _ssd_scratch():
    return [pltpu.VMEM((CHUNK + 8, GW), F32), pltpu.VMEM((CHUNK + 8, D_STATE), F32),
            pltpu.VMEM((CHUNK + 8, D_STATE), F32), pltpu.VMEM((HPAD, CHUNK), F32),
            pltpu.VMEM((CHUNK, GW), F32), pltpu.VMEM((CHUNK, HPAD), F32), pltpu.VMEM((CHUNK, HPAD), F32)]


def _ssd_chunk(ins, scr):
    dt_ref, dtb_ref, alog_ref, tri_ref = ins[5], ins[8], ins[9], ins[12]
    acst_sc, dt_sc, acs_sc = scr[3], scr[5], scr[6]
    dt = _softplus(dt_ref[...] + dtb_ref[...])
    acs = _e01x(tri_ref[...], dt * (-jnp.exp(alog_ref[...])))
    dt_sc[...] = dt
    acs_sc[...] = acs
    acst_sc[...] = acs.T


def _ssd_pre(first, g, refs, scr):
    (z_ref, xs_ref, xsp_ref, b_ref, bp_ref, c_ref, cp_ref, dt_ref, cwx, cwb, cwc, cbx, cbb, cbc,
     dtb_ref, alog_ref, de_ref, gain_ref, tri_ref, eh_ref, e128_ref) = refs
    ext_x, ext_b, ext_c, acst_sc, acse_sc, dt_sc, acs_sc = scr
    p = {}
    px = _conv(xs_ref, xsp_ref, cwx, cbx, ext_x, first, SSM_CONV)
    pb = _conv(b_ref, bp_ref, cwb, cbb, ext_b, first, SSM_CONV)
    pc = _conv(c_ref, cp_ref, cwc, cbc, ext_c, first, SSM_CONV)
    p['xs'], p['dsx'] = _silu_and_grad(px)
    p['bm'], p['dsb'] = _silu_and_grad(pb)
    p['cm'], p['dsc'] = _silu_and_grad(pc)
    p['dt_in'] = dt_ref[...] + dtb_ref[...]
    dt = dt_sc[...]
    p['dt'] = dt
    p['a'] = -jnp.exp(alog_ref[...])
    tri_b = tri_ref[...]
    acs = acs_sc[...]
    eh = eh_ref[...]
    p['eh'] = eh
    p['dt_e'] = _x01(dt, eh)
    acs_e = _x01(acs, eh)
    acse_sc[...] = acs_e
    p['acs_e'] = acs_e
    p['acs_c'] = _x01(acs, e128_ref[...])
    p['acs_last_e'] = acse_sc[pl.ds(CHUNK - 1, 1), :]
    p['xdt'] = p['xs'] * p['dt_e']
    p['decay_e'] = jnp.exp(p['acs_last_e'] - acs_e)
    p['cm_b'] = p['cm'].astype(BF16)
    p['bm_b'] = p['bm'].astype(BF16)
    p['scores'] = _dot_nt(p['cm_b'], p['bm_b'])
    p['tri_b'] = tri_b
    return p


def _ssd_l(p, g, r, acst_sc, tri):
    col = p['acs_c'][:, r * 128:(r + 1) * 128]
    row = acst_sc[pl.ds(g * HEADS_PER_GROUP + r, 1), :]
    return jnp.exp(jnp.where(tri, col - row, -1e30))


def _ssd_fwd(proj, dt_raw, yab, cw, cb, dtb, alog, de, gain, consts, hosted):
    t = proj.shape[0]
    nc = t // CHUNK
    ng = N_SSM_GROUPS
    tri_c, eh_c, e128_c = consts
    n_in = N_SSD_IN
    nh_in, nh_out = len(hosted['arrays']), len(hosted['out_shape'])
    mid_chunk = (7 * nc) // 8

    def body(*refs):
        ins = refs[:n_in]
        hins = refs[n_in + 1:n_in + 1 + nh_in]
        o0 = n_in + 1 + nh_in
        yb_all, ys_all, hs_all = refs[o0:o0 + 3]
        houts = refs[o0 + 3:o0 + 3 + nh_out]
        h_sc = refs[o0 + 3 + nh_out]
        scr = refs[o0 + 4 + nh_out:-2]
        sems = refs[-2:]
        c = pl.program_id(0)
        _host_phase(hosted, 'start', c == 0, hins, houts, sems)
        _host_phase(hosted, 'mid', c == mid_chunk, hins, houts, sems)

        @pl.when(c == 0)
        def _():
            h_sc[...] = jnp.zeros_like(h_sc)

        def group(g, carry):
            grefs = _ssd_group_refs(ins, g)
            z_ref, de_ref, gain_ref = grefs[0], grefs[16], grefs[17]
            yb_ref, ys_ref = _lanes(yb_all, g * GW, GW), _lanes(ys_all, g * GW, GW)
            slab = pl.ds(pl.multiple_of(g * D_STATE, D_STATE), D_STATE)
            p = _ssd_pre(c == 0, g, grefs, scr)
            tri = _tri_mask()
            h_in = h_sc[slab, :]
            hs_all[slab, :] = h_in
            yoff = _dot(p['cm_b'], h_in.astype(BF16)) * jnp.exp(p['acs_e'])
            states = _dot_tn(p['bm_b'], (p['xdt'] * p['decay_e']).astype(BF16))
            lane = lax.broadcasted_iota(jnp.int32, (CHUNK, 128), 1)
            slabs = []
            for r2 in range(HEADS_PER_GROUP // 2):
                xb = p['xdt'][:, r2 * 128:(r2 + 1) * 128].astype(BF16)
                ya = _dot((p['scores'] * _ssd_l(p, g, 2 * r2, scr[3], tri)).astype(BF16), xb)
                yb = _dot((p['scores'] * _ssd_l(p, g, 2 * r2 + 1, scr[3], tri)).astype(BF16), xb)
                slabs.append(jnp.where(lane < HEAD_DIM, ya, yb))
            y = jnp.concatenate(slabs, axis=1) + yoff + de_ref[...] * p['xs']
            ys_ref[...] = y
            h_sc[slab, :] = jnp.exp(p['acs_last_e']) * h_in + states
            zv = z_ref[...]
            yg = y * zv * _sigmoid(zv)
            r = lax.rsqrt(jnp.mean(yg * yg, axis=-1, keepdims=True) + EPS)
            yb_ref[...] = (yg * r * gain_ref[...]).astype(BF16)
            return carry

        _ssd_chunk(ins, scr)
        lax.fori_loop(0, ng, group, 0)
        _host_phase(hosted, 'finish', c == nc - 1, hins, houts, sems)

    h_in, h_out, h_shape, h_scratch, h_alias = _host_plumbing(hosted, n_in + 1, 3)
    in_specs = _ssd_in_specs(lambda i: i) + [pl.BlockSpec(memory_space=pl.ANY)] + h_in
    out_specs = [pl.BlockSpec((CHUNK, D_SSM), lambda i: (i, D_A // D_SSM)),
                 pl.BlockSpec((CHUNK, D_SSM), lambda i: (i, 0)),
                 pl.BlockSpec((None, ng * D_STATE, GW), lambda i: (i, 0, 0))] + h_out
    out_shape = [jax.ShapeDtypeStruct((t, D_MIX), BF16), jax.ShapeDtypeStruct((t, D_SSM), F32),
                 jax.ShapeDtypeStruct((nc, ng * D_STATE, GW), F32)] + h_shape
    return pl.pallas_call(
        body, name="ssd_fwd", grid=(nc,), in_specs=in_specs, out_specs=out_specs, out_shape=out_shape,
        scratch_shapes=[pltpu.VMEM((ng * D_STATE, GW), F32)] + _ssd_scratch() + h_scratch,
        input_output_aliases={n_in: 0, **h_alias},
        compiler_params=_cparams(("arbitrary",)),
    )(proj, proj, proj, proj, proj, dt_raw, cw, cb, dtb, alog, de, gain,
      tri_c, eh_c, e128_c, yab, *hosted['arrays'])


def _rows8(vals, width):
    rid = lax.broadcasted_iota(jnp.int32, (8, width), 0)
    out = jnp.zeros((8, width), F32)
    for k, v in enumerate(vals):
        if v is not None:
            out = out + jnp.where(rid == k, v, 0.0)
    return out


def _ssd_bwd(proj, dt_raw, dyab, ysave, hs, cw, cb, dtb, alog, de, gain, consts, hosted):
    t = proj.shape[0]
    nc = t // CHUNK
    ng = N_SSM_GROUPS
    tri_c, eh_c, e128_c = consts
    n_in = N_SSD_IN
    nh_in, nh_out = len(hosted['arrays']), len(hosted['out_shape'])

    def per_group(g, c, ins, dy_ref, ys_ref, hs_ref, outs, scratch):
        dz_ref, dxs_ref, db_ref, dc_ref, ddt_ref, acc_x, acc_b, acc_c, acc_gain, acc_head = outs
        dh_sc, car_x, car_b, car_c, dext_x, dext_b, dext_c = scratch[:7]
        scr = scratch[7:]
        ext_x, ext_b, ext_c, acst_sc = scr[:4]
        z_ref, cwx, cwb, cwc, de_ref, gain_ref = ins[0], ins[8], ins[9], ins[10], ins[16], ins[17]
        slab = pl.ds(pl.multiple_of(g * D_STATE, D_STATE), D_STATE)
        p = _ssd_pre(c == 0, g, ins, scr)
        tri = _tri_mask()
        xs, dt_e, acs_e, xdt, decay_e = p['xs'], p['dt_e'], p['acs_e'], p['xdt'], p['decay_e']
        cm_b, bm_b, scores, eh = p['cm_b'], p['bm_b'], p['scores'], p['eh']
        h_in = hs_ref[...]
        h_in_b = h_in.astype(BF16)
        e_a = jnp.exp(acs_e)
        raw = _dot(cm_b, h_in_b)

        y = ys_ref[...]
        zv = z_ref[...]
        sz, dsz = _silu_and_grad(zv)
        yg = y * sz
        r = lax.rsqrt(jnp.mean(yg * yg, axis=-1, keepdims=True) + EPS)
        xh = yg * r
        dout = dy_ref[...]
        gain = gain_ref[...]
        dxh = dout * gain
        dyg = r * (dxh - xh * jnp.mean(dxh * xh, axis=-1, keepdims=True))
        dy = dyg * sz
        dz_ref[...] = (dyg * y * dsz).astype(BF16)
        acc_gain[g] += _rows8([jnp.sum(dout * xh, axis=0, keepdims=True)], GW)
        d_skip8 = _x01_nt(_rows8([None, None, jnp.sum(dy * xs, axis=0, keepdims=True)], GW), eh)
        dxs = de_ref[...] * dy

        q = dy * raw * e_a
        draw = (dy * e_a).astype(BF16)
        d_c = _dot_nt(draw, h_in_b)
        dh_in = _dot_tn(cm_b, draw)

        lane = lax.broadcasted_iota(jnp.int32, (CHUNK, 128), 1)
        ones_b = jnp.ones((CHUNK, 128), BF16)
        dscores = jnp.zeros((CHUNK, CHUNK), F32)
        dxdt_slabs, q_slabs = [], []
        for r2 in range(HEADS_PER_GROUP // 2):
            sl = slice(r2 * 128, (r2 + 1) * 128)
            xb = xdt[:, sl].astype(BF16)
            dys = dy[:, sl]
            dys_b = dys.astype(BF16)
            dxh_pair, qv_pair = [], []
            for half in range(2):
                lmat = _ssd_l(p, g, 2 * r2 + half, acst_sc, tri)
                m = scores * lmat
                mine = (lane < HEAD_DIM) if half == 0 else (lane >= HEAD_DIM)
                dm = _dot_nt(jnp.where(mine, dys, 0.0).astype(BF16), xb)
                dscores = dscores + dm * lmat
                gm = dm * m
                dxh_pair.append(_dot_tn(m.astype(BF16), dys_b))
                h3 = _split3(gm)
                colsum = _dot_tn(h3[0], ones_b) + _dot_tn(h3[1], ones_b) + _dot_tn(h3[2], ones_b)
                qv_pair.append(jnp.sum(gm, axis=-1, keepdims=True) - colsum)
            dxdt_slabs.append(jnp.where(lane < HEAD_DIM, dxh_pair[0], dxh_pair[1]))
            q_slabs.append(jnp.where(lane == 0, qv_pair[0], 0.0) + jnp.where(lane == HEAD_DIM, qv_pair[1], 0.0))
        dxdt = jnp.concatenate(dxdt_slabs, axis=1)
        q = q + jnp.concatenate(q_slabs, axis=1)

        dh_out = dh_sc[slab, :]
        dh_out_b = dh_out.astype(BF16)
        e_l = jnp.exp(p['acs_last_e'])
        dh_sc[slab, :] = dh_in + e_l * dh_out
        dlast = jnp.sum(dh_out * h_in, axis=0, keepdims=True) * e_l
        dxd = _dot(bm_b, dh_out_b)
        xd = xdt * decay_e
        dxdt = dxdt + dxd * decay_e
        dd = dxd * xd
        q = q - dd
        dlast = dlast + jnp.sum(dd, axis=0, keepdims=True)
        d_b = _dot_nt(xd.astype(BF16), dh_out_b)
        dsc_b = dscores.astype(BF16)
        d_c = d_c + _dot(dsc_b, bm_b)
        d_b = d_b + _dot_tn(dsc_b, cm_b)

        dxs = dxs + dxdt * dt_e
        rid = lax.broadcasted_iota(jnp.int32, (CHUNK, GW), 0)
        q = q + jnp.where(rid == CHUNK - 1, dlast, 0.0)
        dacs = _x01_nt(q, eh)
        ddt = _x01_nt(dxdt * xs, eh)
        dadt = _e01x_tn(p['tri_b'], dacs)
        ddt = ddt + dadt * p['a']
        d_a = jnp.sum(dadt * p['dt'], axis=0, keepdims=True)
        ddt_raw = ddt * _sigmoid(p['dt_in'])
        acc_head[...] += _rows8([jnp.sum(ddt_raw, axis=0, keepdims=True), d_a * p['a']], HPAD) + d_skip8

        @pl.when(g == 0)
        def _():
            ddt_ref[...] = ddt_raw

        @pl.when(g > 0)
        def _():
            ddt_ref[...] += ddt_raw

        for dv, dsil, ext, dext, car, acc, w_ref, o_ref in (
                (dxs, p['dsx'], ext_x, dext_x, car_x, acc_x, cwx, dxs_ref),
                (d_b, p['dsb'], ext_b, dext_b, car_b, acc_b, cwb, db_ref),
                (d_c, p['dsc'], ext_c, dext_c, car_c, acc_c, cwc, dc_ref)):
            dp = dv * dsil
            width = dp.shape[1]
            rows = [jnp.sum(ext[pl.ds(5 + k, CHUNK), :] * dp, axis=0, keepdims=True) for k in range(SSM_CONV)]
            rows.append(jnp.sum(dp, axis=0, keepdims=True))
            acc[g] += _rows8(rows, width)
            dext[0:CHUNK, :] = dp
            dext[CHUNK:CHUNK + 8, :] = car[g]
            car[g] = dext[0:8, :]
            dx = w_ref[SSM_CONV - 1:SSM_CONV, :] * dext[pl.ds(0, CHUNK), :]
            for k in range(SSM_CONV - 1):
                dx = dx + w_ref[k:k + 1, :] * dext[pl.ds(SSM_CONV - 1 - k, CHUNK), :]
            o_ref[...] = dx.astype(BF16)

    def body(*refs):
        ins = refs[:n_in]
        dy_all, ys_all, hs_all = refs[n_in:n_in + 3]
        hins = refs[n_in + 3:n_in + 3 + nh_in]
        o0 = n_in + 3 + nh_in
        dz_all, dxs_all, db_all, dc_all, ddt_ref = refs[o0:o0 + 5]
        accs = refs[o0 + 5:o0 + 10]
        houts = refs[o0 + 10:o0 + 10 + nh_out]
        scratch = refs[o0 + 10 + nh_out:-2]
        sems = refs[-2:]
        cc = pl.program_id(0)
        _host_phase(hosted, 'start', cc == 0, hins, houts, sems)

        @pl.when(cc == 0)
        def _():
            for a in tuple(accs) + tuple(scratch[:4]):
                a[...] = jnp.zeros_like(a)

        def group(g, carry):
            slab = pl.ds(pl.multiple_of(g * D_STATE, D_STATE), D_STATE)
            outs = (_lanes(dz_all, g * GW, GW), _lanes(dxs_all, g * GW, GW), _lanes(db_all, g * D_STATE, D_STATE),
                    _lanes(dc_all, g * D_STATE, D_STATE), ddt_ref) + tuple(accs)
            per_group(g, nc - 1 - cc, _ssd_group_refs(ins, g), _lanes(dy_all, g * GW, GW),
                      _lanes(ys_all, g * GW, GW), hs_all.at[slab, :], outs, scratch)
            return carry

        _ssd_chunk(ins, scratch[7:])
        lax.fori_loop(0, ng, group, 0)
        _host_phase(hosted, 'finish', cc == nc - 1, hins, houts, sems)

    def cmap(i):
        return nc - 1 - i

    in_specs = _ssd_in_specs(cmap) + [
        pl.BlockSpec((CHUNK, D_SSM), lambda i: (cmap(i), D_A // D_SSM)),
        pl.BlockSpec((CHUNK, D_SSM), lambda i: (cmap(i), 0)),
        pl.BlockSpec((None, ng * D_STATE, GW), lambda i: (cmap(i), 0, 0)),
    ]

    def full(shape):
        return pl.BlockSpec(shape, lambda i: (0,) * len(shape))

    out_specs = [
        pl.BlockSpec((CHUNK, D_SSM), lambda i: (cmap(i), 0)),
        pl.BlockSpec((CHUNK, D_SSM), lambda i: (cmap(i), 0)),
        pl.BlockSpec((CHUNK, D_BC), lambda i: (cmap(i), 0)),
        pl.BlockSpec((CHUNK, D_BC), lambda i: (cmap(i), 0)),
        pl.BlockSpec((CHUNK, HPAD), lambda i: (cmap(i), 0)),
        full((ng, 8, GW)), full((ng, 8, D_STATE)), full((ng, 8, D_STATE)),
        full((ng, 8, GW)), full((8, HPAD)),
    ]
    out_shape = [
        jax.ShapeDtypeStruct((t, D_SSM), BF16), jax.ShapeDtypeStruct((t, D_SSM), BF16),
        jax.ShapeDtypeStruct((t, D_BC), BF16), jax.ShapeDtypeStruct((t, D_BC), BF16),
        jax.ShapeDtypeStruct((t, HPAD), F32),
        jax.ShapeDtypeStruct((ng, 8, GW), F32), jax.ShapeDtypeStruct((ng, 8, D_STATE), F32),
        jax.ShapeDtypeStruct((ng, 8, D_STATE), F32), jax.ShapeDtypeStruct((ng, 8, GW), F32),
        jax.ShapeDtypeStruct((8, HPAD), F32),
    ]
    scratch = [pltpu.VMEM((ng * D_STATE, GW), F32),
               pltpu.VMEM((ng, 8, GW), F32), pltpu.VMEM((ng, 8, D_STATE), F32), pltpu.VMEM((ng, 8, D_STATE), F32),
               pltpu.VMEM((CHUNK + 8, GW), F32), pltpu.VMEM((CHUNK + 8, D_STATE), F32),
               pltpu.VMEM((CHUNK + 8, D_STATE), F32)] + _ssd_scratch()
    h_in, h_out, h_shape, h_scratch, h_alias = _host_plumbing(hosted, n_in + 3, len(out_shape))
    return pl.pallas_call(
        body, name="ssd_bwd", grid=(nc,), in_specs=in_specs + h_in, out_specs=out_specs + h_out,
        out_shape=out_shape + h_shape, scratch_shapes=scratch + h_scratch, input_output_aliases=h_alias,
        compiler_params=_cparams(("arbitrary",)),
    )(proj, proj, proj, proj, proj, dt_raw, cw, cb, dtb, alog, de, gain,
      tri_c, eh_c, e128_c, dyab, ysave, hs, *hosted['arrays'])


ANY = pl.BlockSpec(memory_space=pl.ANY)


def _place():
    x, y, c = lax.axis_index("x"), lax.axis_index("y"), lax.axis_index("c")
    chips = [(1 - x, y), (x, 1 - y), (1 - x, 1 - y)]
    return x, y, c, chips


def _rcopy(src, dst, send_sems, recv_sems, k, dev):
    return pltpu.make_async_remote_copy(src_ref=src, dst_ref=dst, send_sem=send_sems.at[k],
                                        recv_sem=recv_sems.at[k], device_id=dev, device_id_type=MESH)


def _my_chip():
    return 2 * lax.axis_index("x") + lax.axis_index("y")


def _cast_into_slot(w, name):
    r, c = w.shape
    tr = _row_tile(r, c, 2)

    def body(w_ref, o_ref):
        o_ref[...] = w_ref[...].astype(BF16)

    return pl.pallas_call(
        body, name=name, grid=(r // tr,), in_specs=[pl.BlockSpec((tr, c), lambda i: (i, 0))],
        out_specs=pl.BlockSpec((None, tr, c), lambda i: (_my_chip(), i, 0)),
        out_shape=jax.ShapeDtypeStruct((N_CHIPS, r, c), BF16), compiler_params=_cparams(("parallel",)),
    )(w)


def _hosted_gather(bigs):
    nb = len(bigs)

    def rows(a, c):
        half = bigs[a].shape[1] // 2
        return pl.ds(c * half, half)

    def start(ins, outs, send_sems, recv_sems):
        x, y, c, chips = _place()
        q = 2 * x + y
        for a in range(nb):
            for k, chip in enumerate(chips):
                _rcopy(outs[a].at[q, rows(a, c)], outs[a].at[q, rows(a, c)], send_sems, recv_sems, 6 * a + k,
                       (chip[0], chip[1], c)).start()

    def mid(ins, outs, send_sems, recv_sems):
        x, y, c, chips = _place()
        sib = (x, y, 1 - c)
        for a in range(nb):
            for k, chip in enumerate(chips):
                slab = outs[a].at[2 * chip[0] + chip[1], rows(a, c)]
                _rcopy(slab, slab, send_sems, recv_sems, 6 * a + k, sib).wait_recv()
                _rcopy(slab, slab, send_sems, recv_sems, 6 * a + 3 + k, sib).start()

    def finish(ins, outs, send_sems, recv_sems):
        x, y, c, chips = _place()
        q = 2 * x + y
        sib = (x, y, 1 - c)
        for a in range(nb):
            for k, chip in enumerate(chips):
                qk = 2 * chip[0] + chip[1]
                other = outs[a].at[qk, rows(a, 1 - c)]
                _rcopy(other, other, send_sems, recv_sems, 6 * a + 3 + k, sib).wait_recv()
                mine = outs[a].at[q, rows(a, c)]
                _rcopy(mine, mine, send_sems, recv_sems, 6 * a + k, sib).wait_send()
                fwd = outs[a].at[qk, rows(a, c)]
                _rcopy(fwd, fwd, send_sems, recv_sems, 6 * a + 3 + k, sib).wait_send()

    return dict(arrays=list(bigs), out_shape=[jax.ShapeDtypeStruct(b.shape, b.dtype) for b in bigs],
                aliases={a: a for a in range(nb)}, nsem=6 * nb, start=start, mid=mid, finish=finish)


def _hosted_rs_chips(ps):
    na = len(ps)

    def copies(ins, outs, send_sems, recv_sems):
        x, y, c, chips = _place()
        return [_rcopy(ins[a].at[2 * chip[0] + chip[1]], outs[a].at[k], send_sems, recv_sems, 3 * a + k,
                       (chip[0], chip[1], c)) for a in range(na) for k, chip in enumerate(chips)]

    def start(ins, outs, send_sems, recv_sems):
        for cp in copies(ins, outs, send_sems, recv_sems):
            cp.start()

    def finish(ins, outs, send_sems, recv_sems):
        for cp in copies(ins, outs, send_sems, recv_sems):
            cp.wait()

    return dict(arrays=list(ps), out_shape=[jax.ShapeDtypeStruct((3,) + p.shape[1:], p.dtype) for p in ps],
                aliases={}, nsem=3 * na, start=start, mid=None, finish=finish)


def _host_plumbing(hosted, n_in, n_out):
    nh = len(hosted['arrays'])
    return ([ANY] * nh, [ANY] * len(hosted['out_shape']), list(hosted['out_shape']),
            [pltpu.SemaphoreType.DMA((hosted['nsem'],)), pltpu.SemaphoreType.DMA((hosted['nsem'],))],
            {n_in + a: n_out + b for a, b in hosted['aliases'].items()})


def _host_phase(hosted, phase, when, hins, houts, sems):
    fn = hosted[phase]
    if fn is None:
        return

    @pl.when(when)
    def _():
        fn(hins, houts, sems[0], sems[1])


def _gather_weights(bigs, smalls):
    nb, ns = len(bigs), len(smalls)
    na = nb + ns
    nsem = 6 * nb + 3 * ns
    big = _hosted_gather(bigs)

    def body(*refs):
        ins, outs = refs[:na], refs[na:2 * na]
        send_sems, recv_sems, loc_sems = refs[2 * na:]
        x, y, c, chips = _place()
        q = 2 * x + y
        sib = (x, y, 1 - c)
        locs, sends = [], []
        for s in range(ns):
            cp = pltpu.make_async_copy(ins[nb + s], outs[nb + s].at[q], loc_sems.at[s])
            cp.start()
            locs.append(cp)
        big['start'](ins[:nb], outs[:nb], send_sems, recv_sems)
        for s in range(ns):
            a = nb + s
            for k, chip in enumerate(chips):
                cp = _rcopy(ins[a], outs[a].at[q], send_sems, recv_sems, 6 * nb + 3 * s + k,
                            (chip[0], chip[1], c))
                cp.start()
                sends.append(cp)
        big['mid'](ins[:nb], outs[:nb], send_sems, recv_sems)
        big['finish'](ins[:nb], outs[:nb], send_sems, recv_sems)
        for s in range(ns):
            a = nb + s
            for k, chip in enumerate(chips):
                qk = 2 * chip[0] + chip[1]
                _rcopy(ins[a], outs[a].at[qk], send_sems, recv_sems, 6 * nb + 3 * s + k, sib).wait_recv()
        for cp in sends:
            cp.wait_send()
        for cp in locs:
            cp.wait()

    arrs = list(bigs) + list(smalls)
    out_shape = ([jax.ShapeDtypeStruct(a.shape, a.dtype) for a in bigs]
                 + [jax.ShapeDtypeStruct((N_CHIPS,) + a.shape, a.dtype) for a in smalls])
    return pl.pallas_call(
        body, name="gather_weights", in_specs=[ANY] * na, out_specs=[ANY] * na, out_shape=out_shape,
        input_output_aliases={a: a for a in range(nb)},
        scratch_shapes=[pltpu.SemaphoreType.DMA((nsem,)), pltpu.SemaphoreType.DMA((nsem,)),
                        pltpu.SemaphoreType.DMA((max(ns, 1),))],
    )(*arrs)


def _rs_sibling(gs, name):
    na = len(gs)

    def body(*refs):
        ins, recv_o = refs[:na], refs[na:2 * na]
        send_sems, recv_sems = refs[2 * na:]
        x, y, c, _ = _place()
        cps = []
        for a in range(na):
            half = gs[a].shape[1] // 2
            snd = _rcopy(ins[a].at[:, pl.ds((1 - c) * half, half), :], recv_o[a], send_sems, recv_sems, a,
                         (x, y, 1 - c))
            snd.start()
            cps.append(snd)
        for cp in cps:
            cp.wait()

    halves = [jax.ShapeDtypeStruct((N_CHIPS, g.shape[1] // 2, g.shape[2]), g.dtype) for g in gs]
    return pl.pallas_call(
        body, name=name, in_specs=[ANY] * na, out_specs=[ANY] * na, out_shape=halves,
        scratch_shapes=[pltpu.SemaphoreType.DMA((na,)), pltpu.SemaphoreType.DMA((na,))],
    )(*gs)


def _share_sibling(fs):
    na = len(fs)

    def body(*refs):
        outs = refs[na:2 * na]
        send_sems, recv_sems = refs[2 * na:]
        x, y, c, _ = _place()
        cps = []
        for a in range(na):
            half = fs[a].shape[0] // 2
            rows = pl.ds(c * half, half)
            snd = _rcopy(outs[a].at[rows], outs[a].at[rows], send_sems, recv_sems, a, (x, y, 1 - c))
            snd.start()
            cps.append(snd)
        for a, cp in enumerate(cps):
            half = fs[a].shape[0] // 2
            other = pl.ds((1 - c) * half, half)
            cp.wait_send()
            _rcopy(outs[a].at[other], outs[a].at[other], send_sems, recv_sems, a, (x, y, 1 - c)).wait_recv()

    return pl.pallas_call(
        body, name="share_sibling", in_specs=[ANY] * na, out_specs=[ANY] * na,
        out_shape=[jax.ShapeDtypeStruct(f.shape, f.dtype) for f in fs],
        input_output_aliases={a: a for a in range(na)},
        scratch_shapes=[pltpu.SemaphoreType.DMA((na,)), pltpu.SemaphoreType.DMA((na,))],
    )(*fs)


def _allgather_small(buf):
    def body(in_ref, out_ref, send_sems, recv_sems, loc_sem):
        x, y, c, _ = _place()
        me = 4 * x + 2 * y + c
        loc = pltpu.make_async_copy(in_ref, out_ref.at[me], loc_sem)
        loc.start()
        cps = [loc]
        for k in range(1, 8):
            dev = (1 - x if k & 4 else x, 1 - y if k & 2 else y, 1 - c if k & 1 else c)
            snd = _rcopy(in_ref, out_ref.at[me], send_sems, recv_sems, k - 1, dev)
            snd.start()
            cps.append(snd)
        for cp in cps:
            cp.wait()

    return pl.pallas_call(
        body, name="allgather_small", in_specs=[ANY], out_specs=ANY,
        out_shape=jax.ShapeDtypeStruct((8,) + buf.shape, buf.dtype),
        scratch_shapes=[pltpu.SemaphoreType.DMA((7,)), pltpu.SemaphoreType.DMA((7,)), pltpu.SemaphoreType.DMA],
    )(buf)


def _row_tile(rows, cols, nbuf):
    budget = 24 * 1024 * 1024 // (nbuf * cols * 4 * 2)
    return _tile(rows, max(16, budget - budget % 16), 16) if rows % 16 == 0 else rows


def _add_pairs(g, rcv, name):
    s, half, c = rcv.shape
    tr = _row_tile(half, c, 3)
    nh = half // tr

    def body(a_ref, b_ref, o_ref):
        o_ref[...] = (a_ref[...].astype(F32) + b_ref[...].astype(F32)).astype(o_ref.dtype)

    blk = pl.BlockSpec((None, tr, c), lambda j, i: (j, i, 0))
    mine = pl.BlockSpec((None, tr, c), lambda j, i: (j, lax.axis_index("c") * nh + i, 0))
    return pl.pallas_call(
        body, name=name, grid=(s, nh), in_specs=[mine, blk], out_specs=blk,
        out_shape=jax.ShapeDtypeStruct(rcv.shape, rcv.dtype), compiler_params=_cparams(("parallel", "parallel")),
    )(g, rcv)


def _sum_chips(part, rcv, name):
    _, half, c = part.shape
    tr = _row_tile(half, c, 5)
    nh = half // tr

    def body(o_ref, r_ref, out_ref):
        acc = o_ref[...].astype(F32)
        for k in range(3):
            acc = acc + r_ref[k].astype(F32)
        out_ref[...] = acc

    return pl.pallas_call(
        body, name=name, grid=(nh,),
        in_specs=[pl.BlockSpec((None, tr, c), lambda i: (_my_chip(), i, 0)),
                  pl.BlockSpec((3, tr, c), lambda i: (0, i, 0))],
        out_specs=pl.BlockSpec((tr, c), lambda i: (lax.axis_index("c") * nh + i, 0)),
        out_shape=jax.ShapeDtypeStruct((2 * half, c), F32), compiler_params=_cparams(("parallel",)),
    )(part, rcv)


def _sum_devices(parts):
    _, n, _ = parts.shape
    tr = _tile(n, 512, 8)

    def body(p_ref, o_ref):
        acc = p_ref[0]
        for k in range(1, 8):
            acc = acc + p_ref[k]
        o_ref[...] = acc

    return pl.pallas_call(
        body, name="sum_devices", grid=(n // tr,),
        in_specs=[pl.BlockSpec((8, tr, 128), lambda i: (0, i, 0))],
        out_specs=pl.BlockSpec((tr, 128), lambda i: (i, 0)),
        out_shape=jax.ShapeDtypeStruct((n, 128), F32), compiler_params=_cparams(("parallel",)),
    )(parts)


def _adamw(w, g, m, v, name):
    r, c = w.shape
    tr = _row_tile(r, c, 7)
    c1 = 1.0 - ADAM_B1 ** ADAM_STEP
    c2 = 1.0 - ADAM_B2 ** ADAM_STEP

    def body(w_ref, g_ref, m_ref, v_ref, d_ref, mo_ref, vo_ref):
        gv = g_ref[...]
        mn = ADAM_B1 * m_ref[...] + (1.0 - ADAM_B1) * gv
        vn = ADAM_B2 * v_ref[...] + (1.0 - ADAM_B2) * (gv * gv)
        mo_ref[...] = mn
        vo_ref[...] = vn
        m_hat = mn / c1
        v_hat = vn / c2
        d_ref[...] = -ADAM_LR * (m_hat / (jnp.sqrt(v_hat) + ADAM_EPS) + ADAM_WD * w_ref[...])

    blk = pl.BlockSpec((tr, c), lambda i: (i, 0))
    sh = jax.ShapeDtypeStruct((r, c), F32)
    return pl.pallas_call(
        body, name=name, grid=(r // tr,), in_specs=[blk] * 4, out_specs=[blk] * 3, out_shape=[sh] * 3,
        compiler_params=_cparams(("parallel",)),
    )(w, g, m, v)


WEIGHTS = ['norm_mix_g', 'w_in', 'ln_a_g', 'ln_a_b', 'w_s', 'b_s', 'norm_a_g', 'conv_ssm_w', 'conv_ssm_b',
           'dt_bias', 'a_log', 'd_skip', 'ssm_norm_g', 'w_out', 'norm_ffn_g', 'w_up', 'conv_ffn_w',
           'conv_ffn_b', 'w_down', 'norm_ple_g', 'w_ple_gate', 'w_ple', 'norm_final_g']
BIG = ['w_in', 'w_out', 'w_up', 'w_down', 'w_ple_gate', 'w_ple']
SMALL = [n for n in WEIGHTS if n not in BIG]
PACK_ALIGN = 2048


def _pack(arrs):
    parts = []
    for a in arrs:
        f = a.reshape(-1).astype(F32)
        parts.append(jnp.pad(f, (0, (-f.shape[0]) % PACK_ALIGN)))
    return jnp.concatenate(parts).reshape(-1, 128)


def _unpack(buf, shapes):
    flat = buf.reshape(-1)
    out, off = [], 0
    for s in shapes:
        n = math.prod(s)
        out.append(flat[off:off + n].reshape(s))
        off += n + (-n) % PACK_ALIGN
    return out


def _pad_heads(v):
    return jnp.pad(v, ((0, 0), (0, HPAD - v.shape[1])))


def _col_sharded(full):
    r, c4 = full.shape
    return jnp.transpose(full.reshape(r, N_CHIPS, c4 // N_CHIPS), (1, 0, 2))


def _from_col_sharded(g):
    s, r, c = g.shape
    return jnp.transpose(g, (1, 0, 2)).reshape(r, s * c)


def kernel(x, p, norm_mix_g, w_in, ln_a_g, ln_a_b, w_s, b_s, norm_a_g, conv_ssm_w, conv_ssm_b, dt_bias, a_log, d_skip, ssm_norm_g, w_out, norm_ffn_g, w_up, conv_ffn_w, conv_ffn_b, w_down, norm_ple_g, w_ple_gate, w_ple, norm_final_g, loss_target, m_norm_mix_g, m_w_in, m_ln_a_g, m_ln_a_b, m_w_s, m_b_s, m_norm_a_g, m_conv_ssm_w, m_conv_ssm_b, m_dt_bias, m_a_log, m_d_skip, m_ssm_norm_g, m_w_out, m_norm_ffn_g, m_w_up, m_conv_ffn_w, m_conv_ffn_b, m_w_down, m_norm_ple_g, m_w_ple_gate, m_w_ple, m_norm_final_g, v_norm_mix_g, v_w_in, v_ln_a_g, v_ln_a_b, v_w_s, v_b_s, v_norm_a_g, v_conv_ssm_w, v_conv_ssm_b, v_dt_bias, v_a_log, v_d_skip, v_ssm_norm_g, v_w_out, v_norm_ffn_g, v_w_up, v_conv_ffn_w, v_conv_ffn_b, v_w_down, v_norm_ple_g, v_w_ple_gate, v_w_ple, v_norm_final_g):
    given = dict(locals())
    wts = {n: given[n] for n in WEIGHTS}
    mom = {n: given['m_' + n] for n in WEIGHTS}
    var = {n: given['v_' + n] for n in WEIGHTS}
    d = D_MODEL
    xt, pt, tgt = x[0], p[0, 0], loss_target[0]
    chip = 2 * lax.axis_index("x") + lax.axis_index("y")

    slots = {n: _cast_into_slot(wts[n][0], "cast_" + n) for n in BIG}
    g_in, g_cs, g_cf = _gather_weights([slots['w_in']], [conv_ssm_w[0], conv_ffn_w[0]])
    later = [n for n in BIG if n != 'w_in']
    w_in_full = _from_col_sharded(g_in)
    w_main = w_in_full[:, :D_MAIN]
    w_dt = _pad_heads(w_in_full[:, D_MAIN:])
    cs_w = _from_col_sharded(g_cs)
    cf_w = _from_col_sharded(g_cf)
    consts = _ssd_consts()
    dtb, alog = _pad_heads(dt_bias), _pad_heads(a_log)
    de = jnp.repeat(d_skip[0], HEAD_DIM)[None, :]
    b_exp = jnp.broadcast_to(b_s[0][:, :, None], (N_GROUPS_A, CHUNK, CHUNK))

    a1 = _rms_fwd(xt, norm_mix_g, "rms_mix")
    proj = _matmul(a1, w_main, mode='nn', name="mm_proj", tm=512, tn=1024)
    dt_raw = _matmul(a1, w_dt, mode='nn', name="mm_dt", tm=1024, tn=128)
    yab = _gmlp_fwd(proj, ln_a_g, ln_a_b, w_s[0], b_exp, norm_a_g)
    yab, ysave, hs, *gathered = _ssd_fwd(proj, dt_raw, yab, cs_w, conv_ssm_b, dtb, alog, de, ssm_norm_g, consts,
                                         _hosted_gather([slots[n] for n in later]))
    g_out, g_up, g_down, g_pg, g_ple = gathered
    w_out_f = g_out.reshape(D_MIX, d)
    w_down_f = g_down.reshape(D_FF, d)
    w_pg_f = g_pg.reshape(d, d)
    h1 = _matmul(yab, w_out_f, mode='nn', name="mm_out", res=xt, tm=512, tn=1024, tk=4096)
    f = _rms_fwd(h1, norm_ffn_g, "rms_ffn")
    hid = _matmul(f, g_up, mode='nn', name="mm_up", b_sharded=True, tm=512, tn=1408)
    act = _ffn_act_fwd(hid, cf_w, conv_ffn_b)
    h2 = _matmul(act, w_down_f, mode='nn', name="mm_down", res=h1, tm=512, tn=1024, tk=2816)
    n3 = _rms_fwd(h2, norm_ple_g, "rms_ple")
    gl = _matmul(n3, w_pg_f, mode='nn', name="mm_pg", tm=512, tn=1024)
    pe = _matmul(pt, g_ple, mode='nn', name="mm_ple", b_sharded=True, tm=1024, tn=512)
    dh3, dgl, dpe, lossv, dgf = _tail(h2, gl, pe, tgt, norm_final_g[None, :])

    gs_ple = _matmul(pt, dpe, mode='tn', name="mm_dw_ple", out_dtype=BF16, out_shards=N_CHIPS,
                     tm=256, tn=512, tk=2048)
    gs_pg = _matmul(n3, dgl, mode='tn', name="mm_dw_pg", out_dtype=BF16, tm=1024, tn=1024, tk=2048)
    dn3 = _matmul(dgl, w_pg_f, mode='nt', name="mm_dn3", tm=512, tn=1024)
    dh2, dg_ple, dh2_b = _rms_bwd(h2, norm_ple_g, dn3, dh3, "rms_ple_bwd", True)
    dact = _matmul(dh2_b, w_down_f, mode='nt', name="mm_dact", out_dtype=BF16, tm=512, tn=1408)
    gs_down = _matmul(act, dh2_b, mode='tn', name="mm_dw_down", out_dtype=BF16, tm=1408, tn=1024, tk=2048)
    dpg, dpu, wg_acc, wu_acc = _ffn_act_bwd(hid, cf_w, conv_ffn_b, dact)
    hc = N_CHIPS // 2
    gs_up = jnp.concatenate(
        [_matmul(f, dpg, mode='tn', name="mm_dw_up_g", out_dtype=BF16, out_shards=hc, tm=1024, tn=1408, tk=2048),
         _matmul(f, dpu, mode='tn', name="mm_dw_up_u", out_dtype=BF16, out_shards=hc, tm=1024, tn=1408, tk=2048)],
        axis=0)
    early = [gs_up, gs_down.reshape(N_CHIPS, D_FF // N_CHIPS, d), gs_pg.reshape(N_CHIPS, d // N_CHIPS, d), gs_ple]
    part_e = [_add_pairs(a, b, "rs_add_e%d" % i)
              for i, (a, b) in enumerate(zip(early, _rs_sibling(early, "rs_sibling_early")))]
    df = _matmul(dpg, g_up, mode='nt', name="mm_df_g", b_sharded=True, tm=1024, tn=1024, tk=1408)
    df = _matmul(dpu, g_up, mode='nt', name="mm_df_u", b_sharded=True, b_shard_off=hc, res=df,
                 tm=1024, tn=1024, tk=1408)
    dh1, dg_ffn, dh1_b = _rms_bwd(h1, norm_ffn_g, df, dh2, "rms_ffn_bwd", True)
    dyab = _matmul(dh1_b, w_out_f, mode='nt', name="mm_dyab", tm=512, tn=1024)
    gs_out = _matmul(yab, dh1_b, mode='tn', name="mm_dw_out", out_dtype=BF16, tm=1024, tn=1024, tk=2048)
    duv, dws, dbs, dlng, dlnb, dnag = _gmlp_bwd(proj, dyab, ln_a_g, ln_a_b, w_s[0], b_exp, norm_a_g)
    dz, dxs, dbm, dcm, ddt_raw, acc_x, acc_b, acc_c, acc_gain, acc_head, *rcv_e = _ssd_bwd(
        proj, dt_raw, dyab, ysave, hs, cs_w, conv_ssm_b, dtb, alog, de, ssm_norm_g, consts,
        _hosted_rs_chips(part_e))
    dproj = jnp.concatenate([duv, dz, dxs, dbm, dcm], axis=1)
    dw_main = _matmul(a1, dproj, mode='tn', name="mm_dw_main", out_dtype=BF16, tm=1024, tn=1024, tk=2048)
    dw_dt = _matmul(a1, ddt_raw, mode='tn', name="mm_dw_dt", out_dtype=BF16, tm=1024, tn=128, tk=2048)
    gs_in = _col_sharded(jnp.concatenate([dw_main, dw_dt[:, :N_HEADS]], axis=1))
    late = [gs_in, gs_out.reshape(N_CHIPS, D_MIX // N_CHIPS, d)]
    part_l = [_add_pairs(a, b, "rs_add_l%d" % i)
              for i, (a, b) in enumerate(zip(late, _rs_sibling(late, "rs_sibling_late")))]
    da_dt = _matmul(ddt_raw, w_dt, mode='nt', name="mm_da_dt", tm=1024, tn=1024)
    da, *rcv_l = _matmul(dproj, w_main, mode='nt', name="mm_da", res=da_dt, tm=1024, tn=1024, tk=2048,
                         hosted=_hosted_rs_chips(part_l))
    dx, dg_mix = _rms_bwd(xt, norm_mix_g, da, dh1, "rms_mix_bwd", False)

    order = ['w_up', 'w_down', 'w_ple_gate', 'w_ple', 'w_in', 'w_out']
    halves = [_sum_chips(a, b, "rs_sum_%d" % i)
              for i, (a, b) in enumerate(zip(part_e + part_l, list(rcv_e) + list(rcv_l)))]
    g_big = dict(zip(order, _share_sibling(halves)))

    def conv_rows(acc, k):
        return acc[:, k, :].reshape(1, -1)

    dcw = jnp.concatenate([jnp.concatenate([conv_rows(acc_x, k), conv_rows(acc_b, k), conv_rows(acc_c, k)], axis=1)
                           for k in range(SSM_CONV)], axis=0)
    dcb = jnp.concatenate([conv_rows(acc_x, SSM_CONV), conv_rows(acc_b, SSM_CONV), conv_rows(acc_c, SSM_CONV)], axis=1)
    part = {
        'norm_mix_g': dg_mix, 'ln_a_g': dlng, 'ln_a_b': dlnb, 'w_s': dws, 'b_s': dbs, 'norm_a_g': dnag,
        'conv_ssm_w': dcw, 'conv_ssm_b': dcb,
        'dt_bias': acc_head[0:1, :N_HEADS], 'a_log': acc_head[1:2, :N_HEADS], 'd_skip': acc_head[2:3, :N_HEADS],
        'ssm_norm_g': acc_gain[:, 0, :], 'norm_ffn_g': dg_ffn,
        'conv_ffn_w': jnp.concatenate([wg_acc[:FFN_CONV], wu_acc[:FFN_CONV]], axis=1),
        'conv_ffn_b': jnp.concatenate([wg_acc[FFN_CONV:FFN_CONV + 1], wu_acc[FFN_CONV:FFN_CONV + 1]], axis=1),
        'norm_ple_g': dg_ple, 'norm_final_g': dgf,
    }
    full_shapes = {n: wts[n].shape for n in SMALL}
    full_shapes['conv_ssm_w'] = (1, SSM_CONV, D_XBC)
    full_shapes['conv_ffn_w'] = (1, FFN_CONV, 2 * D_FF)
    packed = _pack([part[n] for n in SMALL] + [lossv[:, 0:1]])
    total = _sum_devices(_allgather_small(packed))
    pieces = _unpack(total, [full_shapes[n] for n in SMALL] + [(1,)])
    g_small = dict(zip(SMALL, pieces[:-1]))
    loss = pieces[-1][0]
    for n in ('conv_ssm_w', 'conv_ffn_w'):
        width = wts[n].shape[2]
        g_small[n] = lax.dynamic_slice_in_dim(g_small[n], chip * width, width, axis=2)

    grads, delta, new_m, new_v = {}, {}, {}, {}
    for n in BIG:
        shp = wts[n].shape
        dl, mn, vn = _adamw(wts[n][0], g_big[n], mom[n][0], var[n][0], "adamw_" + n)
        grads[n], delta[n], new_m[n], new_v[n] = (g_big[n].reshape(shp), dl.reshape(shp), mn.reshape(shp),
                                                  vn.reshape(shp))
    shapes = [wts[n].shape for n in SMALL]
    dl, mn, vn = _adamw(_pack([wts[n] for n in SMALL]), _pack([g_small[n] for n in SMALL]),
                        _pack([mom[n] for n in SMALL]), _pack([var[n] for n in SMALL]), "adamw_small")
    for n, a, b, c in zip(SMALL, _unpack(dl, shapes), _unpack(mn, shapes), _unpack(vn, shapes)):
        grads[n], delta[n], new_m[n], new_v[n] = g_small[n], a, b, c

    return (loss, dx[None], *[grads[n] for n in WEIGHTS], *[delta[n] for n in WEIGHTS],
            *[new_m[n] for n in WEIGHTS], *[new_v[n] for n in WEIGHTS])
```

```python
import functools
import math

import jax
import jax.numpy as jnp
from jax import lax
from jax.experimental import pallas as pl
from jax.experimental.pallas import tpu as pltpu

D_MODEL = 2048
SEQ = 8192
D_MIX = 2 * D_MODEL
D_A = D_MIX // 2
CHUNK = 128
N_GROUPS_A = D_A // 128
D_SSM = D_MIX - D_A
HEAD_DIM = 64
N_HEADS = D_SSM // HEAD_DIM
HEADS_PER_GROUP = 4
N_SSM_GROUPS = N_HEADS // HEADS_PER_GROUP
GW = HEADS_PER_GROUP * HEAD_DIM
D_STATE = 128
SSM_CONV = 4
D_BC = N_SSM_GROUPS * D_STATE
D_XBC = D_SSM + 2 * D_BC
D_MAIN = 2 * D_A + D_SSM + D_XBC
D_IN = D_MAIN + N_HEADS
D_FF = (D_MODEL * 11) // 4
FFN_CONV = 3
D_PLE = 256
EPS = 1e-6
HPAD = 128
N_CHIPS = 4

ADAM_LR = 0.001
ADAM_B1 = 0.9
ADAM_B2 = 0.999
ADAM_EPS = 1e-08
ADAM_WD = 0.01
ADAM_STEP = 10

F32 = jnp.float32
BF16 = jnp.bfloat16
MESH = pl.DeviceIdType.MESH
VMEM_LIMIT = 56 * 1024 * 1024


def _cparams(sem):
    return pltpu.CompilerParams(dimension_semantics=sem, vmem_limit_bytes=VMEM_LIMIT)


def _tile(n, pref, mult):
    t = min(pref, n)
    t -= t % mult
    while n % t:
        t -= mult
    return t


def _dot(a, b):
    return jnp.dot(a, b, preferred_element_type=F32)


def _dot_nt(a, b):
    return lax.dot_general(a, b, (((1,), (1,)), ((), ())), preferred_element_type=F32)


def _dot_tn(a, b):
    return lax.dot_general(a, b, (((0,), (0,)), ((), ())), preferred_element_type=F32)


def _split3(x):
    hi = x.astype(BF16)
    r = x - hi.astype(F32)
    mid = r.astype(BF16)
    lo = (r - mid.astype(F32)).astype(BF16)
    return hi, mid, lo


def _x01(x, e):
    h, m, l = _split3(x)
    return _dot(h, e) + _dot(m, e) + _dot(l, e)


def _x01_nt(x, e):
    h, m, l = _split3(x)
    return _dot_nt(h, e) + _dot_nt(m, e) + _dot_nt(l, e)


def _e01x(e, x):
    h, m, l = _split3(x)
    return _dot(e, h) + _dot(e, m) + _dot(e, l)


def _e01x_tn(e, x):
    h, m, l = _split3(x)
    return _dot_tn(e, h) + _dot_tn(e, m) + _dot_tn(e, l)


def _sigmoid(x):
    return 1.0 / (1.0 + jnp.exp(-x))


_GELU_C = math.sqrt(2.0 / math.pi)


def _gelu_and_grad(x):
    x2 = x * x
    th = jnp.tanh(_GELU_C * (x + 0.044715 * x * x2))
    y = 0.5 * x * (1.0 + th)
    dy = 0.5 * (1.0 + th) + 0.5 * x * (1.0 - th * th) * (_GELU_C * (1.0 + 3.0 * 0.044715 * x2))
    return y, dy


def _silu_and_grad(x):
    s = _sigmoid(x)
    return x * s, s * (1.0 + x * (1.0 - s))


def _softplus(x):
    u = jnp.exp(-jnp.abs(x))
    w = 1.0 + u
    l1p = jnp.where(w == 1.0, u, jnp.log(w) * (u / (w - 1.0)))
    return jnp.maximum(x, 0.0) + l1p


def _matmul(a, b, *, mode, name, out_dtype=F32, res=None, tm=512, tn=512, tk=2048,
            b_sharded=False, b_shard_off=0, out_shards=0, hosted=None):
    if mode == 'tn':
        kdim, m = a.shape
        n = b.shape[1]
    else:
        m, kdim = a.shape
        if b_sharded:
            s_b, d1, d2 = b.shape
            n = s_b * d2 if mode == 'nn' else d1
        else:
            n = b.shape[1] if mode == 'nn' else b.shape[0]
    per = None
    if b_sharded:
        per = b.shape[2]
    if out_shards:
        per = n // out_shards
    tm = _tile(m, tm, 128 if mode == 'tn' else 8)
    if mode == 'nt' and b_sharded:
        tn = _tile(n, tn, 128)
        tk = _tile(per, tk, 128)
    elif per is not None:
        tn = _tile(per, tn, 128)
        tk = _tile(kdim, tk, 128 if mode != 'tn' else 8)
    else:
        tn = _tile(n, tn, 128)
        tk = _tile(kdim, tk, 128 if mode != 'tn' else 8)
    nm, nn_, nk = m // tm, n // tn, kdim // tk
    has_res = res is not None
    n_in = 2 + has_res
    nh_in = len(hosted['arrays']) if hosted else 0
    nh_out = len(hosted['out_shape']) if hosted else 0

    def body(*refs):
        a_ref, b_ref = refs[0], refs[1]
        res_ref = refs[2] if has_res else None
        o_ref = refs[n_in + nh_in]
        if hosted:
            hins = refs[n_in:n_in + nh_in]
            houts = refs[n_in + nh_in + 1:n_in + nh_in + 1 + nh_out]
            sems = refs[-2:]
            ids = [pl.program_id(d) for d in range(3)]
            at_first = jnp.logical_and(jnp.logical_and(ids[0] == 0, ids[1] == 0), ids[2] == 0)
            at_last = jnp.logical_and(jnp.logical_and(ids[0] == nn_ - 1, ids[1] == nm - 1), ids[2] == nk - 1)
            _host_phase(hosted, 'start', at_first, hins, houts, sems)
        av = a_ref[...].astype(BF16)
        bv = b_ref[...].astype(BF16)
        if mode == 'nn':
            p = _dot(av, bv)
        elif mode == 'nt':
            p = _dot_nt(av, bv)
        else:
            p = _dot_tn(av, bv)

        def fin(v):
            if has_res:
                v = v + res_ref[...]
            o_ref[...] = v.astype(o_ref.dtype)

        if nk == 1:
            fin(p)
        else:
            acc_ref = refs[n_in + nh_in + 1 + nh_out]
            k = pl.program_id(2)

            @pl.when(k == 0)
            def _():
                acc_ref[...] = p

            @pl.when(k > 0)
            def _():
                acc_ref[...] += p

            @pl.when(k == nk - 1)
            def _():
                fin(acc_ref[...])
        if hosted:
            _host_phase(hosted, 'mid', at_last, hins, houts, sems)
            _host_phase(hosted, 'finish', at_last, hins, houts, sems)

    if mode == 'nn':
        a_spec = pl.BlockSpec((tm, tk), lambda j, i, k: (i, k))
        if b_sharded:
            nps = per // tn
            b_spec = pl.BlockSpec((None, tk, tn), lambda j, i, k: (j // nps, k, j % nps))
        else:
            b_spec = pl.BlockSpec((tk, tn), lambda j, i, k: (k, j))
    elif mode == 'nt':
        a_spec = pl.BlockSpec((tm, tk), lambda j, i, k: (i, k))
        if b_sharded:
            kps = per // tk
            b_spec = pl.BlockSpec((None, tn, tk), lambda j, i, k: (k // kps + b_shard_off, j, k % kps))
        else:
            b_spec = pl.BlockSpec((tn, tk), lambda j, i, k: (j, k))
    else:
        a_spec = pl.BlockSpec((tk, tm), lambda j, i, k: (k, i))
        b_spec = pl.BlockSpec((tk, tn), lambda j, i, k: (k, j))
    in_specs = [a_spec, b_spec]
    args = [a, b]
    if has_res:
        in_specs.append(pl.BlockSpec((tm, tn), lambda j, i, k: (i, j)))
        args.append(res)
    if out_shards:
        nps_o = per // tn
        out_shape = jax.ShapeDtypeStruct((out_shards, m, per), out_dtype)
        out_spec = pl.BlockSpec((None, tm, tn), lambda j, i, k: (j // nps_o, i, j % nps_o))
    else:
        out_shape = jax.ShapeDtypeStruct((m, n), out_dtype)
        out_spec = pl.BlockSpec((tm, tn), lambda j, i, k: (i, j))
    scratch = [pltpu.VMEM((tm, tn), F32)] if nk > 1 else []
    if not hosted:
        return pl.pallas_call(
            body, name=name, grid=(nn_, nm, nk), in_specs=in_specs, out_specs=out_spec,
            out_shape=out_shape, scratch_shapes=scratch,
            compiler_params=_cparams(("parallel", "parallel", "arbitrary")),
        )(*args)
    h_in, h_out, h_shape, h_scratch, h_alias = _host_plumbing(hosted, n_in, 1)
    return pl.pallas_call(
        body, name=name, grid=(nn_, nm, nk), in_specs=in_specs + h_in, out_specs=[out_spec] + h_out,
        out_shape=[out_shape] + h_shape, scratch_shapes=scratch + h_scratch, input_output_aliases=h_alias,
        compiler_params=_cparams(("arbitrary", "arbitrary", "arbitrary")),
    )(*args, *hosted['arrays'])


def _rms_fwd(x, g, name):
    t, d = x.shape
    tt = _tile(t, 512, 8)

    def body(x_ref, g_ref, o_ref):
        xv = x_ref[...]
        r = lax.rsqrt(jnp.mean(xv * xv, axis=-1, keepdims=True) + EPS)
        o_ref[...] = (xv * r * g_ref[...]).astype(o_ref.dtype)

    return pl.pallas_call(
        body, name=name, grid=(t // tt,),
        in_specs=[pl.BlockSpec((tt, d), lambda i: (i, 0)), pl.BlockSpec((1, d), lambda i: (0, 0))],
        out_specs=pl.BlockSpec((tt, d), lambda i: (i, 0)),
        out_shape=jax.ShapeDtypeStruct((t, d), BF16),
        compiler_params=_cparams(("parallel",)),
    )(x, g)


def _rms_bwd(x, g, dy, dres, name, also_bf16):
    t, d = x.shape
    tt = _tile(t, 256, 16)

    def body(x_ref, g_ref, dy_ref, dres_ref, dx_ref, dg_ref, *dxb):
        i = pl.program_id(0)
        xv = x_ref[...]
        r = lax.rsqrt(jnp.mean(xv * xv, axis=-1, keepdims=True) + EPS)
        xh = xv * r
        dyv = dy_ref[...].astype(F32)
        dxh = dyv * g_ref[...]
        c = jnp.mean(dxh * xh, axis=-1, keepdims=True)
        dx = dres_ref[...] + r * (dxh - xh * c)
        dx_ref[...] = dx
        for dxb_ref in dxb:
            dxb_ref[...] = dx.astype(BF16)
        part = jnp.sum(dyv * xh, axis=0, keepdims=True)

        @pl.when(i == 0)
        def _():
            dg_ref[...] = part

        @pl.when(i > 0)
        def _():
            dg_ref[...] += part

    row = pl.BlockSpec((tt, d), lambda i: (i, 0))
    vec = pl.BlockSpec((1, d), lambda i: (0, 0))
    return pl.pallas_call(
        body, name=name, grid=(t // tt,),
        in_specs=[row, vec, row, row], out_specs=[row, vec] + [row] * also_bf16,
        out_shape=[jax.ShapeDtypeStruct((t, d), F32), jax.ShapeDtypeStruct((1, d), F32)]
        + [jax.ShapeDtypeStruct((t, d), BF16)] * also_bf16,
        compiler_params=_cparams(("arbitrary",)),
    )(x, g, dy, dres)


def _tail(h2, gl, pe, target, gfin):
    t, d = h2.shape
    tt = _tile(t, 256, 8)

    def body(h2_ref, gl_ref, pe_ref, tg_ref, gf_ref, dh3_ref, dgl_ref, dpe_ref, loss_ref, dgf_ref):
        i = pl.program_id(0)
        sig = _sigmoid(gl_ref[...])
        pev = pe_ref[...]
        h3 = h2_ref[...] + sig * pev
        r = lax.rsqrt(jnp.mean(h3 * h3, axis=-1, keepdims=True) + EPS)
        xh = h3 * r
        gf = gf_ref[...]
        e = xh * gf - tg_ref[...]
        dy = e * (1.0 / d)
        dxh = dy * gf
        c = jnp.mean(dxh * xh, axis=-1, keepdims=True)
        dh3 = r * (dxh - xh * c)
        dh3_ref[...] = dh3
        dgl_ref[...] = (dh3 * pev * sig * (1.0 - sig)).astype(BF16)
        dpe_ref[...] = (dh3 * sig).astype(BF16)
        lpart = jnp.sum(e * e, axis=0, keepdims=True) * (0.5 / d)
        gpart = jnp.sum(dy * xh, axis=0, keepdims=True)

        @pl.when(i == 0)
        def _():
            loss_ref[...] = lpart
            dgf_ref[...] = gpart

        @pl.when(i > 0)
        def _():
            loss_ref[...] += lpart
            dgf_ref[...] += gpart

        @pl.when(i == t // tt - 1)
        def _():
            loss_ref[...] = jnp.broadcast_to(jnp.sum(loss_ref[...], axis=-1, keepdims=True), (1, d))

    row = pl.BlockSpec((tt, d), lambda i: (i, 0))
    vec = pl.BlockSpec((1, d), lambda i: (0, 0))
    return pl.pallas_call(
        body, name="tail", grid=(t // tt,),
        in_specs=[row, row, row, row, vec], out_specs=[row, row, row, vec, vec],
        out_shape=[jax.ShapeDtypeStruct((t, d), F32), jax.ShapeDtypeStruct((t, d), BF16),
                   jax.ShapeDtypeStruct((t, d), BF16), jax.ShapeDtypeStruct((1, d), F32),
                   jax.ShapeDtypeStruct((1, d), F32)],
        compiler_params=_cparams(("arbitrary",)),
    )(h2, gl, pe, target, gfin)


def _conv(cur_ref, prev_ref, w_ref, b_ref, ext_ref, first, width):
    rows = cur_ref.shape[0]
    ext_ref[0:8, :] = jnp.where(first, 0.0, prev_ref[...])
    ext_ref[8:8 + rows, :] = cur_ref[...]
    acc = b_ref[...]
    for k in range(width):
        acc = acc + w_ref[k:k + 1, :] * ext_ref[pl.ds(9 - width + k, rows), :]
    return acc


def _ffn_specs(t, tt, tc, nf):
    hb = tt // 8
    cur_g = pl.BlockSpec((tt, tc), lambda j, i: (i, j))
    cur_u = pl.BlockSpec((tt, tc), lambda j, i: (i, j + nf))
    prev_g = pl.BlockSpec((8, tc), lambda j, i: (jnp.maximum(i * hb - 1, 0), j))
    prev_u = pl.BlockSpec((8, tc), lambda j, i: (jnp.maximum(i * hb - 1, 0), j + nf))
    w_g = pl.BlockSpec((FFN_CONV, tc), lambda j, i: (0, j))
    w_u = pl.BlockSpec((FFN_CONV, tc), lambda j, i: (0, j + nf))
    b_g = pl.BlockSpec((1, tc), lambda j, i: (0, j))
    b_u = pl.BlockSpec((1, tc), lambda j, i: (0, j + nf))
    return [cur_g, prev_g, cur_u, prev_u, w_g, w_u, b_g, b_u]


FFN_TC = 512


def _shift_down(prev, cur, n, rid):
    return jnp.where(rid < n, pltpu.roll(prev, n, 0), pltpu.roll(cur, n, 0))


def _shift_up(cur, nxt, n, rid):
    return jnp.where(rid < 8 - n, pltpu.roll(cur, 8 - n, 0), pltpu.roll(nxt, 8 - n, 0))


def _conv3_group(prev, cur, w_ref, b_ref, rid):
    x1 = _shift_down(prev, cur, 1, rid)
    x2 = _shift_down(prev, cur, 2, rid)
    return b_ref[...] + w_ref[2:3, :] * cur + w_ref[1:2, :] * x1 + w_ref[0:1, :] * x2


def _ffn_act_fwd(hid, cw, cb):
    t = hid.shape[0]
    tt = _tile(t, 512, 16)
    tc = _tile(D_FF, FFN_TC, 128)
    nf = D_FF // tc

    def body(g_ref, gp_ref, u_ref, up_ref, wg_ref, wu_ref, bg_ref, bu_ref, o_ref):
        first = pl.program_id(1) == 0
        rid = lax.broadcasted_iota(jnp.int32, (8, tc), 0)

        def act(pg, cg, pu, cu):
            gate = _conv3_group(pg, cg, wg_ref, bg_ref, rid)
            up = _conv3_group(pu, cu, wu_ref, bu_ref, rid)
            return gate * _sigmoid(gate) * up

        def step(s, carry):
            pg, pu = carry
            r0 = pl.multiple_of(s * 16, 16)
            g0, g1 = g_ref[pl.ds(r0, 8), :], g_ref[pl.ds(r0 + 8, 8), :]
            u0, u1 = u_ref[pl.ds(r0, 8), :], u_ref[pl.ds(r0 + 8, 8), :]
            out = jnp.concatenate([act(pg, g0, pu, u0), act(g0, g1, u0, u1)], axis=0)
            o_ref[pl.ds(r0, 16), :] = out.astype(BF16)
            return g1, u1

        init = (jnp.where(first, 0.0, gp_ref[...]), jnp.where(first, 0.0, up_ref[...]))
        lax.fori_loop(0, tt // 16, step, init)

    return pl.pallas_call(
        body, name="ffn_act_fwd", grid=(nf, t // tt), in_specs=_ffn_specs(t, tt, tc, nf),
        out_specs=pl.BlockSpec((tt, tc), lambda j, i: (i, j)),
        out_shape=jax.ShapeDtypeStruct((t, D_FF), BF16),
        compiler_params=_cparams(("parallel", "arbitrary")),
    )(hid, hid, hid, hid, cw, cw, cb, cb)


def _ffn_act_bwd(hid, cw, cb, dact):
    t = hid.shape[0]
    tt = _tile(t, 512, 16)
    tc = _tile(D_FF, FFN_TC, 128)
    nf = D_FF // tc
    nt = t // tt
    n16 = tt // 16
    hb = tt // 8

    def body(g_ref, gp_ref, u_ref, up_ref, wg_ref, wu_ref, bg_ref, bu_ref, gn_ref, un_ref, da_ref, dan_ref,
             og_ref, ou_ref, ag_ref, au_ref, accs):
        i = pl.program_id(1)
        first, last = i == 0, i == nt - 1
        rid = lax.broadcasted_iota(jnp.int32, (8, tc), 0)
        accs[...] = jnp.zeros_like(accs)

        def dgroup(pg, cg, pu, cu, da):
            gate = _conv3_group(pg, cg, wg_ref, bg_ref, rid)
            up = _conv3_group(pu, cu, wu_ref, bu_ref, rid)
            sv, sgr = _silu_and_grad(gate)
            return da * up * sgr, da * sv

        def finish(x, d0, d1, w_ref):
            s1 = _shift_up(d0, d1, 1, rid)
            s2 = _shift_up(d0, d1, 2, rid)
            dpre = w_ref[2:3, :] * d0 + w_ref[1:2, :] * s1 + w_ref[0:1, :] * s2
            return dpre, (x * s2, x * s1, x * d0, d0)

        def two_groups(it, carry, x2g, x2u, da_blk, zero_ahead):
            x0g, x0u, d0g, d0u, da1 = carry
            r0 = it * 16 if isinstance(it, int) else pl.multiple_of(it * 16, 16)
            x1g, x1u = g_ref[pl.ds(r0 + 8, 8), :], u_ref[pl.ds(r0 + 8, 8), :]
            d1g, d1u = dgroup(x0g, x1g, x0u, x1u, da1)
            d2g, d2u = dgroup(x1g, x2g, x1u, x2u, da_blk[0:8])
            d2g = jnp.where(zero_ahead, 0.0, d2g)
            d2u = jnp.where(zero_ahead, 0.0, d2u)
            outs = []
            for half, (xa, xb, da_, db_, dc_, w_ref, o_ref) in enumerate((
                    (x0g, x1g, d0g, d1g, d2g, wg_ref, og_ref), (x0u, x1u, d0u, d1u, d2u, wu_ref, ou_ref))):
                pa, prods_a = finish(xa, da_, db_, w_ref)
                pb, prods_b = finish(xb, db_, dc_, w_ref)
                o_ref[pl.ds(r0, 16), :] = jnp.concatenate([pa, pb], axis=0).astype(BF16)
                for k in range(4):
                    accs[4 * half + k] += prods_a[k] + prods_b[k]
            return x2g, x2u, d2g, d2u, da_blk[8:16]

        def step(it, carry):
            r1 = pl.multiple_of(it * 16 + 16, 16)
            return two_groups(it, carry, g_ref[pl.ds(r1, 8), :], u_ref[pl.ds(r1, 8), :],
                              da_ref[pl.ds(r1, 16), :].astype(F32), False)

        da0 = da_ref[pl.ds(0, 16), :].astype(F32)
        x0g, x0u = g_ref[pl.ds(0, 8), :], u_ref[pl.ds(0, 8), :]
        d0g, d0u = dgroup(jnp.where(first, 0.0, gp_ref[...]), x0g, jnp.where(first, 0.0, up_ref[...]), x0u, da0[0:8])
        carry = lax.fori_loop(0, n16 - 1, step, (x0g, x0u, d0g, d0u, da0[8:16]))
        two_groups(n16 - 1, carry, gn_ref[...], un_ref[...], dan_ref[...].astype(F32), last)

        @pl.when(first)
        def _():
            ag_ref[...] = jnp.zeros_like(ag_ref)
            au_ref[...] = jnp.zeros_like(au_ref)

        for half, a_ref in enumerate((ag_ref, au_ref)):
            for k in range(4):
                a_ref[k:k + 1, :] += jnp.sum(accs[4 * half + k], axis=0, keepdims=True)

    def nxt8(i):
        return jnp.minimum((i + 1) * hb, t // 8 - 1)

    def nxt16(i):
        return jnp.minimum((i + 1) * n16, t // 16 - 1)

    out_blk = pl.BlockSpec((tt, tc), lambda j, i: (i, j))
    acc_spec = pl.BlockSpec((8, tc), lambda j, i: (0, j))
    in_specs = _ffn_specs(t, tt, tc, nf) + [
        pl.BlockSpec((8, tc), lambda j, i: (nxt8(i), j)), pl.BlockSpec((8, tc), lambda j, i: (nxt8(i), j + nf)),
        out_blk, pl.BlockSpec((16, tc), lambda j, i: (nxt16(i), j))]
    return pl.pallas_call(
        body, name="ffn_act_bwd", grid=(nf, nt), in_specs=in_specs,
        out_specs=[out_blk, out_blk, acc_spec, acc_spec],
        out_shape=[jax.ShapeDtypeStruct((t, D_FF), BF16), jax.ShapeDtypeStruct((t, D_FF), BF16),
                   jax.ShapeDtypeStruct((8, D_FF), F32), jax.ShapeDtypeStruct((8, D_FF), F32)],
        scratch_shapes=[pltpu.VMEM((8, 8, tc), F32)],
        compiler_params=_cparams(("parallel", "arbitrary")),
    )(hid, hid, hid, hid, cw, cw, cb, cb, hid, hid, dact, dact)


def _tri_mask():
    r = lax.broadcasted_iota(jnp.int32, (CHUNK, CHUNK), 0)
    c = lax.broadcasted_iota(jnp.int32, (CHUNK, CHUNK), 1)
    return r >= c


def _gmlp_group_fwd(uv_ref, lng_ref, lnb_ref, ws_ref, bexp_ref, tri, g, want_grad):
    lo, hi = g * 128, (g + 1) * 128
    u_pre = uv_ref[:, lo:hi]
    v_pre = uv_ref[:, D_A + lo:D_A + hi]
    u, du = _gelu_and_grad(u_pre)
    v, dv = _gelu_and_grad(v_pre)
    mu = jnp.mean(v, axis=-1, keepdims=True)
    dc = v - mu
    rs = lax.rsqrt(jnp.mean(dc * dc, axis=-1, keepdims=True) + EPS)
    xh = dc * rs
    vn = (xh * lng_ref[:, lo:hi] + lnb_ref[:, lo:hi]).astype(BF16)
    w = jnp.where(tri, ws_ref[g], 0.0).astype(BF16)
    sg = _dot(w, vn) + bexp_ref[g]
    if want_grad:
        return u, du, dv, rs, xh, vn, w, sg
    return u * sg


def _gmlp_fwd(proj, ln_g, ln_b, w_s, b_exp, na_g):
    t = proj.shape[0]
    ng = N_GROUPS_A

    def body(uv_ref, lng_ref, lnb_ref, ws_ref, bexp_ref, nag_ref, o_ref):
        tri = _tri_mask()
        ys = [_gmlp_group_fwd(uv_ref, lng_ref, lnb_ref, ws_ref, bexp_ref, tri, g, False) for g in range(ng)]
        ssq = ys[0] * 0.0
        for y in ys:
            ssq = ssq + y * y
        r = lax.rsqrt(jnp.sum(ssq, axis=-1, keepdims=True) * (1.0 / D_A) + EPS)
        for g, y in enumerate(ys):
            o_ref[:, g * 128:(g + 1) * 128] = (y * r * nag_ref[:, g * 128:(g + 1) * 128]).astype(BF16)

    vec = pl.BlockSpec((1, D_A), lambda i: (0, 0))
    cube = pl.BlockSpec((ng, CHUNK, CHUNK), lambda i: (0, 0, 0))
    return pl.pallas_call(
        body, name="gmlp_fwd", grid=(t // CHUNK,),
        in_specs=[pl.BlockSpec((CHUNK, 2 * D_A), lambda i: (i, 0)), vec, vec, cube, cube, vec],
        out_specs=pl.BlockSpec((CHUNK, D_A), lambda i: (i, 0)),
        out_shape=jax.ShapeDtypeStruct((t, D_MIX), BF16),
        compiler_params=_cparams(("parallel",)),
    )(proj, ln_g, ln_b, w_s, b_exp, na_g)


def _gmlp_bwd(proj, dyab, ln_g, ln_b, w_s, b_exp, na_g):
    t = proj.shape[0]
    ng = N_GROUPS_A
    nsteps = t // CHUNK

    def body(uv_ref, dy_ref, lng_ref, lnb_ref, ws_ref, bexp_ref, nag_ref,
             duv_ref, dws_ref, dbs_ref, dlng_ref, dlnb_ref, dnag_ref, dbacc):
        i = pl.program_id(0)
        tri = _tri_mask()

        @pl.when(i == 0)
        def _():
            dws_ref[...] = jnp.zeros_like(dws_ref)
            dbacc[...] = jnp.zeros_like(dbacc)
            dlng_ref[...] = jnp.zeros_like(dlng_ref)
            dlnb_ref[...] = jnp.zeros_like(dlnb_ref)
            dnag_ref[...] = jnp.zeros_like(dnag_ref)

        st = [_gmlp_group_fwd(uv_ref, lng_ref, lnb_ref, ws_ref, bexp_ref, tri, g, True) for g in range(ng)]
        ssq = st[0][0] * 0.0
        for s in st:
            y = s[0] * s[7]
            ssq = ssq + y * y
        r = lax.rsqrt(jnp.sum(ssq, axis=-1, keepdims=True) * (1.0 / D_A) + EPS)
        csum = st[0][0] * 0.0
        for g, s in enumerate(st):
            sl = slice(g * 128, (g + 1) * 128)
            xhy = s[0] * s[7] * r
            dya = dy_ref[:, sl]
            dnag_ref[:, sl] += jnp.sum(dya * xhy, axis=0, keepdims=True)
            csum = csum + dya * nag_ref[:, sl] * xhy
        c1 = jnp.sum(csum, axis=-1, keepdims=True) * (1.0 / D_A)
        for g, s in enumerate(st):
            u, du, dv, rs, xh, vn, w, sg = s
            sl = slice(g * 128, (g + 1) * 128)
            dy = r * (dy_ref[:, sl] * nag_ref[:, sl] - u * sg * r * c1)
            dsg = dy * u
            dsg_b = dsg.astype(BF16)
            dws_ref[g] += _dot_nt(dsg_b, vn)
            dbacc[g] += dsg
            dvn = _dot_tn(w, dsg_b)
            dlnb_ref[:, sl] += jnp.sum(dvn, axis=0, keepdims=True)
            dlng_ref[:, sl] += jnp.sum(dvn * xh, axis=0, keepdims=True)
            dxh = dvn * lng_ref[:, sl]
            dvv = rs * (dxh - jnp.mean(dxh, axis=-1, keepdims=True)
                        - xh * jnp.mean(dxh * xh, axis=-1, keepdims=True))
            duv_ref[:, sl] = (dy * sg * du).astype(BF16)
            duv_ref[:, D_A + g * 128:D_A + (g + 1) * 128] = (dvv * dv).astype(BF16)

        @pl.when(i == nsteps - 1)
        def _():
            for g in range(ng):
                dws_ref[g] = jnp.where(tri, dws_ref[g], 0.0)
                dbs_ref[g] = jnp.sum(dbacc[g], axis=-1, keepdims=True)

    vec = pl.BlockSpec((1, D_A), lambda i: (0, 0))
    cube = pl.BlockSpec((ng, CHUNK, CHUNK), lambda i: (0, 0, 0))
    return pl.pallas_call(
        body, name="gmlp_bwd", grid=(nsteps,),
        in_specs=[pl.BlockSpec((CHUNK, 2 * D_A), lambda i: (i, 0)),
                  pl.BlockSpec((CHUNK, D_A), lambda i: (i, 0)), vec, vec, cube, cube, vec],
        out_specs=[pl.BlockSpec((CHUNK, 2 * D_A), lambda i: (i, 0)), cube,
                   pl.BlockSpec((ng, CHUNK, 1), lambda i: (0, 0, 0)), vec, vec, vec],
        out_shape=[jax.ShapeDtypeStruct((t, 2 * D_A), BF16), jax.ShapeDtypeStruct((ng, CHUNK, CHUNK), F32),
                   jax.ShapeDtypeStruct((ng, CHUNK, 1), F32), jax.ShapeDtypeStruct((1, D_A), F32),
                   jax.ShapeDtypeStruct((1, D_A), F32), jax.ShapeDtypeStruct((1, D_A), F32)],
        scratch_shapes=[pltpu.VMEM((ng, CHUNK, CHUNK), F32)],
        compiler_params=_cparams(("arbitrary",)),
    )(proj, dyab, ln_g, ln_b, w_s, b_exp, na_g)


OFF_Z = 2 * D_A
OFF_XS = OFF_Z + D_SSM
OFF_B = OFF_XS + D_SSM
OFF_C = OFF_B + D_BC


def _ssd_consts():
    tri = jnp.tril(jnp.ones((CHUNK, CHUNK), F32)).astype(BF16)
    h = jnp.arange(HPAD)[None, :, None]
    g = jnp.arange(N_SSM_GROUPS)[:, None, None]
    j1 = jnp.arange(GW)[None, None, :]
    eh = (h == g * HEADS_PER_GROUP + j1 // HEAD_DIM).astype(BF16)
    j2 = jnp.arange(HEADS_PER_GROUP * 128)[None, None, :]
    e128 = (h == g * HEADS_PER_GROUP + j2 // 128).astype(BF16)
    return tri, eh, e128


def _ssd_in_specs(cmap):
    def rows(i):
        return cmap(i)

    def prev8(i):
        return jnp.maximum(cmap(i) * (CHUNK // 8) - 1, 0)

    def whole(*shape):
        return pl.BlockSpec(shape, lambda i: (0,) * len(shape))

    bcw = 2 * D_BC
    specs = [
        pl.BlockSpec((CHUNK, D_SSM), lambda i: (rows(i), OFF_Z // D_SSM)),
        pl.BlockSpec((CHUNK, D_SSM), lambda i: (rows(i), OFF_XS // D_SSM)),
        pl.BlockSpec((8, D_SSM), lambda i: (prev8(i), OFF_XS // D_SSM)),
        pl.BlockSpec((CHUNK, bcw), lambda i: (rows(i), OFF_B // bcw)),
        pl.BlockSpec((8, bcw), lambda i: (prev8(i), OFF_B // bcw)),
        pl.BlockSpec((CHUNK, HPAD), lambda i: (rows(i), 0)),
        whole(SSM_CONV, D_XBC), whole(1, D_XBC),
        whole(1, HPAD), whole(1, HPAD),
        whole(1, D_SSM), whole(1, D_SSM),
        whole(CHUNK, CHUNK),
        whole(N_SSM_GROUPS, HPAD, GW), whole(N_SSM_GROUPS, HPAD, HEADS_PER_GROUP * 128),
    ]
    return specs


N_SSD_IN = 15


def _lanes(ref, start, width):
    return ref.at[:, pl.ds(pl.multiple_of(start, 128), width)]


def _ssd_group_refs(ins, g):
    (z_ref, xs_ref, xsp_ref, bc_ref, bcp_ref, dt_ref, cw_ref, cb_ref, dtb_ref, alog_ref, de_ref, gain_ref,
     tri_ref, eh_ref, e128_ref) = ins
    ox, ob, oc = g * GW, g * D_STATE, D_BC + g * D_STATE
    return (_lanes(z_ref, ox, GW), _lanes(xs_ref, ox, GW), _lanes(xsp_ref, ox, GW),
            _lanes(bc_ref, ob, D_STATE), _lanes(bcp_ref, ob, D_STATE),
            _lanes(bc_ref, oc, D_STATE), _lanes(bcp_ref, oc, D_STATE), dt_ref,
            _lanes(cw_ref, ox, GW), _lanes(cw_ref, D_SSM + ob, D_STATE), _lanes(cw_ref, D_SSM + oc, D_STATE),
            _lanes(cb_ref, ox, GW), _lanes(cb_ref, D_SSM + ob, D_STATE), _lanes(cb_ref, D_SSM + oc, D_STATE),
            dtb_ref, alog_ref, _lanes(de_ref, ox, GW), _lanes(gain_ref, ox, GW), tri_ref,
            eh_ref.at[g], e128_ref.at[g])


N_SSD_OWN = 4


def _ssd_scratch():
    return [pltpu.VMEM((CHUNK + 8, GW), F32), pltpu.VMEM((CHUNK + 8, D_STATE), F32),
            pltpu.VMEM((CHUNK + 8, D_STATE), F32), pltpu.VMEM((CHUNK, GW), F32),
            pltpu.VMEM((HPAD, CHUNK), F32), pltpu.VMEM((CHUNK, HPAD), F32), pltpu.VMEM((CHUNK, HPAD), F32)]


def _ssd_two_sets(alloc):
    a, b = alloc[:7], alloc[7:7 + N_SSD_OWN]
    return ((a[0], a[1], a[2], a[4], a[3], a[5], a[6]), (b[0], b[1], b[2], a[4], b[3], a[5], a[6]))


def _ssd_chunk(ins, scr):
    dt_ref, dtb_ref, alog_ref, tri_ref = ins[5], ins[8], ins[9], ins[12]
    acst_sc, dt_sc, acs_sc = scr[3], scr[5], scr[6]
    dt = _softplus(dt_ref[...] + dtb_ref[...])
    acs = _e01x(tri_ref[...], dt * (-jnp.exp(alog_ref[...])))
    dt_sc[...] = dt
    acs_sc[...] = acs
    acst_sc[...] = acs.T


def _ssd_pre(first, g, refs, scr):
    (z_ref, xs_ref, xsp_ref, b_ref, bp_ref, c_ref, cp_ref, dt_ref, cwx, cwb, cwc, cbx, cbb, cbc,
     dtb_ref, alog_ref, de_ref, gain_ref, tri_ref, eh_ref, e128_ref) = refs
    ext_x, ext_b, ext_c, acst_sc, acse_sc, dt_sc, acs_sc = scr
    p = {}
    px = _conv(xs_ref, xsp_ref, cwx, cbx, ext_x, first, SSM_CONV)
    pb = _conv(b_ref, bp_ref, cwb, cbb, ext_b, first, SSM_CONV)
    pc = _conv(c_ref, cp_ref, cwc, cbc, ext_c, first, SSM_CONV)
    p['xs'], p['dsx'] = _silu_and_grad(px)
    p['bm'], p['dsb'] = _silu_and_grad(pb)
    p['cm'], p['dsc'] = _silu_and_grad(pc)
    p['dt_in'] = dt_ref[...] + dtb_ref[...]
    dt = dt_sc[...]
    p['dt'] = dt
    p['a'] = -jnp.exp(alog_ref[...])
    tri_b = tri_ref[...]
    acs = acs_sc[...]
    eh = eh_ref[...]
    p['eh'] = eh
    p['dt_e'] = _x01(dt, eh)
    acs_e = _x01(acs, eh)
    acse_sc[...] = acs_e
    p['acs_e'] = acs_e
    p['acs_c'] = _x01(acs, e128_ref[...])
    p['acs_last_e'] = acse_sc[pl.ds(CHUNK - 1, 1), :]
    p['xdt'] = p['xs'] * p['dt_e']
    p['decay_e'] = jnp.exp(p['acs_last_e'] - acs_e)
    p['cm_b'] = p['cm'].astype(BF16)
    p['bm_b'] = p['bm'].astype(BF16)
    p['scores'] = _dot_nt(p['cm_b'], p['bm_b'])
    p['tri_b'] = tri_b
    return p


def _ssd_l(p, g, r, acst_sc, tri):
    col = p['acs_c'][:, r * 128:(r + 1) * 128]
    row = acst_sc[pl.ds(g * HEADS_PER_GROUP + r, 1), :]
    return jnp.exp(jnp.where(tri, col - row, -1e30))


def _ssd_fwd(proj, dt_raw, yab, cw, cb, dtb, alog, de, gain, consts, hosted):
    t = proj.shape[0]
    nc = t // CHUNK
    ng = N_SSM_GROUPS
    tri_c, eh_c, e128_c = consts
    n_in = N_SSD_IN
    nh_in, nh_out = len(hosted['arrays']), len(hosted['out_shape'])
    mid_chunk = (7 * nc) // 8

    def body(*refs):
        ins = refs[:n_in]
        hins = refs[n_in + 1:n_in + 1 + nh_in]
        o0 = n_in + 1 + nh_in
        yb_all, ys_all, hs_all = refs[o0:o0 + 3]
        houts = refs[o0 + 3:o0 + 3 + nh_out]
        h_sc = refs[o0 + 3 + nh_out]
        scr_a, scr_b = _ssd_two_sets(refs[o0 + 4 + nh_out:-2])
        sems = refs[-2:]
        c = pl.program_id(0)
        _host_phase(hosted, 'start', c == 0, hins, houts, sems)
        _host_phase(hosted, 'mid', c == mid_chunk, hins, houts, sems)

        @pl.when(c == 0)
        def _():
            h_sc[...] = jnp.zeros_like(h_sc)

        def group(g, scr):
            grefs = _ssd_group_refs(ins, g)
            z_ref, de_ref, gain_ref = grefs[0], grefs[16], grefs[17]
            yb_ref, ys_ref = _lanes(yb_all, g * GW, GW), _lanes(ys_all, g * GW, GW)
            slab = pl.ds(pl.multiple_of(g * D_STATE, D_STATE), D_STATE)
            p = _ssd_pre(c == 0, g, grefs, scr)
            tri = _tri_mask()
            h_in = h_sc[slab, :]
            hs_all[slab, :] = h_in
            yoff = _dot(p['cm_b'], h_in.astype(BF16)) * jnp.exp(p['acs_e'])
            states = _dot_tn(p['bm_b'], (p['xdt'] * p['decay_e']).astype(BF16))
            lane = lax.broadcasted_iota(jnp.int32, (CHUNK, 128), 1)
            slabs = []
            for r2 in range(HEADS_PER_GROUP // 2):
                xb = p['xdt'][:, r2 * 128:(r2 + 1) * 128].astype(BF16)
                ya = _dot((p['scores'] * _ssd_l(p, g, 2 * r2, scr[3], tri)).astype(BF16), xb)
                yb = _dot((p['scores'] * _ssd_l(p, g, 2 * r2 + 1, scr[3], tri)).astype(BF16), xb)
                slabs.append(jnp.where(lane < HEAD_DIM, ya, yb))
            y = jnp.concatenate(slabs, axis=1) + yoff + de_ref[...] * p['xs']
            ys_ref[...] = y
            h_sc[slab, :] = jnp.exp(p['acs_last_e']) * h_in + states
            zv = z_ref[...]
            yg = y * zv * _sigmoid(zv)
            r = lax.rsqrt(jnp.mean(yg * yg, axis=-1, keepdims=True) + EPS)
            yb_ref[...] = (yg * r * gain_ref[...]).astype(BF16)

        def pair(j, carry):
            group(2 * j, scr_a)
            group(2 * j + 1, scr_b)
            return carry

        _ssd_chunk(ins, scr_a)
        lax.fori_loop(0, ng // 2, pair, 0)
        _host_phase(hosted, 'finish', c == nc - 1, hins, houts, sems)

    h_in, h_out, h_shape, h_scratch, h_alias = _host_plumbing(hosted, n_in + 1, 3)
    in_specs = _ssd_in_specs(lambda i: i) + [pl.BlockSpec(memory_space=pl.ANY)] + h_in
    out_specs = [pl.BlockSpec((CHUNK, D_SSM), lambda i: (i, D_A // D_SSM)),
                 pl.BlockSpec((CHUNK, D_SSM), lambda i: (i, 0)),
                 pl.BlockSpec((None, ng * D_STATE, GW), lambda i: (i, 0, 0))] + h_out
    out_shape = [jax.ShapeDtypeStruct((t, D_MIX), BF16), jax.ShapeDtypeStruct((t, D_SSM), F32),
                 jax.ShapeDtypeStruct((nc, ng * D_STATE, GW), F32)] + h_shape
    return pl.pallas_call(
        body, name="ssd_fwd", grid=(nc,), in_specs=in_specs, out_specs=out_specs, out_shape=out_shape,
        scratch_shapes=[pltpu.VMEM((ng * D_STATE, GW), F32)] + _ssd_scratch() + _ssd_scratch()[:N_SSD_OWN]
        + h_scratch,
        input_output_aliases={n_in: 0, **h_alias},
        compiler_params=_cparams(("arbitrary",)),
    )(proj, proj, proj, proj, proj, dt_raw, cw, cb, dtb, alog, de, gain,
      tri_c, eh_c, e128_c, yab, *hosted['arrays'])


def _rows8(vals, width):
    rid = lax.broadcasted_iota(jnp.int32, (8, width), 0)
    out = jnp.zeros((8, width), F32)
    for k, v in enumerate(vals):
        if v is not None:
            out = out + jnp.where(rid == k, v, 0.0)
    return out


def _ssd_bwd(proj, dt_raw, dyab, ysave, hs, cw, cb, dtb, alog, de, gain, consts, hosted):
    t = proj.shape[0]
    nc = t // CHUNK
    ng = N_SSM_GROUPS
    tri_c, eh_c, e128_c = consts
    n_in = N_SSD_IN
    nh_in, nh_out = len(hosted['arrays']), len(hosted['out_shape'])

    def per_group(g, c, ins, dy_ref, ys_ref, hs_ref, outs, scratch):
        dz_ref, dxs_ref, db_ref, dc_ref, ddt_ref, acc_x, acc_b, acc_c, acc_gain, acc_head = outs
        dh_sc, car_x, car_b, car_c, dext_x, dext_b, dext_c = scratch[:7]
        scr = scratch[7:]
        ext_x, ext_b, ext_c, acst_sc = scr[:4]
        z_ref, cwx, cwb, cwc, de_ref, gain_ref = ins[0], ins[8], ins[9], ins[10], ins[16], ins[17]
        slab = pl.ds(pl.multiple_of(g * D_STATE, D_STATE), D_STATE)
        p = _ssd_pre(c == 0, g, ins, scr)
        tri = _tri_mask()
        xs, dt_e, acs_e, xdt, decay_e = p['xs'], p['dt_e'], p['acs_e'], p['xdt'], p['decay_e']
        cm_b, bm_b, scores, eh = p['cm_b'], p['bm_b'], p['scores'], p['eh']
        h_in = hs_ref[...]
        h_in_b = h_in.astype(BF16)
        e_a = jnp.exp(acs_e)
        raw = _dot(cm_b, h_in_b)

        y = ys_ref[...]
        zv = z_ref[...]
        sz, dsz = _silu_and_grad(zv)
        yg = y * sz
        r = lax.rsqrt(jnp.mean(yg * yg, axis=-1, keepdims=True) + EPS)
        xh = yg * r
        dout = dy_ref[...]
        gain = gain_ref[...]
        dxh = dout * gain
        dyg = r * (dxh - xh * jnp.mean(dxh * xh, axis=-1, keepdims=True))
        dy = dyg * sz
        dz_ref[...] = (dyg * y * dsz).astype(BF16)
        acc_gain[g] += _rows8([jnp.sum(dout * xh, axis=0, keepdims=True)], GW)
        d_skip8 = _x01_nt(_rows8([None, None, jnp.sum(dy * xs, axis=0, keepdims=True)], GW), eh)
        dxs = de_ref[...] * dy

        q = dy * raw * e_a
        draw = (dy * e_a).astype(BF16)
        d_c = _dot_nt(draw, h_in_b)
        dh_in = _dot_tn(cm_b, draw)

        lane = lax.broadcasted_iota(jnp.int32, (CHUNK, 128), 1)
        ones_b = jnp.ones((CHUNK, 128), BF16)
        dscores = jnp.zeros((CHUNK, CHUNK), F32)
        dxdt_slabs, q_slabs = [], []
        for r2 in range(HEADS_PER_GROUP // 2):
            sl = slice(r2 * 128, (r2 + 1) * 128)
            xb = xdt[:, sl].astype(BF16)
            dys = dy[:, sl]
            dys_b = dys.astype(BF16)
            dxh_pair, qv_pair = [], []
            for half in range(2):
                lmat = _ssd_l(p, g, 2 * r2 + half, acst_sc, tri)
                m = scores * lmat
                mine = (lane < HEAD_DIM) if half == 0 else (lane >= HEAD_DIM)
                dm = _dot_nt(jnp.where(mine, dys, 0.0).astype(BF16), xb)
                dscores = dscores + dm * lmat
                gm = dm * m
                dxh_pair.append(_dot_tn(m.astype(BF16), dys_b))
                h3 = _split3(gm)
                colsum = _dot_tn(h3[0], ones_b) + _dot_tn(h3[1], ones_b) + _dot_tn(h3[2], ones_b)
                qv_pair.append(jnp.sum(gm, axis=-1, keepdims=True) - colsum)
            dxdt_slabs.append(jnp.where(lane < HEAD_DIM, dxh_pair[0], dxh_pair[1]))
            q_slabs.append(jnp.where(lane == 0, qv_pair[0], 0.0) + jnp.where(lane == HEAD_DIM, qv_pair[1], 0.0))
        dxdt = jnp.concatenate(dxdt_slabs, axis=1)
        q = q + jnp.concatenate(q_slabs, axis=1)

        dh_out = dh_sc[slab, :]
        dh_out_b = dh_out.astype(BF16)
        e_l = jnp.exp(p['acs_last_e'])
        dh_sc[slab, :] = dh_in + e_l * dh_out
        dlast = jnp.sum(dh_out * h_in, axis=0, keepdims=True) * e_l
        dxd = _dot(bm_b, dh_out_b)
        xd = xdt * decay_e
        dxdt = dxdt + dxd * decay_e
        dd = dxd * xd
        q = q - dd
        dlast = dlast + jnp.sum(dd, axis=0, keepdims=True)
        d_b = _dot_nt(xd.astype(BF16), dh_out_b)
        dsc_b = dscores.astype(BF16)
        d_c = d_c + _dot(dsc_b, bm_b)
        d_b = d_b + _dot_tn(dsc_b, cm_b)

        dxs = dxs + dxdt * dt_e
        rid = lax.broadcasted_iota(jnp.int32, (CHUNK, GW), 0)
        q = q + jnp.where(rid == CHUNK - 1, dlast, 0.0)
        dacs = _x01_nt(q, eh)
        ddt = _x01_nt(dxdt * xs, eh)
        dadt = _e01x_tn(p['tri_b'], dacs)
        ddt = ddt + dadt * p['a']
        d_a = jnp.sum(dadt * p['dt'], axis=0, keepdims=True)
        ddt_raw = ddt * _sigmoid(p['dt_in'])
        acc_head[...] += _rows8([jnp.sum(ddt_raw, axis=0, keepdims=True), d_a * p['a']], HPAD) + d_skip8

        @pl.when(g == 0)
        def _():
            ddt_ref[...] = ddt_raw

        @pl.when(g > 0)
        def _():
            ddt_ref[...] += ddt_raw

        for dv, dsil, ext, dext, car, acc, w_ref, o_ref in (
                (dxs, p['dsx'], ext_x, dext_x, car_x, acc_x, cwx, dxs_ref),
                (d_b, p['dsb'], ext_b, dext_b, car_b, acc_b, cwb, db_ref),
                (d_c, p['dsc'], ext_c, dext_c, car_c, acc_c, cwc, dc_ref)):
            dp = dv * dsil
            width = dp.shape[1]
            rows = [jnp.sum(ext[pl.ds(5 + k, CHUNK), :] * dp, axis=0, keepdims=True) for k in range(SSM_CONV)]
            rows.append(jnp.sum(dp, axis=0, keepdims=True))
            acc[g] += _rows8(rows, width)
            dext[0:CHUNK, :] = dp
            dext[CHUNK:CHUNK + 8, :] = car[g]
            car[g] = dext[0:8, :]
            dx = w_ref[SSM_CONV - 1:SSM_CONV, :] * dext[pl.ds(0, CHUNK), :]
            for k in range(SSM_CONV - 1):
                dx = dx + w_ref[k:k + 1, :] * dext[pl.ds(SSM_CONV - 1 - k, CHUNK), :]
            o_ref[...] = dx.astype(BF16)

    def body(*refs):
        ins = refs[:n_in]
        dy_all, ys_all, hs_all = refs[n_in:n_in + 3]
        hins = refs[n_in + 3:n_in + 3 + nh_in]
        o0 = n_in + 3 + nh_in
        dz_all, dxs_all, db_all, dc_all, ddt_ref = refs[o0:o0 + 5]
        accs = refs[o0 + 5:o0 + 10]
        houts = refs[o0 + 10:o0 + 10 + nh_out]
        scratch = refs[o0 + 10 + nh_out:-2]
        sems = refs[-2:]
        cc = pl.program_id(0)
        _host_phase(hosted, 'start', cc == 0, hins, houts, sems)

        @pl.when(cc == 0)
        def _():
            for a in tuple(accs) + tuple(scratch[:4]):
                a[...] = jnp.zeros_like(a)

        shared = tuple(scratch[:4])
        scr_a, scr_b = _ssd_two_sets(tuple(scratch[7:14]) + tuple(scratch[17:17 + N_SSD_OWN]))
        set_a = shared + tuple(scratch[4:7]) + scr_a
        set_b = shared + tuple(scratch[14:17]) + scr_b

        def group(g, own):
            slab = pl.ds(pl.multiple_of(g * D_STATE, D_STATE), D_STATE)
            outs = (_lanes(dz_all, g * GW, GW), _lanes(dxs_all, g * GW, GW), _lanes(db_all, g * D_STATE, D_STATE),
                    _lanes(dc_all, g * D_STATE, D_STATE), ddt_ref) + tuple(accs)
            per_group(g, nc - 1 - cc, _ssd_group_refs(ins, g), _lanes(dy_all, g * GW, GW),
                      _lanes(ys_all, g * GW, GW), hs_all.at[slab, :], outs, own)

        def pair(j, carry):
            group(2 * j, set_a)
            group(2 * j + 1, set_b)
            return carry

        _ssd_chunk(ins, scr_a)
        lax.fori_loop(0, ng // 2, pair, 0)
        _host_phase(hosted, 'finish', cc == nc - 1, hins, houts, sems)

    def cmap(i):
        return nc - 1 - i

    in_specs = _ssd_in_specs(cmap) + [
        pl.BlockSpec((CHUNK, D_SSM), lambda i: (cmap(i), D_A // D_SSM)),
        pl.BlockSpec((CHUNK, D_SSM), lambda i: (cmap(i), 0)),
        pl.BlockSpec((None, ng * D_STATE, GW), lambda i: (cmap(i), 0, 0)),
    ]

    def full(shape):
        return pl.BlockSpec(shape, lambda i: (0,) * len(shape))

    out_specs = [
        pl.BlockSpec((CHUNK, D_SSM), lambda i: (cmap(i), 0)),
        pl.BlockSpec((CHUNK, D_SSM), lambda i: (cmap(i), 0)),
        pl.BlockSpec((CHUNK, D_BC), lambda i: (cmap(i), 0)),
        pl.BlockSpec((CHUNK, D_BC), lambda i: (cmap(i), 0)),
        pl.BlockSpec((CHUNK, HPAD), lambda i: (cmap(i), 0)),
        full((ng, 8, GW)), full((ng, 8, D_STATE)), full((ng, 8, D_STATE)),
        full((ng, 8, GW)), full((8, HPAD)),
    ]
    out_shape = [
        jax.ShapeDtypeStruct((t, D_SSM), BF16), jax.ShapeDtypeStruct((t, D_SSM), BF16),
        jax.ShapeDtypeStruct((t, D_BC), BF16), jax.ShapeDtypeStruct((t, D_BC), BF16),
        jax.ShapeDtypeStruct((t, HPAD), F32),
        jax.ShapeDtypeStruct((ng, 8, GW), F32), jax.ShapeDtypeStruct((ng, 8, D_STATE), F32),
        jax.ShapeDtypeStruct((ng, 8, D_STATE), F32), jax.ShapeDtypeStruct((ng, 8, GW), F32),
        jax.ShapeDtypeStruct((8, HPAD), F32),
    ]
    scratch = [pltpu.VMEM((ng * D_STATE, GW), F32),
               pltpu.VMEM((ng, 8, GW), F32), pltpu.VMEM((ng, 8, D_STATE), F32), pltpu.VMEM((ng, 8, D_STATE), F32),
               pltpu.VMEM((CHUNK + 8, GW), F32), pltpu.VMEM((CHUNK + 8, D_STATE), F32),
               pltpu.VMEM((CHUNK + 8, D_STATE), F32)] + _ssd_scratch()
    scratch += scratch[4:7] + _ssd_scratch()[:N_SSD_OWN]
    h_in, h_out, h_shape, h_scratch, h_alias = _host_plumbing(hosted, n_in + 3, len(out_shape))
    return pl.pallas_call(
        body, name="ssd_bwd", grid=(nc,), in_specs=in_specs + h_in, out_specs=out_specs + h_out,
        out_shape=out_shape + h_shape, scratch_shapes=scratch + h_scratch, input_output_aliases=h_alias,
        compiler_params=_cparams(("arbitrary",)),
    )(proj, proj, proj, proj, proj, dt_raw, cw, cb, dtb, alog, de, gain,
      tri_c, eh_c, e128_c, dyab, ysave, hs, *hosted['arrays'])


ANY = pl.BlockSpec(memory_space=pl.ANY)


def _place():
    x, y, c = lax.axis_index("x"), lax.axis_index("y"), lax.axis_index("c")
    chips = [(1 - x, y), (x, 1 - y), (1 - x, 1 - y)]
    return x, y, c, chips


def _rcopy(src, dst, send_sems, recv_sems, k, dev):
    return pltpu.make_async_remote_copy(src_ref=src, dst_ref=dst, send_sem=send_sems.at[k],
                                        recv_sem=recv_sems.at[k], device_id=dev, device_id_type=MESH)


def _my_chip():
    return 2 * lax.axis_index("x") + lax.axis_index("y")


def _cast_into_slot(w, name):
    r, c = w.shape
    tr = _row_tile(r, c, 2)

    def body(w_ref, o_ref):
        o_ref[...] = w_ref[...].astype(BF16)

    return pl.pallas_call(
        body, name=name, grid=(r // tr,), in_specs=[pl.BlockSpec((tr, c), lambda i: (i, 0))],
        out_specs=pl.BlockSpec((None, tr, c), lambda i: (_my_chip(), i, 0)),
        out_shape=jax.ShapeDtypeStruct((N_CHIPS, r, c), BF16), compiler_params=_cparams(("parallel",)),
    )(w)


def _hosted_gather(bigs):
    nb = len(bigs)

    def rows(a, c):
        half = bigs[a].shape[1] // 2
        return pl.ds(c * half, half)

    def start(ins, outs, send_sems, recv_sems):
        x, y, c, chips = _place()
        q = 2 * x + y
        for a in range(nb):
            for k, chip in enumerate(chips):
                _rcopy(outs[a].at[q, rows(a, c)], outs[a].at[q, rows(a, c)], send_sems, recv_sems, 6 * a + k,
                       (chip[0], chip[1], c)).start()

    def mid(ins, outs, send_sems, recv_sems):
        x, y, c, chips = _place()
        sib = (x, y, 1 - c)
        for a in range(nb):
            for k, chip in enumerate(chips):
                slab = outs[a].at[2 * chip[0] + chip[1], rows(a, c)]
                _rcopy(slab, slab, send_sems, recv_sems, 6 * a + k, sib).wait_recv()
                _rcopy(slab, slab, send_sems, recv_sems, 6 * a + 3 + k, sib).start()

    def finish(ins, outs, send_sems, recv_sems):
        x, y, c, chips = _place()
        q = 2 * x + y
        sib = (x, y, 1 - c)
        for a in range(nb):
            for k, chip in enumerate(chips):
                qk = 2 * chip[0] + chip[1]
                other = outs[a].at[qk, rows(a, 1 - c)]
                _rcopy(other, other, send_sems, recv_sems, 6 * a + 3 + k, sib).wait_recv()
                mine = outs[a].at[q, rows(a, c)]
                _rcopy(mine, mine, send_sems, recv_sems, 6 * a + k, sib).wait_send()
                fwd = outs[a].at[qk, rows(a, c)]
                _rcopy(fwd, fwd, send_sems, recv_sems, 6 * a + 3 + k, sib).wait_send()

    return dict(arrays=list(bigs), out_shape=[jax.ShapeDtypeStruct(b.shape, b.dtype) for b in bigs],
                aliases={a: a for a in range(nb)}, nsem=6 * nb, start=start, mid=mid, finish=finish)


def _hosted_rs_chips(ps):
    na = len(ps)

    def copies(ins, outs, send_sems, recv_sems):
        x, y, c, chips = _place()
        return [_rcopy(ins[a].at[2 * chip[0] + chip[1]], outs[a].at[k], send_sems, recv_sems, 3 * a + k,
                       (chip[0], chip[1], c)) for a in range(na) for k, chip in enumerate(chips)]

    def start(ins, outs, send_sems, recv_sems):
        for cp in copies(ins, outs, send_sems, recv_sems):
            cp.start()

    def finish(ins, outs, send_sems, recv_sems):
        for cp in copies(ins, outs, send_sems, recv_sems):
            cp.wait()

    return dict(arrays=list(ps), out_shape=[jax.ShapeDtypeStruct((3,) + p.shape[1:], p.dtype) for p in ps],
                aliases={}, nsem=3 * na, start=start, mid=None, finish=finish)


def _hosted_rs_sibling(gs):
    na = len(gs)

    def copies(ins, outs, send_sems, recv_sems):
        x, y, c, _ = _place()
        halves = [g.shape[1] // 2 for g in gs]
        return [_rcopy(ins[a].at[:, pl.ds((1 - c) * halves[a], halves[a]), :], outs[a], send_sems, recv_sems, a,
                       (x, y, 1 - c)) for a in range(na)]

    def start(ins, outs, send_sems, recv_sems):
        for cp in copies(ins, outs, send_sems, recv_sems):
            cp.start()

    def finish(ins, outs, send_sems, recv_sems):
        for cp in copies(ins, outs, send_sems, recv_sems):
            cp.wait()

    return dict(arrays=list(gs), aliases={}, nsem=na, start=start, mid=None, finish=finish,
                out_shape=[jax.ShapeDtypeStruct((N_CHIPS, g.shape[1] // 2, g.shape[2]), g.dtype) for g in gs])


def _host_plumbing(hosted, n_in, n_out):
    nh = len(hosted['arrays'])
    return ([ANY] * nh, [ANY] * len(hosted['out_shape']), list(hosted['out_shape']),
            [pltpu.SemaphoreType.DMA((hosted['nsem'],)), pltpu.SemaphoreType.DMA((hosted['nsem'],))],
            {n_in + a: n_out + b for a, b in hosted['aliases'].items()})


def _host_phase(hosted, phase, when, hins, houts, sems):
    fn = hosted[phase]
    if fn is None:
        return

    @pl.when(when)
    def _():
        fn(hins, houts, sems[0], sems[1])


def _gather_weights(bigs, smalls):
    nb, ns = len(bigs), len(smalls)
    na = nb + ns
    nsem = 6 * nb + 3 * ns
    big = _hosted_gather(bigs)

    def body(*refs):
        ins, outs = refs[:na], refs[na:2 * na]
        send_sems, recv_sems, loc_sems = refs[2 * na:]
        x, y, c, chips = _place()
        q = 2 * x + y
        sib = (x, y, 1 - c)
        locs, sends = [], []
        for s in range(ns):
            cp = pltpu.make_async_copy(ins[nb + s], outs[nb + s].at[q], loc_sems.at[s])
            cp.start()
            locs.append(cp)
        big['start'](ins[:nb], outs[:nb], send_sems, recv_sems)
        for s in range(ns):
            a = nb + s
            for k, chip in enumerate(chips):
                cp = _rcopy(ins[a], outs[a].at[q], send_sems, recv_sems, 6 * nb + 3 * s + k,
                            (chip[0], chip[1], c))
                cp.start()
                sends.append(cp)
        big['mid'](ins[:nb], outs[:nb], send_sems, recv_sems)
        big['finish'](ins[:nb], outs[:nb], send_sems, recv_sems)
        for s in range(ns):
            a = nb + s
            for k, chip in enumerate(chips):
                qk = 2 * chip[0] + chip[1]
                _rcopy(ins[a], outs[a].at[qk], send_sems, recv_sems, 6 * nb + 3 * s + k, sib).wait_recv()
        for cp in sends:
            cp.wait_send()
        for cp in locs:
            cp.wait()

    arrs = list(bigs) + list(smalls)
    out_shape = ([jax.ShapeDtypeStruct(a.shape, a.dtype) for a in bigs]
                 + [jax.ShapeDtypeStruct((N_CHIPS,) + a.shape, a.dtype) for a in smalls])
    return pl.pallas_call(
        body, name="gather_weights", in_specs=[ANY] * na, out_specs=[ANY] * na, out_shape=out_shape,
        input_output_aliases={a: a for a in range(nb)},
        scratch_shapes=[pltpu.SemaphoreType.DMA((nsem,)), pltpu.SemaphoreType.DMA((nsem,)),
                        pltpu.SemaphoreType.DMA((max(ns, 1),))],
    )(*arrs)


def _rs_sibling(gs, name):
    na = len(gs)

    def body(*refs):
        ins, recv_o = refs[:na], refs[na:2 * na]
        send_sems, recv_sems = refs[2 * na:]
        x, y, c, _ = _place()
        cps = []
        for a in range(na):
            half = gs[a].shape[1] // 2
            snd = _rcopy(ins[a].at[:, pl.ds((1 - c) * half, half), :], recv_o[a], send_sems, recv_sems, a,
                         (x, y, 1 - c))
            snd.start()
            cps.append(snd)
        for cp in cps:
            cp.wait()

    halves = [jax.ShapeDtypeStruct((N_CHIPS, g.shape[1] // 2, g.shape[2]), g.dtype) for g in gs]
    return pl.pallas_call(
        body, name=name, in_specs=[ANY] * na, out_specs=[ANY] * na, out_shape=halves,
        scratch_shapes=[pltpu.SemaphoreType.DMA((na,)), pltpu.SemaphoreType.DMA((na,))],
    )(*gs)


def _share_sibling(fs):
    na = len(fs)

    def body(*refs):
        outs = refs[na:2 * na]
        send_sems, recv_sems = refs[2 * na:]
        x, y, c, _ = _place()
        cps = []
        for a in range(na):
            half = fs[a].shape[0] // 2
            rows = pl.ds(c * half, half)
            snd = _rcopy(outs[a].at[rows], outs[a].at[rows], send_sems, recv_sems, a, (x, y, 1 - c))
            snd.start()
            cps.append(snd)
        for a, cp in enumerate(cps):
            half = fs[a].shape[0] // 2
            other = pl.ds((1 - c) * half, half)
            cp.wait_send()
            _rcopy(outs[a].at[other], outs[a].at[other], send_sems, recv_sems, a, (x, y, 1 - c)).wait_recv()

    return pl.pallas_call(
        body, name="share_sibling", in_specs=[ANY] * na, out_specs=[ANY] * na,
        out_shape=[jax.ShapeDtypeStruct(f.shape, f.dtype) for f in fs],
        input_output_aliases={a: a for a in range(na)},
        scratch_shapes=[pltpu.SemaphoreType.DMA((na,)), pltpu.SemaphoreType.DMA((na,))],
    )(*fs)


def _allgather_small(buf):
    def body(in_ref, out_ref, send_sems, recv_sems, loc_sem):
        x, y, c, _ = _place()
        me = 4 * x + 2 * y + c
        loc = pltpu.make_async_copy(in_ref, out_ref.at[me], loc_sem)
        loc.start()
        cps = [loc]
        for k in range(1, 8):
            dev = (1 - x if k & 4 else x, 1 - y if k & 2 else y, 1 - c if k & 1 else c)
            snd = _rcopy(in_ref, out_ref.at[me], send_sems, recv_sems, k - 1, dev)
            snd.start()
            cps.append(snd)
        for cp in cps:
            cp.wait()

    return pl.pallas_call(
        body, name="allgather_small", in_specs=[ANY], out_specs=ANY,
        out_shape=jax.ShapeDtypeStruct((8,) + buf.shape, buf.dtype),
        scratch_shapes=[pltpu.SemaphoreType.DMA((7,)), pltpu.SemaphoreType.DMA((7,)), pltpu.SemaphoreType.DMA],
    )(buf)


def _row_tile(rows, cols, nbuf):
    budget = 24 * 1024 * 1024 // (nbuf * cols * 4 * 2)
    return _tile(rows, max(16, budget - budget % 16), 16) if rows % 16 == 0 else rows


def _add_pairs(g, rcv, name):
    s, half, c = rcv.shape
    tr = _row_tile(half, c, 3)
    nh = half // tr

    def body(a_ref, b_ref, o_ref):
        o_ref[...] = (a_ref[...].astype(F32) + b_ref[...].astype(F32)).astype(o_ref.dtype)

    blk = pl.BlockSpec((None, tr, c), lambda j, i: (j, i, 0))
    mine = pl.BlockSpec((None, tr, c), lambda j, i: (j, lax.axis_index("c") * nh + i, 0))
    return pl.pallas_call(
        body, name=name, grid=(s, nh), in_specs=[mine, blk], out_specs=blk,
        out_shape=jax.ShapeDtypeStruct(rcv.shape, rcv.dtype), compiler_params=_cparams(("parallel", "parallel")),
    )(g, rcv)


def _sum_chips(part, rcv, name):
    _, half, c = part.shape
    tr = _row_tile(half, c, 5)
    nh = half // tr

    def body(o_ref, r_ref, out_ref):
        acc = o_ref[...].astype(F32)
        for k in range(3):
            acc = acc + r_ref[k].astype(F32)
        out_ref[...] = acc

    return pl.pallas_call(
        body, name=name, grid=(nh,),
        in_specs=[pl.BlockSpec((None, tr, c), lambda i: (_my_chip(), i, 0)),
                  pl.BlockSpec((3, tr, c), lambda i: (0, i, 0))],
        out_specs=pl.BlockSpec((tr, c), lambda i: (lax.axis_index("c") * nh + i, 0)),
        out_shape=jax.ShapeDtypeStruct((2 * half, c), F32), compiler_params=_cparams(("parallel",)),
    )(part, rcv)


def _sum_devices(parts):
    _, n, _ = parts.shape
    tr = _tile(n, 512, 8)

    def body(p_ref, o_ref):
        acc = p_ref[0]
        for k in range(1, 8):
            acc = acc + p_ref[k]
        o_ref[...] = acc

    return pl.pallas_call(
        body, name="sum_devices", grid=(n // tr,),
        in_specs=[pl.BlockSpec((8, tr, 128), lambda i: (0, i, 0))],
        out_specs=pl.BlockSpec((tr, 128), lambda i: (i, 0)),
        out_shape=jax.ShapeDtypeStruct((n, 128), F32), compiler_params=_cparams(("parallel",)),
    )(parts)


def _adamw(w, g, m, v, name):
    r, c = w.shape
    tr = _row_tile(r, c, 7)
    c1 = 1.0 - ADAM_B1 ** ADAM_STEP
    c2 = 1.0 - ADAM_B2 ** ADAM_STEP

    def body(w_ref, g_ref, m_ref, v_ref, d_ref, mo_ref, vo_ref):
        gv = g_ref[...]
        mn = ADAM_B1 * m_ref[...] + (1.0 - ADAM_B1) * gv
        vn = ADAM_B2 * v_ref[...] + (1.0 - ADAM_B2) * (gv * gv)
        mo_ref[...] = mn
        vo_ref[...] = vn
        m_hat = mn / c1
        v_hat = vn / c2
        d_ref[...] = -ADAM_LR * (m_hat / (jnp.sqrt(v_hat) + ADAM_EPS) + ADAM_WD * w_ref[...])

    blk = pl.BlockSpec((tr, c), lambda i: (i, 0))
    sh = jax.ShapeDtypeStruct((r, c), F32)
    return pl.pallas_call(
        body, name=name, grid=(r // tr,), in_specs=[blk] * 4, out_specs=[blk] * 3, out_shape=[sh] * 3,
        compiler_params=_cparams(("parallel",)),
    )(w, g, m, v)


WEIGHTS = ['norm_mix_g', 'w_in', 'ln_a_g', 'ln_a_b', 'w_s', 'b_s', 'norm_a_g', 'conv_ssm_w', 'conv_ssm_b',
           'dt_bias', 'a_log', 'd_skip', 'ssm_norm_g', 'w_out', 'norm_ffn_g', 'w_up', 'conv_ffn_w',
           'conv_ffn_b', 'w_down', 'norm_ple_g', 'w_ple_gate', 'w_ple', 'norm_final_g']
BIG = ['w_in', 'w_out', 'w_up', 'w_down', 'w_ple_gate', 'w_ple']
SMALL = [n for n in WEIGHTS if n not in BIG]
PACK_ALIGN = 2048


def _pack(arrs):
    parts = []
    for a in arrs:
        f = a.reshape(-1).astype(F32)
        parts.append(jnp.pad(f, (0, (-f.shape[0]) % PACK_ALIGN)))
    return jnp.concatenate(parts).reshape(-1, 128)


def _unpack(buf, shapes):
    flat = buf.reshape(-1)
    out, off = [], 0
    for s in shapes:
        n = math.prod(s)
        out.append(flat[off:off + n].reshape(s))
        off += n + (-n) % PACK_ALIGN
    return out


def _pad_heads(v):
    return jnp.pad(v, ((0, 0), (0, HPAD - v.shape[1])))


def _col_sharded(full):
    r, c4 = full.shape
    return jnp.transpose(full.reshape(r, N_CHIPS, c4 // N_CHIPS), (1, 0, 2))


def _from_col_sharded(g):
    s, r, c = g.shape
    return jnp.transpose(g, (1, 0, 2)).reshape(r, s * c)


def kernel(x, p, norm_mix_g, w_in, ln_a_g, ln_a_b, w_s, b_s, norm_a_g, conv_ssm_w, conv_ssm_b, dt_bias, a_log, d_skip, ssm_norm_g, w_out, norm_ffn_g, w_up, conv_ffn_w, conv_ffn_b, w_down, norm_ple_g, w_ple_gate, w_ple, norm_final_g, loss_target, m_norm_mix_g, m_w_in, m_ln_a_g, m_ln_a_b, m_w_s, m_b_s, m_norm_a_g, m_conv_ssm_w, m_conv_ssm_b, m_dt_bias, m_a_log, m_d_skip, m_ssm_norm_g, m_w_out, m_norm_ffn_g, m_w_up, m_conv_ffn_w, m_conv_ffn_b, m_w_down, m_norm_ple_g, m_w_ple_gate, m_w_ple, m_norm_final_g, v_norm_mix_g, v_w_in, v_ln_a_g, v_ln_a_b, v_w_s, v_b_s, v_norm_a_g, v_conv_ssm_w, v_conv_ssm_b, v_dt_bias, v_a_log, v_d_skip, v_ssm_norm_g, v_w_out, v_norm_ffn_g, v_w_up, v_conv_ffn_w, v_conv_ffn_b, v_w_down, v_norm_ple_g, v_w_ple_gate, v_w_ple, v_norm_final_g):
    given = dict(locals())
    wts = {n: given[n] for n in WEIGHTS}
    mom = {n: given['m_' + n] for n in WEIGHTS}
    var = {n: given['v_' + n] for n in WEIGHTS}
    d = D_MODEL
    xt, pt, tgt = x[0], p[0, 0], loss_target[0]
    chip = 2 * lax.axis_index("x") + lax.axis_index("y")

    slots = {n: _cast_into_slot(wts[n][0], "cast_" + n) for n in BIG}
    g_in, g_cs, g_cf = _gather_weights([slots['w_in']], [conv_ssm_w[0], conv_ffn_w[0]])
    later = [n for n in BIG if n != 'w_in']
    w_in_full = _from_col_sharded(g_in)
    w_main = w_in_full[:, :D_MAIN]
    w_dt = _pad_heads(w_in_full[:, D_MAIN:])
    cs_w = _from_col_sharded(g_cs)
    cf_w = _from_col_sharded(g_cf)
    consts = _ssd_consts()
    dtb, alog = _pad_heads(dt_bias), _pad_heads(a_log)
    de = jnp.repeat(d_skip[0], HEAD_DIM)[None, :]
    b_exp = jnp.broadcast_to(b_s[0][:, :, None], (N_GROUPS_A, CHUNK, CHUNK))

    a1 = _rms_fwd(xt, norm_mix_g, "rms_mix")
    proj = _matmul(a1, w_main, mode='nn', name="mm_proj", tm=1024, tn=1024)
    dt_raw = _matmul(a1, w_dt, mode='nn', name="mm_dt", tm=1024, tn=128)
    yab = _gmlp_fwd(proj, ln_a_g, ln_a_b, w_s[0], b_exp, norm_a_g)
    yab, ysave, hs, *gathered = _ssd_fwd(proj, dt_raw, yab, cs_w, conv_ssm_b, dtb, alog, de, ssm_norm_g, consts,
                                         _hosted_gather([slots[n] for n in later]))
    g_out, g_up, g_down, g_pg, g_ple = gathered
    w_out_f = g_out.reshape(D_MIX, d)
    w_down_f = g_down.reshape(D_FF, d)
    w_pg_f = g_pg.reshape(d, d)
    h1 = _matmul(yab, w_out_f, mode='nn', name="mm_out", res=xt, tm=512, tn=1024, tk=4096)
    f = _rms_fwd(h1, norm_ffn_g, "rms_ffn")
    hid = _matmul(f, g_up, mode='nn', name="mm_up", b_sharded=True, tm=1024, tn=1408)
    act = _ffn_act_fwd(hid, cf_w, conv_ffn_b)
    h2 = _matmul(act, w_down_f, mode='nn', name="mm_down", res=h1, tm=1024, tn=1024, tk=2816)
    n3 = _rms_fwd(h2, norm_ple_g, "rms_ple")
    gl = _matmul(n3, w_pg_f, mode='nn', name="mm_pg", tm=512, tn=1024)
    pe = _matmul(pt, g_ple, mode='nn', name="mm_ple", b_sharded=True, tm=1024, tn=512)
    dh3, dgl, dpe, lossv, dgf = _tail(h2, gl, pe, tgt, norm_final_g[None, :])

    gs_ple = _matmul(pt, dpe, mode='tn', name="mm_dw_ple", out_dtype=BF16, out_shards=N_CHIPS,
                     tm=256, tn=512, tk=2048)
    gs_pg = _matmul(n3, dgl, mode='tn', name="mm_dw_pg", out_dtype=BF16, tm=1024, tn=1024, tk=2048)
    dn3 = _matmul(dgl, w_pg_f, mode='nt', name="mm_dn3", tm=512, tn=1024)
    dh2, dg_ple, dh2_b = _rms_bwd(h2, norm_ple_g, dn3, dh3, "rms_ple_bwd", True)
    dact = _matmul(dh2_b, w_down_f, mode='nt', name="mm_dact", out_dtype=BF16, tm=1024, tn=1408)
    gs_down = _matmul(act, dh2_b, mode='tn', name="mm_dw_down", out_dtype=BF16, tm=1408, tn=1024, tk=2048)
    dpg, dpu, wg_acc, wu_acc = _ffn_act_bwd(hid, cf_w, conv_ffn_b, dact)
    hc = N_CHIPS // 2
    gs_up = jnp.concatenate(
        [_matmul(f, dpg, mode='tn', name="mm_dw_up_g", out_dtype=BF16, out_shards=hc, tm=1024, tn=1408, tk=2048),
         _matmul(f, dpu, mode='tn', name="mm_dw_up_u", out_dtype=BF16, out_shards=hc, tm=1024, tn=1408, tk=2048)],
        axis=0)
    early = [gs_up, gs_down.reshape(N_CHIPS, D_FF // N_CHIPS, d), gs_pg.reshape(N_CHIPS, d // N_CHIPS, d), gs_ple]
    df, *sib_e = _matmul(dpg, g_up, mode='nt', name="mm_df_g", b_sharded=True, tm=1024, tn=1024, tk=1408,
                         hosted=_hosted_rs_sibling(early))
    part_e = [_add_pairs(a, b, "rs_add_e%d" % i) for i, (a, b) in enumerate(zip(early, sib_e))]
    df = _matmul(dpu, g_up, mode='nt', name="mm_df_u", b_sharded=True, b_shard_off=hc, res=df,
                 tm=1024, tn=1024, tk=1408)
    dh1, dg_ffn, dh1_b = _rms_bwd(h1, norm_ffn_g, df, dh2, "rms_ffn_bwd", True)
    dyab = _matmul(dh1_b, w_out_f, mode='nt', name="mm_dyab", tm=1024, tn=1024)
    gs_out = _matmul(yab, dh1_b, mode='tn', name="mm_dw_out", out_dtype=BF16, tm=1024, tn=1024, tk=2048)
    duv, dws, dbs, dlng, dlnb, dnag = _gmlp_bwd(proj, dyab, ln_a_g, ln_a_b, w_s[0], b_exp, norm_a_g)
    dz, dxs, dbm, dcm, ddt_raw, acc_x, acc_b, acc_c, acc_gain, acc_head, *rcv_e = _ssd_bwd(
        proj, dt_raw, dyab, ysave, hs, cs_w, conv_ssm_b, dtb, alog, de, ssm_norm_g, consts,
        _hosted_rs_chips(part_e))
    dproj = jnp.concatenate([duv, dz, dxs, dbm, dcm], axis=1)
    dw_main = _matmul(a1, dproj, mode='tn', name="mm_dw_main", out_dtype=BF16, tm=1024, tn=1024, tk=2048)
    dw_dt = _matmul(a1, ddt_raw, mode='tn', name="mm_dw_dt", out_dtype=BF16, tm=1024, tn=128, tk=2048)
    gs_in = _col_sharded(jnp.concatenate([dw_main, dw_dt[:, :N_HEADS]], axis=1))
    late = [gs_in, gs_out.reshape(N_CHIPS, D_MIX // N_CHIPS, d)]
    part_l = [_add_pairs(a, b, "rs_add_l%d" % i)
              for i, (a, b) in enumerate(zip(late, _rs_sibling(late, "rs_sibling_late")))]
    da_dt = _matmul(ddt_raw, w_dt, mode='nt', name="mm_da_dt", tm=1024, tn=1024)
    da, *rcv_l = _matmul(dproj, w_main, mode='nt', name="mm_da", res=da_dt, tm=1024, tn=1024, tk=2048,
                         hosted=_hosted_rs_chips(part_l))
    dx, dg_mix = _rms_bwd(xt, norm_mix_g, da, dh1, "rms_mix_bwd", False)

    order = ['w_up', 'w_down', 'w_ple_gate', 'w_ple', 'w_in', 'w_out']
    halves = [_sum_chips(a, b, "rs_sum_%d" % i)
              for i, (a, b) in enumerate(zip(part_e + part_l, list(rcv_e) + list(rcv_l)))]
    g_big = dict(zip(order, _share_sibling(halves)))

    def conv_rows(acc, k):
        return acc[:, k, :].reshape(1, -1)

    dcw = jnp.concatenate([jnp.concatenate([conv_rows(acc_x, k), conv_rows(acc_b, k), conv_rows(acc_c, k)], axis=1)
                           for k in range(SSM_CONV)], axis=0)
    dcb = jnp.concatenate([conv_rows(acc_x, SSM_CONV), conv_rows(acc_b, SSM_CONV), conv_rows(acc_c, SSM_CONV)], axis=1)
    part = {
        'norm_mix_g': dg_mix, 'ln_a_g': dlng, 'ln_a_b': dlnb, 'w_s': dws, 'b_s': dbs, 'norm_a_g': dnag,
        'conv_ssm_w': dcw, 'conv_ssm_b': dcb,
        'dt_bias': acc_head[0:1, :N_HEADS], 'a_log': acc_head[1:2, :N_HEADS], 'd_skip': acc_head[2:3, :N_HEADS],
        'ssm_norm_g': acc_gain[:, 0, :], 'norm_ffn_g': dg_ffn,
        'conv_ffn_w': jnp.concatenate([wg_acc[:FFN_CONV], wu_acc[:FFN_CONV]], axis=1),
        'conv_ffn_b': jnp.concatenate([wg_acc[FFN_CONV:FFN_CONV + 1], wu_acc[FFN_CONV:FFN_CONV + 1]], axis=1),
        'norm_ple_g': dg_ple, 'norm_final_g': dgf,
    }
    full_shapes = {n: wts[n].shape for n in SMALL}
    full_shapes['conv_ssm_w'] = (1, SSM_CONV, D_XBC)
    full_shapes['conv_ffn_w'] = (1, FFN_CONV, 2 * D_FF)
    packed = _pack([part[n] for n in SMALL] + [lossv[:, 0:1]])
    total = _sum_devices(_allgather_small(packed))
    pieces = _unpack(total, [full_shapes[n] for n in SMALL] + [(1,)])
    g_small = dict(zip(SMALL, pieces[:-1]))
    loss = pieces[-1][0]
    for n in ('conv_ssm_w', 'conv_ffn_w'):
        width = wts[n].shape[2]
        g_small[n] = lax.dynamic_slice_in_dim(g_small[n], chip * width, width, axis=2)

    grads, delta, new_m, new_v = {}, {}, {}, {}
    for n in BIG:
        shp = wts[n].shape
        dl, mn, vn = _adamw(wts[n][0], g_big[n], mom[n][0], var[n][0], "adamw_" + n)
        grads[n], delta[n], new_m[n], new_v[n] = (g_big[n].reshape(shp), dl.reshape(shp), mn.reshape(shp),
                                                  vn.reshape(shp))
    shapes = [wts[n].shape for n in SMALL]
    dl, mn, vn = _adamw(_pack([wts[n] for n in SMALL]), _pack([g_small[n] for n in SMALL]),
                        _pack([mom[n] for n in SMALL]), _pack([var[n] for n in SMALL]), "adamw_small")
    for n, a, b, c in zip(SMALL, _unpack(dl, shapes), _unpack(mn, shapes), _unpack(vn, shapes)):
        grads[n], delta[n], new_m[n], new_v[n] = g_small[n], a, b, c

    return (loss, dx[None], *[grads[n] for n in WEIGHTS], *[delta[n] for n in WEIGHTS],
            *[new_m[n] for n in WEIGHTS], *[new_v[n] for n in WEIGHTS])
```

```python
import functools
import math

import jax
import jax.numpy as jnp
from jax import lax
from jax.experimental import pallas as pl
from jax.experimental.pallas import tpu as pltpu

D_MODEL = 2048
SEQ = 8192
D_MIX = 2 * D_MODEL
D_A = D_MIX // 2
CHUNK = 128
N_GROUPS_A = D_A // 128
D_SSM = D_MIX - D_A
HEAD_DIM = 64
N_HEADS = D_SSM // HEAD_DIM
HEADS_PER_GROUP = 4
N_SSM_GROUPS = N_HEADS // HEADS_PER_GROUP
GW = HEADS_PER_GROUP * HEAD_DIM
D_STATE = 128
SSM_CONV = 4
D_BC = N_SSM_GROUPS * D_STATE
D_XBC = D_SSM + 2 * D_BC
D_MAIN = 2 * D_A + D_SSM + D_XBC
D_IN = D_MAIN + N_HEADS
D_FF = (D_MODEL * 11) // 4
FFN_CONV = 3
D_PLE = 256
EPS = 1e-6
HPAD = 128
N_CHIPS = 4

ADAM_LR = 0.001
ADAM_B1 = 0.9
ADAM_B2 = 0.999
ADAM_EPS = 1e-08
ADAM_WD = 0.01
ADAM_STEP = 10

F32 = jnp.float32
BF16 = jnp.bfloat16
MESH = pl.DeviceIdType.MESH
VMEM_LIMIT = 56 * 1024 * 1024


def _cparams(sem):
    return pltpu.CompilerParams(dimension_semantics=sem, vmem_limit_bytes=VMEM_LIMIT)


def _tile(n, pref, mult):
    t = min(pref, n)
    t -= t % mult
    while n % t:
        t -= mult
    return t


def _dot(a, b):
    return jnp.dot(a, b, preferred_element_type=F32)


def _dot_nt(a, b):
    return lax.dot_general(a, b, (((1,), (1,)), ((), ())), preferred_element_type=F32)


def _dot_tn(a, b):
    return lax.dot_general(a, b, (((0,), (0,)), ((), ())), preferred_element_type=F32)


def _split3(x):
    hi = x.astype(BF16)
    r = x - hi.astype(F32)
    mid = r.astype(BF16)
    lo = (r - mid.astype(F32)).astype(BF16)
    return hi, mid, lo


def _x01(x, e):
    h, m, l = _split3(x)
    return _dot(h, e) + _dot(m, e) + _dot(l, e)


def _x01_nt(x, e):
    h, m, l = _split3(x)
    return _dot_nt(h, e) + _dot_nt(m, e) + _dot_nt(l, e)


def _e01x(e, x):
    h, m, l = _split3(x)
    return _dot(e, h) + _dot(e, m) + _dot(e, l)


def _e01x_tn(e, x):
    h, m, l = _split3(x)
    return _dot_tn(e, h) + _dot_tn(e, m) + _dot_tn(e, l)


def _sigmoid(x):
    return 1.0 / (1.0 + jnp.exp(-x))


_GELU_C = math.sqrt(2.0 / math.pi)


def _gelu_and_grad(x):
    x2 = x * x
    th = jnp.tanh(_GELU_C * (x + 0.044715 * x * x2))
    y = 0.5 * x * (1.0 + th)
    dy = 0.5 * (1.0 + th) + 0.5 * x * (1.0 - th * th) * (_GELU_C * (1.0 + 3.0 * 0.044715 * x2))
    return y, dy


def _silu_and_grad(x):
    s = _sigmoid(x)
    return x * s, s * (1.0 + x * (1.0 - s))


def _softplus(x):
    u = jnp.exp(-jnp.abs(x))
    w = 1.0 + u
    l1p = jnp.where(w == 1.0, u, jnp.log(w) * (u / (w - 1.0)))
    return jnp.maximum(x, 0.0) + l1p


def _matmul(a, b, *, mode, name, out_dtype=F32, res=None, tm=512, tn=512, tk=2048,
            b_sharded=False, b_shard_off=0, out_shards=0, hosted=None):
    if mode == 'tn':
        kdim, m = a.shape
        n = b.shape[1]
    else:
        m, kdim = a.shape
        if b_sharded:
            s_b, d1, d2 = b.shape
            n = s_b * d2 if mode == 'nn' else d1
        else:
            n = b.shape[1] if mode == 'nn' else b.shape[0]
    per = None
    if b_sharded:
        per = b.shape[2]
    if out_shards:
        per = n // out_shards
    tm = _tile(m, tm, 128 if mode == 'tn' else 8)
    if mode == 'nt' and b_sharded:
        tn = _tile(n, tn, 128)
        tk = _tile(per, tk, 128)
    elif per is not None:
        tn = _tile(per, tn, 128)
        tk = _tile(kdim, tk, 128 if mode != 'tn' else 8)
    else:
        tn = _tile(n, tn, 128)
        tk = _tile(kdim, tk, 128 if mode != 'tn' else 8)
    nm, nn_, nk = m // tm, n // tn, kdim // tk
    has_res = res is not None
    n_in = 2 + has_res
    nh_in = len(hosted['arrays']) if hosted else 0
    nh_out = len(hosted['out_shape']) if hosted else 0

    def body(*refs):
        a_ref, b_ref = refs[0], refs[1]
        res_ref = refs[2] if has_res else None
        o_ref = refs[n_in + nh_in]
        if hosted:
            hins = refs[n_in:n_in + nh_in]
            houts = refs[n_in + nh_in + 1:n_in + nh_in + 1 + nh_out]
            sems = refs[-2:]
            ids = [pl.program_id(d) for d in range(3)]
            at_first = jnp.logical_and(jnp.logical_and(ids[0] == 0, ids[1] == 0), ids[2] == 0)
            at_last = jnp.logical_and(jnp.logical_and(ids[0] == nn_ - 1, ids[1] == nm - 1), ids[2] == nk - 1)
            _host_phase(hosted, 'start', at_first, hins, houts, sems)
        av = a_ref[...].astype(BF16)
        bv = b_ref[...].astype(BF16)
        if mode == 'nn':
            p = _dot(av, bv)
        elif mode == 'nt':
            p = _dot_nt(av, bv)
        else:
            p = _dot_tn(av, bv)

        def fin(v):
            if has_res:
                v = v + res_ref[...]
            o_ref[...] = v.astype(o_ref.dtype)

        if nk == 1:
            fin(p)
        else:
            acc_ref = refs[n_in + nh_in + 1 + nh_out]
            k = pl.program_id(2)

            @pl.when(k == 0)
            def _():
                acc_ref[...] = p

            @pl.when(k > 0)
            def _():
                acc_ref[...] += p

            @pl.when(k == nk - 1)
            def _():
                fin(acc_ref[...])
        if hosted:
            _host_phase(hosted, 'mid', at_last, hins, houts, sems)
            _host_phase(hosted, 'finish', at_last, hins, houts, sems)

    if mode == 'nn':
        a_spec = pl.BlockSpec((tm, tk), lambda j, i, k: (i, k))
        if b_sharded:
            nps = per // tn
            b_spec = pl.BlockSpec((None, tk, tn), lambda j, i, k: (j // nps, k, j % nps))
        else:
            b_spec = pl.BlockSpec((tk, tn), lambda j, i, k: (k, j))
    elif mode == 'nt':
        a_spec = pl.BlockSpec((tm, tk), lambda j, i, k: (i, k))
        if b_sharded:
            kps = per // tk
            b_spec = pl.BlockSpec((None, tn, tk), lambda j, i, k: (k // kps + b_shard_off, j, k % kps))
        else:
            b_spec = pl.BlockSpec((tn, tk), lambda j, i, k: (j, k))
    else:
        a_spec = pl.BlockSpec((tk, tm), lambda j, i, k: (k, i))
        b_spec = pl.BlockSpec((tk, tn), lambda j, i, k: (k, j))
    in_specs = [a_spec, b_spec]
    args = [a, b]
    if has_res:
        in_specs.append(pl.BlockSpec((tm, tn), lambda j, i, k: (i, j)))
        args.append(res)
    if out_shards:
        nps_o = per // tn
        out_shape = jax.ShapeDtypeStruct((out_shards, m, per), out_dtype)
        out_spec = pl.BlockSpec((None, tm, tn), lambda j, i, k: (j // nps_o, i, j % nps_o))
    else:
        out_shape = jax.ShapeDtypeStruct((m, n), out_dtype)
        out_spec = pl.BlockSpec((tm, tn), lambda j, i, k: (i, j))
    scratch = [pltpu.VMEM((tm, tn), F32)] if nk > 1 else []
    if not hosted:
        return pl.pallas_call(
            body, name=name, grid=(nn_, nm, nk), in_specs=in_specs, out_specs=out_spec,
            out_shape=out_shape, scratch_shapes=scratch,
            compiler_params=_cparams(("parallel", "parallel", "arbitrary")),
        )(*args)
    h_in, h_out, h_shape, h_scratch, h_alias = _host_plumbing(hosted, n_in, 1)
    return pl.pallas_call(
        body, name=name, grid=(nn_, nm, nk), in_specs=in_specs + h_in, out_specs=[out_spec] + h_out,
        out_shape=[out_shape] + h_shape, scratch_shapes=scratch + h_scratch, input_output_aliases=h_alias,
        compiler_params=_cparams(("arbitrary", "arbitrary", "arbitrary")),
    )(*args, *hosted['arrays'])


def _rms_fwd(x, g, name):
    t, d = x.shape
    tt = _tile(t, 512, 8)

    def body(x_ref, g_ref, o_ref):
        xv = x_ref[...]
        r = lax.rsqrt(jnp.mean(xv * xv, axis=-1, keepdims=True) + EPS)
        o_ref[...] = (xv * r * g_ref[...]).astype(o_ref.dtype)

    return pl.pallas_call(
        body, name=name, grid=(t // tt,),
        in_specs=[pl.BlockSpec((tt, d), lambda i: (i, 0)), pl.BlockSpec((1, d), lambda i: (0, 0))],
        out_specs=pl.BlockSpec((tt, d), lambda i: (i, 0)),
        out_shape=jax.ShapeDtypeStruct((t, d), BF16),
        compiler_params=_cparams(("parallel",)),
    )(x, g)


def _rms_bwd(x, g, dy, dres, name, also_bf16):
    t, d = x.shape
    tt = _tile(t, 256, 16)

    def body(x_ref, g_ref, dy_ref, dres_ref, dx_ref, dg_ref, *dxb):
        i = pl.program_id(0)
        xv = x_ref[...]
        r = lax.rsqrt(jnp.mean(xv * xv, axis=-1, keepdims=True) + EPS)
        xh = xv * r
        dyv = dy_ref[...].astype(F32)
        dxh = dyv * g_ref[...]
        c = jnp.mean(dxh * xh, axis=-1, keepdims=True)
        dx = dres_ref[...] + r * (dxh - xh * c)
        dx_ref[...] = dx
        for dxb_ref in dxb:
            dxb_ref[...] = dx.astype(BF16)
        part = jnp.sum(dyv * xh, axis=0, keepdims=True)

        @pl.when(i == 0)
        def _():
            dg_ref[...] = part

        @pl.when(i > 0)
        def _():
            dg_ref[...] += part

    row = pl.BlockSpec((tt, d), lambda i: (i, 0))
    vec = pl.BlockSpec((1, d), lambda i: (0, 0))
    return pl.pallas_call(
        body, name=name, grid=(t // tt,),
        in_specs=[row, vec, row, row], out_specs=[row, vec] + [row] * also_bf16,
        out_shape=[jax.ShapeDtypeStruct((t, d), F32), jax.ShapeDtypeStruct((1, d), F32)]
        + [jax.ShapeDtypeStruct((t, d), BF16)] * also_bf16,
        compiler_params=_cparams(("arbitrary",)),
    )(x, g, dy, dres)


def _tail(h2, gl, pe, target, gfin):
    t, d = h2.shape
    tt = _tile(t, 256, 8)

    def body(h2_ref, gl_ref, pe_ref, tg_ref, gf_ref, dh3_ref, dgl_ref, dpe_ref, loss_ref, dgf_ref):
        i = pl.program_id(0)
        sig = _sigmoid(gl_ref[...])
        pev = pe_ref[...]
        h3 = h2_ref[...] + sig * pev
        r = lax.rsqrt(jnp.mean(h3 * h3, axis=-1, keepdims=True) + EPS)
        xh = h3 * r
        gf = gf_ref[...]
        e = xh * gf - tg_ref[...]
        dy = e * (1.0 / d)
        dxh = dy * gf
        c = jnp.mean(dxh * xh, axis=-1, keepdims=True)
        dh3 = r * (dxh - xh * c)
        dh3_ref[...] = dh3
        dgl_ref[...] = (dh3 * pev * sig * (1.0 - sig)).astype(BF16)
        dpe_ref[...] = (dh3 * sig).astype(BF16)
        lpart = jnp.sum(e * e, axis=0, keepdims=True) * (0.5 / d)
        gpart = jnp.sum(dy * xh, axis=0, keepdims=True)

        @pl.when(i == 0)
        def _():
            loss_ref[...] = lpart
            dgf_ref[...] = gpart

        @pl.when(i > 0)
        def _():
            loss_ref[...] += lpart
            dgf_ref[...] += gpart

        @pl.when(i == t // tt - 1)
        def _():
            loss_ref[...] = jnp.broadcast_to(jnp.sum(loss_ref[...], axis=-1, keepdims=True), (1, d))

    row = pl.BlockSpec((tt, d), lambda i: (i, 0))
    vec = pl.BlockSpec((1, d), lambda i: (0, 0))
    return pl.pallas_call(
        body, name="tail", grid=(t // tt,),
        in_specs=[row, row, row, row, vec], out_specs=[row, row, row, vec, vec],
        out_shape=[jax.ShapeDtypeStruct((t, d), F32), jax.ShapeDtypeStruct((t, d), BF16),
                   jax.ShapeDtypeStruct((t, d), BF16), jax.ShapeDtypeStruct((1, d), F32),
                   jax.ShapeDtypeStruct((1, d), F32)],
        compiler_params=_cparams(("arbitrary",)),
    )(h2, gl, pe, target, gfin)


def _conv(cur_ref, prev_ref, w_ref, b_ref, ext_ref, first, width):
    rows = cur_ref.shape[0]
    ext_ref[0:8, :] = jnp.where(first, 0.0, prev_ref[...])
    ext_ref[8:8 + rows, :] = cur_ref[...]
    acc = b_ref[...]
    for k in range(width):
        acc = acc + w_ref[k:k + 1, :] * ext_ref[pl.ds(9 - width + k, rows), :]
    return acc


def _ffn_specs(t, tt, tc, nf):
    hb = tt // 8
    cur_g = pl.BlockSpec((tt, tc), lambda j, i: (i, j))
    cur_u = pl.BlockSpec((tt, tc), lambda j, i: (i, j + nf))
    prev_g = pl.BlockSpec((8, tc), lambda j, i: (jnp.maximum(i * hb - 1, 0), j))
    prev_u = pl.BlockSpec((8, tc), lambda j, i: (jnp.maximum(i * hb - 1, 0), j + nf))
    w_g = pl.BlockSpec((FFN_CONV, tc), lambda j, i: (0, j))
    w_u = pl.BlockSpec((FFN_CONV, tc), lambda j, i: (0, j + nf))
    b_g = pl.BlockSpec((1, tc), lambda j, i: (0, j))
    b_u = pl.BlockSpec((1, tc), lambda j, i: (0, j + nf))
    return [cur_g, prev_g, cur_u, prev_u, w_g, w_u, b_g, b_u]


FFN_TC = 512


def _shift_down(prev, cur, n, rid):
    return jnp.where(rid < n, pltpu.roll(prev, n, 0), pltpu.roll(cur, n, 0))


def _shift_up(cur, nxt, n, rid):
    return jnp.where(rid < 8 - n, pltpu.roll(cur, 8 - n, 0), pltpu.roll(nxt, 8 - n, 0))


def _conv3_group(prev, cur, w_ref, b_ref, rid):
    x1 = _shift_down(prev, cur, 1, rid)
    x2 = _shift_down(prev, cur, 2, rid)
    return b_ref[...] + w_ref[2:3, :] * cur + w_ref[1:2, :] * x1 + w_ref[0:1, :] * x2


def _ffn_act_fwd(hid, cw, cb):
    t = hid.shape[0]
    tt = _tile(t, 512, 16)
    tc = _tile(D_FF, FFN_TC, 128)
    nf = D_FF // tc

    def body(g_ref, gp_ref, u_ref, up_ref, wg_ref, wu_ref, bg_ref, bu_ref, o_ref):
        first = pl.program_id(1) == 0
        rid = lax.broadcasted_iota(jnp.int32, (8, tc), 0)

        def act(pg, cg, pu, cu):
            gate = _conv3_group(pg, cg, wg_ref, bg_ref, rid)
            up = _conv3_group(pu, cu, wu_ref, bu_ref, rid)
            return gate * _sigmoid(gate) * up

        def step(s, carry):
            pg, pu = carry
            r0 = pl.multiple_of(s * 16, 16)
            g0, g1 = g_ref[pl.ds(r0, 8), :], g_ref[pl.ds(r0 + 8, 8), :]
            u0, u1 = u_ref[pl.ds(r0, 8), :], u_ref[pl.ds(r0 + 8, 8), :]
            out = jnp.concatenate([act(pg, g0, pu, u0), act(g0, g1, u0, u1)], axis=0)
            o_ref[pl.ds(r0, 16), :] = out.astype(BF16)
            return g1, u1

        init = (jnp.where(first, 0.0, gp_ref[...]), jnp.where(first, 0.0, up_ref[...]))
        lax.fori_loop(0, tt // 16, step, init)

    return pl.pallas_call(
        body, name="ffn_act_fwd", grid=(nf, t // tt), in_specs=_ffn_specs(t, tt, tc, nf),
        out_specs=pl.BlockSpec((tt, tc), lambda j, i: (i, j)),
        out_shape=jax.ShapeDtypeStruct((t, D_FF), BF16),
        compiler_params=_cparams(("parallel", "arbitrary")),
    )(hid, hid, hid, hid, cw, cw, cb, cb)


def _ffn_act_bwd(hid, cw, cb, dact):
    t = hid.shape[0]
    tt = _tile(t, 512, 16)
    tc = _tile(D_FF, FFN_TC, 128)
    nf = D_FF // tc
    nt = t // tt
    n16 = tt // 16
    hb = tt // 8

    def body(g_ref, gp_ref, u_ref, up_ref, wg_ref, wu_ref, bg_ref, bu_ref, gn_ref, un_ref, da_ref, dan_ref,
             og_ref, ou_ref, ag_ref, au_ref, accs):
        i = pl.program_id(1)
        first, last = i == 0, i == nt - 1
        rid = lax.broadcasted_iota(jnp.int32, (8, tc), 0)
        accs[...] = jnp.zeros_like(accs)

        def dgroup(pg, cg, pu, cu, da):
            gate = _conv3_group(pg, cg, wg_ref, bg_ref, rid)
            up = _conv3_group(pu, cu, wu_ref, bu_ref, rid)
            sv, sgr = _silu_and_grad(gate)
            return da * up * sgr, da * sv

        def finish(x, d0, d1, w_ref):
            s1 = _shift_up(d0, d1, 1, rid)
            s2 = _shift_up(d0, d1, 2, rid)
            dpre = w_ref[2:3, :] * d0 + w_ref[1:2, :] * s1 + w_ref[0:1, :] * s2
            return dpre, (x * s2, x * s1, x * d0, d0)

        def two_groups(it, carry, x2g, x2u, da_blk, zero_ahead):
            x0g, x0u, d0g, d0u, da1 = carry
            r0 = it * 16 if isinstance(it, int) else pl.multiple_of(it * 16, 16)
            x1g, x1u = g_ref[pl.ds(r0 + 8, 8), :], u_ref[pl.ds(r0 + 8, 8), :]
            d1g, d1u = dgroup(x0g, x1g, x0u, x1u, da1)
            d2g, d2u = dgroup(x1g, x2g, x1u, x2u, da_blk[0:8])
            d2g = jnp.where(zero_ahead, 0.0, d2g)
            d2u = jnp.where(zero_ahead, 0.0, d2u)
            outs = []
            for half, (xa, xb, da_, db_, dc_, w_ref, o_ref) in enumerate((
                    (x0g, x1g, d0g, d1g, d2g, wg_ref, og_ref), (x0u, x1u, d0u, d1u, d2u, wu_ref, ou_ref))):
                pa, prods_a = finish(xa, da_, db_, w_ref)
                pb, prods_b = finish(xb, db_, dc_, w_ref)
                o_ref[pl.ds(r0, 16), :] = jnp.concatenate([pa, pb], axis=0).astype(BF16)
                for k in range(4):
                    accs[4 * half + k] += prods_a[k] + prods_b[k]
            return x2g, x2u, d2g, d2u, da_blk[8:16]

        def step(it, carry):
            r1 = pl.multiple_of(it * 16 + 16, 16)
            return two_groups(it, carry, g_ref[pl.ds(r1, 8), :], u_ref[pl.ds(r1, 8), :],
                              da_ref[pl.ds(r1, 16), :].astype(F32), False)

        da0 = da_ref[pl.ds(0, 16), :].astype(F32)
        x0g, x0u = g_ref[pl.ds(0, 8), :], u_ref[pl.ds(0, 8), :]
        d0g, d0u = dgroup(jnp.where(first, 0.0, gp_ref[...]), x0g, jnp.where(first, 0.0, up_ref[...]), x0u, da0[0:8])
        carry = lax.fori_loop(0, n16 - 1, step, (x0g, x0u, d0g, d0u, da0[8:16]))
        two_groups(n16 - 1, carry, gn_ref[...], un_ref[...], dan_ref[...].astype(F32), last)

        @pl.when(first)
        def _():
            ag_ref[...] = jnp.zeros_like(ag_ref)
            au_ref[...] = jnp.zeros_like(au_ref)

        for half, a_ref in enumerate((ag_ref, au_ref)):
            for k in range(4):
                a_ref[k:k + 1, :] += jnp.sum(accs[4 * half + k], axis=0, keepdims=True)

    def nxt8(i):
        return jnp.minimum((i + 1) * hb, t // 8 - 1)

    def nxt16(i):
        return jnp.minimum((i + 1) * n16, t // 16 - 1)

    out_blk = pl.BlockSpec((tt, tc), lambda j, i: (i, j))
    acc_spec = pl.BlockSpec((8, tc), lambda j, i: (0, j))
    in_specs = _ffn_specs(t, tt, tc, nf) + [
        pl.BlockSpec((8, tc), lambda j, i: (nxt8(i), j)), pl.BlockSpec((8, tc), lambda j, i: (nxt8(i), j + nf)),
        out_blk, pl.BlockSpec((16, tc), lambda j, i: (nxt16(i), j))]
    return pl.pallas_call(
        body, name="ffn_act_bwd", grid=(nf, nt), in_specs=in_specs,
        out_specs=[out_blk, out_blk, acc_spec, acc_spec],
        out_shape=[jax.ShapeDtypeStruct((t, D_FF), BF16), jax.ShapeDtypeStruct((t, D_FF), BF16),
                   jax.ShapeDtypeStruct((8, D_FF), F32), jax.ShapeDtypeStruct((8, D_FF), F32)],
        scratch_shapes=[pltpu.VMEM((8, 8, tc), F32)],
        compiler_params=_cparams(("parallel", "arbitrary")),
    )(hid, hid, hid, hid, cw, cw, cb, cb, hid, hid, dact, dact)


def _tri_mask():
    r = lax.broadcasted_iota(jnp.int32, (CHUNK, CHUNK), 0)
    c = lax.broadcasted_iota(jnp.int32, (CHUNK, CHUNK), 1)
    return r >= c


def _gmlp_group_fwd(uv_ref, lng_ref, lnb_ref, ws_ref, bexp_ref, tri, g, want_grad):
    lo, hi = g * 128, (g + 1) * 128
    u_pre = uv_ref[:, lo:hi]
    v_pre = uv_ref[:, D_A + lo:D_A + hi]
    u, du = _gelu_and_grad(u_pre)
    v, dv = _gelu_and_grad(v_pre)
    mu = jnp.mean(v, axis=-1, keepdims=True)
    dc = v - mu
    rs = lax.rsqrt(jnp.mean(dc * dc, axis=-1, keepdims=True) + EPS)
    xh = dc * rs
    vn = (xh * lng_ref[:, lo:hi] + lnb_ref[:, lo:hi]).astype(BF16)
    w = jnp.where(tri, ws_ref[g], 0.0).astype(BF16)
    sg = _dot(w, vn) + bexp_ref[g]
    if want_grad:
        return u, du, dv, rs, xh, vn, w, sg
    return u * sg


def _gmlp_fwd(proj, ln_g, ln_b, w_s, b_exp, na_g):
    t = proj.shape[0]
    ng = N_GROUPS_A

    def body(uv_ref, lng_ref, lnb_ref, ws_ref, bexp_ref, nag_ref, o_ref):
        tri = _tri_mask()
        ys = [_gmlp_group_fwd(uv_ref, lng_ref, lnb_ref, ws_ref, bexp_ref, tri, g, False) for g in range(ng)]
        ssq = ys[0] * 0.0
        for y in ys:
            ssq = ssq + y * y
        r = lax.rsqrt(jnp.sum(ssq, axis=-1, keepdims=True) * (1.0 / D_A) + EPS)
        for g, y in enumerate(ys):
            o_ref[:, g * 128:(g + 1) * 128] = (y * r * nag_ref[:, g * 128:(g + 1) * 128]).astype(BF16)

    vec = pl.BlockSpec((1, D_A), lambda i: (0, 0))
    cube = pl.BlockSpec((ng, CHUNK, CHUNK), lambda i: (0, 0, 0))
    return pl.pallas_call(
        body, name="gmlp_fwd", grid=(t // CHUNK,),
        in_specs=[pl.BlockSpec((CHUNK, 2 * D_A), lambda i: (i, 0)), vec, vec, cube, cube, vec],
        out_specs=pl.BlockSpec((CHUNK, D_A), lambda i: (i, 0)),
        out_shape=jax.ShapeDtypeStruct((t, D_MIX), BF16),
        compiler_params=_cparams(("parallel",)),
    )(proj, ln_g, ln_b, w_s, b_exp, na_g)


def _gmlp_bwd(proj, dyab, ln_g, ln_b, w_s, b_exp, na_g):
    t = proj.shape[0]
    ng = N_GROUPS_A
    nsteps = t // CHUNK

    def body(uv_ref, dy_ref, lng_ref, lnb_ref, ws_ref, bexp_ref, nag_ref,
             duv_ref, dws_ref, dbs_ref, dlng_ref, dlnb_ref, dnag_ref, dbacc):
        i = pl.program_id(0)
        tri = _tri_mask()

        @pl.when(i == 0)
        def _():
            dws_ref[...] = jnp.zeros_like(dws_ref)
            dbacc[...] = jnp.zeros_like(dbacc)
            dlng_ref[...] = jnp.zeros_like(dlng_ref)
            dlnb_ref[...] = jnp.zeros_like(dlnb_ref)
            dnag_ref[...] = jnp.zeros_like(dnag_ref)

        st = [_gmlp_group_fwd(uv_ref, lng_ref, lnb_ref, ws_ref, bexp_ref, tri, g, True) for g in range(ng)]
        ssq = st[0][0] * 0.0
        for s in st:
            y = s[0] * s[7]
            ssq = ssq + y * y
        r = lax.rsqrt(jnp.sum(ssq, axis=-1, keepdims=True) * (1.0 / D_A) + EPS)
        csum = st[0][0] * 0.0
        for g, s in enumerate(st):
            sl = slice(g * 128, (g + 1) * 128)
            xhy = s[0] * s[7] * r
            dya = dy_ref[:, sl]
            dnag_ref[:, sl] += jnp.sum(dya * xhy, axis=0, keepdims=True)
            csum = csum + dya * nag_ref[:, sl] * xhy
        c1 = jnp.sum(csum, axis=-1, keepdims=True) * (1.0 / D_A)
        for g, s in enumerate(st):
            u, du, dv, rs, xh, vn, w, sg = s
            sl = slice(g * 128, (g + 1) * 128)
            dy = r * (dy_ref[:, sl] * nag_ref[:, sl] - u * sg * r * c1)
            dsg = dy * u
            dsg_b = dsg.astype(BF16)
            dws_ref[g] += _dot_nt(dsg_b, vn)
            dbacc[g] += dsg
            dvn = _dot_tn(w, dsg_b)
            dlnb_ref[:, sl] += jnp.sum(dvn, axis=0, keepdims=True)
            dlng_ref[:, sl] += jnp.sum(dvn * xh, axis=0, keepdims=True)
            dxh = dvn * lng_ref[:, sl]
            dvv = rs * (dxh - jnp.mean(dxh, axis=-1, keepdims=True)
                        - xh * jnp.mean(dxh * xh, axis=-1, keepdims=True))
            duv_ref[:, sl] = (dy * sg * du).astype(BF16)
            duv_ref[:, D_A + g * 128:D_A + (g + 1) * 128] = (dvv * dv).astype(BF16)

        @pl.when(i == nsteps - 1)
        def _():
            for g in range(ng):
                dws_ref[g] = jnp.where(tri, dws_ref[g], 0.0)
                dbs_ref[g] = jnp.sum(dbacc[g], axis=-1, keepdims=True)

    vec = pl.BlockSpec((1, D_A), lambda i: (0, 0))
    cube = pl.BlockSpec((ng, CHUNK, CHUNK), lambda i: (0, 0, 0))
    return pl.pallas_call(
        body, name="gmlp_bwd", grid=(nsteps,),
        in_specs=[pl.BlockSpec((CHUNK, 2 * D_A), lambda i: (i, 0)),
                  pl.BlockSpec((CHUNK, D_A), lambda i: (i, 0)), vec, vec, cube, cube, vec],
        out_specs=[pl.BlockSpec((CHUNK, 2 * D_A), lambda i: (i, 0)), cube,
                   pl.BlockSpec((ng, CHUNK, 1), lambda i: (0, 0, 0)), vec, vec, vec],
        out_shape=[jax.ShapeDtypeStruct((t, 2 * D_A), BF16), jax.ShapeDtypeStruct((ng, CHUNK, CHUNK), F32),
                   jax.ShapeDtypeStruct((ng, CHUNK, 1), F32), jax.ShapeDtypeStruct((1, D_A), F32),
                   jax.ShapeDtypeStruct((1, D_A), F32), jax.ShapeDtypeStruct((1, D_A), F32)],
        scratch_shapes=[pltpu.VMEM((ng, CHUNK, CHUNK), F32)],
        compiler_params=_cparams(("arbitrary",)),
    )(proj, dyab, ln_g, ln_b, w_s, b_exp, na_g)


OFF_Z = 2 * D_A
OFF_XS = OFF_Z + D_SSM
OFF_B = OFF_XS + D_SSM
OFF_C = OFF_B + D_BC


def _ssd_consts():
    tri = jnp.tril(jnp.ones((CHUNK, CHUNK), F32)).astype(BF16)
    h = jnp.arange(HPAD)[None, :, None]
    g = jnp.arange(N_SSM_GROUPS)[:, None, None]
    j1 = jnp.arange(GW)[None, None, :]
    eh = (h == g * HEADS_PER_GROUP + j1 // HEAD_DIM).astype(BF16)
    j2 = jnp.arange(HEADS_PER_GROUP * 128)[None, None, :]
    e128 = (h == g * HEADS_PER_GROUP + j2 // 128).astype(BF16)
    return tri, eh, e128


def _ssd_in_specs(cmap):
    def rows(i):
        return cmap(i)

    def prev8(i):
        return jnp.maximum(cmap(i) * (CHUNK // 8) - 1, 0)

    def whole(*shape):
        return pl.BlockSpec(shape, lambda i: (0,) * len(shape))

    bcw = 2 * D_BC
    specs = [
        pl.BlockSpec((CHUNK, D_SSM), lambda i: (rows(i), OFF_Z // D_SSM)),
        pl.BlockSpec((CHUNK, D_SSM), lambda i: (rows(i), OFF_XS // D_SSM)),
        pl.BlockSpec((8, D_SSM), lambda i: (prev8(i), OFF_XS // D_SSM)),
        pl.BlockSpec((CHUNK, bcw), lambda i: (rows(i), OFF_B // bcw)),
        pl.BlockSpec((8, bcw), lambda i: (prev8(i), OFF_B // bcw)),
        pl.BlockSpec((CHUNK, HPAD), lambda i: (rows(i), 0)),
        whole(SSM_CONV, D_XBC), whole(1, D_XBC),
        whole(1, HPAD), whole(1, HPAD),
        whole(1, D_SSM), whole(1, D_SSM),
        whole(CHUNK, CHUNK),
        whole(N_SSM_GROUPS, HPAD, GW), whole(N_SSM_GROUPS, HPAD, HEADS_PER_GROUP * 128),
    ]
    return specs


N_SSD_IN = 15


def _lanes(ref, start, width):
    return ref.at[:, pl.ds(pl.multiple_of(start, 128), width)]


def _ssd_group_refs(ins, g):
    (z_ref, xs_ref, xsp_ref, bc_ref, bcp_ref, dt_ref, cw_ref, cb_ref, dtb_ref, alog_ref, de_ref, gain_ref,
     tri_ref, eh_ref, e128_ref) = ins
    ox, ob, oc = g * GW, g * D_STATE, D_BC + g * D_STATE
    return (_lanes(z_ref, ox, GW), _lanes(xs_ref, ox, GW), _lanes(xsp_ref, ox, GW),
            _lanes(bc_ref, ob, D_STATE), _lanes(bcp_ref, ob, D_STATE),
            _lanes(bc_ref, oc, D_STATE), _lanes(bcp_ref, oc, D_STATE), dt_ref,
            _lanes(cw_ref, ox, GW), _lanes(cw_ref, D_SSM + ob, D_STATE), _lanes(cw_ref, D_SSM + oc, D_STATE),
            _lanes(cb_ref, ox, GW), _lanes(cb_ref, D_SSM + ob, D_STATE), _lanes(cb_ref, D_SSM + oc, D_STATE),
            dtb_ref, alog_ref, _lanes(de_ref, ox, GW), _lanes(gain_ref, ox, GW), tri_ref,
            eh_ref.at[g], e128_ref.at[g])


N_SSD_OWN = 4


def _ssd_scratch():
    return [pltpu.VMEM((CHUNK + 8, GW), F32), pltpu.VMEM((CHUNK + 8, D_STATE), F32),
            pltpu.VMEM((CHUNK + 8, D_STATE), F32), pltpu.VMEM((CHUNK, GW), F32),
            pltpu.VMEM((HPAD, CHUNK), F32), pltpu.VMEM((CHUNK, HPAD), F32), pltpu.VMEM((CHUNK, HPAD), F32)]


def _ssd_two_sets(alloc):
    a, b = alloc[:7], alloc[7:7 + N_SSD_OWN]
    return ((a[0], a[1], a[2], a[4], a[3], a[5], a[6]), (b[0], b[1], b[2], a[4], b[3], a[5], a[6]))


def _ssd_chunk(ins, scr):
    dt_ref, dtb_ref, alog_ref, tri_ref = ins[5], ins[8], ins[9], ins[12]
    acst_sc, dt_sc, acs_sc = scr[3], scr[5], scr[6]
    dt = _softplus(dt_ref[...] + dtb_ref[...])
    acs = _e01x(tri_ref[...], dt * (-jnp.exp(alog_ref[...])))
    dt_sc[...] = dt
    acs_sc[...] = acs
    acst_sc[...] = acs.T


def _ssd_pre(first, g, refs, scr):
    (z_ref, xs_ref, xsp_ref, b_ref, bp_ref, c_ref, cp_ref, dt_ref, cwx, cwb, cwc, cbx, cbb, cbc,
     dtb_ref, alog_ref, de_ref, gain_ref, tri_ref, eh_ref, e128_ref) = refs
    ext_x, ext_b, ext_c, acst_sc, acse_sc, dt_sc, acs_sc = scr
    p = {}
    px = _conv(xs_ref, xsp_ref, cwx, cbx, ext_x, first, SSM_CONV)
    pb = _conv(b_ref, bp_ref, cwb, cbb, ext_b, first, SSM_CONV)
    pc = _conv(c_ref, cp_ref, cwc, cbc, ext_c, first, SSM_CONV)
    p['xs'], p['dsx'] = _silu_and_grad(px)
    p['bm'], p['dsb'] = _silu_and_grad(pb)
    p['cm'], p['dsc'] = _silu_and_grad(pc)
    p['dt_in'] = dt_ref[...] + dtb_ref[...]
    dt = dt_sc[...]
    p['dt'] = dt
    p['a'] = -jnp.exp(alog_ref[...])
    tri_b = tri_ref[...]
    acs = acs_sc[...]
    eh = eh_ref[...]
    p['eh'] = eh
    p['dt_e'] = _x01(dt, eh)
    acs_e = _x01(acs, eh)
    acse_sc[...] = acs_e
    p['acs_e'] = acs_e
    p['acs_c'] = _x01(acs, e128_ref[...])
    p['acs_last_e'] = acse_sc[pl.ds(CHUNK - 1, 1), :]
    p['xdt'] = p['xs'] * p['dt_e']
    p['decay_e'] = jnp.exp(p['acs_last_e'] - acs_e)
    p['cm_b'] = p['cm'].astype(BF16)
    p['bm_b'] = p['bm'].astype(BF16)
    p['scores'] = _dot_nt(p['cm_b'], p['bm_b'])
    p['tri_b'] = tri_b
    return p


def _ssd_l(p, g, r, acst_sc, tri):
    col = p['acs_c'][:, r * 128:(r + 1) * 128]
    row = acst_sc[pl.ds(g * HEADS_PER_GROUP + r, 1), :]
    return jnp.exp(jnp.where(tri, col - row, -1e30))


def _ssd_fwd(proj, dt_raw, yab, cw, cb, dtb, alog, de, gain, consts, hosted):
    t = proj.shape[0]
    nc = t // CHUNK
    ng = N_SSM_GROUPS
    tri_c, eh_c, e128_c = consts
    n_in = N_SSD_IN
    nh_in, nh_out = len(hosted['arrays']), len(hosted['out_shape'])
    mid_chunk = (7 * nc) // 8

    def body(*refs):
        ins = refs[:n_in]
        hins = refs[n_in + 1:n_in + 1 + nh_in]
        o0 = n_in + 1 + nh_in
        yb_all, ys_all, hs_all = refs[o0:o0 + 3]
        houts = refs[o0 + 3:o0 + 3 + nh_out]
        h_sc = refs[o0 + 3 + nh_out]
        scr_a, scr_b = _ssd_two_sets(refs[o0 + 4 + nh_out:-2])
        sems = refs[-2:]
        c = pl.program_id(0)
        _host_phase(hosted, 'start', c == 0, hins, houts, sems)
        _host_phase(hosted, 'mid', c == mid_chunk, hins, houts, sems)

        @pl.when(c == 0)
        def _():
            h_sc[...] = jnp.zeros_like(h_sc)

        def group(g, scr):
            grefs = _ssd_group_refs(ins, g)
            z_ref, de_ref, gain_ref = grefs[0], grefs[16], grefs[17]
            yb_ref, ys_ref = _lanes(yb_all, g * GW, GW), _lanes(ys_all, g * GW, GW)
            slab = pl.ds(pl.multiple_of(g * D_STATE, D_STATE), D_STATE)
            p = _ssd_pre(c == 0, g, grefs, scr)
            tri = _tri_mask()
            h_in = h_sc[slab, :]
            hs_all[slab, :] = h_in
            yoff = _dot(p['cm_b'], h_in.astype(BF16)) * jnp.exp(p['acs_e'])
            states = _dot_tn(p['bm_b'], (p['xdt'] * p['decay_e']).astype(BF16))
            lane = lax.broadcasted_iota(jnp.int32, (CHUNK, 128), 1)
            slabs = []
            for r2 in range(HEADS_PER_GROUP // 2):
                xb = p['xdt'][:, r2 * 128:(r2 + 1) * 128].astype(BF16)
                ya = _dot((p['scores'] * _ssd_l(p, g, 2 * r2, scr[3], tri)).astype(BF16), xb)
                yb = _dot((p['scores'] * _ssd_l(p, g, 2 * r2 + 1, scr[3], tri)).astype(BF16), xb)
                slabs.append(jnp.where(lane < HEAD_DIM, ya, yb))
            y = jnp.concatenate(slabs, axis=1) + yoff + de_ref[...] * p['xs']
            ys_ref[...] = y
            h_sc[slab, :] = jnp.exp(p['acs_last_e']) * h_in + states
            zv = z_ref[...]
            yg = y * zv * _sigmoid(zv)
            r = lax.rsqrt(jnp.mean(yg * yg, axis=-1, keepdims=True) + EPS)
            yb_ref[...] = (yg * r * gain_ref[...]).astype(BF16)

        def pair(j, carry):
            group(2 * j, scr_a)
            group(2 * j + 1, scr_b)
            return carry

        _ssd_chunk(ins, scr_a)
        lax.fori_loop(0, ng // 2, pair, 0)
        _host_phase(hosted, 'finish', c == nc - 1, hins, houts, sems)

    h_in, h_out, h_shape, h_scratch, h_alias = _host_plumbing(hosted, n_in + 1, 3)
    in_specs = _ssd_in_specs(lambda i: i) + [pl.BlockSpec(memory_space=pl.ANY)] + h_in
    out_specs = [pl.BlockSpec((CHUNK, D_SSM), lambda i: (i, D_A // D_SSM)),
                 pl.BlockSpec((CHUNK, D_SSM), lambda i: (i, 0)),
                 pl.BlockSpec((None, ng * D_STATE, GW), lambda i: (i, 0, 0))] + h_out
    out_shape = [jax.ShapeDtypeStruct((t, D_MIX), BF16), jax.ShapeDtypeStruct((t, D_SSM), F32),
                 jax.ShapeDtypeStruct((nc, ng * D_STATE, GW), F32)] + h_shape
    return pl.pallas_call(
        body, name="ssd_fwd", grid=(nc,), in_specs=in_specs, out_specs=out_specs, out_shape=out_shape,
        scratch_shapes=[pltpu.VMEM((ng * D_STATE, GW), F32)] + _ssd_scratch() + _ssd_scratch()[:N_SSD_OWN]
        + h_scratch,
        input_output_aliases={n_in: 0, **h_alias},
        compiler_params=_cparams(("arbitrary",)),
    )(proj, proj, proj, proj, proj, dt_raw, cw, cb, dtb, alog, de, gain,
      tri_c, eh_c, e128_c, yab, *hosted['arrays'])


def _rows8(vals, width):
    rid = lax.broadcasted_iota(jnp.int32, (8, width), 0)
    out = jnp.zeros((8, width), F32)
    for k, v in enumerate(vals):
        if v is not None:
            out = out + jnp.where(rid == k, v, 0.0)
    return out


def _ssd_bwd(proj, dt_raw, dyab, ysave, hs, duv, cw, cb, dtb, alog, de, gain, consts, hosted):
    t = proj.shape[0]
    nc = t // CHUNK
    ng = N_SSM_GROUPS
    tri_c, eh_c, e128_c = consts
    n_in = N_SSD_IN
    nh_in, nh_out = len(hosted['arrays']), len(hosted['out_shape'])

    def per_group(g, c, ins, dy_ref, ys_ref, hs_ref, outs, scratch):
        dz_ref, dxs_ref, db_ref, dc_ref, ddt_ref, acc_x, acc_b, acc_c, acc_gain, acc_head = outs
        dh_sc, car_x, car_b, car_c, dext_x, dext_b, dext_c = scratch[:7]
        scr = scratch[7:]
        ext_x, ext_b, ext_c, acst_sc = scr[:4]
        z_ref, cwx, cwb, cwc, de_ref, gain_ref = ins[0], ins[8], ins[9], ins[10], ins[16], ins[17]
        slab = pl.ds(pl.multiple_of(g * D_STATE, D_STATE), D_STATE)
        p = _ssd_pre(c == 0, g, ins, scr)
        tri = _tri_mask()
        xs, dt_e, acs_e, xdt, decay_e = p['xs'], p['dt_e'], p['acs_e'], p['xdt'], p['decay_e']
        cm_b, bm_b, scores, eh = p['cm_b'], p['bm_b'], p['scores'], p['eh']
        h_in = hs_ref[...]
        h_in_b = h_in.astype(BF16)
        e_a = jnp.exp(acs_e)
        raw = _dot(cm_b, h_in_b)

        y = ys_ref[...]
        zv = z_ref[...]
        sz, dsz = _silu_and_grad(zv)
        yg = y * sz
        r = lax.rsqrt(jnp.mean(yg * yg, axis=-1, keepdims=True) + EPS)
        xh = yg * r
        dout = dy_ref[...]
        gain = gain_ref[...]
        dxh = dout * gain
        dyg = r * (dxh - xh * jnp.mean(dxh * xh, axis=-1, keepdims=True))
        dy = dyg * sz
        dz_ref[...] = (dyg * y * dsz).astype(BF16)
        acc_gain[g] += _rows8([jnp.sum(dout * xh, axis=0, keepdims=True)], GW)
        d_skip8 = _x01_nt(_rows8([None, None, jnp.sum(dy * xs, axis=0, keepdims=True)], GW), eh)
        dxs = de_ref[...] * dy

        q = dy * raw * e_a
        draw = (dy * e_a).astype(BF16)
        d_c = _dot_nt(draw, h_in_b)
        dh_in = _dot_tn(cm_b, draw)

        lane = lax.broadcasted_iota(jnp.int32, (CHUNK, 128), 1)
        ones_b = jnp.ones((CHUNK, 128), BF16)
        dscores = jnp.zeros((CHUNK, CHUNK), F32)
        dxdt_slabs, q_slabs = [], []
        for r2 in range(HEADS_PER_GROUP // 2):
            sl = slice(r2 * 128, (r2 + 1) * 128)
            xb = xdt[:, sl].astype(BF16)
            dys = dy[:, sl]
            dys_b = dys.astype(BF16)
            dxh_pair, qv_pair = [], []
            for half in range(2):
                lmat = _ssd_l(p, g, 2 * r2 + half, acst_sc, tri)
                m = scores * lmat
                mine = (lane < HEAD_DIM) if half == 0 else (lane >= HEAD_DIM)
                dm = _dot_nt(jnp.where(mine, dys, 0.0).astype(BF16), xb)
                dscores = dscores + dm * lmat
                gm = dm * m
                dxh_pair.append(_dot_tn(m.astype(BF16), dys_b))
                h3 = _split3(gm)
                colsum = _dot_tn(h3[0], ones_b) + _dot_tn(h3[1], ones_b) + _dot_tn(h3[2], ones_b)
                qv_pair.append(jnp.sum(gm, axis=-1, keepdims=True) - colsum)
            dxdt_slabs.append(jnp.where(lane < HEAD_DIM, dxh_pair[0], dxh_pair[1]))
            q_slabs.append(jnp.where(lane == 0, qv_pair[0], 0.0) + jnp.where(lane == HEAD_DIM, qv_pair[1], 0.0))
        dxdt = jnp.concatenate(dxdt_slabs, axis=1)
        q = q + jnp.concatenate(q_slabs, axis=1)

        dh_out = dh_sc[slab, :]
        dh_out_b = dh_out.astype(BF16)
        e_l = jnp.exp(p['acs_last_e'])
        dh_sc[slab, :] = dh_in + e_l * dh_out
        dlast = jnp.sum(dh_out * h_in, axis=0, keepdims=True) * e_l
        dxd = _dot(bm_b, dh_out_b)
        xd = xdt * decay_e
        dxdt = dxdt + dxd * decay_e
        dd = dxd * xd
        q = q - dd
        dlast = dlast + jnp.sum(dd, axis=0, keepdims=True)
        d_b = _dot_nt(xd.astype(BF16), dh_out_b)
        dsc_b = dscores.astype(BF16)
        d_c = d_c + _dot(dsc_b, bm_b)
        d_b = d_b + _dot_tn(dsc_b, cm_b)

        dxs = dxs + dxdt * dt_e
        rid = lax.broadcasted_iota(jnp.int32, (CHUNK, GW), 0)
        q = q + jnp.where(rid == CHUNK - 1, dlast, 0.0)
        dacs = _x01_nt(q, eh)
        ddt = _x01_nt(dxdt * xs, eh)
        dadt = _e01x_tn(p['tri_b'], dacs)
        ddt = ddt + dadt * p['a']
        d_a = jnp.sum(dadt * p['dt'], axis=0, keepdims=True)
        ddt_raw = ddt * _sigmoid(p['dt_in'])
        acc_head[...] += _rows8([jnp.sum(ddt_raw, axis=0, keepdims=True), d_a * p['a']], HPAD) + d_skip8

        @pl.when(g == 0)
        def _():
            ddt_ref[...] = ddt_raw

        @pl.when(g > 0)
        def _():
            ddt_ref[...] += ddt_raw

        for dv, dsil, ext, dext, car, acc, w_ref, o_ref in (
                (dxs, p['dsx'], ext_x, dext_x, car_x, acc_x, cwx, dxs_ref),
                (d_b, p['dsb'], ext_b, dext_b, car_b, acc_b, cwb, db_ref),
                (d_c, p['dsc'], ext_c, dext_c, car_c, acc_c, cwc, dc_ref)):
            dp = dv * dsil
            width = dp.shape[1]
            rows = [jnp.sum(ext[pl.ds(5 + k, CHUNK), :] * dp, axis=0, keepdims=True) for k in range(SSM_CONV)]
            rows.append(jnp.sum(dp, axis=0, keepdims=True))
            acc[g] += _rows8(rows, width)
            dext[0:CHUNK, :] = dp
            dext[CHUNK:CHUNK + 8, :] = car[g]
            car[g] = dext[0:8, :]
            dx = w_ref[SSM_CONV - 1:SSM_CONV, :] * dext[pl.ds(0, CHUNK), :]
            for k in range(SSM_CONV - 1):
                dx = dx + w_ref[k:k + 1, :] * dext[pl.ds(SSM_CONV - 1 - k, CHUNK), :]
            o_ref[...] = dx.astype(BF16)

    def body(*refs):
        ins = refs[:n_in]
        dy_all, ys_all, hs_all, duv_ref = refs[n_in:n_in + 4]
        hins = refs[n_in + 4:n_in + 4 + nh_in]
        o0 = n_in + 4 + nh_in
        dproj_ref, ddt_ref = refs[o0:o0 + 2]
        accs = refs[o0 + 2:o0 + 7]
        houts = refs[o0 + 7:o0 + 7 + nh_out]
        scratch = refs[o0 + 7 + nh_out:-2]
        sems = refs[-2:]
        cc = pl.program_id(0)
        _host_phase(hosted, 'start', cc == 0, hins, houts, sems)
        dproj_ref[:, 0:2 * D_A] = duv_ref[...]

        @pl.when(cc == 0)
        def _():
            for a in tuple(accs) + tuple(scratch[:4]):
                a[...] = jnp.zeros_like(a)

        shared = tuple(scratch[:4])
        scr_a, scr_b = _ssd_two_sets(tuple(scratch[7:14]) + tuple(scratch[17:17 + N_SSD_OWN]))
        set_a = shared + tuple(scratch[4:7]) + scr_a
        set_b = shared + tuple(scratch[14:17]) + scr_b

        def group(g, own):
            slab = pl.ds(pl.multiple_of(g * D_STATE, D_STATE), D_STATE)
            outs = (_lanes(dproj_ref, OFF_Z + g * GW, GW), _lanes(dproj_ref, OFF_XS + g * GW, GW),
                    _lanes(dproj_ref, OFF_B + g * D_STATE, D_STATE), _lanes(dproj_ref, OFF_C + g * D_STATE, D_STATE),
                    ddt_ref) + tuple(accs)
            per_group(g, nc - 1 - cc, _ssd_group_refs(ins, g), _lanes(dy_all, g * GW, GW),
                      _lanes(ys_all, g * GW, GW), hs_all.at[slab, :], outs, own)

        def pair(j, carry):
            group(2 * j, set_a)
            group(2 * j + 1, set_b)
            return carry

        _ssd_chunk(ins, scr_a)
        lax.fori_loop(0, ng // 2, pair, 0)
        _host_phase(hosted, 'finish', cc == nc - 1, hins, houts, sems)

    def cmap(i):
        return nc - 1 - i

    in_specs = _ssd_in_specs(cmap) + [
        pl.BlockSpec((CHUNK, D_SSM), lambda i: (cmap(i), D_A // D_SSM)),
        pl.BlockSpec((CHUNK, D_SSM), lambda i: (cmap(i), 0)),
        pl.BlockSpec((None, ng * D_STATE, GW), lambda i: (cmap(i), 0, 0)),
        pl.BlockSpec((CHUNK, 2 * D_A), lambda i: (cmap(i), 0)),
    ]

    def full(shape):
        return pl.BlockSpec(shape, lambda i: (0,) * len(shape))

    out_specs = [
        pl.BlockSpec((CHUNK, D_MAIN), lambda i: (cmap(i), 0)),
        pl.BlockSpec((CHUNK, HPAD), lambda i: (cmap(i), 0)),
        full((ng, 8, GW)), full((ng, 8, D_STATE)), full((ng, 8, D_STATE)),
        full((ng, 8, GW)), full((8, HPAD)),
    ]
    out_shape = [
        jax.ShapeDtypeStruct((t, D_MAIN), BF16),
        jax.ShapeDtypeStruct((t, HPAD), F32),
        jax.ShapeDtypeStruct((ng, 8, GW), F32), jax.ShapeDtypeStruct((ng, 8, D_STATE), F32),
        jax.ShapeDtypeStruct((ng, 8, D_STATE), F32), jax.ShapeDtypeStruct((ng, 8, GW), F32),
        jax.ShapeDtypeStruct((8, HPAD), F32),
    ]
    scratch = [pltpu.VMEM((ng * D_STATE, GW), F32),
               pltpu.VMEM((ng, 8, GW), F32), pltpu.VMEM((ng, 8, D_STATE), F32), pltpu.VMEM((ng, 8, D_STATE), F32),
               pltpu.VMEM((CHUNK + 8, GW), F32), pltpu.VMEM((CHUNK + 8, D_STATE), F32),
               pltpu.VMEM((CHUNK + 8, D_STATE), F32)] + _ssd_scratch()
    scratch += scratch[4:7] + _ssd_scratch()[:N_SSD_OWN]
    h_in, h_out, h_shape, h_scratch, h_alias = _host_plumbing(hosted, n_in + 4, len(out_shape))
    return pl.pallas_call(
        body, name="ssd_bwd", grid=(nc,), in_specs=in_specs + h_in, out_specs=out_specs + h_out,
        out_shape=out_shape + h_shape, scratch_shapes=scratch + h_scratch, input_output_aliases=h_alias,
        compiler_params=_cparams(("arbitrary",)),
    )(proj, proj, proj, proj, proj, dt_raw, cw, cb, dtb, alog, de, gain,
      tri_c, eh_c, e128_c, dyab, ysave, hs, duv, *hosted['arrays'])


ANY = pl.BlockSpec(memory_space=pl.ANY)


def _place():
    x, y, c = lax.axis_index("x"), lax.axis_index("y"), lax.axis_index("c")
    chips = [(1 - x, y), (x, 1 - y), (1 - x, 1 - y)]
    return x, y, c, chips


def _rcopy(src, dst, send_sems, recv_sems, k, dev):
    return pltpu.make_async_remote_copy(src_ref=src, dst_ref=dst, send_sem=send_sems.at[k],
                                        recv_sem=recv_sems.at[k], device_id=dev, device_id_type=MESH)


def _my_chip():
    return 2 * lax.axis_index("x") + lax.axis_index("y")


def _cast_into_slot(w, name):
    r, c = w.shape
    tr = _row_tile(r, c, 2)

    def body(w_ref, o_ref):
        o_ref[...] = w_ref[...].astype(BF16)

    return pl.pallas_call(
        body, name=name, grid=(r // tr,), in_specs=[pl.BlockSpec((tr, c), lambda i: (i, 0))],
        out_specs=pl.BlockSpec((None, tr, c), lambda i: (_my_chip(), i, 0)),
        out_shape=jax.ShapeDtypeStruct((N_CHIPS, r, c), BF16), compiler_params=_cparams(("parallel",)),
    )(w)


def _hosted_gather(bigs):
    nb = len(bigs)

    def rows(a, c):
        half = bigs[a].shape[1] // 2
        return pl.ds(c * half, half)

    def start(ins, outs, send_sems, recv_sems):
        x, y, c, chips = _place()
        q = 2 * x + y
        for a in range(nb):
            for k, chip in enumerate(chips):
                _rcopy(outs[a].at[q, rows(a, c)], outs[a].at[q, rows(a, c)], send_sems, recv_sems, 6 * a + k,
                       (chip[0], chip[1], c)).start()

    def mid(ins, outs, send_sems, recv_sems):
        x, y, c, chips = _place()
        sib = (x, y, 1 - c)
        for a in range(nb):
            for k, chip in enumerate(chips):
                slab = outs[a].at[2 * chip[0] + chip[1], rows(a, c)]
                _rcopy(slab, slab, send_sems, recv_sems, 6 * a + k, sib).wait_recv()
                _rcopy(slab, slab, send_sems, recv_sems, 6 * a + 3 + k, sib).start()

    def finish(ins, outs, send_sems, recv_sems):
        x, y, c, chips = _place()
        q = 2 * x + y
        sib = (x, y, 1 - c)
        for a in range(nb):
            for k, chip in enumerate(chips):
                qk = 2 * chip[0] + chip[1]
                other = outs[a].at[qk, rows(a, 1 - c)]
                _rcopy(other, other, send_sems, recv_sems, 6 * a + 3 + k, sib).wait_recv()
                mine = outs[a].at[q, rows(a, c)]
                _rcopy(mine, mine, send_sems, recv_sems, 6 * a + k, sib).wait_send()
                fwd = outs[a].at[qk, rows(a, c)]
                _rcopy(fwd, fwd, send_sems, recv_sems, 6 * a + 3 + k, sib).wait_send()

    return dict(arrays=list(bigs), out_shape=[jax.ShapeDtypeStruct(b.shape, b.dtype) for b in bigs],
                aliases={a: a for a in range(nb)}, nsem=6 * nb, start=start, mid=mid, finish=finish)


def _hosted_rs_chips(ps):
    na = len(ps)

    def copies(ins, outs, send_sems, recv_sems):
        x, y, c, chips = _place()
        return [_rcopy(ins[a].at[2 * chip[0] + chip[1]], outs[a].at[k], send_sems, recv_sems, 3 * a + k,
                       (chip[0], chip[1], c)) for a in range(na) for k, chip in enumerate(chips)]

    def start(ins, outs, send_sems, recv_sems):
        for cp in copies(ins, outs, send_sems, recv_sems):
            cp.start()

    def finish(ins, outs, send_sems, recv_sems):
        for cp in copies(ins, outs, send_sems, recv_sems):
            cp.wait()

    return dict(arrays=list(ps), out_shape=[jax.ShapeDtypeStruct((3,) + p.shape[1:], p.dtype) for p in ps],
                aliases={}, nsem=3 * na, start=start, mid=None, finish=finish)


def _hosted_rs_sibling(gs):
    na = len(gs)

    def copies(ins, outs, send_sems, recv_sems):
        x, y, c, _ = _place()
        halves = [g.shape[1] // 2 for g in gs]
        return [_rcopy(ins[a].at[:, pl.ds((1 - c) * halves[a], halves[a]), :], outs[a], send_sems, recv_sems, a,
                       (x, y, 1 - c)) for a in range(na)]

    def start(ins, outs, send_sems, recv_sems):
        for cp in copies(ins, outs, send_sems, recv_sems):
            cp.start()

    def finish(ins, outs, send_sems, recv_sems):
        for cp in copies(ins, outs, send_sems, recv_sems):
            cp.wait()

    return dict(arrays=list(gs), aliases={}, nsem=na, start=start, mid=None, finish=finish,
                out_shape=[jax.ShapeDtypeStruct((N_CHIPS, g.shape[1] // 2, g.shape[2]), g.dtype) for g in gs])


def _hosted_share_sibling(fs):
    na = len(fs)

    def rows(a, c):
        half = fs[a].shape[0] // 2
        return pl.ds(c * half, half)

    def start(ins, outs, send_sems, recv_sems):
        x, y, c, _ = _place()
        for a in range(na):
            _rcopy(outs[a].at[rows(a, c)], outs[a].at[rows(a, c)], send_sems, recv_sems, a, (x, y, 1 - c)).start()

    def finish(ins, outs, send_sems, recv_sems):
        x, y, c, _ = _place()
        for a in range(na):
            _rcopy(outs[a].at[rows(a, c)], outs[a].at[rows(a, c)], send_sems, recv_sems, a, (x, y, 1 - c)).wait_send()
            other = outs[a].at[rows(a, 1 - c)]
            _rcopy(other, other, send_sems, recv_sems, a, (x, y, 1 - c)).wait_recv()

    return dict(arrays=list(fs), out_shape=[jax.ShapeDtypeStruct(f.shape, f.dtype) for f in fs],
                aliases={a: a for a in range(na)}, nsem=na, start=start, mid=None, finish=finish)


def _hosted_allgather(buf):
    def copies(ins, outs, send_sems, recv_sems):
        x, y, c, _ = _place()
        me = 4 * x + 2 * y + c
        cps = []
        for k in range(1, 8):
            dev = (1 - x if k & 4 else x, 1 - y if k & 2 else y, 1 - c if k & 1 else c)
            cps.append(_rcopy(ins[0], outs[0].at[me], send_sems, recv_sems, k - 1, dev))
        return pltpu.make_async_copy(ins[0], outs[0].at[me], send_sems.at[7]), cps

    def start(ins, outs, send_sems, recv_sems):
        loc, cps = copies(ins, outs, send_sems, recv_sems)
        loc.start()
        for cp in cps:
            cp.start()

    def finish(ins, outs, send_sems, recv_sems):
        loc, cps = copies(ins, outs, send_sems, recv_sems)
        loc.wait()
        for cp in cps:
            cp.wait()

    return dict(arrays=[buf], out_shape=[jax.ShapeDtypeStruct((8,) + buf.shape, buf.dtype)], aliases={},
                nsem=8, start=start, mid=None, finish=finish)


class _SemWindow:
    def __init__(self, sems, off):
        self._sems, self._off = sems, off

    @property
    def at(self):
        return self

    def __getitem__(self, k):
        return self._sems.at[k + self._off]


def _hosted_join(parts):
    arrays, out_shape, aliases, spans, nsem = [], [], {}, [], 0
    for h in parts:
        spans.append((len(arrays), len(h['arrays']), len(out_shape), len(h['out_shape']), nsem))
        aliases.update({len(arrays) + a: len(out_shape) + b for a, b in h['aliases'].items()})
        arrays += h['arrays']
        out_shape += h['out_shape']
        nsem += h['nsem']

    def phase(name):
        if all(h[name] is None for h in parts):
            return None

        def run(ins, outs, send_sems, recv_sems):
            for h, (ia, na, io, no, s0) in zip(parts, spans):
                if h[name] is not None:
                    h[name](ins[ia:ia + na], outs[io:io + no], _SemWindow(send_sems, s0), _SemWindow(recv_sems, s0))

        return run

    return dict(arrays=arrays, out_shape=out_shape, aliases=aliases, nsem=nsem,
                start=phase('start'), mid=phase('mid'), finish=phase('finish'))


def _run_hosted(hosted, name):
    nh_in, nh_out = len(hosted['arrays']), len(hosted['out_shape'])

    def body(*refs):
        ins, outs, sems = refs[:nh_in], refs[nh_in:nh_in + nh_out], refs[-2:]
        for ph in ('start', 'mid', 'finish'):
            if hosted[ph] is not None:
                hosted[ph](ins, outs, sems[0], sems[1])

    h_in, h_out, h_shape, h_scratch, h_alias = _host_plumbing(hosted, 0, 0)
    return pl.pallas_call(body, name=name, in_specs=h_in, out_specs=h_out, out_shape=h_shape,
                          scratch_shapes=h_scratch, input_output_aliases=h_alias)(*hosted['arrays'])


def _host_plumbing(hosted, n_in, n_out):
    nh = len(hosted['arrays'])
    return ([ANY] * nh, [ANY] * len(hosted['out_shape']), list(hosted['out_shape']),
            [pltpu.SemaphoreType.DMA((hosted['nsem'],)), pltpu.SemaphoreType.DMA((hosted['nsem'],))],
            {n_in + a: n_out + b for a, b in hosted['aliases'].items()})


def _host_phase(hosted, phase, when, hins, houts, sems):
    fn = hosted[phase]
    if fn is None:
        return

    @pl.when(when)
    def _():
        fn(hins, houts, sems[0], sems[1])


def _gather_weights(bigs, smalls):
    nb, ns = len(bigs), len(smalls)
    na = nb + ns
    nsem = 6 * nb + 3 * ns
    big = _hosted_gather(bigs)

    def body(*refs):
        ins, outs = refs[:na], refs[na:2 * na]
        send_sems, recv_sems, loc_sems = refs[2 * na:]
        x, y, c, chips = _place()
        q = 2 * x + y
        sib = (x, y, 1 - c)
        locs, sends = [], []
        for s in range(ns):
            cp = pltpu.make_async_copy(ins[nb + s], outs[nb + s].at[q], loc_sems.at[s])
            cp.start()
            locs.append(cp)
        big['start'](ins[:nb], outs[:nb], send_sems, recv_sems)
        for s in range(ns):
            a = nb + s
            for k, chip in enumerate(chips):
                cp = _rcopy(ins[a], outs[a].at[q], send_sems, recv_sems, 6 * nb + 3 * s + k,
                            (chip[0], chip[1], c))
                cp.start()
                sends.append(cp)
        big['mid'](ins[:nb], outs[:nb], send_sems, recv_sems)
        big['finish'](ins[:nb], outs[:nb], send_sems, recv_sems)
        for s in range(ns):
            a = nb + s
            for k, chip in enumerate(chips):
                qk = 2 * chip[0] + chip[1]
                _rcopy(ins[a], outs[a].at[qk], send_sems, recv_sems, 6 * nb + 3 * s + k, sib).wait_recv()
        for cp in sends:
            cp.wait_send()
        for cp in locs:
            cp.wait()

    arrs = list(bigs) + list(smalls)
    out_shape = ([jax.ShapeDtypeStruct(a.shape, a.dtype) for a in bigs]
                 + [jax.ShapeDtypeStruct((N_CHIPS,) + a.shape, a.dtype) for a in smalls])
    return pl.pallas_call(
        body, name="gather_weights", in_specs=[ANY] * na, out_specs=[ANY] * na, out_shape=out_shape,
        input_output_aliases={a: a for a in range(nb)},
        scratch_shapes=[pltpu.SemaphoreType.DMA((nsem,)), pltpu.SemaphoreType.DMA((nsem,)),
                        pltpu.SemaphoreType.DMA((max(ns, 1),))],
    )(*arrs)


def _row_tile(rows, cols, nbuf):
    budget = 24 * 1024 * 1024 // (nbuf * cols * 4 * 2)
    return _tile(rows, max(16, budget - budget % 16), 16) if rows % 16 == 0 else rows


def _add_pairs(g, rcv, name):
    s, half, c = rcv.shape
    tr = _row_tile(half, c, 3)
    nh = half // tr

    def body(a_ref, b_ref, o_ref):
        o_ref[...] = (a_ref[...].astype(F32) + b_ref[...].astype(F32)).astype(o_ref.dtype)

    blk = pl.BlockSpec((None, tr, c), lambda j, i: (j, i, 0))
    mine = pl.BlockSpec((None, tr, c), lambda j, i: (j, lax.axis_index("c") * nh + i, 0))
    return pl.pallas_call(
        body, name=name, grid=(s, nh), in_specs=[mine, blk], out_specs=blk,
        out_shape=jax.ShapeDtypeStruct(rcv.shape, rcv.dtype), compiler_params=_cparams(("parallel", "parallel")),
    )(g, rcv)


def _sum_chips(part, rcv, name):
    _, half, c = part.shape
    tr = _row_tile(half, c, 5)
    nh = half // tr

    def body(o_ref, r_ref, out_ref):
        acc = o_ref[...].astype(F32)
        for k in range(3):
            acc = acc + r_ref[k].astype(F32)
        out_ref[...] = acc

    return pl.pallas_call(
        body, name=name, grid=(nh,),
        in_specs=[pl.BlockSpec((None, tr, c), lambda i: (_my_chip(), i, 0)),
                  pl.BlockSpec((3, tr, c), lambda i: (0, i, 0))],
        out_specs=pl.BlockSpec((tr, c), lambda i: (lax.axis_index("c") * nh + i, 0)),
        out_shape=jax.ShapeDtypeStruct((2 * half, c), F32), compiler_params=_cparams(("parallel",)),
    )(part, rcv)


def _sum_devices(parts, name):
    _, n, _ = parts.shape
    tr = _tile(n, 512, 8)

    def body(p_ref, o_ref):
        acc = p_ref[0]
        for k in range(1, 8):
            acc = acc + p_ref[k]
        o_ref[...] = acc

    return pl.pallas_call(
        body, name=name, grid=(n // tr,),
        in_specs=[pl.BlockSpec((8, tr, 128), lambda i: (0, i, 0))],
        out_specs=pl.BlockSpec((tr, 128), lambda i: (i, 0)),
        out_shape=jax.ShapeDtypeStruct((n, 128), F32), compiler_params=_cparams(("parallel",)),
    )(parts)


def _adamw(w, g, m, v, name):
    r, c = w.shape
    tr = _row_tile(r, c, 7)
    c1 = 1.0 - ADAM_B1 ** ADAM_STEP
    c2 = 1.0 - ADAM_B2 ** ADAM_STEP

    def body(w_ref, g_ref, m_ref, v_ref, d_ref, mo_ref, vo_ref):
        gv = g_ref[...]
        mn = ADAM_B1 * m_ref[...] + (1.0 - ADAM_B1) * gv
        vn = ADAM_B2 * v_ref[...] + (1.0 - ADAM_B2) * (gv * gv)
        mo_ref[...] = mn
        vo_ref[...] = vn
        m_hat = mn / c1
        v_hat = vn / c2
        d_ref[...] = -ADAM_LR * (m_hat / (jnp.sqrt(v_hat) + ADAM_EPS) + ADAM_WD * w_ref[...])

    blk = pl.BlockSpec((tr, c), lambda i: (i, 0))
    sh = jax.ShapeDtypeStruct((r, c), F32)
    return pl.pallas_call(
        body, name=name, grid=(r // tr,), in_specs=[blk] * 4, out_specs=[blk] * 3, out_shape=[sh] * 3,
        compiler_params=_cparams(("parallel",)),
    )(w, g, m, v)


WEIGHTS = ['norm_mix_g', 'w_in', 'ln_a_g', 'ln_a_b', 'w_s', 'b_s', 'norm_a_g', 'conv_ssm_w', 'conv_ssm_b',
           'dt_bias', 'a_log', 'd_skip', 'ssm_norm_g', 'w_out', 'norm_ffn_g', 'w_up', 'conv_ffn_w',
           'conv_ffn_b', 'w_down', 'norm_ple_g', 'w_ple_gate', 'w_ple', 'norm_final_g']
BIG = ['w_in', 'w_out', 'w_up', 'w_down', 'w_ple_gate', 'w_ple']
SMALL = [n for n in WEIGHTS if n not in BIG]
PACK_ALIGN = 2048


def _pack(arrs):
    parts = []
    for a in arrs:
        f = a.reshape(-1).astype(F32)
        parts.append(jnp.pad(f, (0, (-f.shape[0]) % PACK_ALIGN)))
    return jnp.concatenate(parts).reshape(-1, 128)


def _unpack(buf, shapes):
    flat = buf.reshape(-1)
    out, off = [], 0
    for s in shapes:
        n = math.prod(s)
        out.append(flat[off:off + n].reshape(s))
        off += n + (-n) % PACK_ALIGN
    return out


def _pad_heads(v):
    return jnp.pad(v, ((0, 0), (0, HPAD - v.shape[1])))


def _col_sharded(full):
    r, c4 = full.shape
    return jnp.transpose(full.reshape(r, N_CHIPS, c4 // N_CHIPS), (1, 0, 2))


def _from_col_sharded(g):
    s, r, c = g.shape
    return jnp.transpose(g, (1, 0, 2)).reshape(r, s * c)


def kernel(x, p, norm_mix_g, w_in, ln_a_g, ln_a_b, w_s, b_s, norm_a_g, conv_ssm_w, conv_ssm_b, dt_bias, a_log, d_skip, ssm_norm_g, w_out, norm_ffn_g, w_up, conv_ffn_w, conv_ffn_b, w_down, norm_ple_g, w_ple_gate, w_ple, norm_final_g, loss_target, m_norm_mix_g, m_w_in, m_ln_a_g, m_ln_a_b, m_w_s, m_b_s, m_norm_a_g, m_conv_ssm_w, m_conv_ssm_b, m_dt_bias, m_a_log, m_d_skip, m_ssm_norm_g, m_w_out, m_norm_ffn_g, m_w_up, m_conv_ffn_w, m_conv_ffn_b, m_w_down, m_norm_ple_g, m_w_ple_gate, m_w_ple, m_norm_final_g, v_norm_mix_g, v_w_in, v_ln_a_g, v_ln_a_b, v_w_s, v_b_s, v_norm_a_g, v_conv_ssm_w, v_conv_ssm_b, v_dt_bias, v_a_log, v_d_skip, v_ssm_norm_g, v_w_out, v_norm_ffn_g, v_w_up, v_conv_ffn_w, v_conv_ffn_b, v_w_down, v_norm_ple_g, v_w_ple_gate, v_w_ple, v_norm_final_g):
    given = dict(locals())
    wts = {n: given[n] for n in WEIGHTS}
    mom = {n: given['m_' + n] for n in WEIGHTS}
    var = {n: given['v_' + n] for n in WEIGHTS}
    d = D_MODEL
    xt, pt, tgt = x[0], p[0, 0], loss_target[0]
    chip = 2 * lax.axis_index("x") + lax.axis_index("y")

    slots = {n: _cast_into_slot(wts[n][0], "cast_" + n) for n in BIG}
    g_in, g_cs, g_cf = _gather_weights([slots['w_in']], [conv_ssm_w[0], conv_ffn_w[0]])
    later = [n for n in BIG if n != 'w_in']
    w_in_full = _from_col_sharded(g_in)
    w_main = w_in_full[:, :D_MAIN]
    w_dt = _pad_heads(w_in_full[:, D_MAIN:])
    cs_w = _from_col_sharded(g_cs)
    cf_w = _from_col_sharded(g_cf)
    consts = _ssd_consts()
    dtb, alog = _pad_heads(dt_bias), _pad_heads(a_log)
    de = jnp.repeat(d_skip[0], HEAD_DIM)[None, :]
    b_exp = jnp.broadcast_to(b_s[0][:, :, None], (N_GROUPS_A, CHUNK, CHUNK))

    a1 = _rms_fwd(xt, norm_mix_g, "rms_mix")
    proj = _matmul(a1, w_main, mode='nn', name="mm_proj", tm=1024, tn=1024)
    dt_raw = _matmul(a1, w_dt, mode='nn', name="mm_dt", tm=1024, tn=128)
    yab = _gmlp_fwd(proj, ln_a_g, ln_a_b, w_s[0], b_exp, norm_a_g)
    yab, ysave, hs, *gathered = _ssd_fwd(proj, dt_raw, yab, cs_w, conv_ssm_b, dtb, alog, de, ssm_norm_g, consts,
                                         _hosted_gather([slots[n] for n in later]))
    g_out, g_up, g_down, g_pg, g_ple = gathered
    w_out_f = g_out.reshape(D_MIX, d)
    w_down_f = g_down.reshape(D_FF, d)
    w_pg_f = g_pg.reshape(d, d)
    h1 = _matmul(yab, w_out_f, mode='nn', name="mm_out", res=xt, tm=512, tn=1024, tk=4096)
    f = _rms_fwd(h1, norm_ffn_g, "rms_ffn")
    hid = _matmul(f, g_up, mode='nn', name="mm_up", b_sharded=True, tm=1024, tn=1408)
    act = _ffn_act_fwd(hid, cf_w, conv_ffn_b)
    h2 = _matmul(act, w_down_f, mode='nn', name="mm_down", res=h1, tm=1024, tn=1024, tk=2816)
    n3 = _rms_fwd(h2, norm_ple_g, "rms_ple")
    gl = _matmul(n3, w_pg_f, mode='nn', name="mm_pg", tm=1024, tn=1024)
    pe = _matmul(pt, g_ple, mode='nn', name="mm_ple", b_sharded=True, tm=1024, tn=512)
    dh3, dgl, dpe, lossv, dgf = _tail(h2, gl, pe, tgt, norm_final_g[None, :])

    gs_ple = _matmul(pt, dpe, mode='tn', name="mm_dw_ple", out_dtype=BF16, out_shards=N_CHIPS,
                     tm=256, tn=512, tk=2048)
    gs_pg = _matmul(n3, dgl, mode='tn', name="mm_dw_pg", out_dtype=BF16, tm=1024, tn=1024, tk=2048)
    dn3 = _matmul(dgl, w_pg_f, mode='nt', name="mm_dn3", tm=1024, tn=1024)
    dh2, dg_ple, dh2_b = _rms_bwd(h2, norm_ple_g, dn3, dh3, "rms_ple_bwd", True)
    dact = _matmul(dh2_b, w_down_f, mode='nt', name="mm_dact", out_dtype=BF16, tm=1024, tn=1408)
    gs_down = _matmul(act, dh2_b, mode='tn', name="mm_dw_down", out_dtype=BF16, tm=1408, tn=1024, tk=2048)
    dpg, dpu, wg_acc, wu_acc = _ffn_act_bwd(hid, cf_w, conv_ffn_b, dact)
    hc = N_CHIPS // 2
    gs_up = jnp.concatenate(
        [_matmul(f, dpg, mode='tn', name="mm_dw_up_g", out_dtype=BF16, out_shards=hc, tm=1024, tn=1408, tk=2048),
         _matmul(f, dpu, mode='tn', name="mm_dw_up_u", out_dtype=BF16, out_shards=hc, tm=1024, tn=1408, tk=2048)],
        axis=0)
    early = [gs_up, gs_down.reshape(N_CHIPS, D_FF // N_CHIPS, d), gs_pg.reshape(N_CHIPS, d // N_CHIPS, d), gs_ple]
    df, *sib_e = _matmul(dpg, g_up, mode='nt', name="mm_df_g", b_sharded=True, tm=1024, tn=1024, tk=2816,
                         hosted=_hosted_rs_sibling(early))
    part_e = [_add_pairs(a, b, "rs_add_e%d" % i) for i, (a, b) in enumerate(zip(early, sib_e))]
    df = _matmul(dpu, g_up, mode='nt', name="mm_df_u", b_sharded=True, b_shard_off=hc, res=df,
                 tm=1024, tn=1024, tk=2816)
    dh1, dg_ffn, dh1_b = _rms_bwd(h1, norm_ffn_g, df, dh2, "rms_ffn_bwd", True)
    dyab = _matmul(dh1_b, w_out_f, mode='nt', name="mm_dyab", tm=1024, tn=1024)
    gs_out = _matmul(yab, dh1_b, mode='tn', name="mm_dw_out", out_dtype=BF16, tm=1024, tn=1024, tk=2048)
    duv, dws, dbs, dlng, dlnb, dnag = _gmlp_bwd(proj, dyab, ln_a_g, ln_a_b, w_s[0], b_exp, norm_a_g)
    dproj, ddt_raw, acc_x, acc_b, acc_c, acc_gain, acc_head, *rcv_e = _ssd_bwd(
        proj, dt_raw, dyab, ysave, hs, duv, cs_w, conv_ssm_b, dtb, alog, de, ssm_norm_g, consts,
        _hosted_rs_chips(part_e))
    dw_main = _matmul(a1, dproj, mode='tn', name="mm_dw_main", out_dtype=BF16, tm=1024, tn=1024, tk=2048)
    dw_dt = _matmul(a1, ddt_raw, mode='tn', name="mm_dw_dt", out_dtype=BF16, tm=1024, tn=128, tk=2048)
    gs_in = _col_sharded(jnp.concatenate([dw_main, dw_dt[:, :N_HEADS]], axis=1))
    late = [gs_in, gs_out.reshape(N_CHIPS, D_MIX // N_CHIPS, d)]
    part_l = [_add_pairs(a, b, "rs_add_l%d" % i) for i, (a, b) in enumerate(
        zip(late, _run_hosted(_hosted_rs_sibling(late), "rs_sibling_late")))]

    def conv_rows(acc, k):
        return acc[:, k, :].reshape(1, -1)

    dcw = jnp.concatenate([jnp.concatenate([conv_rows(acc_x, k), conv_rows(acc_b, k), conv_rows(acc_c, k)], axis=1)
                           for k in range(SSM_CONV)], axis=0)
    dcb = jnp.concatenate([conv_rows(acc_x, SSM_CONV), conv_rows(acc_b, SSM_CONV), conv_rows(acc_c, SSM_CONV)], axis=1)
    part = {
        'ln_a_g': dlng, 'ln_a_b': dlnb, 'w_s': dws, 'b_s': dbs, 'norm_a_g': dnag,
        'conv_ssm_w': dcw, 'conv_ssm_b': dcb,
        'dt_bias': acc_head[0:1, :N_HEADS], 'a_log': acc_head[1:2, :N_HEADS], 'd_skip': acc_head[2:3, :N_HEADS],
        'ssm_norm_g': acc_gain[:, 0, :], 'norm_ffn_g': dg_ffn,
        'conv_ffn_w': jnp.concatenate([wg_acc[:FFN_CONV], wu_acc[:FFN_CONV]], axis=1),
        'conv_ffn_b': jnp.concatenate([wg_acc[FFN_CONV:FFN_CONV + 1], wu_acc[FFN_CONV:FFN_CONV + 1]], axis=1),
        'norm_ple_g': dg_ple, 'norm_final_g': dgf,
    }
    full_shapes = {n: wts[n].shape for n in SMALL}
    full_shapes['conv_ssm_w'] = (1, SSM_CONV, D_XBC)
    full_shapes['conv_ffn_w'] = (1, FFN_CONV, 2 * D_FF)
    small_e = [n for n in SMALL if n != 'norm_mix_g']
    packed = _pack([part[n] for n in small_e] + [lossv[:, 0:1]])

    halves_e = [_sum_chips(a, b, "rs_sum_e%d" % i) for i, (a, b) in enumerate(zip(part_e, rcv_e))]
    da_dt = _matmul(ddt_raw, w_dt, mode='nt', name="mm_da_dt", tm=1024, tn=1024)
    da, *moved = _matmul(dproj, w_main, mode='nt', name="mm_da", res=da_dt, tm=1024, tn=1024, tk=2560,
                         hosted=_hosted_join([_hosted_rs_chips(part_l), _hosted_share_sibling(halves_e),
                                              _hosted_allgather(packed)]))
    rcv_l, g_early, gathered = moved[:len(late)], moved[len(late):-1], moved[-1]
    dx, dg_mix = _rms_bwd(xt, norm_mix_g, da, dh1, "rms_mix_bwd", False)

    halves_l = [_sum_chips(a, b, "rs_sum_l%d" % i) for i, (a, b) in enumerate(zip(part_l, rcv_l))]
    g_big = dict(zip(['w_up', 'w_down', 'w_ple_gate', 'w_ple'], g_early))
    g_big.update(zip(['w_in', 'w_out'], _run_hosted(_hosted_share_sibling(halves_l), "share_sibling_late")))

    pieces = _unpack(_sum_devices(gathered, "sum_devices"), [full_shapes[n] for n in small_e] + [(1,)])
    g_small = dict(zip(small_e, pieces[:-1]))
    loss = pieces[-1][0]
    mix = _sum_devices(_run_hosted(_hosted_allgather(_pack([dg_mix])), "allgather_mix")[0], "sum_devices_mix")
    g_small['norm_mix_g'] = _unpack(mix, [full_shapes['norm_mix_g']])[0]
    for n in ('conv_ssm_w', 'conv_ffn_w'):
        width = wts[n].shape[2]
        g_small[n] = lax.dynamic_slice_in_dim(g_small[n], chip * width, width, axis=2)

    grads, delta, new_m, new_v = {}, {}, {}, {}
    for n in BIG:
        shp = wts[n].shape
        dl, mn, vn = _adamw(wts[n][0], g_big[n], mom[n][0], var[n][0], "adamw_" + n)
        grads[n], delta[n], new_m[n], new_v[n] = (g_big[n].reshape(shp), dl.reshape(shp), mn.reshape(shp),
                                                  vn.reshape(shp))
    shapes = [wts[n].shape for n in SMALL]
    dl, mn, vn = _adamw(_pack([wts[n] for n in SMALL]), _pack([g_small[n] for n in SMALL]),
                        _pack([mom[n] for n in SMALL]), _pack([var[n] for n in SMALL]), "adamw_small")
    for n, a, b, c in zip(SMALL, _unpack(dl, shapes), _unpack(mn, shapes), _unpack(vn, shapes)):
        grads[n], delta[n], new_m[n], new_v[n] = g_small[n], a, b, c

    return (loss, dx[None], *[grads[n] for n in WEIGHTS], *[delta[n] for n in WEIGHTS],
            *[new_m[n] for n in WEIGHTS], *[new_v[n] for n in WEIGHTS])
```

```python
import functools
import math

import jax
import jax.numpy as jnp
from jax import lax
from jax.experimental import pallas as pl
from jax.experimental.pallas import tpu as pltpu

D_MODEL = 2048
SEQ = 8192
D_MIX = 2 * D_MODEL
D_A = D_MIX // 2
CHUNK = 128
N_GROUPS_A = D_A // 128
D_SSM = D_MIX - D_A
HEAD_DIM = 64
N_HEADS = D_SSM // HEAD_DIM
HEADS_PER_GROUP = 4
N_SSM_GROUPS = N_HEADS // HEADS_PER_GROUP
GW = HEADS_PER_GROUP * HEAD_DIM
D_STATE = 128
SSM_CONV = 4
D_BC = N_SSM_GROUPS * D_STATE
D_XBC = D_SSM + 2 * D_BC
D_MAIN = 2 * D_A + D_SSM + D_XBC
D_IN = D_MAIN + N_HEADS
D_FF = (D_MODEL * 11) // 4
FFN_CONV = 3
D_PLE = 256
EPS = 1e-6
HPAD = 128
N_CHIPS = 4

ADAM_LR = 0.001
ADAM_B1 = 0.9
ADAM_B2 = 0.999
ADAM_EPS = 1e-08
ADAM_WD = 0.01
ADAM_STEP = 10

F32 = jnp.float32
BF16 = jnp.bfloat16
MESH = pl.DeviceIdType.MESH
VMEM_LIMIT = 56 * 1024 * 1024


def _cparams(sem):
    return pltpu.CompilerParams(dimension_semantics=sem, vmem_limit_bytes=VMEM_LIMIT)


def _tile(n, pref, mult):
    t = min(pref, n)
    t -= t % mult
    while n % t:
        t -= mult
    return t


def _dot(a, b):
    return jnp.dot(a, b, preferred_element_type=F32)


def _dot_nt(a, b):
    return lax.dot_general(a, b, (((1,), (1,)), ((), ())), preferred_element_type=F32)


def _dot_tn(a, b):
    return lax.dot_general(a, b, (((0,), (0,)), ((), ())), preferred_element_type=F32)


def _split3(x):
    hi = x.astype(BF16)
    r = x - hi.astype(F32)
    mid = r.astype(BF16)
    lo = (r - mid.astype(F32)).astype(BF16)
    return hi, mid, lo


def _x01(x, e):
    h, m, l = _split3(x)
    return _dot(h, e) + _dot(m, e) + _dot(l, e)


def _x01_nt(x, e):
    h, m, l = _split3(x)
    return _dot_nt(h, e) + _dot_nt(m, e) + _dot_nt(l, e)


def _e01x(e, x):
    h, m, l = _split3(x)
    return _dot(e, h) + _dot(e, m) + _dot(e, l)


def _e01x_tn(e, x):
    h, m, l = _split3(x)
    return _dot_tn(e, h) + _dot_tn(e, m) + _dot_tn(e, l)


def _sigmoid(x):
    return 1.0 / (1.0 + jnp.exp(-x))


_GELU_C = math.sqrt(2.0 / math.pi)


def _gelu_and_grad(x):
    x2 = x * x
    th = jnp.tanh(_GELU_C * (x + 0.044715 * x * x2))
    y = 0.5 * x * (1.0 + th)
    dy = 0.5 * (1.0 + th) + 0.5 * x * (1.0 - th * th) * (_GELU_C * (1.0 + 3.0 * 0.044715 * x2))
    return y, dy


def _silu_and_grad(x):
    s = _sigmoid(x)
    return x * s, s * (1.0 + x * (1.0 - s))


def _softplus(x):
    u = jnp.exp(-jnp.abs(x))
    w = 1.0 + u
    l1p = jnp.where(w == 1.0, u, jnp.log(w) * (u / (w - 1.0)))
    return jnp.maximum(x, 0.0) + l1p


def _matmul(a, b, *, mode, name, out_dtype=F32, res=None, tm=512, tn=512, tk=2048,
            b_sharded=False, b_shard_off=0, out_shards=0, hosted=None):
    if mode == 'tn':
        kdim, m = a.shape
        n = b.shape[1]
    else:
        m, kdim = a.shape
        if b_sharded:
            s_b, d1, d2 = b.shape
            n = s_b * d2 if mode == 'nn' else d1
        else:
            n = b.shape[1] if mode == 'nn' else b.shape[0]
    per = None
    if b_sharded:
        per = b.shape[2]
    if out_shards:
        per = n // out_shards
    tm = _tile(m, tm, 128 if mode == 'tn' else 8)
    if mode == 'nt' and b_sharded:
        tn = _tile(n, tn, 128)
        tk = _tile(per, tk, 128)
    elif per is not None:
        tn = _tile(per, tn, 128)
        tk = _tile(kdim, tk, 128 if mode != 'tn' else 8)
    else:
        tn = _tile(n, tn, 128)
        tk = _tile(kdim, tk, 128 if mode != 'tn' else 8)
    nm, nn_, nk = m // tm, n // tn, kdim // tk
    has_res = res is not None
    n_in = 2 + has_res
    nh_in = len(hosted['arrays']) if hosted else 0
    nh_out = len(hosted['out_shape']) if hosted else 0

    def body(*refs):
        a_ref, b_ref = refs[0], refs[1]
        res_ref = refs[2] if has_res else None
        o_ref = refs[n_in + nh_in]
        if hosted:
            hins = refs[n_in:n_in + nh_in]
            houts = refs[n_in + nh_in + 1:n_in + nh_in + 1 + nh_out]
            sems = refs[-2:]
            ids = [pl.program_id(d) for d in range(3)]
            at_first = jnp.logical_and(jnp.logical_and(ids[0] == 0, ids[1] == 0), ids[2] == 0)
            at_last = jnp.logical_and(jnp.logical_and(ids[0] == nn_ - 1, ids[1] == nm - 1), ids[2] == nk - 1)
            _host_phase(hosted, 'start', at_first, hins, houts, sems)
        av = a_ref[...].astype(BF16)
        bv = b_ref[...].astype(BF16)
        if mode == 'nn':
            p = _dot(av, bv)
        elif mode == 'nt':
            p = _dot_nt(av, bv)
        else:
            p = _dot_tn(av, bv)

        def fin(v):
            if has_res:
                v = v + res_ref[...]
            o_ref[...] = v.astype(o_ref.dtype)

        if nk == 1:
            fin(p)
        else:
            acc_ref = refs[n_in + nh_in + 1 + nh_out]
            k = pl.program_id(2)

            @pl.when(k == 0)
            def _():
                acc_ref[...] = p

            @pl.when(k > 0)
            def _():
                acc_ref[...] += p

            @pl.when(k == nk - 1)
            def _():
                fin(acc_ref[...])
        if hosted:
            _host_phase(hosted, 'mid', at_last, hins, houts, sems)
            _host_phase(hosted, 'finish', at_last, hins, houts, sems)

    if mode == 'nn':
        a_spec = pl.BlockSpec((tm, tk), lambda j, i, k: (i, k))
        if b_sharded:
            nps = per // tn
            b_spec = pl.BlockSpec((None, tk, tn), lambda j, i, k: (j // nps, k, j % nps))
        else:
            b_spec = pl.BlockSpec((tk, tn), lambda j, i, k: (k, j))
    elif mode == 'nt':
        a_spec = pl.BlockSpec((tm, tk), lambda j, i, k: (i, k))
        if b_sharded:
            kps = per // tk
            b_spec = pl.BlockSpec((None, tn, tk), lambda j, i, k: (k // kps + b_shard_off, j, k % kps))
        else:
            b_spec = pl.BlockSpec((tn, tk), lambda j, i, k: (j, k))
    else:
        a_spec = pl.BlockSpec((tk, tm), lambda j, i, k: (k, i))
        b_spec = pl.BlockSpec((tk, tn), lambda j, i, k: (k, j))
    in_specs = [a_spec, b_spec]
    args = [a, b]
    if has_res:
        in_specs.append(pl.BlockSpec((tm, tn), lambda j, i, k: (i, j)))
        args.append(res)
    if out_shards:
        nps_o = per // tn
        out_shape = jax.ShapeDtypeStruct((out_shards, m, per), out_dtype)
        out_spec = pl.BlockSpec((None, tm, tn), lambda j, i, k: (j // nps_o, i, j % nps_o))
    else:
        out_shape = jax.ShapeDtypeStruct((m, n), out_dtype)
        out_spec = pl.BlockSpec((tm, tn), lambda j, i, k: (i, j))
    scratch = [pltpu.VMEM((tm, tn), F32)] if nk > 1 else []
    if not hosted:
        return pl.pallas_call(
            body, name=name, grid=(nn_, nm, nk), in_specs=in_specs, out_specs=out_spec,
            out_shape=out_shape, scratch_shapes=scratch,
            compiler_params=_cparams(("parallel", "parallel", "arbitrary")),
        )(*args)
    h_in, h_out, h_shape, h_scratch, h_alias = _host_plumbing(hosted, n_in, 1)
    return pl.pallas_call(
        body, name=name, grid=(nn_, nm, nk), in_specs=in_specs + h_in, out_specs=[out_spec] + h_out,
        out_shape=[out_shape] + h_shape, scratch_shapes=scratch + h_scratch, input_output_aliases=h_alias,
        compiler_params=_cparams(("arbitrary", "arbitrary", "arbitrary")),
    )(*args, *hosted['arrays'])


def _rms_fwd(x, g, name):
    t, d = x.shape
    tt = _tile(t, 512, 8)

    def body(x_ref, g_ref, o_ref):
        xv = x_ref[...]
        r = lax.rsqrt(jnp.mean(xv * xv, axis=-1, keepdims=True) + EPS)
        o_ref[...] = (xv * r * g_ref[...]).astype(o_ref.dtype)

    return pl.pallas_call(
        body, name=name, grid=(t // tt,),
        in_specs=[pl.BlockSpec((tt, d), lambda i: (i, 0)), pl.BlockSpec((1, d), lambda i: (0, 0))],
        out_specs=pl.BlockSpec((tt, d), lambda i: (i, 0)),
        out_shape=jax.ShapeDtypeStruct((t, d), BF16),
        compiler_params=_cparams(("parallel",)),
    )(x, g)


def _rms_bwd(x, g, dy, dres, name, also_bf16):
    t, d = x.shape
    tt = _tile(t, 256, 16)

    def body(x_ref, g_ref, dy_ref, dres_ref, dx_ref, dg_ref, *dxb):
        i = pl.program_id(0)
        xv = x_ref[...]
        r = lax.rsqrt(jnp.mean(xv * xv, axis=-1, keepdims=True) + EPS)
        xh = xv * r
        dyv = dy_ref[...].astype(F32)
        dxh = dyv * g_ref[...]
        c = jnp.mean(dxh * xh, axis=-1, keepdims=True)
        dx = dres_ref[...] + r * (dxh - xh * c)
        dx_ref[...] = dx
        for dxb_ref in dxb:
            dxb_ref[...] = dx.astype(BF16)
        part = jnp.sum(dyv * xh, axis=0, keepdims=True)

        @pl.when(i == 0)
        def _():
            dg_ref[...] = part

        @pl.when(i > 0)
        def _():
            dg_ref[...] += part

    row = pl.BlockSpec((tt, d), lambda i: (i, 0))
    vec = pl.BlockSpec((1, d), lambda i: (0, 0))
    return pl.pallas_call(
        body, name=name, grid=(t // tt,),
        in_specs=[row, vec, row, row], out_specs=[row, vec] + [row] * also_bf16,
        out_shape=[jax.ShapeDtypeStruct((t, d), F32), jax.ShapeDtypeStruct((1, d), F32)]
        + [jax.ShapeDtypeStruct((t, d), BF16)] * also_bf16,
        compiler_params=_cparams(("arbitrary",)),
    )(x, g, dy, dres)


def _tail(h2, gl, pe, target, gfin):
    t, d = h2.shape
    tt = _tile(t, 256, 8)

    def body(h2_ref, gl_ref, pe_ref, tg_ref, gf_ref, dh3_ref, dgl_ref, dpe_ref, loss_ref, dgf_ref):
        i = pl.program_id(0)
        sig = _sigmoid(gl_ref[...])
        pev = pe_ref[...]
        h3 = h2_ref[...] + sig * pev
        r = lax.rsqrt(jnp.mean(h3 * h3, axis=-1, keepdims=True) + EPS)
        xh = h3 * r
        gf = gf_ref[...]
        e = xh * gf - tg_ref[...]
        dy = e * (1.0 / d)
        dxh = dy * gf
        c = jnp.mean(dxh * xh, axis=-1, keepdims=True)
        dh3 = r * (dxh - xh * c)
        dh3_ref[...] = dh3
        dgl_ref[...] = (dh3 * pev * sig * (1.0 - sig)).astype(BF16)
        dpe_ref[...] = (dh3 * sig).astype(BF16)
        lpart = jnp.sum(e * e, axis=0, keepdims=True) * (0.5 / d)
        gpart = jnp.sum(dy * xh, axis=0, keepdims=True)

        @pl.when(i == 0)
        def _():
            loss_ref[...] = lpart
            dgf_ref[...] = gpart

        @pl.when(i > 0)
        def _():
            loss_ref[...] += lpart
            dgf_ref[...] += gpart

        @pl.when(i == t // tt - 1)
        def _():
            loss_ref[...] = jnp.broadcast_to(jnp.sum(loss_ref[...], axis=-1, keepdims=True), (1, d))

    row = pl.BlockSpec((tt, d), lambda i: (i, 0))
    vec = pl.BlockSpec((1, d), lambda i: (0, 0))
    return pl.pallas_call(
        body, name="tail", grid=(t // tt,),
        in_specs=[row, row, row, row, vec], out_specs=[row, row, row, vec, vec],
        out_shape=[jax.ShapeDtypeStruct((t, d), F32), jax.ShapeDtypeStruct((t, d), BF16),
                   jax.ShapeDtypeStruct((t, d), BF16), jax.ShapeDtypeStruct((1, d), F32),
                   jax.ShapeDtypeStruct((1, d), F32)],
        compiler_params=_cparams(("arbitrary",)),
    )(h2, gl, pe, target, gfin)


def _conv(cur_ref, prev_ref, w_ref, b_ref, ext_ref, first, width):
    rows = cur_ref.shape[0]
    ext_ref[0:8, :] = jnp.where(first, 0.0, prev_ref[...])
    ext_ref[8:8 + rows, :] = cur_ref[...]
    acc = b_ref[...]
    for k in range(width):
        acc = acc + w_ref[k:k + 1, :] * ext_ref[pl.ds(9 - width + k, rows), :]
    return acc


def _ffn_specs(t, tt, tc, nf):
    hb = tt // 8
    cur_g = pl.BlockSpec((tt, tc), lambda j, i: (i, j))
    cur_u = pl.BlockSpec((tt, tc), lambda j, i: (i, j + nf))
    prev_g = pl.BlockSpec((8, tc), lambda j, i: (jnp.maximum(i * hb - 1, 0), j))
    prev_u = pl.BlockSpec((8, tc), lambda j, i: (jnp.maximum(i * hb - 1, 0), j + nf))
    w_g = pl.BlockSpec((FFN_CONV, tc), lambda j, i: (0, j))
    w_u = pl.BlockSpec((FFN_CONV, tc), lambda j, i: (0, j + nf))
    b_g = pl.BlockSpec((1, tc), lambda j, i: (0, j))
    b_u = pl.BlockSpec((1, tc), lambda j, i: (0, j + nf))
    return [cur_g, prev_g, cur_u, prev_u, w_g, w_u, b_g, b_u]


FFN_TC = 512


def _shift_down(prev, cur, n, rid):
    return jnp.where(rid < n, pltpu.roll(prev, n, 0), pltpu.roll(cur, n, 0))


def _shift_up(cur, nxt, n, rid):
    return jnp.where(rid < 8 - n, pltpu.roll(cur, 8 - n, 0), pltpu.roll(nxt, 8 - n, 0))


def _conv3_group(prev, cur, w_ref, b_ref, rid):
    x1 = _shift_down(prev, cur, 1, rid)
    x2 = _shift_down(prev, cur, 2, rid)
    return b_ref[...] + w_ref[2:3, :] * cur + w_ref[1:2, :] * x1 + w_ref[0:1, :] * x2


def _ffn_act_fwd(hid, cw, cb):
    t = hid.shape[0]
    tt = _tile(t, 512, 16)
    tc = _tile(D_FF, FFN_TC, 128)
    nf = D_FF // tc

    def body(g_ref, gp_ref, u_ref, up_ref, wg_ref, wu_ref, bg_ref, bu_ref, o_ref):
        first = pl.program_id(1) == 0
        rid = lax.broadcasted_iota(jnp.int32, (8, tc), 0)

        def act(pg, cg, pu, cu):
            gate = _conv3_group(pg, cg, wg_ref, bg_ref, rid)
            up = _conv3_group(pu, cu, wu_ref, bu_ref, rid)
            return gate * _sigmoid(gate) * up

        def step(s, carry):
            pg, pu = carry
            r0 = pl.multiple_of(s * 16, 16)
            g0, g1 = g_ref[pl.ds(r0, 8), :], g_ref[pl.ds(r0 + 8, 8), :]
            u0, u1 = u_ref[pl.ds(r0, 8), :], u_ref[pl.ds(r0 + 8, 8), :]
            out = jnp.concatenate([act(pg, g0, pu, u0), act(g0, g1, u0, u1)], axis=0)
            o_ref[pl.ds(r0, 16), :] = out.astype(BF16)
            return g1, u1

        init = (jnp.where(first, 0.0, gp_ref[...]), jnp.where(first, 0.0, up_ref[...]))
        lax.fori_loop(0, tt // 16, step, init)

    return pl.pallas_call(
        body, name="ffn_act_fwd", grid=(nf, t // tt), in_specs=_ffn_specs(t, tt, tc, nf),
        out_specs=pl.BlockSpec((tt, tc), lambda j, i: (i, j)),
        out_shape=jax.ShapeDtypeStruct((t, D_FF), BF16),
        compiler_params=_cparams(("parallel", "arbitrary")),
    )(hid, hid, hid, hid, cw, cw, cb, cb)


def _ffn_act_bwd(hid, cw, cb, dact):
    t = hid.shape[0]
    tt = _tile(t, 512, 16)
    tc = _tile(D_FF, FFN_TC, 128)
    nf = D_FF // tc
    nt = t // tt
    n16 = tt // 16
    hb = tt // 8

    def body(g_ref, gp_ref, u_ref, up_ref, wg_ref, wu_ref, bg_ref, bu_ref, gn_ref, un_ref, da_ref, dan_ref,
             og_ref, ou_ref, ag_ref, au_ref, accs):
        i = pl.program_id(1)
        first, last = i == 0, i == nt - 1
        rid = lax.broadcasted_iota(jnp.int32, (8, tc), 0)
        accs[...] = jnp.zeros_like(accs)

        def dgroup(pg, cg, pu, cu, da):
            gate = _conv3_group(pg, cg, wg_ref, bg_ref, rid)
            up = _conv3_group(pu, cu, wu_ref, bu_ref, rid)
            sv, sgr = _silu_and_grad(gate)
            return da * up * sgr, da * sv

        def finish(x, d0, d1, w_ref):
            s1 = _shift_up(d0, d1, 1, rid)
            s2 = _shift_up(d0, d1, 2, rid)
            dpre = w_ref[2:3, :] * d0 + w_ref[1:2, :] * s1 + w_ref[0:1, :] * s2
            return dpre, (x * s2, x * s1, x * d0, d0)

        def two_groups(it, carry, x2g, x2u, da_blk, zero_ahead):
            x0g, x0u, d0g, d0u, da1 = carry
            r0 = it * 16 if isinstance(it, int) else pl.multiple_of(it * 16, 16)
            x1g, x1u = g_ref[pl.ds(r0 + 8, 8), :], u_ref[pl.ds(r0 + 8, 8), :]
            d1g, d1u = dgroup(x0g, x1g, x0u, x1u, da1)
            d2g, d2u = dgroup(x1g, x2g, x1u, x2u, da_blk[0:8])
            d2g = jnp.where(zero_ahead, 0.0, d2g)
            d2u = jnp.where(zero_ahead, 0.0, d2u)
            outs = []
            for half, (xa, xb, da_, db_, dc_, w_ref, o_ref) in enumerate((
                    (x0g, x1g, d0g, d1g, d2g, wg_ref, og_ref), (x0u, x1u, d0u, d1u, d2u, wu_ref, ou_ref))):
                pa, prods_a = finish(xa, da_, db_, w_ref)
                pb, prods_b = finish(xb, db_, dc_, w_ref)
                o_ref[pl.ds(r0, 16), :] = jnp.concatenate([pa, pb], axis=0).astype(BF16)
                for k in range(4):
                    accs[4 * half + k] += prods_a[k] + prods_b[k]
            return x2g, x2u, d2g, d2u, da_blk[8:16]

        def step(it, carry):
            r1 = pl.multiple_of(it * 16 + 16, 16)
            return two_groups(it, carry, g_ref[pl.ds(r1, 8), :], u_ref[pl.ds(r1, 8), :],
                              da_ref[pl.ds(r1, 16), :].astype(F32), False)

        da0 = da_ref[pl.ds(0, 16), :].astype(F32)
        x0g, x0u = g_ref[pl.ds(0, 8), :], u_ref[pl.ds(0, 8), :]
        d0g, d0u = dgroup(jnp.where(first, 0.0, gp_ref[...]), x0g, jnp.where(first, 0.0, up_ref[...]), x0u, da0[0:8])
        carry = lax.fori_loop(0, n16 - 1, step, (x0g, x0u, d0g, d0u, da0[8:16]))
        two_groups(n16 - 1, carry, gn_ref[...], un_ref[...], dan_ref[...].astype(F32), last)

        @pl.when(first)
        def _():
            ag_ref[...] = jnp.zeros_like(ag_ref)
            au_ref[...] = jnp.zeros_like(au_ref)

        for half, a_ref in enumerate((ag_ref, au_ref)):
            for k in range(4):
                a_ref[k:k + 1, :] += jnp.sum(accs[4 * half + k], axis=0, keepdims=True)

    def nxt8(i):
        return jnp.minimum((i + 1) * hb, t // 8 - 1)

    def nxt16(i):
        return jnp.minimum((i + 1) * n16, t // 16 - 1)

    out_blk = pl.BlockSpec((tt, tc), lambda j, i: (i, j))
    acc_spec = pl.BlockSpec((8, tc), lambda j, i: (0, j))
    in_specs = _ffn_specs(t, tt, tc, nf) + [
        pl.BlockSpec((8, tc), lambda j, i: (nxt8(i), j)), pl.BlockSpec((8, tc), lambda j, i: (nxt8(i), j + nf)),
        out_blk, pl.BlockSpec((16, tc), lambda j, i: (nxt16(i), j))]
    return pl.pallas_call(
        body, name="ffn_act_bwd", grid=(nf, nt), in_specs=in_specs,
        out_specs=[out_blk, out_blk, acc_spec, acc_spec],
        out_shape=[jax.ShapeDtypeStruct((t, D_FF), BF16), jax.ShapeDtypeStruct((t, D_FF), BF16),
                   jax.ShapeDtypeStruct((8, D_FF), F32), jax.ShapeDtypeStruct((8, D_FF), F32)],
        scratch_shapes=[pltpu.VMEM((8, 8, tc), F32)],
        compiler_params=_cparams(("parallel", "arbitrary")),
    )(hid, hid, hid, hid, cw, cw, cb, cb, hid, hid, dact, dact)


def _tri_mask():
    r = lax.broadcasted_iota(jnp.int32, (CHUNK, CHUNK), 0)
    c = lax.broadcasted_iota(jnp.int32, (CHUNK, CHUNK), 1)
    return r >= c


def _gmlp_group_fwd(uv_ref, lng_ref, lnb_ref, ws_ref, bexp_ref, tri, g, want_grad):
    lo, hi = g * 128, (g + 1) * 128
    u_pre = uv_ref[:, lo:hi]
    v_pre = uv_ref[:, D_A + lo:D_A + hi]
    u, du = _gelu_and_grad(u_pre)
    v, dv = _gelu_and_grad(v_pre)
    mu = jnp.mean(v, axis=-1, keepdims=True)
    dc = v - mu
    rs = lax.rsqrt(jnp.mean(dc * dc, axis=-1, keepdims=True) + EPS)
    xh = dc * rs
    vn = (xh * lng_ref[:, lo:hi] + lnb_ref[:, lo:hi]).astype(BF16)
    w = jnp.where(tri, ws_ref[g], 0.0).astype(BF16)
    sg = _dot(w, vn) + bexp_ref[g]
    if want_grad:
        return u, du, dv, rs, xh, vn, w, sg
    return u * sg


def _gmlp_fwd(proj, ln_g, ln_b, w_s, b_exp, na_g):
    t = proj.shape[0]
    ng = N_GROUPS_A

    def body(uv_ref, lng_ref, lnb_ref, ws_ref, bexp_ref, nag_ref, o_ref):
        tri = _tri_mask()
        ys = [_gmlp_group_fwd(uv_ref, lng_ref, lnb_ref, ws_ref, bexp_ref, tri, g, False) for g in range(ng)]
        ssq = ys[0] * 0.0
        for y in ys:
            ssq = ssq + y * y
        r = lax.rsqrt(jnp.sum(ssq, axis=-1, keepdims=True) * (1.0 / D_A) + EPS)
        for g, y in enumerate(ys):
            o_ref[:, g * 128:(g + 1) * 128] = (y * r * nag_ref[:, g * 128:(g + 1) * 128]).astype(BF16)

    vec = pl.BlockSpec((1, D_A), lambda i: (0, 0))
    cube = pl.BlockSpec((ng, CHUNK, CHUNK), lambda i: (0, 0, 0))
    return pl.pallas_call(
        body, name="gmlp_fwd", grid=(t // CHUNK,),
        in_specs=[pl.BlockSpec((CHUNK, 2 * D_A), lambda i: (i, 0)), vec, vec, cube, cube, vec],
        out_specs=pl.BlockSpec((CHUNK, D_A), lambda i: (i, 0)),
        out_shape=jax.ShapeDtypeStruct((t, D_A), BF16),
        compiler_params=_cparams(("parallel",)),
    )(proj, ln_g, ln_b, w_s, b_exp, na_g)


def _gmlp_bwd(proj, dyab, ln_g, ln_b, w_s, b_exp, na_g):
    t = proj.shape[0]
    ng = N_GROUPS_A
    nsteps = t // CHUNK

    def body(uv_ref, dy_ref, lng_ref, lnb_ref, ws_ref, bexp_ref, nag_ref,
             duv_ref, dws_ref, dbs_ref, dlng_ref, dlnb_ref, dnag_ref, dbacc):
        i = pl.program_id(0)
        tri = _tri_mask()

        @pl.when(i == 0)
        def _():
            dws_ref[...] = jnp.zeros_like(dws_ref)
            dbacc[...] = jnp.zeros_like(dbacc)
            dlng_ref[...] = jnp.zeros_like(dlng_ref)
            dlnb_ref[...] = jnp.zeros_like(dlnb_ref)
            dnag_ref[...] = jnp.zeros_like(dnag_ref)

        st = [_gmlp_group_fwd(uv_ref, lng_ref, lnb_ref, ws_ref, bexp_ref, tri, g, True) for g in range(ng)]
        ssq = st[0][0] * 0.0
        for s in st:
            y = s[0] * s[7]
            ssq = ssq + y * y
        r = lax.rsqrt(jnp.sum(ssq, axis=-1, keepdims=True) * (1.0 / D_A) + EPS)
        csum = st[0][0] * 0.0
        for g, s in enumerate(st):
            sl = slice(g * 128, (g + 1) * 128)
            xhy = s[0] * s[7] * r
            dya = dy_ref[:, sl]
            dnag_ref[:, sl] += jnp.sum(dya * xhy, axis=0, keepdims=True)
            csum = csum + dya * nag_ref[:, sl] * xhy
        c1 = jnp.sum(csum, axis=-1, keepdims=True) * (1.0 / D_A)
        for g, s in enumerate(st):
            u, du, dv, rs, xh, vn, w, sg = s
            sl = slice(g * 128, (g + 1) * 128)
            dy = r * (dy_ref[:, sl] * nag_ref[:, sl] - u * sg * r * c1)
            dsg = dy * u
            dsg_b = dsg.astype(BF16)
            dws_ref[g] += _dot_nt(dsg_b, vn)
            dbacc[g] += dsg
            dvn = _dot_tn(w, dsg_b)
            dlnb_ref[:, sl] += jnp.sum(dvn, axis=0, keepdims=True)
            dlng_ref[:, sl] += jnp.sum(dvn * xh, axis=0, keepdims=True)
            dxh = dvn * lng_ref[:, sl]
            dvv = rs * (dxh - jnp.mean(dxh, axis=-1, keepdims=True)
                        - xh * jnp.mean(dxh * xh, axis=-1, keepdims=True))
            duv_ref[:, sl] = (dy * sg * du).astype(BF16)
            duv_ref[:, D_A + g * 128:D_A + (g + 1) * 128] = (dvv * dv).astype(BF16)

        @pl.when(i == nsteps - 1)
        def _():
            for g in range(ng):
                dws_ref[g] = jnp.where(tri, dws_ref[g], 0.0)
                dbs_ref[g] = jnp.sum(dbacc[g], axis=-1, keepdims=True)

    vec = pl.BlockSpec((1, D_A), lambda i: (0, 0))
    cube = pl.BlockSpec((ng, CHUNK, CHUNK), lambda i: (0, 0, 0))
    return pl.pallas_call(
        body, name="gmlp_bwd", grid=(nsteps,),
        in_specs=[pl.BlockSpec((CHUNK, 2 * D_A), lambda i: (i, 0)),
                  pl.BlockSpec((CHUNK, D_A), lambda i: (i, 0)), vec, vec, cube, cube, vec],
        out_specs=[pl.BlockSpec((CHUNK, 2 * D_A), lambda i: (i, 0)), cube,
                   pl.BlockSpec((ng, CHUNK, 1), lambda i: (0, 0, 0)), vec, vec, vec],
        out_shape=[jax.ShapeDtypeStruct((t, 2 * D_A), BF16), jax.ShapeDtypeStruct((ng, CHUNK, CHUNK), F32),
                   jax.ShapeDtypeStruct((ng, CHUNK, 1), F32), jax.ShapeDtypeStruct((1, D_A), F32),
                   jax.ShapeDtypeStruct((1, D_A), F32), jax.ShapeDtypeStruct((1, D_A), F32)],
        scratch_shapes=[pltpu.VMEM((ng, CHUNK, CHUNK), F32)],
        compiler_params=_cparams(("arbitrary",)),
    )(proj, dyab, ln_g, ln_b, w_s, b_exp, na_g)


OFF_Z = 2 * D_A
OFF_XS = OFF_Z + D_SSM
OFF_B = OFF_XS + D_SSM
OFF_C = OFF_B + D_BC


def _ssd_consts():
    tri = jnp.tril(jnp.ones((CHUNK, CHUNK), F32)).astype(BF16)
    h = jnp.arange(HPAD)[None, :, None]
    g = jnp.arange(N_SSM_GROUPS)[:, None, None]
    j1 = jnp.arange(GW)[None, None, :]
    eh = (h == g * HEADS_PER_GROUP + j1 // HEAD_DIM).astype(BF16)
    j2 = jnp.arange(HEADS_PER_GROUP * 128)[None, None, :]
    e128 = (h == g * HEADS_PER_GROUP + j2 // 128).astype(BF16)
    return tri, eh, e128


def _ssd_in_specs(cmap):
    def rows(i):
        return cmap(i)

    def prev8(i):
        return jnp.maximum(cmap(i) * (CHUNK // 8) - 1, 0)

    def whole(*shape):
        return pl.BlockSpec(shape, lambda i: (0,) * len(shape))

    bcw = 2 * D_BC
    specs = [
        pl.BlockSpec((CHUNK, D_SSM), lambda i: (rows(i), 0)),
        pl.BlockSpec((CHUNK, D_SSM), lambda i: (rows(i), 0)),
        pl.BlockSpec((8, D_SSM), lambda i: (prev8(i), 0)),
        pl.BlockSpec((CHUNK, bcw), lambda i: (rows(i), 0)),
        pl.BlockSpec((8, bcw), lambda i: (prev8(i), 0)),
        pl.BlockSpec((CHUNK, HPAD), lambda i: (rows(i), 0)),
        whole(SSM_CONV, D_XBC), whole(1, D_XBC),
        whole(1, HPAD), whole(1, HPAD),
        whole(1, D_SSM), whole(1, D_SSM),
        whole(CHUNK, CHUNK),
        whole(N_SSM_GROUPS, HPAD, GW), whole(N_SSM_GROUPS, HPAD, HEADS_PER_GROUP * 128),
    ]
    return specs


N_SSD_IN = 15


def _lanes(ref, start, width):
    return ref.at[:, pl.ds(pl.multiple_of(start, 128), width)]


def _ssd_group_refs(ins, g):
    (z_ref, xs_ref, xsp_ref, bc_ref, bcp_ref, dt_ref, cw_ref, cb_ref, dtb_ref, alog_ref, de_ref, gain_ref,
     tri_ref, eh_ref, e128_ref) = ins
    ox, ob, oc = g * GW, g * D_STATE, D_BC + g * D_STATE
    return (_lanes(z_ref, ox, GW), _lanes(xs_ref, ox, GW), _lanes(xsp_ref, ox, GW),
            _lanes(bc_ref, ob, D_STATE), _lanes(bcp_ref, ob, D_STATE),
            _lanes(bc_ref, oc, D_STATE), _lanes(bcp_ref, oc, D_STATE), dt_ref,
            _lanes(cw_ref, ox, GW), _lanes(cw_ref, D_SSM + ob, D_STATE), _lanes(cw_ref, D_SSM + oc, D_STATE),
            _lanes(cb_ref, ox, GW), _lanes(cb_ref, D_SSM + ob, D_STATE), _lanes(cb_ref, D_SSM + oc, D_STATE),
            dtb_ref, alog_ref, _lanes(de_ref, ox, GW), _lanes(gain_ref, ox, GW), tri_ref,
            eh_ref.at[g], e128_ref.at[g])


N_SSD_OWN = 4


def _ssd_scratch():
    return [pltpu.VMEM((CHUNK + 8, GW), F32), pltpu.VMEM((CHUNK + 8, D_STATE), F32),
            pltpu.VMEM((CHUNK + 8, D_STATE), F32), pltpu.VMEM((CHUNK, GW), F32),
            pltpu.VMEM((HPAD, CHUNK), F32), pltpu.VMEM((CHUNK, HPAD), F32), pltpu.VMEM((CHUNK, HPAD), F32)]


def _ssd_two_sets(alloc):
    a, b = alloc[:7], alloc[7:7 + N_SSD_OWN]
    return ((a[0], a[1], a[2], a[4], a[3], a[5], a[6]), (b[0], b[1], b[2], a[4], b[3], a[5], a[6]))


def _ssd_chunk(ins, scr):
    dt_ref, dtb_ref, alog_ref, tri_ref = ins[5], ins[8], ins[9], ins[12]
    acst_sc, dt_sc, acs_sc = scr[3], scr[5], scr[6]
    dt = _softplus(dt_ref[...] + dtb_ref[...])
    acs = _e01x(tri_ref[...], dt * (-jnp.exp(alog_ref[...])))
    dt_sc[...] = dt
    acs_sc[...] = acs
    acst_sc[...] = acs.T


def _ssd_pre(first, g, refs, scr):
    (z_ref, xs_ref, xsp_ref, b_ref, bp_ref, c_ref, cp_ref, dt_ref, cwx, cwb, cwc, cbx, cbb, cbc,
     dtb_ref, alog_ref, de_ref, gain_ref, tri_ref, eh_ref, e128_ref) = refs
    ext_x, ext_b, ext_c, acst_sc, acse_sc, dt_sc, acs_sc = scr
    p = {}
    px = _conv(xs_ref, xsp_ref, cwx, cbx, ext_x, first, SSM_CONV)
    pb = _conv(b_ref, bp_ref, cwb, cbb, ext_b, first, SSM_CONV)
    pc = _conv(c_ref, cp_ref, cwc, cbc, ext_c, first, SSM_CONV)
    p['xs'], p['dsx'] = _silu_and_grad(px)
    p['bm'], p['dsb'] = _silu_and_grad(pb)
    p['cm'], p['dsc'] = _silu_and_grad(pc)
    p['dt_in'] = dt_ref[...] + dtb_ref[...]
    dt = dt_sc[...]
    p['dt'] = dt
    p['a'] = -jnp.exp(alog_ref[...])
    tri_b = tri_ref[...]
    acs = acs_sc[...]
    eh = eh_ref[...]
    p['eh'] = eh
    p['dt_e'] = _x01(dt, eh)
    acs_e = _x01(acs, eh)
    acse_sc[...] = acs_e
    p['acs_e'] = acs_e
    p['acs_c'] = _x01(acs, e128_ref[...])
    p['acs_last_e'] = acse_sc[pl.ds(CHUNK - 1, 1), :]
    p['xdt'] = p['xs'] * p['dt_e']
    p['decay_e'] = jnp.exp(p['acs_last_e'] - acs_e)
    p['cm_b'] = p['cm'].astype(BF16)
    p['bm_b'] = p['bm'].astype(BF16)
    p['scores'] = _dot_nt(p['cm_b'], p['bm_b'])
    p['tri_b'] = tri_b
    return p


def _ssd_l(p, g, r, acst_sc, tri):
    col = p['acs_c'][:, r * 128:(r + 1) * 128]
    row = acst_sc[pl.ds(g * HEADS_PER_GROUP + r, 1), :]
    return jnp.exp(jnp.where(tri, col - row, -1e30))


def _ssd_fwd(pz, pxs, pbc, dt_raw, cw, cb, dtb, alog, de, gain, consts, hosted):
    t = pz.shape[0]
    nc = t // CHUNK
    ng = N_SSM_GROUPS
    tri_c, eh_c, e128_c = consts
    n_in = N_SSD_IN
    nh_in, nh_out = len(hosted['arrays']), len(hosted['out_shape'])
    mid_chunk = (7 * nc) // 8

    def body(*refs):
        ins = refs[:n_in]
        hins = refs[n_in:n_in + nh_in]
        o0 = n_in + nh_in
        yb_all, ys_all, hs_all = refs[o0:o0 + 3]
        houts = refs[o0 + 3:o0 + 3 + nh_out]
        h_sc = refs[o0 + 3 + nh_out]
        scr_a, scr_b = _ssd_two_sets(refs[o0 + 4 + nh_out:-2])
        sems = refs[-2:]
        c = pl.program_id(0)
        _host_phase(hosted, 'start', c == 0, hins, houts, sems)
        _host_phase(hosted, 'mid', c == mid_chunk, hins, houts, sems)

        @pl.when(c == 0)
        def _():
            h_sc[...] = jnp.zeros_like(h_sc)

        def group(g, scr):
            grefs = _ssd_group_refs(ins, g)
            z_ref, de_ref, gain_ref = grefs[0], grefs[16], grefs[17]
            yb_ref, ys_ref = _lanes(yb_all, g * GW, GW), _lanes(ys_all, g * GW, GW)
            slab = pl.ds(pl.multiple_of(g * D_STATE, D_STATE), D_STATE)
            p = _ssd_pre(c == 0, g, grefs, scr)
            tri = _tri_mask()
            h_in = h_sc[slab, :]
            hs_all[slab, :] = h_in
            yoff = _dot(p['cm_b'], h_in.astype(BF16)) * jnp.exp(p['acs_e'])
            states = _dot_tn(p['bm_b'], (p['xdt'] * p['decay_e']).astype(BF16))
            lane = lax.broadcasted_iota(jnp.int32, (CHUNK, 128), 1)
            slabs = []
            for r2 in range(HEADS_PER_GROUP // 2):
                xb = p['xdt'][:, r2 * 128:(r2 + 1) * 128].astype(BF16)
                ya = _dot((p['scores'] * _ssd_l(p, g, 2 * r2, scr[3], tri)).astype(BF16), xb)
                yb = _dot((p['scores'] * _ssd_l(p, g, 2 * r2 + 1, scr[3], tri)).astype(BF16), xb)
                slabs.append(jnp.where(lane < HEAD_DIM, ya, yb))
            y = jnp.concatenate(slabs, axis=1) + yoff + de_ref[...] * p['xs']
            ys_ref[...] = y
            h_sc[slab, :] = jnp.exp(p['acs_last_e']) * h_in + states
            zv = z_ref[...]
            yg = y * zv * _sigmoid(zv)
            r = lax.rsqrt(jnp.mean(yg * yg, axis=-1, keepdims=True) + EPS)
            yb_ref[...] = (yg * r * gain_ref[...]).astype(BF16)

        def pair(j, carry):
            group(2 * j, scr_a)
            group(2 * j + 1, scr_b)
            return carry

        _ssd_chunk(ins, scr_a)
        lax.fori_loop(0, ng // 2, pair, 0)
        _host_phase(hosted, 'finish', c == nc - 1, hins, houts, sems)

    h_in, h_out, h_shape, h_scratch, h_alias = _host_plumbing(hosted, n_in, 3)
    in_specs = _ssd_in_specs(lambda i: i) + h_in
    out_specs = [pl.BlockSpec((CHUNK, D_SSM), lambda i: (i, 0)),
                 pl.BlockSpec((CHUNK, D_SSM), lambda i: (i, 0)),
                 pl.BlockSpec((None, ng * D_STATE, GW), lambda i: (i, 0, 0))] + h_out
    out_shape = [jax.ShapeDtypeStruct((t, D_SSM), BF16), jax.ShapeDtypeStruct((t, D_SSM), F32),
                 jax.ShapeDtypeStruct((nc, ng * D_STATE, GW), F32)] + h_shape
    return pl.pallas_call(
        body, name="ssd_fwd", grid=(nc,), in_specs=in_specs, out_specs=out_specs, out_shape=out_shape,
        scratch_shapes=[pltpu.VMEM((ng * D_STATE, GW), F32)] + _ssd_scratch() + _ssd_scratch()[:N_SSD_OWN]
        + h_scratch,
        input_output_aliases=h_alias,
        compiler_params=_cparams(("arbitrary",)),
    )(pz, pxs, pxs, pbc, pbc, dt_raw, cw, cb, dtb, alog, de, gain,
      tri_c, eh_c, e128_c, *hosted['arrays'])


def _rows8(vals, width):
    rid = lax.broadcasted_iota(jnp.int32, (8, width), 0)
    out = jnp.zeros((8, width), F32)
    for k, v in enumerate(vals):
        if v is not None:
            out = out + jnp.where(rid == k, v, 0.0)
    return out


def _ssd_bwd(pz, pxs, pbc, dt_raw, dyb, ysave, hs, duv, cw, cb, dtb, alog, de, gain, consts, hosted):
    t = pz.shape[0]
    nc = t // CHUNK
    ng = N_SSM_GROUPS
    tri_c, eh_c, e128_c = consts
    n_in = N_SSD_IN
    nh_in, nh_out = len(hosted['arrays']), len(hosted['out_shape'])

    def per_group(g, c, ins, dy_ref, ys_ref, hs_ref, outs, scratch):
        dz_ref, dxs_ref, db_ref, dc_ref, ddt_ref, acc_x, acc_b, acc_c, acc_gain, acc_head = outs
        dh_sc, car_x, car_b, car_c, dext_x, dext_b, dext_c = scratch[:7]
        scr = scratch[7:]
        ext_x, ext_b, ext_c, acst_sc = scr[:4]
        z_ref, cwx, cwb, cwc, de_ref, gain_ref = ins[0], ins[8], ins[9], ins[10], ins[16], ins[17]
        slab = pl.ds(pl.multiple_of(g * D_STATE, D_STATE), D_STATE)
        p = _ssd_pre(c == 0, g, ins, scr)
        tri = _tri_mask()
        xs, dt_e, acs_e, xdt, decay_e = p['xs'], p['dt_e'], p['acs_e'], p['xdt'], p['decay_e']
        cm_b, bm_b, scores, eh = p['cm_b'], p['bm_b'], p['scores'], p['eh']
        h_in = hs_ref[...]
        h_in_b = h_in.astype(BF16)
        e_a = jnp.exp(acs_e)
        raw = _dot(cm_b, h_in_b)

        y = ys_ref[...]
        zv = z_ref[...]
        sz, dsz = _silu_and_grad(zv)
        yg = y * sz
        r = lax.rsqrt(jnp.mean(yg * yg, axis=-1, keepdims=True) + EPS)
        xh = yg * r
        dout = dy_ref[...]
        gain = gain_ref[...]
        dxh = dout * gain
        dyg = r * (dxh - xh * jnp.mean(dxh * xh, axis=-1, keepdims=True))
        dy = dyg * sz
        dz_ref[...] = (dyg * y * dsz).astype(BF16)
        acc_gain[g] += _rows8([jnp.sum(dout * xh, axis=0, keepdims=True)], GW)
        d_skip8 = _x01_nt(_rows8([None, None, jnp.sum(dy * xs, axis=0, keepdims=True)], GW), eh)
        dxs = de_ref[...] * dy

        q = dy * raw * e_a
        draw = (dy * e_a).astype(BF16)
        d_c = _dot_nt(draw, h_in_b)
        dh_in = _dot_tn(cm_b, draw)

        lane = lax.broadcasted_iota(jnp.int32, (CHUNK, 128), 1)
        ones_b = jnp.ones((CHUNK, 128), BF16)
        dscores = jnp.zeros((CHUNK, CHUNK), F32)
        dxdt_slabs, q_slabs = [], []
        for r2 in range(HEADS_PER_GROUP // 2):
            sl = slice(r2 * 128, (r2 + 1) * 128)
            xb = xdt[:, sl].astype(BF16)
            dys = dy[:, sl]
            dys_b = dys.astype(BF16)
            dxh_pair, qv_pair = [], []
            for half in range(2):
                lmat = _ssd_l(p, g, 2 * r2 + half, acst_sc, tri)
                m = scores * lmat
                mine = (lane < HEAD_DIM) if half == 0 else (lane >= HEAD_DIM)
                dm = _dot_nt(jnp.where(mine, dys, 0.0).astype(BF16), xb)
                dscores = dscores + dm * lmat
                gm = dm * m
                dxh_pair.append(_dot_tn(m.astype(BF16), dys_b))
                h3 = _split3(gm)
                colsum = _dot_tn(h3[0], ones_b) + _dot_tn(h3[1], ones_b) + _dot_tn(h3[2], ones_b)
                qv_pair.append(jnp.sum(gm, axis=-1, keepdims=True) - colsum)
            dxdt_slabs.append(jnp.where(lane < HEAD_DIM, dxh_pair[0], dxh_pair[1]))
            q_slabs.append(jnp.where(lane == 0, qv_pair[0], 0.0) + jnp.where(lane == HEAD_DIM, qv_pair[1], 0.0))
        dxdt = jnp.concatenate(dxdt_slabs, axis=1)
        q = q + jnp.concatenate(q_slabs, axis=1)

        dh_out = dh_sc[slab, :]
        dh_out_b = dh_out.astype(BF16)
        e_l = jnp.exp(p['acs_last_e'])
        dh_sc[slab, :] = dh_in + e_l * dh_out
        dlast = jnp.sum(dh_out * h_in, axis=0, keepdims=True) * e_l
        dxd = _dot(bm_b, dh_out_b)
        xd = xdt * decay_e
        dxdt = dxdt + dxd * decay_e
        dd = dxd * xd
        q = q - dd
        dlast = dlast + jnp.sum(dd, axis=0, keepdims=True)
        d_b = _dot_nt(xd.astype(BF16), dh_out_b)
        dsc_b = dscores.astype(BF16)
        d_c = d_c + _dot(dsc_b, bm_b)
        d_b = d_b + _dot_tn(dsc_b, cm_b)

        dxs = dxs + dxdt * dt_e
        rid = lax.broadcasted_iota(jnp.int32, (CHUNK, GW), 0)
        q = q + jnp.where(rid == CHUNK - 1, dlast, 0.0)
        dacs = _x01_nt(q, eh)
        ddt = _x01_nt(dxdt * xs, eh)
        dadt = _e01x_tn(p['tri_b'], dacs)
        ddt = ddt + dadt * p['a']
        d_a = jnp.sum(dadt * p['dt'], axis=0, keepdims=True)
        ddt_raw = ddt * _sigmoid(p['dt_in'])
        acc_head[...] += _rows8([jnp.sum(ddt_raw, axis=0, keepdims=True), d_a * p['a']], HPAD) + d_skip8

        @pl.when(g == 0)
        def _():
            ddt_ref[...] = ddt_raw

        @pl.when(g > 0)
        def _():
            ddt_ref[...] += ddt_raw

        for dv, dsil, ext, dext, car, acc, w_ref, o_ref in (
                (dxs, p['dsx'], ext_x, dext_x, car_x, acc_x, cwx, dxs_ref),
                (d_b, p['dsb'], ext_b, dext_b, car_b, acc_b, cwb, db_ref),
                (d_c, p['dsc'], ext_c, dext_c, car_c, acc_c, cwc, dc_ref)):
            dp = dv * dsil
            width = dp.shape[1]
            rows = [jnp.sum(ext[pl.ds(5 + k, CHUNK), :] * dp, axis=0, keepdims=True) for k in range(SSM_CONV)]
            rows.append(jnp.sum(dp, axis=0, keepdims=True))
            acc[g] += _rows8(rows, width)
            dext[0:CHUNK, :] = dp
            dext[CHUNK:CHUNK + 8, :] = car[g]
            car[g] = dext[0:8, :]
            dx = w_ref[SSM_CONV - 1:SSM_CONV, :] * dext[pl.ds(0, CHUNK), :]
            for k in range(SSM_CONV - 1):
                dx = dx + w_ref[k:k + 1, :] * dext[pl.ds(SSM_CONV - 1 - k, CHUNK), :]
            o_ref[...] = dx.astype(BF16)

    def body(*refs):
        ins = refs[:n_in]
        dy_all, ys_all, hs_all, duv_ref = refs[n_in:n_in + 4]
        hins = refs[n_in + 4:n_in + 4 + nh_in]
        o0 = n_in + 4 + nh_in
        dproj_ref, ddt_ref = refs[o0:o0 + 2]
        accs = refs[o0 + 2:o0 + 7]
        houts = refs[o0 + 7:o0 + 7 + nh_out]
        scratch = refs[o0 + 7 + nh_out:-2]
        sems = refs[-2:]
        cc = pl.program_id(0)
        _host_phase(hosted, 'start', cc == 0, hins, houts, sems)
        dproj_ref[:, 0:2 * D_A] = duv_ref[...]

        @pl.when(cc == 0)
        def _():
            for a in tuple(accs) + tuple(scratch[:4]):
                a[...] = jnp.zeros_like(a)

        shared = tuple(scratch[:4])
        scr_a, scr_b = _ssd_two_sets(tuple(scratch[7:14]) + tuple(scratch[17:17 + N_SSD_OWN]))
        set_a = shared + tuple(scratch[4:7]) + scr_a
        set_b = shared + tuple(scratch[14:17]) + scr_b

        def group(g, own):
            slab = pl.ds(pl.multiple_of(g * D_STATE, D_STATE), D_STATE)
            outs = (_lanes(dproj_ref, OFF_Z + g * GW, GW), _lanes(dproj_ref, OFF_XS + g * GW, GW),
                    _lanes(dproj_ref, OFF_B + g * D_STATE, D_STATE), _lanes(dproj_ref, OFF_C + g * D_STATE, D_STATE),
                    ddt_ref) + tuple(accs)
            per_group(g, nc - 1 - cc, _ssd_group_refs(ins, g), _lanes(dy_all, g * GW, GW),
                      _lanes(ys_all, g * GW, GW), hs_all.at[slab, :], outs, own)

        def pair(j, carry):
            group(2 * j, set_a)
            group(2 * j + 1, set_b)
            return carry

        _ssd_chunk(ins, scr_a)
        lax.fori_loop(0, ng // 2, pair, 0)
        _host_phase(hosted, 'finish', cc == nc - 1, hins, houts, sems)

    def cmap(i):
        return nc - 1 - i

    in_specs = _ssd_in_specs(cmap) + [
        pl.BlockSpec((CHUNK, D_SSM), lambda i: (cmap(i), 0)),
        pl.BlockSpec((CHUNK, D_SSM), lambda i: (cmap(i), 0)),
        pl.BlockSpec((None, ng * D_STATE, GW), lambda i: (cmap(i), 0, 0)),
        pl.BlockSpec((CHUNK, 2 * D_A), lambda i: (cmap(i), 0)),
    ]

    def full(shape):
        return pl.BlockSpec(shape, lambda i: (0,) * len(shape))

    out_specs = [
        pl.BlockSpec((CHUNK, D_MAIN), lambda i: (cmap(i), 0)),
        pl.BlockSpec((CHUNK, HPAD), lambda i: (cmap(i), 0)),
        full((ng, 8, GW)), full((ng, 8, D_STATE)), full((ng, 8, D_STATE)),
        full((ng, 8, GW)), full((8, HPAD)),
    ]
    out_shape = [
        jax.ShapeDtypeStruct((t, D_MAIN), BF16),
        jax.ShapeDtypeStruct((t, HPAD), F32),
        jax.ShapeDtypeStruct((ng, 8, GW), F32), jax.ShapeDtypeStruct((ng, 8, D_STATE), F32),
        jax.ShapeDtypeStruct((ng, 8, D_STATE), F32), jax.ShapeDtypeStruct((ng, 8, GW), F32),
        jax.ShapeDtypeStruct((8, HPAD), F32),
    ]
    scratch = [pltpu.VMEM((ng * D_STATE, GW), F32),
               pltpu.VMEM((ng, 8, GW), F32), pltpu.VMEM((ng, 8, D_STATE), F32), pltpu.VMEM((ng, 8, D_STATE), F32),
               pltpu.VMEM((CHUNK + 8, GW), F32), pltpu.VMEM((CHUNK + 8, D_STATE), F32),
               pltpu.VMEM((CHUNK + 8, D_STATE), F32)] + _ssd_scratch()
    scratch += scratch[4:7] + _ssd_scratch()[:N_SSD_OWN]
    h_in, h_out, h_shape, h_scratch, h_alias = _host_plumbing(hosted, n_in + 4, len(out_shape))
    return pl.pallas_call(
        body, name="ssd_bwd", grid=(nc,), in_specs=in_specs + h_in, out_specs=out_specs + h_out,
        out_shape=out_shape + h_shape, scratch_shapes=scratch + h_scratch, input_output_aliases=h_alias,
        compiler_params=_cparams(("arbitrary",)),
    )(pz, pxs, pxs, pbc, pbc, dt_raw, cw, cb, dtb, alog, de, gain,
      tri_c, eh_c, e128_c, dyb, ysave, hs, duv, *hosted['arrays'])


ANY = pl.BlockSpec(memory_space=pl.ANY)


def _place():
    x, y, c = lax.axis_index("x"), lax.axis_index("y"), lax.axis_index("c")
    chips = [(1 - x, y), (x, 1 - y), (1 - x, 1 - y)]
    return x, y, c, chips


def _rcopy(src, dst, send_sems, recv_sems, k, dev):
    return pltpu.make_async_remote_copy(src_ref=src, dst_ref=dst, send_sem=send_sems.at[k],
                                        recv_sem=recv_sems.at[k], device_id=dev, device_id_type=MESH)


def _my_chip():
    return 2 * lax.axis_index("x") + lax.axis_index("y")


def _cast_into_slot(w, name):
    r, c = w.shape
    tr = _row_tile(r, c, 2)

    def body(w_ref, o_ref):
        o_ref[...] = w_ref[...].astype(BF16)

    return pl.pallas_call(
        body, name=name, grid=(r // tr,), in_specs=[pl.BlockSpec((tr, c), lambda i: (i, 0))],
        out_specs=pl.BlockSpec((None, tr, c), lambda i: (_my_chip(), i, 0)),
        out_shape=jax.ShapeDtypeStruct((N_CHIPS, r, c), BF16), compiler_params=_cparams(("parallel",)),
    )(w)


def _hosted_gather(bigs):
    nb = len(bigs)

    def rows(a, c):
        half = bigs[a].shape[1] // 2
        return pl.ds(c * half, half)

    def start(ins, outs, send_sems, recv_sems):
        x, y, c, chips = _place()
        q = 2 * x + y
        for a in range(nb):
            for k, chip in enumerate(chips):
                _rcopy(outs[a].at[q, rows(a, c)], outs[a].at[q, rows(a, c)], send_sems, recv_sems, 6 * a + k,
                       (chip[0], chip[1], c)).start()

    def mid(ins, outs, send_sems, recv_sems):
        x, y, c, chips = _place()
        sib = (x, y, 1 - c)
        for a in range(nb):
            for k, chip in enumerate(chips):
                slab = outs[a].at[2 * chip[0] + chip[1], rows(a, c)]
                _rcopy(slab, slab, send_sems, recv_sems, 6 * a + k, sib).wait_recv()
                _rcopy(slab, slab, send_sems, recv_sems, 6 * a + 3 + k, sib).start()

    def finish(ins, outs, send_sems, recv_sems):
        x, y, c, chips = _place()
        q = 2 * x + y
        sib = (x, y, 1 - c)
        for a in range(nb):
            for k, chip in enumerate(chips):
                qk = 2 * chip[0] + chip[1]
                other = outs[a].at[qk, rows(a, 1 - c)]
                _rcopy(other, other, send_sems, recv_sems, 6 * a + 3 + k, sib).wait_recv()
                mine = outs[a].at[q, rows(a, c)]
                _rcopy(mine, mine, send_sems, recv_sems, 6 * a + k, sib).wait_send()
                fwd = outs[a].at[qk, rows(a, c)]
                _rcopy(fwd, fwd, send_sems, recv_sems, 6 * a + 3 + k, sib).wait_send()

    return dict(arrays=list(bigs), out_shape=[jax.ShapeDtypeStruct(b.shape, b.dtype) for b in bigs],
                aliases={a: a for a in range(nb)}, nsem=6 * nb, start=start, mid=mid, finish=finish)


def _hosted_rs_chips(ps):
    na = len(ps)

    def copies(ins, outs, send_sems, recv_sems):
        x, y, c, chips = _place()
        return [_rcopy(ins[a].at[2 * chip[0] + chip[1]], outs[a].at[k], send_sems, recv_sems, 3 * a + k,
                       (chip[0], chip[1], c)) for a in range(na) for k, chip in enumerate(chips)]

    def start(ins, outs, send_sems, recv_sems):
        for cp in copies(ins, outs, send_sems, recv_sems):
            cp.start()

    def finish(ins, outs, send_sems, recv_sems):
        for cp in copies(ins, outs, send_sems, recv_sems):
            cp.wait()

    return dict(arrays=list(ps), out_shape=[jax.ShapeDtypeStruct((3,) + p.shape[1:], p.dtype) for p in ps],
                aliases={}, nsem=3 * na, start=start, mid=None, finish=finish)


def _hosted_rs_sibling(gs):
    na = len(gs)

    def copies(ins, outs, send_sems, recv_sems):
        x, y, c, _ = _place()
        halves = [g.shape[1] // 2 for g in gs]
        return [_rcopy(ins[a].at[:, pl.ds((1 - c) * halves[a], halves[a]), :], outs[a], send_sems, recv_sems, a,
                       (x, y, 1 - c)) for a in range(na)]

    def start(ins, outs, send_sems, recv_sems):
        for cp in copies(ins, outs, send_sems, recv_sems):
            cp.start()

    def finish(ins, outs, send_sems, recv_sems):
        for cp in copies(ins, outs, send_sems, recv_sems):
            cp.wait()

    return dict(arrays=list(gs), aliases={}, nsem=na, start=start, mid=None, finish=finish,
                out_shape=[jax.ShapeDtypeStruct((N_CHIPS, g.shape[1] // 2, g.shape[2]), g.dtype) for g in gs])


def _hosted_share_sibling(fs):
    na = len(fs)

    def rows(a, c):
        half = fs[a].shape[0] // 2
        return pl.ds(c * half, half)

    def start(ins, outs, send_sems, recv_sems):
        x, y, c, _ = _place()
        for a in range(na):
            _rcopy(outs[a].at[rows(a, c)], outs[a].at[rows(a, c)], send_sems, recv_sems, a, (x, y, 1 - c)).start()

    def finish(ins, outs, send_sems, recv_sems):
        x, y, c, _ = _place()
        for a in range(na):
            _rcopy(outs[a].at[rows(a, c)], outs[a].at[rows(a, c)], send_sems, recv_sems, a, (x, y, 1 - c)).wait_send()
            other = outs[a].at[rows(a, 1 - c)]
            _rcopy(other, other, send_sems, recv_sems, a, (x, y, 1 - c)).wait_recv()

    return dict(arrays=list(fs), out_shape=[jax.ShapeDtypeStruct(f.shape, f.dtype) for f in fs],
                aliases={a: a for a in range(na)}, nsem=na, start=start, mid=None, finish=finish)


def _hosted_allgather(buf):
    def copies(ins, outs, send_sems, recv_sems):
        x, y, c, _ = _place()
        me = 4 * x + 2 * y + c
        cps = []
        for k in range(1, 8):
            dev = (1 - x if k & 4 else x, 1 - y if k & 2 else y, 1 - c if k & 1 else c)
            cps.append(_rcopy(ins[0], outs[0].at[me], send_sems, recv_sems, k - 1, dev))
        return pltpu.make_async_copy(ins[0], outs[0].at[me], send_sems.at[7]), cps

    def start(ins, outs, send_sems, recv_sems):
        loc, cps = copies(ins, outs, send_sems, recv_sems)
        loc.start()
        for cp in cps:
            cp.start()

    def finish(ins, outs, send_sems, recv_sems):
        loc, cps = copies(ins, outs, send_sems, recv_sems)
        loc.wait()
        for cp in cps:
            cp.wait()

    return dict(arrays=[buf], out_shape=[jax.ShapeDtypeStruct((8,) + buf.shape, buf.dtype)], aliases={},
                nsem=8, start=start, mid=None, finish=finish)


class _SemWindow:
    def __init__(self, sems, off):
        self._sems, self._off = sems, off

    @property
    def at(self):
        return self

    def __getitem__(self, k):
        return self._sems.at[k + self._off]


def _hosted_join(parts):
    arrays, out_shape, aliases, spans, nsem = [], [], {}, [], 0
    for h in parts:
        spans.append((len(arrays), len(h['arrays']), len(out_shape), len(h['out_shape']), nsem))
        aliases.update({len(arrays) + a: len(out_shape) + b for a, b in h['aliases'].items()})
        arrays += h['arrays']
        out_shape += h['out_shape']
        nsem += h['nsem']

    def phase(name):
        if all(h[name] is None for h in parts):
            return None

        def run(ins, outs, send_sems, recv_sems):
            for h, (ia, na, io, no, s0) in zip(parts, spans):
                if h[name] is not None:
                    h[name](ins[ia:ia + na], outs[io:io + no], _SemWindow(send_sems, s0), _SemWindow(recv_sems, s0))

        return run

    return dict(arrays=arrays, out_shape=out_shape, aliases=aliases, nsem=nsem,
                start=phase('start'), mid=phase('mid'), finish=phase('finish'))


def _run_hosted(hosted, name):
    nh_in, nh_out = len(hosted['arrays']), len(hosted['out_shape'])

    def body(*refs):
        ins, outs, sems = refs[:nh_in], refs[nh_in:nh_in + nh_out], refs[-2:]
        for ph in ('start', 'mid', 'finish'):
            if hosted[ph] is not None:
                hosted[ph](ins, outs, sems[0], sems[1])

    h_in, h_out, h_shape, h_scratch, h_alias = _host_plumbing(hosted, 0, 0)
    return pl.pallas_call(body, name=name, in_specs=h_in, out_specs=h_out, out_shape=h_shape,
                          scratch_shapes=h_scratch, input_output_aliases=h_alias)(*hosted['arrays'])


def _host_plumbing(hosted, n_in, n_out):
    nh = len(hosted['arrays'])
    return ([ANY] * nh, [ANY] * len(hosted['out_shape']), list(hosted['out_shape']),
            [pltpu.SemaphoreType.DMA((hosted['nsem'],)), pltpu.SemaphoreType.DMA((hosted['nsem'],))],
            {n_in + a: n_out + b for a, b in hosted['aliases'].items()})


def _host_phase(hosted, phase, when, hins, houts, sems):
    fn = hosted[phase]
    if fn is None:
        return

    @pl.when(when)
    def _():
        fn(hins, houts, sems[0], sems[1])


def _gather_weights(bigs, smalls):
    nb, ns = len(bigs), len(smalls)
    na = nb + ns
    nsem = 6 * nb + 3 * ns
    big = _hosted_gather(bigs)

    def body(*refs):
        ins, outs = refs[:na], refs[na:2 * na]
        send_sems, recv_sems, loc_sems = refs[2 * na:]
        x, y, c, chips = _place()
        q = 2 * x + y
        sib = (x, y, 1 - c)
        locs, sends = [], []
        for s in range(ns):
            cp = pltpu.make_async_copy(ins[nb + s], outs[nb + s].at[q], loc_sems.at[s])
            cp.start()
            locs.append(cp)
        big['start'](ins[:nb], outs[:nb], send_sems, recv_sems)
        for s in range(ns):
            a = nb + s
            for k, chip in enumerate(chips):
                cp = _rcopy(ins[a], outs[a].at[q], send_sems, recv_sems, 6 * nb + 3 * s + k,
                            (chip[0], chip[1], c))
                cp.start()
                sends.append(cp)
        big['mid'](ins[:nb], outs[:nb], send_sems, recv_sems)
        big['finish'](ins[:nb], outs[:nb], send_sems, recv_sems)
        for s in range(ns):
            a = nb + s
            for k, chip in enumerate(chips):
                qk = 2 * chip[0] + chip[1]
                _rcopy(ins[a], outs[a].at[qk], send_sems, recv_sems, 6 * nb + 3 * s + k, sib).wait_recv()
        for cp in sends:
            cp.wait_send()
        for cp in locs:
            cp.wait()

    arrs = list(bigs) + list(smalls)
    out_shape = ([jax.ShapeDtypeStruct(a.shape, a.dtype) for a in bigs]
                 + [jax.ShapeDtypeStruct((N_CHIPS,) + a.shape, a.dtype) for a in smalls])
    return pl.pallas_call(
        body, name="gather_weights", in_specs=[ANY] * na, out_specs=[ANY] * na, out_shape=out_shape,
        input_output_aliases={a: a for a in range(nb)},
        scratch_shapes=[pltpu.SemaphoreType.DMA((nsem,)), pltpu.SemaphoreType.DMA((nsem,)),
                        pltpu.SemaphoreType.DMA((max(ns, 1),))],
    )(*arrs)


def _row_tile(rows, cols, nbuf):
    budget = 24 * 1024 * 1024 // (nbuf * cols * 4 * 2)
    return _tile(rows, max(16, budget - budget % 16), 16) if rows % 16 == 0 else rows


def _add_pairs(g, rcv, name):
    s, half, c = rcv.shape
    tr = _row_tile(half, c, 3)
    nh = half // tr

    def body(a_ref, b_ref, o_ref):
        o_ref[...] = (a_ref[...].astype(F32) + b_ref[...].astype(F32)).astype(o_ref.dtype)

    blk = pl.BlockSpec((None, tr, c), lambda j, i: (j, i, 0))
    mine = pl.BlockSpec((None, tr, c), lambda j, i: (j, lax.axis_index("c") * nh + i, 0))
    return pl.pallas_call(
        body, name=name, grid=(s, nh), in_specs=[mine, blk], out_specs=blk,
        out_shape=jax.ShapeDtypeStruct(rcv.shape, rcv.dtype), compiler_params=_cparams(("parallel", "parallel")),
    )(g, rcv)


def _sum_chips(part, rcv, name):
    _, half, c = part.shape
    tr = _row_tile(half, c, 5)
    nh = half // tr

    def body(o_ref, r_ref, out_ref):
        acc = o_ref[...].astype(F32)
        for k in range(3):
            acc = acc + r_ref[k].astype(F32)
        out_ref[...] = acc

    return pl.pallas_call(
        body, name=name, grid=(nh,),
        in_specs=[pl.BlockSpec((None, tr, c), lambda i: (_my_chip(), i, 0)),
                  pl.BlockSpec((3, tr, c), lambda i: (0, i, 0))],
        out_specs=pl.BlockSpec((tr, c), lambda i: (lax.axis_index("c") * nh + i, 0)),
        out_shape=jax.ShapeDtypeStruct((2 * half, c), F32), compiler_params=_cparams(("parallel",)),
    )(part, rcv)


def _sum_devices(parts, name):
    _, n, _ = parts.shape
    tr = n if n <= 4096 else _tile(n, 512, 8)

    def body(p_ref, o_ref):
        acc = p_ref[0]
        for k in range(1, 8):
            acc = acc + p_ref[k]
        o_ref[...] = acc

    return pl.pallas_call(
        body, name=name, grid=(n // tr,),
        in_specs=[pl.BlockSpec((8, tr, 128), lambda i: (0, i, 0))],
        out_specs=pl.BlockSpec((tr, 128), lambda i: (i, 0)),
        out_shape=jax.ShapeDtypeStruct((n, 128), F32), compiler_params=_cparams(("parallel",)),
    )(parts)


def _adamw(w, g, m, v, name):
    r, c = w.shape
    tr = _row_tile(r, c, 7)
    c1 = 1.0 - ADAM_B1 ** ADAM_STEP
    c2 = 1.0 - ADAM_B2 ** ADAM_STEP

    def body(w_ref, g_ref, m_ref, v_ref, d_ref, mo_ref, vo_ref):
        gv = g_ref[...]
        mn = ADAM_B1 * m_ref[...] + (1.0 - ADAM_B1) * gv
        vn = ADAM_B2 * v_ref[...] + (1.0 - ADAM_B2) * (gv * gv)
        mo_ref[...] = mn
        vo_ref[...] = vn
        m_hat = mn / c1
        v_hat = vn / c2
        d_ref[...] = -ADAM_LR * (m_hat / (jnp.sqrt(v_hat) + ADAM_EPS) + ADAM_WD * w_ref[...])

    blk = pl.BlockSpec((tr, c), lambda i: (i, 0))
    sh = jax.ShapeDtypeStruct((r, c), F32)
    return pl.pallas_call(
        body, name=name, grid=(r // tr,), in_specs=[blk] * 4, out_specs=[blk] * 3, out_shape=[sh] * 3,
        compiler_params=_cparams(("parallel",)),
    )(w, g, m, v)


WEIGHTS = ['norm_mix_g', 'w_in', 'ln_a_g', 'ln_a_b', 'w_s', 'b_s', 'norm_a_g', 'conv_ssm_w', 'conv_ssm_b',
           'dt_bias', 'a_log', 'd_skip', 'ssm_norm_g', 'w_out', 'norm_ffn_g', 'w_up', 'conv_ffn_w',
           'conv_ffn_b', 'w_down', 'norm_ple_g', 'w_ple_gate', 'w_ple', 'norm_final_g']
BIG = ['w_in', 'w_out', 'w_up', 'w_down', 'w_ple_gate', 'w_ple']
SMALL = [n for n in WEIGHTS if n not in BIG]
PACK_ALIGN = 2048


def _pack(arrs):
    parts = []
    for a in arrs:
        f = a.reshape(-1).astype(F32)
        parts.append(jnp.pad(f, (0, (-f.shape[0]) % PACK_ALIGN)))
    return jnp.concatenate(parts).reshape(-1, 128)


def _unpack(buf, shapes):
    flat = buf.reshape(-1)
    out, off = [], 0
    for s in shapes:
        n = math.prod(s)
        out.append(flat[off:off + n].reshape(s))
        off += n + (-n) % PACK_ALIGN
    return out


def _pad_heads(v):
    return jnp.pad(v, ((0, 0), (0, HPAD - v.shape[1])))


def _col_sharded(full):
    r, c4 = full.shape
    return jnp.transpose(full.reshape(r, N_CHIPS, c4 // N_CHIPS), (1, 0, 2))


def _from_col_sharded(g):
    s, r, c = g.shape
    return jnp.transpose(g, (1, 0, 2)).reshape(r, s * c)


def kernel(x, p, norm_mix_g, w_in, ln_a_g, ln_a_b, w_s, b_s, norm_a_g, conv_ssm_w, conv_ssm_b, dt_bias, a_log, d_skip, ssm_norm_g, w_out, norm_ffn_g, w_up, conv_ffn_w, conv_ffn_b, w_down, norm_ple_g, w_ple_gate, w_ple, norm_final_g, loss_target, m_norm_mix_g, m_w_in, m_ln_a_g, m_ln_a_b, m_w_s, m_b_s, m_norm_a_g, m_conv_ssm_w, m_conv_ssm_b, m_dt_bias, m_a_log, m_d_skip, m_ssm_norm_g, m_w_out, m_norm_ffn_g, m_w_up, m_conv_ffn_w, m_conv_ffn_b, m_w_down, m_norm_ple_g, m_w_ple_gate, m_w_ple, m_norm_final_g, v_norm_mix_g, v_w_in, v_ln_a_g, v_ln_a_b, v_w_s, v_b_s, v_norm_a_g, v_conv_ssm_w, v_conv_ssm_b, v_dt_bias, v_a_log, v_d_skip, v_ssm_norm_g, v_w_out, v_norm_ffn_g, v_w_up, v_conv_ffn_w, v_conv_ffn_b, v_w_down, v_norm_ple_g, v_w_ple_gate, v_w_ple, v_norm_final_g):
    given = dict(locals())
    wts = {n: given[n] for n in WEIGHTS}
    mom = {n: given['m_' + n] for n in WEIGHTS}
    var = {n: given['v_' + n] for n in WEIGHTS}
    d = D_MODEL
    xt, pt, tgt = x[0], p[0, 0], loss_target[0]
    chip = 2 * lax.axis_index("x") + lax.axis_index("y")

    slots = {n: _cast_into_slot(wts[n][0], "cast_" + n) for n in BIG}
    g_in, g_cs, g_cf = _gather_weights([slots['w_in']], [conv_ssm_w[0], conv_ffn_w[0]])
    later = [n for n in BIG if n != 'w_in']
    w_in_full = _from_col_sharded(g_in)
    w_main = w_in_full[:, :D_MAIN]
    w_dt = _pad_heads(w_in_full[:, D_MAIN:])
    cs_w = _from_col_sharded(g_cs)
    cf_w = _from_col_sharded(g_cf)
    consts = _ssd_consts()
    dtb, alog = _pad_heads(dt_bias), _pad_heads(a_log)
    de = jnp.repeat(d_skip[0], HEAD_DIM)[None, :]
    b_exp = jnp.broadcast_to(b_s[0][:, :, None], (N_GROUPS_A, CHUNK, CHUNK))

    a1 = _rms_fwd(xt, norm_mix_g, "rms_mix")
    p_uv = _matmul(a1, w_main[:, :OFF_Z], mode='nn', name="mm_proj_uv", tm=1024, tn=1024)
    p_z = _matmul(a1, w_main[:, OFF_Z:OFF_XS], mode='nn', name="mm_proj_z", tm=1024, tn=1024)
    p_xs = _matmul(a1, w_main[:, OFF_XS:OFF_B], mode='nn', name="mm_proj_xs", tm=1024, tn=1024)
    p_bc = _matmul(a1, w_main[:, OFF_B:], mode='nn', name="mm_proj_bc", tm=1024, tn=1024)
    dt_raw = _matmul(a1, w_dt, mode='nn', name="mm_dt", tm=1024, tn=128)
    ya = _gmlp_fwd(p_uv, ln_a_g, ln_a_b, w_s[0], b_exp, norm_a_g)
    yb, ysave, hs, *gathered = _ssd_fwd(p_z, p_xs, p_bc, dt_raw, cs_w, conv_ssm_b, dtb, alog, de, ssm_norm_g, consts,
                                        _hosted_gather([slots[n] for n in later]))
    g_out, g_up, g_down, g_pg, g_ple = gathered
    w_out_a, w_out_b = g_out[:N_CHIPS // 2].reshape(D_A, d), g_out[N_CHIPS // 2:].reshape(D_SSM, d)
    w_down_f = g_down.reshape(D_FF, d)
    w_pg_f = g_pg.reshape(d, d)
    h1 = _matmul(ya, w_out_a, mode='nn', name="mm_out_a", res=xt, tm=1024, tn=1024)
    h1 = _matmul(yb, w_out_b, mode='nn', name="mm_out_b", res=h1, tm=1024, tn=1024)
    f = _rms_fwd(h1, norm_ffn_g, "rms_ffn")
    hid = _matmul(f, g_up, mode='nn', name="mm_up", b_sharded=True, tm=1024, tn=1408)
    act = _ffn_act_fwd(hid, cf_w, conv_ffn_b)
    h2 = _matmul(act, w_down_f, mode='nn', name="mm_down", res=h1, tm=1024, tn=1024, tk=2816)
    n3 = _rms_fwd(h2, norm_ple_g, "rms_ple")
    gl = _matmul(n3, w_pg_f, mode='nn', name="mm_pg", tm=1024, tn=1024)
    pe = _matmul(pt, g_ple, mode='nn', name="mm_ple", b_sharded=True, tm=1024, tn=512)
    dh3, dgl, dpe, lossv, dgf = _tail(h2, gl, pe, tgt, norm_final_g[None, :])

    gs_ple = _matmul(pt, dpe, mode='tn', name="mm_dw_ple", out_dtype=BF16, out_shards=N_CHIPS,
                     tm=256, tn=512, tk=2048)
    gs_pg = _matmul(n3, dgl, mode='tn', name="mm_dw_pg", out_dtype=BF16, tm=1024, tn=1024, tk=2048)
    dn3 = _matmul(dgl, w_pg_f, mode='nt', name="mm_dn3", tm=1024, tn=1024)
    dh2, dg_ple, dh2_b = _rms_bwd(h2, norm_ple_g, dn3, dh3, "rms_ple_bwd", True)
    dact = _matmul(dh2_b, w_down_f, mode='nt', name="mm_dact", out_dtype=BF16, tm=1024, tn=1408)
    gs_down = _matmul(act, dh2_b, mode='tn', name="mm_dw_down", out_dtype=BF16, tm=1408, tn=1024, tk=2048)
    dpg, dpu, wg_acc, wu_acc = _ffn_act_bwd(hid, cf_w, conv_ffn_b, dact)
    hc = N_CHIPS // 2
    gs_up = jnp.concatenate(
        [_matmul(f, dpg, mode='tn', name="mm_dw_up_g", out_dtype=BF16, out_shards=hc, tm=1024, tn=1408, tk=2048),
         _matmul(f, dpu, mode='tn', name="mm_dw_up_u", out_dtype=BF16, out_shards=hc, tm=1024, tn=1408, tk=2048)],
        axis=0)
    early = [gs_up, gs_down.reshape(N_CHIPS, D_FF // N_CHIPS, d), gs_pg.reshape(N_CHIPS, d // N_CHIPS, d), gs_ple]
    df, *sib_e = _matmul(dpg, g_up, mode='nt', name="mm_df_g", b_sharded=True, tm=1024, tn=1024, tk=2816,
                         hosted=_hosted_rs_sibling(early))
    part_e = [_add_pairs(a, b, "rs_add_e%d" % i) for i, (a, b) in enumerate(zip(early, sib_e))]
    df = _matmul(dpu, g_up, mode='nt', name="mm_df_u", b_sharded=True, b_shard_off=hc, res=df,
                 tm=1024, tn=1024, tk=2816)
    dh1, dg_ffn, dh1_b = _rms_bwd(h1, norm_ffn_g, df, dh2, "rms_ffn_bwd", True)
    dya = _matmul(dh1_b, w_out_a, mode='nt', name="mm_dya", tm=1024, tn=1024)
    dyb = _matmul(dh1_b, w_out_b, mode='nt', name="mm_dyb", tm=1024, tn=1024)
    gs_out = jnp.concatenate(
        [_matmul(ya, dh1_b, mode='tn', name="mm_dw_out_a", out_dtype=BF16, tm=1024, tn=1024, tk=2048),
         _matmul(yb, dh1_b, mode='tn', name="mm_dw_out_b", out_dtype=BF16, tm=1024, tn=1024, tk=2048)], axis=0)
    duv, dws, dbs, dlng, dlnb, dnag = _gmlp_bwd(p_uv, dya, ln_a_g, ln_a_b, w_s[0], b_exp, norm_a_g)
    dproj, ddt_raw, acc_x, acc_b, acc_c, acc_gain, acc_head, *rcv_e = _ssd_bwd(
        p_z, p_xs, p_bc, dt_raw, dyb, ysave, hs, duv, cs_w, conv_ssm_b, dtb, alog, de, ssm_norm_g, consts,
        _hosted_rs_chips(part_e))
    dw_main = _matmul(a1, dproj, mode='tn', name="mm_dw_main", out_dtype=BF16, tm=1024, tn=1024, tk=2048)
    dw_dt = _matmul(a1, ddt_raw, mode='tn', name="mm_dw_dt", out_dtype=BF16, tm=1024, tn=128, tk=2048)
    gs_in = _col_sharded(jnp.concatenate([dw_main, dw_dt[:, :N_HEADS]], axis=1))
    late = [gs_in, gs_out.reshape(N_CHIPS, D_MIX // N_CHIPS, d)]
    part_l = [_add_pairs(a, b, "rs_add_l%d" % i) for i, (a, b) in enumerate(
        zip(late, _run_hosted(_hosted_rs_sibling(late), "rs_sibling_late")))]

    def conv_rows(acc, k):
        return acc[:, k, :].reshape(1, -1)

    dcw = jnp.concatenate([jnp.concatenate([conv_rows(acc_x, k), conv_rows(acc_b, k), conv_rows(acc_c, k)], axis=1)
                           for k in range(SSM_CONV)], axis=0)
    dcb = jnp.concatenate([conv_rows(acc_x, SSM_CONV), conv_rows(acc_b, SSM_CONV), conv_rows(acc_c, SSM_CONV)], axis=1)
    part = {
        'ln_a_g': dlng, 'ln_a_b': dlnb, 'w_s': dws, 'b_s': dbs, 'norm_a_g': dnag,
        'conv_ssm_w': dcw, 'conv_ssm_b': dcb,
        'dt_bias': acc_head[0:1, :N_HEADS], 'a_log': acc_head[1:2, :N_HEADS], 'd_skip': acc_head[2:3, :N_HEADS],
        'ssm_norm_g': acc_gain[:, 0, :], 'norm_ffn_g': dg_ffn,
        'conv_ffn_w': jnp.concatenate([wg_acc[:FFN_CONV], wu_acc[:FFN_CONV]], axis=1),
        'conv_ffn_b': jnp.concatenate([wg_acc[FFN_CONV:FFN_CONV + 1], wu_acc[FFN_CONV:FFN_CONV + 1]], axis=1),
        'norm_ple_g': dg_ple, 'norm_final_g': dgf,
    }
    full_shapes = {n: wts[n].shape for n in SMALL}
    full_shapes['conv_ssm_w'] = (1, SSM_CONV, D_XBC)
    full_shapes['conv_ffn_w'] = (1, FFN_CONV, 2 * D_FF)
    small_e = [n for n in SMALL if n != 'norm_mix_g']
    packed = _pack([part[n] for n in small_e] + [lossv[:, 0:1]])

    halves_e = [_sum_chips(a, b, "rs_sum_e%d" % i) for i, (a, b) in enumerate(zip(part_e, rcv_e))]
    da_dt = _matmul(ddt_raw, w_dt, mode='nt', name="mm_da_dt", tm=1024, tn=1024)
    da, *moved = _matmul(dproj, w_main, mode='nt', name="mm_da", res=da_dt, tm=1024, tn=1024, tk=2560,
                         hosted=_hosted_join([_hosted_rs_chips(part_l), _hosted_share_sibling(halves_e),
                                              _hosted_allgather(packed)]))
    rcv_l, g_early, gathered = moved[:len(late)], moved[len(late):-1], moved[-1]
    dx, dg_mix = _rms_bwd(xt, norm_mix_g, da, dh1, "rms_mix_bwd", False)

    halves_l = [_sum_chips(a, b, "rs_sum_l%d" % i) for i, (a, b) in enumerate(zip(part_l, rcv_l))]
    g_big = dict(zip(['w_up', 'w_down', 'w_ple_gate', 'w_ple'], g_early))
    g_big.update(zip(['w_in', 'w_out'], _run_hosted(_hosted_share_sibling(halves_l), "share_sibling_late")))

    pieces = _unpack(_sum_devices(gathered, "sum_devices"), [full_shapes[n] for n in small_e] + [(1,)])
    g_small = dict(zip(small_e, pieces[:-1]))
    loss = pieces[-1][0]
    mix = _sum_devices(_run_hosted(_hosted_allgather(_pack([dg_mix])), "allgather_mix")[0], "sum_devices_mix")
    g_small['norm_mix_g'] = _unpack(mix, [full_shapes['norm_mix_g']])[0]
    for n in ('conv_ssm_w', 'conv_ffn_w'):
        width = wts[n].shape[2]
        g_small[n] = lax.dynamic_slice_in_dim(g_small[n], chip * width, width, axis=2)

    grads, delta, new_m, new_v = {}, {}, {}, {}
    for n in BIG:
        shp = wts[n].shape
        dl, mn, vn = _adamw(wts[n][0], g_big[n], mom[n][0], var[n][0], "adamw_" + n)
        grads[n], delta[n], new_m[n], new_v[n] = (g_big[n].reshape(shp), dl.reshape(shp), mn.reshape(shp),
                                                  vn.reshape(shp))
    shapes = [wts[n].shape for n in SMALL]
    dl, mn, vn = _adamw(_pack([wts[n] for n in SMALL]), _pack([g_small[n] for n in SMALL]),
                        _pack([mom[n] for n in SMALL]), _pack([var[n] for n in SMALL]), "adamw_small")
    for n, a, b, c in zip(SMALL, _unpack(dl, shapes), _unpack(mn, shapes), _unpack(vn, shapes)):
        grads[n], delta[n], new_m[n], new_v[n] = g_small[n], a, b, c

    return (loss, dx[None], *[grads[n] for n in WEIGHTS], *[delta[n] for n in WEIGHTS],
            *[new_m[n] for n in WEIGHTS], *[new_v[n] for n in WEIGHTS])
```

```python
import functools
import math

import jax
import jax.numpy as jnp
from jax import lax
from jax.experimental import pallas as pl
from jax.experimental.pallas import tpu as pltpu

D_MODEL = 2048
SEQ = 8192
D_MIX = 2 * D_MODEL
D_A = D_MIX // 2
CHUNK = 128
N_GROUPS_A = D_A // 128
D_SSM = D_MIX - D_A
HEAD_DIM = 64
N_HEADS = D_SSM // HEAD_DIM
HEADS_PER_GROUP = 4
N_SSM_GROUPS = N_HEADS // HEADS_PER_GROUP
GW = HEADS_PER_GROUP * HEAD_DIM
D_STATE = 128
SSM_CONV = 4
D_BC = N_SSM_GROUPS * D_STATE
D_XBC = D_SSM + 2 * D_BC
D_MAIN = 2 * D_A + D_SSM + D_XBC
D_IN = D_MAIN + N_HEADS
D_FF = (D_MODEL * 11) // 4
FFN_CONV = 3
D_PLE = 256
EPS = 1e-6
HPAD = 128
N_CHIPS = 4

ADAM_LR = 0.001
ADAM_B1 = 0.9
ADAM_B2 = 0.999
ADAM_EPS = 1e-08
ADAM_WD = 0.01
ADAM_STEP = 10

F32 = jnp.float32
BF16 = jnp.bfloat16
MESH = pl.DeviceIdType.MESH
VMEM_LIMIT = 56 * 1024 * 1024


def _cparams(sem):
    return pltpu.CompilerParams(dimension_semantics=sem, vmem_limit_bytes=VMEM_LIMIT)


def _tile(n, pref, mult):
    t = min(pref, n)
    t -= t % mult
    while n % t:
        t -= mult
    return t


def _dot(a, b):
    return jnp.dot(a, b, preferred_element_type=F32)


def _dot_nt(a, b):
    return lax.dot_general(a, b, (((1,), (1,)), ((), ())), preferred_element_type=F32)


def _dot_tn(a, b):
    return lax.dot_general(a, b, (((0,), (0,)), ((), ())), preferred_element_type=F32)


def _split3(x):
    hi = x.astype(BF16)
    r = x - hi.astype(F32)
    mid = r.astype(BF16)
    lo = (r - mid.astype(F32)).astype(BF16)
    return hi, mid, lo


def _x01(x, e):
    h, m, l = _split3(x)
    return _dot(h, e) + _dot(m, e) + _dot(l, e)


def _x01_nt(x, e):
    h, m, l = _split3(x)
    return _dot_nt(h, e) + _dot_nt(m, e) + _dot_nt(l, e)


def _e01x(e, x):
    h, m, l = _split3(x)
    return _dot(e, h) + _dot(e, m) + _dot(e, l)


def _e01x_tn(e, x):
    h, m, l = _split3(x)
    return _dot_tn(e, h) + _dot_tn(e, m) + _dot_tn(e, l)


def _sigmoid(x):
    return 1.0 / (1.0 + jnp.exp(-x))


_GELU_C = math.sqrt(2.0 / math.pi)


def _gelu_and_grad(x):
    x2 = x * x
    th = jnp.tanh(_GELU_C * (x + 0.044715 * x * x2))
    y = 0.5 * x * (1.0 + th)
    dy = 0.5 * (1.0 + th) + 0.5 * x * (1.0 - th * th) * (_GELU_C * (1.0 + 3.0 * 0.044715 * x2))
    return y, dy


def _silu_and_grad(x):
    s = _sigmoid(x)
    return x * s, s * (1.0 + x * (1.0 - s))


def _softplus(x):
    u = jnp.exp(-jnp.abs(x))
    w = 1.0 + u
    l1p = jnp.where(w == 1.0, u, jnp.log(w) * (u / (w - 1.0)))
    return jnp.maximum(x, 0.0) + l1p


def _matmul(a, b, *, mode, name, out_dtype=F32, res=None, tm=512, tn=512, tk=2048,
            b_sharded=False, b_shard_off=0, out_shards=0, hosted=None):
    if mode == 'tn':
        kdim, m = a.shape
        n = b.shape[1]
    else:
        m, kdim = a.shape
        if b_sharded:
            s_b, d1, d2 = b.shape
            n = s_b * d2 if mode == 'nn' else d1
        else:
            n = b.shape[1] if mode == 'nn' else b.shape[0]
    per = None
    if b_sharded:
        per = b.shape[2]
    if out_shards:
        per = n // out_shards
    tm = _tile(m, tm, 128 if mode == 'tn' else 8)
    if mode == 'nt' and b_sharded:
        tn = _tile(n, tn, 128)
        tk = _tile(per, tk, 128)
    elif per is not None:
        tn = _tile(per, tn, 128)
        tk = _tile(kdim, tk, 128 if mode != 'tn' else 8)
    else:
        tn = _tile(n, tn, 128)
        tk = _tile(kdim, tk, 128 if mode != 'tn' else 8)
    nm, nn_, nk = m // tm, n // tn, kdim // tk
    has_res = res is not None
    n_in = 2 + has_res
    nh_in = len(hosted['arrays']) if hosted else 0
    nh_out = len(hosted['out_shape']) if hosted else 0

    def body(*refs):
        a_ref, b_ref = refs[0], refs[1]
        res_ref = refs[2] if has_res else None
        o_ref = refs[n_in + nh_in]
        if hosted:
            hins = refs[n_in:n_in + nh_in]
            houts = refs[n_in + nh_in + 1:n_in + nh_in + 1 + nh_out]
            sems = refs[-2:]
            ids = [pl.program_id(d) for d in range(3)]
            at_first = jnp.logical_and(jnp.logical_and(ids[0] == 0, ids[1] == 0), ids[2] == 0)
            at_last = jnp.logical_and(jnp.logical_and(ids[0] == nn_ - 1, ids[1] == nm - 1), ids[2] == nk - 1)
            _host_phase(hosted, 'start', at_first, hins, houts, sems)
        av = a_ref[...].astype(BF16)
        bv = b_ref[...].astype(BF16)
        if mode == 'nn':
            p = _dot(av, bv)
        elif mode == 'nt':
            p = _dot_nt(av, bv)
        else:
            p = _dot_tn(av, bv)

        def fin(v):
            if has_res:
                v = v + res_ref[...]
            o_ref[...] = v.astype(o_ref.dtype)

        if nk == 1:
            fin(p)
        else:
            acc_ref = refs[n_in + nh_in + 1 + nh_out]
            k = pl.program_id(2)

            @pl.when(k == 0)
            def _():
                acc_ref[...] = p

            @pl.when(k > 0)
            def _():
                acc_ref[...] += p

            @pl.when(k == nk - 1)
            def _():
                fin(acc_ref[...])
        if hosted:
            _host_phase(hosted, 'mid', at_last, hins, houts, sems)
            _host_phase(hosted, 'finish', at_last, hins, houts, sems)

    if mode == 'nn':
        a_spec = pl.BlockSpec((tm, tk), lambda j, i, k: (i, k))
        if b_sharded:
            nps = per // tn
            b_spec = pl.BlockSpec((None, tk, tn), lambda j, i, k: (j // nps, k, j % nps))
        else:
            b_spec = pl.BlockSpec((tk, tn), lambda j, i, k: (k, j))
    elif mode == 'nt':
        a_spec = pl.BlockSpec((tm, tk), lambda j, i, k: (i, k))
        if b_sharded:
            kps = per // tk
            b_spec = pl.BlockSpec((None, tn, tk), lambda j, i, k: (k // kps + b_shard_off, j, k % kps))
        else:
            b_spec = pl.BlockSpec((tn, tk), lambda j, i, k: (j, k))
    else:
        a_spec = pl.BlockSpec((tk, tm), lambda j, i, k: (k, i))
        b_spec = pl.BlockSpec((tk, tn), lambda j, i, k: (k, j))
    in_specs = [a_spec, b_spec]
    args = [a, b]
    if has_res:
        in_specs.append(pl.BlockSpec((tm, tn), lambda j, i, k: (i, j)))
        args.append(res)
    if out_shards:
        nps_o = per // tn
        out_shape = jax.ShapeDtypeStruct((out_shards, m, per), out_dtype)
        out_spec = pl.BlockSpec((None, tm, tn), lambda j, i, k: (j // nps_o, i, j % nps_o))
    else:
        out_shape = jax.ShapeDtypeStruct((m, n), out_dtype)
        out_spec = pl.BlockSpec((tm, tn), lambda j, i, k: (i, j))
    scratch = [pltpu.VMEM((tm, tn), F32)] if nk > 1 else []
    if not hosted:
        return pl.pallas_call(
            body, name=name, grid=(nn_, nm, nk), in_specs=in_specs, out_specs=out_spec,
            out_shape=out_shape, scratch_shapes=scratch,
            compiler_params=_cparams(("parallel", "parallel", "arbitrary")),
        )(*args)
    h_in, h_out, h_shape, h_scratch, h_alias = _host_plumbing(hosted, n_in, 1)
    return pl.pallas_call(
        body, name=name, grid=(nn_, nm, nk), in_specs=in_specs + h_in, out_specs=[out_spec] + h_out,
        out_shape=[out_shape] + h_shape, scratch_shapes=scratch + h_scratch, input_output_aliases=h_alias,
        compiler_params=_cparams(("arbitrary", "arbitrary", "arbitrary")),
    )(*args, *hosted['arrays'])


def _rms_fwd(x, g, name):
    t, d = x.shape
    tt = _tile(t, 512, 8)

    def body(x_ref, g_ref, o_ref):
        xv = x_ref[...]
        r = lax.rsqrt(jnp.mean(xv * xv, axis=-1, keepdims=True) + EPS)
        o_ref[...] = (xv * r * g_ref[...]).astype(o_ref.dtype)

    return pl.pallas_call(
        body, name=name, grid=(t // tt,),
        in_specs=[pl.BlockSpec((tt, d), lambda i: (i, 0)), pl.BlockSpec((1, d), lambda i: (0, 0))],
        out_specs=pl.BlockSpec((tt, d), lambda i: (i, 0)),
        out_shape=jax.ShapeDtypeStruct((t, d), BF16),
        compiler_params=_cparams(("parallel",)),
    )(x, g)


def _rms_bwd(x, g, dy, dres, name, also_bf16):
    t, d = x.shape
    tt = _tile(t, 256, 16)

    def body(x_ref, g_ref, dy_ref, dres_ref, dx_ref, dg_ref, *dxb):
        i = pl.program_id(0)
        xv = x_ref[...]
        r = lax.rsqrt(jnp.mean(xv * xv, axis=-1, keepdims=True) + EPS)
        xh = xv * r
        dyv = dy_ref[...].astype(F32)
        dxh = dyv * g_ref[...]
        c = jnp.mean(dxh * xh, axis=-1, keepdims=True)
        dx = dres_ref[...] + r * (dxh - xh * c)
        dx_ref[...] = dx
        for dxb_ref in dxb:
            dxb_ref[...] = dx.astype(BF16)
        part = jnp.sum(dyv * xh, axis=0, keepdims=True)

        @pl.when(i == 0)
        def _():
            dg_ref[...] = part

        @pl.when(i > 0)
        def _():
            dg_ref[...] += part

    row = pl.BlockSpec((tt, d), lambda i: (i, 0))
    vec = pl.BlockSpec((1, d), lambda i: (0, 0))
    return pl.pallas_call(
        body, name=name, grid=(t // tt,),
        in_specs=[row, vec, row, row], out_specs=[row, vec] + [row] * also_bf16,
        out_shape=[jax.ShapeDtypeStruct((t, d), F32), jax.ShapeDtypeStruct((1, d), F32)]
        + [jax.ShapeDtypeStruct((t, d), BF16)] * also_bf16,
        compiler_params=_cparams(("arbitrary",)),
    )(x, g, dy, dres)


def _tail(h2, gl, pe, target, gfin):
    t, d = h2.shape
    tt = _tile(t, 256, 8)

    def body(h2_ref, gl_ref, pe_ref, tg_ref, gf_ref, dh3_ref, dgl_ref, dpe_ref, loss_ref, dgf_ref):
        i = pl.program_id(0)
        sig = _sigmoid(gl_ref[...])
        pev = pe_ref[...]
        h3 = h2_ref[...] + sig * pev
        r = lax.rsqrt(jnp.mean(h3 * h3, axis=-1, keepdims=True) + EPS)
        xh = h3 * r
        gf = gf_ref[...]
        e = xh * gf - tg_ref[...]
        dy = e * (1.0 / d)
        dxh = dy * gf
        c = jnp.mean(dxh * xh, axis=-1, keepdims=True)
        dh3 = r * (dxh - xh * c)
        dh3_ref[...] = dh3
        dgl_ref[...] = (dh3 * pev * sig * (1.0 - sig)).astype(BF16)
        dpe_ref[...] = (dh3 * sig).astype(BF16)
        lpart = jnp.sum(e * e, axis=0, keepdims=True) * (0.5 / d)
        gpart = jnp.sum(dy * xh, axis=0, keepdims=True)

        @pl.when(i == 0)
        def _():
            loss_ref[...] = lpart
            dgf_ref[...] = gpart

        @pl.when(i > 0)
        def _():
            loss_ref[...] += lpart
            dgf_ref[...] += gpart

        @pl.when(i == t // tt - 1)
        def _():
            loss_ref[...] = jnp.broadcast_to(jnp.sum(loss_ref[...], axis=-1, keepdims=True), (1, d))

    row = pl.BlockSpec((tt, d), lambda i: (i, 0))
    vec = pl.BlockSpec((1, d), lambda i: (0, 0))
    return pl.pallas_call(
        body, name="tail", grid=(t // tt,),
        in_specs=[row, row, row, row, vec], out_specs=[row, row, row, vec, vec],
        out_shape=[jax.ShapeDtypeStruct((t, d), F32), jax.ShapeDtypeStruct((t, d), BF16),
                   jax.ShapeDtypeStruct((t, d), BF16), jax.ShapeDtypeStruct((1, d), F32),
                   jax.ShapeDtypeStruct((1, d), F32)],
        compiler_params=_cparams(("arbitrary",)),
    )(h2, gl, pe, target, gfin)


def _conv(cur_ref, prev_ref, w_ref, b_ref, ext_ref, first, width):
    rows = cur_ref.shape[0]
    ext_ref[0:8, :] = jnp.where(first, 0.0, prev_ref[...])
    ext_ref[8:8 + rows, :] = cur_ref[...]
    acc = b_ref[...]
    for k in range(width):
        acc = acc + w_ref[k:k + 1, :] * ext_ref[pl.ds(9 - width + k, rows), :]
    return acc


def _ffn_specs(t, tt, tc, nf):
    hb = tt // 8
    cur_g = pl.BlockSpec((tt, tc), lambda j, i: (i, j))
    cur_u = pl.BlockSpec((tt, tc), lambda j, i: (i, j + nf))
    prev_g = pl.BlockSpec((8, tc), lambda j, i: (jnp.maximum(i * hb - 1, 0), j))
    prev_u = pl.BlockSpec((8, tc), lambda j, i: (jnp.maximum(i * hb - 1, 0), j + nf))
    w_g = pl.BlockSpec((FFN_CONV, tc), lambda j, i: (0, j))
    w_u = pl.BlockSpec((FFN_CONV, tc), lambda j, i: (0, j + nf))
    b_g = pl.BlockSpec((1, tc), lambda j, i: (0, j))
    b_u = pl.BlockSpec((1, tc), lambda j, i: (0, j + nf))
    return [cur_g, prev_g, cur_u, prev_u, w_g, w_u, b_g, b_u]


FFN_TC = 512


def _shift_down(prev, cur, n, rid):
    return jnp.where(rid < n, pltpu.roll(prev, n, 0), pltpu.roll(cur, n, 0))


def _shift_up(cur, nxt, n, rid):
    return jnp.where(rid < 8 - n, pltpu.roll(cur, 8 - n, 0), pltpu.roll(nxt, 8 - n, 0))


def _conv3_group(prev, cur, w_ref, b_ref, rid):
    x1 = _shift_down(prev, cur, 1, rid)
    x2 = _shift_down(prev, cur, 2, rid)
    return b_ref[...] + w_ref[2:3, :] * cur + w_ref[1:2, :] * x1 + w_ref[0:1, :] * x2


def _ffn_act_fwd(hid, cw, cb):
    t = hid.shape[0]
    tt = _tile(t, 512, 16)
    tc = _tile(D_FF, FFN_TC, 128)
    nf = D_FF // tc

    def body(g_ref, gp_ref, u_ref, up_ref, wg_ref, wu_ref, bg_ref, bu_ref, o_ref, cg_ref, cu_ref):
        first = pl.program_id(1) == 0
        rid = lax.broadcasted_iota(jnp.int32, (8, tc), 0)

        def step(s, carry):
            pg, pu = carry
            r0 = pl.multiple_of(s * 16, 16)
            g0, g1 = g_ref[pl.ds(r0, 8), :], g_ref[pl.ds(r0 + 8, 8), :]
            u0, u1 = u_ref[pl.ds(r0, 8), :], u_ref[pl.ds(r0 + 8, 8), :]
            gate = jnp.concatenate([_conv3_group(pg, g0, wg_ref, bg_ref, rid),
                                    _conv3_group(g0, g1, wg_ref, bg_ref, rid)], axis=0)
            up = jnp.concatenate([_conv3_group(pu, u0, wu_ref, bu_ref, rid),
                                  _conv3_group(u0, u1, wu_ref, bu_ref, rid)], axis=0)
            cg_ref[pl.ds(r0, 16), :] = gate.astype(BF16)
            cu_ref[pl.ds(r0, 16), :] = up.astype(BF16)
            o_ref[pl.ds(r0, 16), :] = (gate * _sigmoid(gate) * up).astype(BF16)
            return g1, u1

        init = (jnp.where(first, 0.0, gp_ref[...]), jnp.where(first, 0.0, up_ref[...]))
        lax.fori_loop(0, tt // 16, step, init)

    blk = pl.BlockSpec((tt, tc), lambda j, i: (i, j))
    return pl.pallas_call(
        body, name="ffn_act_fwd", grid=(nf, t // tt), in_specs=_ffn_specs(t, tt, tc, nf),
        out_specs=[blk, blk, blk],
        out_shape=[jax.ShapeDtypeStruct((t, D_FF), BF16)] * 3,
        compiler_params=_cparams(("parallel", "arbitrary")),
    )(hid, hid, hid, hid, cw, cw, cb, cb)


def _ffn_act_bwd(hid, cw, conv_g, conv_u, dact):
    t = hid.shape[0]
    tt = _tile(t, 512, 16)
    tc = _tile(D_FF, FFN_TC, 128)
    nf = D_FF // tc
    nt = t // tt
    n16 = tt // 16

    def body(g_ref, u_ref, wg_ref, wu_ref, cg_ref, cgn_ref, cu_ref, cun_ref, da_ref, dan_ref,
             og_ref, ou_ref, ag_ref, au_ref, accs):
        i = pl.program_id(1)
        first, last = i == 0, i == nt - 1
        rid = lax.broadcasted_iota(jnp.int32, (8, tc), 0)
        accs[...] = jnp.zeros_like(accs)

        def dgroup(gate, up, da):
            sv, sgr = _silu_and_grad(gate)
            return da * up * sgr, da * sv

        def finish(x, d0, d1, w_ref):
            s1 = _shift_up(d0, d1, 1, rid)
            s2 = _shift_up(d0, d1, 2, rid)
            dpre = w_ref[2:3, :] * d0 + w_ref[1:2, :] * s1 + w_ref[0:1, :] * s2
            return dpre, (x * s2, x * s1, x * d0, d0)

        def two_groups(it, carry, gate_blk, up_blk, da_blk, zero_ahead):
            d0g, d0u, gate1, up1, da1 = carry
            r0 = it * 16 if isinstance(it, int) else pl.multiple_of(it * 16, 16)
            x0g, x0u = g_ref[pl.ds(r0, 8), :], u_ref[pl.ds(r0, 8), :]
            x1g, x1u = g_ref[pl.ds(r0 + 8, 8), :], u_ref[pl.ds(r0 + 8, 8), :]
            d1g, d1u = dgroup(gate1, up1, da1)
            d2g, d2u = dgroup(gate_blk[0:8], up_blk[0:8], da_blk[0:8])
            d2g = jnp.where(zero_ahead, 0.0, d2g)
            d2u = jnp.where(zero_ahead, 0.0, d2u)
            for half, (xa, xb, da_, db_, dc_, w_ref, o_ref) in enumerate((
                    (x0g, x1g, d0g, d1g, d2g, wg_ref, og_ref), (x0u, x1u, d0u, d1u, d2u, wu_ref, ou_ref))):
                pa, prods_a = finish(xa, da_, db_, w_ref)
                pb, prods_b = finish(xb, db_, dc_, w_ref)
                o_ref[pl.ds(r0, 16), :] = jnp.concatenate([pa, pb], axis=0).astype(BF16)
                for k in range(4):
                    accs[4 * half + k] += prods_a[k] + prods_b[k]
            return d2g, d2u, gate_blk[8:16], up_blk[8:16], da_blk[8:16]

        def rows16(ref, r):
            return ref[pl.ds(r, 16), :].astype(F32)

        def step(it, carry):
            r1 = pl.multiple_of(it * 16 + 16, 16)
            return two_groups(it, carry, rows16(cg_ref, r1), rows16(cu_ref, r1), rows16(da_ref, r1), False)

        g0, u0, da0 = rows16(cg_ref, 0), rows16(cu_ref, 0), rows16(da_ref, 0)
        d0g, d0u = dgroup(g0[0:8], u0[0:8], da0[0:8])
        carry = lax.fori_loop(0, n16 - 1, step, (d0g, d0u, g0[8:16], u0[8:16], da0[8:16]))
        two_groups(n16 - 1, carry, cgn_ref[...].astype(F32), cun_ref[...].astype(F32), dan_ref[...].astype(F32), last)

        @pl.when(first)
        def _():
            ag_ref[...] = jnp.zeros_like(ag_ref)
            au_ref[...] = jnp.zeros_like(au_ref)

        for half, a_ref in enumerate((ag_ref, au_ref)):
            for k in range(4):
                a_ref[k:k + 1, :] += jnp.sum(accs[4 * half + k], axis=0, keepdims=True)

    def nxt16(i):
        return jnp.minimum((i + 1) * n16, t // 16 - 1)

    out_blk = pl.BlockSpec((tt, tc), lambda j, i: (i, j))
    acc_spec = pl.BlockSpec((8, tc), lambda j, i: (0, j))
    nxt_blk = pl.BlockSpec((16, tc), lambda j, i: (nxt16(i), j))
    in_specs = [out_blk, pl.BlockSpec((tt, tc), lambda j, i: (i, j + nf)),
                pl.BlockSpec((FFN_CONV, tc), lambda j, i: (0, j)), pl.BlockSpec((FFN_CONV, tc), lambda j, i: (0, j + nf)),
                out_blk, nxt_blk, out_blk, nxt_blk, out_blk, nxt_blk]
    return pl.pallas_call(
        body, name="ffn_act_bwd", grid=(nf, nt), in_specs=in_specs,
        out_specs=[out_blk, out_blk, acc_spec, acc_spec],
        out_shape=[jax.ShapeDtypeStruct((t, D_FF), BF16), jax.ShapeDtypeStruct((t, D_FF), BF16),
                   jax.ShapeDtypeStruct((8, D_FF), F32), jax.ShapeDtypeStruct((8, D_FF), F32)],
        scratch_shapes=[pltpu.VMEM((8, 8, tc), F32)],
        compiler_params=_cparams(("parallel", "arbitrary")),
    )(hid, hid, cw, cw, conv_g, conv_g, conv_u, conv_u, dact, dact)


def _tri_mask():
    r = lax.broadcasted_iota(jnp.int32, (CHUNK, CHUNK), 0)
    c = lax.broadcasted_iota(jnp.int32, (CHUNK, CHUNK), 1)
    return r >= c


def _gmlp_group_fwd(uv_ref, lng_ref, lnb_ref, ws_ref, bexp_ref, tri, g, want_grad):
    lo, hi = g * 128, (g + 1) * 128
    u_pre = uv_ref[:, lo:hi]
    v_pre = uv_ref[:, D_A + lo:D_A + hi]
    u, du = _gelu_and_grad(u_pre)
    v, dv = _gelu_and_grad(v_pre)
    mu = jnp.mean(v, axis=-1, keepdims=True)
    dc = v - mu
    rs = lax.rsqrt(jnp.mean(dc * dc, axis=-1, keepdims=True) + EPS)
    xh = dc * rs
    vn = (xh * lng_ref[:, lo:hi] + lnb_ref[:, lo:hi]).astype(BF16)
    w = jnp.where(tri, ws_ref[g], 0.0).astype(BF16)
    sg = _dot(w, vn) + bexp_ref[g]
    if want_grad:
        return u, du, dv, rs, xh, vn, w, sg
    return u * sg


def _gmlp_fwd(proj, ln_g, ln_b, w_s, b_exp, na_g):
    t = proj.shape[0]
    ng = N_GROUPS_A

    def body(uv_ref, lng_ref, lnb_ref, ws_ref, bexp_ref, nag_ref, o_ref):
        tri = _tri_mask()
        ys = [_gmlp_group_fwd(uv_ref, lng_ref, lnb_ref, ws_ref, bexp_ref, tri, g, False) for g in range(ng)]
        ssq = ys[0] * 0.0
        for y in ys:
            ssq = ssq + y * y
        r = lax.rsqrt(jnp.sum(ssq, axis=-1, keepdims=True) * (1.0 / D_A) + EPS)
        for g, y in enumerate(ys):
            o_ref[:, g * 128:(g + 1) * 128] = (y * r * nag_ref[:, g * 128:(g + 1) * 128]).astype(BF16)

    vec = pl.BlockSpec((1, D_A), lambda i: (0, 0))
    cube = pl.BlockSpec((ng, CHUNK, CHUNK), lambda i: (0, 0, 0))
    return pl.pallas_call(
        body, name="gmlp_fwd", grid=(t // CHUNK,),
        in_specs=[pl.BlockSpec((CHUNK, 2 * D_A), lambda i: (i, 0)), vec, vec, cube, cube, vec],
        out_specs=pl.BlockSpec((CHUNK, D_A), lambda i: (i, 0)),
        out_shape=jax.ShapeDtypeStruct((t, D_MIX), BF16),
        compiler_params=_cparams(("parallel",)),
    )(proj, ln_g, ln_b, w_s, b_exp, na_g)


def _gmlp_bwd(proj, dyab, ln_g, ln_b, w_s, b_exp, na_g):
    t = proj.shape[0]
    ng = N_GROUPS_A
    nsteps = t // CHUNK

    def body(uv_ref, dy_ref, lng_ref, lnb_ref, ws_ref, bexp_ref, nag_ref,
             duv_ref, dws_ref, dbs_ref, dlng_ref, dlnb_ref, dnag_ref, dbacc):
        i = pl.program_id(0)
        tri = _tri_mask()

        @pl.when(i == 0)
        def _():
            dws_ref[...] = jnp.zeros_like(dws_ref)
            dbacc[...] = jnp.zeros_like(dbacc)
            dlng_ref[...] = jnp.zeros_like(dlng_ref)
            dlnb_ref[...] = jnp.zeros_like(dlnb_ref)
            dnag_ref[...] = jnp.zeros_like(dnag_ref)

        st = [_gmlp_group_fwd(uv_ref, lng_ref, lnb_ref, ws_ref, bexp_ref, tri, g, True) for g in range(ng)]
        ssq = st[0][0] * 0.0
        for s in st:
            y = s[0] * s[7]
            ssq = ssq + y * y
        r = lax.rsqrt(jnp.sum(ssq, axis=-1, keepdims=True) * (1.0 / D_A) + EPS)
        csum = st[0][0] * 0.0
        for g, s in enumerate(st):
            sl = slice(g * 128, (g + 1) * 128)
            xhy = s[0] * s[7] * r
            dya = dy_ref[:, sl]
            dnag_ref[:, sl] += jnp.sum(dya * xhy, axis=0, keepdims=True)
            csum = csum + dya * nag_ref[:, sl] * xhy
        c1 = jnp.sum(csum, axis=-1, keepdims=True) * (1.0 / D_A)
        for g, s in enumerate(st):
            u, du, dv, rs, xh, vn, w, sg = s
            sl = slice(g * 128, (g + 1) * 128)
            dy = r * (dy_ref[:, sl] * nag_ref[:, sl] - u * sg * r * c1)
            dsg = dy * u
            dsg_b = dsg.astype(BF16)
            dws_ref[g] += _dot_nt(dsg_b, vn)
            dbacc[g] += dsg
            dvn = _dot_tn(w, dsg_b)
            dlnb_ref[:, sl] += jnp.sum(dvn, axis=0, keepdims=True)
            dlng_ref[:, sl] += jnp.sum(dvn * xh, axis=0, keepdims=True)
            dxh = dvn * lng_ref[:, sl]
            dvv = rs * (dxh - jnp.mean(dxh, axis=-1, keepdims=True)
                        - xh * jnp.mean(dxh * xh, axis=-1, keepdims=True))
            duv_ref[:, sl] = (dy * sg * du).astype(BF16)
            duv_ref[:, D_A + g * 128:D_A + (g + 1) * 128] = (dvv * dv).astype(BF16)

        @pl.when(i == nsteps - 1)
        def _():
            for g in range(ng):
                dws_ref[g] = jnp.where(tri, dws_ref[g], 0.0)
                dbs_ref[g] = jnp.sum(dbacc[g], axis=-1, keepdims=True)

    vec = pl.BlockSpec((1, D_A), lambda i: (0, 0))
    cube = pl.BlockSpec((ng, CHUNK, CHUNK), lambda i: (0, 0, 0))
    return pl.pallas_call(
        body, name="gmlp_bwd", grid=(nsteps,),
        in_specs=[pl.BlockSpec((CHUNK, 2 * D_A), lambda i: (i, 0)),
                  pl.BlockSpec((CHUNK, D_A), lambda i: (i, 0)), vec, vec, cube, cube, vec],
        out_specs=[pl.BlockSpec((CHUNK, 2 * D_A), lambda i: (i, 0)), cube,
                   pl.BlockSpec((ng, CHUNK, 1), lambda i: (0, 0, 0)), vec, vec, vec],
        out_shape=[jax.ShapeDtypeStruct((t, 2 * D_A), BF16), jax.ShapeDtypeStruct((ng, CHUNK, CHUNK), F32),
                   jax.ShapeDtypeStruct((ng, CHUNK, 1), F32), jax.ShapeDtypeStruct((1, D_A), F32),
                   jax.ShapeDtypeStruct((1, D_A), F32), jax.ShapeDtypeStruct((1, D_A), F32)],
        scratch_shapes=[pltpu.VMEM((ng, CHUNK, CHUNK), F32)],
        compiler_params=_cparams(("arbitrary",)),
    )(proj, dyab, ln_g, ln_b, w_s, b_exp, na_g)


OFF_Z = 2 * D_A
OFF_XS = OFF_Z + D_SSM
OFF_B = OFF_XS + D_SSM
OFF_C = OFF_B + D_BC


def _ssd_consts():
    tri = jnp.tril(jnp.ones((CHUNK, CHUNK), F32)).astype(BF16)
    h = jnp.arange(HPAD)[None, :, None]
    g = jnp.arange(N_SSM_GROUPS)[:, None, None]
    j1 = jnp.arange(GW)[None, None, :]
    eh = (h == g * HEADS_PER_GROUP + j1 // HEAD_DIM).astype(BF16)
    j2 = jnp.arange(HEADS_PER_GROUP * 128)[None, None, :]
    e128 = (h == g * HEADS_PER_GROUP + j2 // 128).astype(BF16)
    return tri, eh, e128


def _ssd_in_specs(cmap):
    def rows(i):
        return cmap(i)

    def prev8(i):
        return jnp.maximum(cmap(i) * (CHUNK // 8) - 1, 0)

    def whole(*shape):
        return pl.BlockSpec(shape, lambda i: (0,) * len(shape))

    bcw = 2 * D_BC
    specs = [
        pl.BlockSpec((CHUNK, D_SSM), lambda i: (rows(i), OFF_Z // D_SSM)),
        pl.BlockSpec((CHUNK, D_SSM), lambda i: (rows(i), OFF_XS // D_SSM)),
        pl.BlockSpec((8, D_SSM), lambda i: (prev8(i), OFF_XS // D_SSM)),
        pl.BlockSpec((CHUNK, bcw), lambda i: (rows(i), OFF_B // bcw)),
        pl.BlockSpec((8, bcw), lambda i: (prev8(i), OFF_B // bcw)),
        pl.BlockSpec((CHUNK, HPAD), lambda i: (rows(i), 0)),
        whole(SSM_CONV, D_XBC), whole(1, D_XBC),
        whole(1, HPAD), whole(1, HPAD),
        whole(1, D_SSM), whole(1, D_SSM),
        whole(CHUNK, CHUNK),
        whole(N_SSM_GROUPS, HPAD, GW), whole(N_SSM_GROUPS, HPAD, HEADS_PER_GROUP * 128),
    ]
    return specs


N_SSD_IN = 15


def _lanes(ref, start, width):
    return ref.at[:, pl.ds(pl.multiple_of(start, 128), width)]


def _ssd_group_refs(ins, g):
    (z_ref, xs_ref, xsp_ref, bc_ref, bcp_ref, dt_ref, cw_ref, cb_ref, dtb_ref, alog_ref, de_ref, gain_ref,
     tri_ref, eh_ref, e128_ref) = ins
    ox, ob, oc = g * GW, g * D_STATE, D_BC + g * D_STATE
    return (_lanes(z_ref, ox, GW), _lanes(xs_ref, ox, GW), _lanes(xsp_ref, ox, GW),
            _lanes(bc_ref, ob, D_STATE), _lanes(bcp_ref, ob, D_STATE),
            _lanes(bc_ref, oc, D_STATE), _lanes(bcp_ref, oc, D_STATE), dt_ref,
            _lanes(cw_ref, ox, GW), _lanes(cw_ref, D_SSM + ob, D_STATE), _lanes(cw_ref, D_SSM + oc, D_STATE),
            _lanes(cb_ref, ox, GW), _lanes(cb_ref, D_SSM + ob, D_STATE), _lanes(cb_ref, D_SSM + oc, D_STATE),
            dtb_ref, alog_ref, _lanes(de_ref, ox, GW), _lanes(gain_ref, ox, GW), tri_ref,
            eh_ref.at[g], e128_ref.at[g])


N_SSD_OWN = 4


def _ssd_scratch():
    return [pltpu.VMEM((CHUNK + 8, GW), F32), pltpu.VMEM((CHUNK + 8, D_STATE), F32),
            pltpu.VMEM((CHUNK + 8, D_STATE), F32), pltpu.VMEM((CHUNK, GW), F32),
            pltpu.VMEM((HPAD, CHUNK), F32), pltpu.VMEM((CHUNK, HPAD), F32), pltpu.VMEM((CHUNK, HPAD), F32)]


def _ssd_two_sets(alloc):
    a, b = alloc[:7], alloc[7:7 + N_SSD_OWN]
    return ((a[0], a[1], a[2], a[4], a[3], a[5], a[6]), (b[0], b[1], b[2], a[4], b[3], a[5], a[6]))


def _ssd_chunk(ins, scr):
    dt_ref, dtb_ref, alog_ref, tri_ref = ins[5], ins[8], ins[9], ins[12]
    acst_sc, dt_sc, acs_sc = scr[3], scr[5], scr[6]
    dt = _softplus(dt_ref[...] + dtb_ref[...])
    acs = _e01x(tri_ref[...], dt * (-jnp.exp(alog_ref[...])))
    dt_sc[...] = dt
    acs_sc[...] = acs
    acst_sc[...] = acs.T


def _ssd_pre(first, g, refs, scr):
    (z_ref, xs_ref, xsp_ref, b_ref, bp_ref, c_ref, cp_ref, dt_ref, cwx, cwb, cwc, cbx, cbb, cbc,
     dtb_ref, alog_ref, de_ref, gain_ref, tri_ref, eh_ref, e128_ref) = refs
    ext_x, ext_b, ext_c, acst_sc, acse_sc, dt_sc, acs_sc = scr
    p = {}
    px = _conv(xs_ref, xsp_ref, cwx, cbx, ext_x, first, SSM_CONV)
    pb = _conv(b_ref, bp_ref, cwb, cbb, ext_b, first, SSM_CONV)
    pc = _conv(c_ref, cp_ref, cwc, cbc, ext_c, first, SSM_CONV)
    p['xs'], p['dsx'] = _silu_and_grad(px)
    p['bm'], p['dsb'] = _silu_and_grad(pb)
    p['cm'], p['dsc'] = _silu_and_grad(pc)
    p['dt_in'] = dt_ref[...] + dtb_ref[...]
    dt = dt_sc[...]
    p['dt'] = dt
    p['a'] = -jnp.exp(alog_ref[...])
    tri_b = tri_ref[...]
    acs = acs_sc[...]
    eh = eh_ref[...]
    p['eh'] = eh
    p['dt_e'] = _x01(dt, eh)
    acs_e = _x01(acs, eh)
    acse_sc[...] = acs_e
    p['acs_e'] = acs_e
    p['acs_c'] = _x01(acs, e128_ref[...])
    p['acs_last_e'] = acse_sc[pl.ds(CHUNK - 1, 1), :]
    p['xdt'] = p['xs'] * p['dt_e']
    p['decay_e'] = jnp.exp(p['acs_last_e'] - acs_e)
    p['cm_b'] = p['cm'].astype(BF16)
    p['bm_b'] = p['bm'].astype(BF16)
    p['scores'] = _dot_nt(p['cm_b'], p['bm_b'])
    p['tri_b'] = tri_b
    return p


def _ssd_l(p, g, r, acst_sc, tri):
    col = p['acs_c'][:, r * 128:(r + 1) * 128]
    row = acst_sc[pl.ds(g * HEADS_PER_GROUP + r, 1), :]
    return jnp.where(tri, jnp.exp(jnp.minimum(col - row, 0.0)), 0.0)


def _ssd_fwd(proj, dt_raw, yab, cw, cb, dtb, alog, de, gain, consts, hosted):
    t = proj.shape[0]
    nc = t // CHUNK
    ng = N_SSM_GROUPS
    tri_c, eh_c, e128_c = consts
    n_in = N_SSD_IN
    nh_in, nh_out = len(hosted['arrays']), len(hosted['out_shape'])
    mid_chunk = (7 * nc) // 8

    def body(*refs):
        ins = refs[:n_in]
        hins = refs[n_in + 1:n_in + 1 + nh_in]
        o0 = n_in + 1 + nh_in
        yb_all, ys_all, hs_all = refs[o0:o0 + 3]
        houts = refs[o0 + 3:o0 + 3 + nh_out]
        h_sc = refs[o0 + 3 + nh_out]
        scr_a, scr_b = _ssd_two_sets(refs[o0 + 4 + nh_out:-2])
        sems = refs[-2:]
        c = pl.program_id(0)
        _host_phase(hosted, 'start', c == 0, hins, houts, sems)
        _host_phase(hosted, 'mid', c == mid_chunk, hins, houts, sems)

        @pl.when(c == 0)
        def _():
            h_sc[...] = jnp.zeros_like(h_sc)

        def group(g, scr):
            grefs = _ssd_group_refs(ins, g)
            z_ref, de_ref, gain_ref = grefs[0], grefs[16], grefs[17]
            yb_ref, ys_ref = _lanes(yb_all, g * GW, GW), _lanes(ys_all, g * GW, GW)
            slab = pl.ds(pl.multiple_of(g * D_STATE, D_STATE), D_STATE)
            p = _ssd_pre(c == 0, g, grefs, scr)
            tri = _tri_mask()
            h_in = h_sc[slab, :]
            hs_all[slab, :] = h_in
            yoff = _dot(p['cm_b'], h_in.astype(BF16)) * jnp.exp(p['acs_e'])
            states = _dot_tn(p['bm_b'], (p['xdt'] * p['decay_e']).astype(BF16))
            lane = lax.broadcasted_iota(jnp.int32, (CHUNK, 128), 1)
            slabs = []
            for r2 in range(HEADS_PER_GROUP // 2):
                xb = p['xdt'][:, r2 * 128:(r2 + 1) * 128].astype(BF16)
                ya = _dot((p['scores'] * _ssd_l(p, g, 2 * r2, scr[3], tri)).astype(BF16), xb)
                yb = _dot((p['scores'] * _ssd_l(p, g, 2 * r2 + 1, scr[3], tri)).astype(BF16), xb)
                slabs.append(jnp.where(lane < HEAD_DIM, ya, yb))
            y = jnp.concatenate(slabs, axis=1) + yoff + de_ref[...] * p['xs']
            ys_ref[...] = y
            h_sc[slab, :] = jnp.exp(p['acs_last_e']) * h_in + states
            zv = z_ref[...]
            yg = y * zv * _sigmoid(zv)
            r = lax.rsqrt(jnp.mean(yg * yg, axis=-1, keepdims=True) + EPS)
            yb_ref[...] = (yg * r * gain_ref[...]).astype(BF16)

        def pair(j, carry):
            group(2 * j, scr_a)
            group(2 * j + 1, scr_b)
            return carry

        _ssd_chunk(ins, scr_a)
        lax.fori_loop(0, ng // 2, pair, 0)
        _host_phase(hosted, 'finish', c == nc - 1, hins, houts, sems)

    h_in, h_out, h_shape, h_scratch, h_alias = _host_plumbing(hosted, n_in + 1, 3)
    in_specs = _ssd_in_specs(lambda i: i) + [pl.BlockSpec(memory_space=pl.ANY)] + h_in
    out_specs = [pl.BlockSpec((CHUNK, D_SSM), lambda i: (i, D_A // D_SSM)),
                 pl.BlockSpec((CHUNK, D_SSM), lambda i: (i, 0)),
                 pl.BlockSpec((None, ng * D_STATE, GW), lambda i: (i, 0, 0))] + h_out
    out_shape = [jax.ShapeDtypeStruct((t, D_MIX), BF16), jax.ShapeDtypeStruct((t, D_SSM), F32),
                 jax.ShapeDtypeStruct((nc, ng * D_STATE, GW), F32)] + h_shape
    return pl.pallas_call(
        body, name="ssd_fwd", grid=(nc,), in_specs=in_specs, out_specs=out_specs, out_shape=out_shape,
        scratch_shapes=[pltpu.VMEM((ng * D_STATE, GW), F32)] + _ssd_scratch() + _ssd_scratch()[:N_SSD_OWN]
        + h_scratch,
        input_output_aliases={n_in: 0, **h_alias},
        compiler_params=_cparams(("arbitrary",)),
    )(proj, proj, proj, proj, proj, dt_raw, cw, cb, dtb, alog, de, gain,
      tri_c, eh_c, e128_c, yab, *hosted['arrays'])


def _rows8(vals, width):
    rid = lax.broadcasted_iota(jnp.int32, (8, width), 0)
    out = jnp.zeros((8, width), F32)
    for k, v in enumerate(vals):
        if v is not None:
            out = out + jnp.where(rid == k, v, 0.0)
    return out


def _ssd_bwd(proj, dt_raw, dyab, ysave, hs, duv, cw, cb, dtb, alog, de, gain, consts, hosted):
    t = proj.shape[0]
    nc = t // CHUNK
    ng = N_SSM_GROUPS
    tri_c, eh_c, e128_c = consts
    n_in = N_SSD_IN
    nh_in, nh_out = len(hosted['arrays']), len(hosted['out_shape'])

    def per_group(g, c, ins, dy_ref, ys_ref, hs_ref, outs, scratch):
        dz_ref, dxs_ref, db_ref, dc_ref, ddt_ref, acc_x, acc_b, acc_c, acc_gain, acc_head = outs
        dh_sc, car_x, car_b, car_c, dext_x, dext_b, dext_c = scratch[:7]
        scr = scratch[7:]
        ext_x, ext_b, ext_c, acst_sc = scr[:4]
        z_ref, cwx, cwb, cwc, de_ref, gain_ref = ins[0], ins[8], ins[9], ins[10], ins[16], ins[17]
        slab = pl.ds(pl.multiple_of(g * D_STATE, D_STATE), D_STATE)
        p = _ssd_pre(c == 0, g, ins, scr)
        tri = _tri_mask()
        xs, dt_e, acs_e, xdt, decay_e = p['xs'], p['dt_e'], p['acs_e'], p['xdt'], p['decay_e']
        cm_b, bm_b, scores, eh = p['cm_b'], p['bm_b'], p['scores'], p['eh']
        h_in = hs_ref[...]
        h_in_b = h_in.astype(BF16)
        e_a = jnp.exp(acs_e)
        raw = _dot(cm_b, h_in_b)

        y = ys_ref[...]
        zv = z_ref[...]
        sz, dsz = _silu_and_grad(zv)
        yg = y * sz
        r = lax.rsqrt(jnp.mean(yg * yg, axis=-1, keepdims=True) + EPS)
        xh = yg * r
        dout = dy_ref[...]
        gain = gain_ref[...]
        dxh = dout * gain
        dyg = r * (dxh - xh * jnp.mean(dxh * xh, axis=-1, keepdims=True))
        dy = dyg * sz
        dz_ref[...] = (dyg * y * dsz).astype(BF16)
        acc_gain[g] += _rows8([jnp.sum(dout * xh, axis=0, keepdims=True)], GW)
        d_skip8 = _x01_nt(_rows8([None, None, jnp.sum(dy * xs, axis=0, keepdims=True)], GW), eh)
        dxs = de_ref[...] * dy

        q = dy * raw * e_a
        draw = (dy * e_a).astype(BF16)
        d_c = _dot_nt(draw, h_in_b)
        dh_in = _dot_tn(cm_b, draw)

        lane = lax.broadcasted_iota(jnp.int32, (CHUNK, 128), 1)
        ones_b = jnp.ones((CHUNK, 128), BF16)
        dscores = jnp.zeros((CHUNK, CHUNK), F32)
        dxdt_slabs, q_slabs = [], []
        for r2 in range(HEADS_PER_GROUP // 2):
            sl = slice(r2 * 128, (r2 + 1) * 128)
            xb = xdt[:, sl].astype(BF16)
            dys = dy[:, sl]
            dys_b = dys.astype(BF16)
            dxh_pair, qv_pair = [], []
            for half in range(2):
                lmat = _ssd_l(p, g, 2 * r2 + half, acst_sc, tri)
                m = scores * lmat
                mine = (lane < HEAD_DIM) if half == 0 else (lane >= HEAD_DIM)
                dm = _dot_nt(jnp.where(mine, dys, 0.0).astype(BF16), xb)
                dscores = dscores + dm * lmat
                gm = dm * m
                dxh_pair.append(_dot_tn(m.astype(BF16), dys_b))
                h3 = _split3(gm)
                colsum = _dot_tn(h3[0], ones_b) + _dot_tn(h3[1], ones_b) + _dot_tn(h3[2], ones_b)
                qv_pair.append(jnp.sum(gm, axis=-1, keepdims=True) - colsum)
            dxdt_slabs.append(jnp.where(lane < HEAD_DIM, dxh_pair[0], dxh_pair[1]))
            q_slabs.append(jnp.where(lane == 0, qv_pair[0], 0.0) + jnp.where(lane == HEAD_DIM, qv_pair[1], 0.0))
        dxdt = jnp.concatenate(dxdt_slabs, axis=1)
        q = q + jnp.concatenate(q_slabs, axis=1)

        dh_out = dh_sc[slab, :]
        dh_out_b = dh_out.astype(BF16)
        e_l = jnp.exp(p['acs_last_e'])
        dh_sc[slab, :] = dh_in + e_l * dh_out
        dlast = jnp.sum(dh_out * h_in, axis=0, keepdims=True) * e_l
        dxd = _dot(bm_b, dh_out_b)
        xd = xdt * decay_e
        dxdt = dxdt + dxd * decay_e
        dd = dxd * xd
        q = q - dd
        dlast = dlast + jnp.sum(dd, axis=0, keepdims=True)
        d_b = _dot_nt(xd.astype(BF16), dh_out_b)
        dsc_b = dscores.astype(BF16)
        d_c = d_c + _dot(dsc_b, bm_b)
        d_b = d_b + _dot_tn(dsc_b, cm_b)

        dxs = dxs + dxdt * dt_e
        rid = lax.broadcasted_iota(jnp.int32, (CHUNK, GW), 0)
        q = q + jnp.where(rid == CHUNK - 1, dlast, 0.0)
        dacs = _x01_nt(q, eh)
        ddt = _x01_nt(dxdt * xs, eh)
        dadt = _e01x_tn(p['tri_b'], dacs)
        ddt = ddt + dadt * p['a']
        d_a = jnp.sum(dadt * p['dt'], axis=0, keepdims=True)
        ddt_raw = ddt * _sigmoid(p['dt_in'])
        acc_head[...] += _rows8([jnp.sum(ddt_raw, axis=0, keepdims=True), d_a * p['a']], HPAD) + d_skip8

        @pl.when(g == 0)
        def _():
            ddt_ref[...] = ddt_raw

        @pl.when(g > 0)
        def _():
            ddt_ref[...] += ddt_raw

        for dv, dsil, ext, dext, car, acc, w_ref, o_ref in (
                (dxs, p['dsx'], ext_x, dext_x, car_x, acc_x, cwx, dxs_ref),
                (d_b, p['dsb'], ext_b, dext_b, car_b, acc_b, cwb, db_ref),
                (d_c, p['dsc'], ext_c, dext_c, car_c, acc_c, cwc, dc_ref)):
            dp = dv * dsil
            width = dp.shape[1]
            rows = [jnp.sum(ext[pl.ds(5 + k, CHUNK), :] * dp, axis=0, keepdims=True) for k in range(SSM_CONV)]
            rows.append(jnp.sum(dp, axis=0, keepdims=True))
            acc[g] += _rows8(rows, width)
            dext[0:CHUNK, :] = dp
            dext[CHUNK:CHUNK + 8, :] = car[g]
            car[g] = dext[0:8, :]
            dx = w_ref[SSM_CONV - 1:SSM_CONV, :] * dext[pl.ds(0, CHUNK), :]
            for k in range(SSM_CONV - 1):
                dx = dx + w_ref[k:k + 1, :] * dext[pl.ds(SSM_CONV - 1 - k, CHUNK), :]
            o_ref[...] = dx.astype(BF16)

    def body(*refs):
        ins = refs[:n_in]
        dy_all, ys_all, hs_all, duv_ref = refs[n_in:n_in + 4]
        hins = refs[n_in + 4:n_in + 4 + nh_in]
        o0 = n_in + 4 + nh_in
        dproj_ref, ddt_ref = refs[o0:o0 + 2]
        accs = refs[o0 + 2:o0 + 7]
        houts = refs[o0 + 7:o0 + 7 + nh_out]
        scratch = refs[o0 + 7 + nh_out:-2]
        sems = refs[-2:]
        cc = pl.program_id(0)
        _host_phase(hosted, 'start', cc == 0, hins, houts, sems)
        dproj_ref[:, 0:2 * D_A] = duv_ref[...]

        @pl.when(cc == 0)
        def _():
            for a in tuple(accs) + tuple(scratch[:4]):
                a[...] = jnp.zeros_like(a)

        shared = tuple(scratch[:4])
        scr_a, scr_b = _ssd_two_sets(tuple(scratch[7:14]) + tuple(scratch[17:17 + N_SSD_OWN]))
        set_a = shared + tuple(scratch[4:7]) + scr_a
        set_b = shared + tuple(scratch[14:17]) + scr_b

        def group(g, own):
            slab = pl.ds(pl.multiple_of(g * D_STATE, D_STATE), D_STATE)
            outs = (_lanes(dproj_ref, OFF_Z + g * GW, GW), _lanes(dproj_ref, OFF_XS + g * GW, GW),
                    _lanes(dproj_ref, OFF_B + g * D_STATE, D_STATE), _lanes(dproj_ref, OFF_C + g * D_STATE, D_STATE),
                    ddt_ref) + tuple(accs)
            per_group(g, nc - 1 - cc, _ssd_group_refs(ins, g), _lanes(dy_all, g * GW, GW),
                      _lanes(ys_all, g * GW, GW), hs_all.at[slab, :], outs, own)

        def pair(j, carry):
            group(2 * j, set_a)
            group(2 * j + 1, set_b)
            return carry

        _ssd_chunk(ins, scr_a)
        lax.fori_loop(0, ng // 2, pair, 0)
        _host_phase(hosted, 'finish', cc == nc - 1, hins, houts, sems)

    def cmap(i):
        return nc - 1 - i

    in_specs = _ssd_in_specs(cmap) + [
        pl.BlockSpec((CHUNK, D_SSM), lambda i: (cmap(i), D_A // D_SSM)),
        pl.BlockSpec((CHUNK, D_SSM), lambda i: (cmap(i), 0)),
        pl.BlockSpec((None, ng * D_STATE, GW), lambda i: (cmap(i), 0, 0)),
        pl.BlockSpec((CHUNK, 2 * D_A), lambda i: (cmap(i), 0)),
    ]

    def full(shape):
        return pl.BlockSpec(shape, lambda i: (0,) * len(shape))

    out_specs = [
        pl.BlockSpec((CHUNK, D_MAIN), lambda i: (cmap(i), 0)),
        pl.BlockSpec((CHUNK, HPAD), lambda i: (cmap(i), 0)),
        full((ng, 8, GW)), full((ng, 8, D_STATE)), full((ng, 8, D_STATE)),
        full((ng, 8, GW)), full((8, HPAD)),
    ]
    out_shape = [
        jax.ShapeDtypeStruct((t, D_MAIN), BF16),
        jax.ShapeDtypeStruct((t, HPAD), F32),
        jax.ShapeDtypeStruct((ng, 8, GW), F32), jax.ShapeDtypeStruct((ng, 8, D_STATE), F32),
        jax.ShapeDtypeStruct((ng, 8, D_STATE), F32), jax.ShapeDtypeStruct((ng, 8, GW), F32),
        jax.ShapeDtypeStruct((8, HPAD), F32),
    ]
    scratch = [pltpu.VMEM((ng * D_STATE, GW), F32),
               pltpu.VMEM((ng, 8, GW), F32), pltpu.VMEM((ng, 8, D_STATE), F32), pltpu.VMEM((ng, 8, D_STATE), F32),
               pltpu.VMEM((CHUNK + 8, GW), F32), pltpu.VMEM((CHUNK + 8, D_STATE), F32),
               pltpu.VMEM((CHUNK + 8, D_STATE), F32)] + _ssd_scratch()
    scratch += scratch[4:7] + _ssd_scratch()[:N_SSD_OWN]
    h_in, h_out, h_shape, h_scratch, h_alias = _host_plumbing(hosted, n_in + 4, len(out_shape))
    return pl.pallas_call(
        body, name="ssd_bwd", grid=(nc,), in_specs=in_specs + h_in, out_specs=out_specs + h_out,
        out_shape=out_shape + h_shape, scratch_shapes=scratch + h_scratch, input_output_aliases=h_alias,
        compiler_params=_cparams(("arbitrary",)),
    )(proj, proj, proj, proj, proj, dt_raw, cw, cb, dtb, alog, de, gain,
      tri_c, eh_c, e128_c, dyab, ysave, hs, duv, *hosted['arrays'])


ANY = pl.BlockSpec(memory_space=pl.ANY)


def _place():
    x, y, c = lax.axis_index("x"), lax.axis_index("y"), lax.axis_index("c")
    chips = [(1 - x, y), (x, 1 - y), (1 - x, 1 - y)]
    return x, y, c, chips


def _rcopy(src, dst, send_sems, recv_sems, k, dev):
    return pltpu.make_async_remote_copy(src_ref=src, dst_ref=dst, send_sem=send_sems.at[k],
                                        recv_sem=recv_sems.at[k], device_id=dev, device_id_type=MESH)


def _my_chip():
    return 2 * lax.axis_index("x") + lax.axis_index("y")


def _cast_into_slot(w, name):
    r, c = w.shape
    tr = _row_tile(r, c, 2)

    def body(w_ref, o_ref):
        o_ref[...] = w_ref[...].astype(BF16)

    return pl.pallas_call(
        body, name=name, grid=(r // tr,), in_specs=[pl.BlockSpec((tr, c), lambda i: (i, 0))],
        out_specs=pl.BlockSpec((None, tr, c), lambda i: (_my_chip(), i, 0)),
        out_shape=jax.ShapeDtypeStruct((N_CHIPS, r, c), BF16), compiler_params=_cparams(("parallel",)),
    )(w)


def _hosted_gather(bigs):
    nb = len(bigs)

    def rows(a, c):
        half = bigs[a].shape[1] // 2
        return pl.ds(c * half, half)

    def start(ins, outs, send_sems, recv_sems):
        x, y, c, chips = _place()
        q = 2 * x + y
        for a in range(nb):
            for k, chip in enumerate(chips):
                _rcopy(outs[a].at[q, rows(a, c)], outs[a].at[q, rows(a, c)], send_sems, recv_sems, 6 * a + k,
                       (chip[0], chip[1], c)).start()

    def mid(ins, outs, send_sems, recv_sems):
        x, y, c, chips = _place()
        sib = (x, y, 1 - c)
        for a in range(nb):
            for k, chip in enumerate(chips):
                slab = outs[a].at[2 * chip[0] + chip[1], rows(a, c)]
                _rcopy(slab, slab, send_sems, recv_sems, 6 * a + k, sib).wait_recv()
                _rcopy(slab, slab, send_sems, recv_sems, 6 * a + 3 + k, sib).start()

    def finish(ins, outs, send_sems, recv_sems):
        x, y, c, chips = _place()
        q = 2 * x + y
        sib = (x, y, 1 - c)
        for a in range(nb):
            for k, chip in enumerate(chips):
                qk = 2 * chip[0] + chip[1]
                other = outs[a].at[qk, rows(a, 1 - c)]
                _rcopy(other, other, send_sems, recv_sems, 6 * a + 3 + k, sib).wait_recv()
                mine = outs[a].at[q, rows(a, c)]
                _rcopy(mine, mine, send_sems, recv_sems, 6 * a + k, sib).wait_send()
                fwd = outs[a].at[qk, rows(a, c)]
                _rcopy(fwd, fwd, send_sems, recv_sems, 6 * a + 3 + k, sib).wait_send()

    return dict(arrays=list(bigs), out_shape=[jax.ShapeDtypeStruct(b.shape, b.dtype) for b in bigs],
                aliases={a: a for a in range(nb)}, nsem=6 * nb, start=start, mid=mid, finish=finish)


def _hosted_rs_chips(ps):
    na = len(ps)

    def copies(ins, outs, send_sems, recv_sems):
        x, y, c, chips = _place()
        return [_rcopy(ins[a].at[2 * chip[0] + chip[1]], outs[a].at[k], send_sems, recv_sems, 3 * a + k,
                       (chip[0], chip[1], c)) for a in range(na) for k, chip in enumerate(chips)]

    def start(ins, outs, send_sems, recv_sems):
        for cp in copies(ins, outs, send_sems, recv_sems):
            cp.start()

    def finish(ins, outs, send_sems, recv_sems):
        for cp in copies(ins, outs, send_sems, recv_sems):
            cp.wait()

    return dict(arrays=list(ps), out_shape=[jax.ShapeDtypeStruct((3,) + p.shape[1:], p.dtype) for p in ps],
                aliases={}, nsem=3 * na, start=start, mid=None, finish=finish)


def _hosted_rs_sibling(gs):
    na = len(gs)

    def copies(ins, outs, send_sems, recv_sems):
        x, y, c, _ = _place()
        halves = [g.shape[1] // 2 for g in gs]
        return [_rcopy(ins[a].at[:, pl.ds((1 - c) * halves[a], halves[a]), :], outs[a], send_sems, recv_sems, a,
                       (x, y, 1 - c)) for a in range(na)]

    def start(ins, outs, send_sems, recv_sems):
        for cp in copies(ins, outs, send_sems, recv_sems):
            cp.start()

    def finish(ins, outs, send_sems, recv_sems):
        for cp in copies(ins, outs, send_sems, recv_sems):
            cp.wait()

    return dict(arrays=list(gs), aliases={}, nsem=na, start=start, mid=None, finish=finish,
                out_shape=[jax.ShapeDtypeStruct((N_CHIPS, g.shape[1] // 2, g.shape[2]), g.dtype) for g in gs])


def _hosted_share_sibling(fs):
    na = len(fs)

    def rows(a, c):
        half = fs[a].shape[0] // 2
        return pl.ds(c * half, half)

    def start(ins, outs, send_sems, recv_sems):
        x, y, c, _ = _place()
        for a in range(na):
            _rcopy(outs[a].at[rows(a, c)], outs[a].at[rows(a, c)], send_sems, recv_sems, a, (x, y, 1 - c)).start()

    def finish(ins, outs, send_sems, recv_sems):
        x, y, c, _ = _place()
        for a in range(na):
            _rcopy(outs[a].at[rows(a, c)], outs[a].at[rows(a, c)], send_sems, recv_sems, a, (x, y, 1 - c)).wait_send()
            other = outs[a].at[rows(a, 1 - c)]
            _rcopy(other, other, send_sems, recv_sems, a, (x, y, 1 - c)).wait_recv()

    return dict(arrays=list(fs), out_shape=[jax.ShapeDtypeStruct(f.shape, f.dtype) for f in fs],
                aliases={a: a for a in range(na)}, nsem=na, start=start, mid=None, finish=finish)


def _hosted_allgather(buf):
    def copies(ins, outs, send_sems, recv_sems):
        x, y, c, _ = _place()
        me = 4 * x + 2 * y + c
        cps = []
        for k in range(1, 8):
            dev = (1 - x if k & 4 else x, 1 - y if k & 2 else y, 1 - c if k & 1 else c)
            cps.append(_rcopy(ins[0], outs[0].at[me], send_sems, recv_sems, k - 1, dev))
        return pltpu.make_async_copy(ins[0], outs[0].at[me], send_sems.at[7]), cps

    def start(ins, outs, send_sems, recv_sems):
        loc, cps = copies(ins, outs, send_sems, recv_sems)
        loc.start()
        for cp in cps:
            cp.start()

    def finish(ins, outs, send_sems, recv_sems):
        loc, cps = copies(ins, outs, send_sems, recv_sems)
        loc.wait()
        for cp in cps:
            cp.wait()

    return dict(arrays=[buf], out_shape=[jax.ShapeDtypeStruct((8,) + buf.shape, buf.dtype)], aliases={},
                nsem=8, start=start, mid=None, finish=finish)


class _SemWindow:
    def __init__(self, sems, off):
        self._sems, self._off = sems, off

    @property
    def at(self):
        return self

    def __getitem__(self, k):
        return self._sems.at[k + self._off]


def _hosted_join(parts):
    arrays, out_shape, aliases, spans, nsem = [], [], {}, [], 0
    for h in parts:
        spans.append((len(arrays), len(h['arrays']), len(out_shape), len(h['out_shape']), nsem))
        aliases.update({len(arrays) + a: len(out_shape) + b for a, b in h['aliases'].items()})
        arrays += h['arrays']
        out_shape += h['out_shape']
        nsem += h['nsem']

    def phase(name):
        if all(h[name] is None for h in parts):
            return None

        def run(ins, outs, send_sems, recv_sems):
            for h, (ia, na, io, no, s0) in zip(parts, spans):
                if h[name] is not None:
                    h[name](ins[ia:ia + na], outs[io:io + no], _SemWindow(send_sems, s0), _SemWindow(recv_sems, s0))

        return run

    return dict(arrays=arrays, out_shape=out_shape, aliases=aliases, nsem=nsem,
                start=phase('start'), mid=phase('mid'), finish=phase('finish'))


def _run_hosted(hosted, name):
    nh_in, nh_out = len(hosted['arrays']), len(hosted['out_shape'])

    def body(*refs):
        ins, outs, sems = refs[:nh_in], refs[nh_in:nh_in + nh_out], refs[-2:]
        for ph in ('start', 'mid', 'finish'):
            if hosted[ph] is not None:
                hosted[ph](ins, outs, sems[0], sems[1])

    h_in, h_out, h_shape, h_scratch, h_alias = _host_plumbing(hosted, 0, 0)
    return pl.pallas_call(body, name=name, in_specs=h_in, out_specs=h_out, out_shape=h_shape,
                          scratch_shapes=h_scratch, input_output_aliases=h_alias)(*hosted['arrays'])


def _host_plumbing(hosted, n_in, n_out):
    nh = len(hosted['arrays'])
    return ([ANY] * nh, [ANY] * len(hosted['out_shape']), list(hosted['out_shape']),
            [pltpu.SemaphoreType.DMA((hosted['nsem'],)), pltpu.SemaphoreType.DMA((hosted['nsem'],))],
            {n_in + a: n_out + b for a, b in hosted['aliases'].items()})


def _host_phase(hosted, phase, when, hins, houts, sems):
    fn = hosted[phase]
    if fn is None:
        return

    @pl.when(when)
    def _():
        fn(hins, houts, sems[0], sems[1])


def _gather_weights(bigs, smalls):
    nb, ns = len(bigs), len(smalls)
    na = nb + ns
    nsem = 6 * nb + 3 * ns
    big = _hosted_gather(bigs)

    def body(*refs):
        ins, outs = refs[:na], refs[na:2 * na]
        send_sems, recv_sems, loc_sems = refs[2 * na:]
        x, y, c, chips = _place()
        q = 2 * x + y
        sib = (x, y, 1 - c)
        locs, sends = [], []
        for s in range(ns):
            cp = pltpu.make_async_copy(ins[nb + s], outs[nb + s].at[q], loc_sems.at[s])
            cp.start()
            locs.append(cp)
        big['start'](ins[:nb], outs[:nb], send_sems, recv_sems)
        for s in range(ns):
            a = nb + s
            for k, chip in enumerate(chips):
                cp = _rcopy(ins[a], outs[a].at[q], send_sems, recv_sems, 6 * nb + 3 * s + k,
                            (chip[0], chip[1], c))
                cp.start()
                sends.append(cp)
        big['mid'](ins[:nb], outs[:nb], send_sems, recv_sems)
        big['finish'](ins[:nb], outs[:nb], send_sems, recv_sems)
        for s in range(ns):
            a = nb + s
            for k, chip in enumerate(chips):
                qk = 2 * chip[0] + chip[1]
                _rcopy(ins[a], outs[a].at[qk], send_sems, recv_sems, 6 * nb + 3 * s + k, sib).wait_recv()
        for cp in sends:
            cp.wait_send()
        for cp in locs:
            cp.wait()

    arrs = list(bigs) + list(smalls)
    out_shape = ([jax.ShapeDtypeStruct(a.shape, a.dtype) for a in bigs]
                 + [jax.ShapeDtypeStruct((N_CHIPS,) + a.shape, a.dtype) for a in smalls])
    return pl.pallas_call(
        body, name="gather_weights", in_specs=[ANY] * na, out_specs=[ANY] * na, out_shape=out_shape,
        input_output_aliases={a: a for a in range(nb)},
        scratch_shapes=[pltpu.SemaphoreType.DMA((nsem,)), pltpu.SemaphoreType.DMA((nsem,)),
                        pltpu.SemaphoreType.DMA((max(ns, 1),))],
    )(*arrs)


def _row_tile(rows, cols, nbuf):
    budget = 24 * 1024 * 1024 // (nbuf * cols * 4 * 2)
    return _tile(rows, max(16, budget - budget % 16), 16) if rows % 16 == 0 else rows


def _add_pairs(g, rcv, name):
    s, half, c = rcv.shape
    tr = _row_tile(half, c, 3)
    nh = half // tr

    def body(a_ref, b_ref, o_ref):
        o_ref[...] = (a_ref[...].astype(F32) + b_ref[...].astype(F32)).astype(o_ref.dtype)

    blk = pl.BlockSpec((None, tr, c), lambda j, i: (j, i, 0))
    mine = pl.BlockSpec((None, tr, c), lambda j, i: (j, lax.axis_index("c") * nh + i, 0))
    return pl.pallas_call(
        body, name=name, grid=(s, nh), in_specs=[mine, blk], out_specs=blk,
        out_shape=jax.ShapeDtypeStruct(rcv.shape, rcv.dtype), compiler_params=_cparams(("parallel", "parallel")),
    )(g, rcv)


def _sum_chips(part, rcv, name):
    _, half, c = part.shape
    tr = _row_tile(half, c, 5)
    nh = half // tr

    def body(o_ref, r_ref, out_ref):
        acc = o_ref[...].astype(F32)
        for k in range(3):
            acc = acc + r_ref[k].astype(F32)
        out_ref[...] = acc

    return pl.pallas_call(
        body, name=name, grid=(nh,),
        in_specs=[pl.BlockSpec((None, tr, c), lambda i: (_my_chip(), i, 0)),
                  pl.BlockSpec((3, tr, c), lambda i: (0, i, 0))],
        out_specs=pl.BlockSpec((tr, c), lambda i: (lax.axis_index("c") * nh + i, 0)),
        out_shape=jax.ShapeDtypeStruct((2 * half, c), F32), compiler_params=_cparams(("parallel",)),
    )(part, rcv)


def _sum_devices(parts, name):
    _, n, _ = parts.shape
    tr = n if n <= 4096 else _tile(n, 512, 8)

    def body(p_ref, o_ref):
        acc = p_ref[0]
        for k in range(1, 8):
            acc = acc + p_ref[k]
        o_ref[...] = acc

    return pl.pallas_call(
        body, name=name, grid=(n // tr,),
        in_specs=[pl.BlockSpec((8, tr, 128), lambda i: (0, i, 0))],
        out_specs=pl.BlockSpec((tr, 128), lambda i: (i, 0)),
        out_shape=jax.ShapeDtypeStruct((n, 128), F32), compiler_params=_cparams(("parallel",)),
    )(parts)


def _adamw(w, g, m, v, name):
    r, c = w.shape
    tr = _row_tile(r, c, 7)
    c1 = 1.0 - ADAM_B1 ** ADAM_STEP
    c2 = 1.0 - ADAM_B2 ** ADAM_STEP

    def body(w_ref, g_ref, m_ref, v_ref, d_ref, mo_ref, vo_ref):
        gv = g_ref[...]
        mn = ADAM_B1 * m_ref[...] + (1.0 - ADAM_B1) * gv
        vn = ADAM_B2 * v_ref[...] + (1.0 - ADAM_B2) * (gv * gv)
        mo_ref[...] = mn
        vo_ref[...] = vn
        m_hat = mn / c1
        v_hat = vn / c2
        d_ref[...] = -ADAM_LR * (m_hat / (jnp.sqrt(v_hat) + ADAM_EPS) + ADAM_WD * w_ref[...])

    blk = pl.BlockSpec((tr, c), lambda i: (i, 0))
    sh = jax.ShapeDtypeStruct((r, c), F32)
    return pl.pallas_call(
        body, name=name, grid=(r // tr,), in_specs=[blk] * 4, out_specs=[blk] * 3, out_shape=[sh] * 3,
        compiler_params=_cparams(("parallel",)),
    )(w, g, m, v)


WEIGHTS = ['norm_mix_g', 'w_in', 'ln_a_g', 'ln_a_b', 'w_s', 'b_s', 'norm_a_g', 'conv_ssm_w', 'conv_ssm_b',
           'dt_bias', 'a_log', 'd_skip', 'ssm_norm_g', 'w_out', 'norm_ffn_g', 'w_up', 'conv_ffn_w',
           'conv_ffn_b', 'w_down', 'norm_ple_g', 'w_ple_gate', 'w_ple', 'norm_final_g']
BIG = ['w_in', 'w_out', 'w_up', 'w_down', 'w_ple_gate', 'w_ple']
SMALL = [n for n in WEIGHTS if n not in BIG]
PACK_ALIGN = 2048


def _pack(arrs):
    parts = []
    for a in arrs:
        f = a.reshape(-1).astype(F32)
        parts.append(jnp.pad(f, (0, (-f.shape[0]) % PACK_ALIGN)))
    return jnp.concatenate(parts).reshape(-1, 128)


def _unpack(buf, shapes):
    flat = buf.reshape(-1)
    out, off = [], 0
    for s in shapes:
        n = math.prod(s)
        out.append(flat[off:off + n].reshape(s))
        off += n + (-n) % PACK_ALIGN
    return out


def _pad_heads(v):
    return jnp.pad(v, ((0, 0), (0, HPAD - v.shape[1])))


def _col_sharded(full):
    r, c4 = full.shape
    return jnp.transpose(full.reshape(r, N_CHIPS, c4 // N_CHIPS), (1, 0, 2))


def _from_col_sharded(g):
    s, r, c = g.shape
    return jnp.transpose(g, (1, 0, 2)).reshape(r, s * c)


def kernel(x, p, norm_mix_g, w_in, ln_a_g, ln_a_b, w_s, b_s, norm_a_g, conv_ssm_w, conv_ssm_b, dt_bias, a_log, d_skip, ssm_norm_g, w_out, norm_ffn_g, w_up, conv_ffn_w, conv_ffn_b, w_down, norm_ple_g, w_ple_gate, w_ple, norm_final_g, loss_target, m_norm_mix_g, m_w_in, m_ln_a_g, m_ln_a_b, m_w_s, m_b_s, m_norm_a_g, m_conv_ssm_w, m_conv_ssm_b, m_dt_bias, m_a_log, m_d_skip, m_ssm_norm_g, m_w_out, m_norm_ffn_g, m_w_up, m_conv_ffn_w, m_conv_ffn_b, m_w_down, m_norm_ple_g, m_w_ple_gate, m_w_ple, m_norm_final_g, v_norm_mix_g, v_w_in, v_ln_a_g, v_ln_a_b, v_w_s, v_b_s, v_norm_a_g, v_conv_ssm_w, v_conv_ssm_b, v_dt_bias, v_a_log, v_d_skip, v_ssm_norm_g, v_w_out, v_norm_ffn_g, v_w_up, v_conv_ffn_w, v_conv_ffn_b, v_w_down, v_norm_ple_g, v_w_ple_gate, v_w_ple, v_norm_final_g):
    given = dict(locals())
    wts = {n: given[n] for n in WEIGHTS}
    mom = {n: given['m_' + n] for n in WEIGHTS}
    var = {n: given['v_' + n] for n in WEIGHTS}
    d = D_MODEL
    xt, pt, tgt = x[0], p[0, 0], loss_target[0]
    chip = 2 * lax.axis_index("x") + lax.axis_index("y")

    slots = {n: _cast_into_slot(wts[n][0], "cast_" + n) for n in BIG}
    g_in, g_cs, g_cf = _gather_weights([slots['w_in']], [conv_ssm_w[0], conv_ffn_w[0]])
    later = [n for n in BIG if n != 'w_in']
    w_in_full = _from_col_sharded(g_in)
    w_main = w_in_full[:, :D_MAIN]
    w_dt = _pad_heads(w_in_full[:, D_MAIN:])
    cs_w = _from_col_sharded(g_cs)
    cf_w = _from_col_sharded(g_cf)
    consts = _ssd_consts()
    dtb, alog = _pad_heads(dt_bias), _pad_heads(a_log)
    de = jnp.repeat(d_skip[0], HEAD_DIM)[None, :]
    b_exp = jnp.broadcast_to(b_s[0][:, :, None], (N_GROUPS_A, CHUNK, CHUNK))

    a1 = _rms_fwd(xt, norm_mix_g, "rms_mix")
    proj = _matmul(a1, w_main, mode='nn', name="mm_proj", tm=1024, tn=1024)
    dt_raw = _matmul(a1, w_dt, mode='nn', name="mm_dt", tm=1024, tn=128)
    yab = _gmlp_fwd(proj, ln_a_g, ln_a_b, w_s[0], b_exp, norm_a_g)
    yab, ysave, hs, *gathered = _ssd_fwd(proj, dt_raw, yab, cs_w, conv_ssm_b, dtb, alog, de, ssm_norm_g, consts,
                                         _hosted_gather([slots[n] for n in later]))
    g_out, g_up, g_down, g_pg, g_ple = gathered
    w_out_f = g_out.reshape(D_MIX, d)
    w_down_f = g_down.reshape(D_FF, d)
    w_pg_f = g_pg.reshape(d, d)
    h1 = _matmul(yab, w_out_f, mode='nn', name="mm_out", res=xt, tm=512, tn=1024, tk=4096)
    f = _rms_fwd(h1, norm_ffn_g, "rms_ffn")
    hid = _matmul(f, g_up, mode='nn', name="mm_up", b_sharded=True, tm=1024, tn=1408)
    act, conv_g, conv_u = _ffn_act_fwd(hid, cf_w, conv_ffn_b)
    h2 = _matmul(act, w_down_f, mode='nn', name="mm_down", res=h1, tm=1024, tn=1024, tk=2816)
    n3 = _rms_fwd(h2, norm_ple_g, "rms_ple")
    gl = _matmul(n3, w_pg_f, mode='nn', name="mm_pg", tm=1024, tn=1024)
    pe = _matmul(pt, g_ple, mode='nn', name="mm_ple", b_sharded=True, tm=1024, tn=512)
    dh3, dgl, dpe, lossv, dgf = _tail(h2, gl, pe, tgt, norm_final_g[None, :])

    gs_ple = _matmul(pt, dpe, mode='tn', name="mm_dw_ple", out_dtype=BF16, out_shards=N_CHIPS,
                     tm=256, tn=512, tk=2048)
    gs_pg = _matmul(n3, dgl, mode='tn', name="mm_dw_pg", out_dtype=BF16, tm=1024, tn=1024, tk=2048)
    dn3 = _matmul(dgl, w_pg_f, mode='nt', name="mm_dn3", tm=1024, tn=1024)
    dh2, dg_ple, dh2_b = _rms_bwd(h2, norm_ple_g, dn3, dh3, "rms_ple_bwd", True)
    dact = _matmul(dh2_b, w_down_f, mode='nt', name="mm_dact", out_dtype=BF16, tm=1024, tn=1408)
    gs_down = _matmul(act, dh2_b, mode='tn', name="mm_dw_down", out_dtype=BF16, tm=1408, tn=1024, tk=2048)
    dpg, dpu, wg_acc, wu_acc = _ffn_act_bwd(hid, cf_w, conv_g, conv_u, dact)
    hc = N_CHIPS // 2
    gs_up = jnp.concatenate(
        [_matmul(f, dpg, mode='tn', name="mm_dw_up_g", out_dtype=BF16, out_shards=hc, tm=1024, tn=1408, tk=2048),
         _matmul(f, dpu, mode='tn', name="mm_dw_up_u", out_dtype=BF16, out_shards=hc, tm=1024, tn=1408, tk=2048)],
        axis=0)
    early = [gs_up, gs_down.reshape(N_CHIPS, D_FF // N_CHIPS, d), gs_pg.reshape(N_CHIPS, d // N_CHIPS, d), gs_ple]
    df, *sib_e = _matmul(dpg, g_up, mode='nt', name="mm_df_g", b_sharded=True, tm=1024, tn=1024, tk=2816,
                         hosted=_hosted_rs_sibling(early))
    part_e = [_add_pairs(a, b, "rs_add_e%d" % i) for i, (a, b) in enumerate(zip(early, sib_e))]
    df = _matmul(dpu, g_up, mode='nt', name="mm_df_u", b_sharded=True, b_shard_off=hc, res=df,
                 tm=1024, tn=1024, tk=2816)
    dh1, dg_ffn, dh1_b = _rms_bwd(h1, norm_ffn_g, df, dh2, "rms_ffn_bwd", True)
    dyab = _matmul(dh1_b, w_out_f, mode='nt', name="mm_dyab", tm=1024, tn=1024)
    gs_out = _matmul(yab, dh1_b, mode='tn', name="mm_dw_out", out_dtype=BF16, tm=1024, tn=1024, tk=4096)
    duv, dws, dbs, dlng, dlnb, dnag = _gmlp_bwd(proj, dyab, ln_a_g, ln_a_b, w_s[0], b_exp, norm_a_g)
    dproj, ddt_raw, acc_x, acc_b, acc_c, acc_gain, acc_head, *rcv_e = _ssd_bwd(
        proj, dt_raw, dyab, ysave, hs, duv, cs_w, conv_ssm_b, dtb, alog, de, ssm_norm_g, consts,
        _hosted_rs_chips(part_e))
    dw_main = _matmul(a1, dproj, mode='tn', name="mm_dw_main", out_dtype=BF16, tm=1024, tn=1024, tk=4096)
    dw_dt = _matmul(a1, ddt_raw, mode='tn', name="mm_dw_dt", out_dtype=BF16, tm=1024, tn=128, tk=2048)
    gs_in = _col_sharded(jnp.concatenate([dw_main, dw_dt[:, :N_HEADS]], axis=1))
    late = [gs_in, gs_out.reshape(N_CHIPS, D_MIX // N_CHIPS, d)]
    part_l = [_add_pairs(a, b, "rs_add_l%d" % i) for i, (a, b) in enumerate(
        zip(late, _run_hosted(_hosted_rs_sibling(late), "rs_sibling_late")))]

    def conv_rows(acc, k):
        return acc[:, k, :].reshape(1, -1)

    dcw = jnp.concatenate([jnp.concatenate([conv_rows(acc_x, k), conv_rows(acc_b, k), conv_rows(acc_c, k)], axis=1)
                           for k in range(SSM_CONV)], axis=0)
    dcb = jnp.concatenate([conv_rows(acc_x, SSM_CONV), conv_rows(acc_b, SSM_CONV), conv_rows(acc_c, SSM_CONV)], axis=1)
    part = {
        'ln_a_g': dlng, 'ln_a_b': dlnb, 'w_s': dws, 'b_s': dbs, 'norm_a_g': dnag,
        'conv_ssm_w': dcw, 'conv_ssm_b': dcb,
        'dt_bias': acc_head[0:1, :N_HEADS], 'a_log': acc_head[1:2, :N_HEADS], 'd_skip': acc_head[2:3, :N_HEADS],
        'ssm_norm_g': acc_gain[:, 0, :], 'norm_ffn_g': dg_ffn,
        'conv_ffn_w': jnp.concatenate([wg_acc[:FFN_CONV], wu_acc[:FFN_CONV]], axis=1),
        'conv_ffn_b': jnp.concatenate([wg_acc[FFN_CONV:FFN_CONV + 1], wu_acc[FFN_CONV:FFN_CONV + 1]], axis=1),
        'norm_ple_g': dg_ple, 'norm_final_g': dgf,
    }
    full_shapes = {n: wts[n].shape for n in SMALL}
    full_shapes['conv_ssm_w'] = (1, SSM_CONV, D_XBC)
    full_shapes['conv_ffn_w'] = (1, FFN_CONV, 2 * D_FF)
    small_e = [n for n in SMALL if n != 'norm_mix_g']
    packed = _pack([part[n] for n in small_e] + [lossv[:, 0:1]])

    halves_e = [_sum_chips(a, b, "rs_sum_e%d" % i) for i, (a, b) in enumerate(zip(part_e, rcv_e))]
    da_dt = _matmul(ddt_raw, w_dt, mode='nt', name="mm_da_dt", tm=1024, tn=1024)
    da, *moved = _matmul(dproj, w_main, mode='nt', name="mm_da", res=da_dt, tm=1024, tn=1024, tk=2560,
                         hosted=_hosted_join([_hosted_rs_chips(part_l), _hosted_share_sibling(halves_e),
                                              _hosted_allgather(packed)]))
    rcv_l, g_early, gathered = moved[:len(late)], moved[len(late):-1], moved[-1]
    dx, dg_mix = _rms_bwd(xt, norm_mix_g, da, dh1, "rms_mix_bwd", False)

    halves_l = [_sum_chips(a, b, "rs_sum_l%d" % i) for i, (a, b) in enumerate(zip(part_l, rcv_l))]
    g_big = dict(zip(['w_up', 'w_down', 'w_ple_gate', 'w_ple'], g_early))
    g_big.update(zip(['w_in', 'w_out'], _run_hosted(_hosted_share_sibling(halves_l), "share_sibling_late")))

    pieces = _unpack(_sum_devices(gathered, "sum_devices"), [full_shapes[n] for n in small_e] + [(1,)])
    g_small = dict(zip(small_e, pieces[:-1]))
    loss = pieces[-1][0]
    mix = _sum_devices(_run_hosted(_hosted_allgather(_pack([dg_mix])), "allgather_mix")[0], "sum_devices_mix")
    g_small['norm_mix_g'] = _unpack(mix, [full_shapes['norm_mix_g']])[0]
    for n in ('conv_ssm_w', 'conv_ffn_w'):
        width = wts[n].shape[2]
        g_small[n] = lax.dynamic_slice_in_dim(g_small[n], chip * width, width, axis=2)

    grads, delta, new_m, new_v = {}, {}, {}, {}
    for n in BIG:
        shp = wts[n].shape
        dl, mn, vn = _adamw(wts[n][0], g_big[n], mom[n][0], var[n][0], "adamw_" + n)
        grads[n], delta[n], new_m[n], new_v[n] = (g_big[n].reshape(shp), dl.reshape(shp), mn.reshape(shp),
                                                  vn.reshape(shp))
    shapes = [wts[n].shape for n in SMALL]
    dl, mn, vn = _adamw(_pack([wts[n] for n in SMALL]), _pack([g_small[n] for n in SMALL]),
                        _pack([mom[n] for n in SMALL]), _pack([var[n] for n in SMALL]), "adamw_small")
    for n, a, b, c in zip(SMALL, _unpack(dl, shapes), _unpack(mn, shapes), _unpack(vn, shapes)):
        grads[n], delta[n], new_m[n], new_v[n] = g_small[n], a, b, c

    return (loss, dx[None], *[grads[n] for n in WEIGHTS], *[delta[n] for n in WEIGHTS],
            *[new_m[n] for n in WEIGHTS], *[new_v[n] for n in WEIGHTS])
```

```python
import functools
import math

import jax
import jax.numpy as jnp
from jax import lax
from jax.experimental import pallas as pl
from jax.experimental.pallas import tpu as pltpu

D_MODEL = 2048
SEQ = 8192
D_MIX = 2 * D_MODEL
D_A = D_MIX // 2
CHUNK = 128
N_GROUPS_A = D_A // 128
D_SSM = D_MIX - D_A
HEAD_DIM = 64
N_HEADS = D_SSM // HEAD_DIM
HEADS_PER_GROUP = 4
N_SSM_GROUPS = N_HEADS // HEADS_PER_GROUP
GW = HEADS_PER_GROUP * HEAD_DIM
D_STATE = 128
SSM_CONV = 4
D_BC = N_SSM_GROUPS * D_STATE
D_XBC = D_SSM + 2 * D_BC
D_MAIN = 2 * D_A + D_SSM + D_XBC
D_IN = D_MAIN + N_HEADS
D_FF = (D_MODEL * 11) // 4
FFN_CONV = 3
D_PLE = 256
EPS = 1e-6
HPAD = 128
N_CHIPS = 4

ADAM_LR = 0.001
ADAM_B1 = 0.9
ADAM_B2 = 0.999
ADAM_EPS = 1e-08
ADAM_WD = 0.01
ADAM_STEP = 10

F32 = jnp.float32
BF16 = jnp.bfloat16
MESH = pl.DeviceIdType.MESH
VMEM_LIMIT = 56 * 1024 * 1024


def _cparams(sem):
    return pltpu.CompilerParams(dimension_semantics=sem, vmem_limit_bytes=VMEM_LIMIT)


def _tile(n, pref, mult):
    t = min(pref, n)
    t -= t % mult
    while n % t:
        t -= mult
    return t


def _dot(a, b):
    return jnp.dot(a, b, preferred_element_type=F32)


def _dot_nt(a, b):
    return lax.dot_general(a, b, (((1,), (1,)), ((), ())), preferred_element_type=F32)


def _dot_tn(a, b):
    return lax.dot_general(a, b, (((0,), (0,)), ((), ())), preferred_element_type=F32)


def _split3(x):
    hi = x.astype(BF16)
    r = x - hi.astype(F32)
    mid = r.astype(BF16)
    lo = (r - mid.astype(F32)).astype(BF16)
    return hi, mid, lo


def _x01(x, e):
    h, m, l = _split3(x)
    return _dot(h, e) + _dot(m, e) + _dot(l, e)


def _x01_nt(x, e):
    h, m, l = _split3(x)
    return _dot_nt(h, e) + _dot_nt(m, e) + _dot_nt(l, e)


def _e01x(e, x):
    h, m, l = _split3(x)
    return _dot(e, h) + _dot(e, m) + _dot(e, l)


def _e01x_tn(e, x):
    h, m, l = _split3(x)
    return _dot_tn(e, h) + _dot_tn(e, m) + _dot_tn(e, l)


def _sigmoid(x):
    return 1.0 / (1.0 + jnp.exp(-x))


_GELU_C = math.sqrt(2.0 / math.pi)


def _gelu_and_grad(x):
    x2 = x * x
    th = jnp.tanh(_GELU_C * (x + 0.044715 * x * x2))
    y = 0.5 * x * (1.0 + th)
    dy = 0.5 * (1.0 + th) + 0.5 * x * (1.0 - th * th) * (_GELU_C * (1.0 + 3.0 * 0.044715 * x2))
    return y, dy


def _silu_and_grad(x):
    s = _sigmoid(x)
    return x * s, s * (1.0 + x * (1.0 - s))


def _softplus(x):
    u = jnp.exp(-jnp.abs(x))
    w = 1.0 + u
    l1p = jnp.where(w == 1.0, u, jnp.log(w) * (u / (w - 1.0)))
    return jnp.maximum(x, 0.0) + l1p


def _matmul(a, b, *, mode, name, out_dtype=F32, res=None, tm=512, tn=512, tk=2048,
            b_sharded=False, b_shard_off=0, b_cols=None, out_shards=0, hosted=None):
    if mode == 'tn':
        kdim, m = a.shape
        n = b.shape[1]
    else:
        m, kdim = a.shape
        if b_sharded:
            s_b, d1, d2 = b.shape
            n = s_b * d2 if mode == 'nn' else d1
        else:
            n = b.shape[1] if mode == 'nn' else b.shape[0]
        if b_cols is not None:
            n = b_cols
    per = None
    if b_sharded:
        per = b.shape[2]
    if out_shards:
        per = n // out_shards
    tm = _tile(m, tm, 128 if mode == 'tn' else 8)
    if mode == 'nt' and b_sharded:
        tn = _tile(n, tn, 128)
        tk = _tile(per, tk, 128)
    elif per is not None:
        tn = _tile(per, tn, 128)
        tk = _tile(kdim, tk, 128 if mode != 'tn' else 8)
    else:
        tn = _tile(n, tn, 128)
        tk = _tile(kdim, tk, 128 if mode != 'tn' else 8)
    nm, nn_, nk = m // tm, n // tn, kdim // tk
    has_res = res is not None
    n_in = 2 + has_res
    nh_in = len(hosted['arrays']) if hosted else 0
    nh_out = len(hosted['out_shape']) if hosted else 0

    def body(*refs):
        a_ref, b_ref = refs[0], refs[1]
        res_ref = refs[2] if has_res else None
        o_ref = refs[n_in + nh_in]
        if hosted:
            hins = refs[n_in:n_in + nh_in]
            houts = refs[n_in + nh_in + 1:n_in + nh_in + 1 + nh_out]
            sems = refs[-2:]
            ids = [pl.program_id(d) for d in range(3)]
            at_first = jnp.logical_and(jnp.logical_and(ids[0] == 0, ids[1] == 0), ids[2] == 0)
            at_last = jnp.logical_and(jnp.logical_and(ids[0] == nn_ - 1, ids[1] == nm - 1), ids[2] == nk - 1)
            _host_phase(hosted, 'start', at_first, hins, houts, sems)
        av = a_ref[...].astype(BF16)
        bv = b_ref[...].astype(BF16)
        if mode == 'nn':
            p = _dot(av, bv)
        elif mode == 'nt':
            p = _dot_nt(av, bv)
        else:
            p = _dot_tn(av, bv)

        def fin(v):
            if has_res:
                v = v + res_ref[...]
            o_ref[...] = v.astype(o_ref.dtype)

        if nk == 1:
            fin(p)
        else:
            acc_ref = refs[n_in + nh_in + 1 + nh_out]
            k = pl.program_id(2)

            @pl.when(k == 0)
            def _():
                acc_ref[...] = p

            @pl.when(k > 0)
            def _():
                acc_ref[...] += p

            @pl.when(k == nk - 1)
            def _():
                fin(acc_ref[...])
        if hosted:
            _host_phase(hosted, 'mid', at_last, hins, houts, sems)
            _host_phase(hosted, 'finish', at_last, hins, houts, sems)

    if mode == 'nn':
        a_spec = pl.BlockSpec((tm, tk), lambda j, i, k: (i, k))
        if b_sharded:
            nps = per // tn
            b_spec = pl.BlockSpec((None, tk, tn), lambda j, i, k: (j // nps, k, j % nps))
        else:
            b_spec = pl.BlockSpec((tk, tn), lambda j, i, k: (k, j))
    elif mode == 'nt':
        a_spec = pl.BlockSpec((tm, tk), lambda j, i, k: (i, k))
        if b_sharded:
            kps = per // tk
            b_spec = pl.BlockSpec((None, tn, tk), lambda j, i, k: (k // kps + b_shard_off, j, k % kps))
        else:
            b_spec = pl.BlockSpec((tn, tk), lambda j, i, k: (j, k))
    else:
        a_spec = pl.BlockSpec((tk, tm), lambda j, i, k: (k, i))
        b_spec = pl.BlockSpec((tk, tn), lambda j, i, k: (k, j))
    in_specs = [a_spec, b_spec]
    args = [a, b]
    if has_res:
        in_specs.append(pl.BlockSpec((tm, tn), lambda j, i, k: (i, j)))
        args.append(res)
    if out_shards:
        nps_o = per // tn
        out_shape = jax.ShapeDtypeStruct((out_shards, m, per), out_dtype)
        out_spec = pl.BlockSpec((None, tm, tn), lambda j, i, k: (j // nps_o, i, j % nps_o))
    else:
        out_shape = jax.ShapeDtypeStruct((m, n), out_dtype)
        out_spec = pl.BlockSpec((tm, tn), lambda j, i, k: (i, j))
    scratch = [pltpu.VMEM((tm, tn), F32)] if nk > 1 else []
    if not hosted:
        return pl.pallas_call(
            body, name=name, grid=(nn_, nm, nk), in_specs=in_specs, out_specs=out_spec,
            out_shape=out_shape, scratch_shapes=scratch,
            compiler_params=_cparams(("parallel", "parallel", "arbitrary")),
        )(*args)
    h_in, h_out, h_shape, h_scratch, h_alias = _host_plumbing(hosted, n_in, 1)
    return pl.pallas_call(
        body, name=name, grid=(nn_, nm, nk), in_specs=in_specs + h_in, out_specs=[out_spec] + h_out,
        out_shape=[out_shape] + h_shape, scratch_shapes=scratch + h_scratch, input_output_aliases=h_alias,
        compiler_params=_cparams(("arbitrary", "arbitrary", "arbitrary")),
    )(*args, *hosted['arrays'])


def _rms_fwd(x, g, name):
    t, d = x.shape
    tt = _tile(t, 512, 8)

    def body(x_ref, g_ref, o_ref):
        xv = x_ref[...]
        r = lax.rsqrt(jnp.mean(xv * xv, axis=-1, keepdims=True) + EPS)
        o_ref[...] = (xv * r * g_ref[...]).astype(o_ref.dtype)

    return pl.pallas_call(
        body, name=name, grid=(t // tt,),
        in_specs=[pl.BlockSpec((tt, d), lambda i: (i, 0)), pl.BlockSpec((1, d), lambda i: (0, 0))],
        out_specs=pl.BlockSpec((tt, d), lambda i: (i, 0)),
        out_shape=jax.ShapeDtypeStruct((t, d), BF16),
        compiler_params=_cparams(("parallel",)),
    )(x, g)


def _rms_bwd(x, g, dy, dres, name, also_bf16):
    t, d = x.shape
    tt = _tile(t, 256, 16)

    def body(x_ref, g_ref, dy_ref, dres_ref, dx_ref, dg_ref, *dxb):
        i = pl.program_id(0)
        xv = x_ref[...]
        r = lax.rsqrt(jnp.mean(xv * xv, axis=-1, keepdims=True) + EPS)
        xh = xv * r
        dyv = dy_ref[...].astype(F32)
        dxh = dyv * g_ref[...]
        c = jnp.mean(dxh * xh, axis=-1, keepdims=True)
        dx = dres_ref[...] + r * (dxh - xh * c)
        dx_ref[...] = dx
        for dxb_ref in dxb:
            dxb_ref[...] = dx.astype(BF16)
        part = jnp.sum(dyv * xh, axis=0, keepdims=True)

        @pl.when(i == 0)
        def _():
            dg_ref[...] = part

        @pl.when(i > 0)
        def _():
            dg_ref[...] += part

    row = pl.BlockSpec((tt, d), lambda i: (i, 0))
    vec = pl.BlockSpec((1, d), lambda i: (0, 0))
    return pl.pallas_call(
        body, name=name, grid=(t // tt,),
        in_specs=[row, vec, row, row], out_specs=[row, vec] + [row] * also_bf16,
        out_shape=[jax.ShapeDtypeStruct((t, d), F32), jax.ShapeDtypeStruct((1, d), F32)]
        + [jax.ShapeDtypeStruct((t, d), BF16)] * also_bf16,
        compiler_params=_cparams(("arbitrary",)),
    )(x, g, dy, dres)


def _tail(h2, gl, pe, target, gfin):
    t, d = h2.shape
    tt = _tile(t, 256, 8)

    def body(h2_ref, gl_ref, pe_ref, tg_ref, gf_ref, dh3_ref, dgl_ref, dpe_ref, loss_ref, dgf_ref):
        i = pl.program_id(0)
        sig = _sigmoid(gl_ref[...])
        pev = pe_ref[...]
        h3 = h2_ref[...] + sig * pev
        r = lax.rsqrt(jnp.mean(h3 * h3, axis=-1, keepdims=True) + EPS)
        xh = h3 * r
        gf = gf_ref[...]
        e = xh * gf - tg_ref[...]
        dy = e * (1.0 / d)
        dxh = dy * gf
        c = jnp.mean(dxh * xh, axis=-1, keepdims=True)
        dh3 = r * (dxh - xh * c)
        dh3_ref[...] = dh3
        dgl_ref[...] = (dh3 * pev * sig * (1.0 - sig)).astype(BF16)
        dpe_ref[...] = (dh3 * sig).astype(BF16)
        lpart = jnp.sum(e * e, axis=0, keepdims=True) * (0.5 / d)
        gpart = jnp.sum(dy * xh, axis=0, keepdims=True)

        @pl.when(i == 0)
        def _():
            loss_ref[...] = lpart
            dgf_ref[...] = gpart

        @pl.when(i > 0)
        def _():
            loss_ref[...] += lpart
            dgf_ref[...] += gpart

        @pl.when(i == t // tt - 1)
        def _():
            loss_ref[...] = jnp.broadcast_to(jnp.sum(loss_ref[...], axis=-1, keepdims=True), (1, d))

    row = pl.BlockSpec((tt, d), lambda i: (i, 0))
    vec = pl.BlockSpec((1, d), lambda i: (0, 0))
    return pl.pallas_call(
        body, name="tail", grid=(t // tt,),
        in_specs=[row, row, row, row, vec], out_specs=[row, row, row, vec, vec],
        out_shape=[jax.ShapeDtypeStruct((t, d), F32), jax.ShapeDtypeStruct((t, d), BF16),
                   jax.ShapeDtypeStruct((t, d), BF16), jax.ShapeDtypeStruct((1, d), F32),
                   jax.ShapeDtypeStruct((1, d), F32)],
        compiler_params=_cparams(("arbitrary",)),
    )(h2, gl, pe, target, gfin)


def _conv(cur_ref, prev_ref, w_ref, b_ref, ext_ref, first, width):
    rows = cur_ref.shape[0]
    ext_ref[0:8, :] = jnp.where(first, 0.0, prev_ref[...])
    ext_ref[8:8 + rows, :] = cur_ref[...]
    acc = b_ref[...]
    for k in range(width):
        acc = acc + w_ref[k:k + 1, :] * ext_ref[pl.ds(9 - width + k, rows), :]
    return acc


def _ffn_specs(t, tt, tc, nf):
    hb = tt // 8
    cur_g = pl.BlockSpec((tt, tc), lambda j, i: (i, j))
    cur_u = pl.BlockSpec((tt, tc), lambda j, i: (i, j + nf))
    prev_g = pl.BlockSpec((8, tc), lambda j, i: (jnp.maximum(i * hb - 1, 0), j))
    prev_u = pl.BlockSpec((8, tc), lambda j, i: (jnp.maximum(i * hb - 1, 0), j + nf))
    w_g = pl.BlockSpec((FFN_CONV, tc), lambda j, i: (0, j))
    w_u = pl.BlockSpec((FFN_CONV, tc), lambda j, i: (0, j + nf))
    b_g = pl.BlockSpec((1, tc), lambda j, i: (0, j))
    b_u = pl.BlockSpec((1, tc), lambda j, i: (0, j + nf))
    return [cur_g, prev_g, cur_u, prev_u, w_g, w_u, b_g, b_u]


FFN_TC = 512


def _shift_down(prev, cur, n, rid):
    return jnp.where(rid < n, pltpu.roll(prev, n, 0), pltpu.roll(cur, n, 0))


def _shift_up(cur, nxt, n, rid):
    return jnp.where(rid < 8 - n, pltpu.roll(cur, 8 - n, 0), pltpu.roll(nxt, 8 - n, 0))


def _conv3_group(prev, cur, w_ref, b_ref, rid):
    x1 = _shift_down(prev, cur, 1, rid)
    x2 = _shift_down(prev, cur, 2, rid)
    return b_ref[...] + w_ref[2:3, :] * cur + w_ref[1:2, :] * x1 + w_ref[0:1, :] * x2


def _ffn_act_fwd(hid, cw, cb):
    t = hid.shape[0]
    tt = _tile(t, 512, 16)
    tc = _tile(D_FF, FFN_TC, 128)
    nf = D_FF // tc

    def body(g_ref, gp_ref, u_ref, up_ref, wg_ref, wu_ref, bg_ref, bu_ref, o_ref, cg_ref, cu_ref):
        first = pl.program_id(1) == 0
        rid = lax.broadcasted_iota(jnp.int32, (8, tc), 0)

        def step(s, carry):
            pg, pu = carry
            r0 = pl.multiple_of(s * 16, 16)
            g0, g1 = g_ref[pl.ds(r0, 8), :], g_ref[pl.ds(r0 + 8, 8), :]
            u0, u1 = u_ref[pl.ds(r0, 8), :], u_ref[pl.ds(r0 + 8, 8), :]
            gate = jnp.concatenate([_conv3_group(pg, g0, wg_ref, bg_ref, rid),
                                    _conv3_group(g0, g1, wg_ref, bg_ref, rid)], axis=0)
            up = jnp.concatenate([_conv3_group(pu, u0, wu_ref, bu_ref, rid),
                                  _conv3_group(u0, u1, wu_ref, bu_ref, rid)], axis=0)
            cg_ref[pl.ds(r0, 16), :] = gate.astype(BF16)
            cu_ref[pl.ds(r0, 16), :] = up.astype(BF16)
            o_ref[pl.ds(r0, 16), :] = (gate * _sigmoid(gate) * up).astype(BF16)
            return g1, u1

        init = (jnp.where(first, 0.0, gp_ref[...]), jnp.where(first, 0.0, up_ref[...]))
        lax.fori_loop(0, tt // 16, step, init)

    blk = pl.BlockSpec((tt, tc), lambda j, i: (i, j))
    return pl.pallas_call(
        body, name="ffn_act_fwd", grid=(nf, t // tt), in_specs=_ffn_specs(t, tt, tc, nf),
        out_specs=[blk, blk, blk],
        out_shape=[jax.ShapeDtypeStruct((t, D_FF), BF16)] * 3,
        compiler_params=_cparams(("parallel", "arbitrary")),
    )(hid, hid, hid, hid, cw, cw, cb, cb)


def _ffn_act_bwd(hid, cw, conv_g, conv_u, dact):
    t = hid.shape[0]
    tt = _tile(t, 512, 16)
    tc = _tile(D_FF, FFN_TC, 128)
    nf = D_FF // tc
    nt = t // tt
    n16 = tt // 16

    def body(g_ref, u_ref, wg_ref, wu_ref, cg_ref, cgn_ref, cu_ref, cun_ref, da_ref, dan_ref,
             og_ref, ou_ref, ag_ref, au_ref, accs):
        i = pl.program_id(1)
        first, last = i == 0, i == nt - 1
        rid = lax.broadcasted_iota(jnp.int32, (8, tc), 0)
        accs[...] = jnp.zeros_like(accs)

        def dgroup(gate, up, da):
            sv, sgr = _silu_and_grad(gate)
            return da * up * sgr, da * sv

        def finish(x, d0, d1, w_ref):
            s1 = _shift_up(d0, d1, 1, rid)
            s2 = _shift_up(d0, d1, 2, rid)
            dpre = w_ref[2:3, :] * d0 + w_ref[1:2, :] * s1 + w_ref[0:1, :] * s2
            return dpre, (x * s2, x * s1, x * d0, d0)

        def two_groups(it, carry, gate_blk, up_blk, da_blk, zero_ahead):
            d0g, d0u, gate1, up1, da1 = carry
            r0 = it * 16 if isinstance(it, int) else pl.multiple_of(it * 16, 16)
            x0g, x0u = g_ref[pl.ds(r0, 8), :], u_ref[pl.ds(r0, 8), :]
            x1g, x1u = g_ref[pl.ds(r0 + 8, 8), :], u_ref[pl.ds(r0 + 8, 8), :]
            d1g, d1u = dgroup(gate1, up1, da1)
            d2g, d2u = dgroup(gate_blk[0:8], up_blk[0:8], da_blk[0:8])
            d2g = jnp.where(zero_ahead, 0.0, d2g)
            d2u = jnp.where(zero_ahead, 0.0, d2u)
            for half, (xa, xb, da_, db_, dc_, w_ref, o_ref) in enumerate((
                    (x0g, x1g, d0g, d1g, d2g, wg_ref, og_ref), (x0u, x1u, d0u, d1u, d2u, wu_ref, ou_ref))):
                pa, prods_a = finish(xa, da_, db_, w_ref)
                pb, prods_b = finish(xb, db_, dc_, w_ref)
                o_ref[pl.ds(r0, 16), :] = jnp.concatenate([pa, pb], axis=0).astype(BF16)
                for k in range(4):
                    accs[4 * half + k] += prods_a[k] + prods_b[k]
            return d2g, d2u, gate_blk[8:16], up_blk[8:16], da_blk[8:16]

        def rows16(ref, r):
            return ref[pl.ds(r, 16), :].astype(F32)

        def step(it, carry):
            r1 = pl.multiple_of(it * 16 + 16, 16)
            return two_groups(it, carry, rows16(cg_ref, r1), rows16(cu_ref, r1), rows16(da_ref, r1), False)

        g0, u0, da0 = rows16(cg_ref, 0), rows16(cu_ref, 0), rows16(da_ref, 0)
        d0g, d0u = dgroup(g0[0:8], u0[0:8], da0[0:8])
        carry = lax.fori_loop(0, n16 - 1, step, (d0g, d0u, g0[8:16], u0[8:16], da0[8:16]))
        two_groups(n16 - 1, carry, cgn_ref[...].astype(F32), cun_ref[...].astype(F32), dan_ref[...].astype(F32), last)

        @pl.when(first)
        def _():
            ag_ref[...] = jnp.zeros_like(ag_ref)
            au_ref[...] = jnp.zeros_like(au_ref)

        for half, a_ref in enumerate((ag_ref, au_ref)):
            for k in range(4):
                a_ref[k:k + 1, :] += jnp.sum(accs[4 * half + k], axis=0, keepdims=True)

    def nxt16(i):
        return jnp.minimum((i + 1) * n16, t // 16 - 1)

    out_blk = pl.BlockSpec((tt, tc), lambda j, i: (i, j))
    acc_spec = pl.BlockSpec((8, tc), lambda j, i: (0, j))
    nxt_blk = pl.BlockSpec((16, tc), lambda j, i: (nxt16(i), j))
    in_specs = [out_blk, pl.BlockSpec((tt, tc), lambda j, i: (i, j + nf)),
                pl.BlockSpec((FFN_CONV, tc), lambda j, i: (0, j)), pl.BlockSpec((FFN_CONV, tc), lambda j, i: (0, j + nf)),
                out_blk, nxt_blk, out_blk, nxt_blk, out_blk, nxt_blk]
    return pl.pallas_call(
        body, name="ffn_act_bwd", grid=(nf, nt), in_specs=in_specs,
        out_specs=[out_blk, out_blk, acc_spec, acc_spec],
        out_shape=[jax.ShapeDtypeStruct((t, D_FF), BF16), jax.ShapeDtypeStruct((t, D_FF), BF16),
                   jax.ShapeDtypeStruct((8, D_FF), F32), jax.ShapeDtypeStruct((8, D_FF), F32)],
        scratch_shapes=[pltpu.VMEM((8, 8, tc), F32)],
        compiler_params=_cparams(("parallel", "arbitrary")),
    )(hid, hid, cw, cw, conv_g, conv_g, conv_u, conv_u, dact, dact)


def _tri_mask():
    r = lax.broadcasted_iota(jnp.int32, (CHUNK, CHUNK), 0)
    c = lax.broadcasted_iota(jnp.int32, (CHUNK, CHUNK), 1)
    return r >= c


def _gmlp_group_fwd(uv_ref, lng_ref, lnb_ref, ws_ref, bexp_ref, tri, g, want_grad):
    lo, hi = g * 128, (g + 1) * 128
    u_pre = uv_ref[:, lo:hi]
    v_pre = uv_ref[:, D_A + lo:D_A + hi]
    u, du = _gelu_and_grad(u_pre)
    v, dv = _gelu_and_grad(v_pre)
    mu = jnp.mean(v, axis=-1, keepdims=True)
    dc = v - mu
    rs = lax.rsqrt(jnp.mean(dc * dc, axis=-1, keepdims=True) + EPS)
    xh = dc * rs
    vn = (xh * lng_ref[:, lo:hi] + lnb_ref[:, lo:hi]).astype(BF16)
    w = jnp.where(tri, ws_ref[g], 0.0).astype(BF16)
    sg = _dot(w, vn) + bexp_ref[g]
    if want_grad:
        return u, du, dv, rs, xh, vn, w, sg
    return u * sg


def _gmlp_fwd(proj, ln_g, ln_b, w_s, b_exp, na_g):
    t = proj.shape[0]
    ng = N_GROUPS_A

    def body(uv_ref, lng_ref, lnb_ref, ws_ref, bexp_ref, nag_ref, o_ref):
        tri = _tri_mask()
        ys = [_gmlp_group_fwd(uv_ref, lng_ref, lnb_ref, ws_ref, bexp_ref, tri, g, False) for g in range(ng)]
        ssq = ys[0] * 0.0
        for y in ys:
            ssq = ssq + y * y
        r = lax.rsqrt(jnp.sum(ssq, axis=-1, keepdims=True) * (1.0 / D_A) + EPS)
        for g, y in enumerate(ys):
            o_ref[:, g * 128:(g + 1) * 128] = (y * r * nag_ref[:, g * 128:(g + 1) * 128]).astype(BF16)

    vec = pl.BlockSpec((1, D_A), lambda i: (0, 0))
    cube = pl.BlockSpec((ng, CHUNK, CHUNK), lambda i: (0, 0, 0))
    return pl.pallas_call(
        body, name="gmlp_fwd", grid=(t // CHUNK,),
        in_specs=[pl.BlockSpec((CHUNK, 2 * D_A), lambda i: (i, 0)), vec, vec, cube, cube, vec],
        out_specs=pl.BlockSpec((CHUNK, D_A), lambda i: (i, 0)),
        out_shape=jax.ShapeDtypeStruct((t, D_MIX), BF16),
        compiler_params=_cparams(("parallel",)),
    )(proj, ln_g, ln_b, w_s, b_exp, na_g)


def _gmlp_bwd(proj, dyab, ln_g, ln_b, w_s, b_exp, na_g):
    t = proj.shape[0]
    ng = N_GROUPS_A
    nsteps = t // CHUNK

    def body(uv_ref, dy_ref, lng_ref, lnb_ref, ws_ref, bexp_ref, nag_ref,
             duv_ref, dws_ref, dbs_ref, dlng_ref, dlnb_ref, dnag_ref, dbacc):
        i = pl.program_id(0)
        tri = _tri_mask()

        @pl.when(i == 0)
        def _():
            dws_ref[...] = jnp.zeros_like(dws_ref)
            dbacc[...] = jnp.zeros_like(dbacc)
            dlng_ref[...] = jnp.zeros_like(dlng_ref)
            dlnb_ref[...] = jnp.zeros_like(dlnb_ref)
            dnag_ref[...] = jnp.zeros_like(dnag_ref)

        st = [_gmlp_group_fwd(uv_ref, lng_ref, lnb_ref, ws_ref, bexp_ref, tri, g, True) for g in range(ng)]
        ssq = st[0][0] * 0.0
        for s in st:
            y = s[0] * s[7]
            ssq = ssq + y * y
        r = lax.rsqrt(jnp.sum(ssq, axis=-1, keepdims=True) * (1.0 / D_A) + EPS)
        csum = st[0][0] * 0.0
        for g, s in enumerate(st):
            sl = slice(g * 128, (g + 1) * 128)
            xhy = s[0] * s[7] * r
            dya = dy_ref[:, sl].astype(F32)
            dnag_ref[:, sl] += jnp.sum(dya * xhy, axis=0, keepdims=True)
            csum = csum + dya * nag_ref[:, sl] * xhy
        c1 = jnp.sum(csum, axis=-1, keepdims=True) * (1.0 / D_A)
        for g, s in enumerate(st):
            u, du, dv, rs, xh, vn, w, sg = s
            sl = slice(g * 128, (g + 1) * 128)
            dy = r * (dy_ref[:, sl].astype(F32) * nag_ref[:, sl] - u * sg * r * c1)
            dsg = dy * u
            dsg_b = dsg.astype(BF16)
            dws_ref[g] += _dot_nt(dsg_b, vn)
            dbacc[g] += dsg
            dvn = _dot_tn(w, dsg_b)
            dlnb_ref[:, sl] += jnp.sum(dvn, axis=0, keepdims=True)
            dlng_ref[:, sl] += jnp.sum(dvn * xh, axis=0, keepdims=True)
            dxh = dvn * lng_ref[:, sl]
            dvv = rs * (dxh - jnp.mean(dxh, axis=-1, keepdims=True)
                        - xh * jnp.mean(dxh * xh, axis=-1, keepdims=True))
            duv_ref[:, sl] = (dy * sg * du).astype(BF16)
            duv_ref[:, D_A + g * 128:D_A + (g + 1) * 128] = (dvv * dv).astype(BF16)

        @pl.when(i == nsteps - 1)
        def _():
            for g in range(ng):
                dws_ref[g] = jnp.where(tri, dws_ref[g], 0.0)
                dbs_ref[g] = jnp.sum(dbacc[g], axis=-1, keepdims=True)

    vec = pl.BlockSpec((1, D_A), lambda i: (0, 0))
    cube = pl.BlockSpec((ng, CHUNK, CHUNK), lambda i: (0, 0, 0))
    return pl.pallas_call(
        body, name="gmlp_bwd", grid=(nsteps,),
        in_specs=[pl.BlockSpec((CHUNK, 2 * D_A), lambda i: (i, 0)),
                  pl.BlockSpec((CHUNK, D_A), lambda i: (i, 0)), vec, vec, cube, cube, vec],
        out_specs=[pl.BlockSpec((CHUNK, 2 * D_A), lambda i: (i, 0)), cube,
                   pl.BlockSpec((ng, CHUNK, 1), lambda i: (0, 0, 0)), vec, vec, vec],
        out_shape=[jax.ShapeDtypeStruct((t, 2 * D_A), BF16), jax.ShapeDtypeStruct((ng, CHUNK, CHUNK), F32),
                   jax.ShapeDtypeStruct((ng, CHUNK, 1), F32), jax.ShapeDtypeStruct((1, D_A), F32),
                   jax.ShapeDtypeStruct((1, D_A), F32), jax.ShapeDtypeStruct((1, D_A), F32)],
        scratch_shapes=[pltpu.VMEM((ng, CHUNK, CHUNK), F32)],
        compiler_params=_cparams(("arbitrary",)),
    )(proj, dyab, ln_g, ln_b, w_s, b_exp, na_g)


OFF_Z = 2 * D_A
OFF_XS = OFF_Z + D_SSM
OFF_B = OFF_XS + D_SSM
OFF_C = OFF_B + D_BC


def _ssd_consts():
    tri = jnp.tril(jnp.ones((CHUNK, CHUNK), F32)).astype(BF16)
    h = jnp.arange(HPAD)[None, :, None]
    g = jnp.arange(N_SSM_GROUPS)[:, None, None]
    j1 = jnp.arange(GW)[None, None, :]
    eh = (h == g * HEADS_PER_GROUP + j1 // HEAD_DIM).astype(BF16)
    j2 = jnp.arange(HEADS_PER_GROUP * 128)[None, None, :]
    e128 = (h == g * HEADS_PER_GROUP + j2 // 128).astype(BF16)
    return tri, eh, e128


def _ssd_in_specs(cmap):
    def rows(i):
        return cmap(i)

    def prev8(i):
        return jnp.maximum(cmap(i) * (CHUNK // 8) - 1, 0)

    def whole(*shape):
        return pl.BlockSpec(shape, lambda i: (0,) * len(shape))

    bcw = 2 * D_BC
    specs = [
        pl.BlockSpec((CHUNK, D_SSM), lambda i: (rows(i), OFF_Z // D_SSM)),
        pl.BlockSpec((CHUNK, D_SSM), lambda i: (rows(i), OFF_XS // D_SSM)),
        pl.BlockSpec((8, D_SSM), lambda i: (prev8(i), OFF_XS // D_SSM)),
        pl.BlockSpec((CHUNK, bcw), lambda i: (rows(i), OFF_B // bcw)),
        pl.BlockSpec((8, bcw), lambda i: (prev8(i), OFF_B // bcw)),
        pl.BlockSpec((CHUNK, HPAD), lambda i: (rows(i), 0)),
        whole(SSM_CONV, D_XBC), whole(1, D_XBC),
        whole(1, HPAD), whole(1, HPAD),
        whole(1, D_SSM), whole(1, D_SSM),
        whole(CHUNK, CHUNK),
        whole(N_SSM_GROUPS, HPAD, GW), whole(N_SSM_GROUPS, HPAD, HEADS_PER_GROUP * 128),
    ]
    return specs


N_SSD_IN = 15


def _lanes(ref, start, width):
    return ref.at[:, pl.ds(pl.multiple_of(start, 128), width)]


def _ssd_group_refs(ins, g):
    (z_ref, xs_ref, xsp_ref, bc_ref, bcp_ref, dt_ref, cw_ref, cb_ref, dtb_ref, alog_ref, de_ref, gain_ref,
     tri_ref, eh_ref, e128_ref) = ins
    ox, ob, oc = g * GW, g * D_STATE, D_BC + g * D_STATE
    return (_lanes(z_ref, ox, GW), _lanes(xs_ref, ox, GW), _lanes(xsp_ref, ox, GW),
            _lanes(bc_ref, ob, D_STATE), _lanes(bcp_ref, ob, D_STATE),
            _lanes(bc_ref, oc, D_STATE), _lanes(bcp_ref, oc, D_STATE), dt_ref,
            _lanes(cw_ref, ox, GW), _lanes(cw_ref, D_SSM + ob, D_STATE), _lanes(cw_ref, D_SSM + oc, D_STATE),
            _lanes(cb_ref, ox, GW), _lanes(cb_ref, D_SSM + ob, D_STATE), _lanes(cb_ref, D_SSM + oc, D_STATE),
            dtb_ref, alog_ref, _lanes(de_ref, ox, GW), _lanes(gain_ref, ox, GW), tri_ref,
            eh_ref.at[g], e128_ref.at[g])


N_SSD_OWN = 4


def _ssd_scratch():
    return [pltpu.VMEM((CHUNK + 8, GW), F32), pltpu.VMEM((CHUNK + 8, D_STATE), F32),
            pltpu.VMEM((CHUNK + 8, D_STATE), F32), pltpu.VMEM((CHUNK, GW), F32),
            pltpu.VMEM((HPAD, CHUNK), F32), pltpu.VMEM((CHUNK, HPAD), F32), pltpu.VMEM((CHUNK, HPAD), F32)]


def _ssd_two_sets(alloc):
    a, b = alloc[:7], alloc[7:7 + N_SSD_OWN]
    return ((a[0], a[1], a[2], a[4], a[3], a[5], a[6]), (b[0], b[1], b[2], a[4], b[3], a[5], a[6]))


def _ssd_chunk(ins, scr):
    dt_ref, dtb_ref, alog_ref, tri_ref = ins[5], ins[8], ins[9], ins[12]
    acst_sc, dt_sc, acs_sc = scr[3], scr[5], scr[6]
    dt = _softplus(dt_ref[...] + dtb_ref[...])
    acs = _e01x(tri_ref[...], dt * (-jnp.exp(alog_ref[...])))
    dt_sc[...] = dt
    acs_sc[...] = acs
    acst_sc[...] = acs.T


def _ssd_pre(first, g, refs, scr):
    (z_ref, xs_ref, xsp_ref, b_ref, bp_ref, c_ref, cp_ref, dt_ref, cwx, cwb, cwc, cbx, cbb, cbc,
     dtb_ref, alog_ref, de_ref, gain_ref, tri_ref, eh_ref, e128_ref) = refs
    ext_x, ext_b, ext_c, acst_sc, acse_sc, dt_sc, acs_sc = scr
    p = {}
    px = _conv(xs_ref, xsp_ref, cwx, cbx, ext_x, first, SSM_CONV)
    pb = _conv(b_ref, bp_ref, cwb, cbb, ext_b, first, SSM_CONV)
    pc = _conv(c_ref, cp_ref, cwc, cbc, ext_c, first, SSM_CONV)
    p['xs'], p['dsx'] = _silu_and_grad(px)
    p['bm'], p['dsb'] = _silu_and_grad(pb)
    p['cm'], p['dsc'] = _silu_and_grad(pc)
    p['dt_in'] = dt_ref[...] + dtb_ref[...]
    dt = dt_sc[...]
    p['dt'] = dt
    p['a'] = -jnp.exp(alog_ref[...])
    tri_b = tri_ref[...]
    acs = acs_sc[...]
    eh = eh_ref[...]
    p['eh'] = eh
    p['dt_e'] = _x01(dt, eh)
    acs_e = _x01(acs, eh)
    acse_sc[...] = acs_e
    p['acs_e'] = acs_e
    p['acs_c'] = _x01(acs, e128_ref[...])
    p['acs_last_e'] = acse_sc[pl.ds(CHUNK - 1, 1), :]
    p['xdt'] = p['xs'] * p['dt_e']
    p['decay_e'] = jnp.exp(p['acs_last_e'] - acs_e)
    p['cm_b'] = p['cm'].astype(BF16)
    p['bm_b'] = p['bm'].astype(BF16)
    p['scores'] = _dot_nt(p['cm_b'], p['bm_b'])
    p['tri_b'] = tri_b
    return p


def _ssd_l(p, g, r, acst_sc, tri):
    col = p['acs_c'][:, r * 128:(r + 1) * 128]
    row = acst_sc[pl.ds(g * HEADS_PER_GROUP + r, 1), :]
    return jnp.where(tri, jnp.exp(jnp.minimum(col - row, 0.0)), 0.0)


def _ssd_fwd(proj, dt_raw, yab, cw, cb, dtb, alog, de, gain, consts, hosted):
    t = proj.shape[0]
    nc = t // CHUNK
    ng = N_SSM_GROUPS
    tri_c, eh_c, e128_c = consts
    n_in = N_SSD_IN
    nh_in, nh_out = len(hosted['arrays']), len(hosted['out_shape'])
    mid_chunk = (7 * nc) // 8

    def body(*refs):
        ins = refs[:n_in]
        hins = refs[n_in + 1:n_in + 1 + nh_in]
        o0 = n_in + 1 + nh_in
        yb_all, ys_all, hs_all = refs[o0:o0 + 3]
        houts = refs[o0 + 3:o0 + 3 + nh_out]
        h_sc = refs[o0 + 3 + nh_out]
        scr_a, scr_b = _ssd_two_sets(refs[o0 + 4 + nh_out:-2])
        sems = refs[-2:]
        c = pl.program_id(0)
        _host_phase(hosted, 'start', c == 0, hins, houts, sems)
        _host_phase(hosted, 'mid', c == mid_chunk, hins, houts, sems)

        @pl.when(c == 0)
        def _():
            h_sc[...] = jnp.zeros_like(h_sc)

        def group(g, scr):
            grefs = _ssd_group_refs(ins, g)
            z_ref, de_ref, gain_ref = grefs[0], grefs[16], grefs[17]
            yb_ref, ys_ref = _lanes(yb_all, g * GW, GW), _lanes(ys_all, g * GW, GW)
            slab = pl.ds(pl.multiple_of(g * D_STATE, D_STATE), D_STATE)
            p = _ssd_pre(c == 0, g, grefs, scr)
            tri = _tri_mask()
            h_in = h_sc[slab, :]
            hs_all[slab, :] = h_in
            yoff = _dot(p['cm_b'], h_in.astype(BF16)) * jnp.exp(p['acs_e'])
            states = _dot_tn(p['bm_b'], (p['xdt'] * p['decay_e']).astype(BF16))
            lane = lax.broadcasted_iota(jnp.int32, (CHUNK, 128), 1)
            slabs = []
            for r2 in range(HEADS_PER_GROUP // 2):
                xb = p['xdt'][:, r2 * 128:(r2 + 1) * 128].astype(BF16)
                ya = _dot((p['scores'] * _ssd_l(p, g, 2 * r2, scr[3], tri)).astype(BF16), xb)
                yb = _dot((p['scores'] * _ssd_l(p, g, 2 * r2 + 1, scr[3], tri)).astype(BF16), xb)
                slabs.append(jnp.where(lane < HEAD_DIM, ya, yb))
            y = jnp.concatenate(slabs, axis=1) + yoff + de_ref[...] * p['xs']
            ys_ref[...] = y
            h_sc[slab, :] = jnp.exp(p['acs_last_e']) * h_in + states
            zv = z_ref[...]
            yg = y * zv * _sigmoid(zv)
            r = lax.rsqrt(jnp.mean(yg * yg, axis=-1, keepdims=True) + EPS)
            yb_ref[...] = (yg * r * gain_ref[...]).astype(BF16)

        def pair(j, carry):
            group(2 * j, scr_a)
            group(2 * j + 1, scr_b)
            return carry

        _ssd_chunk(ins, scr_a)
        lax.fori_loop(0, ng // 2, pair, 0)
        _host_phase(hosted, 'finish', c == nc - 1, hins, houts, sems)

    h_in, h_out, h_shape, h_scratch, h_alias = _host_plumbing(hosted, n_in + 1, 3)
    in_specs = _ssd_in_specs(lambda i: i) + [pl.BlockSpec(memory_space=pl.ANY)] + h_in
    out_specs = [pl.BlockSpec((CHUNK, D_SSM), lambda i: (i, D_A // D_SSM)),
                 pl.BlockSpec((CHUNK, D_SSM), lambda i: (i, 0)),
                 pl.BlockSpec((None, ng * D_STATE, GW), lambda i: (i, 0, 0))] + h_out
    out_shape = [jax.ShapeDtypeStruct((t, D_MIX), BF16), jax.ShapeDtypeStruct((t, D_SSM), F32),
                 jax.ShapeDtypeStruct((nc, ng * D_STATE, GW), F32)] + h_shape
    return pl.pallas_call(
        body, name="ssd_fwd", grid=(nc,), in_specs=in_specs, out_specs=out_specs, out_shape=out_shape,
        scratch_shapes=[pltpu.VMEM((ng * D_STATE, GW), F32)] + _ssd_scratch() + _ssd_scratch()[:N_SSD_OWN]
        + h_scratch,
        input_output_aliases={n_in: 0, **h_alias},
        compiler_params=_cparams(("arbitrary",)),
    )(proj, proj, proj, proj, proj, dt_raw, cw, cb, dtb, alog, de, gain,
      tri_c, eh_c, e128_c, yab, *hosted['arrays'])


def _rows8(vals, width):
    rid = lax.broadcasted_iota(jnp.int32, (8, width), 0)
    out = jnp.zeros((8, width), F32)
    for k, v in enumerate(vals):
        if v is not None:
            out = out + jnp.where(rid == k, v, 0.0)
    return out


def _ssd_bwd(proj, dt_raw, dyab, ysave, hs, duv, cw, cb, dtb, alog, de, gain, consts, hosted):
    t = proj.shape[0]
    nc = t // CHUNK
    ng = N_SSM_GROUPS
    tri_c, eh_c, e128_c = consts
    n_in = N_SSD_IN
    nh_in, nh_out = len(hosted['arrays']), len(hosted['out_shape'])

    def per_group(g, c, ins, dy_ref, ys_ref, hs_ref, outs, scratch):
        dz_ref, dxs_ref, db_ref, dc_ref, ddt_ref, acc_x, acc_b, acc_c, acc_gain, acc_head = outs
        dh_sc, car_x, car_b, car_c, dext_x, dext_b, dext_c = scratch[:7]
        scr = scratch[7:]
        ext_x, ext_b, ext_c, acst_sc = scr[:4]
        z_ref, cwx, cwb, cwc, de_ref, gain_ref = ins[0], ins[8], ins[9], ins[10], ins[16], ins[17]
        slab = pl.ds(pl.multiple_of(g * D_STATE, D_STATE), D_STATE)
        p = _ssd_pre(c == 0, g, ins, scr)
        tri = _tri_mask()
        xs, dt_e, acs_e, xdt, decay_e = p['xs'], p['dt_e'], p['acs_e'], p['xdt'], p['decay_e']
        cm_b, bm_b, scores, eh = p['cm_b'], p['bm_b'], p['scores'], p['eh']
        h_in = hs_ref[...]
        h_in_b = h_in.astype(BF16)
        e_a = jnp.exp(acs_e)
        raw = _dot(cm_b, h_in_b)

        y = ys_ref[...]
        zv = z_ref[...]
        sz, dsz = _silu_and_grad(zv)
        yg = y * sz
        r = lax.rsqrt(jnp.mean(yg * yg, axis=-1, keepdims=True) + EPS)
        xh = yg * r
        dout = dy_ref[...].astype(F32)
        gain = gain_ref[...]
        dxh = dout * gain
        dyg = r * (dxh - xh * jnp.mean(dxh * xh, axis=-1, keepdims=True))
        dy = dyg * sz
        dz_ref[...] = (dyg * y * dsz).astype(BF16)
        acc_gain[g] += _rows8([jnp.sum(dout * xh, axis=0, keepdims=True)], GW)
        d_skip8 = _x01_nt(_rows8([None, None, jnp.sum(dy * xs, axis=0, keepdims=True)], GW), eh)
        dxs = de_ref[...] * dy

        q = dy * raw * e_a
        draw = (dy * e_a).astype(BF16)
        d_c = _dot_nt(draw, h_in_b)
        dh_in = _dot_tn(cm_b, draw)

        lane = lax.broadcasted_iota(jnp.int32, (CHUNK, 128), 1)
        ones_b = jnp.ones((CHUNK, 128), BF16)
        dscores = jnp.zeros((CHUNK, CHUNK), F32)
        dxdt_slabs, q_slabs = [], []
        for r2 in range(HEADS_PER_GROUP // 2):
            sl = slice(r2 * 128, (r2 + 1) * 128)
            xb = xdt[:, sl].astype(BF16)
            dys = dy[:, sl]
            dys_b = dys.astype(BF16)
            dxh_pair, qv_pair = [], []
            for half in range(2):
                lmat = _ssd_l(p, g, 2 * r2 + half, acst_sc, tri)
                m = scores * lmat
                mine = (lane < HEAD_DIM) if half == 0 else (lane >= HEAD_DIM)
                dm = _dot_nt(jnp.where(mine, dys, 0.0).astype(BF16), xb)
                dscores = dscores + dm * lmat
                gm = dm * m
                dxh_pair.append(_dot_tn(m.astype(BF16), dys_b))
                h3 = _split3(gm)
                colsum = _dot_tn(h3[0], ones_b) + _dot_tn(h3[1], ones_b) + _dot_tn(h3[2], ones_b)
                qv_pair.append(jnp.sum(gm, axis=-1, keepdims=True) - colsum)
            dxdt_slabs.append(jnp.where(lane < HEAD_DIM, dxh_pair[0], dxh_pair[1]))
            q_slabs.append(jnp.where(lane == 0, qv_pair[0], 0.0) + jnp.where(lane == HEAD_DIM, qv_pair[1], 0.0))
        dxdt = jnp.concatenate(dxdt_slabs, axis=1)
        q = q + jnp.concatenate(q_slabs, axis=1)

        dh_out = dh_sc[slab, :]
        dh_out_b = dh_out.astype(BF16)
        e_l = jnp.exp(p['acs_last_e'])
        dh_sc[slab, :] = dh_in + e_l * dh_out
        dlast = jnp.sum(dh_out * h_in, axis=0, keepdims=True) * e_l
        dxd = _dot(bm_b, dh_out_b)
        xd = xdt * decay_e
        dxdt = dxdt + dxd * decay_e
        dd = dxd * xd
        q = q - dd
        dlast = dlast + jnp.sum(dd, axis=0, keepdims=True)
        d_b = _dot_nt(xd.astype(BF16), dh_out_b)
        dsc_b = dscores.astype(BF16)
        d_c = d_c + _dot(dsc_b, bm_b)
        d_b = d_b + _dot_tn(dsc_b, cm_b)

        dxs = dxs + dxdt * dt_e
        rid = lax.broadcasted_iota(jnp.int32, (CHUNK, GW), 0)
        q = q + jnp.where(rid == CHUNK - 1, dlast, 0.0)
        dacs = _x01_nt(q, eh)
        ddt = _x01_nt(dxdt * xs, eh)
        dadt = _e01x_tn(p['tri_b'], dacs)
        ddt = ddt + dadt * p['a']
        d_a = jnp.sum(dadt * p['dt'], axis=0, keepdims=True)
        ddt_raw = ddt * _sigmoid(p['dt_in'])
        acc_head[...] += _rows8([jnp.sum(ddt_raw, axis=0, keepdims=True), d_a * p['a']], HPAD) + d_skip8

        @pl.when(g == 0)
        def _():
            ddt_ref[...] = ddt_raw

        @pl.when(g > 0)
        def _():
            ddt_ref[...] += ddt_raw

        for dv, dsil, ext, dext, car, acc, w_ref, o_ref in (
                (dxs, p['dsx'], ext_x, dext_x, car_x, acc_x, cwx, dxs_ref),
                (d_b, p['dsb'], ext_b, dext_b, car_b, acc_b, cwb, db_ref),
                (d_c, p['dsc'], ext_c, dext_c, car_c, acc_c, cwc, dc_ref)):
            dp = dv * dsil
            width = dp.shape[1]
            rows = [jnp.sum(ext[pl.ds(5 + k, CHUNK), :] * dp, axis=0, keepdims=True) for k in range(SSM_CONV)]
            rows.append(jnp.sum(dp, axis=0, keepdims=True))
            acc[g] += _rows8(rows, width)
            dext[0:CHUNK, :] = dp
            dext[CHUNK:CHUNK + 8, :] = car[g]
            car[g] = dext[0:8, :]
            dx = w_ref[SSM_CONV - 1:SSM_CONV, :] * dext[pl.ds(0, CHUNK), :]
            for k in range(SSM_CONV - 1):
                dx = dx + w_ref[k:k + 1, :] * dext[pl.ds(SSM_CONV - 1 - k, CHUNK), :]
            o_ref[...] = dx.astype(BF16)

    def body(*refs):
        ins = refs[:n_in]
        dy_all, ys_all, hs_all, duv_ref = refs[n_in:n_in + 4]
        hins = refs[n_in + 4:n_in + 4 + nh_in]
        o0 = n_in + 4 + nh_in
        dproj_ref, ddt_ref = refs[o0:o0 + 2]
        accs = refs[o0 + 2:o0 + 7]
        houts = refs[o0 + 7:o0 + 7 + nh_out]
        scratch = refs[o0 + 7 + nh_out:-2]
        sems = refs[-2:]
        cc = pl.program_id(0)
        _host_phase(hosted, 'start', cc == 0, hins, houts, sems)
        dproj_ref[:, 0:2 * D_A] = duv_ref[...]

        @pl.when(cc == 0)
        def _():
            for a in tuple(accs) + tuple(scratch[:4]):
                a[...] = jnp.zeros_like(a)

        shared = tuple(scratch[:4])
        scr_a, scr_b = _ssd_two_sets(tuple(scratch[7:14]) + tuple(scratch[17:17 + N_SSD_OWN]))
        set_a = shared + tuple(scratch[4:7]) + scr_a
        set_b = shared + tuple(scratch[14:17]) + scr_b

        def group(g, own):
            slab = pl.ds(pl.multiple_of(g * D_STATE, D_STATE), D_STATE)
            outs = (_lanes(dproj_ref, OFF_Z + g * GW, GW), _lanes(dproj_ref, OFF_XS + g * GW, GW),
                    _lanes(dproj_ref, OFF_B + g * D_STATE, D_STATE), _lanes(dproj_ref, OFF_C + g * D_STATE, D_STATE),
                    ddt_ref) + tuple(accs)
            per_group(g, nc - 1 - cc, _ssd_group_refs(ins, g), _lanes(dy_all, g * GW, GW),
                      _lanes(ys_all, g * GW, GW), hs_all.at[slab, :], outs, own)

        def pair(j, carry):
            group(2 * j, set_a)
            group(2 * j + 1, set_b)
            return carry

        _ssd_chunk(ins, scr_a)
        lax.fori_loop(0, ng // 2, pair, 0)
        _host_phase(hosted, 'finish', cc == nc - 1, hins, houts, sems)

    def cmap(i):
        return nc - 1 - i

    in_specs = _ssd_in_specs(cmap) + [
        pl.BlockSpec((CHUNK, D_SSM), lambda i: (cmap(i), D_A // D_SSM)),
        pl.BlockSpec((CHUNK, D_SSM), lambda i: (cmap(i), 0)),
        pl.BlockSpec((None, ng * D_STATE, GW), lambda i: (cmap(i), 0, 0)),
        pl.BlockSpec((CHUNK, 2 * D_A), lambda i: (cmap(i), 0)),
    ]

    def full(shape):
        return pl.BlockSpec(shape, lambda i: (0,) * len(shape))

    out_specs = [
        pl.BlockSpec((CHUNK, D_MAIN), lambda i: (cmap(i), 0)),
        pl.BlockSpec((CHUNK, HPAD), lambda i: (cmap(i), 0)),
        full((ng, 8, GW)), full((ng, 8, D_STATE)), full((ng, 8, D_STATE)),
        full((ng, 8, GW)), full((8, HPAD)),
    ]
    out_shape = [
        jax.ShapeDtypeStruct((t, D_MAIN), BF16),
        jax.ShapeDtypeStruct((t, HPAD), F32),
        jax.ShapeDtypeStruct((ng, 8, GW), F32), jax.ShapeDtypeStruct((ng, 8, D_STATE), F32),
        jax.ShapeDtypeStruct((ng, 8, D_STATE), F32), jax.ShapeDtypeStruct((ng, 8, GW), F32),
        jax.ShapeDtypeStruct((8, HPAD), F32),
    ]
    scratch = [pltpu.VMEM((ng * D_STATE, GW), F32),
               pltpu.VMEM((ng, 8, GW), F32), pltpu.VMEM((ng, 8, D_STATE), F32), pltpu.VMEM((ng, 8, D_STATE), F32),
               pltpu.VMEM((CHUNK + 8, GW), F32), pltpu.VMEM((CHUNK + 8, D_STATE), F32),
               pltpu.VMEM((CHUNK + 8, D_STATE), F32)] + _ssd_scratch()
    scratch += scratch[4:7] + _ssd_scratch()[:N_SSD_OWN]
    h_in, h_out, h_shape, h_scratch, h_alias = _host_plumbing(hosted, n_in + 4, len(out_shape))
    return pl.pallas_call(
        body, name="ssd_bwd", grid=(nc,), in_specs=in_specs + h_in, out_specs=out_specs + h_out,
        out_shape=out_shape + h_shape, scratch_shapes=scratch + h_scratch, input_output_aliases=h_alias,
        compiler_params=_cparams(("arbitrary",)),
    )(proj, proj, proj, proj, proj, dt_raw, cw, cb, dtb, alog, de, gain,
      tri_c, eh_c, e128_c, dyab, ysave, hs, duv, *hosted['arrays'])


ANY = pl.BlockSpec(memory_space=pl.ANY)


def _place():
    x, y, c = lax.axis_index("x"), lax.axis_index("y"), lax.axis_index("c")
    chips = [(1 - x, y), (x, 1 - y), (1 - x, 1 - y)]
    return x, y, c, chips


def _rcopy(src, dst, send_sems, recv_sems, k, dev):
    return pltpu.make_async_remote_copy(src_ref=src, dst_ref=dst, send_sem=send_sems.at[k],
                                        recv_sem=recv_sems.at[k], device_id=dev, device_id_type=MESH)


def _my_chip():
    return 2 * lax.axis_index("x") + lax.axis_index("y")


def _cast_into_slot(w, name):
    r, c = w.shape
    tr = _row_tile(r, c, 2)

    def body(w_ref, o_ref):
        o_ref[...] = w_ref[...].astype(BF16)

    return pl.pallas_call(
        body, name=name, grid=(r // tr,), in_specs=[pl.BlockSpec((tr, c), lambda i: (i, 0))],
        out_specs=pl.BlockSpec((None, tr, c), lambda i: (_my_chip(), i, 0)),
        out_shape=jax.ShapeDtypeStruct((N_CHIPS, r, c), BF16), compiler_params=_cparams(("parallel",)),
    )(w)


def _hosted_gather(bigs):
    nb = len(bigs)

    def rows(a, c):
        half = bigs[a].shape[1] // 2
        return pl.ds(c * half, half)

    def start(ins, outs, send_sems, recv_sems):
        x, y, c, chips = _place()
        q = 2 * x + y
        for a in range(nb):
            for k, chip in enumerate(chips):
                _rcopy(outs[a].at[q, rows(a, c)], outs[a].at[q, rows(a, c)], send_sems, recv_sems, 6 * a + k,
                       (chip[0], chip[1], c)).start()

    def mid(ins, outs, send_sems, recv_sems):
        x, y, c, chips = _place()
        sib = (x, y, 1 - c)
        for a in range(nb):
            for k, chip in enumerate(chips):
                slab = outs[a].at[2 * chip[0] + chip[1], rows(a, c)]
                _rcopy(slab, slab, send_sems, recv_sems, 6 * a + k, sib).wait_recv()
                _rcopy(slab, slab, send_sems, recv_sems, 6 * a + 3 + k, sib).start()

    def finish(ins, outs, send_sems, recv_sems):
        x, y, c, chips = _place()
        q = 2 * x + y
        sib = (x, y, 1 - c)
        for a in range(nb):
            for k, chip in enumerate(chips):
                qk = 2 * chip[0] + chip[1]
                other = outs[a].at[qk, rows(a, 1 - c)]
                _rcopy(other, other, send_sems, recv_sems, 6 * a + 3 + k, sib).wait_recv()
                mine = outs[a].at[q, rows(a, c)]
                _rcopy(mine, mine, send_sems, recv_sems, 6 * a + k, sib).wait_send()
                fwd = outs[a].at[qk, rows(a, c)]
                _rcopy(fwd, fwd, send_sems, recv_sems, 6 * a + 3 + k, sib).wait_send()

    return dict(arrays=list(bigs), out_shape=[jax.ShapeDtypeStruct(b.shape, b.dtype) for b in bigs],
                aliases={a: a for a in range(nb)}, nsem=6 * nb, start=start, mid=mid, finish=finish)


def _hosted_rs_chips(ps):
    na = len(ps)

    def copies(ins, outs, send_sems, recv_sems):
        x, y, c, chips = _place()
        return [_rcopy(ins[a].at[2 * chip[0] + chip[1]], outs[a].at[k], send_sems, recv_sems, 3 * a + k,
                       (chip[0], chip[1], c)) for a in range(na) for k, chip in enumerate(chips)]

    def start(ins, outs, send_sems, recv_sems):
        for cp in copies(ins, outs, send_sems, recv_sems):
            cp.start()

    def finish(ins, outs, send_sems, recv_sems):
        for cp in copies(ins, outs, send_sems, recv_sems):
            cp.wait()

    return dict(arrays=list(ps), out_shape=[jax.ShapeDtypeStruct((3,) + p.shape[1:], p.dtype) for p in ps],
                aliases={}, nsem=3 * na, start=start, mid=None, finish=finish)


def _hosted_rs_sibling(gs):
    na = len(gs)

    def copies(ins, outs, send_sems, recv_sems):
        x, y, c, _ = _place()
        halves = [g.shape[1] // 2 for g in gs]
        return [_rcopy(ins[a].at[:, pl.ds((1 - c) * halves[a], halves[a]), :], outs[a], send_sems, recv_sems, a,
                       (x, y, 1 - c)) for a in range(na)]

    def start(ins, outs, send_sems, recv_sems):
        for cp in copies(ins, outs, send_sems, recv_sems):
            cp.start()

    def finish(ins, outs, send_sems, recv_sems):
        for cp in copies(ins, outs, send_sems, recv_sems):
            cp.wait()

    return dict(arrays=list(gs), aliases={}, nsem=na, start=start, mid=None, finish=finish,
                out_shape=[jax.ShapeDtypeStruct((N_CHIPS, g.shape[1] // 2, g.shape[2]), g.dtype) for g in gs])


def _hosted_share_sibling(fs):
    na = len(fs)

    def rows(a, c):
        half = fs[a].shape[0] // 2
        return pl.ds(c * half, half)

    def start(ins, outs, send_sems, recv_sems):
        x, y, c, _ = _place()
        for a in range(na):
            _rcopy(outs[a].at[rows(a, c)], outs[a].at[rows(a, c)], send_sems, recv_sems, a, (x, y, 1 - c)).start()

    def finish(ins, outs, send_sems, recv_sems):
        x, y, c, _ = _place()
        for a in range(na):
            _rcopy(outs[a].at[rows(a, c)], outs[a].at[rows(a, c)], send_sems, recv_sems, a, (x, y, 1 - c)).wait_send()
            other = outs[a].at[rows(a, 1 - c)]
            _rcopy(other, other, send_sems, recv_sems, a, (x, y, 1 - c)).wait_recv()

    return dict(arrays=list(fs), out_shape=[jax.ShapeDtypeStruct(f.shape, f.dtype) for f in fs],
                aliases={a: a for a in range(na)}, nsem=na, start=start, mid=None, finish=finish)


def _hosted_allgather(buf):
    def copies(ins, outs, send_sems, recv_sems):
        x, y, c, _ = _place()
        me = 4 * x + 2 * y + c
        cps = []
        for k in range(1, 8):
            dev = (1 - x if k & 4 else x, 1 - y if k & 2 else y, 1 - c if k & 1 else c)
            cps.append(_rcopy(ins[0], outs[0].at[me], send_sems, recv_sems, k - 1, dev))
        return pltpu.make_async_copy(ins[0], outs[0].at[me], send_sems.at[7]), cps

    def start(ins, outs, send_sems, recv_sems):
        loc, cps = copies(ins, outs, send_sems, recv_sems)
        loc.start()
        for cp in cps:
            cp.start()

    def finish(ins, outs, send_sems, recv_sems):
        loc, cps = copies(ins, outs, send_sems, recv_sems)
        loc.wait()
        for cp in cps:
            cp.wait()

    return dict(arrays=[buf], out_shape=[jax.ShapeDtypeStruct((8,) + buf.shape, buf.dtype)], aliases={},
                nsem=8, start=start, mid=None, finish=finish)


class _SemWindow:
    def __init__(self, sems, off):
        self._sems, self._off = sems, off

    @property
    def at(self):
        return self

    def __getitem__(self, k):
        return self._sems.at[k + self._off]


def _hosted_join(parts):
    arrays, out_shape, aliases, spans, nsem = [], [], {}, [], 0
    for h in parts:
        spans.append((len(arrays), len(h['arrays']), len(out_shape), len(h['out_shape']), nsem))
        aliases.update({len(arrays) + a: len(out_shape) + b for a, b in h['aliases'].items()})
        arrays += h['arrays']
        out_shape += h['out_shape']
        nsem += h['nsem']

    def phase(name):
        if all(h[name] is None for h in parts):
            return None

        def run(ins, outs, send_sems, recv_sems):
            for h, (ia, na, io, no, s0) in zip(parts, spans):
                if h[name] is not None:
                    h[name](ins[ia:ia + na], outs[io:io + no], _SemWindow(send_sems, s0), _SemWindow(recv_sems, s0))

        return run

    return dict(arrays=arrays, out_shape=out_shape, aliases=aliases, nsem=nsem,
                start=phase('start'), mid=phase('mid'), finish=phase('finish'))


def _run_hosted(hosted, name):
    nh_in, nh_out = len(hosted['arrays']), len(hosted['out_shape'])

    def body(*refs):
        ins, outs, sems = refs[:nh_in], refs[nh_in:nh_in + nh_out], refs[-2:]
        for ph in ('start', 'mid', 'finish'):
            if hosted[ph] is not None:
                hosted[ph](ins, outs, sems[0], sems[1])

    h_in, h_out, h_shape, h_scratch, h_alias = _host_plumbing(hosted, 0, 0)
    return pl.pallas_call(body, name=name, in_specs=h_in, out_specs=h_out, out_shape=h_shape,
                          scratch_shapes=h_scratch, input_output_aliases=h_alias)(*hosted['arrays'])


def _host_plumbing(hosted, n_in, n_out):
    nh = len(hosted['arrays'])
    return ([ANY] * nh, [ANY] * len(hosted['out_shape']), list(hosted['out_shape']),
            [pltpu.SemaphoreType.DMA((hosted['nsem'],)), pltpu.SemaphoreType.DMA((hosted['nsem'],))],
            {n_in + a: n_out + b for a, b in hosted['aliases'].items()})


def _host_phase(hosted, phase, when, hins, houts, sems):
    fn = hosted[phase]
    if fn is None:
        return

    @pl.when(when)
    def _():
        fn(hins, houts, sems[0], sems[1])


def _gather_weights(bigs, smalls):
    nb, ns = len(bigs), len(smalls)
    na = nb + ns
    nsem = 6 * nb + 3 * ns
    big = _hosted_gather(bigs)

    def body(*refs):
        ins, outs = refs[:na], refs[na:2 * na]
        send_sems, recv_sems, loc_sems = refs[2 * na:]
        x, y, c, chips = _place()
        q = 2 * x + y
        sib = (x, y, 1 - c)
        locs, sends = [], []
        for s in range(ns):
            cp = pltpu.make_async_copy(ins[nb + s], outs[nb + s].at[q], loc_sems.at[s])
            cp.start()
            locs.append(cp)
        big['start'](ins[:nb], outs[:nb], send_sems, recv_sems)
        for s in range(ns):
            a = nb + s
            for k, chip in enumerate(chips):
                cp = _rcopy(ins[a], outs[a].at[q], send_sems, recv_sems, 6 * nb + 3 * s + k,
                            (chip[0], chip[1], c))
                cp.start()
                sends.append(cp)
        big['mid'](ins[:nb], outs[:nb], send_sems, recv_sems)
        big['finish'](ins[:nb], outs[:nb], send_sems, recv_sems)
        for s in range(ns):
            a = nb + s
            for k, chip in enumerate(chips):
                qk = 2 * chip[0] + chip[1]
                _rcopy(ins[a], outs[a].at[qk], send_sems, recv_sems, 6 * nb + 3 * s + k, sib).wait_recv()
        for cp in sends:
            cp.wait_send()
        for cp in locs:
            cp.wait()

    arrs = list(bigs) + list(smalls)
    out_shape = ([jax.ShapeDtypeStruct(a.shape, a.dtype) for a in bigs]
                 + [jax.ShapeDtypeStruct((N_CHIPS,) + a.shape, a.dtype) for a in smalls])
    return pl.pallas_call(
        body, name="gather_weights", in_specs=[ANY] * na, out_specs=[ANY] * na, out_shape=out_shape,
        input_output_aliases={a: a for a in range(nb)},
        scratch_shapes=[pltpu.SemaphoreType.DMA((nsem,)), pltpu.SemaphoreType.DMA((nsem,)),
                        pltpu.SemaphoreType.DMA((max(ns, 1),))],
    )(*arrs)


def _row_tile(rows, cols, nbuf):
    budget = 24 * 1024 * 1024 // (nbuf * cols * 4 * 2)
    return _tile(rows, max(16, budget - budget % 16), 16) if rows % 16 == 0 else rows


def _add_pairs(g, rcv, name):
    s, half, c = rcv.shape
    tr = _row_tile(half, c, 3)
    nh = half // tr

    def body(a_ref, b_ref, o_ref):
        o_ref[...] = (a_ref[...].astype(F32) + b_ref[...].astype(F32)).astype(o_ref.dtype)

    blk = pl.BlockSpec((None, tr, c), lambda j, i: (j, i, 0))
    mine = pl.BlockSpec((None, tr, c), lambda j, i: (j, lax.axis_index("c") * nh + i, 0))
    return pl.pallas_call(
        body, name=name, grid=(s, nh), in_specs=[mine, blk], out_specs=blk,
        out_shape=jax.ShapeDtypeStruct(rcv.shape, rcv.dtype), compiler_params=_cparams(("parallel", "parallel")),
    )(g, rcv)


def _sum_chips(part, rcv, name):
    _, half, c = part.shape
    tr = _row_tile(half, c, 5)
    nh = half // tr

    def body(o_ref, r_ref, out_ref):
        acc = o_ref[...].astype(F32)
        for k in range(3):
            acc = acc + r_ref[k].astype(F32)
        out_ref[...] = acc

    return pl.pallas_call(
        body, name=name, grid=(nh,),
        in_specs=[pl.BlockSpec((None, tr, c), lambda i: (_my_chip(), i, 0)),
                  pl.BlockSpec((3, tr, c), lambda i: (0, i, 0))],
        out_specs=pl.BlockSpec((tr, c), lambda i: (lax.axis_index("c") * nh + i, 0)),
        out_shape=jax.ShapeDtypeStruct((2 * half, c), F32), compiler_params=_cparams(("parallel",)),
    )(part, rcv)


def _sum_devices(parts, name):
    _, n, _ = parts.shape
    tr = n if n <= 4096 else _tile(n, 512, 8)

    def body(p_ref, o_ref):
        acc = p_ref[0]
        for k in range(1, 8):
            acc = acc + p_ref[k]
        o_ref[...] = acc

    return pl.pallas_call(
        body, name=name, grid=(n // tr,),
        in_specs=[pl.BlockSpec((8, tr, 128), lambda i: (0, i, 0))],
        out_specs=pl.BlockSpec((tr, 128), lambda i: (i, 0)),
        out_shape=jax.ShapeDtypeStruct((n, 128), F32), compiler_params=_cparams(("parallel",)),
    )(parts)


def _adamw(w, g, m, v, name):
    r, c = w.shape
    tr = _row_tile(r, c, 7)
    c1 = 1.0 - ADAM_B1 ** ADAM_STEP
    c2 = 1.0 - ADAM_B2 ** ADAM_STEP

    def body(w_ref, g_ref, m_ref, v_ref, d_ref, mo_ref, vo_ref):
        gv = g_ref[...]
        mn = ADAM_B1 * m_ref[...] + (1.0 - ADAM_B1) * gv
        vn = ADAM_B2 * v_ref[...] + (1.0 - ADAM_B2) * (gv * gv)
        mo_ref[...] = mn
        vo_ref[...] = vn
        m_hat = mn / c1
        v_hat = vn / c2
        d_ref[...] = -ADAM_LR * (m_hat / (jnp.sqrt(v_hat) + ADAM_EPS) + ADAM_WD * w_ref[...])

    blk = pl.BlockSpec((tr, c), lambda i: (i, 0))
    sh = jax.ShapeDtypeStruct((r, c), F32)
    return pl.pallas_call(
        body, name=name, grid=(r // tr,), in_specs=[blk] * 4, out_specs=[blk] * 3, out_shape=[sh] * 3,
        compiler_params=_cparams(("parallel",)),
    )(w, g, m, v)


WEIGHTS = ['norm_mix_g', 'w_in', 'ln_a_g', 'ln_a_b', 'w_s', 'b_s', 'norm_a_g', 'conv_ssm_w', 'conv_ssm_b',
           'dt_bias', 'a_log', 'd_skip', 'ssm_norm_g', 'w_out', 'norm_ffn_g', 'w_up', 'conv_ffn_w',
           'conv_ffn_b', 'w_down', 'norm_ple_g', 'w_ple_gate', 'w_ple', 'norm_final_g']
BIG = ['w_in', 'w_out', 'w_up', 'w_down', 'w_ple_gate', 'w_ple']
SMALL = [n for n in WEIGHTS if n not in BIG]
PACK_ALIGN = 2048


def _pack(arrs):
    parts = []
    for a in arrs:
        f = a.reshape(-1).astype(F32)
        parts.append(jnp.pad(f, (0, (-f.shape[0]) % PACK_ALIGN)))
    return jnp.concatenate(parts).reshape(-1, 128)


def _unpack(buf, shapes):
    flat = buf.reshape(-1)
    out, off = [], 0
    for s in shapes:
        n = math.prod(s)
        out.append(flat[off:off + n].reshape(s))
        off += n + (-n) % PACK_ALIGN
    return out


def _pad_heads(v):
    return jnp.pad(v, ((0, 0), (0, HPAD - v.shape[1])))


def _col_sharded(full):
    r, c4 = full.shape
    return jnp.transpose(full.reshape(r, N_CHIPS, c4 // N_CHIPS), (1, 0, 2))


def _from_col_sharded(g):
    s, r, c = g.shape
    return jnp.transpose(g, (1, 0, 2)).reshape(r, s * c)


def kernel(x, p, norm_mix_g, w_in, ln_a_g, ln_a_b, w_s, b_s, norm_a_g, conv_ssm_w, conv_ssm_b, dt_bias, a_log, d_skip, ssm_norm_g, w_out, norm_ffn_g, w_up, conv_ffn_w, conv_ffn_b, w_down, norm_ple_g, w_ple_gate, w_ple, norm_final_g, loss_target, m_norm_mix_g, m_w_in, m_ln_a_g, m_ln_a_b, m_w_s, m_b_s, m_norm_a_g, m_conv_ssm_w, m_conv_ssm_b, m_dt_bias, m_a_log, m_d_skip, m_ssm_norm_g, m_w_out, m_norm_ffn_g, m_w_up, m_conv_ffn_w, m_conv_ffn_b, m_w_down, m_norm_ple_g, m_w_ple_gate, m_w_ple, m_norm_final_g, v_norm_mix_g, v_w_in, v_ln_a_g, v_ln_a_b, v_w_s, v_b_s, v_norm_a_g, v_conv_ssm_w, v_conv_ssm_b, v_dt_bias, v_a_log, v_d_skip, v_ssm_norm_g, v_w_out, v_norm_ffn_g, v_w_up, v_conv_ffn_w, v_conv_ffn_b, v_w_down, v_norm_ple_g, v_w_ple_gate, v_w_ple, v_norm_final_g):
    given = dict(locals())
    wts = {n: given[n] for n in WEIGHTS}
    mom = {n: given['m_' + n] for n in WEIGHTS}
    var = {n: given['v_' + n] for n in WEIGHTS}
    d = D_MODEL
    xt, pt, tgt = x[0], p[0, 0], loss_target[0]
    chip = 2 * lax.axis_index("x") + lax.axis_index("y")

    slots = {n: _cast_into_slot(wts[n][0], "cast_" + n) for n in BIG}
    g_in, g_cs, g_cf = _gather_weights([slots['w_in']], [conv_ssm_w[0], conv_ffn_w[0]])
    later = [n for n in BIG if n != 'w_in']
    w_in_full = _from_col_sharded(g_in)
    w_main = w_in_full
    w_dt = _pad_heads(w_in_full[:, D_MAIN:])
    cs_w = _from_col_sharded(g_cs)
    cf_w = _from_col_sharded(g_cf)
    consts = _ssd_consts()
    dtb, alog = _pad_heads(dt_bias), _pad_heads(a_log)
    de = jnp.repeat(d_skip[0], HEAD_DIM)[None, :]
    b_exp = jnp.broadcast_to(b_s[0][:, :, None], (N_GROUPS_A, CHUNK, CHUNK))

    a1 = _rms_fwd(xt, norm_mix_g, "rms_mix")
    proj = _matmul(a1, w_main, mode='nn', name="mm_proj", tm=1024, tn=1024, b_cols=D_MAIN)
    dt_raw = _matmul(a1, w_dt, mode='nn', name="mm_dt", tm=1024, tn=128)
    yab = _gmlp_fwd(proj, ln_a_g, ln_a_b, w_s[0], b_exp, norm_a_g)
    yab, ysave, hs, *gathered = _ssd_fwd(proj, dt_raw, yab, cs_w, conv_ssm_b, dtb, alog, de, ssm_norm_g, consts,
                                         _hosted_gather([slots[n] for n in later]))
    g_out, g_up, g_down, g_pg, g_ple = gathered
    w_out_f = g_out.reshape(D_MIX, d)
    w_down_f = g_down.reshape(D_FF, d)
    w_pg_f = g_pg.reshape(d, d)
    h1 = _matmul(yab, w_out_f, mode='nn', name="mm_out", res=xt, tm=512, tn=1024, tk=4096)
    f = _rms_fwd(h1, norm_ffn_g, "rms_ffn")
    hid = _matmul(f, g_up, mode='nn', name="mm_up", b_sharded=True, tm=1024, tn=1408)
    act, conv_g, conv_u = _ffn_act_fwd(hid, cf_w, conv_ffn_b)
    h2 = _matmul(act, w_down_f, mode='nn', name="mm_down", res=h1, tm=1024, tn=1024, tk=2816)
    n3 = _rms_fwd(h2, norm_ple_g, "rms_ple")
    gl = _matmul(n3, w_pg_f, mode='nn', name="mm_pg", tm=1024, tn=1024)
    pe = _matmul(pt, g_ple, mode='nn', name="mm_ple", b_sharded=True, tm=1024, tn=512)
    dh3, dgl, dpe, lossv, dgf = _tail(h2, gl, pe, tgt, norm_final_g[None, :])

    gs_ple = _matmul(pt, dpe, mode='tn', name="mm_dw_ple", out_dtype=BF16, out_shards=N_CHIPS,
                     tm=256, tn=512, tk=2048)
    gs_pg = _matmul(n3, dgl, mode='tn', name="mm_dw_pg", out_dtype=BF16, tm=1024, tn=1024, tk=4096)
    dn3 = _matmul(dgl, w_pg_f, mode='nt', name="mm_dn3", out_dtype=BF16, tm=1024, tn=1024)
    dh2, dg_ple, dh2_b = _rms_bwd(h2, norm_ple_g, dn3, dh3, "rms_ple_bwd", True)
    dact = _matmul(dh2_b, w_down_f, mode='nt', name="mm_dact", out_dtype=BF16, tm=1024, tn=1408)
    gs_down = _matmul(act, dh2_b, mode='tn', name="mm_dw_down", out_dtype=BF16, tm=1408, tn=1024, tk=2048)
    dpg, dpu, wg_acc, wu_acc = _ffn_act_bwd(hid, cf_w, conv_g, conv_u, dact)
    hc = N_CHIPS // 2
    gs_up = jnp.concatenate(
        [_matmul(f, dpg, mode='tn', name="mm_dw_up_g", out_dtype=BF16, out_shards=hc, tm=1024, tn=1408, tk=2048),
         _matmul(f, dpu, mode='tn', name="mm_dw_up_u", out_dtype=BF16, out_shards=hc, tm=1024, tn=1408, tk=2048)],
        axis=0)
    early = [gs_up, gs_down.reshape(N_CHIPS, D_FF // N_CHIPS, d), gs_pg.reshape(N_CHIPS, d // N_CHIPS, d), gs_ple]
    df, *sib_e = _matmul(dpg, g_up, mode='nt', name="mm_df_g", b_sharded=True, tm=1024, tn=1024, tk=2816,
                         hosted=_hosted_rs_sibling(early))
    part_e = [_add_pairs(a, b, "rs_add_e%d" % i) for i, (a, b) in enumerate(zip(early, sib_e))]
    df = _matmul(dpu, g_up, mode='nt', name="mm_df_u", b_sharded=True, b_shard_off=hc, res=df, out_dtype=BF16,
                 tm=1024, tn=1024, tk=2816)
    dh1, dg_ffn, dh1_b = _rms_bwd(h1, norm_ffn_g, df, dh2, "rms_ffn_bwd", True)
    dyab = _matmul(dh1_b, w_out_f, mode='nt', name="mm_dyab", out_dtype=BF16, tm=1024, tn=1024)
    gs_out = _matmul(yab, dh1_b, mode='tn', name="mm_dw_out", out_dtype=BF16, tm=1024, tn=1024, tk=4096)
    duv, dws, dbs, dlng, dlnb, dnag = _gmlp_bwd(proj, dyab, ln_a_g, ln_a_b, w_s[0], b_exp, norm_a_g)
    dproj, ddt_raw, acc_x, acc_b, acc_c, acc_gain, acc_head, *rcv_e = _ssd_bwd(
        proj, dt_raw, dyab, ysave, hs, duv, cs_w, conv_ssm_b, dtb, alog, de, ssm_norm_g, consts,
        _hosted_rs_chips(part_e))
    dw_main = _matmul(a1, dproj, mode='tn', name="mm_dw_main", out_dtype=BF16, tm=1024, tn=1024, tk=4096)
    dw_dt = _matmul(a1, ddt_raw, mode='tn', name="mm_dw_dt", out_dtype=BF16, tm=1024, tn=128, tk=2048)
    gs_in = _col_sharded(jnp.concatenate([dw_main, dw_dt[:, :N_HEADS]], axis=1))
    late = [gs_in, gs_out.reshape(N_CHIPS, D_MIX // N_CHIPS, d)]
    part_l = [_add_pairs(a, b, "rs_add_l%d" % i) for i, (a, b) in enumerate(
        zip(late, _run_hosted(_hosted_rs_sibling(late), "rs_sibling_late")))]

    def conv_rows(acc, k):
        return acc[:, k, :].reshape(1, -1)

    dcw = jnp.concatenate([jnp.concatenate([conv_rows(acc_x, k), conv_rows(acc_b, k), conv_rows(acc_c, k)], axis=1)
                           for k in range(SSM_CONV)], axis=0)
    dcb = jnp.concatenate([conv_rows(acc_x, SSM_CONV), conv_rows(acc_b, SSM_CONV), conv_rows(acc_c, SSM_CONV)], axis=1)
    part = {
        'ln_a_g': dlng, 'ln_a_b': dlnb, 'w_s': dws, 'b_s': dbs, 'norm_a_g': dnag,
        'conv_ssm_w': dcw, 'conv_ssm_b': dcb,
        'dt_bias': acc_head[0:1, :N_HEADS], 'a_log': acc_head[1:2, :N_HEADS], 'd_skip': acc_head[2:3, :N_HEADS],
        'ssm_norm_g': acc_gain[:, 0, :], 'norm_ffn_g': dg_ffn,
        'conv_ffn_w': jnp.concatenate([wg_acc[:FFN_CONV], wu_acc[:FFN_CONV]], axis=1),
        'conv_ffn_b': jnp.concatenate([wg_acc[FFN_CONV:FFN_CONV + 1], wu_acc[FFN_CONV:FFN_CONV + 1]], axis=1),
        'norm_ple_g': dg_ple, 'norm_final_g': dgf,
    }
    full_shapes = {n: wts[n].shape for n in SMALL}
    full_shapes['conv_ssm_w'] = (1, SSM_CONV, D_XBC)
    full_shapes['conv_ffn_w'] = (1, FFN_CONV, 2 * D_FF)
    small_e = [n for n in SMALL if n != 'norm_mix_g']
    packed = _pack([part[n] for n in small_e] + [lossv[:, 0:1]])

    halves_e = [_sum_chips(a, b, "rs_sum_e%d" % i) for i, (a, b) in enumerate(zip(part_e, rcv_e))]
    da_dt = _matmul(ddt_raw, w_dt, mode='nt', name="mm_da_dt", tm=1024, tn=1024)
    da, *moved = _matmul(dproj, w_main, mode='nt', name="mm_da", res=da_dt, out_dtype=BF16, tm=1024, tn=1024, tk=2560,
                         hosted=_hosted_join([_hosted_rs_chips(part_l), _hosted_share_sibling(halves_e),
                                              _hosted_allgather(packed)]))
    rcv_l, g_early, gathered = moved[:len(late)], moved[len(late):-1], moved[-1]
    dx, dg_mix = _rms_bwd(xt, norm_mix_g, da, dh1, "rms_mix_bwd", False)

    halves_l = [_sum_chips(a, b, "rs_sum_l%d" % i) for i, (a, b) in enumerate(zip(part_l, rcv_l))]
    g_big = dict(zip(['w_up', 'w_down', 'w_ple_gate', 'w_ple'], g_early))
    g_big.update(zip(['w_in', 'w_out'], _run_hosted(_hosted_share_sibling(halves_l), "share_sibling_late")))

    pieces = _unpack(_sum_devices(gathered, "sum_devices"), [full_shapes[n] for n in small_e] + [(1,)])
    g_small = dict(zip(small_e, pieces[:-1]))
    loss = pieces[-1][0]
    mix = _sum_devices(_run_hosted(_hosted_allgather(_pack([dg_mix])), "allgather_mix")[0], "sum_devices_mix")
    g_small['norm_mix_g'] = _unpack(mix, [full_shapes['norm_mix_g']])[0]
    for n in ('conv_ssm_w', 'conv_ffn_w'):
        width = wts[n].shape[2]
        g_small[n] = lax.dynamic_slice_in_dim(g_small[n], chip * width, width, axis=2)

    grads, delta, new_m, new_v = {}, {}, {}, {}
    for n in BIG:
        shp = wts[n].shape
        dl, mn, vn = _adamw(wts[n][0], g_big[n], mom[n][0], var[n][0], "adamw_" + n)
        grads[n], delta[n], new_m[n], new_v[n] = (g_big[n].reshape(shp), dl.reshape(shp), mn.reshape(shp),
                                                  vn.reshape(shp))
    shapes = [wts[n].shape for n in SMALL]
    dl, mn, vn = _adamw(_pack([wts[n] for n in SMALL]), _pack([g_small[n] for n in SMALL]),
                        _pack([mom[n] for n in SMALL]), _pack([var[n] for n in SMALL]), "adamw_small")
    for n, a, b, c in zip(SMALL, _unpack(dl, shapes), _unpack(mn, shapes), _unpack(vn, shapes)):
        grads[n], delta[n], new_m[n], new_v[n] = g_small[n], a, b, c

    return (loss, dx[None], *[grads[n] for n in WEIGHTS], *[delta[n] for n in WEIGHTS],
            *[new_m[n] for n in WEIGHTS], *[new_v[n] for n in WEIGHTS])
```

```python
import functools
import math

import jax
import jax.numpy as jnp
from jax import lax
from jax.experimental import pallas as pl
from jax.experimental.pallas import tpu as pltpu

D_MODEL = 2048
SEQ = 8192
D_MIX = 2 * D_MODEL
D_A = D_MIX // 2
CHUNK = 128
N_GROUPS_A = D_A // 128
D_SSM = D_MIX - D_A
HEAD_DIM = 64
N_HEADS = D_SSM // HEAD_DIM
HEADS_PER_GROUP = 4
N_SSM_GROUPS = N_HEADS // HEADS_PER_GROUP
GW = HEADS_PER_GROUP * HEAD_DIM
D_STATE = 128
SSM_CONV = 4
D_BC = N_SSM_GROUPS * D_STATE
D_XBC = D_SSM + 2 * D_BC
D_MAIN = 2 * D_A + D_SSM + D_XBC
D_IN = D_MAIN + N_HEADS
D_FF = (D_MODEL * 11) // 4
FFN_CONV = 3
D_PLE = 256
EPS = 1e-6
HPAD = 128
N_CHIPS = 4

ADAM_LR = 0.001
ADAM_B1 = 0.9
ADAM_B2 = 0.999
ADAM_EPS = 1e-08
ADAM_WD = 0.01
ADAM_STEP = 10

F32 = jnp.float32
BF16 = jnp.bfloat16
MESH = pl.DeviceIdType.MESH
VMEM_LIMIT = 56 * 1024 * 1024


def _cparams(sem):
    return pltpu.CompilerParams(dimension_semantics=sem, vmem_limit_bytes=VMEM_LIMIT)


def _tile(n, pref, mult):
    t = min(pref, n)
    t -= t % mult
    while n % t:
        t -= mult
    return t


def _dot(a, b):
    return jnp.dot(a, b, preferred_element_type=F32)


def _dot_nt(a, b):
    return lax.dot_general(a, b, (((1,), (1,)), ((), ())), preferred_element_type=F32)


def _dot_tn(a, b):
    return lax.dot_general(a, b, (((0,), (0,)), ((), ())), preferred_element_type=F32)


def _split3(x):
    hi = x.astype(BF16)
    r = x - hi.astype(F32)
    mid = r.astype(BF16)
    lo = (r - mid.astype(F32)).astype(BF16)
    return hi, mid, lo


def _x01(x, e):
    h, m, l = _split3(x)
    return _dot(h, e) + _dot(m, e) + _dot(l, e)


def _x01_nt(x, e):
    h, m, l = _split3(x)
    return _dot_nt(h, e) + _dot_nt(m, e) + _dot_nt(l, e)


def _e01x(e, x):
    h, m, l = _split3(x)
    return _dot(e, h) + _dot(e, m) + _dot(e, l)


def _e01x_tn(e, x):
    h, m, l = _split3(x)
    return _dot_tn(e, h) + _dot_tn(e, m) + _dot_tn(e, l)


def _sigmoid(x):
    return 1.0 / (1.0 + jnp.exp(-x))


_GELU_C = math.sqrt(2.0 / math.pi)


def _gelu_and_grad(x):
    x2 = x * x
    th = jnp.tanh(_GELU_C * (x + 0.044715 * x * x2))
    y = 0.5 * x * (1.0 + th)
    dy = 0.5 * (1.0 + th) + 0.5 * x * (1.0 - th * th) * (_GELU_C * (1.0 + 3.0 * 0.044715 * x2))
    return y, dy


def _silu_and_grad(x):
    s = _sigmoid(x)
    return x * s, s * (1.0 + x * (1.0 - s))


def _softplus(x):
    u = jnp.exp(-jnp.abs(x))
    w = 1.0 + u
    l1p = jnp.where(w == 1.0, u, jnp.log(w) * (u / (w - 1.0)))
    return jnp.maximum(x, 0.0) + l1p


def _matmul(a, b, *, mode, name, out_dtype=F32, res=None, tm=512, tn=512, tk=2048,
            b_sharded=False, b_shard_off=0, b_cols=None, out_shards=0, hosted=None):
    if mode == 'tn':
        kdim, m = a.shape
        n = b.shape[1]
    else:
        m, kdim = a.shape
        if b_sharded:
            s_b, d1, d2 = b.shape
            n = s_b * d2 if mode == 'nn' else d1
        else:
            n = b.shape[1] if mode == 'nn' else b.shape[0]
        if b_cols is not None:
            n = b_cols
    per = None
    if b_sharded:
        per = b.shape[2]
    if out_shards:
        per = n // out_shards
    tm = _tile(m, tm, 128 if mode == 'tn' else 8)
    if mode == 'nt' and b_sharded:
        tn = _tile(n, tn, 128)
        tk = _tile(per, tk, 128)
    elif per is not None:
        tn = _tile(per, tn, 128)
        tk = _tile(kdim, tk, 128 if mode != 'tn' else 8)
    else:
        tn = _tile(n, tn, 128)
        tk = _tile(kdim, tk, 128 if mode != 'tn' else 8)
    nm, nn_, nk = m // tm, n // tn, kdim // tk
    has_res = res is not None
    n_in = 2 + has_res
    nh_in = len(hosted['arrays']) if hosted else 0
    nh_out = len(hosted['out_shape']) if hosted else 0

    def body(*refs):
        a_ref, b_ref = refs[0], refs[1]
        res_ref = refs[2] if has_res else None
        o_ref = refs[n_in + nh_in]
        if hosted:
            hins = refs[n_in:n_in + nh_in]
            houts = refs[n_in + nh_in + 1:n_in + nh_in + 1 + nh_out]
            sems = refs[-2:]
            ids = [pl.program_id(d) for d in range(3)]
            step = (ids[0] * nm + ids[1]) * nk + ids[2]
            n_steps = nn_ * nm * nk
            at_first, at_mid, at_last = step == 0, step == (7 * n_steps) // 8, step == n_steps - 1
            _host_phase(hosted, 'start', at_first, hins, houts, sems)
        av = a_ref[...].astype(BF16)
        bv = b_ref[...].astype(BF16)
        if mode == 'nn':
            p = _dot(av, bv)
        elif mode == 'nt':
            p = _dot_nt(av, bv)
        else:
            p = _dot_tn(av, bv)

        def fin(v):
            if has_res:
                v = v + res_ref[...]
            o_ref[...] = v.astype(o_ref.dtype)

        if nk == 1:
            fin(p)
        else:
            acc_ref = refs[n_in + nh_in + 1 + nh_out]
            k = pl.program_id(2)

            @pl.when(k == 0)
            def _():
                acc_ref[...] = p

            @pl.when(k > 0)
            def _():
                acc_ref[...] += p

            @pl.when(k == nk - 1)
            def _():
                fin(acc_ref[...])
        if hosted:
            _host_phase(hosted, 'mid', at_mid, hins, houts, sems)
            _host_phase(hosted, 'finish', at_last, hins, houts, sems)

    if mode == 'nn':
        a_spec = pl.BlockSpec((tm, tk), lambda j, i, k: (i, k))
        if b_sharded:
            nps = per // tn
            b_spec = pl.BlockSpec((None, tk, tn), lambda j, i, k: (j // nps, k, j % nps))
        else:
            b_spec = pl.BlockSpec((tk, tn), lambda j, i, k: (k, j))
    elif mode == 'nt':
        a_spec = pl.BlockSpec((tm, tk), lambda j, i, k: (i, k))
        if b_sharded:
            kps = per // tk
            b_spec = pl.BlockSpec((None, tn, tk), lambda j, i, k: (k // kps + b_shard_off, j, k % kps))
        else:
            b_spec = pl.BlockSpec((tn, tk), lambda j, i, k: (j, k))
    else:
        a_spec = pl.BlockSpec((tk, tm), lambda j, i, k: (k, i))
        b_spec = pl.BlockSpec((tk, tn), lambda j, i, k: (k, j))
    in_specs = [a_spec, b_spec]
    args = [a, b]
    if has_res:
        in_specs.append(pl.BlockSpec((tm, tn), lambda j, i, k: (i, j)))
        args.append(res)
    if out_shards:
        nps_o = per // tn
        out_shape = jax.ShapeDtypeStruct((out_shards, m, per), out_dtype)
        out_spec = pl.BlockSpec((None, tm, tn), lambda j, i, k: (j // nps_o, i, j % nps_o))
    else:
        out_shape = jax.ShapeDtypeStruct((m, n), out_dtype)
        out_spec = pl.BlockSpec((tm, tn), lambda j, i, k: (i, j))
    scratch = [pltpu.VMEM((tm, tn), F32)] if nk > 1 else []
    if not hosted:
        return pl.pallas_call(
            body, name=name, grid=(nn_, nm, nk), in_specs=in_specs, out_specs=out_spec,
            out_shape=out_shape, scratch_shapes=scratch,
            compiler_params=_cparams(("parallel", "parallel", "arbitrary")),
        )(*args)
    h_in, h_out, h_shape, h_scratch, h_alias = _host_plumbing(hosted, n_in, 1)
    return pl.pallas_call(
        body, name=name, grid=(nn_, nm, nk), in_specs=in_specs + h_in, out_specs=[out_spec] + h_out,
        out_shape=[out_shape] + h_shape, scratch_shapes=scratch + h_scratch, input_output_aliases=h_alias,
        compiler_params=_cparams(("arbitrary", "arbitrary", "arbitrary")),
    )(*args, *hosted['arrays'])


def _rms_fwd(x, g, name):
    t, d = x.shape
    tt = _tile(t, 512, 8)

    def body(x_ref, g_ref, o_ref):
        xv = x_ref[...]
        r = lax.rsqrt(jnp.mean(xv * xv, axis=-1, keepdims=True) + EPS)
        o_ref[...] = (xv * r * g_ref[...]).astype(o_ref.dtype)

    return pl.pallas_call(
        body, name=name, grid=(t // tt,),
        in_specs=[pl.BlockSpec((tt, d), lambda i: (i, 0)), pl.BlockSpec((1, d), lambda i: (0, 0))],
        out_specs=pl.BlockSpec((tt, d), lambda i: (i, 0)),
        out_shape=jax.ShapeDtypeStruct((t, d), BF16),
        compiler_params=_cparams(("parallel",)),
    )(x, g)


def _rms_bwd(x, g, dy, dres, name, also_bf16):
    t, d = x.shape
    tt = _tile(t, 256, 16)

    def body(x_ref, g_ref, dy_ref, dres_ref, dx_ref, dg_ref, *dxb):
        i = pl.program_id(0)
        xv = x_ref[...]
        r = lax.rsqrt(jnp.mean(xv * xv, axis=-1, keepdims=True) + EPS)
        xh = xv * r
        dyv = dy_ref[...].astype(F32)
        dxh = dyv * g_ref[...]
        c = jnp.mean(dxh * xh, axis=-1, keepdims=True)
        dx = dres_ref[...] + r * (dxh - xh * c)
        dx_ref[...] = dx
        for dxb_ref in dxb:
            dxb_ref[...] = dx.astype(BF16)
        part = jnp.sum(dyv * xh, axis=0, keepdims=True)

        @pl.when(i == 0)
        def _():
            dg_ref[...] = part

        @pl.when(i > 0)
        def _():
            dg_ref[...] += part

    row = pl.BlockSpec((tt, d), lambda i: (i, 0))
    vec = pl.BlockSpec((1, d), lambda i: (0, 0))
    return pl.pallas_call(
        body, name=name, grid=(t // tt,),
        in_specs=[row, vec, row, row], out_specs=[row, vec] + [row] * also_bf16,
        out_shape=[jax.ShapeDtypeStruct((t, d), F32), jax.ShapeDtypeStruct((1, d), F32)]
        + [jax.ShapeDtypeStruct((t, d), BF16)] * also_bf16,
        compiler_params=_cparams(("arbitrary",)),
    )(x, g, dy, dres)


def _tail(h2, gl, pe, target, gfin):
    t, d = h2.shape
    tt = _tile(t, 256, 8)

    def body(h2_ref, gl_ref, pe_ref, tg_ref, gf_ref, dh3_ref, dgl_ref, dpe_ref, loss_ref, dgf_ref):
        i = pl.program_id(0)
        sig = _sigmoid(gl_ref[...])
        pev = pe_ref[...]
        h3 = h2_ref[...] + sig * pev
        r = lax.rsqrt(jnp.mean(h3 * h3, axis=-1, keepdims=True) + EPS)
        xh = h3 * r
        gf = gf_ref[...]
        e = xh * gf - tg_ref[...]
        dy = e * (1.0 / d)
        dxh = dy * gf
        c = jnp.mean(dxh * xh, axis=-1, keepdims=True)
        dh3 = r * (dxh - xh * c)
        dh3_ref[...] = dh3
        dgl_ref[...] = (dh3 * pev * sig * (1.0 - sig)).astype(BF16)
        dpe_ref[...] = (dh3 * sig).astype(BF16)
        lpart = jnp.sum(e * e, axis=0, keepdims=True) * (0.5 / d)
        gpart = jnp.sum(dy * xh, axis=0, keepdims=True)

        @pl.when(i == 0)
        def _():
            loss_ref[...] = lpart
            dgf_ref[...] = gpart

        @pl.when(i > 0)
        def _():
            loss_ref[...] += lpart
            dgf_ref[...] += gpart

        @pl.when(i == t // tt - 1)
        def _():
            loss_ref[...] = jnp.broadcast_to(jnp.sum(loss_ref[...], axis=-1, keepdims=True), (1, d))

    row = pl.BlockSpec((tt, d), lambda i: (i, 0))
    vec = pl.BlockSpec((1, d), lambda i: (0, 0))
    return pl.pallas_call(
        body, name="tail", grid=(t // tt,),
        in_specs=[row, row, row, row, vec], out_specs=[row, row, row, vec, vec],
        out_shape=[jax.ShapeDtypeStruct((t, d), F32), jax.ShapeDtypeStruct((t, d), BF16),
                   jax.ShapeDtypeStruct((t, d), BF16), jax.ShapeDtypeStruct((1, d), F32),
                   jax.ShapeDtypeStruct((1, d), F32)],
        compiler_params=_cparams(("arbitrary",)),
    )(h2, gl, pe, target, gfin)


def _conv(cur_ref, prev_ref, w_ref, b_ref, ext_ref, first, width):
    rows = cur_ref.shape[0]
    ext_ref[0:8, :] = jnp.where(first, 0.0, prev_ref[...])
    ext_ref[8:8 + rows, :] = cur_ref[...]
    acc = b_ref[...]
    for k in range(width):
        acc = acc + w_ref[k:k + 1, :] * ext_ref[pl.ds(9 - width + k, rows), :]
    return acc


def _ffn_specs(t, tt, tc, nf):
    hb = tt // 8
    cur_g = pl.BlockSpec((tt, tc), lambda j, i: (i, j))
    cur_u = pl.BlockSpec((tt, tc), lambda j, i: (i, j + nf))
    prev_g = pl.BlockSpec((8, tc), lambda j, i: (jnp.maximum(i * hb - 1, 0), j))
    prev_u = pl.BlockSpec((8, tc), lambda j, i: (jnp.maximum(i * hb - 1, 0), j + nf))
    w_g = pl.BlockSpec((FFN_CONV, tc), lambda j, i: (0, j))
    w_u = pl.BlockSpec((FFN_CONV, tc), lambda j, i: (0, j + nf))
    b_g = pl.BlockSpec((1, tc), lambda j, i: (0, j))
    b_u = pl.BlockSpec((1, tc), lambda j, i: (0, j + nf))
    return [cur_g, prev_g, cur_u, prev_u, w_g, w_u, b_g, b_u]


FFN_TC = 512


def _shift_down(prev, cur, n, rid):
    return jnp.where(rid < n, pltpu.roll(prev, n, 0), pltpu.roll(cur, n, 0))


def _shift_up(cur, nxt, n, rid):
    return jnp.where(rid < 8 - n, pltpu.roll(cur, 8 - n, 0), pltpu.roll(nxt, 8 - n, 0))


def _conv3_group(prev, cur, w_ref, b_ref, rid):
    x1 = _shift_down(prev, cur, 1, rid)
    x2 = _shift_down(prev, cur, 2, rid)
    return b_ref[...] + w_ref[2:3, :] * cur + w_ref[1:2, :] * x1 + w_ref[0:1, :] * x2


def _ffn_act_fwd(hid, cw, cb):
    t = hid.shape[0]
    tt = _tile(t, 512, 16)
    tc = _tile(D_FF, FFN_TC, 128)
    nf = D_FF // tc

    def body(g_ref, gp_ref, u_ref, up_ref, wg_ref, wu_ref, bg_ref, bu_ref, o_ref, cg_ref, cu_ref):
        first = pl.program_id(1) == 0
        rid = lax.broadcasted_iota(jnp.int32, (8, tc), 0)

        def step(s, carry):
            pg, pu = carry
            r0 = pl.multiple_of(s * 16, 16)
            g0, g1 = g_ref[pl.ds(r0, 8), :], g_ref[pl.ds(r0 + 8, 8), :]
            u0, u1 = u_ref[pl.ds(r0, 8), :], u_ref[pl.ds(r0 + 8, 8), :]
            gate = jnp.concatenate([_conv3_group(pg, g0, wg_ref, bg_ref, rid),
                                    _conv3_group(g0, g1, wg_ref, bg_ref, rid)], axis=0)
            up = jnp.concatenate([_conv3_group(pu, u0, wu_ref, bu_ref, rid),
                                  _conv3_group(u0, u1, wu_ref, bu_ref, rid)], axis=0)
            cg_ref[pl.ds(r0, 16), :] = gate.astype(BF16)
            cu_ref[pl.ds(r0, 16), :] = up.astype(BF16)
            o_ref[pl.ds(r0, 16), :] = (gate * _sigmoid(gate) * up).astype(BF16)
            return g1, u1

        init = (jnp.where(first, 0.0, gp_ref[...]), jnp.where(first, 0.0, up_ref[...]))
        lax.fori_loop(0, tt // 16, step, init)

    blk = pl.BlockSpec((tt, tc), lambda j, i: (i, j))
    return pl.pallas_call(
        body, name="ffn_act_fwd", grid=(nf, t // tt), in_specs=_ffn_specs(t, tt, tc, nf),
        out_specs=[blk, blk, blk],
        out_shape=[jax.ShapeDtypeStruct((t, D_FF), BF16)] * 3,
        compiler_params=_cparams(("parallel", "arbitrary")),
    )(hid, hid, hid, hid, cw, cw, cb, cb)


def _ffn_act_bwd(hid, cw, conv_g, conv_u, dact):
    t = hid.shape[0]
    tt = _tile(t, 512, 16)
    tc = _tile(D_FF, FFN_TC, 128)
    nf = D_FF // tc
    nt = t // tt
    n16 = tt // 16

    def body(g_ref, u_ref, wg_ref, wu_ref, cg_ref, cgn_ref, cu_ref, cun_ref, da_ref, dan_ref,
             og_ref, ou_ref, ag_ref, au_ref, accs):
        i = pl.program_id(1)
        first, last = i == 0, i == nt - 1
        rid = lax.broadcasted_iota(jnp.int32, (8, tc), 0)
        accs[...] = jnp.zeros_like(accs)

        def dgroup(gate, up, da):
            sv, sgr = _silu_and_grad(gate)
            return da * up * sgr, da * sv

        def finish(x, d0, d1, w_ref):
            s1 = _shift_up(d0, d1, 1, rid)
            s2 = _shift_up(d0, d1, 2, rid)
            dpre = w_ref[2:3, :] * d0 + w_ref[1:2, :] * s1 + w_ref[0:1, :] * s2
            return dpre, (x * s2, x * s1, x * d0, d0)

        def two_groups(it, carry, gate_blk, up_blk, da_blk, zero_ahead):
            d0g, d0u, gate1, up1, da1 = carry
            r0 = it * 16 if isinstance(it, int) else pl.multiple_of(it * 16, 16)
            x0g, x0u = g_ref[pl.ds(r0, 8), :], u_ref[pl.ds(r0, 8), :]
            x1g, x1u = g_ref[pl.ds(r0 + 8, 8), :], u_ref[pl.ds(r0 + 8, 8), :]
            d1g, d1u = dgroup(gate1, up1, da1)
            d2g, d2u = dgroup(gate_blk[0:8], up_blk[0:8], da_blk[0:8])
            d2g = jnp.where(zero_ahead, 0.0, d2g)
            d2u = jnp.where(zero_ahead, 0.0, d2u)
            for half, (xa, xb, da_, db_, dc_, w_ref, o_ref) in enumerate((
                    (x0g, x1g, d0g, d1g, d2g, wg_ref, og_ref), (x0u, x1u, d0u, d1u, d2u, wu_ref, ou_ref))):
                pa, prods_a = finish(xa, da_, db_, w_ref)
                pb, prods_b = finish(xb, db_, dc_, w_ref)
                o_ref[pl.ds(r0, 16), :] = jnp.concatenate([pa, pb], axis=0).astype(BF16)
                for k in range(4):
                    accs[4 * half + k] += prods_a[k] + prods_b[k]
            return d2g, d2u, gate_blk[8:16], up_blk[8:16], da_blk[8:16]

        def rows16(ref, r):
            return ref[pl.ds(r, 16), :].astype(F32)

        def step(it, carry):
            r1 = pl.multiple_of(it * 16 + 16, 16)
            return two_groups(it, carry, rows16(cg_ref, r1), rows16(cu_ref, r1), rows16(da_ref, r1), False)

        g0, u0, da0 = rows16(cg_ref, 0), rows16(cu_ref, 0), rows16(da_ref, 0)
        d0g, d0u = dgroup(g0[0:8], u0[0:8], da0[0:8])
        carry = lax.fori_loop(0, n16 - 1, step, (d0g, d0u, g0[8:16], u0[8:16], da0[8:16]))
        two_groups(n16 - 1, carry, cgn_ref[...].astype(F32), cun_ref[...].astype(F32), dan_ref[...].astype(F32), last)

        @pl.when(first)
        def _():
            ag_ref[...] = jnp.zeros_like(ag_ref)
            au_ref[...] = jnp.zeros_like(au_ref)

        for half, a_ref in enumerate((ag_ref, au_ref)):
            for k in range(4):
                a_ref[k:k + 1, :] += jnp.sum(accs[4 * half + k], axis=0, keepdims=True)

    def nxt16(i):
        return jnp.minimum((i + 1) * n16, t // 16 - 1)

    out_blk = pl.BlockSpec((tt, tc), lambda j, i: (i, j))
    acc_spec = pl.BlockSpec((8, tc), lambda j, i: (0, j))
    nxt_blk = pl.BlockSpec((16, tc), lambda j, i: (nxt16(i), j))
    in_specs = [out_blk, pl.BlockSpec((tt, tc), lambda j, i: (i, j + nf)),
                pl.BlockSpec((FFN_CONV, tc), lambda j, i: (0, j)), pl.BlockSpec((FFN_CONV, tc), lambda j, i: (0, j + nf)),
                out_blk, nxt_blk, out_blk, nxt_blk, out_blk, nxt_blk]
    return pl.pallas_call(
        body, name="ffn_act_bwd", grid=(nf, nt), in_specs=in_specs,
        out_specs=[out_blk, out_blk, acc_spec, acc_spec],
        out_shape=[jax.ShapeDtypeStruct((t, D_FF), BF16), jax.ShapeDtypeStruct((t, D_FF), BF16),
                   jax.ShapeDtypeStruct((8, D_FF), F32), jax.ShapeDtypeStruct((8, D_FF), F32)],
        scratch_shapes=[pltpu.VMEM((8, 8, tc), F32)],
        compiler_params=_cparams(("parallel", "arbitrary")),
    )(hid, hid, cw, cw, conv_g, conv_g, conv_u, conv_u, dact, dact)


def _tri_mask():
    r = lax.broadcasted_iota(jnp.int32, (CHUNK, CHUNK), 0)
    c = lax.broadcasted_iota(jnp.int32, (CHUNK, CHUNK), 1)
    return r >= c


def _gmlp_group_fwd(uv_ref, lng_ref, lnb_ref, ws_ref, bexp_ref, tri, g, want_grad):
    lo, hi = g * 128, (g + 1) * 128
    u_pre = uv_ref[:, lo:hi]
    v_pre = uv_ref[:, D_A + lo:D_A + hi]
    u, du = _gelu_and_grad(u_pre)
    v, dv = _gelu_and_grad(v_pre)
    mu = jnp.mean(v, axis=-1, keepdims=True)
    dc = v - mu
    rs = lax.rsqrt(jnp.mean(dc * dc, axis=-1, keepdims=True) + EPS)
    xh = dc * rs
    vn = (xh * lng_ref[:, lo:hi] + lnb_ref[:, lo:hi]).astype(BF16)
    w = jnp.where(tri, ws_ref[g], 0.0).astype(BF16)
    sg = _dot(w, vn) + bexp_ref[g]
    if want_grad:
        return u, du, dv, rs, xh, vn, w, sg
    return u * sg


def _gmlp_fwd(proj, ln_g, ln_b, w_s, b_exp, na_g):
    t = proj.shape[0]
    ng = N_GROUPS_A

    def body(uv_ref, lng_ref, lnb_ref, ws_ref, bexp_ref, nag_ref, o_ref):
        tri = _tri_mask()
        ys = [_gmlp_group_fwd(uv_ref, lng_ref, lnb_ref, ws_ref, bexp_ref, tri, g, False) for g in range(ng)]
        ssq = ys[0] * 0.0
        for y in ys:
            ssq = ssq + y * y
        r = lax.rsqrt(jnp.sum(ssq, axis=-1, keepdims=True) * (1.0 / D_A) + EPS)
        for g, y in enumerate(ys):
            o_ref[:, g * 128:(g + 1) * 128] = (y * r * nag_ref[:, g * 128:(g + 1) * 128]).astype(BF16)

    vec = pl.BlockSpec((1, D_A), lambda i: (0, 0))
    cube = pl.BlockSpec((ng, CHUNK, CHUNK), lambda i: (0, 0, 0))
    return pl.pallas_call(
        body, name="gmlp_fwd", grid=(t // CHUNK,),
        in_specs=[pl.BlockSpec((CHUNK, 2 * D_A), lambda i: (i, 0)), vec, vec, cube, cube, vec],
        out_specs=pl.BlockSpec((CHUNK, D_A), lambda i: (i, 0)),
        out_shape=jax.ShapeDtypeStruct((t, D_MIX), BF16),
        compiler_params=_cparams(("parallel",)),
    )(proj, ln_g, ln_b, w_s, b_exp, na_g)


def _gmlp_bwd(proj, dyab, ln_g, ln_b, w_s, b_exp, na_g):
    t = proj.shape[0]
    ng = N_GROUPS_A
    nsteps = t // CHUNK

    def body(uv_ref, dy_ref, lng_ref, lnb_ref, ws_ref, bexp_ref, nag_ref,
             duv_ref, dws_ref, dbs_ref, dlng_ref, dlnb_ref, dnag_ref, dbacc):
        i = pl.program_id(0)
        tri = _tri_mask()

        @pl.when(i == 0)
        def _():
            dws_ref[...] = jnp.zeros_like(dws_ref)
            dbacc[...] = jnp.zeros_like(dbacc)
            dlng_ref[...] = jnp.zeros_like(dlng_ref)
            dlnb_ref[...] = jnp.zeros_like(dlnb_ref)
            dnag_ref[...] = jnp.zeros_like(dnag_ref)

        st = [_gmlp_group_fwd(uv_ref, lng_ref, lnb_ref, ws_ref, bexp_ref, tri, g, True) for g in range(ng)]
        ssq = st[0][0] * 0.0
        for s in st:
            y = s[0] * s[7]
            ssq = ssq + y * y
        r = lax.rsqrt(jnp.sum(ssq, axis=-1, keepdims=True) * (1.0 / D_A) + EPS)
        csum = st[0][0] * 0.0
        for g, s in enumerate(st):
            sl = slice(g * 128, (g + 1) * 128)
            xhy = s[0] * s[7] * r
            dya = dy_ref[:, sl].astype(F32)
            dnag_ref[:, sl] += jnp.sum(dya * xhy, axis=0, keepdims=True)
            csum = csum + dya * nag_ref[:, sl] * xhy
        c1 = jnp.sum(csum, axis=-1, keepdims=True) * (1.0 / D_A)
        for g, s in enumerate(st):
            u, du, dv, rs, xh, vn, w, sg = s
            sl = slice(g * 128, (g + 1) * 128)
            dy = r * (dy_ref[:, sl].astype(F32) * nag_ref[:, sl] - u * sg * r * c1)
            dsg = dy * u
            dsg_b = dsg.astype(BF16)
            dws_ref[g] += _dot_nt(dsg_b, vn)
            dbacc[g] += dsg
            dvn = _dot_tn(w, dsg_b)
            dlnb_ref[:, sl] += jnp.sum(dvn, axis=0, keepdims=True)
            dlng_ref[:, sl] += jnp.sum(dvn * xh, axis=0, keepdims=True)
            dxh = dvn * lng_ref[:, sl]
            dvv = rs * (dxh - jnp.mean(dxh, axis=-1, keepdims=True)
                        - xh * jnp.mean(dxh * xh, axis=-1, keepdims=True))
            duv_ref[:, sl] = (dy * sg * du).astype(BF16)
            duv_ref[:, D_A + g * 128:D_A + (g + 1) * 128] = (dvv * dv).astype(BF16)

        @pl.when(i == nsteps - 1)
        def _():
            for g in range(ng):
                dws_ref[g] = jnp.where(tri, dws_ref[g], 0.0)
                dbs_ref[g] = jnp.sum(dbacc[g], axis=-1, keepdims=True)

    vec = pl.BlockSpec((1, D_A), lambda i: (0, 0))
    cube = pl.BlockSpec((ng, CHUNK, CHUNK), lambda i: (0, 0, 0))
    return pl.pallas_call(
        body, name="gmlp_bwd", grid=(nsteps,),
        in_specs=[pl.BlockSpec((CHUNK, 2 * D_A), lambda i: (i, 0)),
                  pl.BlockSpec((CHUNK, D_A), lambda i: (i, 0)), vec, vec, cube, cube, vec],
        out_specs=[pl.BlockSpec((CHUNK, 2 * D_A), lambda i: (i, 0)), cube,
                   pl.BlockSpec((ng, CHUNK, 1), lambda i: (0, 0, 0)), vec, vec, vec],
        out_shape=[jax.ShapeDtypeStruct((t, 2 * D_A), BF16), jax.ShapeDtypeStruct((ng, CHUNK, CHUNK), F32),
                   jax.ShapeDtypeStruct((ng, CHUNK, 1), F32), jax.ShapeDtypeStruct((1, D_A), F32),
                   jax.ShapeDtypeStruct((1, D_A), F32), jax.ShapeDtypeStruct((1, D_A), F32)],
        scratch_shapes=[pltpu.VMEM((ng, CHUNK, CHUNK), F32)],
        compiler_params=_cparams(("arbitrary",)),
    )(proj, dyab, ln_g, ln_b, w_s, b_exp, na_g)


OFF_Z = 2 * D_A
OFF_XS = OFF_Z + D_SSM
OFF_B = OFF_XS + D_SSM
OFF_C = OFF_B + D_BC


def _ssd_consts():
    tri = jnp.tril(jnp.ones((CHUNK, CHUNK), F32)).astype(BF16)
    h = jnp.arange(HPAD)[None, :, None]
    g = jnp.arange(N_SSM_GROUPS)[:, None, None]
    j1 = jnp.arange(GW)[None, None, :]
    eh = (h == g * HEADS_PER_GROUP + j1 // HEAD_DIM).astype(BF16)
    j2 = jnp.arange(HEADS_PER_GROUP * 128)[None, None, :]
    e128 = (h == g * HEADS_PER_GROUP + j2 // 128).astype(BF16)
    return tri, eh, e128


def _ssd_in_specs(cmap):
    def rows(i):
        return cmap(i)

    def prev8(i):
        return jnp.maximum(cmap(i) * (CHUNK // 8) - 1, 0)

    def whole(*shape):
        return pl.BlockSpec(shape, lambda i: (0,) * len(shape))

    bcw = 2 * D_BC
    specs = [
        pl.BlockSpec((CHUNK, D_SSM), lambda i: (rows(i), OFF_Z // D_SSM)),
        pl.BlockSpec((CHUNK, D_SSM), lambda i: (rows(i), OFF_XS // D_SSM)),
        pl.BlockSpec((8, D_SSM), lambda i: (prev8(i), OFF_XS // D_SSM)),
        pl.BlockSpec((CHUNK, bcw), lambda i: (rows(i), OFF_B // bcw)),
        pl.BlockSpec((8, bcw), lambda i: (prev8(i), OFF_B // bcw)),
        pl.BlockSpec((CHUNK, HPAD), lambda i: (rows(i), 0)),
        whole(SSM_CONV, D_XBC), whole(1, D_XBC),
        whole(1, HPAD), whole(1, HPAD),
        whole(1, D_SSM), whole(1, D_SSM),
        whole(CHUNK, CHUNK),
        whole(N_SSM_GROUPS, HPAD, GW), whole(N_SSM_GROUPS, HPAD, HEADS_PER_GROUP * 128),
    ]
    return specs


N_SSD_IN = 15


def _lanes(ref, start, width):
    return ref.at[:, pl.ds(pl.multiple_of(start, 128), width)]


def _ssd_group_refs(ins, g):
    (z_ref, xs_ref, xsp_ref, bc_ref, bcp_ref, dt_ref, cw_ref, cb_ref, dtb_ref, alog_ref, de_ref, gain_ref,
     tri_ref, eh_ref, e128_ref) = ins
    ox, ob, oc = g * GW, g * D_STATE, D_BC + g * D_STATE
    return (_lanes(z_ref, ox, GW), _lanes(xs_ref, ox, GW), _lanes(xsp_ref, ox, GW),
            _lanes(bc_ref, ob, D_STATE), _lanes(bcp_ref, ob, D_STATE),
            _lanes(bc_ref, oc, D_STATE), _lanes(bcp_ref, oc, D_STATE), dt_ref,
            _lanes(cw_ref, ox, GW), _lanes(cw_ref, D_SSM + ob, D_STATE), _lanes(cw_ref, D_SSM + oc, D_STATE),
            _lanes(cb_ref, ox, GW), _lanes(cb_ref, D_SSM + ob, D_STATE), _lanes(cb_ref, D_SSM + oc, D_STATE),
            dtb_ref, alog_ref, _lanes(de_ref, ox, GW), _lanes(gain_ref, ox, GW), tri_ref,
            eh_ref.at[g], e128_ref.at[g])


N_SSD_OWN = 4


def _ssd_scratch():
    return [pltpu.VMEM((CHUNK + 8, GW), F32), pltpu.VMEM((CHUNK + 8, D_STATE), F32),
            pltpu.VMEM((CHUNK + 8, D_STATE), F32), pltpu.VMEM((CHUNK, GW), F32),
            pltpu.VMEM((HPAD, CHUNK), F32), pltpu.VMEM((CHUNK, HPAD), F32), pltpu.VMEM((CHUNK, HPAD), F32)]


def _ssd_two_sets(alloc):
    a, b = alloc[:7], alloc[7:7 + N_SSD_OWN]
    return ((a[0], a[1], a[2], a[4], a[3], a[5], a[6]), (b[0], b[1], b[2], a[4], b[3], a[5], a[6]))


def _ssd_chunk(ins, scr):
    dt_ref, dtb_ref, alog_ref, tri_ref = ins[5], ins[8], ins[9], ins[12]
    acst_sc, dt_sc, acs_sc = scr[3], scr[5], scr[6]
    dt = _softplus(dt_ref[...] + dtb_ref[...])
    acs = _e01x(tri_ref[...], dt * (-jnp.exp(alog_ref[...])))
    dt_sc[...] = dt
    acs_sc[...] = acs
    acst_sc[...] = acs.T


def _ssd_pre(first, g, refs, scr):
    (z_ref, xs_ref, xsp_ref, b_ref, bp_ref, c_ref, cp_ref, dt_ref, cwx, cwb, cwc, cbx, cbb, cbc,
     dtb_ref, alog_ref, de_ref, gain_ref, tri_ref, eh_ref, e128_ref) = refs
    ext_x, ext_b, ext_c, acst_sc, acse_sc, dt_sc, acs_sc = scr
    p = {}
    px = _conv(xs_ref, xsp_ref, cwx, cbx, ext_x, first, SSM_CONV)
    pb = _conv(b_ref, bp_ref, cwb, cbb, ext_b, first, SSM_CONV)
    pc = _conv(c_ref, cp_ref, cwc, cbc, ext_c, first, SSM_CONV)
    p['xs'], p['dsx'] = _silu_and_grad(px)
    p['bm'], p['dsb'] = _silu_and_grad(pb)
    p['cm'], p['dsc'] = _silu_and_grad(pc)
    p['dt_in'] = dt_ref[...] + dtb_ref[...]
    dt = dt_sc[...]
    p['dt'] = dt
    p['a'] = -jnp.exp(alog_ref[...])
    tri_b = tri_ref[...]
    acs = acs_sc[...]
    eh = eh_ref[...]
    p['eh'] = eh
    p['dt_e'] = _x01(dt, eh)
    acs_e = _x01(acs, eh)
    acse_sc[...] = acs_e
    p['acs_e'] = acs_e
    p['acs_c'] = _x01(acs, e128_ref[...])
    p['acs_last_e'] = acse_sc[pl.ds(CHUNK - 1, 1), :]
    p['xdt'] = p['xs'] * p['dt_e']
    p['decay_e'] = jnp.exp(p['acs_last_e'] - acs_e)
    p['cm_b'] = p['cm'].astype(BF16)
    p['bm_b'] = p['bm'].astype(BF16)
    p['scores'] = _dot_nt(p['cm_b'], p['bm_b'])
    p['tri_b'] = tri_b
    return p


def _ssd_l(p, g, r, acst_sc, tri):
    col = p['acs_c'][:, r * 128:(r + 1) * 128]
    row = acst_sc[pl.ds(g * HEADS_PER_GROUP + r, 1), :]
    return jnp.where(tri, jnp.exp(jnp.minimum(col - row, 0.0)), 0.0)


def _ssd_fwd(proj, dt_raw, yab, cw, cb, dtb, alog, de, gain, consts, hosted):
    t = proj.shape[0]
    nc = t // CHUNK
    ng = N_SSM_GROUPS
    tri_c, eh_c, e128_c = consts
    n_in = N_SSD_IN
    nh_in, nh_out = len(hosted['arrays']), len(hosted['out_shape'])
    mid_chunk = (7 * nc) // 8

    def body(*refs):
        ins = refs[:n_in]
        hins = refs[n_in + 1:n_in + 1 + nh_in]
        o0 = n_in + 1 + nh_in
        yb_all, ys_all, hs_all = refs[o0:o0 + 3]
        houts = refs[o0 + 3:o0 + 3 + nh_out]
        h_sc = refs[o0 + 3 + nh_out]
        scr_a, scr_b = _ssd_two_sets(refs[o0 + 4 + nh_out:-2])
        sems = refs[-2:]
        c = pl.program_id(0)
        _host_phase(hosted, 'start', c == 0, hins, houts, sems)
        _host_phase(hosted, 'mid', c == mid_chunk, hins, houts, sems)

        @pl.when(c == 0)
        def _():
            h_sc[...] = jnp.zeros_like(h_sc)

        def group(g, scr):
            grefs = _ssd_group_refs(ins, g)
            z_ref, de_ref, gain_ref = grefs[0], grefs[16], grefs[17]
            yb_ref, ys_ref = _lanes(yb_all, g * GW, GW), _lanes(ys_all, g * GW, GW)
            slab = pl.ds(pl.multiple_of(g * D_STATE, D_STATE), D_STATE)
            p = _ssd_pre(c == 0, g, grefs, scr)
            tri = _tri_mask()
            h_in = h_sc[slab, :]
            hs_all[slab, :] = h_in
            yoff = _dot(p['cm_b'], h_in.astype(BF16)) * jnp.exp(p['acs_e'])
            states = _dot_tn(p['bm_b'], (p['xdt'] * p['decay_e']).astype(BF16))
            lane = lax.broadcasted_iota(jnp.int32, (CHUNK, 128), 1)
            slabs = []
            for r2 in range(HEADS_PER_GROUP // 2):
                xb = p['xdt'][:, r2 * 128:(r2 + 1) * 128].astype(BF16)
                ya = _dot((p['scores'] * _ssd_l(p, g, 2 * r2, scr[3], tri)).astype(BF16), xb)
                yb = _dot((p['scores'] * _ssd_l(p, g, 2 * r2 + 1, scr[3], tri)).astype(BF16), xb)
                slabs.append(jnp.where(lane < HEAD_DIM, ya, yb))
            y = jnp.concatenate(slabs, axis=1) + yoff + de_ref[...] * p['xs']
            ys_ref[...] = y
            h_sc[slab, :] = jnp.exp(p['acs_last_e']) * h_in + states
            zv = z_ref[...]
            yg = y * zv * _sigmoid(zv)
            r = lax.rsqrt(jnp.mean(yg * yg, axis=-1, keepdims=True) + EPS)
            yb_ref[...] = (yg * r * gain_ref[...]).astype(BF16)

        def pair(j, carry):
            group(2 * j, scr_a)
            group(2 * j + 1, scr_b)
            return carry

        _ssd_chunk(ins, scr_a)
        lax.fori_loop(0, ng // 2, pair, 0)
        _host_phase(hosted, 'finish', c == nc - 1, hins, houts, sems)

    h_in, h_out, h_shape, h_scratch, h_alias = _host_plumbing(hosted, n_in + 1, 3)
    in_specs = _ssd_in_specs(lambda i: i) + [pl.BlockSpec(memory_space=pl.ANY)] + h_in
    out_specs = [pl.BlockSpec((CHUNK, D_SSM), lambda i: (i, D_A // D_SSM)),
                 pl.BlockSpec((CHUNK, D_SSM), lambda i: (i, 0)),
                 pl.BlockSpec((None, ng * D_STATE, GW), lambda i: (i, 0, 0))] + h_out
    out_shape = [jax.ShapeDtypeStruct((t, D_MIX), BF16), jax.ShapeDtypeStruct((t, D_SSM), F32),
                 jax.ShapeDtypeStruct((nc, ng * D_STATE, GW), F32)] + h_shape
    return pl.pallas_call(
        body, name="ssd_fwd", grid=(nc,), in_specs=in_specs, out_specs=out_specs, out_shape=out_shape,
        scratch_shapes=[pltpu.VMEM((ng * D_STATE, GW), F32)] + _ssd_scratch() + _ssd_scratch()[:N_SSD_OWN]
        + h_scratch,
        input_output_aliases={n_in: 0, **h_alias},
        compiler_params=_cparams(("arbitrary",)),
    )(proj, proj, proj, proj, proj, dt_raw, cw, cb, dtb, alog, de, gain,
      tri_c, eh_c, e128_c, yab, *hosted['arrays'])


def _rows8(vals, width):
    rid = lax.broadcasted_iota(jnp.int32, (8, width), 0)
    out = jnp.zeros((8, width), F32)
    for k, v in enumerate(vals):
        if v is not None:
            out = out + jnp.where(rid == k, v, 0.0)
    return out


def _ssd_bwd(proj, dt_raw, dyab, ysave, hs, duv, cw, cb, dtb, alog, de, gain, consts, hosted):
    t = proj.shape[0]
    nc = t // CHUNK
    ng = N_SSM_GROUPS
    tri_c, eh_c, e128_c = consts
    n_in = N_SSD_IN
    nh_in, nh_out = len(hosted['arrays']), len(hosted['out_shape'])

    def per_group(g, c, ins, dy_ref, ys_ref, hs_ref, outs, scratch):
        dz_ref, dxs_ref, db_ref, dc_ref, ddt_ref, acc_x, acc_b, acc_c, acc_gain, acc_head = outs
        dh_sc, car_x, car_b, car_c, dext_x, dext_b, dext_c = scratch[:7]
        scr = scratch[7:]
        ext_x, ext_b, ext_c, acst_sc = scr[:4]
        z_ref, cwx, cwb, cwc, de_ref, gain_ref = ins[0], ins[8], ins[9], ins[10], ins[16], ins[17]
        slab = pl.ds(pl.multiple_of(g * D_STATE, D_STATE), D_STATE)
        p = _ssd_pre(c == 0, g, ins, scr)
        tri = _tri_mask()
        xs, dt_e, acs_e, xdt, decay_e = p['xs'], p['dt_e'], p['acs_e'], p['xdt'], p['decay_e']
        cm_b, bm_b, scores, eh = p['cm_b'], p['bm_b'], p['scores'], p['eh']
        h_in = hs_ref[...]
        h_in_b = h_in.astype(BF16)
        e_a = jnp.exp(acs_e)
        raw = _dot(cm_b, h_in_b)

        y = ys_ref[...]
        zv = z_ref[...]
        sz, dsz = _silu_and_grad(zv)
        yg = y * sz
        r = lax.rsqrt(jnp.mean(yg * yg, axis=-1, keepdims=True) + EPS)
        xh = yg * r
        dout = dy_ref[...].astype(F32)
        gain = gain_ref[...]
        dxh = dout * gain
        dyg = r * (dxh - xh * jnp.mean(dxh * xh, axis=-1, keepdims=True))
        dy = dyg * sz
        dz_ref[...] = (dyg * y * dsz).astype(BF16)
        acc_gain[g] += _rows8([jnp.sum(dout * xh, axis=0, keepdims=True)], GW)
        d_skip8 = _x01_nt(_rows8([None, None, jnp.sum(dy * xs, axis=0, keepdims=True)], GW), eh)
        dxs = de_ref[...] * dy

        q = dy * raw * e_a
        draw = (dy * e_a).astype(BF16)
        d_c = _dot_nt(draw, h_in_b)
        dh_in = _dot_tn(cm_b, draw)

        lane = lax.broadcasted_iota(jnp.int32, (CHUNK, 128), 1)
        ones_b = jnp.ones((CHUNK, 128), BF16)
        dscores = jnp.zeros((CHUNK, CHUNK), F32)
        dxdt_slabs, q_slabs = [], []
        for r2 in range(HEADS_PER_GROUP // 2):
            sl = slice(r2 * 128, (r2 + 1) * 128)
            xb = xdt[:, sl].astype(BF16)
            dys = dy[:, sl]
            dys_b = dys.astype(BF16)
            dxh_pair, qv_pair = [], []
            for half in range(2):
                lmat = _ssd_l(p, g, 2 * r2 + half, acst_sc, tri)
                m = scores * lmat
                mine = (lane < HEAD_DIM) if half == 0 else (lane >= HEAD_DIM)
                dm = _dot_nt(jnp.where(mine, dys, 0.0).astype(BF16), xb)
                dscores = dscores + dm * lmat
                gm = dm * m
                dxh_pair.append(_dot_tn(m.astype(BF16), dys_b))
                h3 = _split3(gm)
                colsum = _dot_tn(h3[0], ones_b) + _dot_tn(h3[1], ones_b) + _dot_tn(h3[2], ones_b)
                qv_pair.append(jnp.sum(gm, axis=-1, keepdims=True) - colsum)
            dxdt_slabs.append(jnp.where(lane < HEAD_DIM, dxh_pair[0], dxh_pair[1]))
            q_slabs.append(jnp.where(lane == 0, qv_pair[0], 0.0) + jnp.where(lane == HEAD_DIM, qv_pair[1], 0.0))
        dxdt = jnp.concatenate(dxdt_slabs, axis=1)
        q = q + jnp.concatenate(q_slabs, axis=1)

        dh_out = dh_sc[slab, :]
        dh_out_b = dh_out.astype(BF16)
        e_l = jnp.exp(p['acs_last_e'])
        dh_sc[slab, :] = dh_in + e_l * dh_out
        dlast = jnp.sum(dh_out * h_in, axis=0, keepdims=True) * e_l
        dxd = _dot(bm_b, dh_out_b)
        xd = xdt * decay_e
        dxdt = dxdt + dxd * decay_e
        dd = dxd * xd
        q = q - dd
        dlast = dlast + jnp.sum(dd, axis=0, keepdims=True)
        d_b = _dot_nt(xd.astype(BF16), dh_out_b)
        dsc_b = dscores.astype(BF16)
        d_c = d_c + _dot(dsc_b, bm_b)
        d_b = d_b + _dot_tn(dsc_b, cm_b)

        dxs = dxs + dxdt * dt_e
        rid = lax.broadcasted_iota(jnp.int32, (CHUNK, GW), 0)
        q = q + jnp.where(rid == CHUNK - 1, dlast, 0.0)
        dacs = _x01_nt(q, eh)
        ddt = _x01_nt(dxdt * xs, eh)
        dadt = _e01x_tn(p['tri_b'], dacs)
        ddt = ddt + dadt * p['a']
        d_a = jnp.sum(dadt * p['dt'], axis=0, keepdims=True)
        ddt_raw = ddt * _sigmoid(p['dt_in'])
        acc_head[...] += _rows8([jnp.sum(ddt_raw, axis=0, keepdims=True), d_a * p['a']], HPAD) + d_skip8

        @pl.when(g == 0)
        def _():
            ddt_ref[...] = ddt_raw

        @pl.when(g > 0)
        def _():
            ddt_ref[...] += ddt_raw

        for dv, dsil, ext, dext, car, acc, w_ref, o_ref in (
                (dxs, p['dsx'], ext_x, dext_x, car_x, acc_x, cwx, dxs_ref),
                (d_b, p['dsb'], ext_b, dext_b, car_b, acc_b, cwb, db_ref),
                (d_c, p['dsc'], ext_c, dext_c, car_c, acc_c, cwc, dc_ref)):
            dp = dv * dsil
            width = dp.shape[1]
            rows = [jnp.sum(ext[pl.ds(5 + k, CHUNK), :] * dp, axis=0, keepdims=True) for k in range(SSM_CONV)]
            rows.append(jnp.sum(dp, axis=0, keepdims=True))
            acc[g] += _rows8(rows, width)
            dext[0:CHUNK, :] = dp
            dext[CHUNK:CHUNK + 8, :] = car[g]
            car[g] = dext[0:8, :]
            dx = w_ref[SSM_CONV - 1:SSM_CONV, :] * dext[pl.ds(0, CHUNK), :]
            for k in range(SSM_CONV - 1):
                dx = dx + w_ref[k:k + 1, :] * dext[pl.ds(SSM_CONV - 1 - k, CHUNK), :]
            o_ref[...] = dx.astype(BF16)

    def body(*refs):
        ins = refs[:n_in]
        dy_all, ys_all, hs_all, duv_ref = refs[n_in:n_in + 4]
        hins = refs[n_in + 4:n_in + 4 + nh_in]
        o0 = n_in + 4 + nh_in
        dproj_ref, ddt_ref = refs[o0:o0 + 2]
        accs = refs[o0 + 2:o0 + 7]
        houts = refs[o0 + 7:o0 + 7 + nh_out]
        scratch = refs[o0 + 7 + nh_out:-2]
        sems = refs[-2:]
        cc = pl.program_id(0)
        _host_phase(hosted, 'start', cc == 0, hins, houts, sems)
        dproj_ref[:, 0:2 * D_A] = duv_ref[...]

        @pl.when(cc == 0)
        def _():
            for a in tuple(accs) + tuple(scratch[:4]):
                a[...] = jnp.zeros_like(a)

        shared = tuple(scratch[:4])
        scr_a, scr_b = _ssd_two_sets(tuple(scratch[7:14]) + tuple(scratch[17:17 + N_SSD_OWN]))
        set_a = shared + tuple(scratch[4:7]) + scr_a
        set_b = shared + tuple(scratch[14:17]) + scr_b

        def group(g, own):
            slab = pl.ds(pl.multiple_of(g * D_STATE, D_STATE), D_STATE)
            outs = (_lanes(dproj_ref, OFF_Z + g * GW, GW), _lanes(dproj_ref, OFF_XS + g * GW, GW),
                    _lanes(dproj_ref, OFF_B + g * D_STATE, D_STATE), _lanes(dproj_ref, OFF_C + g * D_STATE, D_STATE),
                    ddt_ref) + tuple(accs)
            per_group(g, nc - 1 - cc, _ssd_group_refs(ins, g), _lanes(dy_all, g * GW, GW),
                      _lanes(ys_all, g * GW, GW), hs_all.at[slab, :], outs, own)

        def pair(j, carry):
            group(2 * j, set_a)
            group(2 * j + 1, set_b)
            return carry

        _ssd_chunk(ins, scr_a)
        lax.fori_loop(0, ng // 2, pair, 0)
        _host_phase(hosted, 'finish', cc == nc - 1, hins, houts, sems)

    def cmap(i):
        return nc - 1 - i

    in_specs = _ssd_in_specs(cmap) + [
        pl.BlockSpec((CHUNK, D_SSM), lambda i: (cmap(i), D_A // D_SSM)),
        pl.BlockSpec((CHUNK, D_SSM), lambda i: (cmap(i), 0)),
        pl.BlockSpec((None, ng * D_STATE, GW), lambda i: (cmap(i), 0, 0)),
        pl.BlockSpec((CHUNK, 2 * D_A), lambda i: (cmap(i), 0)),
    ]

    def full(shape):
        return pl.BlockSpec(shape, lambda i: (0,) * len(shape))

    out_specs = [
        pl.BlockSpec((CHUNK, D_MAIN), lambda i: (cmap(i), 0)),
        pl.BlockSpec((CHUNK, HPAD), lambda i: (cmap(i), 0)),
        full((ng, 8, GW)), full((ng, 8, D_STATE)), full((ng, 8, D_STATE)),
        full((ng, 8, GW)), full((8, HPAD)),
    ]
    out_shape = [
        jax.ShapeDtypeStruct((t, D_MAIN), BF16),
        jax.ShapeDtypeStruct((t, HPAD), F32),
        jax.ShapeDtypeStruct((ng, 8, GW), F32), jax.ShapeDtypeStruct((ng, 8, D_STATE), F32),
        jax.ShapeDtypeStruct((ng, 8, D_STATE), F32), jax.ShapeDtypeStruct((ng, 8, GW), F32),
        jax.ShapeDtypeStruct((8, HPAD), F32),
    ]
    scratch = [pltpu.VMEM((ng * D_STATE, GW), F32),
               pltpu.VMEM((ng, 8, GW), F32), pltpu.VMEM((ng, 8, D_STATE), F32), pltpu.VMEM((ng, 8, D_STATE), F32),
               pltpu.VMEM((CHUNK + 8, GW), F32), pltpu.VMEM((CHUNK + 8, D_STATE), F32),
               pltpu.VMEM((CHUNK + 8, D_STATE), F32)] + _ssd_scratch()
    scratch += scratch[4:7] + _ssd_scratch()[:N_SSD_OWN]
    h_in, h_out, h_shape, h_scratch, h_alias = _host_plumbing(hosted, n_in + 4, len(out_shape))
    return pl.pallas_call(
        body, name="ssd_bwd", grid=(nc,), in_specs=in_specs + h_in, out_specs=out_specs + h_out,
        out_shape=out_shape + h_shape, scratch_shapes=scratch + h_scratch, input_output_aliases=h_alias,
        compiler_params=_cparams(("arbitrary",)),
    )(proj, proj, proj, proj, proj, dt_raw, cw, cb, dtb, alog, de, gain,
      tri_c, eh_c, e128_c, dyab, ysave, hs, duv, *hosted['arrays'])


ANY = pl.BlockSpec(memory_space=pl.ANY)


def _place():
    x, y, c = lax.axis_index("x"), lax.axis_index("y"), lax.axis_index("c")
    chips = [(1 - x, y), (x, 1 - y), (1 - x, 1 - y)]
    return x, y, c, chips


def _rcopy(src, dst, send_sems, recv_sems, k, dev):
    return pltpu.make_async_remote_copy(src_ref=src, dst_ref=dst, send_sem=send_sems.at[k],
                                        recv_sem=recv_sems.at[k], device_id=dev, device_id_type=MESH)


def _my_chip():
    return 2 * lax.axis_index("x") + lax.axis_index("y")


def _cast_into_slot(w, name):
    r, c = w.shape
    tr = _row_tile(r, c, 2)

    def body(w_ref, o_ref):
        o_ref[...] = w_ref[...].astype(BF16)

    return pl.pallas_call(
        body, name=name, grid=(r // tr,), in_specs=[pl.BlockSpec((tr, c), lambda i: (i, 0))],
        out_specs=pl.BlockSpec((None, tr, c), lambda i: (_my_chip(), i, 0)),
        out_shape=jax.ShapeDtypeStruct((N_CHIPS, r, c), BF16), compiler_params=_cparams(("parallel",)),
    )(w)


def _hosted_gather(bigs):
    nb = len(bigs)

    def rows(a, c):
        half = bigs[a].shape[1] // 2
        return pl.ds(c * half, half)

    def start(ins, outs, send_sems, recv_sems):
        x, y, c, chips = _place()
        q = 2 * x + y
        for a in range(nb):
            for k, chip in enumerate(chips):
                _rcopy(outs[a].at[q, rows(a, c)], outs[a].at[q, rows(a, c)], send_sems, recv_sems, 6 * a + k,
                       (chip[0], chip[1], c)).start()

    def mid(ins, outs, send_sems, recv_sems):
        x, y, c, chips = _place()
        sib = (x, y, 1 - c)
        for a in range(nb):
            for k, chip in enumerate(chips):
                slab = outs[a].at[2 * chip[0] + chip[1], rows(a, c)]
                _rcopy(slab, slab, send_sems, recv_sems, 6 * a + k, sib).wait_recv()
                _rcopy(slab, slab, send_sems, recv_sems, 6 * a + 3 + k, sib).start()

    def finish(ins, outs, send_sems, recv_sems):
        x, y, c, chips = _place()
        q = 2 * x + y
        sib = (x, y, 1 - c)
        for a in range(nb):
            for k, chip in enumerate(chips):
                qk = 2 * chip[0] + chip[1]
                other = outs[a].at[qk, rows(a, 1 - c)]
                _rcopy(other, other, send_sems, recv_sems, 6 * a + 3 + k, sib).wait_recv()
                mine = outs[a].at[q, rows(a, c)]
                _rcopy(mine, mine, send_sems, recv_sems, 6 * a + k, sib).wait_send()
                fwd = outs[a].at[qk, rows(a, c)]
                _rcopy(fwd, fwd, send_sems, recv_sems, 6 * a + 3 + k, sib).wait_send()

    return dict(arrays=list(bigs), out_shape=[jax.ShapeDtypeStruct(b.shape, b.dtype) for b in bigs],
                aliases={a: a for a in range(nb)}, nsem=6 * nb, start=start, mid=mid, finish=finish)


def _hosted_rs_chips(ps):
    na = len(ps)

    def copies(ins, outs, send_sems, recv_sems):
        x, y, c, chips = _place()
        return [_rcopy(ins[a].at[2 * chip[0] + chip[1]], outs[a].at[k], send_sems, recv_sems, 3 * a + k,
                       (chip[0], chip[1], c)) for a in range(na) for k, chip in enumerate(chips)]

    def start(ins, outs, send_sems, recv_sems):
        for cp in copies(ins, outs, send_sems, recv_sems):
            cp.start()

    def finish(ins, outs, send_sems, recv_sems):
        for cp in copies(ins, outs, send_sems, recv_sems):
            cp.wait()

    return dict(arrays=list(ps), out_shape=[jax.ShapeDtypeStruct((3,) + p.shape[1:], p.dtype) for p in ps],
                aliases={}, nsem=3 * na, start=start, mid=None, finish=finish)


def _hosted_rs_sibling(gs):
    na = len(gs)

    def copies(ins, outs, send_sems, recv_sems):
        x, y, c, _ = _place()
        halves = [g.shape[1] // 2 for g in gs]
        return [_rcopy(ins[a].at[:, pl.ds((1 - c) * halves[a], halves[a]), :], outs[a], send_sems, recv_sems, a,
                       (x, y, 1 - c)) for a in range(na)]

    def start(ins, outs, send_sems, recv_sems):
        for cp in copies(ins, outs, send_sems, recv_sems):
            cp.start()

    def finish(ins, outs, send_sems, recv_sems):
        for cp in copies(ins, outs, send_sems, recv_sems):
            cp.wait()

    return dict(arrays=list(gs), aliases={}, nsem=na, start=start, mid=None, finish=finish,
                out_shape=[jax.ShapeDtypeStruct((N_CHIPS, g.shape[1] // 2, g.shape[2]), g.dtype) for g in gs])


def _hosted_share_sibling(fs):
    na = len(fs)

    def rows(a, c):
        half = fs[a].shape[0] // 2
        return pl.ds(c * half, half)

    def start(ins, outs, send_sems, recv_sems):
        x, y, c, _ = _place()
        for a in range(na):
            _rcopy(outs[a].at[rows(a, c)], outs[a].at[rows(a, c)], send_sems, recv_sems, a, (x, y, 1 - c)).start()

    def finish(ins, outs, send_sems, recv_sems):
        x, y, c, _ = _place()
        for a in range(na):
            _rcopy(outs[a].at[rows(a, c)], outs[a].at[rows(a, c)], send_sems, recv_sems, a, (x, y, 1 - c)).wait_send()
            other = outs[a].at[rows(a, 1 - c)]
            _rcopy(other, other, send_sems, recv_sems, a, (x, y, 1 - c)).wait_recv()

    return dict(arrays=list(fs), out_shape=[jax.ShapeDtypeStruct(f.shape, f.dtype) for f in fs],
                aliases={a: a for a in range(na)}, nsem=na, start=start, mid=None, finish=finish)


def _hosted_allgather(buf):
    def copies(ins, outs, send_sems, recv_sems):
        x, y, c, _ = _place()
        me = 4 * x + 2 * y + c
        cps = []
        for k in range(1, 8):
            dev = (1 - x if k & 4 else x, 1 - y if k & 2 else y, 1 - c if k & 1 else c)
            cps.append(_rcopy(ins[0], outs[0].at[me], send_sems, recv_sems, k - 1, dev))
        return pltpu.make_async_copy(ins[0], outs[0].at[me], send_sems.at[7]), cps

    def start(ins, outs, send_sems, recv_sems):
        loc, cps = copies(ins, outs, send_sems, recv_sems)
        loc.start()
        for cp in cps:
            cp.start()

    def finish(ins, outs, send_sems, recv_sems):
        loc, cps = copies(ins, outs, send_sems, recv_sems)
        loc.wait()
        for cp in cps:
            cp.wait()

    return dict(arrays=[buf], out_shape=[jax.ShapeDtypeStruct((8,) + buf.shape, buf.dtype)], aliases={},
                nsem=8, start=start, mid=None, finish=finish)


class _SemWindow:
    def __init__(self, sems, off):
        self._sems, self._off = sems, off

    @property
    def at(self):
        return self

    def __getitem__(self, k):
        return self._sems.at[k + self._off]


def _hosted_join(parts):
    arrays, out_shape, aliases, spans, nsem = [], [], {}, [], 0
    for h in parts:
        spans.append((len(arrays), len(h['arrays']), len(out_shape), len(h['out_shape']), nsem))
        aliases.update({len(arrays) + a: len(out_shape) + b for a, b in h['aliases'].items()})
        arrays += h['arrays']
        out_shape += h['out_shape']
        nsem += h['nsem']

    def phase(name):
        if all(h[name] is None for h in parts):
            return None

        def run(ins, outs, send_sems, recv_sems):
            for h, (ia, na, io, no, s0) in zip(parts, spans):
                if h[name] is not None:
                    h[name](ins[ia:ia + na], outs[io:io + no], _SemWindow(send_sems, s0), _SemWindow(recv_sems, s0))

        return run

    return dict(arrays=arrays, out_shape=out_shape, aliases=aliases, nsem=nsem,
                start=phase('start'), mid=phase('mid'), finish=phase('finish'))


def _run_hosted(hosted, name):
    nh_in, nh_out = len(hosted['arrays']), len(hosted['out_shape'])

    def body(*refs):
        ins, outs, sems = refs[:nh_in], refs[nh_in:nh_in + nh_out], refs[-2:]
        for ph in ('start', 'mid', 'finish'):
            if hosted[ph] is not None:
                hosted[ph](ins, outs, sems[0], sems[1])

    h_in, h_out, h_shape, h_scratch, h_alias = _host_plumbing(hosted, 0, 0)
    return pl.pallas_call(body, name=name, in_specs=h_in, out_specs=h_out, out_shape=h_shape,
                          scratch_shapes=h_scratch, input_output_aliases=h_alias)(*hosted['arrays'])


def _host_plumbing(hosted, n_in, n_out):
    nh = len(hosted['arrays'])
    return ([ANY] * nh, [ANY] * len(hosted['out_shape']), list(hosted['out_shape']),
            [pltpu.SemaphoreType.DMA((hosted['nsem'],)), pltpu.SemaphoreType.DMA((hosted['nsem'],))],
            {n_in + a: n_out + b for a, b in hosted['aliases'].items()})


def _host_phase(hosted, phase, when, hins, houts, sems):
    fn = hosted[phase]
    if fn is None:
        return

    @pl.when(when)
    def _():
        fn(hins, houts, sems[0], sems[1])


def _gather_weights(bigs, smalls):
    nb, ns = len(bigs), len(smalls)
    na = nb + ns
    nsem = 6 * nb + 3 * ns
    big = _hosted_gather(bigs)

    def body(*refs):
        ins, outs = refs[:na], refs[na:2 * na]
        send_sems, recv_sems, loc_sems = refs[2 * na:]
        x, y, c, chips = _place()
        q = 2 * x + y
        sib = (x, y, 1 - c)
        locs, sends = [], []
        for s in range(ns):
            cp = pltpu.make_async_copy(ins[nb + s], outs[nb + s].at[q], loc_sems.at[s])
            cp.start()
            locs.append(cp)
        big['start'](ins[:nb], outs[:nb], send_sems, recv_sems)
        for s in range(ns):
            a = nb + s
            for k, chip in enumerate(chips):
                cp = _rcopy(ins[a], outs[a].at[q], send_sems, recv_sems, 6 * nb + 3 * s + k,
                            (chip[0], chip[1], c))
                cp.start()
                sends.append(cp)
        big['mid'](ins[:nb], outs[:nb], send_sems, recv_sems)
        big['finish'](ins[:nb], outs[:nb], send_sems, recv_sems)
        for s in range(ns):
            a = nb + s
            for k, chip in enumerate(chips):
                qk = 2 * chip[0] + chip[1]
                _rcopy(ins[a], outs[a].at[qk], send_sems, recv_sems, 6 * nb + 3 * s + k, sib).wait_recv()
        for cp in sends:
            cp.wait_send()
        for cp in locs:
            cp.wait()

    arrs = list(bigs) + list(smalls)
    out_shape = ([jax.ShapeDtypeStruct(a.shape, a.dtype) for a in bigs]
                 + [jax.ShapeDtypeStruct((N_CHIPS,) + a.shape, a.dtype) for a in smalls])
    return pl.pallas_call(
        body, name="gather_weights", in_specs=[ANY] * na, out_specs=[ANY] * na, out_shape=out_shape,
        input_output_aliases={a: a for a in range(nb)},
        scratch_shapes=[pltpu.SemaphoreType.DMA((nsem,)), pltpu.SemaphoreType.DMA((nsem,)),
                        pltpu.SemaphoreType.DMA((max(ns, 1),))],
    )(*arrs)


def _row_tile(rows, cols, nbuf):
    budget = 24 * 1024 * 1024 // (nbuf * cols * 4 * 2)
    return _tile(rows, max(16, budget - budget % 16), 16) if rows % 16 == 0 else rows


def _add_pairs(g, rcv, name):
    s, half, c = rcv.shape
    tr = _row_tile(half, c, 3)
    nh = half // tr

    def body(a_ref, b_ref, o_ref):
        o_ref[...] = (a_ref[...].astype(F32) + b_ref[...].astype(F32)).astype(o_ref.dtype)

    blk = pl.BlockSpec((None, tr, c), lambda j, i: (j, i, 0))
    mine = pl.BlockSpec((None, tr, c), lambda j, i: (j, lax.axis_index("c") * nh + i, 0))
    return pl.pallas_call(
        body, name=name, grid=(s, nh), in_specs=[mine, blk], out_specs=blk,
        out_shape=jax.ShapeDtypeStruct(rcv.shape, rcv.dtype), compiler_params=_cparams(("parallel", "parallel")),
    )(g, rcv)


def _sum_chips(part, rcv, name):
    _, half, c = part.shape
    tr = _row_tile(half, c, 5)
    nh = half // tr

    def body(o_ref, r_ref, out_ref):
        acc = o_ref[...].astype(F32)
        for k in range(3):
            acc = acc + r_ref[k].astype(F32)
        out_ref[...] = acc

    return pl.pallas_call(
        body, name=name, grid=(nh,),
        in_specs=[pl.BlockSpec((None, tr, c), lambda i: (_my_chip(), i, 0)),
                  pl.BlockSpec((3, tr, c), lambda i: (0, i, 0))],
        out_specs=pl.BlockSpec((tr, c), lambda i: (lax.axis_index("c") * nh + i, 0)),
        out_shape=jax.ShapeDtypeStruct((2 * half, c), F32), compiler_params=_cparams(("parallel",)),
    )(part, rcv)


def _sum_devices(parts, name):
    _, n, _ = parts.shape
    tr = n if n <= 4096 else _tile(n, 512, 8)

    def body(p_ref, o_ref):
        acc = p_ref[0]
        for k in range(1, 8):
            acc = acc + p_ref[k]
        o_ref[...] = acc

    return pl.pallas_call(
        body, name=name, grid=(n // tr,),
        in_specs=[pl.BlockSpec((8, tr, 128), lambda i: (0, i, 0))],
        out_specs=pl.BlockSpec((tr, 128), lambda i: (i, 0)),
        out_shape=jax.ShapeDtypeStruct((n, 128), F32), compiler_params=_cparams(("parallel",)),
    )(parts)


def _adamw(w, g, m, v, name):
    r, c = w.shape
    tr = _row_tile(r, c, 7)
    c1 = 1.0 - ADAM_B1 ** ADAM_STEP
    c2 = 1.0 - ADAM_B2 ** ADAM_STEP

    def body(w_ref, g_ref, m_ref, v_ref, d_ref, mo_ref, vo_ref):
        gv = g_ref[...]
        mn = ADAM_B1 * m_ref[...] + (1.0 - ADAM_B1) * gv
        vn = ADAM_B2 * v_ref[...] + (1.0 - ADAM_B2) * (gv * gv)
        mo_ref[...] = mn
        vo_ref[...] = vn
        m_hat = mn / c1
        v_hat = vn / c2
        d_ref[...] = -ADAM_LR * (m_hat / (jnp.sqrt(v_hat) + ADAM_EPS) + ADAM_WD * w_ref[...])

    blk = pl.BlockSpec((tr, c), lambda i: (i, 0))
    sh = jax.ShapeDtypeStruct((r, c), F32)
    return pl.pallas_call(
        body, name=name, grid=(r // tr,), in_specs=[blk] * 4, out_specs=[blk] * 3, out_shape=[sh] * 3,
        compiler_params=_cparams(("parallel",)),
    )(w, g, m, v)


WEIGHTS = ['norm_mix_g', 'w_in', 'ln_a_g', 'ln_a_b', 'w_s', 'b_s', 'norm_a_g', 'conv_ssm_w', 'conv_ssm_b',
           'dt_bias', 'a_log', 'd_skip', 'ssm_norm_g', 'w_out', 'norm_ffn_g', 'w_up', 'conv_ffn_w',
           'conv_ffn_b', 'w_down', 'norm_ple_g', 'w_ple_gate', 'w_ple', 'norm_final_g']
BIG = ['w_in', 'w_out', 'w_up', 'w_down', 'w_ple_gate', 'w_ple']
SMALL = [n for n in WEIGHTS if n not in BIG]
PACK_ALIGN = 2048


def _pack(arrs):
    parts = []
    for a in arrs:
        f = a.reshape(-1).astype(F32)
        parts.append(jnp.pad(f, (0, (-f.shape[0]) % PACK_ALIGN)))
    return jnp.concatenate(parts).reshape(-1, 128)


def _unpack(buf, shapes):
    flat = buf.reshape(-1)
    out, off = [], 0
    for s in shapes:
        n = math.prod(s)
        out.append(flat[off:off + n].reshape(s))
        off += n + (-n) % PACK_ALIGN
    return out


def _pad_heads(v):
    return jnp.pad(v, ((0, 0), (0, HPAD - v.shape[1])))


def _col_sharded(full):
    r, c4 = full.shape
    return jnp.transpose(full.reshape(r, N_CHIPS, c4 // N_CHIPS), (1, 0, 2))


def _from_col_sharded(g):
    s, r, c = g.shape
    return jnp.transpose(g, (1, 0, 2)).reshape(r, s * c)


def kernel(x, p, norm_mix_g, w_in, ln_a_g, ln_a_b, w_s, b_s, norm_a_g, conv_ssm_w, conv_ssm_b, dt_bias, a_log, d_skip, ssm_norm_g, w_out, norm_ffn_g, w_up, conv_ffn_w, conv_ffn_b, w_down, norm_ple_g, w_ple_gate, w_ple, norm_final_g, loss_target, m_norm_mix_g, m_w_in, m_ln_a_g, m_ln_a_b, m_w_s, m_b_s, m_norm_a_g, m_conv_ssm_w, m_conv_ssm_b, m_dt_bias, m_a_log, m_d_skip, m_ssm_norm_g, m_w_out, m_norm_ffn_g, m_w_up, m_conv_ffn_w, m_conv_ffn_b, m_w_down, m_norm_ple_g, m_w_ple_gate, m_w_ple, m_norm_final_g, v_norm_mix_g, v_w_in, v_ln_a_g, v_ln_a_b, v_w_s, v_b_s, v_norm_a_g, v_conv_ssm_w, v_conv_ssm_b, v_dt_bias, v_a_log, v_d_skip, v_ssm_norm_g, v_w_out, v_norm_ffn_g, v_w_up, v_conv_ffn_w, v_conv_ffn_b, v_w_down, v_norm_ple_g, v_w_ple_gate, v_w_ple, v_norm_final_g):
    given = dict(locals())
    wts = {n: given[n] for n in WEIGHTS}
    mom = {n: given['m_' + n] for n in WEIGHTS}
    var = {n: given['v_' + n] for n in WEIGHTS}
    d = D_MODEL
    xt, pt, tgt = x[0], p[0, 0], loss_target[0]
    chip = 2 * lax.axis_index("x") + lax.axis_index("y")

    slots = {n: _cast_into_slot(wts[n][0], "cast_" + n) for n in BIG}
    g_in, g_cs, g_cf = _gather_weights([slots['w_in']], [conv_ssm_w[0], conv_ffn_w[0]])
    w_in_full = _from_col_sharded(g_in)
    w_main = w_in_full
    w_dt = _pad_heads(w_in_full[:, D_MAIN:])
    cs_w = _from_col_sharded(g_cs)
    cf_w = _from_col_sharded(g_cf)
    consts = _ssd_consts()
    dtb, alog = _pad_heads(dt_bias), _pad_heads(a_log)
    de = jnp.repeat(d_skip[0], HEAD_DIM)[None, :]
    b_exp = jnp.broadcast_to(b_s[0][:, :, None], (N_GROUPS_A, CHUNK, CHUNK))

    a1 = _rms_fwd(xt, norm_mix_g, "rms_mix")
    proj, g_out = _matmul(a1, w_main, mode='nn', name="mm_proj", tm=1024, tn=1024, b_cols=D_MAIN,
                          hosted=_hosted_gather([slots['w_out']]))
    dt_raw = _matmul(a1, w_dt, mode='nn', name="mm_dt", tm=1024, tn=128)
    yab = _gmlp_fwd(proj, ln_a_g, ln_a_b, w_s[0], b_exp, norm_a_g)
    yab, ysave, hs, g_up = _ssd_fwd(proj, dt_raw, yab, cs_w, conv_ssm_b, dtb, alog, de, ssm_norm_g, consts,
                                    _hosted_gather([slots['w_up']]))
    w_out_f = g_out.reshape(D_MIX, d)
    h1 = _matmul(yab, w_out_f, mode='nn', name="mm_out", res=xt, tm=512, tn=1024, tk=4096)
    f = _rms_fwd(h1, norm_ffn_g, "rms_ffn")
    hid, g_down, g_pg, g_ple = _matmul(
        f, g_up, mode='nn', name="mm_up", b_sharded=True, tm=1024, tn=1408,
        hosted=_hosted_gather([slots[n] for n in ('w_down', 'w_ple_gate', 'w_ple')]))
    w_down_f = g_down.reshape(D_FF, d)
    w_pg_f = g_pg.reshape(d, d)
    act, conv_g, conv_u = _ffn_act_fwd(hid, cf_w, conv_ffn_b)
    h2 = _matmul(act, w_down_f, mode='nn', name="mm_down", res=h1, tm=1024, tn=1024, tk=2816)
    n3 = _rms_fwd(h2, norm_ple_g, "rms_ple")
    gl = _matmul(n3, w_pg_f, mode='nn', name="mm_pg", tm=1024, tn=1024)
    pe = _matmul(pt, g_ple, mode='nn', name="mm_ple", b_sharded=True, tm=1024, tn=512)
    dh3, dgl, dpe, lossv, dgf = _tail(h2, gl, pe, tgt, norm_final_g[None, :])

    gs_ple = _matmul(pt, dpe, mode='tn', name="mm_dw_ple", out_dtype=BF16, out_shards=N_CHIPS,
                     tm=256, tn=512, tk=2048)
    gs_pg = _matmul(n3, dgl, mode='tn', name="mm_dw_pg", out_dtype=BF16, tm=1024, tn=1024, tk=4096)
    dn3 = _matmul(dgl, w_pg_f, mode='nt', name="mm_dn3", out_dtype=BF16, tm=1024, tn=1024)
    dh2, dg_ple, dh2_b = _rms_bwd(h2, norm_ple_g, dn3, dh3, "rms_ple_bwd", True)
    dact = _matmul(dh2_b, w_down_f, mode='nt', name="mm_dact", out_dtype=BF16, tm=1024, tn=1408)
    gs_down = _matmul(act, dh2_b, mode='tn', name="mm_dw_down", out_dtype=BF16, tm=1408, tn=1024, tk=2048)
    dpg, dpu, wg_acc, wu_acc = _ffn_act_bwd(hid, cf_w, conv_g, conv_u, dact)
    hc = N_CHIPS // 2
    gs_up = jnp.concatenate(
        [_matmul(f, dpg, mode='tn', name="mm_dw_up_g", out_dtype=BF16, out_shards=hc, tm=1024, tn=1408, tk=2048),
         _matmul(f, dpu, mode='tn', name="mm_dw_up_u", out_dtype=BF16, out_shards=hc, tm=1024, tn=1408, tk=2048)],
        axis=0)
    early = [gs_up, gs_down.reshape(N_CHIPS, D_FF // N_CHIPS, d), gs_pg.reshape(N_CHIPS, d // N_CHIPS, d), gs_ple]
    df, *sib_e = _matmul(dpg, g_up, mode='nt', name="mm_df_g", b_sharded=True, tm=1024, tn=1024, tk=2816,
                         hosted=_hosted_rs_sibling(early))
    part_e = [_add_pairs(a, b, "rs_add_e%d" % i) for i, (a, b) in enumerate(zip(early, sib_e))]
    df = _matmul(dpu, g_up, mode='nt', name="mm_df_u", b_sharded=True, b_shard_off=hc, res=df, out_dtype=BF16,
                 tm=1024, tn=1024, tk=2816)
    dh1, dg_ffn, dh1_b = _rms_bwd(h1, norm_ffn_g, df, dh2, "rms_ffn_bwd", True)
    dyab = _matmul(dh1_b, w_out_f, mode='nt', name="mm_dyab", out_dtype=BF16, tm=1024, tn=1024)
    gs_out = _matmul(yab, dh1_b, mode='tn', name="mm_dw_out", out_dtype=BF16, tm=1024, tn=1024, tk=4096)
    duv, dws, dbs, dlng, dlnb, dnag = _gmlp_bwd(proj, dyab, ln_a_g, ln_a_b, w_s[0], b_exp, norm_a_g)
    dproj, ddt_raw, acc_x, acc_b, acc_c, acc_gain, acc_head, *rcv_e = _ssd_bwd(
        proj, dt_raw, dyab, ysave, hs, duv, cs_w, conv_ssm_b, dtb, alog, de, ssm_norm_g, consts,
        _hosted_rs_chips(part_e))
    dw_main = _matmul(a1, dproj, mode='tn', name="mm_dw_main", out_dtype=BF16, tm=1024, tn=1024, tk=4096)
    dw_dt = _matmul(a1, ddt_raw, mode='tn', name="mm_dw_dt", out_dtype=BF16, tm=1024, tn=128, tk=2048)
    gs_in = _col_sharded(jnp.concatenate([dw_main, dw_dt[:, :N_HEADS]], axis=1))
    late = [gs_in, gs_out.reshape(N_CHIPS, D_MIX // N_CHIPS, d)]
    part_l = [_add_pairs(a, b, "rs_add_l%d" % i) for i, (a, b) in enumerate(
        zip(late, _run_hosted(_hosted_rs_sibling(late), "rs_sibling_late")))]

    def conv_rows(acc, k):
        return acc[:, k, :].reshape(1, -1)

    dcw = jnp.concatenate([jnp.concatenate([conv_rows(acc_x, k), conv_rows(acc_b, k), conv_rows(acc_c, k)], axis=1)
                           for k in range(SSM_CONV)], axis=0)
    dcb = jnp.concatenate([conv_rows(acc_x, SSM_CONV), conv_rows(acc_b, SSM_CONV), conv_rows(acc_c, SSM_CONV)], axis=1)
    part = {
        'ln_a_g': dlng, 'ln_a_b': dlnb, 'w_s': dws, 'b_s': dbs, 'norm_a_g': dnag,
        'conv_ssm_w': dcw, 'conv_ssm_b': dcb,
        'dt_bias': acc_head[0:1, :N_HEADS], 'a_log': acc_head[1:2, :N_HEADS], 'd_skip': acc_head[2:3, :N_HEADS],
        'ssm_norm_g': acc_gain[:, 0, :], 'norm_ffn_g': dg_ffn,
        'conv_ffn_w': jnp.concatenate([wg_acc[:FFN_CONV], wu_acc[:FFN_CONV]], axis=1),
        'conv_ffn_b': jnp.concatenate([wg_acc[FFN_CONV:FFN_CONV + 1], wu_acc[FFN_CONV:FFN_CONV + 1]], axis=1),
        'norm_ple_g': dg_ple, 'norm_final_g': dgf,
    }
    full_shapes = {n: wts[n].shape for n in SMALL}
    full_shapes['conv_ssm_w'] = (1, SSM_CONV, D_XBC)
    full_shapes['conv_ffn_w'] = (1, FFN_CONV, 2 * D_FF)
    small_e = [n for n in SMALL if n != 'norm_mix_g']
    packed = _pack([part[n] for n in small_e] + [lossv[:, 0:1]])

    halves_e = [_sum_chips(a, b, "rs_sum_e%d" % i) for i, (a, b) in enumerate(zip(part_e, rcv_e))]
    da_dt = _matmul(ddt_raw, w_dt, mode='nt', name="mm_da_dt", tm=1024, tn=1024)
    da, *moved = _matmul(dproj, w_main, mode='nt', name="mm_da", res=da_dt, out_dtype=BF16, tm=1024, tn=1024, tk=2560,
                         hosted=_hosted_join([_hosted_rs_chips(part_l), _hosted_share_sibling(halves_e),
                                              _hosted_allgather(packed)]))
    rcv_l, g_early, gathered = moved[:len(late)], moved[len(late):-1], moved[-1]
    dx, dg_mix = _rms_bwd(xt, norm_mix_g, da, dh1, "rms_mix_bwd", False)

    halves_l = [_sum_chips(a, b, "rs_sum_l%d" % i) for i, (a, b) in enumerate(zip(part_l, rcv_l))]
    g_big = dict(zip(['w_up', 'w_down', 'w_ple_gate', 'w_ple'], g_early))
    g_big.update(zip(['w_in', 'w_out'], _run_hosted(_hosted_share_sibling(halves_l), "share_sibling_late")))

    pieces = _unpack(_sum_devices(gathered, "sum_devices"), [full_shapes[n] for n in small_e] + [(1,)])
    g_small = dict(zip(small_e, pieces[:-1]))
    loss = pieces[-1][0]
    mix = _sum_devices(_run_hosted(_hosted_allgather(_pack([dg_mix])), "allgather_mix")[0], "sum_devices_mix")
    g_small['norm_mix_g'] = _unpack(mix, [full_shapes['norm_mix_g']])[0]
    for n in ('conv_ssm_w', 'conv_ffn_w'):
        width = wts[n].shape[2]
        g_small[n] = lax.dynamic_slice_in_dim(g_small[n], chip * width, width, axis=2)

    grads, delta, new_m, new_v = {}, {}, {}, {}
    for n in BIG:
        shp = wts[n].shape
        dl, mn, vn = _adamw(wts[n][0], g_big[n], mom[n][0], var[n][0], "adamw_" + n)
        grads[n], delta[n], new_m[n], new_v[n] = (g_big[n].reshape(shp), dl.reshape(shp), mn.reshape(shp),
                                                  vn.reshape(shp))
    shapes = [wts[n].shape for n in SMALL]
    dl, mn, vn = _adamw(_pack([wts[n] for n in SMALL]), _pack([g_small[n] for n in SMALL]),
                        _pack([mom[n] for n in SMALL]), _pack([var[n] for n in SMALL]), "adamw_small")
    for n, a, b, c in zip(SMALL, _unpack(dl, shapes), _unpack(mn, shapes), _unpack(vn, shapes)):
        grads[n], delta[n], new_m[n], new_v[n] = g_small[n], a, b, c

    return (loss, dx[None], *[grads[n] for n in WEIGHTS], *[delta[n] for n in WEIGHTS],
            *[new_m[n] for n in WEIGHTS], *[new_v[n] for n in WEIGHTS])
```

```python
import functools
import math

import jax
import jax.numpy as jnp
from jax import lax
from jax.experimental import pallas as pl
from jax.experimental.pallas import tpu as pltpu

D_MODEL = 2048
SEQ = 8192
D_MIX = 2 * D_MODEL
D_A = D_MIX // 2
CHUNK = 128
N_GROUPS_A = D_A // 128
D_SSM = D_MIX - D_A
HEAD_DIM = 64
N_HEADS = D_SSM // HEAD_DIM
HEADS_PER_GROUP = 4
N_SSM_GROUPS = N_HEADS // HEADS_PER_GROUP
GW = HEADS_PER_GROUP * HEAD_DIM
D_STATE = 128
SSM_CONV = 4
D_BC = N_SSM_GROUPS * D_STATE
D_XBC = D_SSM + 2 * D_BC
D_MAIN = 2 * D_A + D_SSM + D_XBC
D_IN = D_MAIN + N_HEADS
D_FF = (D_MODEL * 11) // 4
FFN_CONV = 3
D_PLE = 256
EPS = 1e-6
HPAD = 128
N_CHIPS = 4

ADAM_LR = 0.001
ADAM_B1 = 0.9
ADAM_B2 = 0.999
ADAM_EPS = 1e-08
ADAM_WD = 0.01
ADAM_STEP = 10

F32 = jnp.float32
BF16 = jnp.bfloat16
MESH = pl.DeviceIdType.MESH
VMEM_LIMIT = 56 * 1024 * 1024


def _cparams(sem):
    return pltpu.CompilerParams(dimension_semantics=sem, vmem_limit_bytes=VMEM_LIMIT)


def _tile(n, pref, mult):
    t = min(pref, n)
    t -= t % mult
    while n % t:
        t -= mult
    return t


def _dot(a, b):
    return jnp.dot(a, b, preferred_element_type=F32)


def _dot_nt(a, b):
    return lax.dot_general(a, b, (((1,), (1,)), ((), ())), preferred_element_type=F32)


def _dot_tn(a, b):
    return lax.dot_general(a, b, (((0,), (0,)), ((), ())), preferred_element_type=F32)


def _split3(x):
    hi = x.astype(BF16)
    r = x - hi.astype(F32)
    mid = r.astype(BF16)
    lo = (r - mid.astype(F32)).astype(BF16)
    return hi, mid, lo


def _x01(x, e):
    h, m, l = _split3(x)
    return _dot(h, e) + _dot(m, e) + _dot(l, e)


def _x01_nt(x, e):
    h, m, l = _split3(x)
    return _dot_nt(h, e) + _dot_nt(m, e) + _dot_nt(l, e)


def _e01x(e, x):
    h, m, l = _split3(x)
    return _dot(e, h) + _dot(e, m) + _dot(e, l)


def _e01x_tn(e, x):
    h, m, l = _split3(x)
    return _dot_tn(e, h) + _dot_tn(e, m) + _dot_tn(e, l)


def _sigmoid(x):
    return 1.0 / (1.0 + jnp.exp(-x))


_GELU_C = math.sqrt(2.0 / math.pi)


def _gelu_and_grad(x):
    x2 = x * x
    th = jnp.tanh(_GELU_C * (x + 0.044715 * x * x2))
    y = 0.5 * x * (1.0 + th)
    dy = 0.5 * (1.0 + th) + 0.5 * x * (1.0 - th * th) * (_GELU_C * (1.0 + 3.0 * 0.044715 * x2))
    return y, dy


def _silu_and_grad(x):
    s = _sigmoid(x)
    return x * s, s * (1.0 + x * (1.0 - s))


def _softplus(x):
    u = jnp.exp(-jnp.abs(x))
    w = 1.0 + u
    l1p = jnp.where(w == 1.0, u, jnp.log(w) * (u / (w - 1.0)))
    return jnp.maximum(x, 0.0) + l1p


def _matmul(a, b, *, mode, name, out_dtype=F32, res=None, tm=512, tn=512, tk=2048,
            b_sharded=False, b_shard_off=0, b_cols=None, out_shards=0, hosted=None):
    if mode == 'tn':
        kdim, m = a.shape
        n = b.shape[1]
    else:
        m, kdim = a.shape
        if b_sharded:
            s_b, d1, d2 = b.shape
            n = s_b * d2 if mode == 'nn' else d1
        else:
            n = b.shape[1] if mode == 'nn' else b.shape[0]
        if b_cols is not None:
            n = b_cols
    per = None
    if b_sharded:
        per = b.shape[2]
    if out_shards:
        per = n // out_shards
    tm = _tile(m, tm, 128 if mode == 'tn' else 8)
    if mode == 'nt' and b_sharded:
        tn = _tile(n, tn, 128)
        tk = _tile(per, tk, 128)
    elif per is not None:
        tn = _tile(per, tn, 128)
        tk = _tile(kdim, tk, 128 if mode != 'tn' else 8)
    else:
        tn = _tile(n, tn, 128)
        tk = _tile(kdim, tk, 128 if mode != 'tn' else 8)
    nm, nn_, nk = m // tm, n // tn, kdim // tk
    has_res = res is not None
    n_in = 2 + has_res
    nh_in = len(hosted['arrays']) if hosted else 0
    nh_out = len(hosted['out_shape']) if hosted else 0

    def body(*refs):
        a_ref, b_ref = refs[0], refs[1]
        res_ref = refs[2] if has_res else None
        o_ref = refs[n_in + nh_in]
        if hosted:
            hins = refs[n_in:n_in + nh_in]
            houts = refs[n_in + nh_in + 1:n_in + nh_in + 1 + nh_out]
            sems = refs[-2:]
            ids = [pl.program_id(d) for d in range(3)]
            step = (ids[0] * nm + ids[1]) * nk + ids[2]
            n_steps = nn_ * nm * nk
            at_first, at_mid, at_last = step == 0, step == (7 * n_steps) // 8, step == n_steps - 1
            _host_phase(hosted, 'start', at_first, hins, houts, sems)
        av = a_ref[...].astype(BF16)
        bv = b_ref[...].astype(BF16)
        if mode == 'nn':
            p = _dot(av, bv)
        elif mode == 'nt':
            p = _dot_nt(av, bv)
        else:
            p = _dot_tn(av, bv)

        def fin(v):
            if has_res:
                v = v + res_ref[...]
            o_ref[...] = v.astype(o_ref.dtype)

        if nk == 1:
            fin(p)
        else:
            acc_ref = refs[n_in + nh_in + 1 + nh_out]
            k = pl.program_id(2)

            @pl.when(k == 0)
            def _():
                acc_ref[...] = p

            @pl.when(k > 0)
            def _():
                acc_ref[...] += p

            @pl.when(k == nk - 1)
            def _():
                fin(acc_ref[...])
        if hosted:
            _host_phase(hosted, 'mid', at_mid, hins, houts, sems)
            _host_phase(hosted, 'finish', at_last, hins, houts, sems)

    if mode == 'nn':
        a_spec = pl.BlockSpec((tm, tk), lambda j, i, k: (i, k))
        if b_sharded:
            nps = per // tn
            b_spec = pl.BlockSpec((None, tk, tn), lambda j, i, k: (j // nps, k, j % nps))
        else:
            b_spec = pl.BlockSpec((tk, tn), lambda j, i, k: (k, j))
    elif mode == 'nt':
        a_spec = pl.BlockSpec((tm, tk), lambda j, i, k: (i, k))
        if b_sharded:
            kps = per // tk
            b_spec = pl.BlockSpec((None, tn, tk), lambda j, i, k: (k // kps + b_shard_off, j, k % kps))
        else:
            b_spec = pl.BlockSpec((tn, tk), lambda j, i, k: (j, k))
    else:
        a_spec = pl.BlockSpec((tk, tm), lambda j, i, k: (k, i))
        b_spec = pl.BlockSpec((tk, tn), lambda j, i, k: (k, j))
    in_specs = [a_spec, b_spec]
    args = [a, b]
    if has_res:
        in_specs.append(pl.BlockSpec((tm, tn), lambda j, i, k: (i, j)))
        args.append(res)
    if out_shards:
        nps_o = per // tn
        out_shape = jax.ShapeDtypeStruct((out_shards, m, per), out_dtype)
        out_spec = pl.BlockSpec((None, tm, tn), lambda j, i, k: (j // nps_o, i, j % nps_o))
    else:
        out_shape = jax.ShapeDtypeStruct((m, n), out_dtype)
        out_spec = pl.BlockSpec((tm, tn), lambda j, i, k: (i, j))
    scratch = [pltpu.VMEM((tm, tn), F32)] if nk > 1 else []
    if not hosted:
        return pl.pallas_call(
            body, name=name, grid=(nn_, nm, nk), in_specs=in_specs, out_specs=out_spec,
            out_shape=out_shape, scratch_shapes=scratch,
            compiler_params=_cparams(("parallel", "parallel", "arbitrary")),
        )(*args)
    h_in, h_out, h_shape, h_scratch, h_alias = _host_plumbing(hosted, n_in, 1)
    return pl.pallas_call(
        body, name=name, grid=(nn_, nm, nk), in_specs=in_specs + h_in, out_specs=[out_spec] + h_out,
        out_shape=[out_shape] + h_shape, scratch_shapes=scratch + h_scratch, input_output_aliases=h_alias,
        compiler_params=_cparams(("arbitrary", "arbitrary", "arbitrary")),
    )(*args, *hosted['arrays'])


def _rms_fwd(x, g, name):
    t, d = x.shape
    tt = _tile(t, 512, 8)

    def body(x_ref, g_ref, o_ref):
        xv = x_ref[...]
        r = lax.rsqrt(jnp.mean(xv * xv, axis=-1, keepdims=True) + EPS)
        o_ref[...] = (xv * r * g_ref[...]).astype(o_ref.dtype)

    return pl.pallas_call(
        body, name=name, grid=(t // tt,),
        in_specs=[pl.BlockSpec((tt, d), lambda i: (i, 0)), pl.BlockSpec((1, d), lambda i: (0, 0))],
        out_specs=pl.BlockSpec((tt, d), lambda i: (i, 0)),
        out_shape=jax.ShapeDtypeStruct((t, d), BF16),
        compiler_params=_cparams(("parallel",)),
    )(x, g)


def _rms_bwd(x, g, dy, dres, name, also_bf16):
    t, d = x.shape
    tt = _tile(t, 512, 16)

    def body(x_ref, g_ref, dy_ref, dres_ref, dx_ref, dg_ref, *dxb):
        i = pl.program_id(0)
        xv = x_ref[...]
        r = lax.rsqrt(jnp.mean(xv * xv, axis=-1, keepdims=True) + EPS)
        xh = xv * r
        dyv = dy_ref[...].astype(F32)
        dxh = dyv * g_ref[...]
        c = jnp.mean(dxh * xh, axis=-1, keepdims=True)
        dx = dres_ref[...] + r * (dxh - xh * c)
        dx_ref[...] = dx
        for dxb_ref in dxb:
            dxb_ref[...] = dx.astype(BF16)
        part = jnp.sum(dyv * xh, axis=0, keepdims=True)

        @pl.when(i == 0)
        def _():
            dg_ref[...] = part

        @pl.when(i > 0)
        def _():
            dg_ref[...] += part

    row = pl.BlockSpec((tt, d), lambda i: (i, 0))
    vec = pl.BlockSpec((1, d), lambda i: (0, 0))
    return pl.pallas_call(
        body, name=name, grid=(t // tt,),
        in_specs=[row, vec, row, row], out_specs=[row, vec] + [row] * also_bf16,
        out_shape=[jax.ShapeDtypeStruct((t, d), F32), jax.ShapeDtypeStruct((1, d), F32)]
        + [jax.ShapeDtypeStruct((t, d), BF16)] * also_bf16,
        compiler_params=_cparams(("arbitrary",)),
    )(x, g, dy, dres)


def _tail(h2, gl, pe, target, gfin):
    t, d = h2.shape
    tt = _tile(t, 256, 8)

    def body(h2_ref, gl_ref, pe_ref, tg_ref, gf_ref, dh3_ref, dgl_ref, dpe_ref, loss_ref, dgf_ref):
        i = pl.program_id(0)
        sig = _sigmoid(gl_ref[...])
        pev = pe_ref[...]
        h3 = h2_ref[...] + sig * pev
        r = lax.rsqrt(jnp.mean(h3 * h3, axis=-1, keepdims=True) + EPS)
        xh = h3 * r
        gf = gf_ref[...]
        e = xh * gf - tg_ref[...]
        dy = e * (1.0 / d)
        dxh = dy * gf
        c = jnp.mean(dxh * xh, axis=-1, keepdims=True)
        dh3 = r * (dxh - xh * c)
        dh3_ref[...] = dh3
        dgl_ref[...] = (dh3 * pev * sig * (1.0 - sig)).astype(BF16)
        dpe_ref[...] = (dh3 * sig).astype(BF16)
        lpart = jnp.sum(e * e, axis=0, keepdims=True) * (0.5 / d)
        gpart = jnp.sum(dy * xh, axis=0, keepdims=True)

        @pl.when(i == 0)
        def _():
            loss_ref[...] = lpart
            dgf_ref[...] = gpart

        @pl.when(i > 0)
        def _():
            loss_ref[...] += lpart
            dgf_ref[...] += gpart

        @pl.when(i == t // tt - 1)
        def _():
            loss_ref[...] = jnp.broadcast_to(jnp.sum(loss_ref[...], axis=-1, keepdims=True), (1, d))

    row = pl.BlockSpec((tt, d), lambda i: (i, 0))
    vec = pl.BlockSpec((1, d), lambda i: (0, 0))
    return pl.pallas_call(
        body, name="tail", grid=(t // tt,),
        in_specs=[row, row, row, row, vec], out_specs=[row, row, row, vec, vec],
        out_shape=[jax.ShapeDtypeStruct((t, d), F32), jax.ShapeDtypeStruct((t, d), BF16),
                   jax.ShapeDtypeStruct((t, d), BF16), jax.ShapeDtypeStruct((1, d), F32),
                   jax.ShapeDtypeStruct((1, d), F32)],
        compiler_params=_cparams(("arbitrary",)),
    )(h2, gl, pe, target, gfin)


def _conv(cur_ref, prev_ref, w_ref, b_ref, ext_ref, first, width):
    rows = cur_ref.shape[0]
    ext_ref[0:8, :] = jnp.where(first, 0.0, prev_ref[...])
    ext_ref[8:8 + rows, :] = cur_ref[...]
    acc = b_ref[...]
    for k in range(width):
        acc = acc + w_ref[k:k + 1, :] * ext_ref[pl.ds(9 - width + k, rows), :]
    return acc


def _ffn_specs(t, tt, tc, nf):
    hb = tt // 8
    cur_g = pl.BlockSpec((tt, tc), lambda j, i: (i, j))
    cur_u = pl.BlockSpec((tt, tc), lambda j, i: (i, j + nf))
    prev_g = pl.BlockSpec((8, tc), lambda j, i: (jnp.maximum(i * hb - 1, 0), j))
    prev_u = pl.BlockSpec((8, tc), lambda j, i: (jnp.maximum(i * hb - 1, 0), j + nf))
    w_g = pl.BlockSpec((FFN_CONV, tc), lambda j, i: (0, j))
    w_u = pl.BlockSpec((FFN_CONV, tc), lambda j, i: (0, j + nf))
    b_g = pl.BlockSpec((1, tc), lambda j, i: (0, j))
    b_u = pl.BlockSpec((1, tc), lambda j, i: (0, j + nf))
    return [cur_g, prev_g, cur_u, prev_u, w_g, w_u, b_g, b_u]


FFN_TC = 512


def _shift_down(prev, cur, n, rid):
    return jnp.where(rid < n, pltpu.roll(prev, n, 0), pltpu.roll(cur, n, 0))


def _shift_up(cur, nxt, n, rid):
    return jnp.where(rid < 8 - n, pltpu.roll(cur, 8 - n, 0), pltpu.roll(nxt, 8 - n, 0))


def _conv3_group(prev, cur, w_ref, b_ref, rid):
    x1 = _shift_down(prev, cur, 1, rid)
    x2 = _shift_down(prev, cur, 2, rid)
    return b_ref[...] + w_ref[2:3, :] * cur + w_ref[1:2, :] * x1 + w_ref[0:1, :] * x2


def _ffn_act_fwd(hid, cw, cb):
    t = hid.shape[0]
    tt = _tile(t, 1024, 16)
    tc = _tile(D_FF, FFN_TC, 128)
    nf = D_FF // tc

    def body(g_ref, gp_ref, u_ref, up_ref, wg_ref, wu_ref, bg_ref, bu_ref, o_ref, cg_ref, cu_ref):
        first = pl.program_id(1) == 0
        rid = lax.broadcasted_iota(jnp.int32, (8, tc), 0)

        def step(s, carry):
            pg, pu = carry
            r0 = pl.multiple_of(s * 16, 16)
            g0, g1 = g_ref[pl.ds(r0, 8), :], g_ref[pl.ds(r0 + 8, 8), :]
            u0, u1 = u_ref[pl.ds(r0, 8), :], u_ref[pl.ds(r0 + 8, 8), :]
            gate = jnp.concatenate([_conv3_group(pg, g0, wg_ref, bg_ref, rid),
                                    _conv3_group(g0, g1, wg_ref, bg_ref, rid)], axis=0)
            up = jnp.concatenate([_conv3_group(pu, u0, wu_ref, bu_ref, rid),
                                  _conv3_group(u0, u1, wu_ref, bu_ref, rid)], axis=0)
            cg_ref[pl.ds(r0, 16), :] = gate.astype(BF16)
            cu_ref[pl.ds(r0, 16), :] = up.astype(BF16)
            o_ref[pl.ds(r0, 16), :] = (gate * _sigmoid(gate) * up).astype(BF16)
            return g1, u1

        init = (jnp.where(first, 0.0, gp_ref[...]), jnp.where(first, 0.0, up_ref[...]))
        lax.fori_loop(0, tt // 16, step, init)

    blk = pl.BlockSpec((tt, tc), lambda j, i: (i, j))
    return pl.pallas_call(
        body, name="ffn_act_fwd", grid=(nf, t // tt), in_specs=_ffn_specs(t, tt, tc, nf),
        out_specs=[blk, blk, blk],
        out_shape=[jax.ShapeDtypeStruct((t, D_FF), BF16)] * 3,
        compiler_params=_cparams(("parallel", "arbitrary")),
    )(hid, hid, hid, hid, cw, cw, cb, cb)


def _ffn_act_bwd(hid, cw, conv_g, conv_u, dact):
    t = hid.shape[0]
    tt = _tile(t, 1024, 16)
    tc = _tile(D_FF, FFN_TC, 128)
    nf = D_FF // tc
    nt = t // tt
    n16 = tt // 16

    def body(g_ref, u_ref, wg_ref, wu_ref, cg_ref, cgn_ref, cu_ref, cun_ref, da_ref, dan_ref,
             og_ref, ou_ref, ag_ref, au_ref, accs):
        i = pl.program_id(1)
        first, last = i == 0, i == nt - 1
        rid = lax.broadcasted_iota(jnp.int32, (8, tc), 0)
        accs[...] = jnp.zeros_like(accs)

        def dgroup(gate, up, da):
            sv, sgr = _silu_and_grad(gate)
            return da * up * sgr, da * sv

        def finish(x, d0, d1, w_ref):
            s1 = _shift_up(d0, d1, 1, rid)
            s2 = _shift_up(d0, d1, 2, rid)
            dpre = w_ref[2:3, :] * d0 + w_ref[1:2, :] * s1 + w_ref[0:1, :] * s2
            return dpre, (x * s2, x * s1, x * d0, d0)

        def two_groups(it, carry, gate_blk, up_blk, da_blk, zero_ahead):
            d0g, d0u, gate1, up1, da1 = carry
            r0 = it * 16 if isinstance(it, int) else pl.multiple_of(it * 16, 16)
            x0g, x0u = g_ref[pl.ds(r0, 8), :], u_ref[pl.ds(r0, 8), :]
            x1g, x1u = g_ref[pl.ds(r0 + 8, 8), :], u_ref[pl.ds(r0 + 8, 8), :]
            d1g, d1u = dgroup(gate1, up1, da1)
            d2g, d2u = dgroup(gate_blk[0:8], up_blk[0:8], da_blk[0:8])
            d2g = jnp.where(zero_ahead, 0.0, d2g)
            d2u = jnp.where(zero_ahead, 0.0, d2u)
            for half, (xa, xb, da_, db_, dc_, w_ref, o_ref) in enumerate((
                    (x0g, x1g, d0g, d1g, d2g, wg_ref, og_ref), (x0u, x1u, d0u, d1u, d2u, wu_ref, ou_ref))):
                pa, prods_a = finish(xa, da_, db_, w_ref)
                pb, prods_b = finish(xb, db_, dc_, w_ref)
                o_ref[pl.ds(r0, 16), :] = jnp.concatenate([pa, pb], axis=0).astype(BF16)
                for k in range(4):
                    accs[4 * half + k] += prods_a[k] + prods_b[k]
            return d2g, d2u, gate_blk[8:16], up_blk[8:16], da_blk[8:16]

        def rows16(ref, r):
            return ref[pl.ds(r, 16), :].astype(F32)

        def step(it, carry):
            r1 = pl.multiple_of(it * 16 + 16, 16)
            return two_groups(it, carry, rows16(cg_ref, r1), rows16(cu_ref, r1), rows16(da_ref, r1), False)

        g0, u0, da0 = rows16(cg_ref, 0), rows16(cu_ref, 0), rows16(da_ref, 0)
        d0g, d0u = dgroup(g0[0:8], u0[0:8], da0[0:8])
        carry = lax.fori_loop(0, n16 - 1, step, (d0g, d0u, g0[8:16], u0[8:16], da0[8:16]))
        two_groups(n16 - 1, carry, cgn_ref[...].astype(F32), cun_ref[...].astype(F32), dan_ref[...].astype(F32), last)

        @pl.when(first)
        def _():
            ag_ref[...] = jnp.zeros_like(ag_ref)
            au_ref[...] = jnp.zeros_like(au_ref)

        for half, a_ref in enumerate((ag_ref, au_ref)):
            for k in range(4):
                a_ref[k:k + 1, :] += jnp.sum(accs[4 * half + k], axis=0, keepdims=True)

    def nxt16(i):
        return jnp.minimum((i + 1) * n16, t // 16 - 1)

    out_blk = pl.BlockSpec((tt, tc), lambda j, i: (i, j))
    acc_spec = pl.BlockSpec((8, tc), lambda j, i: (0, j))
    nxt_blk = pl.BlockSpec((16, tc), lambda j, i: (nxt16(i), j))
    in_specs = [out_blk, pl.BlockSpec((tt, tc), lambda j, i: (i, j + nf)),
                pl.BlockSpec((FFN_CONV, tc), lambda j, i: (0, j)), pl.BlockSpec((FFN_CONV, tc), lambda j, i: (0, j + nf)),
                out_blk, nxt_blk, out_blk, nxt_blk, out_blk, nxt_blk]
    return pl.pallas_call(
        body, name="ffn_act_bwd", grid=(nf, nt), in_specs=in_specs,
        out_specs=[out_blk, out_blk, acc_spec, acc_spec],
        out_shape=[jax.ShapeDtypeStruct((t, D_FF), BF16), jax.ShapeDtypeStruct((t, D_FF), BF16),
                   jax.ShapeDtypeStruct((8, D_FF), F32), jax.ShapeDtypeStruct((8, D_FF), F32)],
        scratch_shapes=[pltpu.VMEM((8, 8, tc), F32)],
        compiler_params=_cparams(("parallel", "arbitrary")),
    )(hid, hid, cw, cw, conv_g, conv_g, conv_u, conv_u, dact, dact)


def _tri_mask():
    r = lax.broadcasted_iota(jnp.int32, (CHUNK, CHUNK), 0)
    c = lax.broadcasted_iota(jnp.int32, (CHUNK, CHUNK), 1)
    return r >= c


def _gmlp_group_fwd(uv_ref, lng_ref, lnb_ref, ws_ref, bexp_ref, tri, g, want_grad):
    lo, hi = g * 128, (g + 1) * 128
    u_pre = uv_ref[:, lo:hi]
    v_pre = uv_ref[:, D_A + lo:D_A + hi]
    u, du = _gelu_and_grad(u_pre)
    v, dv = _gelu_and_grad(v_pre)
    mu = jnp.mean(v, axis=-1, keepdims=True)
    dc = v - mu
    rs = lax.rsqrt(jnp.mean(dc * dc, axis=-1, keepdims=True) + EPS)
    xh = dc * rs
    vn = (xh * lng_ref[:, lo:hi] + lnb_ref[:, lo:hi]).astype(BF16)
    w = jnp.where(tri, ws_ref[g], 0.0).astype(BF16)
    sg = _dot(w, vn) + bexp_ref[g]
    if want_grad:
        return u, du, dv, rs, xh, vn, w, sg
    return u * sg


def _gmlp_fwd(proj, ln_g, ln_b, w_s, b_exp, na_g):
    t = proj.shape[0]
    ng = N_GROUPS_A

    def body(uv_ref, lng_ref, lnb_ref, ws_ref, bexp_ref, nag_ref, o_ref):
        tri = _tri_mask()
        ys = [_gmlp_group_fwd(uv_ref, lng_ref, lnb_ref, ws_ref, bexp_ref, tri, g, False) for g in range(ng)]
        ssq = ys[0] * 0.0
        for y in ys:
            ssq = ssq + y * y
        r = lax.rsqrt(jnp.sum(ssq, axis=-1, keepdims=True) * (1.0 / D_A) + EPS)
        for g, y in enumerate(ys):
            o_ref[:, g * 128:(g + 1) * 128] = (y * r * nag_ref[:, g * 128:(g + 1) * 128]).astype(BF16)

    vec = pl.BlockSpec((1, D_A), lambda i: (0, 0))
    cube = pl.BlockSpec((ng, CHUNK, CHUNK), lambda i: (0, 0, 0))
    return pl.pallas_call(
        body, name="gmlp_fwd", grid=(t // CHUNK,),
        in_specs=[pl.BlockSpec((CHUNK, 2 * D_A), lambda i: (i, 0)), vec, vec, cube, cube, vec],
        out_specs=pl.BlockSpec((CHUNK, D_A), lambda i: (i, 0)),
        out_shape=jax.ShapeDtypeStruct((t, D_MIX), BF16),
        compiler_params=_cparams(("parallel",)),
    )(proj, ln_g, ln_b, w_s, b_exp, na_g)


def _gmlp_bwd(proj, dyab, ln_g, ln_b, w_s, b_exp, na_g):
    t = proj.shape[0]
    ng = N_GROUPS_A
    nsteps = t // CHUNK

    def body(uv_ref, dy_ref, lng_ref, lnb_ref, ws_ref, bexp_ref, nag_ref,
             duv_ref, dws_ref, dbs_ref, dlng_ref, dlnb_ref, dnag_ref, dbacc):
        i = pl.program_id(0)
        tri = _tri_mask()

        @pl.when(i == 0)
        def _():
            dws_ref[...] = jnp.zeros_like(dws_ref)
            dbacc[...] = jnp.zeros_like(dbacc)
            dlng_ref[...] = jnp.zeros_like(dlng_ref)
            dlnb_ref[...] = jnp.zeros_like(dlnb_ref)
            dnag_ref[...] = jnp.zeros_like(dnag_ref)

        st = [_gmlp_group_fwd(uv_ref, lng_ref, lnb_ref, ws_ref, bexp_ref, tri, g, True) for g in range(ng)]
        ssq = st[0][0] * 0.0
        for s in st:
            y = s[0] * s[7]
            ssq = ssq + y * y
        r = lax.rsqrt(jnp.sum(ssq, axis=-1, keepdims=True) * (1.0 / D_A) + EPS)
        csum = st[0][0] * 0.0
        for g, s in enumerate(st):
            sl = slice(g * 128, (g + 1) * 128)
            xhy = s[0] * s[7] * r
            dya = dy_ref[:, sl].astype(F32)
            dnag_ref[:, sl] += jnp.sum(dya * xhy, axis=0, keepdims=True)
            csum = csum + dya * nag_ref[:, sl] * xhy
        c1 = jnp.sum(csum, axis=-1, keepdims=True) * (1.0 / D_A)
        for g, s in enumerate(st):
            u, du, dv, rs, xh, vn, w, sg = s
            sl = slice(g * 128, (g + 1) * 128)
            dy = r * (dy_ref[:, sl].astype(F32) * nag_ref[:, sl] - u * sg * r * c1)
            dsg = dy * u
            dsg_b = dsg.astype(BF16)
            dws_ref[g] += _dot_nt(dsg_b, vn)
            dbacc[g] += dsg
            dvn = _dot_tn(w, dsg_b)
            dlnb_ref[:, sl] += jnp.sum(dvn, axis=0, keepdims=True)
            dlng_ref[:, sl] += jnp.sum(dvn * xh, axis=0, keepdims=True)
            dxh = dvn * lng_ref[:, sl]
            dvv = rs * (dxh - jnp.mean(dxh, axis=-1, keepdims=True)
                        - xh * jnp.mean(dxh * xh, axis=-1, keepdims=True))
            duv_ref[:, sl] = (dy * sg * du).astype(BF16)
            duv_ref[:, D_A + g * 128:D_A + (g + 1) * 128] = (dvv * dv).astype(BF16)

        @pl.when(i == nsteps - 1)
        def _():
            for g in range(ng):
                dws_ref[g] = jnp.where(tri, dws_ref[g], 0.0)
                dbs_ref[g] = jnp.sum(dbacc[g], axis=-1, keepdims=True)

    vec = pl.BlockSpec((1, D_A), lambda i: (0, 0))
    cube = pl.BlockSpec((ng, CHUNK, CHUNK), lambda i: (0, 0, 0))
    return pl.pallas_call(
        body, name="gmlp_bwd", grid=(nsteps,),
        in_specs=[pl.BlockSpec((CHUNK, 2 * D_A), lambda i: (i, 0)),
                  pl.BlockSpec((CHUNK, D_A), lambda i: (i, 0)), vec, vec, cube, cube, vec],
        out_specs=[pl.BlockSpec((CHUNK, 2 * D_A), lambda i: (i, 0)), cube,
                   pl.BlockSpec((ng, CHUNK, 1), lambda i: (0, 0, 0)), vec, vec, vec],
        out_shape=[jax.ShapeDtypeStruct((t, 2 * D_A), BF16), jax.ShapeDtypeStruct((ng, CHUNK, CHUNK), F32),
                   jax.ShapeDtypeStruct((ng, CHUNK, 1), F32), jax.ShapeDtypeStruct((1, D_A), F32),
                   jax.ShapeDtypeStruct((1, D_A), F32), jax.ShapeDtypeStruct((1, D_A), F32)],
        scratch_shapes=[pltpu.VMEM((ng, CHUNK, CHUNK), F32)],
        compiler_params=_cparams(("arbitrary",)),
    )(proj, dyab, ln_g, ln_b, w_s, b_exp, na_g)


OFF_Z = 2 * D_A
OFF_XS = OFF_Z + D_SSM
OFF_B = OFF_XS + D_SSM
OFF_C = OFF_B + D_BC


def _ssd_consts():
    tri = jnp.tril(jnp.ones((CHUNK, CHUNK), F32)).astype(BF16)
    h = jnp.arange(HPAD)[None, :, None]
    g = jnp.arange(N_SSM_GROUPS)[:, None, None]
    j1 = jnp.arange(GW)[None, None, :]
    eh = (h == g * HEADS_PER_GROUP + j1 // HEAD_DIM).astype(BF16)
    j2 = jnp.arange(HEADS_PER_GROUP * 128)[None, None, :]
    e128 = (h == g * HEADS_PER_GROUP + j2 // 128).astype(BF16)
    return tri, eh, e128


def _ssd_in_specs(cmap):
    def rows(i):
        return cmap(i)

    def prev8(i):
        return jnp.maximum(cmap(i) * (CHUNK // 8) - 1, 0)

    def whole(*shape):
        return pl.BlockSpec(shape, lambda i: (0,) * len(shape))

    bcw = 2 * D_BC
    specs = [
        pl.BlockSpec((CHUNK, D_SSM), lambda i: (rows(i), OFF_Z // D_SSM)),
        pl.BlockSpec((CHUNK, D_SSM), lambda i: (rows(i), OFF_XS // D_SSM)),
        pl.BlockSpec((8, D_SSM), lambda i: (prev8(i), OFF_XS // D_SSM)),
        pl.BlockSpec((CHUNK, bcw), lambda i: (rows(i), OFF_B // bcw)),
        pl.BlockSpec((8, bcw), lambda i: (prev8(i), OFF_B // bcw)),
        pl.BlockSpec((CHUNK, HPAD), lambda i: (rows(i), 0)),
        whole(SSM_CONV, D_XBC), whole(1, D_XBC),
        whole(1, HPAD), whole(1, HPAD),
        whole(1, D_SSM), whole(1, D_SSM),
        whole(CHUNK, CHUNK),
        whole(N_SSM_GROUPS, HPAD, GW), whole(N_SSM_GROUPS, HPAD, HEADS_PER_GROUP * 128),
    ]
    return specs


N_SSD_IN = 15


def _lanes(ref, start, width):
    return ref.at[:, pl.ds(pl.multiple_of(start, 128), width)]


def _ssd_group_refs(ins, g):
    (z_ref, xs_ref, xsp_ref, bc_ref, bcp_ref, dt_ref, cw_ref, cb_ref, dtb_ref, alog_ref, de_ref, gain_ref,
     tri_ref, eh_ref, e128_ref) = ins
    ox, ob, oc = g * GW, g * D_STATE, D_BC + g * D_STATE
    return (_lanes(z_ref, ox, GW), _lanes(xs_ref, ox, GW), _lanes(xsp_ref, ox, GW),
            _lanes(bc_ref, ob, D_STATE), _lanes(bcp_ref, ob, D_STATE),
            _lanes(bc_ref, oc, D_STATE), _lanes(bcp_ref, oc, D_STATE), dt_ref,
            _lanes(cw_ref, ox, GW), _lanes(cw_ref, D_SSM + ob, D_STATE), _lanes(cw_ref, D_SSM + oc, D_STATE),
            _lanes(cb_ref, ox, GW), _lanes(cb_ref, D_SSM + ob, D_STATE), _lanes(cb_ref, D_SSM + oc, D_STATE),
            dtb_ref, alog_ref, _lanes(de_ref, ox, GW), _lanes(gain_ref, ox, GW), tri_ref,
            eh_ref.at[g], e128_ref.at[g])


N_SSD_OWN = 4


def _ssd_scratch():
    return [pltpu.VMEM((CHUNK + 8, GW), F32), pltpu.VMEM((CHUNK + 8, D_STATE), F32),
            pltpu.VMEM((CHUNK + 8, D_STATE), F32), pltpu.VMEM((CHUNK, GW), F32),
            pltpu.VMEM((HPAD, CHUNK), F32), pltpu.VMEM((CHUNK, HPAD), F32), pltpu.VMEM((CHUNK, HPAD), F32)]


def _ssd_two_sets(alloc):
    a, b = alloc[:7], alloc[7:7 + N_SSD_OWN]
    return ((a[0], a[1], a[2], a[4], a[3], a[5], a[6]), (b[0], b[1], b[2], a[4], b[3], a[5], a[6]))


def _ssd_chunk(ins, scr):
    dt_ref, dtb_ref, alog_ref, tri_ref = ins[5], ins[8], ins[9], ins[12]
    acst_sc, dt_sc, acs_sc = scr[3], scr[5], scr[6]
    dt = _softplus(dt_ref[...] + dtb_ref[...])
    acs = _e01x(tri_ref[...], dt * (-jnp.exp(alog_ref[...])))
    dt_sc[...] = dt
    acs_sc[...] = acs
    acst_sc[...] = acs.T


def _ssd_pre(first, g, refs, scr):
    (z_ref, xs_ref, xsp_ref, b_ref, bp_ref, c_ref, cp_ref, dt_ref, cwx, cwb, cwc, cbx, cbb, cbc,
     dtb_ref, alog_ref, de_ref, gain_ref, tri_ref, eh_ref, e128_ref) = refs
    ext_x, ext_b, ext_c, acst_sc, acse_sc, dt_sc, acs_sc = scr
    p = {}
    px = _conv(xs_ref, xsp_ref, cwx, cbx, ext_x, first, SSM_CONV)
    pb = _conv(b_ref, bp_ref, cwb, cbb, ext_b, first, SSM_CONV)
    pc = _conv(c_ref, cp_ref, cwc, cbc, ext_c, first, SSM_CONV)
    p['xs'], p['dsx'] = _silu_and_grad(px)
    p['bm'], p['dsb'] = _silu_and_grad(pb)
    p['cm'], p['dsc'] = _silu_and_grad(pc)
    p['dt_in'] = dt_ref[...] + dtb_ref[...]
    dt = dt_sc[...]
    p['dt'] = dt
    p['a'] = -jnp.exp(alog_ref[...])
    tri_b = tri_ref[...]
    acs = acs_sc[...]
    eh = eh_ref[...]
    p['eh'] = eh
    p['dt_e'] = _x01(dt, eh)
    acs_e = _x01(acs, eh)
    acse_sc[...] = acs_e
    p['acs_e'] = acs_e
    p['acs_c'] = _x01(acs, e128_ref[...])
    p['acs_last_e'] = acse_sc[pl.ds(CHUNK - 1, 1), :]
    p['xdt'] = p['xs'] * p['dt_e']
    p['decay_e'] = jnp.exp(p['acs_last_e'] - acs_e)
    p['cm_b'] = p['cm'].astype(BF16)
    p['bm_b'] = p['bm'].astype(BF16)
    p['scores'] = _dot_nt(p['cm_b'], p['bm_b'])
    p['tri_b'] = tri_b
    return p


def _ssd_l(p, g, r, acst_sc, tri):
    col = p['acs_c'][:, r * 128:(r + 1) * 128]
    row = acst_sc[pl.ds(g * HEADS_PER_GROUP + r, 1), :]
    return jnp.where(tri, jnp.exp(jnp.minimum(col - row, 0.0)), 0.0)


def _ssd_fwd(proj, dt_raw, yab, cw, cb, dtb, alog, de, gain, consts, hosted):
    t = proj.shape[0]
    nc = t // CHUNK
    ng = N_SSM_GROUPS
    tri_c, eh_c, e128_c = consts
    n_in = N_SSD_IN
    nh_in, nh_out = len(hosted['arrays']), len(hosted['out_shape'])
    mid_chunk = (7 * nc) // 8

    def body(*refs):
        ins = refs[:n_in]
        hins = refs[n_in + 1:n_in + 1 + nh_in]
        o0 = n_in + 1 + nh_in
        yb_all, ys_all, hs_all = refs[o0:o0 + 3]
        houts = refs[o0 + 3:o0 + 3 + nh_out]
        h_sc = refs[o0 + 3 + nh_out]
        scr_a, scr_b = _ssd_two_sets(refs[o0 + 4 + nh_out:-2])
        sems = refs[-2:]
        c = pl.program_id(0)
        _host_phase(hosted, 'start', c == 0, hins, houts, sems)
        _host_phase(hosted, 'mid', c == mid_chunk, hins, houts, sems)

        @pl.when(c == 0)
        def _():
            h_sc[...] = jnp.zeros_like(h_sc)

        def group(g, scr):
            grefs = _ssd_group_refs(ins, g)
            z_ref, de_ref, gain_ref = grefs[0], grefs[16], grefs[17]
            yb_ref, ys_ref = _lanes(yb_all, g * GW, GW), _lanes(ys_all, g * GW, GW)
            slab = pl.ds(pl.multiple_of(g * D_STATE, D_STATE), D_STATE)
            p = _ssd_pre(c == 0, g, grefs, scr)
            tri = _tri_mask()
            h_in = h_sc[slab, :]
            hs_all[slab, :] = h_in
            yoff = _dot(p['cm_b'], h_in.astype(BF16)) * jnp.exp(p['acs_e'])
            states = _dot_tn(p['bm_b'], (p['xdt'] * p['decay_e']).astype(BF16))
            lane = lax.broadcasted_iota(jnp.int32, (CHUNK, 128), 1)
            slabs = []
            for r2 in range(HEADS_PER_GROUP // 2):
                xb = p['xdt'][:, r2 * 128:(r2 + 1) * 128].astype(BF16)
                ya = _dot((p['scores'] * _ssd_l(p, g, 2 * r2, scr[3], tri)).astype(BF16), xb)
                yb = _dot((p['scores'] * _ssd_l(p, g, 2 * r2 + 1, scr[3], tri)).astype(BF16), xb)
                slabs.append(jnp.where(lane < HEAD_DIM, ya, yb))
            y = jnp.concatenate(slabs, axis=1) + yoff + de_ref[...] * p['xs']
            ys_ref[...] = y
            h_sc[slab, :] = jnp.exp(p['acs_last_e']) * h_in + states
            zv = z_ref[...]
            yg = y * zv * _sigmoid(zv)
            r = lax.rsqrt(jnp.mean(yg * yg, axis=-1, keepdims=True) + EPS)
            yb_ref[...] = (yg * r * gain_ref[...]).astype(BF16)

        def pair(j, carry):
            group(2 * j, scr_a)
            group(2 * j + 1, scr_b)
            return carry

        _ssd_chunk(ins, scr_a)
        lax.fori_loop(0, ng // 2, pair, 0)
        _host_phase(hosted, 'finish', c == nc - 1, hins, houts, sems)

    h_in, h_out, h_shape, h_scratch, h_alias = _host_plumbing(hosted, n_in + 1, 3)
    in_specs = _ssd_in_specs(lambda i: i) + [pl.BlockSpec(memory_space=pl.ANY)] + h_in
    out_specs = [pl.BlockSpec((CHUNK, D_SSM), lambda i: (i, D_A // D_SSM)),
                 pl.BlockSpec((CHUNK, D_SSM), lambda i: (i, 0)),
                 pl.BlockSpec((None, ng * D_STATE, GW), lambda i: (i, 0, 0))] + h_out
    out_shape = [jax.ShapeDtypeStruct((t, D_MIX), BF16), jax.ShapeDtypeStruct((t, D_SSM), F32),
                 jax.ShapeDtypeStruct((nc, ng * D_STATE, GW), F32)] + h_shape
    return pl.pallas_call(
        body, name="ssd_fwd", grid=(nc,), in_specs=in_specs, out_specs=out_specs, out_shape=out_shape,
        scratch_shapes=[pltpu.VMEM((ng * D_STATE, GW), F32)] + _ssd_scratch() + _ssd_scratch()[:N_SSD_OWN]
        + h_scratch,
        input_output_aliases={n_in: 0, **h_alias},
        compiler_params=_cparams(("arbitrary",)),
    )(proj, proj, proj, proj, proj, dt_raw, cw, cb, dtb, alog, de, gain,
      tri_c, eh_c, e128_c, yab, *hosted['arrays'])


def _rows8(vals, width):
    rid = lax.broadcasted_iota(jnp.int32, (8, width), 0)
    out = jnp.zeros((8, width), F32)
    for k, v in enumerate(vals):
        if v is not None:
            out = out + jnp.where(rid == k, v, 0.0)
    return out


def _ssd_bwd(proj, dt_raw, dyab, ysave, hs, duv, cw, cb, dtb, alog, de, gain, consts, hosted):
    t = proj.shape[0]
    nc = t // CHUNK
    ng = N_SSM_GROUPS
    tri_c, eh_c, e128_c = consts
    n_in = N_SSD_IN
    nh_in, nh_out = len(hosted['arrays']), len(hosted['out_shape'])

    def per_group(g, c, ins, dy_ref, ys_ref, hs_ref, outs, scratch):
        dz_ref, dxs_ref, db_ref, dc_ref, ddt_ref, acc_x, acc_b, acc_c, acc_gain, acc_head = outs
        dh_sc, car_x, car_b, car_c, dext_x, dext_b, dext_c = scratch[:7]
        scr = scratch[7:]
        ext_x, ext_b, ext_c, acst_sc = scr[:4]
        z_ref, cwx, cwb, cwc, de_ref, gain_ref = ins[0], ins[8], ins[9], ins[10], ins[16], ins[17]
        slab = pl.ds(pl.multiple_of(g * D_STATE, D_STATE), D_STATE)
        p = _ssd_pre(c == 0, g, ins, scr)
        tri = _tri_mask()
        xs, dt_e, acs_e, xdt, decay_e = p['xs'], p['dt_e'], p['acs_e'], p['xdt'], p['decay_e']
        cm_b, bm_b, scores, eh = p['cm_b'], p['bm_b'], p['scores'], p['eh']
        h_in = hs_ref[...]
        h_in_b = h_in.astype(BF16)
        e_a = jnp.exp(acs_e)
        raw = _dot(cm_b, h_in_b)

        y = ys_ref[...]
        zv = z_ref[...]
        sz, dsz = _silu_and_grad(zv)
        yg = y * sz
        r = lax.rsqrt(jnp.mean(yg * yg, axis=-1, keepdims=True) + EPS)
        xh = yg * r
        dout = dy_ref[...].astype(F32)
        gain = gain_ref[...]
        dxh = dout * gain
        dyg = r * (dxh - xh * jnp.mean(dxh * xh, axis=-1, keepdims=True))
        dy = dyg * sz
        dz_ref[...] = (dyg * y * dsz).astype(BF16)
        acc_gain[g] += _rows8([jnp.sum(dout * xh, axis=0, keepdims=True)], GW)
        d_skip8 = _x01_nt(_rows8([None, None, jnp.sum(dy * xs, axis=0, keepdims=True)], GW), eh)
        dxs = de_ref[...] * dy

        q = dy * raw * e_a
        draw = (dy * e_a).astype(BF16)
        d_c = _dot_nt(draw, h_in_b)
        dh_in = _dot_tn(cm_b, draw)

        lane = lax.broadcasted_iota(jnp.int32, (CHUNK, 128), 1)
        ones_b = jnp.ones((CHUNK, 128), BF16)
        dscores = jnp.zeros((CHUNK, CHUNK), F32)
        dxdt_slabs, q_slabs = [], []
        for r2 in range(HEADS_PER_GROUP // 2):
            sl = slice(r2 * 128, (r2 + 1) * 128)
            xb = xdt[:, sl].astype(BF16)
            dys = dy[:, sl]
            dys_b = dys.astype(BF16)
            dxh_pair, qv_pair = [], []
            for half in range(2):
                lmat = _ssd_l(p, g, 2 * r2 + half, acst_sc, tri)
                m = scores * lmat
                mine = (lane < HEAD_DIM) if half == 0 else (lane >= HEAD_DIM)
                dm = _dot_nt(jnp.where(mine, dys, 0.0).astype(BF16), xb)
                dscores = dscores + dm * lmat
                gm = dm * m
                dxh_pair.append(_dot_tn(m.astype(BF16), dys_b))
                h3 = _split3(gm)
                colsum = _dot_tn(h3[0], ones_b) + _dot_tn(h3[1], ones_b) + _dot_tn(h3[2], ones_b)
                qv_pair.append(jnp.sum(gm, axis=-1, keepdims=True) - colsum)
            dxdt_slabs.append(jnp.where(lane < HEAD_DIM, dxh_pair[0], dxh_pair[1]))
            q_slabs.append(jnp.where(lane == 0, qv_pair[0], 0.0) + jnp.where(lane == HEAD_DIM, qv_pair[1], 0.0))
        dxdt = jnp.concatenate(dxdt_slabs, axis=1)
        q = q + jnp.concatenate(q_slabs, axis=1)

        dh_out = dh_sc[slab, :]
        dh_out_b = dh_out.astype(BF16)
        e_l = jnp.exp(p['acs_last_e'])
        dh_sc[slab, :] = dh_in + e_l * dh_out
        dlast = jnp.sum(dh_out * h_in, axis=0, keepdims=True) * e_l
        dxd = _dot(bm_b, dh_out_b)
        xd = xdt * decay_e
        dxdt = dxdt + dxd * decay_e
        dd = dxd * xd
        q = q - dd
        dlast = dlast + jnp.sum(dd, axis=0, keepdims=True)
        d_b = _dot_nt(xd.astype(BF16), dh_out_b)
        dsc_b = dscores.astype(BF16)
        d_c = d_c + _dot(dsc_b, bm_b)
        d_b = d_b + _dot_tn(dsc_b, cm_b)

        dxs = dxs + dxdt * dt_e
        rid = lax.broadcasted_iota(jnp.int32, (CHUNK, GW), 0)
        q = q + jnp.where(rid == CHUNK - 1, dlast, 0.0)
        dacs = _x01_nt(q, eh)
        ddt = _x01_nt(dxdt * xs, eh)
        dadt = _e01x_tn(p['tri_b'], dacs)
        ddt = ddt + dadt * p['a']
        d_a = jnp.sum(dadt * p['dt'], axis=0, keepdims=True)
        ddt_raw = ddt * _sigmoid(p['dt_in'])
        acc_head[...] += _rows8([jnp.sum(ddt_raw, axis=0, keepdims=True), d_a * p['a']], HPAD) + d_skip8

        @pl.when(g == 0)
        def _():
            ddt_ref[...] = ddt_raw

        @pl.when(g > 0)
        def _():
            ddt_ref[...] += ddt_raw

        for dv, dsil, ext, dext, car, acc, w_ref, o_ref in (
                (dxs, p['dsx'], ext_x, dext_x, car_x, acc_x, cwx, dxs_ref),
                (d_b, p['dsb'], ext_b, dext_b, car_b, acc_b, cwb, db_ref),
                (d_c, p['dsc'], ext_c, dext_c, car_c, acc_c, cwc, dc_ref)):
            dp = dv * dsil
            width = dp.shape[1]
            rows = [jnp.sum(ext[pl.ds(5 + k, CHUNK), :] * dp, axis=0, keepdims=True) for k in range(SSM_CONV)]
            rows.append(jnp.sum(dp, axis=0, keepdims=True))
            acc[g] += _rows8(rows, width)
            dext[0:CHUNK, :] = dp
            dext[CHUNK:CHUNK + 8, :] = car[g]
            car[g] = dext[0:8, :]
            dx = w_ref[SSM_CONV - 1:SSM_CONV, :] * dext[pl.ds(0, CHUNK), :]
            for k in range(SSM_CONV - 1):
                dx = dx + w_ref[k:k + 1, :] * dext[pl.ds(SSM_CONV - 1 - k, CHUNK), :]
            o_ref[...] = dx.astype(BF16)

    def body(*refs):
        ins = refs[:n_in]
        dy_all, ys_all, hs_all, duv_ref = refs[n_in:n_in + 4]
        hins = refs[n_in + 4:n_in + 4 + nh_in]
        o0 = n_in + 4 + nh_in
        dproj_ref, ddt_ref = refs[o0:o0 + 2]
        accs = refs[o0 + 2:o0 + 7]
        houts = refs[o0 + 7:o0 + 7 + nh_out]
        scratch = refs[o0 + 7 + nh_out:-2]
        sems = refs[-2:]
        cc = pl.program_id(0)
        _host_phase(hosted, 'start', cc == 0, hins, houts, sems)
        dproj_ref[:, 0:2 * D_A] = duv_ref[...]

        @pl.when(cc == 0)
        def _():
            for a in tuple(accs) + tuple(scratch[:4]):
                a[...] = jnp.zeros_like(a)

        shared = tuple(scratch[:4])
        scr_a, scr_b = _ssd_two_sets(tuple(scratch[7:14]) + tuple(scratch[17:17 + N_SSD_OWN]))
        set_a = shared + tuple(scratch[4:7]) + scr_a
        set_b = shared + tuple(scratch[14:17]) + scr_b

        def group(g, own):
            slab = pl.ds(pl.multiple_of(g * D_STATE, D_STATE), D_STATE)
            outs = (_lanes(dproj_ref, OFF_Z + g * GW, GW), _lanes(dproj_ref, OFF_XS + g * GW, GW),
                    _lanes(dproj_ref, OFF_B + g * D_STATE, D_STATE), _lanes(dproj_ref, OFF_C + g * D_STATE, D_STATE),
                    ddt_ref) + tuple(accs)
            per_group(g, nc - 1 - cc, _ssd_group_refs(ins, g), _lanes(dy_all, g * GW, GW),
                      _lanes(ys_all, g * GW, GW), hs_all.at[slab, :], outs, own)

        def pair(j, carry):
            group(2 * j, set_a)
            group(2 * j + 1, set_b)
            return carry

        _ssd_chunk(ins, scr_a)
        lax.fori_loop(0, ng // 2, pair, 0)
        _host_phase(hosted, 'finish', cc == nc - 1, hins, houts, sems)

    def cmap(i):
        return nc - 1 - i

    in_specs = _ssd_in_specs(cmap) + [
        pl.BlockSpec((CHUNK, D_SSM), lambda i: (cmap(i), D_A // D_SSM)),
        pl.BlockSpec((CHUNK, D_SSM), lambda i: (cmap(i), 0)),
        pl.BlockSpec((None, ng * D_STATE, GW), lambda i: (cmap(i), 0, 0)),
        pl.BlockSpec((CHUNK, 2 * D_A), lambda i: (cmap(i), 0)),
    ]

    def full(shape):
        return pl.BlockSpec(shape, lambda i: (0,) * len(shape))

    out_specs = [
        pl.BlockSpec((CHUNK, D_MAIN), lambda i: (cmap(i), 0)),
        pl.BlockSpec((CHUNK, HPAD), lambda i: (cmap(i), 0)),
        full((ng, 8, GW)), full((ng, 8, D_STATE)), full((ng, 8, D_STATE)),
        full((ng, 8, GW)), full((8, HPAD)),
    ]
    out_shape = [
        jax.ShapeDtypeStruct((t, D_MAIN), BF16),
        jax.ShapeDtypeStruct((t, HPAD), F32),
        jax.ShapeDtypeStruct((ng, 8, GW), F32), jax.ShapeDtypeStruct((ng, 8, D_STATE), F32),
        jax.ShapeDtypeStruct((ng, 8, D_STATE), F32), jax.ShapeDtypeStruct((ng, 8, GW), F32),
        jax.ShapeDtypeStruct((8, HPAD), F32),
    ]
    scratch = [pltpu.VMEM((ng * D_STATE, GW), F32),
               pltpu.VMEM((ng, 8, GW), F32), pltpu.VMEM((ng, 8, D_STATE), F32), pltpu.VMEM((ng, 8, D_STATE), F32),
               pltpu.VMEM((CHUNK + 8, GW), F32), pltpu.VMEM((CHUNK + 8, D_STATE), F32),
               pltpu.VMEM((CHUNK + 8, D_STATE), F32)] + _ssd_scratch()
    scratch += scratch[4:7] + _ssd_scratch()[:N_SSD_OWN]
    h_in, h_out, h_shape, h_scratch, h_alias = _host_plumbing(hosted, n_in + 4, len(out_shape))
    return pl.pallas_call(
        body, name="ssd_bwd", grid=(nc,), in_specs=in_specs + h_in, out_specs=out_specs + h_out,
        out_shape=out_shape + h_shape, scratch_shapes=scratch + h_scratch, input_output_aliases=h_alias,
        compiler_params=_cparams(("arbitrary",)),
    )(proj, proj, proj, proj, proj, dt_raw, cw, cb, dtb, alog, de, gain,
      tri_c, eh_c, e128_c, dyab, ysave, hs, duv, *hosted['arrays'])


ANY = pl.BlockSpec(memory_space=pl.ANY)


def _place():
    x, y, c = lax.axis_index("x"), lax.axis_index("y"), lax.axis_index("c")
    chips = [(1 - x, y), (x, 1 - y), (1 - x, 1 - y)]
    return x, y, c, chips


def _rcopy(src, dst, send_sems, recv_sems, k, dev):
    return pltpu.make_async_remote_copy(src_ref=src, dst_ref=dst, send_sem=send_sems.at[k],
                                        recv_sem=recv_sems.at[k], device_id=dev, device_id_type=MESH)


def _my_chip():
    return 2 * lax.axis_index("x") + lax.axis_index("y")


def _cast_into_slot(w, name):
    r, c = w.shape
    tr = _row_tile(r, c, 2)

    def body(w_ref, o_ref):
        o_ref[...] = w_ref[...].astype(BF16)

    return pl.pallas_call(
        body, name=name, grid=(r // tr,), in_specs=[pl.BlockSpec((tr, c), lambda i: (i, 0))],
        out_specs=pl.BlockSpec((None, tr, c), lambda i: (_my_chip(), i, 0)),
        out_shape=jax.ShapeDtypeStruct((N_CHIPS, r, c), BF16), compiler_params=_cparams(("parallel",)),
    )(w)


def _hosted_gather(bigs):
    nb = len(bigs)

    def rows(a, c):
        half = bigs[a].shape[1] // 2
        return pl.ds(c * half, half)

    def start(ins, outs, send_sems, recv_sems):
        x, y, c, chips = _place()
        q = 2 * x + y
        for a in range(nb):
            for k, chip in enumerate(chips):
                _rcopy(outs[a].at[q, rows(a, c)], outs[a].at[q, rows(a, c)], send_sems, recv_sems, 6 * a + k,
                       (chip[0], chip[1], c)).start()

    def mid(ins, outs, send_sems, recv_sems):
        x, y, c, chips = _place()
        sib = (x, y, 1 - c)
        for a in range(nb):
            for k, chip in enumerate(chips):
                slab = outs[a].at[2 * chip[0] + chip[1], rows(a, c)]
                _rcopy(slab, slab, send_sems, recv_sems, 6 * a + k, sib).wait_recv()
                _rcopy(slab, slab, send_sems, recv_sems, 6 * a + 3 + k, sib).start()

    def finish(ins, outs, send_sems, recv_sems):
        x, y, c, chips = _place()
        q = 2 * x + y
        sib = (x, y, 1 - c)
        for a in range(nb):
            for k, chip in enumerate(chips):
                qk = 2 * chip[0] + chip[1]
                other = outs[a].at[qk, rows(a, 1 - c)]
                _rcopy(other, other, send_sems, recv_sems, 6 * a + 3 + k, sib).wait_recv()
                mine = outs[a].at[q, rows(a, c)]
                _rcopy(mine, mine, send_sems, recv_sems, 6 * a + k, sib).wait_send()
                fwd = outs[a].at[qk, rows(a, c)]
                _rcopy(fwd, fwd, send_sems, recv_sems, 6 * a + 3 + k, sib).wait_send()

    return dict(arrays=list(bigs), out_shape=[jax.ShapeDtypeStruct(b.shape, b.dtype) for b in bigs],
                aliases={a: a for a in range(nb)}, nsem=6 * nb, start=start, mid=mid, finish=finish)


def _hosted_rs_chips(ps):
    na = len(ps)

    def copies(ins, outs, send_sems, recv_sems):
        x, y, c, chips = _place()
        return [_rcopy(ins[a].at[2 * chip[0] + chip[1]], outs[a].at[k], send_sems, recv_sems, 3 * a + k,
                       (chip[0], chip[1], c)) for a in range(na) for k, chip in enumerate(chips)]

    def start(ins, outs, send_sems, recv_sems):
        for cp in copies(ins, outs, send_sems, recv_sems):
            cp.start()

    def finish(ins, outs, send_sems, recv_sems):
        for cp in copies(ins, outs, send_sems, recv_sems):
            cp.wait()

    return dict(arrays=list(ps), out_shape=[jax.ShapeDtypeStruct((3,) + p.shape[1:], p.dtype) for p in ps],
                aliases={}, nsem=3 * na, start=start, mid=None, finish=finish)


def _hosted_rs_sibling(gs):
    na = len(gs)

    def copies(ins, outs, send_sems, recv_sems):
        x, y, c, _ = _place()
        halves = [g.shape[1] // 2 for g in gs]
        return [_rcopy(ins[a].at[:, pl.ds((1 - c) * halves[a], halves[a]), :], outs[a], send_sems, recv_sems, a,
                       (x, y, 1 - c)) for a in range(na)]

    def start(ins, outs, send_sems, recv_sems):
        for cp in copies(ins, outs, send_sems, recv_sems):
            cp.start()

    def finish(ins, outs, send_sems, recv_sems):
        for cp in copies(ins, outs, send_sems, recv_sems):
            cp.wait()

    return dict(arrays=list(gs), aliases={}, nsem=na, start=start, mid=None, finish=finish,
                out_shape=[jax.ShapeDtypeStruct((N_CHIPS, g.shape[1] // 2, g.shape[2]), g.dtype) for g in gs])


def _hosted_share_sibling(fs):
    na = len(fs)

    def rows(a, c):
        half = fs[a].shape[0] // 2
        return pl.ds(c * half, half)

    def start(ins, outs, send_sems, recv_sems):
        x, y, c, _ = _place()
        for a in range(na):
            _rcopy(outs[a].at[rows(a, c)], outs[a].at[rows(a, c)], send_sems, recv_sems, a, (x, y, 1 - c)).start()

    def finish(ins, outs, send_sems, recv_sems):
        x, y, c, _ = _place()
        for a in range(na):
            _rcopy(outs[a].at[rows(a, c)], outs[a].at[rows(a, c)], send_sems, recv_sems, a, (x, y, 1 - c)).wait_send()
            other = outs[a].at[rows(a, 1 - c)]
            _rcopy(other, other, send_sems, recv_sems, a, (x, y, 1 - c)).wait_recv()

    return dict(arrays=list(fs), out_shape=[jax.ShapeDtypeStruct(f.shape, f.dtype) for f in fs],
                aliases={a: a for a in range(na)}, nsem=na, start=start, mid=None, finish=finish)


def _hosted_allgather(buf):
    def copies(ins, outs, send_sems, recv_sems):
        x, y, c, _ = _place()
        me = 4 * x + 2 * y + c
        cps = []
        for k in range(1, 8):
            dev = (1 - x if k & 4 else x, 1 - y if k & 2 else y, 1 - c if k & 1 else c)
            cps.append(_rcopy(ins[0], outs[0].at[me], send_sems, recv_sems, k - 1, dev))
        return pltpu.make_async_copy(ins[0], outs[0].at[me], send_sems.at[7]), cps

    def start(ins, outs, send_sems, recv_sems):
        loc, cps = copies(ins, outs, send_sems, recv_sems)
        loc.start()
        for cp in cps:
            cp.start()

    def finish(ins, outs, send_sems, recv_sems):
        loc, cps = copies(ins, outs, send_sems, recv_sems)
        loc.wait()
        for cp in cps:
            cp.wait()

    return dict(arrays=[buf], out_shape=[jax.ShapeDtypeStruct((8,) + buf.shape, buf.dtype)], aliases={},
                nsem=8, start=start, mid=None, finish=finish)


class _SemWindow:
    def __init__(self, sems, off):
        self._sems, self._off = sems, off

    @property
    def at(self):
        return self

    def __getitem__(self, k):
        return self._sems.at[k + self._off]


def _hosted_join(parts):
    arrays, out_shape, aliases, spans, nsem = [], [], {}, [], 0
    for h in parts:
        spans.append((len(arrays), len(h['arrays']), len(out_shape), len(h['out_shape']), nsem))
        aliases.update({len(arrays) + a: len(out_shape) + b for a, b in h['aliases'].items()})
        arrays += h['arrays']
        out_shape += h['out_shape']
        nsem += h['nsem']

    def phase(name):
        if all(h[name] is None for h in parts):
            return None

        def run(ins, outs, send_sems, recv_sems):
            for h, (ia, na, io, no, s0) in zip(parts, spans):
                if h[name] is not None:
                    h[name](ins[ia:ia + na], outs[io:io + no], _SemWindow(send_sems, s0), _SemWindow(recv_sems, s0))

        return run

    return dict(arrays=arrays, out_shape=out_shape, aliases=aliases, nsem=nsem,
                start=phase('start'), mid=phase('mid'), finish=phase('finish'))


def _run_hosted(hosted, name):
    nh_in, nh_out = len(hosted['arrays']), len(hosted['out_shape'])

    def body(*refs):
        ins, outs, sems = refs[:nh_in], refs[nh_in:nh_in + nh_out], refs[-2:]
        for ph in ('start', 'mid', 'finish'):
            if hosted[ph] is not None:
                hosted[ph](ins, outs, sems[0], sems[1])

    h_in, h_out, h_shape, h_scratch, h_alias = _host_plumbing(hosted, 0, 0)
    return pl.pallas_call(body, name=name, in_specs=h_in, out_specs=h_out, out_shape=h_shape,
                          scratch_shapes=h_scratch, input_output_aliases=h_alias)(*hosted['arrays'])


def _host_plumbing(hosted, n_in, n_out):
    nh = len(hosted['arrays'])
    return ([ANY] * nh, [ANY] * len(hosted['out_shape']), list(hosted['out_shape']),
            [pltpu.SemaphoreType.DMA((hosted['nsem'],)), pltpu.SemaphoreType.DMA((hosted['nsem'],))],
            {n_in + a: n_out + b for a, b in hosted['aliases'].items()})


def _host_phase(hosted, phase, when, hins, houts, sems):
    fn = hosted[phase]
    if fn is None:
        return

    @pl.when(when)
    def _():
        fn(hins, houts, sems[0], sems[1])


def _gather_weights(bigs, smalls):
    nb, ns = len(bigs), len(smalls)
    na = nb + ns
    nsem = 6 * nb + 3 * ns
    big = _hosted_gather(bigs)

    def body(*refs):
        ins, outs = refs[:na], refs[na:2 * na]
        send_sems, recv_sems, loc_sems = refs[2 * na:]
        x, y, c, chips = _place()
        q = 2 * x + y
        sib = (x, y, 1 - c)
        locs, sends = [], []
        for s in range(ns):
            cp = pltpu.make_async_copy(ins[nb + s], outs[nb + s].at[q], loc_sems.at[s])
            cp.start()
            locs.append(cp)
        big['start'](ins[:nb], outs[:nb], send_sems, recv_sems)
        for s in range(ns):
            a = nb + s
            for k, chip in enumerate(chips):
                cp = _rcopy(ins[a], outs[a].at[q], send_sems, recv_sems, 6 * nb + 3 * s + k,
                            (chip[0], chip[1], c))
                cp.start()
                sends.append(cp)
        big['mid'](ins[:nb], outs[:nb], send_sems, recv_sems)
        big['finish'](ins[:nb], outs[:nb], send_sems, recv_sems)
        for s in range(ns):
            a = nb + s
            for k, chip in enumerate(chips):
                qk = 2 * chip[0] + chip[1]
                _rcopy(ins[a], outs[a].at[qk], send_sems, recv_sems, 6 * nb + 3 * s + k, sib).wait_recv()
        for cp in sends:
            cp.wait_send()
        for cp in locs:
            cp.wait()

    arrs = list(bigs) + list(smalls)
    out_shape = ([jax.ShapeDtypeStruct(a.shape, a.dtype) for a in bigs]
                 + [jax.ShapeDtypeStruct((N_CHIPS,) + a.shape, a.dtype) for a in smalls])
    return pl.pallas_call(
        body, name="gather_weights", in_specs=[ANY] * na, out_specs=[ANY] * na, out_shape=out_shape,
        input_output_aliases={a: a for a in range(nb)},
        scratch_shapes=[pltpu.SemaphoreType.DMA((nsem,)), pltpu.SemaphoreType.DMA((nsem,)),
                        pltpu.SemaphoreType.DMA((max(ns, 1),))],
    )(*arrs)


EW_VMEM_BUDGET = 24 * 1024 * 1024


def _row_tile(rows, cols, nbuf):
    budget = EW_VMEM_BUDGET // (nbuf * cols * 4 * 2)
    return _tile(rows, max(16, budget - budget % 16), 16) if rows % 16 == 0 else rows


def _add_pairs(g, rcv, name):
    s, half, c = rcv.shape
    tr = _row_tile(half, c, 3)
    nh = half // tr

    def body(a_ref, b_ref, o_ref):
        o_ref[...] = (a_ref[...].astype(F32) + b_ref[...].astype(F32)).astype(o_ref.dtype)

    blk = pl.BlockSpec((None, tr, c), lambda j, i: (j, i, 0))
    mine = pl.BlockSpec((None, tr, c), lambda j, i: (j, lax.axis_index("c") * nh + i, 0))
    return pl.pallas_call(
        body, name=name, grid=(s, nh), in_specs=[mine, blk], out_specs=blk,
        out_shape=jax.ShapeDtypeStruct(rcv.shape, rcv.dtype), compiler_params=_cparams(("parallel", "parallel")),
    )(g, rcv)


def _sum_chips(part, rcv, name):
    _, half, c = part.shape
    tr = _row_tile(half, c, 5)
    nh = half // tr

    def body(o_ref, r_ref, out_ref):
        acc = o_ref[...].astype(F32)
        for k in range(3):
            acc = acc + r_ref[k].astype(F32)
        out_ref[...] = acc

    return pl.pallas_call(
        body, name=name, grid=(nh,),
        in_specs=[pl.BlockSpec((None, tr, c), lambda i: (_my_chip(), i, 0)),
                  pl.BlockSpec((3, tr, c), lambda i: (0, i, 0))],
        out_specs=pl.BlockSpec((tr, c), lambda i: (lax.axis_index("c") * nh + i, 0)),
        out_shape=jax.ShapeDtypeStruct((2 * half, c), F32), compiler_params=_cparams(("parallel",)),
    )(part, rcv)


def _sum_devices(parts, name):
    _, n, _ = parts.shape
    tr = n if n <= 4096 else _tile(n, 512, 8)

    def body(p_ref, o_ref):
        acc = p_ref[0]
        for k in range(1, 8):
            acc = acc + p_ref[k]
        o_ref[...] = acc

    return pl.pallas_call(
        body, name=name, grid=(n // tr,),
        in_specs=[pl.BlockSpec((8, tr, 128), lambda i: (0, i, 0))],
        out_specs=pl.BlockSpec((tr, 128), lambda i: (i, 0)),
        out_shape=jax.ShapeDtypeStruct((n, 128), F32), compiler_params=_cparams(("parallel",)),
    )(parts)


def _adamw(w, g, m, v, name):
    r, c = w.shape
    tr = _row_tile(r, c, 7)
    c1 = 1.0 - ADAM_B1 ** ADAM_STEP
    c2 = 1.0 - ADAM_B2 ** ADAM_STEP

    def body(w_ref, g_ref, m_ref, v_ref, d_ref, mo_ref, vo_ref):
        gv = g_ref[...]
        mn = ADAM_B1 * m_ref[...] + (1.0 - ADAM_B1) * gv
        vn = ADAM_B2 * v_ref[...] + (1.0 - ADAM_B2) * (gv * gv)
        mo_ref[...] = mn
        vo_ref[...] = vn
        m_hat = mn / c1
        v_hat = vn / c2
        d_ref[...] = -ADAM_LR * (m_hat / (jnp.sqrt(v_hat) + ADAM_EPS) + ADAM_WD * w_ref[...])

    blk = pl.BlockSpec((tr, c), lambda i: (i, 0))
    sh = jax.ShapeDtypeStruct((r, c), F32)
    return pl.pallas_call(
        body, name=name, grid=(r // tr,), in_specs=[blk] * 4, out_specs=[blk] * 3, out_shape=[sh] * 3,
        compiler_params=_cparams(("parallel",)),
    )(w, g, m, v)


WEIGHTS = ['norm_mix_g', 'w_in', 'ln_a_g', 'ln_a_b', 'w_s', 'b_s', 'norm_a_g', 'conv_ssm_w', 'conv_ssm_b',
           'dt_bias', 'a_log', 'd_skip', 'ssm_norm_g', 'w_out', 'norm_ffn_g', 'w_up', 'conv_ffn_w',
           'conv_ffn_b', 'w_down', 'norm_ple_g', 'w_ple_gate', 'w_ple', 'norm_final_g']
BIG = ['w_in', 'w_out', 'w_up', 'w_down', 'w_ple_gate', 'w_ple']
SMALL = [n for n in WEIGHTS if n not in BIG]
PACK_ALIGN = 2048


def _pack(arrs):
    parts = []
    for a in arrs:
        f = a.reshape(-1).astype(F32)
        parts.append(jnp.pad(f, (0, (-f.shape[0]) % PACK_ALIGN)))
    return jnp.concatenate(parts).reshape(-1, 128)


def _unpack(buf, shapes):
    flat = buf.reshape(-1)
    out, off = [], 0
    for s in shapes:
        n = math.prod(s)
        out.append(flat[off:off + n].reshape(s))
        off += n + (-n) % PACK_ALIGN
    return out


def _pad_heads(v):
    return jnp.pad(v, ((0, 0), (0, HPAD - v.shape[1])))


def _col_sharded(full):
    r, c4 = full.shape
    return jnp.transpose(full.reshape(r, N_CHIPS, c4 // N_CHIPS), (1, 0, 2))


def _from_col_sharded(g):
    s, r, c = g.shape
    return jnp.transpose(g, (1, 0, 2)).reshape(r, s * c)


def kernel(x, p, norm_mix_g, w_in, ln_a_g, ln_a_b, w_s, b_s, norm_a_g, conv_ssm_w, conv_ssm_b, dt_bias, a_log, d_skip, ssm_norm_g, w_out, norm_ffn_g, w_up, conv_ffn_w, conv_ffn_b, w_down, norm_ple_g, w_ple_gate, w_ple, norm_final_g, loss_target, m_norm_mix_g, m_w_in, m_ln_a_g, m_ln_a_b, m_w_s, m_b_s, m_norm_a_g, m_conv_ssm_w, m_conv_ssm_b, m_dt_bias, m_a_log, m_d_skip, m_ssm_norm_g, m_w_out, m_norm_ffn_g, m_w_up, m_conv_ffn_w, m_conv_ffn_b, m_w_down, m_norm_ple_g, m_w_ple_gate, m_w_ple, m_norm_final_g, v_norm_mix_g, v_w_in, v_ln_a_g, v_ln_a_b, v_w_s, v_b_s, v_norm_a_g, v_conv_ssm_w, v_conv_ssm_b, v_dt_bias, v_a_log, v_d_skip, v_ssm_norm_g, v_w_out, v_norm_ffn_g, v_w_up, v_conv_ffn_w, v_conv_ffn_b, v_w_down, v_norm_ple_g, v_w_ple_gate, v_w_ple, v_norm_final_g):
    given = dict(locals())
    wts = {n: given[n] for n in WEIGHTS}
    mom = {n: given['m_' + n] for n in WEIGHTS}
    var = {n: given['v_' + n] for n in WEIGHTS}
    d = D_MODEL
    xt, pt, tgt = x[0], p[0, 0], loss_target[0]
    chip = 2 * lax.axis_index("x") + lax.axis_index("y")

    slots = {n: _cast_into_slot(wts[n][0], "cast_" + n) for n in BIG}
    g_in, g_cs, g_cf = _gather_weights([slots['w_in']], [conv_ssm_w[0], conv_ffn_w[0]])
    w_in_full = _from_col_sharded(g_in)
    w_main = w_in_full
    w_dt = _pad_heads(w_in_full[:, D_MAIN:])
    cs_w = _from_col_sharded(g_cs)
    cf_w = _from_col_sharded(g_cf)
    consts = _ssd_consts()
    dtb, alog = _pad_heads(dt_bias), _pad_heads(a_log)
    de = jnp.repeat(d_skip[0], HEAD_DIM)[None, :]
    b_exp = jnp.broadcast_to(b_s[0][:, :, None], (N_GROUPS_A, CHUNK, CHUNK))

    a1 = _rms_fwd(xt, norm_mix_g, "rms_mix")
    proj, g_out = _matmul(a1, w_main, mode='nn', name="mm_proj", tm=1024, tn=1024, b_cols=D_MAIN,
                          hosted=_hosted_gather([slots['w_out']]))
    dt_raw = _matmul(a1, w_dt, mode='nn', name="mm_dt", tm=1024, tn=128)
    yab = _gmlp_fwd(proj, ln_a_g, ln_a_b, w_s[0], b_exp, norm_a_g)
    yab, ysave, hs, g_up = _ssd_fwd(proj, dt_raw, yab, cs_w, conv_ssm_b, dtb, alog, de, ssm_norm_g, consts,
                                    _hosted_gather([slots['w_up']]))
    w_out_f = g_out.reshape(D_MIX, d)
    h1 = _matmul(yab, w_out_f, mode='nn', name="mm_out", res=xt, tm=1024, tn=1024, tk=2048)
    f = _rms_fwd(h1, norm_ffn_g, "rms_ffn")
    hid, g_down, g_pg, g_ple = _matmul(
        f, g_up, mode='nn', name="mm_up", b_sharded=True, tm=1024, tn=1408,
        hosted=_hosted_gather([slots[n] for n in ('w_down', 'w_ple_gate', 'w_ple')]))
    w_down_f = g_down.reshape(D_FF, d)
    w_pg_f = g_pg.reshape(d, d)
    act, conv_g, conv_u = _ffn_act_fwd(hid, cf_w, conv_ffn_b)
    h2 = _matmul(act, w_down_f, mode='nn', name="mm_down", res=h1, tm=1024, tn=1024, tk=2816)
    n3 = _rms_fwd(h2, norm_ple_g, "rms_ple")
    gl = _matmul(n3, w_pg_f, mode='nn', name="mm_pg", tm=1024, tn=1024)
    pe = _matmul(pt, g_ple, mode='nn', name="mm_ple", b_sharded=True, tm=1024, tn=512)
    dh3, dgl, dpe, lossv, dgf = _tail(h2, gl, pe, tgt, norm_final_g[None, :])

    gs_ple = _matmul(pt, dpe, mode='tn', name="mm_dw_ple", out_dtype=BF16, out_shards=N_CHIPS,
                     tm=256, tn=512, tk=2048)
    gs_pg = _matmul(n3, dgl, mode='tn', name="mm_dw_pg", out_dtype=BF16, tm=1024, tn=1024, tk=4096)
    dn3 = _matmul(dgl, w_pg_f, mode='nt', name="mm_dn3", out_dtype=BF16, tm=1024, tn=1024)
    dh2, dg_ple, dh2_b = _rms_bwd(h2, norm_ple_g, dn3, dh3, "rms_ple_bwd", True)
    dact = _matmul(dh2_b, w_down_f, mode='nt', name="mm_dact", out_dtype=BF16, tm=1024, tn=1408)
    gs_down = _matmul(act, dh2_b, mode='tn', name="mm_dw_down", out_dtype=BF16, tm=1408, tn=1024, tk=2048)
    dpg, dpu, wg_acc, wu_acc = _ffn_act_bwd(hid, cf_w, conv_g, conv_u, dact)
    hc = N_CHIPS // 2
    gs_up = jnp.concatenate(
        [_matmul(f, dpg, mode='tn', name="mm_dw_up_g", out_dtype=BF16, out_shards=hc, tm=1024, tn=1408, tk=2048),
         _matmul(f, dpu, mode='tn', name="mm_dw_up_u", out_dtype=BF16, out_shards=hc, tm=1024, tn=1408, tk=2048)],
        axis=0)
    early = [gs_up, gs_down.reshape(N_CHIPS, D_FF // N_CHIPS, d), gs_pg.reshape(N_CHIPS, d // N_CHIPS, d), gs_ple]
    df, *sib_e = _matmul(dpg, g_up, mode='nt', name="mm_df_g", b_sharded=True, tm=1024, tn=1024, tk=2816,
                         hosted=_hosted_rs_sibling(early))
    part_e = [_add_pairs(a, b, "rs_add_e%d" % i) for i, (a, b) in enumerate(zip(early, sib_e))]
    df = _matmul(dpu, g_up, mode='nt', name="mm_df_u", b_sharded=True, b_shard_off=hc, res=df, out_dtype=BF16,
                 tm=1024, tn=1024, tk=2816)
    dh1, dg_ffn, dh1_b = _rms_bwd(h1, norm_ffn_g, df, dh2, "rms_ffn_bwd", True)
    dyab = _matmul(dh1_b, w_out_f, mode='nt', name="mm_dyab", out_dtype=BF16, tm=1024, tn=1024)
    gs_out = _matmul(yab, dh1_b, mode='tn', name="mm_dw_out", out_dtype=BF16, tm=1024, tn=1024, tk=4096)
    duv, dws, dbs, dlng, dlnb, dnag = _gmlp_bwd(proj, dyab, ln_a_g, ln_a_b, w_s[0], b_exp, norm_a_g)
    dproj, ddt_raw, acc_x, acc_b, acc_c, acc_gain, acc_head, *rcv_e = _ssd_bwd(
        proj, dt_raw, dyab, ysave, hs, duv, cs_w, conv_ssm_b, dtb, alog, de, ssm_norm_g, consts,
        _hosted_rs_chips(part_e))
    dw_main = _matmul(a1, dproj, mode='tn', name="mm_dw_main", out_dtype=BF16, tm=1024, tn=1024, tk=4096)
    dw_dt = _matmul(a1, ddt_raw, mode='tn', name="mm_dw_dt", out_dtype=BF16, tm=1024, tn=128, tk=2048)
    gs_in = _col_sharded(jnp.concatenate([dw_main, dw_dt[:, :N_HEADS]], axis=1))
    late = [gs_in, gs_out.reshape(N_CHIPS, D_MIX // N_CHIPS, d)]
    part_l = [_add_pairs(a, b, "rs_add_l%d" % i) for i, (a, b) in enumerate(
        zip(late, _run_hosted(_hosted_rs_sibling(late), "rs_sibling_late")))]

    def conv_rows(acc, k):
        return acc[:, k, :].reshape(1, -1)

    dcw = jnp.concatenate([jnp.concatenate([conv_rows(acc_x, k), conv_rows(acc_b, k), conv_rows(acc_c, k)], axis=1)
                           for k in range(SSM_CONV)], axis=0)
    dcb = jnp.concatenate([conv_rows(acc_x, SSM_CONV), conv_rows(acc_b, SSM_CONV), conv_rows(acc_c, SSM_CONV)], axis=1)
    part = {
        'ln_a_g': dlng, 'ln_a_b': dlnb, 'w_s': dws, 'b_s': dbs, 'norm_a_g': dnag,
        'conv_ssm_w': dcw, 'conv_ssm_b': dcb,
        'dt_bias': acc_head[0:1, :N_HEADS], 'a_log': acc_head[1:2, :N_HEADS], 'd_skip': acc_head[2:3, :N_HEADS],
        'ssm_norm_g': acc_gain[:, 0, :], 'norm_ffn_g': dg_ffn,
        'conv_ffn_w': jnp.concatenate([wg_acc[:FFN_CONV], wu_acc[:FFN_CONV]], axis=1),
        'conv_ffn_b': jnp.concatenate([wg_acc[FFN_CONV:FFN_CONV + 1], wu_acc[FFN_CONV:FFN_CONV + 1]], axis=1),
        'norm_ple_g': dg_ple, 'norm_final_g': dgf,
    }
    full_shapes = {n: wts[n].shape for n in SMALL}
    full_shapes['conv_ssm_w'] = (1, SSM_CONV, D_XBC)
    full_shapes['conv_ffn_w'] = (1, FFN_CONV, 2 * D_FF)
    small_e = [n for n in SMALL if n != 'norm_mix_g']
    packed = _pack([part[n] for n in small_e] + [lossv[:, 0:1]])

    halves_e = [_sum_chips(a, b, "rs_sum_e%d" % i) for i, (a, b) in enumerate(zip(part_e, rcv_e))]
    da_dt = _matmul(ddt_raw, w_dt, mode='nt', name="mm_da_dt", tm=1024, tn=1024)
    da, *moved = _matmul(dproj, w_main, mode='nt', name="mm_da", res=da_dt, out_dtype=BF16, tm=1024, tn=1024, tk=2560,
                         hosted=_hosted_join([_hosted_rs_chips(part_l), _hosted_share_sibling(halves_e),
                                              _hosted_allgather(packed)]))
    rcv_l, g_early, gathered = moved[:len(late)], moved[len(late):-1], moved[-1]
    dx, dg_mix = _rms_bwd(xt, norm_mix_g, da, dh1, "rms_mix_bwd", False)

    halves_l = [_sum_chips(a, b, "rs_sum_l%d" % i) for i, (a, b) in enumerate(zip(part_l, rcv_l))]
    g_big = dict(zip(['w_up', 'w_down', 'w_ple_gate', 'w_ple'], g_early))
    g_big.update(zip(['w_in', 'w_out'], _run_hosted(_hosted_share_sibling(halves_l), "share_sibling_late")))

    pieces = _unpack(_sum_devices(gathered, "sum_devices"), [full_shapes[n] for n in small_e] + [(1,)])
    g_small = dict(zip(small_e, pieces[:-1]))
    loss = pieces[-1][0]
    mix = _sum_devices(_run_hosted(_hosted_allgather(_pack([dg_mix])), "allgather_mix")[0], "sum_devices_mix")
    g_small['norm_mix_g'] = _unpack(mix, [full_shapes['norm_mix_g']])[0]
    for n in ('conv_ssm_w', 'conv_ffn_w'):
        width = wts[n].shape[2]
        g_small[n] = lax.dynamic_slice_in_dim(g_small[n], chip * width, width, axis=2)

    grads, delta, new_m, new_v = {}, {}, {}, {}
    for n in BIG:
        shp = wts[n].shape
        dl, mn, vn = _adamw(wts[n][0], g_big[n], mom[n][0], var[n][0], "adamw_" + n)
        grads[n], delta[n], new_m[n], new_v[n] = (g_big[n].reshape(shp), dl.reshape(shp), mn.reshape(shp),
                                                  vn.reshape(shp))
    shapes = [wts[n].shape for n in SMALL]
    dl, mn, vn = _adamw(_pack([wts[n] for n in SMALL]), _pack([g_small[n] for n in SMALL]),
                        _pack([mom[n] for n in SMALL]), _pack([var[n] for n in SMALL]), "adamw_small")
    for n, a, b, c in zip(SMALL, _unpack(dl, shapes), _unpack(mn, shapes), _unpack(vn, shapes)):
        grads[n], delta[n], new_m[n], new_v[n] = g_small[n], a, b, c

    return (loss, dx[None], *[grads[n] for n in WEIGHTS], *[delta[n] for n in WEIGHTS],
            *[new_m[n] for n in WEIGHTS], *[new_v[n] for n in WEIGHTS])
```

```python
import functools
import math

import jax
import jax.numpy as jnp
from jax import lax
from jax.experimental import pallas as pl
from jax.experimental.pallas import tpu as pltpu

D_MODEL = 2048
SEQ = 8192
D_MIX = 2 * D_MODEL
D_A = D_MIX // 2
CHUNK = 128
N_GROUPS_A = D_A // 128
D_SSM = D_MIX - D_A
HEAD_DIM = 64
N_HEADS = D_SSM // HEAD_DIM
HEADS_PER_GROUP = 4
N_SSM_GROUPS = N_HEADS // HEADS_PER_GROUP
GW = HEADS_PER_GROUP * HEAD_DIM
D_STATE = 128
SSM_CONV = 4
D_BC = N_SSM_GROUPS * D_STATE
D_XBC = D_SSM + 2 * D_BC
D_MAIN = 2 * D_A + D_SSM + D_XBC
D_IN = D_MAIN + N_HEADS
D_FF = (D_MODEL * 11) // 4
FFN_CONV = 3
D_PLE = 256
EPS = 1e-6
HPAD = 128
N_CHIPS = 4

ADAM_LR = 0.001
ADAM_B1 = 0.9
ADAM_B2 = 0.999
ADAM_EPS = 1e-08
ADAM_WD = 0.01
ADAM_STEP = 10

F32 = jnp.float32
BF16 = jnp.bfloat16
MESH = pl.DeviceIdType.MESH
VMEM_LIMIT = 56 * 1024 * 1024


def _cparams(sem):
    return pltpu.CompilerParams(dimension_semantics=sem, vmem_limit_bytes=VMEM_LIMIT)


def _tile(n, pref, mult):
    t = min(pref, n)
    t -= t % mult
    while n % t:
        t -= mult
    return t


def _dot(a, b):
    return jnp.dot(a, b, preferred_element_type=F32)


def _dot_nt(a, b):
    return lax.dot_general(a, b, (((1,), (1,)), ((), ())), preferred_element_type=F32)


def _dot_tn(a, b):
    return lax.dot_general(a, b, (((0,), (0,)), ((), ())), preferred_element_type=F32)


def _split3(x):
    hi = x.astype(BF16)
    r = x - hi.astype(F32)
    mid = r.astype(BF16)
    lo = (r - mid.astype(F32)).astype(BF16)
    return hi, mid, lo


def _x01(x, e):
    h, m, l = _split3(x)
    return _dot(h, e) + _dot(m, e) + _dot(l, e)


def _x01_nt(x, e):
    h, m, l = _split3(x)
    return _dot_nt(h, e) + _dot_nt(m, e) + _dot_nt(l, e)


def _e01x(e, x):
    h, m, l = _split3(x)
    return _dot(e, h) + _dot(e, m) + _dot(e, l)


def _e01x_tn(e, x):
    h, m, l = _split3(x)
    return _dot_tn(e, h) + _dot_tn(e, m) + _dot_tn(e, l)


def _sigmoid(x):
    return 1.0 / (1.0 + jnp.exp(-x))


_GELU_C = math.sqrt(2.0 / math.pi)


def _gelu_and_grad(x):
    x2 = x * x
    th = jnp.tanh(_GELU_C * (x + 0.044715 * x * x2))
    y = 0.5 * x * (1.0 + th)
    dy = 0.5 * (1.0 + th) + 0.5 * x * (1.0 - th * th) * (_GELU_C * (1.0 + 3.0 * 0.044715 * x2))
    return y, dy


def _silu_and_grad(x):
    s = _sigmoid(x)
    return x * s, s * (1.0 + x * (1.0 - s))


def _softplus(x):
    u = jnp.exp(-jnp.abs(x))
    w = 1.0 + u
    l1p = jnp.where(w == 1.0, u, jnp.log(w) * (u / (w - 1.0)))
    return jnp.maximum(x, 0.0) + l1p


def _matmul(a, b, *, mode, name, out_dtype=F32, res=None, tm=512, tn=512, tk=2048,
            b_sharded=False, b_shard_off=0, b_cols=None, out_shards=0, hosted=None):
    if mode == 'tn':
        kdim, m = a.shape
        n = b.shape[1]
    else:
        m, kdim = a.shape
        if b_sharded:
            s_b, d1, d2 = b.shape
            n = s_b * d2 if mode == 'nn' else d1
        else:
            n = b.shape[1] if mode == 'nn' else b.shape[0]
        if b_cols is not None:
            n = b_cols
    per = None
    if b_sharded:
        per = b.shape[2]
    if out_shards:
        per = n // out_shards
    tm = _tile(m, tm, 128 if mode == 'tn' else 8)
    if mode == 'nt' and b_sharded:
        tn = _tile(n, tn, 128)
        tk = _tile(per, tk, 128)
    elif per is not None:
        tn = _tile(per, tn, 128)
        tk = _tile(kdim, tk, 128 if mode != 'tn' else 8)
    else:
        tn = _tile(n, tn, 128)
        tk = _tile(kdim, tk, 128 if mode != 'tn' else 8)
    nm, nn_, nk = m // tm, n // tn, kdim // tk
    has_res = res is not None
    n_in = 2 + has_res
    nh_in = len(hosted['arrays']) if hosted else 0
    nh_out = len(hosted['out_shape']) if hosted else 0

    def body(*refs):
        a_ref, b_ref = refs[0], refs[1]
        res_ref = refs[2] if has_res else None
        o_ref = refs[n_in + nh_in]
        if hosted:
            hins = refs[n_in:n_in + nh_in]
            houts = refs[n_in + nh_in + 1:n_in + nh_in + 1 + nh_out]
            sems = refs[-2:]
            ids = [pl.program_id(d) for d in range(3)]
            step = (ids[0] * nm + ids[1]) * nk + ids[2]
            n_steps = nn_ * nm * nk
            at_first, at_mid, at_last = step == 0, step == (7 * n_steps) // 8, step == n_steps - 1
            _host_phase(hosted, 'start', at_first, hins, houts, sems)
        av = a_ref[...].astype(BF16)
        bv = b_ref[...].astype(BF16)
        if mode == 'nn':
            p = _dot(av, bv)
        elif mode == 'nt':
            p = _dot_nt(av, bv)
        else:
            p = _dot_tn(av, bv)

        def fin(v):
            if has_res:
                v = v + res_ref[...]
            o_ref[...] = v.astype(o_ref.dtype)

        if nk == 1:
            fin(p)
        else:
            acc_ref = refs[n_in + nh_in + 1 + nh_out]
            k = pl.program_id(2)

            @pl.when(k == 0)
            def _():
                acc_ref[...] = p

            @pl.when(k > 0)
            def _():
                acc_ref[...] += p

            @pl.when(k == nk - 1)
            def _():
                fin(acc_ref[...])
        if hosted:
            _host_phase(hosted, 'mid', at_mid, hins, houts, sems)
            _host_phase(hosted, 'finish', at_last, hins, houts, sems)

    if mode == 'nn':
        a_spec = pl.BlockSpec((tm, tk), lambda j, i, k: (i, k))
        if b_sharded:
            nps = per // tn
            b_spec = pl.BlockSpec((None, tk, tn), lambda j, i, k: (j // nps, k, j % nps))
        else:
            b_spec = pl.BlockSpec((tk, tn), lambda j, i, k: (k, j))
    elif mode == 'nt':
        a_spec = pl.BlockSpec((tm, tk), lambda j, i, k: (i, k))
        if b_sharded:
            kps = per // tk
            b_spec = pl.BlockSpec((None, tn, tk), lambda j, i, k: (k // kps + b_shard_off, j, k % kps))
        else:
            b_spec = pl.BlockSpec((tn, tk), lambda j, i, k: (j, k))
    else:
        a_spec = pl.BlockSpec((tk, tm), lambda j, i, k: (k, i))
        b_spec = pl.BlockSpec((tk, tn), lambda j, i, k: (k, j))
    in_specs = [a_spec, b_spec]
    args = [a, b]
    if has_res:
        in_specs.append(pl.BlockSpec((tm, tn), lambda j, i, k: (i, j)))
        args.append(res)
    if out_shards:
        nps_o = per // tn
        out_shape = jax.ShapeDtypeStruct((out_shards, m, per), out_dtype)
        out_spec = pl.BlockSpec((None, tm, tn), lambda j, i, k: (j // nps_o, i, j % nps_o))
    else:
        out_shape = jax.ShapeDtypeStruct((m, n), out_dtype)
        out_spec = pl.BlockSpec((tm, tn), lambda j, i, k: (i, j))
    scratch = [pltpu.VMEM((tm, tn), F32)] if nk > 1 else []
    if not hosted:
        return pl.pallas_call(
            body, name=name, grid=(nn_, nm, nk), in_specs=in_specs, out_specs=out_spec,
            out_shape=out_shape, scratch_shapes=scratch,
            compiler_params=_cparams(("parallel", "parallel", "arbitrary")),
        )(*args)
    h_in, h_out, h_shape, h_scratch, h_alias = _host_plumbing(hosted, n_in, 1)
    return pl.pallas_call(
        body, name=name, grid=(nn_, nm, nk), in_specs=in_specs + h_in, out_specs=[out_spec] + h_out,
        out_shape=[out_shape] + h_shape, scratch_shapes=scratch + h_scratch, input_output_aliases=h_alias,
        compiler_params=_cparams(("arbitrary", "arbitrary", "arbitrary")),
    )(*args, *hosted['arrays'])


def _rms_fwd(x, g, name):
    t, d = x.shape
    tt = _tile(t, 512, 8)

    def body(x_ref, g_ref, o_ref):
        xv = x_ref[...]
        r = lax.rsqrt(jnp.mean(xv * xv, axis=-1, keepdims=True) + EPS)
        o_ref[...] = (xv * r * g_ref[...]).astype(o_ref.dtype)

    return pl.pallas_call(
        body, name=name, grid=(t // tt,),
        in_specs=[pl.BlockSpec((tt, d), lambda i: (i, 0)), pl.BlockSpec((1, d), lambda i: (0, 0))],
        out_specs=pl.BlockSpec((tt, d), lambda i: (i, 0)),
        out_shape=jax.ShapeDtypeStruct((t, d), BF16),
        compiler_params=_cparams(("parallel",)),
    )(x, g)


def _rms_bwd(x, g, dy, dres, name, also_bf16):
    t, d = x.shape
    tt = _tile(t, 512, 16)

    def body(x_ref, g_ref, dy_ref, dres_ref, dx_ref, dg_ref, *dxb):
        i = pl.program_id(0)
        xv = x_ref[...]
        r = lax.rsqrt(jnp.mean(xv * xv, axis=-1, keepdims=True) + EPS)
        xh = xv * r
        dyv = dy_ref[...].astype(F32)
        dxh = dyv * g_ref[...]
        c = jnp.mean(dxh * xh, axis=-1, keepdims=True)
        dx = dres_ref[...] + r * (dxh - xh * c)
        dx_ref[...] = dx
        for dxb_ref in dxb:
            dxb_ref[...] = dx.astype(BF16)
        part = jnp.sum(dyv * xh, axis=0, keepdims=True)

        @pl.when(i == 0)
        def _():
            dg_ref[...] = part

        @pl.when(i > 0)
        def _():
            dg_ref[...] += part

    row = pl.BlockSpec((tt, d), lambda i: (i, 0))
    vec = pl.BlockSpec((1, d), lambda i: (0, 0))
    return pl.pallas_call(
        body, name=name, grid=(t // tt,),
        in_specs=[row, vec, row, row], out_specs=[row, vec] + [row] * also_bf16,
        out_shape=[jax.ShapeDtypeStruct((t, d), F32), jax.ShapeDtypeStruct((1, d), F32)]
        + [jax.ShapeDtypeStruct((t, d), BF16)] * also_bf16,
        compiler_params=_cparams(("arbitrary",)),
    )(x, g, dy, dres)


def _tail(h2, gl, pe, target, gfin):
    t, d = h2.shape
    tt = _tile(t, 256, 8)

    def body(h2_ref, gl_ref, pe_ref, tg_ref, gf_ref, dh3_ref, dgl_ref, dpe_ref, loss_ref, dgf_ref):
        i = pl.program_id(0)
        sig = _sigmoid(gl_ref[...])
        pev = pe_ref[...]
        h3 = h2_ref[...] + sig * pev
        r = lax.rsqrt(jnp.mean(h3 * h3, axis=-1, keepdims=True) + EPS)
        xh = h3 * r
        gf = gf_ref[...]
        e = xh * gf - tg_ref[...]
        dy = e * (1.0 / d)
        dxh = dy * gf
        c = jnp.mean(dxh * xh, axis=-1, keepdims=True)
        dh3 = r * (dxh - xh * c)
        dh3_ref[...] = dh3
        dgl_ref[...] = (dh3 * pev * sig * (1.0 - sig)).astype(BF16)
        dpe_ref[...] = (dh3 * sig).astype(BF16)
        lpart = jnp.sum(e * e, axis=0, keepdims=True) * (0.5 / d)
        gpart = jnp.sum(dy * xh, axis=0, keepdims=True)

        @pl.when(i == 0)
        def _():
            loss_ref[...] = lpart
            dgf_ref[...] = gpart

        @pl.when(i > 0)
        def _():
            loss_ref[...] += lpart
            dgf_ref[...] += gpart

        @pl.when(i == t // tt - 1)
        def _():
            loss_ref[...] = jnp.broadcast_to(jnp.sum(loss_ref[...], axis=-1, keepdims=True), (1, d))

    row = pl.BlockSpec((tt, d), lambda i: (i, 0))
    vec = pl.BlockSpec((1, d), lambda i: (0, 0))
    return pl.pallas_call(
        body, name="tail", grid=(t // tt,),
        in_specs=[row, row, row, row, vec], out_specs=[row, row, row, vec, vec],
        out_shape=[jax.ShapeDtypeStruct((t, d), F32), jax.ShapeDtypeStruct((t, d), BF16),
                   jax.ShapeDtypeStruct((t, d), BF16), jax.ShapeDtypeStruct((1, d), F32),
                   jax.ShapeDtypeStruct((1, d), F32)],
        compiler_params=_cparams(("arbitrary",)),
    )(h2, gl, pe, target, gfin)


def _conv(cur_ref, prev_ref, w_ref, b_ref, ext_ref, first, width):
    rows = cur_ref.shape[0]
    ext_ref[0:8, :] = jnp.where(first, 0.0, prev_ref[...])
    ext_ref[8:8 + rows, :] = cur_ref[...]
    acc = b_ref[...]
    for k in range(width):
        acc = acc + w_ref[k:k + 1, :] * ext_ref[pl.ds(9 - width + k, rows), :]
    return acc


def _ffn_specs(t, tt, tc, nf):
    hb = tt // 8
    cur_g = pl.BlockSpec((tt, tc), lambda j, i: (i, j))
    cur_u = pl.BlockSpec((tt, tc), lambda j, i: (i, j + nf))
    prev_g = pl.BlockSpec((8, tc), lambda j, i: (jnp.maximum(i * hb - 1, 0), j))
    prev_u = pl.BlockSpec((8, tc), lambda j, i: (jnp.maximum(i * hb - 1, 0), j + nf))
    w_g = pl.BlockSpec((FFN_CONV, tc), lambda j, i: (0, j))
    w_u = pl.BlockSpec((FFN_CONV, tc), lambda j, i: (0, j + nf))
    b_g = pl.BlockSpec((1, tc), lambda j, i: (0, j))
    b_u = pl.BlockSpec((1, tc), lambda j, i: (0, j + nf))
    return [cur_g, prev_g, cur_u, prev_u, w_g, w_u, b_g, b_u]


FFN_TC = 512


def _shift_down(prev, cur, n, rid):
    return jnp.where(rid < n, pltpu.roll(prev, n, 0), pltpu.roll(cur, n, 0))


def _shift_up(cur, nxt, n, rid):
    return jnp.where(rid < 8 - n, pltpu.roll(cur, 8 - n, 0), pltpu.roll(nxt, 8 - n, 0))


def _conv3_group(prev, cur, w_ref, b_ref, rid):
    x1 = _shift_down(prev, cur, 1, rid)
    x2 = _shift_down(prev, cur, 2, rid)
    return b_ref[...] + w_ref[2:3, :] * cur + w_ref[1:2, :] * x1 + w_ref[0:1, :] * x2


def _ffn_act_fwd(hid, cw, cb):
    t = hid.shape[0]
    tt = _tile(t, 2048, 16)
    tc = _tile(D_FF, FFN_TC, 128)
    nf = D_FF // tc

    def body(g_ref, gp_ref, u_ref, up_ref, wg_ref, wu_ref, bg_ref, bu_ref, o_ref, cg_ref, cu_ref):
        first = pl.program_id(1) == 0
        rid = lax.broadcasted_iota(jnp.int32, (8, tc), 0)

        def step(s, carry):
            pg, pu = carry
            r0 = pl.multiple_of(s * 16, 16)
            g0, g1 = g_ref[pl.ds(r0, 8), :], g_ref[pl.ds(r0 + 8, 8), :]
            u0, u1 = u_ref[pl.ds(r0, 8), :], u_ref[pl.ds(r0 + 8, 8), :]
            gate = jnp.concatenate([_conv3_group(pg, g0, wg_ref, bg_ref, rid),
                                    _conv3_group(g0, g1, wg_ref, bg_ref, rid)], axis=0)
            up = jnp.concatenate([_conv3_group(pu, u0, wu_ref, bu_ref, rid),
                                  _conv3_group(u0, u1, wu_ref, bu_ref, rid)], axis=0)
            cg_ref[pl.ds(r0, 16), :] = gate.astype(BF16)
            cu_ref[pl.ds(r0, 16), :] = up.astype(BF16)
            o_ref[pl.ds(r0, 16), :] = (gate * _sigmoid(gate) * up).astype(BF16)
            return g1, u1

        init = (jnp.where(first, 0.0, gp_ref[...]), jnp.where(first, 0.0, up_ref[...]))
        lax.fori_loop(0, tt // 16, step, init)

    blk = pl.BlockSpec((tt, tc), lambda j, i: (i, j))
    return pl.pallas_call(
        body, name="ffn_act_fwd", grid=(nf, t // tt), in_specs=_ffn_specs(t, tt, tc, nf),
        out_specs=[blk, blk, blk],
        out_shape=[jax.ShapeDtypeStruct((t, D_FF), BF16)] * 3,
        compiler_params=_cparams(("parallel", "arbitrary")),
    )(hid, hid, hid, hid, cw, cw, cb, cb)


def _ffn_act_bwd(hid, cw, conv_g, conv_u, dact):
    t = hid.shape[0]
    tt = _tile(t, 2048, 16)
    tc = _tile(D_FF, FFN_TC, 128)
    nf = D_FF // tc
    nt = t // tt
    n16 = tt // 16

    def body(g_ref, u_ref, wg_ref, wu_ref, cg_ref, cgn_ref, cu_ref, cun_ref, da_ref, dan_ref,
             og_ref, ou_ref, ag_ref, au_ref, accs):
        i = pl.program_id(1)
        first, last = i == 0, i == nt - 1
        rid = lax.broadcasted_iota(jnp.int32, (8, tc), 0)
        accs[...] = jnp.zeros_like(accs)

        def dgroup(gate, up, da):
            sv, sgr = _silu_and_grad(gate)
            return da * up * sgr, da * sv

        def finish(x, d0, d1, w_ref):
            s1 = _shift_up(d0, d1, 1, rid)
            s2 = _shift_up(d0, d1, 2, rid)
            dpre = w_ref[2:3, :] * d0 + w_ref[1:2, :] * s1 + w_ref[0:1, :] * s2
            return dpre, (x * s2, x * s1, x * d0, d0)

        def two_groups(it, carry, gate_blk, up_blk, da_blk, zero_ahead):
            d0g, d0u, gate1, up1, da1 = carry
            r0 = it * 16 if isinstance(it, int) else pl.multiple_of(it * 16, 16)
            x0g, x0u = g_ref[pl.ds(r0, 8), :], u_ref[pl.ds(r0, 8), :]
            x1g, x1u = g_ref[pl.ds(r0 + 8, 8), :], u_ref[pl.ds(r0 + 8, 8), :]
            d1g, d1u = dgroup(gate1, up1, da1)
            d2g, d2u = dgroup(gate_blk[0:8], up_blk[0:8], da_blk[0:8])
            d2g = jnp.where(zero_ahead, 0.0, d2g)
            d2u = jnp.where(zero_ahead, 0.0, d2u)
            for half, (xa, xb, da_, db_, dc_, w_ref, o_ref) in enumerate((
                    (x0g, x1g, d0g, d1g, d2g, wg_ref, og_ref), (x0u, x1u, d0u, d1u, d2u, wu_ref, ou_ref))):
                pa, prods_a = finish(xa, da_, db_, w_ref)
                pb, prods_b = finish(xb, db_, dc_, w_ref)
                o_ref[pl.ds(r0, 16), :] = jnp.concatenate([pa, pb], axis=0).astype(BF16)
                for k in range(4):
                    accs[4 * half + k] += prods_a[k] + prods_b[k]
            return d2g, d2u, gate_blk[8:16], up_blk[8:16], da_blk[8:16]

        def rows16(ref, r):
            return ref[pl.ds(r, 16), :].astype(F32)

        def step(it, carry):
            r1 = pl.multiple_of(it * 16 + 16, 16)
            return two_groups(it, carry, rows16(cg_ref, r1), rows16(cu_ref, r1), rows16(da_ref, r1), False)

        g0, u0, da0 = rows16(cg_ref, 0), rows16(cu_ref, 0), rows16(da_ref, 0)
        d0g, d0u = dgroup(g0[0:8], u0[0:8], da0[0:8])
        carry = lax.fori_loop(0, n16 - 1, step, (d0g, d0u, g0[8:16], u0[8:16], da0[8:16]))
        two_groups(n16 - 1, carry, cgn_ref[...].astype(F32), cun_ref[...].astype(F32), dan_ref[...].astype(F32), last)

        @pl.when(first)
        def _():
            ag_ref[...] = jnp.zeros_like(ag_ref)
            au_ref[...] = jnp.zeros_like(au_ref)

        for half, a_ref in enumerate((ag_ref, au_ref)):
            for k in range(4):
                a_ref[k:k + 1, :] += jnp.sum(accs[4 * half + k], axis=0, keepdims=True)

    def nxt16(i):
        return jnp.minimum((i + 1) * n16, t // 16 - 1)

    out_blk = pl.BlockSpec((tt, tc), lambda j, i: (i, j))
    acc_spec = pl.BlockSpec((8, tc), lambda j, i: (0, j))
    nxt_blk = pl.BlockSpec((16, tc), lambda j, i: (nxt16(i), j))
    in_specs = [out_blk, pl.BlockSpec((tt, tc), lambda j, i: (i, j + nf)),
                pl.BlockSpec((FFN_CONV, tc), lambda j, i: (0, j)), pl.BlockSpec((FFN_CONV, tc), lambda j, i: (0, j + nf)),
                out_blk, nxt_blk, out_blk, nxt_blk, out_blk, nxt_blk]
    return pl.pallas_call(
        body, name="ffn_act_bwd", grid=(nf, nt), in_specs=in_specs,
        out_specs=[out_blk, out_blk, acc_spec, acc_spec],
        out_shape=[jax.ShapeDtypeStruct((t, D_FF), BF16), jax.ShapeDtypeStruct((t, D_FF), BF16),
                   jax.ShapeDtypeStruct((8, D_FF), F32), jax.ShapeDtypeStruct((8, D_FF), F32)],
        scratch_shapes=[pltpu.VMEM((8, 8, tc), F32)],
        compiler_params=_cparams(("parallel", "arbitrary")),
    )(hid, hid, cw, cw, conv_g, conv_g, conv_u, conv_u, dact, dact)


def _tri_mask():
    r = lax.broadcasted_iota(jnp.int32, (CHUNK, CHUNK), 0)
    c = lax.broadcasted_iota(jnp.int32, (CHUNK, CHUNK), 1)
    return r >= c


def _gmlp_group_fwd(uv_ref, lng_ref, lnb_ref, ws_ref, bexp_ref, tri, g, want_grad):
    lo, hi = g * 128, (g + 1) * 128
    u_pre = uv_ref[:, lo:hi]
    v_pre = uv_ref[:, D_A + lo:D_A + hi]
    u, du = _gelu_and_grad(u_pre)
    v, dv = _gelu_and_grad(v_pre)
    mu = jnp.mean(v, axis=-1, keepdims=True)
    dc = v - mu
    rs = lax.rsqrt(jnp.mean(dc * dc, axis=-1, keepdims=True) + EPS)
    xh = dc * rs
    vn = (xh * lng_ref[:, lo:hi] + lnb_ref[:, lo:hi]).astype(BF16)
    w = jnp.where(tri, ws_ref[g], 0.0).astype(BF16)
    sg = _dot(w, vn) + bexp_ref[g]
    if want_grad:
        return u, du, dv, rs, xh, vn, w, sg
    return u * sg


def _gmlp_fwd(proj, ln_g, ln_b, w_s, b_exp, na_g):
    t = proj.shape[0]
    ng = N_GROUPS_A

    def body(uv_ref, lng_ref, lnb_ref, ws_ref, bexp_ref, nag_ref, o_ref):
        tri = _tri_mask()
        ys = [_gmlp_group_fwd(uv_ref, lng_ref, lnb_ref, ws_ref, bexp_ref, tri, g, False) for g in range(ng)]
        ssq = ys[0] * 0.0
        for y in ys:
            ssq = ssq + y * y
        r = lax.rsqrt(jnp.sum(ssq, axis=-1, keepdims=True) * (1.0 / D_A) + EPS)
        for g, y in enumerate(ys):
            o_ref[:, g * 128:(g + 1) * 128] = (y * r * nag_ref[:, g * 128:(g + 1) * 128]).astype(BF16)

    vec = pl.BlockSpec((1, D_A), lambda i: (0, 0))
    cube = pl.BlockSpec((ng, CHUNK, CHUNK), lambda i: (0, 0, 0))
    return pl.pallas_call(
        body, name="gmlp_fwd", grid=(t // CHUNK,),
        in_specs=[pl.BlockSpec((CHUNK, 2 * D_A), lambda i: (i, 0)), vec, vec, cube, cube, vec],
        out_specs=pl.BlockSpec((CHUNK, D_A), lambda i: (i, 0)),
        out_shape=jax.ShapeDtypeStruct((t, D_MIX), BF16),
        compiler_params=_cparams(("parallel",)),
    )(proj, ln_g, ln_b, w_s, b_exp, na_g)


def _gmlp_bwd(proj, dyab, ln_g, ln_b, w_s, b_exp, na_g):
    t = proj.shape[0]
    ng = N_GROUPS_A
    nsteps = t // CHUNK

    def body(uv_ref, dy_ref, lng_ref, lnb_ref, ws_ref, bexp_ref, nag_ref,
             duv_ref, dws_ref, dbs_ref, dlng_ref, dlnb_ref, dnag_ref, dbacc):
        i = pl.program_id(0)
        tri = _tri_mask()

        @pl.when(i == 0)
        def _():
            dws_ref[...] = jnp.zeros_like(dws_ref)
            dbacc[...] = jnp.zeros_like(dbacc)
            dlng_ref[...] = jnp.zeros_like(dlng_ref)
            dlnb_ref[...] = jnp.zeros_like(dlnb_ref)
            dnag_ref[...] = jnp.zeros_like(dnag_ref)

        st = [_gmlp_group_fwd(uv_ref, lng_ref, lnb_ref, ws_ref, bexp_ref, tri, g, True) for g in range(ng)]
        ssq = st[0][0] * 0.0
        for s in st:
            y = s[0] * s[7]
            ssq = ssq + y * y
        r = lax.rsqrt(jnp.sum(ssq, axis=-1, keepdims=True) * (1.0 / D_A) + EPS)
        csum = st[0][0] * 0.0
        for g, s in enumerate(st):
            sl = slice(g * 128, (g + 1) * 128)
            xhy = s[0] * s[7] * r
            dya = dy_ref[:, sl].astype(F32)
            dnag_ref[:, sl] += jnp.sum(dya * xhy, axis=0, keepdims=True)
            csum = csum + dya * nag_ref[:, sl] * xhy
        c1 = jnp.sum(csum, axis=-1, keepdims=True) * (1.0 / D_A)
        for g, s in enumerate(st):
            u, du, dv, rs, xh, vn, w, sg = s
            sl = slice(g * 128, (g + 1) * 128)
            dy = r * (dy_ref[:, sl].astype(F32) * nag_ref[:, sl] - u * sg * r * c1)
            dsg = dy * u
            dsg_b = dsg.astype(BF16)
            dws_ref[g] += _dot_nt(dsg_b, vn)
            dbacc[g] += dsg
            dvn = _dot_tn(w, dsg_b)
            dlnb_ref[:, sl] += jnp.sum(dvn, axis=0, keepdims=True)
            dlng_ref[:, sl] += jnp.sum(dvn * xh, axis=0, keepdims=True)
            dxh = dvn * lng_ref[:, sl]
            dvv = rs * (dxh - jnp.mean(dxh, axis=-1, keepdims=True)
                        - xh * jnp.mean(dxh * xh, axis=-1, keepdims=True))
            duv_ref[:, sl] = (dy * sg * du).astype(BF16)
            duv_ref[:, D_A + g * 128:D_A + (g + 1) * 128] = (dvv * dv).astype(BF16)

        @pl.when(i == nsteps - 1)
        def _():
            for g in range(ng):
                dws_ref[g] = jnp.where(tri, dws_ref[g], 0.0)
                dbs_ref[g] = jnp.sum(dbacc[g], axis=-1, keepdims=True)

    vec = pl.BlockSpec((1, D_A), lambda i: (0, 0))
    cube = pl.BlockSpec((ng, CHUNK, CHUNK), lambda i: (0, 0, 0))
    return pl.pallas_call(
        body, name="gmlp_bwd", grid=(nsteps,),
        in_specs=[pl.BlockSpec((CHUNK, 2 * D_A), lambda i: (i, 0)),
                  pl.BlockSpec((CHUNK, D_A), lambda i: (i, 0)), vec, vec, cube, cube, vec],
        out_specs=[pl.BlockSpec((CHUNK, 2 * D_A), lambda i: (i, 0)), cube,
                   pl.BlockSpec((ng, CHUNK, 1), lambda i: (0, 0, 0)), vec, vec, vec],
        out_shape=[jax.ShapeDtypeStruct((t, 2 * D_A), BF16), jax.ShapeDtypeStruct((ng, CHUNK, CHUNK), F32),
                   jax.ShapeDtypeStruct((ng, CHUNK, 1), F32), jax.ShapeDtypeStruct((1, D_A), F32),
                   jax.ShapeDtypeStruct((1, D_A), F32), jax.ShapeDtypeStruct((1, D_A), F32)],
        scratch_shapes=[pltpu.VMEM((ng, CHUNK, CHUNK), F32)],
        compiler_params=_cparams(("arbitrary",)),
    )(proj, dyab, ln_g, ln_b, w_s, b_exp, na_g)


OFF_Z = 2 * D_A
OFF_XS = OFF_Z + D_SSM
OFF_B = OFF_XS + D_SSM
OFF_C = OFF_B + D_BC


def _ssd_consts():
    tri = jnp.tril(jnp.ones((CHUNK, CHUNK), F32)).astype(BF16)
    h = jnp.arange(HPAD)[None, :, None]
    g = jnp.arange(N_SSM_GROUPS)[:, None, None]
    j1 = jnp.arange(GW)[None, None, :]
    eh = (h == g * HEADS_PER_GROUP + j1 // HEAD_DIM).astype(BF16)
    j2 = jnp.arange(HEADS_PER_GROUP * 128)[None, None, :]
    e128 = (h == g * HEADS_PER_GROUP + j2 // 128).astype(BF16)
    return tri, eh, e128


def _ssd_in_specs(cmap):
    def rows(i):
        return cmap(i)

    def prev8(i):
        return jnp.maximum(cmap(i) * (CHUNK // 8) - 1, 0)

    def whole(*shape):
        return pl.BlockSpec(shape, lambda i: (0,) * len(shape))

    bcw = 2 * D_BC
    specs = [
        pl.BlockSpec((CHUNK, D_SSM), lambda i: (rows(i), OFF_Z // D_SSM)),
        pl.BlockSpec((CHUNK, D_SSM), lambda i: (rows(i), OFF_XS // D_SSM)),
        pl.BlockSpec((8, D_SSM), lambda i: (prev8(i), OFF_XS // D_SSM)),
        pl.BlockSpec((CHUNK, bcw), lambda i: (rows(i), OFF_B // bcw)),
        pl.BlockSpec((8, bcw), lambda i: (prev8(i), OFF_B // bcw)),
        pl.BlockSpec((CHUNK, HPAD), lambda i: (rows(i), 0)),
        whole(SSM_CONV, D_XBC), whole(1, D_XBC),
        whole(1, HPAD), whole(1, HPAD),
        whole(1, D_SSM), whole(1, D_SSM),
        whole(CHUNK, CHUNK),
        whole(N_SSM_GROUPS, HPAD, GW), whole(N_SSM_GROUPS, HPAD, HEADS_PER_GROUP * 128),
    ]
    return specs


N_SSD_IN = 15


def _lanes(ref, start, width):
    return ref.at[:, pl.ds(pl.multiple_of(start, 128), width)]


def _ssd_group_refs(ins, g):
    (z_ref, xs_ref, xsp_ref, bc_ref, bcp_ref, dt_ref, cw_ref, cb_ref, dtb_ref, alog_ref, de_ref, gain_ref,
     tri_ref, eh_ref, e128_ref) = ins
    ox, ob, oc = g * GW, g * D_STATE, D_BC + g * D_STATE
    return (_lanes(z_ref, ox, GW), _lanes(xs_ref, ox, GW), _lanes(xsp_ref, ox, GW),
            _lanes(bc_ref, ob, D_STATE), _lanes(bcp_ref, ob, D_STATE),
            _lanes(bc_ref, oc, D_STATE), _lanes(bcp_ref, oc, D_STATE), dt_ref,
            _lanes(cw_ref, ox, GW), _lanes(cw_ref, D_SSM + ob, D_STATE), _lanes(cw_ref, D_SSM + oc, D_STATE),
            _lanes(cb_ref, ox, GW), _lanes(cb_ref, D_SSM + ob, D_STATE), _lanes(cb_ref, D_SSM + oc, D_STATE),
            dtb_ref, alog_ref, _lanes(de_ref, ox, GW), _lanes(gain_ref, ox, GW), tri_ref,
            eh_ref.at[g], e128_ref.at[g])


N_SSD_OWN = 4


def _ssd_scratch():
    return [pltpu.VMEM((CHUNK + 8, GW), F32), pltpu.VMEM((CHUNK + 8, D_STATE), F32),
            pltpu.VMEM((CHUNK + 8, D_STATE), F32), pltpu.VMEM((CHUNK, GW), F32),
            pltpu.VMEM((HPAD, CHUNK), F32), pltpu.VMEM((CHUNK, HPAD), F32), pltpu.VMEM((CHUNK, HPAD), F32)]


def _ssd_two_sets(alloc):
    a, b = alloc[:7], alloc[7:7 + N_SSD_OWN]
    return ((a[0], a[1], a[2], a[4], a[3], a[5], a[6]), (b[0], b[1], b[2], a[4], b[3], a[5], a[6]))


def _ssd_chunk(ins, scr):
    dt_ref, dtb_ref, alog_ref, tri_ref = ins[5], ins[8], ins[9], ins[12]
    acst_sc, dt_sc, acs_sc = scr[3], scr[5], scr[6]
    dt = _softplus(dt_ref[...] + dtb_ref[...])
    acs = _e01x(tri_ref[...], dt * (-jnp.exp(alog_ref[...])))
    dt_sc[...] = dt
    acs_sc[...] = acs
    acst_sc[...] = acs.T


def _ssd_pre(first, g, refs, scr):
    (z_ref, xs_ref, xsp_ref, b_ref, bp_ref, c_ref, cp_ref, dt_ref, cwx, cwb, cwc, cbx, cbb, cbc,
     dtb_ref, alog_ref, de_ref, gain_ref, tri_ref, eh_ref, e128_ref) = refs
    ext_x, ext_b, ext_c, acst_sc, acse_sc, dt_sc, acs_sc = scr
    p = {}
    px = _conv(xs_ref, xsp_ref, cwx, cbx, ext_x, first, SSM_CONV)
    pb = _conv(b_ref, bp_ref, cwb, cbb, ext_b, first, SSM_CONV)
    pc = _conv(c_ref, cp_ref, cwc, cbc, ext_c, first, SSM_CONV)
    p['xs'], p['dsx'] = _silu_and_grad(px)
    p['bm'], p['dsb'] = _silu_and_grad(pb)
    p['cm'], p['dsc'] = _silu_and_grad(pc)
    p['dt_in'] = dt_ref[...] + dtb_ref[...]
    dt = dt_sc[...]
    p['dt'] = dt
    p['a'] = -jnp.exp(alog_ref[...])
    tri_b = tri_ref[...]
    acs = acs_sc[...]
    eh = eh_ref[...]
    p['eh'] = eh
    p['dt_e'] = _x01(dt, eh)
    acs_e = _x01(acs, eh)
    acse_sc[...] = acs_e
    p['acs_e'] = acs_e
    p['acs_c'] = _x01(acs, e128_ref[...])
    p['acs_last_e'] = acse_sc[pl.ds(CHUNK - 1, 1), :]
    p['xdt'] = p['xs'] * p['dt_e']
    p['decay_e'] = jnp.exp(p['acs_last_e'] - acs_e)
    p['cm_b'] = p['cm'].astype(BF16)
    p['bm_b'] = p['bm'].astype(BF16)
    p['scores'] = _dot_nt(p['cm_b'], p['bm_b'])
    p['tri_b'] = tri_b
    return p


def _ssd_l(p, g, r, acst_sc, tri):
    col = p['acs_c'][:, r * 128:(r + 1) * 128]
    row = acst_sc[pl.ds(g * HEADS_PER_GROUP + r, 1), :]
    return jnp.where(tri, jnp.exp(jnp.minimum(col - row, 0.0)), 0.0)


def _ssd_fwd(proj, dt_raw, yab, cw, cb, dtb, alog, de, gain, consts, hosted):
    t = proj.shape[0]
    nc = t // CHUNK
    ng = N_SSM_GROUPS
    tri_c, eh_c, e128_c = consts
    n_in = N_SSD_IN
    nh_in, nh_out = len(hosted['arrays']), len(hosted['out_shape'])
    mid_chunk = (7 * nc) // 8

    def body(*refs):
        ins = refs[:n_in]
        hins = refs[n_in + 1:n_in + 1 + nh_in]
        o0 = n_in + 1 + nh_in
        yb_all, ys_all, hs_all = refs[o0:o0 + 3]
        houts = refs[o0 + 3:o0 + 3 + nh_out]
        h_sc = refs[o0 + 3 + nh_out]
        scr_a, scr_b = _ssd_two_sets(refs[o0 + 4 + nh_out:-2])
        sems = refs[-2:]
        c = pl.program_id(0)
        _host_phase(hosted, 'start', c == 0, hins, houts, sems)
        _host_phase(hosted, 'mid', c == mid_chunk, hins, houts, sems)

        @pl.when(c == 0)
        def _():
            h_sc[...] = jnp.zeros_like(h_sc)

        def group(g, scr):
            grefs = _ssd_group_refs(ins, g)
            z_ref, de_ref, gain_ref = grefs[0], grefs[16], grefs[17]
            yb_ref, ys_ref = _lanes(yb_all, g * GW, GW), _lanes(ys_all, g * GW, GW)
            slab = pl.ds(pl.multiple_of(g * D_STATE, D_STATE), D_STATE)
            p = _ssd_pre(c == 0, g, grefs, scr)
            tri = _tri_mask()
            h_in = h_sc[slab, :]
            hs_all[slab, :] = h_in
            yoff = _dot(p['cm_b'], h_in.astype(BF16)) * jnp.exp(p['acs_e'])
            states = _dot_tn(p['bm_b'], (p['xdt'] * p['decay_e']).astype(BF16))
            lane = lax.broadcasted_iota(jnp.int32, (CHUNK, 128), 1)
            slabs = []
            for r2 in range(HEADS_PER_GROUP // 2):
                xb = p['xdt'][:, r2 * 128:(r2 + 1) * 128].astype(BF16)
                ya = _dot((p['scores'] * _ssd_l(p, g, 2 * r2, scr[3], tri)).astype(BF16), xb)
                yb = _dot((p['scores'] * _ssd_l(p, g, 2 * r2 + 1, scr[3], tri)).astype(BF16), xb)
                slabs.append(jnp.where(lane < HEAD_DIM, ya, yb))
            y = jnp.concatenate(slabs, axis=1) + yoff + de_ref[...] * p['xs']
            ys_ref[...] = y
            h_sc[slab, :] = jnp.exp(p['acs_last_e']) * h_in + states
            zv = z_ref[...]
            yg = y * zv * _sigmoid(zv)
            r = lax.rsqrt(jnp.mean(yg * yg, axis=-1, keepdims=True) + EPS)
            yb_ref[...] = (yg * r * gain_ref[...]).astype(BF16)

        def pair(j, carry):
            group(2 * j, scr_a)
            group(2 * j + 1, scr_b)
            return carry

        _ssd_chunk(ins, scr_a)
        lax.fori_loop(0, ng // 2, pair, 0)
        _host_phase(hosted, 'finish', c == nc - 1, hins, houts, sems)

    h_in, h_out, h_shape, h_scratch, h_alias = _host_plumbing(hosted, n_in + 1, 3)
    in_specs = _ssd_in_specs(lambda i: i) + [pl.BlockSpec(memory_space=pl.ANY)] + h_in
    out_specs = [pl.BlockSpec((CHUNK, D_SSM), lambda i: (i, D_A // D_SSM)),
                 pl.BlockSpec((CHUNK, D_SSM), lambda i: (i, 0)),
                 pl.BlockSpec((None, ng * D_STATE, GW), lambda i: (i, 0, 0))] + h_out
    out_shape = [jax.ShapeDtypeStruct((t, D_MIX), BF16), jax.ShapeDtypeStruct((t, D_SSM), F32),
                 jax.ShapeDtypeStruct((nc, ng * D_STATE, GW), F32)] + h_shape
    return pl.pallas_call(
        body, name="ssd_fwd", grid=(nc,), in_specs=in_specs, out_specs=out_specs, out_shape=out_shape,
        scratch_shapes=[pltpu.VMEM((ng * D_STATE, GW), F32)] + _ssd_scratch() + _ssd_scratch()[:N_SSD_OWN]
        + h_scratch,
        input_output_aliases={n_in: 0, **h_alias},
        compiler_params=_cparams(("arbitrary",)),
    )(proj, proj, proj, proj, proj, dt_raw, cw, cb, dtb, alog, de, gain,
      tri_c, eh_c, e128_c, yab, *hosted['arrays'])


def _rows8(vals, width):
    rid = lax.broadcasted_iota(jnp.int32, (8, width), 0)
    out = jnp.zeros((8, width), F32)
    for k, v in enumerate(vals):
        if v is not None:
            out = out + jnp.where(rid == k, v, 0.0)
    return out


def _ssd_bwd(proj, dt_raw, dyab, ysave, hs, duv, cw, cb, dtb, alog, de, gain, consts, hosted):
    t = proj.shape[0]
    nc = t // CHUNK
    ng = N_SSM_GROUPS
    tri_c, eh_c, e128_c = consts
    n_in = N_SSD_IN
    nh_in, nh_out = len(hosted['arrays']), len(hosted['out_shape'])

    def per_group(g, c, ins, dy_ref, ys_ref, hs_ref, outs, scratch):
        dz_ref, dxs_ref, db_ref, dc_ref, ddt_ref, acc_x, acc_b, acc_c, acc_gain, acc_head = outs
        dh_sc, car_x, car_b, car_c, dext_x, dext_b, dext_c = scratch[:7]
        scr = scratch[7:]
        ext_x, ext_b, ext_c, acst_sc = scr[:4]
        z_ref, cwx, cwb, cwc, de_ref, gain_ref = ins[0], ins[8], ins[9], ins[10], ins[16], ins[17]
        slab = pl.ds(pl.multiple_of(g * D_STATE, D_STATE), D_STATE)
        p = _ssd_pre(c == 0, g, ins, scr)
        tri = _tri_mask()
        xs, dt_e, acs_e, xdt, decay_e = p['xs'], p['dt_e'], p['acs_e'], p['xdt'], p['decay_e']
        cm_b, bm_b, scores, eh = p['cm_b'], p['bm_b'], p['scores'], p['eh']
        h_in = hs_ref[...]
        h_in_b = h_in.astype(BF16)
        e_a = jnp.exp(acs_e)
        raw = _dot(cm_b, h_in_b)

        y = ys_ref[...]
        zv = z_ref[...]
        sz, dsz = _silu_and_grad(zv)
        yg = y * sz
        r = lax.rsqrt(jnp.mean(yg * yg, axis=-1, keepdims=True) + EPS)
        xh = yg * r
        dout = dy_ref[...].astype(F32)
        gain = gain_ref[...]
        dxh = dout * gain
        dyg = r * (dxh - xh * jnp.mean(dxh * xh, axis=-1, keepdims=True))
        dy = dyg * sz
        dz_ref[...] = (dyg * y * dsz).astype(BF16)
        acc_gain[g] += _rows8([jnp.sum(dout * xh, axis=0, keepdims=True)], GW)
        d_skip8 = _x01_nt(_rows8([None, None, jnp.sum(dy * xs, axis=0, keepdims=True)], GW), eh)
        dxs = de_ref[...] * dy

        q = dy * raw * e_a
        draw = (dy * e_a).astype(BF16)
        d_c = _dot_nt(draw, h_in_b)
        dh_in = _dot_tn(cm_b, draw)

        lane = lax.broadcasted_iota(jnp.int32, (CHUNK, 128), 1)
        ones_b = jnp.ones((CHUNK, 128), BF16)
        dscores = jnp.zeros((CHUNK, CHUNK), F32)
        dxdt_slabs, q_slabs = [], []
        for r2 in range(HEADS_PER_GROUP // 2):
            sl = slice(r2 * 128, (r2 + 1) * 128)
            xb = xdt[:, sl].astype(BF16)
            dys = dy[:, sl]
            dys_b = dys.astype(BF16)
            dxh_pair, qv_pair = [], []
            for half in range(2):
                lmat = _ssd_l(p, g, 2 * r2 + half, acst_sc, tri)
                m = scores * lmat
                mine = (lane < HEAD_DIM) if half == 0 else (lane >= HEAD_DIM)
                dm = _dot_nt(jnp.where(mine, dys, 0.0).astype(BF16), xb)
                dscores = dscores + dm * lmat
                gm = dm * m
                dxh_pair.append(_dot_tn(m.astype(BF16), dys_b))
                h3 = _split3(gm)
                colsum = _dot_tn(h3[0], ones_b) + _dot_tn(h3[1], ones_b) + _dot_tn(h3[2], ones_b)
                qv_pair.append(jnp.sum(gm, axis=-1, keepdims=True) - colsum)
            dxdt_slabs.append(jnp.where(lane < HEAD_DIM, dxh_pair[0], dxh_pair[1]))
            q_slabs.append(jnp.where(lane == 0, qv_pair[0], 0.0) + jnp.where(lane == HEAD_DIM, qv_pair[1], 0.0))
        dxdt = jnp.concatenate(dxdt_slabs, axis=1)
        q = q + jnp.concatenate(q_slabs, axis=1)

        dh_out = dh_sc[slab, :]
        dh_out_b = dh_out.astype(BF16)
        e_l = jnp.exp(p['acs_last_e'])
        dh_sc[slab, :] = dh_in + e_l * dh_out
        dlast = jnp.sum(dh_out * h_in, axis=0, keepdims=True) * e_l
        dxd = _dot(bm_b, dh_out_b)
        xd = xdt * decay_e
        dxdt = dxdt + dxd * decay_e
        dd = dxd * xd
        q = q - dd
        dlast = dlast + jnp.sum(dd, axis=0, keepdims=True)
        d_b = _dot_nt(xd.astype(BF16), dh_out_b)
        dsc_b = dscores.astype(BF16)
        d_c = d_c + _dot(dsc_b, bm_b)
        d_b = d_b + _dot_tn(dsc_b, cm_b)

        dxs = dxs + dxdt * dt_e
        rid = lax.broadcasted_iota(jnp.int32, (CHUNK, GW), 0)
        q = q + jnp.where(rid == CHUNK - 1, dlast, 0.0)
        dacs = _x01_nt(q, eh)
        ddt = _x01_nt(dxdt * xs, eh)
        dadt = _e01x_tn(p['tri_b'], dacs)
        ddt = ddt + dadt * p['a']
        d_a = jnp.sum(dadt * p['dt'], axis=0, keepdims=True)
        ddt_raw = ddt * _sigmoid(p['dt_in'])
        acc_head[...] += _rows8([jnp.sum(ddt_raw, axis=0, keepdims=True), d_a * p['a']], HPAD) + d_skip8

        @pl.when(g == 0)
        def _():
            ddt_ref[...] = ddt_raw

        @pl.when(g > 0)
        def _():
            ddt_ref[...] += ddt_raw

        for dv, dsil, ext, dext, car, acc, w_ref, o_ref in (
                (dxs, p['dsx'], ext_x, dext_x, car_x, acc_x, cwx, dxs_ref),
                (d_b, p['dsb'], ext_b, dext_b, car_b, acc_b, cwb, db_ref),
                (d_c, p['dsc'], ext_c, dext_c, car_c, acc_c, cwc, dc_ref)):
            dp = dv * dsil
            width = dp.shape[1]
            rows = [jnp.sum(ext[pl.ds(5 + k, CHUNK), :] * dp, axis=0, keepdims=True) for k in range(SSM_CONV)]
            rows.append(jnp.sum(dp, axis=0, keepdims=True))
            acc[g] += _rows8(rows, width)
            dext[0:CHUNK, :] = dp
            dext[CHUNK:CHUNK + 8, :] = car[g]
            car[g] = dext[0:8, :]
            dx = w_ref[SSM_CONV - 1:SSM_CONV, :] * dext[pl.ds(0, CHUNK), :]
            for k in range(SSM_CONV - 1):
                dx = dx + w_ref[k:k + 1, :] * dext[pl.ds(SSM_CONV - 1 - k, CHUNK), :]
            o_ref[...] = dx.astype(BF16)

    def body(*refs):
        ins = refs[:n_in]
        dy_all, ys_all, hs_all, duv_ref = refs[n_in:n_in + 4]
        hins = refs[n_in + 4:n_in + 4 + nh_in]
        o0 = n_in + 4 + nh_in
        dproj_ref, ddt_ref = refs[o0:o0 + 2]
        accs = refs[o0 + 2:o0 + 7]
        houts = refs[o0 + 7:o0 + 7 + nh_out]
        scratch = refs[o0 + 7 + nh_out:-2]
        sems = refs[-2:]
        cc = pl.program_id(0)
        _host_phase(hosted, 'start', cc == 0, hins, houts, sems)
        dproj_ref[:, 0:2 * D_A] = duv_ref[...]

        @pl.when(cc == 0)
        def _():
            for a in tuple(accs) + tuple(scratch[:4]):
                a[...] = jnp.zeros_like(a)

        shared = tuple(scratch[:4])
        scr_a, scr_b = _ssd_two_sets(tuple(scratch[7:14]) + tuple(scratch[17:17 + N_SSD_OWN]))
        set_a = shared + tuple(scratch[4:7]) + scr_a
        set_b = shared + tuple(scratch[14:17]) + scr_b

        def group(g, own):
            slab = pl.ds(pl.multiple_of(g * D_STATE, D_STATE), D_STATE)
            outs = (_lanes(dproj_ref, OFF_Z + g * GW, GW), _lanes(dproj_ref, OFF_XS + g * GW, GW),
                    _lanes(dproj_ref, OFF_B + g * D_STATE, D_STATE), _lanes(dproj_ref, OFF_C + g * D_STATE, D_STATE),
                    ddt_ref) + tuple(accs)
            per_group(g, nc - 1 - cc, _ssd_group_refs(ins, g), _lanes(dy_all, g * GW, GW),
                      _lanes(ys_all, g * GW, GW), hs_all.at[slab, :], outs, own)

        def pair(j, carry):
            group(2 * j, set_a)
            group(2 * j + 1, set_b)
            return carry

        _ssd_chunk(ins, scr_a)
        lax.fori_loop(0, ng // 2, pair, 0)
        _host_phase(hosted, 'finish', cc == nc - 1, hins, houts, sems)

    def cmap(i):
        return nc - 1 - i

    in_specs = _ssd_in_specs(cmap) + [
        pl.BlockSpec((CHUNK, D_SSM), lambda i: (cmap(i), D_A // D_SSM)),
        pl.BlockSpec((CHUNK, D_SSM), lambda i: (cmap(i), 0)),
        pl.BlockSpec((None, ng * D_STATE, GW), lambda i: (cmap(i), 0, 0)),
        pl.BlockSpec((CHUNK, 2 * D_A), lambda i: (cmap(i), 0)),
    ]

    def full(shape):
        return pl.BlockSpec(shape, lambda i: (0,) * len(shape))

    out_specs = [
        pl.BlockSpec((CHUNK, D_MAIN), lambda i: (cmap(i), 0)),
        pl.BlockSpec((CHUNK, HPAD), lambda i: (cmap(i), 0)),
        full((ng, 8, GW)), full((ng, 8, D_STATE)), full((ng, 8, D_STATE)),
        full((ng, 8, GW)), full((8, HPAD)),
    ]
    out_shape = [
        jax.ShapeDtypeStruct((t, D_MAIN), BF16),
        jax.ShapeDtypeStruct((t, HPAD), F32),
        jax.ShapeDtypeStruct((ng, 8, GW), F32), jax.ShapeDtypeStruct((ng, 8, D_STATE), F32),
        jax.ShapeDtypeStruct((ng, 8, D_STATE), F32), jax.ShapeDtypeStruct((ng, 8, GW), F32),
        jax.ShapeDtypeStruct((8, HPAD), F32),
    ]
    scratch = [pltpu.VMEM((ng * D_STATE, GW), F32),
               pltpu.VMEM((ng, 8, GW), F32), pltpu.VMEM((ng, 8, D_STATE), F32), pltpu.VMEM((ng, 8, D_STATE), F32),
               pltpu.VMEM((CHUNK + 8, GW), F32), pltpu.VMEM((CHUNK + 8, D_STATE), F32),
               pltpu.VMEM((CHUNK + 8, D_STATE), F32)] + _ssd_scratch()
    scratch += scratch[4:7] + _ssd_scratch()[:N_SSD_OWN]
    h_in, h_out, h_shape, h_scratch, h_alias = _host_plumbing(hosted, n_in + 4, len(out_shape))
    return pl.pallas_call(
        body, name="ssd_bwd", grid=(nc,), in_specs=in_specs + h_in, out_specs=out_specs + h_out,
        out_shape=out_shape + h_shape, scratch_shapes=scratch + h_scratch, input_output_aliases=h_alias,
        compiler_params=_cparams(("arbitrary",)),
    )(proj, proj, proj, proj, proj, dt_raw, cw, cb, dtb, alog, de, gain,
      tri_c, eh_c, e128_c, dyab, ysave, hs, duv, *hosted['arrays'])


ANY = pl.BlockSpec(memory_space=pl.ANY)


def _place():
    x, y, c = lax.axis_index("x"), lax.axis_index("y"), lax.axis_index("c")
    chips = [(1 - x, y), (x, 1 - y), (1 - x, 1 - y)]
    return x, y, c, chips


def _rcopy(src, dst, send_sems, recv_sems, k, dev):
    return pltpu.make_async_remote_copy(src_ref=src, dst_ref=dst, send_sem=send_sems.at[k],
                                        recv_sem=recv_sems.at[k], device_id=dev, device_id_type=MESH)


def _my_chip():
    return 2 * lax.axis_index("x") + lax.axis_index("y")


def _cast_into_slot(w, name):
    r, c = w.shape
    tr = _row_tile(r, c, 2)

    def body(w_ref, o_ref):
        o_ref[...] = w_ref[...].astype(BF16)

    return pl.pallas_call(
        body, name=name, grid=(r // tr,), in_specs=[pl.BlockSpec((tr, c), lambda i: (i, 0))],
        out_specs=pl.BlockSpec((None, tr, c), lambda i: (_my_chip(), i, 0)),
        out_shape=jax.ShapeDtypeStruct((N_CHIPS, r, c), BF16), compiler_params=_cparams(("parallel",)),
    )(w)


def _hosted_gather(bigs):
    nb = len(bigs)

    def rows(a, c):
        half = bigs[a].shape[1] // 2
        return pl.ds(c * half, half)

    def start(ins, outs, send_sems, recv_sems):
        x, y, c, chips = _place()
        q = 2 * x + y
        for a in range(nb):
            for k, chip in enumerate(chips):
                _rcopy(outs[a].at[q, rows(a, c)], outs[a].at[q, rows(a, c)], send_sems, recv_sems, 6 * a + k,
                       (chip[0], chip[1], c)).start()

    def mid(ins, outs, send_sems, recv_sems):
        x, y, c, chips = _place()
        sib = (x, y, 1 - c)
        for a in range(nb):
            for k, chip in enumerate(chips):
                slab = outs[a].at[2 * chip[0] + chip[1], rows(a, c)]
                _rcopy(slab, slab, send_sems, recv_sems, 6 * a + k, sib).wait_recv()
                _rcopy(slab, slab, send_sems, recv_sems, 6 * a + 3 + k, sib).start()

    def finish(ins, outs, send_sems, recv_sems):
        x, y, c, chips = _place()
        q = 2 * x + y
        sib = (x, y, 1 - c)
        for a in range(nb):
            for k, chip in enumerate(chips):
                qk = 2 * chip[0] + chip[1]
                other = outs[a].at[qk, rows(a, 1 - c)]
                _rcopy(other, other, send_sems, recv_sems, 6 * a + 3 + k, sib).wait_recv()
                mine = outs[a].at[q, rows(a, c)]
                _rcopy(mine, mine, send_sems, recv_sems, 6 * a + k, sib).wait_send()
                fwd = outs[a].at[qk, rows(a, c)]
                _rcopy(fwd, fwd, send_sems, recv_sems, 6 * a + 3 + k, sib).wait_send()

    return dict(arrays=list(bigs), out_shape=[jax.ShapeDtypeStruct(b.shape, b.dtype) for b in bigs],
                aliases={a: a for a in range(nb)}, nsem=6 * nb, start=start, mid=mid, finish=finish)


def _hosted_rs_chips(ps):
    na = len(ps)

    def copies(ins, outs, send_sems, recv_sems):
        x, y, c, chips = _place()
        return [_rcopy(ins[a].at[2 * chip[0] + chip[1]], outs[a].at[k], send_sems, recv_sems, 3 * a + k,
                       (chip[0], chip[1], c)) for a in range(na) for k, chip in enumerate(chips)]

    def start(ins, outs, send_sems, recv_sems):
        for cp in copies(ins, outs, send_sems, recv_sems):
            cp.start()

    def finish(ins, outs, send_sems, recv_sems):
        for cp in copies(ins, outs, send_sems, recv_sems):
            cp.wait()

    return dict(arrays=list(ps), out_shape=[jax.ShapeDtypeStruct((3,) + p.shape[1:], p.dtype) for p in ps],
                aliases={}, nsem=3 * na, start=start, mid=None, finish=finish)


def _hosted_rs_sibling(gs):
    na = len(gs)

    def copies(ins, outs, send_sems, recv_sems):
        x, y, c, _ = _place()
        halves = [g.shape[1] // 2 for g in gs]
        return [_rcopy(ins[a].at[:, pl.ds((1 - c) * halves[a], halves[a]), :], outs[a], send_sems, recv_sems, a,
                       (x, y, 1 - c)) for a in range(na)]

    def start(ins, outs, send_sems, recv_sems):
        for cp in copies(ins, outs, send_sems, recv_sems):
            cp.start()

    def finish(ins, outs, send_sems, recv_sems):
        for cp in copies(ins, outs, send_sems, recv_sems):
            cp.wait()

    return dict(arrays=list(gs), aliases={}, nsem=na, start=start, mid=None, finish=finish,
                out_shape=[jax.ShapeDtypeStruct((N_CHIPS, g.shape[1] // 2, g.shape[2]), g.dtype) for g in gs])


def _hosted_share_sibling(fs):
    na = len(fs)

    def rows(a, c):
        half = fs[a].shape[0] // 2
        return pl.ds(c * half, half)

    def start(ins, outs, send_sems, recv_sems):
        x, y, c, _ = _place()
        for a in range(na):
            _rcopy(outs[a].at[rows(a, c)], outs[a].at[rows(a, c)], send_sems, recv_sems, a, (x, y, 1 - c)).start()

    def finish(ins, outs, send_sems, recv_sems):
        x, y, c, _ = _place()
        for a in range(na):
            _rcopy(outs[a].at[rows(a, c)], outs[a].at[rows(a, c)], send_sems, recv_sems, a, (x, y, 1 - c)).wait_send()
            other = outs[a].at[rows(a, 1 - c)]
            _rcopy(other, other, send_sems, recv_sems, a, (x, y, 1 - c)).wait_recv()

    return dict(arrays=list(fs), out_shape=[jax.ShapeDtypeStruct(f.shape, f.dtype) for f in fs],
                aliases={a: a for a in range(na)}, nsem=na, start=start, mid=None, finish=finish)


def _hosted_allgather(buf):
    def copies(ins, outs, send_sems, recv_sems):
        x, y, c, _ = _place()
        me = 4 * x + 2 * y + c
        cps = []
        for k in range(1, 8):
            dev = (1 - x if k & 4 else x, 1 - y if k & 2 else y, 1 - c if k & 1 else c)
            cps.append(_rcopy(ins[0], outs[0].at[me], send_sems, recv_sems, k - 1, dev))
        return pltpu.make_async_copy(ins[0], outs[0].at[me], send_sems.at[7]), cps

    def start(ins, outs, send_sems, recv_sems):
        loc, cps = copies(ins, outs, send_sems, recv_sems)
        loc.start()
        for cp in cps:
            cp.start()

    def finish(ins, outs, send_sems, recv_sems):
        loc, cps = copies(ins, outs, send_sems, recv_sems)
        loc.wait()
        for cp in cps:
            cp.wait()

    return dict(arrays=[buf], out_shape=[jax.ShapeDtypeStruct((8,) + buf.shape, buf.dtype)], aliases={},
                nsem=8, start=start, mid=None, finish=finish)


class _SemWindow:
    def __init__(self, sems, off):
        self._sems, self._off = sems, off

    @property
    def at(self):
        return self

    def __getitem__(self, k):
        return self._sems.at[k + self._off]


def _hosted_join(parts):
    arrays, out_shape, aliases, spans, nsem = [], [], {}, [], 0
    for h in parts:
        spans.append((len(arrays), len(h['arrays']), len(out_shape), len(h['out_shape']), nsem))
        aliases.update({len(arrays) + a: len(out_shape) + b for a, b in h['aliases'].items()})
        arrays += h['arrays']
        out_shape += h['out_shape']
        nsem += h['nsem']

    def phase(name):
        if all(h[name] is None for h in parts):
            return None

        def run(ins, outs, send_sems, recv_sems):
            for h, (ia, na, io, no, s0) in zip(parts, spans):
                if h[name] is not None:
                    h[name](ins[ia:ia + na], outs[io:io + no], _SemWindow(send_sems, s0), _SemWindow(recv_sems, s0))

        return run

    return dict(arrays=arrays, out_shape=out_shape, aliases=aliases, nsem=nsem,
                start=phase('start'), mid=phase('mid'), finish=phase('finish'))


def _run_hosted(hosted, name):
    nh_in, nh_out = len(hosted['arrays']), len(hosted['out_shape'])

    def body(*refs):
        ins, outs, sems = refs[:nh_in], refs[nh_in:nh_in + nh_out], refs[-2:]
        for ph in ('start', 'mid', 'finish'):
            if hosted[ph] is not None:
                hosted[ph](ins, outs, sems[0], sems[1])

    h_in, h_out, h_shape, h_scratch, h_alias = _host_plumbing(hosted, 0, 0)
    return pl.pallas_call(body, name=name, in_specs=h_in, out_specs=h_out, out_shape=h_shape,
                          scratch_shapes=h_scratch, input_output_aliases=h_alias)(*hosted['arrays'])


def _host_plumbing(hosted, n_in, n_out):
    nh = len(hosted['arrays'])
    return ([ANY] * nh, [ANY] * len(hosted['out_shape']), list(hosted['out_shape']),
            [pltpu.SemaphoreType.DMA((hosted['nsem'],)), pltpu.SemaphoreType.DMA((hosted['nsem'],))],
            {n_in + a: n_out + b for a, b in hosted['aliases'].items()})


def _host_phase(hosted, phase, when, hins, houts, sems):
    fn = hosted[phase]
    if fn is None:
        return

    @pl.when(when)
    def _():
        fn(hins, houts, sems[0], sems[1])


def _gather_weights(bigs, smalls):
    nb, ns = len(bigs), len(smalls)
    na = nb + ns
    nsem = 6 * nb + 3 * ns
    big = _hosted_gather(bigs)

    def body(*refs):
        ins, outs = refs[:na], refs[na:2 * na]
        send_sems, recv_sems, loc_sems = refs[2 * na:]
        x, y, c, chips = _place()
        q = 2 * x + y
        sib = (x, y, 1 - c)
        locs, sends = [], []
        for s in range(ns):
            cp = pltpu.make_async_copy(ins[nb + s], outs[nb + s].at[q], loc_sems.at[s])
            cp.start()
            locs.append(cp)
        big['start'](ins[:nb], outs[:nb], send_sems, recv_sems)
        for s in range(ns):
            a = nb + s
            for k, chip in enumerate(chips):
                cp = _rcopy(ins[a], outs[a].at[q], send_sems, recv_sems, 6 * nb + 3 * s + k,
                            (chip[0], chip[1], c))
                cp.start()
                sends.append(cp)
        big['mid'](ins[:nb], outs[:nb], send_sems, recv_sems)
        big['finish'](ins[:nb], outs[:nb], send_sems, recv_sems)
        for s in range(ns):
            a = nb + s
            for k, chip in enumerate(chips):
                qk = 2 * chip[0] + chip[1]
                _rcopy(ins[a], outs[a].at[qk], send_sems, recv_sems, 6 * nb + 3 * s + k, sib).wait_recv()
        for cp in sends:
            cp.wait_send()
        for cp in locs:
            cp.wait()

    arrs = list(bigs) + list(smalls)
    out_shape = ([jax.ShapeDtypeStruct(a.shape, a.dtype) for a in bigs]
                 + [jax.ShapeDtypeStruct((N_CHIPS,) + a.shape, a.dtype) for a in smalls])
    return pl.pallas_call(
        body, name="gather_weights", in_specs=[ANY] * na, out_specs=[ANY] * na, out_shape=out_shape,
        input_output_aliases={a: a for a in range(nb)},
        scratch_shapes=[pltpu.SemaphoreType.DMA((nsem,)), pltpu.SemaphoreType.DMA((nsem,)),
                        pltpu.SemaphoreType.DMA((max(ns, 1),))],
    )(*arrs)


EW_VMEM_BUDGET = 24 * 1024 * 1024


def _row_tile(rows, cols, nbuf):
    budget = EW_VMEM_BUDGET // (nbuf * cols * 4 * 2)
    return _tile(rows, max(16, budget - budget % 16), 16) if rows % 16 == 0 else rows


def _add_pairs(g, rcv, name):
    s, half, c = rcv.shape
    tr = _row_tile(half, c, 3)
    nh = half // tr

    def body(a_ref, b_ref, o_ref):
        o_ref[...] = (a_ref[...].astype(F32) + b_ref[...].astype(F32)).astype(o_ref.dtype)

    blk = pl.BlockSpec((None, tr, c), lambda j, i: (j, i, 0))
    mine = pl.BlockSpec((None, tr, c), lambda j, i: (j, lax.axis_index("c") * nh + i, 0))
    return pl.pallas_call(
        body, name=name, grid=(s, nh), in_specs=[mine, blk], out_specs=blk,
        out_shape=jax.ShapeDtypeStruct(rcv.shape, rcv.dtype), compiler_params=_cparams(("parallel", "parallel")),
    )(g, rcv)


def _sum_chips(part, rcv, name):
    _, half, c = part.shape
    tr = _row_tile(half, c, 5)
    nh = half // tr

    def body(o_ref, r_ref, out_ref):
        acc = o_ref[...].astype(F32)
        for k in range(3):
            acc = acc + r_ref[k].astype(F32)
        out_ref[...] = acc

    return pl.pallas_call(
        body, name=name, grid=(nh,),
        in_specs=[pl.BlockSpec((None, tr, c), lambda i: (_my_chip(), i, 0)),
                  pl.BlockSpec((3, tr, c), lambda i: (0, i, 0))],
        out_specs=pl.BlockSpec((tr, c), lambda i: (lax.axis_index("c") * nh + i, 0)),
        out_shape=jax.ShapeDtypeStruct((2 * half, c), F32), compiler_params=_cparams(("parallel",)),
    )(part, rcv)


def _sum_devices(parts, name):
    _, n, _ = parts.shape
    tr = n if n <= 4096 else _tile(n, 512, 8)

    def body(p_ref, o_ref):
        acc = p_ref[0]
        for k in range(1, 8):
            acc = acc + p_ref[k]
        o_ref[...] = acc

    return pl.pallas_call(
        body, name=name, grid=(n // tr,),
        in_specs=[pl.BlockSpec((8, tr, 128), lambda i: (0, i, 0))],
        out_specs=pl.BlockSpec((tr, 128), lambda i: (i, 0)),
        out_shape=jax.ShapeDtypeStruct((n, 128), F32), compiler_params=_cparams(("parallel",)),
    )(parts)


def _adamw(w, g, m, v, name):
    r, c = w.shape
    tr = _row_tile(r, c, 7)
    c1 = 1.0 - ADAM_B1 ** ADAM_STEP
    c2 = 1.0 - ADAM_B2 ** ADAM_STEP

    def body(w_ref, g_ref, m_ref, v_ref, d_ref, mo_ref, vo_ref):
        gv = g_ref[...]
        mn = ADAM_B1 * m_ref[...] + (1.0 - ADAM_B1) * gv
        vn = ADAM_B2 * v_ref[...] + (1.0 - ADAM_B2) * (gv * gv)
        mo_ref[...] = mn
        vo_ref[...] = vn
        m_hat = mn / c1
        v_hat = vn / c2
        d_ref[...] = -ADAM_LR * (m_hat / (jnp.sqrt(v_hat) + ADAM_EPS) + ADAM_WD * w_ref[...])

    blk = pl.BlockSpec((tr, c), lambda i: (i, 0))
    sh = jax.ShapeDtypeStruct((r, c), F32)
    return pl.pallas_call(
        body, name=name, grid=(r // tr,), in_specs=[blk] * 4, out_specs=[blk] * 3, out_shape=[sh] * 3,
        compiler_params=_cparams(("parallel",)),
    )(w, g, m, v)


WEIGHTS = ['norm_mix_g', 'w_in', 'ln_a_g', 'ln_a_b', 'w_s', 'b_s', 'norm_a_g', 'conv_ssm_w', 'conv_ssm_b',
           'dt_bias', 'a_log', 'd_skip', 'ssm_norm_g', 'w_out', 'norm_ffn_g', 'w_up', 'conv_ffn_w',
           'conv_ffn_b', 'w_down', 'norm_ple_g', 'w_ple_gate', 'w_ple', 'norm_final_g']
BIG = ['w_in', 'w_out', 'w_up', 'w_down', 'w_ple_gate', 'w_ple']
SMALL = [n for n in WEIGHTS if n not in BIG]
PACK_ALIGN = 2048


def _pack(arrs):
    parts = []
    for a in arrs:
        f = a.reshape(-1).astype(F32)
        parts.append(jnp.pad(f, (0, (-f.shape[0]) % PACK_ALIGN)))
    return jnp.concatenate(parts).reshape(-1, 128)


def _unpack(buf, shapes):
    flat = buf.reshape(-1)
    out, off = [], 0
    for s in shapes:
        n = math.prod(s)
        out.append(flat[off:off + n].reshape(s))
        off += n + (-n) % PACK_ALIGN
    return out


def _pad_heads(v):
    return jnp.pad(v, ((0, 0), (0, HPAD - v.shape[1])))


def _col_sharded(full):
    r, c4 = full.shape
    return jnp.transpose(full.reshape(r, N_CHIPS, c4 // N_CHIPS), (1, 0, 2))


def _from_col_sharded(g):
    s, r, c = g.shape
    return jnp.transpose(g, (1, 0, 2)).reshape(r, s * c)


def kernel(x, p, norm_mix_g, w_in, ln_a_g, ln_a_b, w_s, b_s, norm_a_g, conv_ssm_w, conv_ssm_b, dt_bias, a_log, d_skip, ssm_norm_g, w_out, norm_ffn_g, w_up, conv_ffn_w, conv_ffn_b, w_down, norm_ple_g, w_ple_gate, w_ple, norm_final_g, loss_target, m_norm_mix_g, m_w_in, m_ln_a_g, m_ln_a_b, m_w_s, m_b_s, m_norm_a_g, m_conv_ssm_w, m_conv_ssm_b, m_dt_bias, m_a_log, m_d_skip, m_ssm_norm_g, m_w_out, m_norm_ffn_g, m_w_up, m_conv_ffn_w, m_conv_ffn_b, m_w_down, m_norm_ple_g, m_w_ple_gate, m_w_ple, m_norm_final_g, v_norm_mix_g, v_w_in, v_ln_a_g, v_ln_a_b, v_w_s, v_b_s, v_norm_a_g, v_conv_ssm_w, v_conv_ssm_b, v_dt_bias, v_a_log, v_d_skip, v_ssm_norm_g, v_w_out, v_norm_ffn_g, v_w_up, v_conv_ffn_w, v_conv_ffn_b, v_w_down, v_norm_ple_g, v_w_ple_gate, v_w_ple, v_norm_final_g):
    given = dict(locals())
    wts = {n: given[n] for n in WEIGHTS}
    mom = {n: given['m_' + n] for n in WEIGHTS}
    var = {n: given['v_' + n] for n in WEIGHTS}
    d = D_MODEL
    xt, pt, tgt = x[0], p[0, 0], loss_target[0]
    chip = 2 * lax.axis_index("x") + lax.axis_index("y")

    slots = {n: _cast_into_slot(wts[n][0], "cast_" + n) for n in BIG}
    g_in, g_cs, g_cf = _gather_weights([slots['w_in']], [conv_ssm_w[0], conv_ffn_w[0]])
    w_in_full = _from_col_sharded(g_in)
    w_main = w_in_full
    w_dt = _pad_heads(w_in_full[:, D_MAIN:])
    cs_w = _from_col_sharded(g_cs)
    cf_w = _from_col_sharded(g_cf)
    consts = _ssd_consts()
    dtb, alog = _pad_heads(dt_bias), _pad_heads(a_log)
    de = jnp.repeat(d_skip[0], HEAD_DIM)[None, :]
    b_exp = jnp.broadcast_to(b_s[0][:, :, None], (N_GROUPS_A, CHUNK, CHUNK))

    a1 = _rms_fwd(xt, norm_mix_g, "rms_mix")
    proj, g_out = _matmul(a1, w_main, mode='nn', name="mm_proj", tm=1024, tn=1024, b_cols=D_MAIN,
                          hosted=_hosted_gather([slots['w_out']]))
    dt_raw = _matmul(a1, w_dt, mode='nn', name="mm_dt", tm=1024, tn=128)
    yab = _gmlp_fwd(proj, ln_a_g, ln_a_b, w_s[0], b_exp, norm_a_g)
    yab, ysave, hs, g_up = _ssd_fwd(proj, dt_raw, yab, cs_w, conv_ssm_b, dtb, alog, de, ssm_norm_g, consts,
                                    _hosted_gather([slots['w_up']]))
    w_out_f = g_out.reshape(D_MIX, d)
    h1 = _matmul(yab, w_out_f, mode='nn', name="mm_out", res=xt, tm=512, tn=1024, tk=4096)
    f = _rms_fwd(h1, norm_ffn_g, "rms_ffn")
    hid, g_down, g_pg, g_ple = _matmul(
        f, g_up, mode='nn', name="mm_up", b_sharded=True, tm=1024, tn=1408,
        hosted=_hosted_gather([slots[n] for n in ('w_down', 'w_ple_gate', 'w_ple')]))
    w_down_f = g_down.reshape(D_FF, d)
    w_pg_f = g_pg.reshape(d, d)
    act, conv_g, conv_u = _ffn_act_fwd(hid, cf_w, conv_ffn_b)
    h2 = _matmul(act, w_down_f, mode='nn', name="mm_down", res=h1, tm=1024, tn=1024, tk=2816)
    n3 = _rms_fwd(h2, norm_ple_g, "rms_ple")
    gl = _matmul(n3, w_pg_f, mode='nn', name="mm_pg", tm=1024, tn=1024)
    pe = _matmul(pt, g_ple, mode='nn', name="mm_ple", b_sharded=True, tm=1024, tn=512)
    dh3, dgl, dpe, lossv, dgf = _tail(h2, gl, pe, tgt, norm_final_g[None, :])

    gs_ple = _matmul(pt, dpe, mode='tn', name="mm_dw_ple", out_dtype=BF16, out_shards=N_CHIPS,
                     tm=256, tn=512, tk=2048)
    gs_pg = _matmul(n3, dgl, mode='tn', name="mm_dw_pg", out_dtype=BF16, tm=1024, tn=1024, tk=4096)
    dn3 = _matmul(dgl, w_pg_f, mode='nt', name="mm_dn3", out_dtype=BF16, tm=1024, tn=1024)
    dh2, dg_ple, dh2_b = _rms_bwd(h2, norm_ple_g, dn3, dh3, "rms_ple_bwd", True)
    dact = _matmul(dh2_b, w_down_f, mode='nt', name="mm_dact", out_dtype=BF16, tm=1024, tn=1408)
    gs_down = _matmul(act, dh2_b, mode='tn', name="mm_dw_down", out_dtype=BF16, tm=1408, tn=1024, tk=2048)
    dpg, dpu, wg_acc, wu_acc = _ffn_act_bwd(hid, cf_w, conv_g, conv_u, dact)
    hc = N_CHIPS // 2
    gs_up = jnp.concatenate(
        [_matmul(f, dpg, mode='tn', name="mm_dw_up_g", out_dtype=BF16, out_shards=hc, tm=1024, tn=1408, tk=2048),
         _matmul(f, dpu, mode='tn', name="mm_dw_up_u", out_dtype=BF16, out_shards=hc, tm=1024, tn=1408, tk=2048)],
        axis=0)
    early = [gs_up, gs_down.reshape(N_CHIPS, D_FF // N_CHIPS, d), gs_pg.reshape(N_CHIPS, d // N_CHIPS, d), gs_ple]
    df, *sib_e = _matmul(dpg, g_up, mode='nt', name="mm_df_g", b_sharded=True, tm=1024, tn=1024, tk=2816,
                         hosted=_hosted_rs_sibling(early))
    part_e = [_add_pairs(a, b, "rs_add_e%d" % i) for i, (a, b) in enumerate(zip(early, sib_e))]
    df = _matmul(dpu, g_up, mode='nt', name="mm_df_u", b_sharded=True, b_shard_off=hc, res=df, out_dtype=BF16,
                 tm=1024, tn=1024, tk=2816)
    dh1, dg_ffn, dh1_b = _rms_bwd(h1, norm_ffn_g, df, dh2, "rms_ffn_bwd", True)
    dyab = _matmul(dh1_b, w_out_f, mode='nt', name="mm_dyab", out_dtype=BF16, tm=1024, tn=1024)
    gs_out = _matmul(yab, dh1_b, mode='tn', name="mm_dw_out", out_dtype=BF16, tm=1024, tn=1024, tk=4096)
    duv, dws, dbs, dlng, dlnb, dnag = _gmlp_bwd(proj, dyab, ln_a_g, ln_a_b, w_s[0], b_exp, norm_a_g)
    dproj, ddt_raw, acc_x, acc_b, acc_c, acc_gain, acc_head, *rcv_e = _ssd_bwd(
        proj, dt_raw, dyab, ysave, hs, duv, cs_w, conv_ssm_b, dtb, alog, de, ssm_norm_g, consts,
        _hosted_rs_chips(part_e))
    dw_main = _matmul(a1, dproj, mode='tn', name="mm_dw_main", out_dtype=BF16, tm=1024, tn=1024, tk=4096)
    dw_dt = _matmul(a1, ddt_raw, mode='tn', name="mm_dw_dt", out_dtype=BF16, tm=1024, tn=128, tk=2048)
    gs_in = _col_sharded(jnp.concatenate([dw_main, dw_dt[:, :N_HEADS]], axis=1))
    late = [gs_in, gs_out.reshape(N_CHIPS, D_MIX // N_CHIPS, d)]
    part_l = [_add_pairs(a, b, "rs_add_l%d" % i) for i, (a, b) in enumerate(
        zip(late, _run_hosted(_hosted_rs_sibling(late), "rs_sibling_late")))]

    def conv_rows(acc, k):
        return acc[:, k, :].reshape(1, -1)

    dcw = jnp.concatenate([jnp.concatenate([conv_rows(acc_x, k), conv_rows(acc_b, k), conv_rows(acc_c, k)], axis=1)
                           for k in range(SSM_CONV)], axis=0)
    dcb = jnp.concatenate([conv_rows(acc_x, SSM_CONV), conv_rows(acc_b, SSM_CONV), conv_rows(acc_c, SSM_CONV)], axis=1)
    part = {
        'ln_a_g': dlng, 'ln_a_b': dlnb, 'w_s': dws, 'b_s': dbs, 'norm_a_g': dnag,
        'conv_ssm_w': dcw, 'conv_ssm_b': dcb,
        'dt_bias': acc_head[0:1, :N_HEADS], 'a_log': acc_head[1:2, :N_HEADS], 'd_skip': acc_head[2:3, :N_HEADS],
        'ssm_norm_g': acc_gain[:, 0, :], 'norm_ffn_g': dg_ffn,
        'conv_ffn_w': jnp.concatenate([wg_acc[:FFN_CONV], wu_acc[:FFN_CONV]], axis=1),
        'conv_ffn_b': jnp.concatenate([wg_acc[FFN_CONV:FFN_CONV + 1], wu_acc[FFN_CONV:FFN_CONV + 1]], axis=1),
        'norm_ple_g': dg_ple, 'norm_final_g': dgf,
    }
    full_shapes = {n: wts[n].shape for n in SMALL}
    full_shapes['conv_ssm_w'] = (1, SSM_CONV, D_XBC)
    full_shapes['conv_ffn_w'] = (1, FFN_CONV, 2 * D_FF)
    small_e = [n for n in SMALL if n != 'norm_mix_g']
    packed = _pack([part[n] for n in small_e] + [lossv[:, 0:1]])

    halves_e = [_sum_chips(a, b, "rs_sum_e%d" % i) for i, (a, b) in enumerate(zip(part_e, rcv_e))]
    da_dt = _matmul(ddt_raw, w_dt, mode='nt', name="mm_da_dt", tm=1024, tn=1024)
    da, *moved = _matmul(dproj, w_main, mode='nt', name="mm_da", res=da_dt, out_dtype=BF16, tm=1024, tn=1024, tk=2560,
                         hosted=_hosted_join([_hosted_rs_chips(part_l), _hosted_share_sibling(halves_e),
                                              _hosted_allgather(packed)]))
    rcv_l, g_early, gathered = moved[:len(late)], moved[len(late):-1], moved[-1]
    dx, dg_mix = _rms_bwd(xt, norm_mix_g, da, dh1, "rms_mix_bwd", False)

    halves_l = [_sum_chips(a, b, "rs_sum_l%d" % i) for i, (a, b) in enumerate(zip(part_l, rcv_l))]
    g_big = dict(zip(['w_up', 'w_down', 'w_ple_gate', 'w_ple'], g_early))
    g_big.update(zip(['w_in', 'w_out'], _run_hosted(_hosted_share_sibling(halves_l), "share_sibling_late")))

    pieces = _unpack(_sum_devices(gathered, "sum_devices"), [full_shapes[n] for n in small_e] + [(1,)])
    g_small = dict(zip(small_e, pieces[:-1]))
    loss = pieces[-1][0]
    mix = _sum_devices(_run_hosted(_hosted_allgather(_pack([dg_mix])), "allgather_mix")[0], "sum_devices_mix")
    g_small['norm_mix_g'] = _unpack(mix, [full_shapes['norm_mix_g']])[0]
    for n in ('conv_ssm_w', 'conv_ffn_w'):
        width = wts[n].shape[2]
        g_small[n] = lax.dynamic_slice_in_dim(g_small[n], chip * width, width, axis=2)

    grads, delta, new_m, new_v = {}, {}, {}, {}
    for n in BIG:
        shp = wts[n].shape
        dl, mn, vn = _adamw(wts[n][0], g_big[n], mom[n][0], var[n][0], "adamw_" + n)
        grads[n], delta[n], new_m[n], new_v[n] = (g_big[n].reshape(shp), dl.reshape(shp), mn.reshape(shp),
                                                  vn.reshape(shp))
    shapes = [wts[n].shape for n in SMALL]
    dl, mn, vn = _adamw(_pack([wts[n] for n in SMALL]), _pack([g_small[n] for n in SMALL]),
                        _pack([mom[n] for n in SMALL]), _pack([var[n] for n in SMALL]), "adamw_small")
    for n, a, b, c in zip(SMALL, _unpack(dl, shapes), _unpack(mn, shapes), _unpack(vn, shapes)):
        grads[n], delta[n], new_m[n], new_v[n] = g_small[n], a, b, c

    return (loss, dx[None], *[grads[n] for n in WEIGHTS], *[delta[n] for n in WEIGHTS],
            *[new_m[n] for n in WEIGHTS], *[new_v[n] for n in WEIGHTS])
```

```python
import math

import jax
import jax.numpy as jnp
from jax import lax
from jax.experimental import pallas as pl
from jax.experimental.pallas import tpu as pltpu

D_MODEL = 2048
SEQ = 8192
D_MIX = 2 * D_MODEL
D_A = D_MIX // 2
CHUNK = 128
N_GROUPS_A = D_A // 128
D_SSM = D_MIX - D_A
HEAD_DIM = 64
N_HEADS = D_SSM // HEAD_DIM
HEADS_PER_GROUP = 4
N_SSM_GROUPS = N_HEADS // HEADS_PER_GROUP
GW = HEADS_PER_GROUP * HEAD_DIM
D_STATE = 128
SSM_CONV = 4
D_BC = N_SSM_GROUPS * D_STATE
D_XBC = D_SSM + 2 * D_BC
D_MAIN = 2 * D_A + D_SSM + D_XBC
D_IN = D_MAIN + N_HEADS
D_FF = (D_MODEL * 11) // 4
FFN_CONV = 3
D_PLE = 256
EPS = 1e-6
HPAD = 128
N_CHIPS = 4

ADAM_LR = 0.001
ADAM_B1 = 0.9
ADAM_B2 = 0.999
ADAM_EPS = 1e-08
ADAM_WD = 0.01
ADAM_STEP = 10

F32 = jnp.float32
BF16 = jnp.bfloat16
MESH = pl.DeviceIdType.MESH
VMEM_LIMIT = 56 * 1024 * 1024


def _cparams(sem):
    return pltpu.CompilerParams(dimension_semantics=sem, vmem_limit_bytes=VMEM_LIMIT)


def _forward_step(n_steps):
    return (7 * n_steps) // 8


def _tile(n, pref, mult):
    t = min(pref, n)
    t -= t % mult
    while n % t:
        t -= mult
    return t


def _dot(a, b):
    return jnp.dot(a, b, preferred_element_type=F32)


def _dot_nt(a, b):
    return lax.dot_general(a, b, (((1,), (1,)), ((), ())), preferred_element_type=F32)


def _dot_tn(a, b):
    return lax.dot_general(a, b, (((0,), (0,)), ((), ())), preferred_element_type=F32)


def _split3(x):
    hi = x.astype(BF16)
    r = x - hi.astype(F32)
    mid = r.astype(BF16)
    lo = (r - mid.astype(F32)).astype(BF16)
    return hi, mid, lo


def _x01(x, e):
    h, m, l = _split3(x)
    return _dot(h, e) + _dot(m, e) + _dot(l, e)


def _x01_nt(x, e):
    h, m, l = _split3(x)
    return _dot_nt(h, e) + _dot_nt(m, e) + _dot_nt(l, e)


def _e01x(e, x):
    h, m, l = _split3(x)
    return _dot(e, h) + _dot(e, m) + _dot(e, l)


def _e01x_tn(e, x):
    h, m, l = _split3(x)
    return _dot_tn(e, h) + _dot_tn(e, m) + _dot_tn(e, l)


def _sigmoid(x):
    return 1.0 / (1.0 + jnp.exp(-x))


_GELU_C = math.sqrt(2.0 / math.pi)


def _gelu_and_grad(x):
    x2 = x * x
    th = jnp.tanh(_GELU_C * (x + 0.044715 * x * x2))
    y = 0.5 * x * (1.0 + th)
    dy = 0.5 * (1.0 + th) + 0.5 * x * (1.0 - th * th) * (_GELU_C * (1.0 + 3.0 * 0.044715 * x2))
    return y, dy


def _silu_and_grad(x):
    s = _sigmoid(x)
    return x * s, s * (1.0 + x * (1.0 - s))


def _softplus(x):
    u = jnp.exp(-jnp.abs(x))
    w = 1.0 + u
    l1p = jnp.where(w == 1.0, u, jnp.log(w) * (u / (w - 1.0)))
    return jnp.maximum(x, 0.0) + l1p


def _matmul(a, b, *, mode, name, out_dtype=F32, res=None, tm=512, tn=512, tk=2048,
            b_sharded=False, b_shard_off=0, b_cols=None, out_shards=0, hosted=None):
    if mode == 'tn':
        kdim, m = a.shape
        n = b.shape[1]
    else:
        m, kdim = a.shape
        if b_sharded:
            s_b, d1, d2 = b.shape
            n = s_b * d2 if mode == 'nn' else d1
        else:
            n = b.shape[1] if mode == 'nn' else b.shape[0]
        if b_cols is not None:
            n = b_cols
    per = None
    if b_sharded:
        per = b.shape[2]
    if out_shards:
        per = n // out_shards
    tm = _tile(m, tm, 128 if mode == 'tn' else 8)
    if mode == 'nt' and b_sharded:
        tn = _tile(n, tn, 128)
        tk = _tile(per, tk, 128)
    elif per is not None:
        tn = _tile(per, tn, 128)
        tk = _tile(kdim, tk, 128 if mode != 'tn' else 8)
    else:
        tn = _tile(n, tn, 128)
        tk = _tile(kdim, tk, 128 if mode != 'tn' else 8)
    nm, nn_, nk = m // tm, n // tn, kdim // tk
    has_res = res is not None
    n_in = 2 + has_res
    nh_in = len(hosted['arrays']) if hosted else 0
    nh_out = len(hosted['out_shape']) if hosted else 0

    def body(*refs):
        a_ref, b_ref = refs[0], refs[1]
        res_ref = refs[2] if has_res else None
        o_ref = refs[n_in + nh_in]
        if hosted:
            hins = refs[n_in:n_in + nh_in]
            houts = refs[n_in + nh_in + 1:n_in + nh_in + 1 + nh_out]
            sems = refs[-2:]
            ids = [pl.program_id(d) for d in range(3)]
            step = (ids[0] * nm + ids[1]) * nk + ids[2]
            n_steps = nn_ * nm * nk
            at_first, at_mid, at_last = step == 0, step == _forward_step(n_steps), step == n_steps - 1
            _host_phase(hosted, 'start', at_first, hins, houts, sems)
        av = a_ref[...].astype(BF16)
        bv = b_ref[...].astype(BF16)
        if mode == 'nn':
            p = _dot(av, bv)
        elif mode == 'nt':
            p = _dot_nt(av, bv)
        else:
            p = _dot_tn(av, bv)

        def fin(v):
            if has_res:
                v = v + res_ref[...]
            o_ref[...] = v.astype(o_ref.dtype)

        if nk == 1:
            fin(p)
        else:
            acc_ref = refs[n_in + nh_in + 1 + nh_out]
            k = pl.program_id(2)

            @pl.when(k == 0)
            def _():
                acc_ref[...] = p

            @pl.when(k > 0)
            def _():
                acc_ref[...] += p

            @pl.when(k == nk - 1)
            def _():
                fin(acc_ref[...])
        if hosted:
            _host_phase(hosted, 'mid', at_mid, hins, houts, sems)
            _host_phase(hosted, 'finish', at_last, hins, houts, sems)

    if mode == 'nn':
        a_spec = pl.BlockSpec((tm, tk), lambda j, i, k: (i, k))
        if b_sharded:
            nps = per // tn
            b_spec = pl.BlockSpec((None, tk, tn), lambda j, i, k: (j // nps, k, j % nps))
        else:
            b_spec = pl.BlockSpec((tk, tn), lambda j, i, k: (k, j))
    elif mode == 'nt':
        a_spec = pl.BlockSpec((tm, tk), lambda j, i, k: (i, k))
        if b_sharded:
            kps = per // tk
            b_spec = pl.BlockSpec((None, tn, tk), lambda j, i, k: (k // kps + b_shard_off, j, k % kps))
        else:
            b_spec = pl.BlockSpec((tn, tk), lambda j, i, k: (j, k))
    else:
        a_spec = pl.BlockSpec((tk, tm), lambda j, i, k: (k, i))
        b_spec = pl.BlockSpec((tk, tn), lambda j, i, k: (k, j))
    in_specs = [a_spec, b_spec]
    args = [a, b]
    if has_res:
        in_specs.append(pl.BlockSpec((tm, tn), lambda j, i, k: (i, j)))
        args.append(res)
    if out_shards:
        nps_o = per // tn
        out_shape = jax.ShapeDtypeStruct((out_shards, m, per), out_dtype)
        out_spec = pl.BlockSpec((None, tm, tn), lambda j, i, k: (j // nps_o, i, j % nps_o))
    else:
        out_shape = jax.ShapeDtypeStruct((m, n), out_dtype)
        out_spec = pl.BlockSpec((tm, tn), lambda j, i, k: (i, j))
    scratch = [pltpu.VMEM((tm, tn), F32)] if nk > 1 else []
    if not hosted:
        return pl.pallas_call(
            body, name=name, grid=(nn_, nm, nk), in_specs=in_specs, out_specs=out_spec,
            out_shape=out_shape, scratch_shapes=scratch,
            compiler_params=_cparams(("parallel", "parallel", "arbitrary")),
        )(*args)
    h_in, h_out, h_shape, h_scratch, h_alias = _host_plumbing(hosted, n_in, 1)
    return pl.pallas_call(
        body, name=name, grid=(nn_, nm, nk), in_specs=in_specs + h_in, out_specs=[out_spec] + h_out,
        out_shape=[out_shape] + h_shape, scratch_shapes=scratch + h_scratch, input_output_aliases=h_alias,
        compiler_params=_cparams(("arbitrary", "arbitrary", "arbitrary")),
    )(*args, *hosted['arrays'])


def _rms_fwd(x, g, name):
    t, d = x.shape
    tt = _tile(t, 512, 8)

    def body(x_ref, g_ref, o_ref):
        xv = x_ref[...]
        r = lax.rsqrt(jnp.mean(xv * xv, axis=-1, keepdims=True) + EPS)
        o_ref[...] = (xv * r * g_ref[...]).astype(o_ref.dtype)

    return pl.pallas_call(
        body, name=name, grid=(t // tt,),
        in_specs=[pl.BlockSpec((tt, d), lambda i: (i, 0)), pl.BlockSpec((1, d), lambda i: (0, 0))],
        out_specs=pl.BlockSpec((tt, d), lambda i: (i, 0)),
        out_shape=jax.ShapeDtypeStruct((t, d), BF16),
        compiler_params=_cparams(("parallel",)),
    )(x, g)


def _rms_bwd(x, g, dy, dres, name, also_bf16):
    t, d = x.shape
    tt = _tile(t, 512, 16)

    def body(x_ref, g_ref, dy_ref, dres_ref, dx_ref, dg_ref, *dxb):
        i = pl.program_id(0)
        xv = x_ref[...]
        r = lax.rsqrt(jnp.mean(xv * xv, axis=-1, keepdims=True) + EPS)
        xh = xv * r
        dyv = dy_ref[...].astype(F32)
        dxh = dyv * g_ref[...]
        c = jnp.mean(dxh * xh, axis=-1, keepdims=True)
        dx = dres_ref[...] + r * (dxh - xh * c)
        dx_ref[...] = dx
        for dxb_ref in dxb:
            dxb_ref[...] = dx.astype(BF16)
        part = jnp.sum(dyv * xh, axis=0, keepdims=True)

        @pl.when(i == 0)
        def _():
            dg_ref[...] = part

        @pl.when(i > 0)
        def _():
            dg_ref[...] += part

    row = pl.BlockSpec((tt, d), lambda i: (i, 0))
    vec = pl.BlockSpec((1, d), lambda i: (0, 0))
    return pl.pallas_call(
        body, name=name, grid=(t // tt,),
        in_specs=[row, vec, row, row], out_specs=[row, vec] + [row] * also_bf16,
        out_shape=[jax.ShapeDtypeStruct((t, d), F32), jax.ShapeDtypeStruct((1, d), F32)]
        + [jax.ShapeDtypeStruct((t, d), BF16)] * also_bf16,
        compiler_params=_cparams(("arbitrary",)),
    )(x, g, dy, dres)


def _tail(h2, gl, pe, target, gfin):
    t, d = h2.shape
    tt = _tile(t, 256, 16)

    def body(h2_ref, gl_ref, pe_ref, tg_ref, gf_ref, dh3_ref, dgl_ref, dpe_ref, loss_ref, dgf_ref):
        i = pl.program_id(0)
        sig = _sigmoid(gl_ref[...])
        pev = pe_ref[...]
        h3 = h2_ref[...] + sig * pev
        r = lax.rsqrt(jnp.mean(h3 * h3, axis=-1, keepdims=True) + EPS)
        xh = h3 * r
        gf = gf_ref[...]
        e = xh * gf - tg_ref[...]
        dy = e * (1.0 / d)
        dxh = dy * gf
        c = jnp.mean(dxh * xh, axis=-1, keepdims=True)
        dh3 = r * (dxh - xh * c)
        dh3_ref[...] = dh3
        dgl_ref[...] = (dh3 * pev * sig * (1.0 - sig)).astype(BF16)
        dpe_ref[...] = (dh3 * sig).astype(BF16)
        lpart = jnp.sum(e * e, axis=0, keepdims=True) * (0.5 / d)
        gpart = jnp.sum(dy * xh, axis=0, keepdims=True)

        @pl.when(i == 0)
        def _():
            loss_ref[...] = lpart
            dgf_ref[...] = gpart

        @pl.when(i > 0)
        def _():
            loss_ref[...] += lpart
            dgf_ref[...] += gpart

        @pl.when(i == t // tt - 1)
        def _():
            loss_ref[...] = jnp.broadcast_to(jnp.sum(loss_ref[...], axis=-1, keepdims=True), (1, d))

    row = pl.BlockSpec((tt, d), lambda i: (i, 0))
    vec = pl.BlockSpec((1, d), lambda i: (0, 0))
    return pl.pallas_call(
        body, name="tail", grid=(t // tt,),
        in_specs=[row, row, row, row, vec], out_specs=[row, row, row, vec, vec],
        out_shape=[jax.ShapeDtypeStruct((t, d), F32), jax.ShapeDtypeStruct((t, d), BF16),
                   jax.ShapeDtypeStruct((t, d), BF16), jax.ShapeDtypeStruct((1, d), F32),
                   jax.ShapeDtypeStruct((1, d), F32)],
        compiler_params=_cparams(("arbitrary",)),
    )(h2, gl, pe, target, gfin)


def _conv(cur_ref, prev_ref, w_ref, b_ref, ext_ref, first, width):
    rows = cur_ref.shape[0]
    ext_ref[0:8, :] = jnp.where(first, 0.0, prev_ref[...])
    ext_ref[8:8 + rows, :] = cur_ref[...]
    acc = b_ref[...]
    for k in range(width):
        acc = acc + w_ref[k:k + 1, :] * ext_ref[pl.ds(9 - width + k, rows), :]
    return acc


def _ffn_specs(t, tt, tc, nf):
    hb = tt // 8
    cur_g = pl.BlockSpec((tt, tc), lambda j, i: (i, j))
    cur_u = pl.BlockSpec((tt, tc), lambda j, i: (i, j + nf))
    prev_g = pl.BlockSpec((8, tc), lambda j, i: (jnp.maximum(i * hb - 1, 0), j))
    prev_u = pl.BlockSpec((8, tc), lambda j, i: (jnp.maximum(i * hb - 1, 0), j + nf))
    w_g = pl.BlockSpec((FFN_CONV, tc), lambda j, i: (0, j))
    w_u = pl.BlockSpec((FFN_CONV, tc), lambda j, i: (0, j + nf))
    b_g = pl.BlockSpec((1, tc), lambda j, i: (0, j))
    b_u = pl.BlockSpec((1, tc), lambda j, i: (0, j + nf))
    return [cur_g, prev_g, cur_u, prev_u, w_g, w_u, b_g, b_u]


FFN_TC = 512


def _shift_down(prev, cur, n, rid):
    return jnp.where(rid < n, pltpu.roll(prev, n, 0), pltpu.roll(cur, n, 0))


def _shift_up(cur, nxt, n, rid):
    return jnp.where(rid < 8 - n, pltpu.roll(cur, 8 - n, 0), pltpu.roll(nxt, 8 - n, 0))


def _conv3_group(prev, cur, w_ref, b_ref, rid):
    x1 = _shift_down(prev, cur, 1, rid)
    x2 = _shift_down(prev, cur, 2, rid)
    return b_ref[...] + w_ref[2:3, :] * cur + w_ref[1:2, :] * x1 + w_ref[0:1, :] * x2


def _ffn_act_fwd(hid, cw, cb):
    t = hid.shape[0]
    tt = _tile(t, 2048, 16)
    tc = _tile(D_FF, FFN_TC, 128)
    nf = D_FF // tc

    def body(g_ref, gp_ref, u_ref, up_ref, wg_ref, wu_ref, bg_ref, bu_ref, o_ref, cg_ref, cu_ref):
        first = pl.program_id(1) == 0
        rid = lax.broadcasted_iota(jnp.int32, (8, tc), 0)

        def step(s, carry):
            pg, pu = carry
            r0 = pl.multiple_of(s * 16, 16)
            g0, g1 = g_ref[pl.ds(r0, 8), :], g_ref[pl.ds(r0 + 8, 8), :]
            u0, u1 = u_ref[pl.ds(r0, 8), :], u_ref[pl.ds(r0 + 8, 8), :]
            gate = jnp.concatenate([_conv3_group(pg, g0, wg_ref, bg_ref, rid),
                                    _conv3_group(g0, g1, wg_ref, bg_ref, rid)], axis=0)
            up = jnp.concatenate([_conv3_group(pu, u0, wu_ref, bu_ref, rid),
                                  _conv3_group(u0, u1, wu_ref, bu_ref, rid)], axis=0)
            cg_ref[pl.ds(r0, 16), :] = gate.astype(BF16)
            cu_ref[pl.ds(r0, 16), :] = up.astype(BF16)
            o_ref[pl.ds(r0, 16), :] = (gate * _sigmoid(gate) * up).astype(BF16)
            return g1, u1

        init = (jnp.where(first, 0.0, gp_ref[...]), jnp.where(first, 0.0, up_ref[...]))
        lax.fori_loop(0, tt // 16, step, init, unroll=2)

    blk = pl.BlockSpec((tt, tc), lambda j, i: (i, j))
    return pl.pallas_call(
        body, name="ffn_act_fwd", grid=(nf, t // tt), in_specs=_ffn_specs(t, tt, tc, nf),
        out_specs=[blk, blk, blk],
        out_shape=[jax.ShapeDtypeStruct((t, D_FF), BF16)] * 3,
        compiler_params=_cparams(("parallel", "arbitrary")),
    )(hid, hid, hid, hid, cw, cw, cb, cb)


def _ffn_act_bwd(hid, cw, conv_g, conv_u, dact):
    t = hid.shape[0]
    tt = _tile(t, 2048, 16)
    tc = _tile(D_FF, FFN_TC, 128)
    nf = D_FF // tc
    nt = t // tt
    n16 = tt // 16

    def body(g_ref, u_ref, wg_ref, wu_ref, cg_ref, cgn_ref, cu_ref, cun_ref, da_ref, dan_ref,
             og_ref, ou_ref, ag_ref, au_ref, accs):
        i = pl.program_id(1)
        first, last = i == 0, i == nt - 1
        rid = lax.broadcasted_iota(jnp.int32, (8, tc), 0)
        accs[...] = jnp.zeros_like(accs)

        def dgroup(gate, up, da):
            sv, sgr = _silu_and_grad(gate)
            return da * up * sgr, da * sv

        def finish(x, d0, d1, w_ref):
            s1 = _shift_up(d0, d1, 1, rid)
            s2 = _shift_up(d0, d1, 2, rid)
            dpre = w_ref[2:3, :] * d0 + w_ref[1:2, :] * s1 + w_ref[0:1, :] * s2
            return dpre, (x * s2, x * s1, x * d0, d0)

        def two_groups(it, carry, gate_blk, up_blk, da_blk, zero_ahead):
            d0g, d0u, gate1, up1, da1 = carry
            r0 = it * 16 if isinstance(it, int) else pl.multiple_of(it * 16, 16)
            x0g, x0u = g_ref[pl.ds(r0, 8), :], u_ref[pl.ds(r0, 8), :]
            x1g, x1u = g_ref[pl.ds(r0 + 8, 8), :], u_ref[pl.ds(r0 + 8, 8), :]
            d1g, d1u = dgroup(gate1, up1, da1)
            d2g, d2u = dgroup(gate_blk[0:8], up_blk[0:8], da_blk[0:8])
            d2g = jnp.where(zero_ahead, 0.0, d2g)
            d2u = jnp.where(zero_ahead, 0.0, d2u)
            for half, (xa, xb, da_, db_, dc_, w_ref, o_ref) in enumerate((
                    (x0g, x1g, d0g, d1g, d2g, wg_ref, og_ref), (x0u, x1u, d0u, d1u, d2u, wu_ref, ou_ref))):
                pa, prods_a = finish(xa, da_, db_, w_ref)
                pb, prods_b = finish(xb, db_, dc_, w_ref)
                o_ref[pl.ds(r0, 16), :] = jnp.concatenate([pa, pb], axis=0).astype(BF16)
                for k in range(4):
                    accs[4 * half + k] += prods_a[k] + prods_b[k]
            return d2g, d2u, gate_blk[8:16], up_blk[8:16], da_blk[8:16]

        def rows16(ref, r):
            return ref[pl.ds(r, 16), :].astype(F32)

        def step(it, carry):
            r1 = pl.multiple_of(it * 16 + 16, 16)
            return two_groups(it, carry, rows16(cg_ref, r1), rows16(cu_ref, r1), rows16(da_ref, r1), False)

        g0, u0, da0 = rows16(cg_ref, 0), rows16(cu_ref, 0), rows16(da_ref, 0)
        d0g, d0u = dgroup(g0[0:8], u0[0:8], da0[0:8])
        carry = lax.fori_loop(0, n16 - 1, step, (d0g, d0u, g0[8:16], u0[8:16], da0[8:16]))
        two_groups(n16 - 1, carry, cgn_ref[...].astype(F32), cun_ref[...].astype(F32), dan_ref[...].astype(F32), last)

        @pl.when(first)
        def _():
            ag_ref[...] = jnp.zeros_like(ag_ref)
            au_ref[...] = jnp.zeros_like(au_ref)

        for half, a_ref in enumerate((ag_ref, au_ref)):
            for k in range(4):
                a_ref[k:k + 1, :] += jnp.sum(accs[4 * half + k], axis=0, keepdims=True)

    def nxt16(i):
        return jnp.minimum((i + 1) * n16, t // 16 - 1)

    out_blk = pl.BlockSpec((tt, tc), lambda j, i: (i, j))
    acc_spec = pl.BlockSpec((8, tc), lambda j, i: (0, j))
    nxt_blk = pl.BlockSpec((16, tc), lambda j, i: (nxt16(i), j))
    in_specs = [out_blk, pl.BlockSpec((tt, tc), lambda j, i: (i, j + nf)),
                pl.BlockSpec((FFN_CONV, tc), lambda j, i: (0, j)), pl.BlockSpec((FFN_CONV, tc), lambda j, i: (0, j + nf)),
                out_blk, nxt_blk, out_blk, nxt_blk, out_blk, nxt_blk]
    return pl.pallas_call(
        body, name="ffn_act_bwd", grid=(nf, nt), in_specs=in_specs,
        out_specs=[out_blk, out_blk, acc_spec, acc_spec],
        out_shape=[jax.ShapeDtypeStruct((t, D_FF), BF16), jax.ShapeDtypeStruct((t, D_FF), BF16),
                   jax.ShapeDtypeStruct((8, D_FF), F32), jax.ShapeDtypeStruct((8, D_FF), F32)],
        scratch_shapes=[pltpu.VMEM((8, 8, tc), F32)],
        compiler_params=_cparams(("parallel", "arbitrary")),
    )(hid, hid, cw, cw, conv_g, conv_g, conv_u, conv_u, dact, dact)


def _tri_mask():
    r = lax.broadcasted_iota(jnp.int32, (CHUNK, CHUNK), 0)
    c = lax.broadcasted_iota(jnp.int32, (CHUNK, CHUNK), 1)
    return r >= c


def _gmlp_group_fwd(uv_ref, lng_ref, lnb_ref, ws_ref, bexp_ref, tri, g, want_grad):
    lo, hi = g * 128, (g + 1) * 128
    u_pre = uv_ref[:, lo:hi]
    v_pre = uv_ref[:, D_A + lo:D_A + hi]
    u, du = _gelu_and_grad(u_pre)
    v, dv = _gelu_and_grad(v_pre)
    mu = jnp.mean(v, axis=-1, keepdims=True)
    dc = v - mu
    rs = lax.rsqrt(jnp.mean(dc * dc, axis=-1, keepdims=True) + EPS)
    xh = dc * rs
    vn = (xh * lng_ref[:, lo:hi] + lnb_ref[:, lo:hi]).astype(BF16)
    w = jnp.where(tri, ws_ref[g], 0.0).astype(BF16)
    sg = _dot(w, vn) + bexp_ref[g]
    if want_grad:
        return u, du, dv, rs, xh, vn, w, sg
    return u * sg


def _gmlp_fwd(proj, ln_g, ln_b, w_s, b_exp, na_g):
    t = proj.shape[0]
    ng = N_GROUPS_A

    def body(uv_ref, lng_ref, lnb_ref, ws_ref, bexp_ref, nag_ref, o_ref):
        tri = _tri_mask()
        ys = [_gmlp_group_fwd(uv_ref, lng_ref, lnb_ref, ws_ref, bexp_ref, tri, g, False) for g in range(ng)]
        ssq = ys[0] * 0.0
        for y in ys:
            ssq = ssq + y * y
        r = lax.rsqrt(jnp.sum(ssq, axis=-1, keepdims=True) * (1.0 / D_A) + EPS)
        for g, y in enumerate(ys):
            o_ref[:, g * 128:(g + 1) * 128] = (y * r * nag_ref[:, g * 128:(g + 1) * 128]).astype(BF16)

    vec = pl.BlockSpec((1, D_A), lambda i: (0, 0))
    cube = pl.BlockSpec((ng, CHUNK, CHUNK), lambda i: (0, 0, 0))
    return pl.pallas_call(
        body, name="gmlp_fwd", grid=(t // CHUNK,),
        in_specs=[pl.BlockSpec((CHUNK, 2 * D_A), lambda i: (i, 0)), vec, vec, cube, cube, vec],
        out_specs=pl.BlockSpec((CHUNK, D_A), lambda i: (i, 0)),
        out_shape=jax.ShapeDtypeStruct((t, D_MIX), BF16),
        compiler_params=_cparams(("parallel",)),
    )(proj, ln_g, ln_b, w_s, b_exp, na_g)


def _gmlp_bwd(proj, dyab, ln_g, ln_b, w_s, b_exp, na_g):
    t = proj.shape[0]
    ng = N_GROUPS_A
    nsteps = t // CHUNK

    def body(uv_ref, dy_ref, lng_ref, lnb_ref, ws_ref, bexp_ref, nag_ref,
             duv_ref, dws_ref, dbs_ref, dlng_ref, dlnb_ref, dnag_ref, dbacc):
        i = pl.program_id(0)
        tri = _tri_mask()

        @pl.when(i == 0)
        def _():
            dws_ref[...] = jnp.zeros_like(dws_ref)
            dbacc[...] = jnp.zeros_like(dbacc)
            dlng_ref[...] = jnp.zeros_like(dlng_ref)
            dlnb_ref[...] = jnp.zeros_like(dlnb_ref)
            dnag_ref[...] = jnp.zeros_like(dnag_ref)

        st = [_gmlp_group_fwd(uv_ref, lng_ref, lnb_ref, ws_ref, bexp_ref, tri, g, True) for g in range(ng)]
        ssq = st[0][0] * 0.0
        for s in st:
            y = s[0] * s[7]
            ssq = ssq + y * y
        r = lax.rsqrt(jnp.sum(ssq, axis=-1, keepdims=True) * (1.0 / D_A) + EPS)
        csum = st[0][0] * 0.0
        for g, s in enumerate(st):
            sl = slice(g * 128, (g + 1) * 128)
            xhy = s[0] * s[7] * r
            dya = dy_ref[:, sl].astype(F32)
            dnag_ref[:, sl] += jnp.sum(dya * xhy, axis=0, keepdims=True)
            csum = csum + dya * nag_ref[:, sl] * xhy
        c1 = jnp.sum(csum, axis=-1, keepdims=True) * (1.0 / D_A)
        for g, s in enumerate(st):
            u, du, dv, rs, xh, vn, w, sg = s
            sl = slice(g * 128, (g + 1) * 128)
            dy = r * (dy_ref[:, sl].astype(F32) * nag_ref[:, sl] - u * sg * r * c1)
            dsg = dy * u
            dsg_b = dsg.astype(BF16)
            dws_ref[g] += _dot_nt(dsg_b, vn)
            dbacc[g] += dsg
            dvn = _dot_tn(w, dsg_b)
            dlnb_ref[:, sl] += jnp.sum(dvn, axis=0, keepdims=True)
            dlng_ref[:, sl] += jnp.sum(dvn * xh, axis=0, keepdims=True)
            dxh = dvn * lng_ref[:, sl]
            dvv = rs * (dxh - jnp.mean(dxh, axis=-1, keepdims=True)
                        - xh * jnp.mean(dxh * xh, axis=-1, keepdims=True))
            duv_ref[:, sl] = (dy * sg * du).astype(BF16)
            duv_ref[:, D_A + g * 128:D_A + (g + 1) * 128] = (dvv * dv).astype(BF16)

        @pl.when(i == nsteps - 1)
        def _():
            for g in range(ng):
                dws_ref[g] = jnp.where(tri, dws_ref[g], 0.0)
                dbs_ref[g] = jnp.sum(dbacc[g], axis=-1, keepdims=True)

    vec = pl.BlockSpec((1, D_A), lambda i: (0, 0))
    cube = pl.BlockSpec((ng, CHUNK, CHUNK), lambda i: (0, 0, 0))
    return pl.pallas_call(
        body, name="gmlp_bwd", grid=(nsteps,),
        in_specs=[pl.BlockSpec((CHUNK, 2 * D_A), lambda i: (i, 0)),
                  pl.BlockSpec((CHUNK, D_A), lambda i: (i, 0)), vec, vec, cube, cube, vec],
        out_specs=[pl.BlockSpec((CHUNK, 2 * D_A), lambda i: (i, 0)), cube,
                   pl.BlockSpec((ng, CHUNK, 1), lambda i: (0, 0, 0)), vec, vec, vec],
        out_shape=[jax.ShapeDtypeStruct((t, 2 * D_A), BF16), jax.ShapeDtypeStruct((ng, CHUNK, CHUNK), F32),
                   jax.ShapeDtypeStruct((ng, CHUNK, 1), F32), jax.ShapeDtypeStruct((1, D_A), F32),
                   jax.ShapeDtypeStruct((1, D_A), F32), jax.ShapeDtypeStruct((1, D_A), F32)],
        scratch_shapes=[pltpu.VMEM((ng, CHUNK, CHUNK), F32)],
        compiler_params=_cparams(("arbitrary",)),
    )(proj, dyab, ln_g, ln_b, w_s, b_exp, na_g)


OFF_Z = 2 * D_A
OFF_XS = OFF_Z + D_SSM
OFF_B = OFF_XS + D_SSM
OFF_C = OFF_B + D_BC


def _ssd_consts():
    tri = jnp.tril(jnp.ones((CHUNK, CHUNK), F32)).astype(BF16)
    h = jnp.arange(HPAD)[None, :, None]
    g = jnp.arange(N_SSM_GROUPS)[:, None, None]
    j1 = jnp.arange(GW)[None, None, :]
    eh = (h == g * HEADS_PER_GROUP + j1 // HEAD_DIM).astype(BF16)
    j2 = jnp.arange(HEADS_PER_GROUP * 128)[None, None, :]
    e128 = (h == g * HEADS_PER_GROUP + j2 // 128).astype(BF16)
    return tri, eh, e128


def _ssd_in_specs(cmap):
    def rows(i):
        return cmap(i)

    def prev8(i):
        return jnp.maximum(cmap(i) * (CHUNK // 8) - 1, 0)

    def whole(*shape):
        return pl.BlockSpec(shape, lambda i: (0,) * len(shape))

    bcw = 2 * D_BC
    specs = [
        pl.BlockSpec((CHUNK, D_SSM), lambda i: (rows(i), OFF_Z // D_SSM)),
        pl.BlockSpec((CHUNK, D_SSM), lambda i: (rows(i), OFF_XS // D_SSM)),
        pl.BlockSpec((8, D_SSM), lambda i: (prev8(i), OFF_XS // D_SSM)),
        pl.BlockSpec((CHUNK, bcw), lambda i: (rows(i), OFF_B // bcw)),
        pl.BlockSpec((8, bcw), lambda i: (prev8(i), OFF_B // bcw)),
        pl.BlockSpec((CHUNK, HPAD), lambda i: (rows(i), 0)),
        whole(SSM_CONV, D_XBC), whole(1, D_XBC),
        whole(1, HPAD), whole(1, HPAD),
        whole(1, D_SSM), whole(1, D_SSM),
        whole(CHUNK, CHUNK),
        whole(N_SSM_GROUPS, HPAD, GW), whole(N_SSM_GROUPS, HPAD, HEADS_PER_GROUP * 128),
    ]
    return specs


N_SSD_IN = 15


def _lanes(ref, start, width):
    return ref.at[:, pl.ds(pl.multiple_of(start, 128), width)]


def _ssd_group_refs(ins, g):
    (z_ref, xs_ref, xsp_ref, bc_ref, bcp_ref, dt_ref, cw_ref, cb_ref, dtb_ref, alog_ref, de_ref, gain_ref,
     tri_ref, eh_ref, e128_ref) = ins
    ox, ob, oc = g * GW, g * D_STATE, D_BC + g * D_STATE
    return (_lanes(z_ref, ox, GW), _lanes(xs_ref, ox, GW), _lanes(xsp_ref, ox, GW),
            _lanes(bc_ref, ob, D_STATE), _lanes(bcp_ref, ob, D_STATE),
            _lanes(bc_ref, oc, D_STATE), _lanes(bcp_ref, oc, D_STATE), dt_ref,
            _lanes(cw_ref, ox, GW), _lanes(cw_ref, D_SSM + ob, D_STATE), _lanes(cw_ref, D_SSM + oc, D_STATE),
            _lanes(cb_ref, ox, GW), _lanes(cb_ref, D_SSM + ob, D_STATE), _lanes(cb_ref, D_SSM + oc, D_STATE),
            dtb_ref, alog_ref, _lanes(de_ref, ox, GW), _lanes(gain_ref, ox, GW), tri_ref,
            eh_ref.at[g], e128_ref.at[g])


N_SSD_OWN = 4


def _ssd_scratch():
    return [pltpu.VMEM((CHUNK + 8, GW), F32), pltpu.VMEM((CHUNK + 8, D_STATE), F32),
            pltpu.VMEM((CHUNK + 8, D_STATE), F32), pltpu.VMEM((CHUNK, GW), F32),
            pltpu.VMEM((HPAD, CHUNK), F32), pltpu.VMEM((CHUNK, HPAD), F32), pltpu.VMEM((CHUNK, HPAD), F32)]


def _ssd_two_sets(alloc):
    a, b = alloc[:7], alloc[7:7 + N_SSD_OWN]
    return ((a[0], a[1], a[2], a[4], a[3], a[5], a[6]), (b[0], b[1], b[2], a[4], b[3], a[5], a[6]))


def _ssd_chunk(ins, scr):
    dt_ref, dtb_ref, alog_ref, tri_ref = ins[5], ins[8], ins[9], ins[12]
    acst_sc, dt_sc, acs_sc = scr[3], scr[5], scr[6]
    dt = _softplus(dt_ref[...] + dtb_ref[...])
    acs = _e01x(tri_ref[...], dt * (-jnp.exp(alog_ref[...])))
    dt_sc[...] = dt
    acs_sc[...] = acs
    acst_sc[...] = acs.T


def _ssd_pre(first, g, refs, scr):
    (z_ref, xs_ref, xsp_ref, b_ref, bp_ref, c_ref, cp_ref, dt_ref, cwx, cwb, cwc, cbx, cbb, cbc,
     dtb_ref, alog_ref, de_ref, gain_ref, tri_ref, eh_ref, e128_ref) = refs
    ext_x, ext_b, ext_c, acst_sc, acse_sc, dt_sc, acs_sc = scr
    p = {}
    px = _conv(xs_ref, xsp_ref, cwx, cbx, ext_x, first, SSM_CONV)
    pb = _conv(b_ref, bp_ref, cwb, cbb, ext_b, first, SSM_CONV)
    pc = _conv(c_ref, cp_ref, cwc, cbc, ext_c, first, SSM_CONV)
    p['xs'], p['dsx'] = _silu_and_grad(px)
    p['bm'], p['dsb'] = _silu_and_grad(pb)
    p['cm'], p['dsc'] = _silu_and_grad(pc)
    p['dt_in'] = dt_ref[...] + dtb_ref[...]
    dt = dt_sc[...]
    p['dt'] = dt
    p['a'] = -jnp.exp(alog_ref[...])
    tri_b = tri_ref[...]
    acs = acs_sc[...]
    eh = eh_ref[...]
    p['eh'] = eh
    p['dt_e'] = _x01(dt, eh)
    acs_e = _x01(acs, eh)
    acse_sc[...] = acs_e
    p['acs_e'] = acs_e
    p['acs_c'] = _x01(acs, e128_ref[...])
    p['acs_last_e'] = acse_sc[pl.ds(CHUNK - 1, 1), :]
    p['xdt'] = p['xs'] * p['dt_e']
    p['decay_e'] = jnp.exp(p['acs_last_e'] - acs_e)
    p['cm_b'] = p['cm'].astype(BF16)
    p['bm_b'] = p['bm'].astype(BF16)
    p['scores'] = _dot_nt(p['cm_b'], p['bm_b'])
    p['tri_b'] = tri_b
    return p


def _ssd_l(p, g, r, acst_sc, tri):
    col = p['acs_c'][:, r * 128:(r + 1) * 128]
    row = acst_sc[pl.ds(g * HEADS_PER_GROUP + r, 1), :]
    return jnp.where(tri, jnp.exp(jnp.minimum(col - row, 0.0)), 0.0)


def _ssd_fwd(proj, dt_raw, yab, cw, cb, dtb, alog, de, gain, consts, hosted):
    t = proj.shape[0]
    nc = t // CHUNK
    ng = N_SSM_GROUPS
    tri_c, eh_c, e128_c = consts
    n_in = N_SSD_IN
    nh_in, nh_out = len(hosted['arrays']), len(hosted['out_shape'])
    mid_chunk = _forward_step(nc)

    def body(*refs):
        ins = refs[:n_in]
        hins = refs[n_in + 1:n_in + 1 + nh_in]
        o0 = n_in + 1 + nh_in
        yb_all, ys_all, hs_all = refs[o0:o0 + 3]
        houts = refs[o0 + 3:o0 + 3 + nh_out]
        h_sc = refs[o0 + 3 + nh_out]
        scr_a, scr_b = _ssd_two_sets(refs[o0 + 4 + nh_out:-2])
        sems = refs[-2:]
        c = pl.program_id(0)
        _host_phase(hosted, 'start', c == 0, hins, houts, sems)
        _host_phase(hosted, 'mid', c == mid_chunk, hins, houts, sems)

        @pl.when(c == 0)
        def _():
            h_sc[...] = jnp.zeros_like(h_sc)

        def group(g, scr):
            grefs = _ssd_group_refs(ins, g)
            z_ref, de_ref, gain_ref = grefs[0], grefs[16], grefs[17]
            yb_ref, ys_ref = _lanes(yb_all, g * GW, GW), _lanes(ys_all, g * GW, GW)
            slab = pl.ds(pl.multiple_of(g * D_STATE, D_STATE), D_STATE)
            p = _ssd_pre(c == 0, g, grefs, scr)
            tri = _tri_mask()
            h_in = h_sc[slab, :]
            hs_all[slab, :] = h_in
            yoff = _dot(p['cm_b'], h_in.astype(BF16)) * jnp.exp(p['acs_e'])
            states = _dot_tn(p['bm_b'], (p['xdt'] * p['decay_e']).astype(BF16))
            lane = lax.broadcasted_iota(jnp.int32, (CHUNK, 128), 1)
            slabs = []
            for r2 in range(HEADS_PER_GROUP // 2):
                xb = p['xdt'][:, r2 * 128:(r2 + 1) * 128].astype(BF16)
                ya = _dot((p['scores'] * _ssd_l(p, g, 2 * r2, scr[3], tri)).astype(BF16), xb)
                yb = _dot((p['scores'] * _ssd_l(p, g, 2 * r2 + 1, scr[3], tri)).astype(BF16), xb)
                slabs.append(jnp.where(lane < HEAD_DIM, ya, yb))
            y = jnp.concatenate(slabs, axis=1) + yoff + de_ref[...] * p['xs']
            ys_ref[...] = y
            h_sc[slab, :] = jnp.exp(p['acs_last_e']) * h_in + states
            zv = z_ref[...]
            yg = y * zv * _sigmoid(zv)
            r = lax.rsqrt(jnp.mean(yg * yg, axis=-1, keepdims=True) + EPS)
            yb_ref[...] = (yg * r * gain_ref[...]).astype(BF16)

        def pair(j, carry):
            group(2 * j, scr_a)
            group(2 * j + 1, scr_b)
            return carry

        _ssd_chunk(ins, scr_a)
        lax.fori_loop(0, ng // 2, pair, 0)
        _host_phase(hosted, 'finish', c == nc - 1, hins, houts, sems)

    h_in, h_out, h_shape, h_scratch, h_alias = _host_plumbing(hosted, n_in + 1, 3)
    in_specs = _ssd_in_specs(lambda i: i) + [pl.BlockSpec(memory_space=pl.ANY)] + h_in
    out_specs = [pl.BlockSpec((CHUNK, D_SSM), lambda i: (i, D_A // D_SSM)),
                 pl.BlockSpec((CHUNK, D_SSM), lambda i: (i, 0)),
                 pl.BlockSpec((None, ng * D_STATE, GW), lambda i: (i, 0, 0))] + h_out
    out_shape = [jax.ShapeDtypeStruct((t, D_MIX), BF16), jax.ShapeDtypeStruct((t, D_SSM), F32),
                 jax.ShapeDtypeStruct((nc, ng * D_STATE, GW), F32)] + h_shape
    return pl.pallas_call(
        body, name="ssd_fwd", grid=(nc,), in_specs=in_specs, out_specs=out_specs, out_shape=out_shape,
        scratch_shapes=[pltpu.VMEM((ng * D_STATE, GW), F32)] + _ssd_scratch() + _ssd_scratch()[:N_SSD_OWN]
        + h_scratch,
        input_output_aliases={n_in: 0, **h_alias},
        compiler_params=_cparams(("arbitrary",)),
    )(proj, proj, proj, proj, proj, dt_raw, cw, cb, dtb, alog, de, gain,
      tri_c, eh_c, e128_c, yab, *hosted['arrays'])


def _rows8(vals, width):
    rid = lax.broadcasted_iota(jnp.int32, (8, width), 0)
    out = jnp.zeros((8, width), F32)
    for k, v in enumerate(vals):
        if v is not None:
            out = out + jnp.where(rid == k, v, 0.0)
    return out


def _ssd_bwd(proj, dt_raw, dyab, ysave, hs, duv, cw, cb, dtb, alog, de, gain, consts, hosted):
    t = proj.shape[0]
    nc = t // CHUNK
    ng = N_SSM_GROUPS
    tri_c, eh_c, e128_c = consts
    n_in = N_SSD_IN
    nh_in, nh_out = len(hosted['arrays']), len(hosted['out_shape'])

    def per_group(g, c, ins, dy_ref, ys_ref, hs_ref, outs, scratch):
        dz_ref, dxs_ref, db_ref, dc_ref, ddt_ref, acc_x, acc_b, acc_c, acc_gain, acc_head = outs
        dh_sc, car_x, car_b, car_c, dext_x, dext_b, dext_c = scratch[:7]
        scr = scratch[7:]
        ext_x, ext_b, ext_c, acst_sc = scr[:4]
        z_ref, cwx, cwb, cwc, de_ref, gain_ref = ins[0], ins[8], ins[9], ins[10], ins[16], ins[17]
        slab = pl.ds(pl.multiple_of(g * D_STATE, D_STATE), D_STATE)
        p = _ssd_pre(c == 0, g, ins, scr)
        tri = _tri_mask()
        xs, dt_e, acs_e, xdt, decay_e = p['xs'], p['dt_e'], p['acs_e'], p['xdt'], p['decay_e']
        cm_b, bm_b, scores, eh = p['cm_b'], p['bm_b'], p['scores'], p['eh']
        h_in = hs_ref[...]
        h_in_b = h_in.astype(BF16)
        e_a = jnp.exp(acs_e)
        raw = _dot(cm_b, h_in_b)

        y = ys_ref[...]
        zv = z_ref[...]
        sz, dsz = _silu_and_grad(zv)
        yg = y * sz
        r = lax.rsqrt(jnp.mean(yg * yg, axis=-1, keepdims=True) + EPS)
        xh = yg * r
        dout = dy_ref[...].astype(F32)
        gain = gain_ref[...]
        dxh = dout * gain
        dyg = r * (dxh - xh * jnp.mean(dxh * xh, axis=-1, keepdims=True))
        dy = dyg * sz
        dz_ref[...] = (dyg * y * dsz).astype(BF16)
        acc_gain[g] += _rows8([jnp.sum(dout * xh, axis=0, keepdims=True)], GW)
        d_skip8 = _x01_nt(_rows8([None, None, jnp.sum(dy * xs, axis=0, keepdims=True)], GW), eh)
        dxs = de_ref[...] * dy

        q = dy * raw * e_a
        draw = (dy * e_a).astype(BF16)
        d_c = _dot_nt(draw, h_in_b)
        dh_in = _dot_tn(cm_b, draw)

        lane = lax.broadcasted_iota(jnp.int32, (CHUNK, 128), 1)
        ones_b = jnp.ones((CHUNK, 128), BF16)
        dscores = jnp.zeros((CHUNK, CHUNK), F32)
        dxdt_slabs, q_slabs = [], []
        for r2 in range(HEADS_PER_GROUP // 2):
            sl = slice(r2 * 128, (r2 + 1) * 128)
            xb = xdt[:, sl].astype(BF16)
            dys = dy[:, sl]
            dys_b = dys.astype(BF16)
            dxh_pair, qv_pair = [], []
            for half in range(2):
                lmat = _ssd_l(p, g, 2 * r2 + half, acst_sc, tri)
                m = scores * lmat
                mine = (lane < HEAD_DIM) if half == 0 else (lane >= HEAD_DIM)
                dm = _dot_nt(jnp.where(mine, dys, 0.0).astype(BF16), xb)
                dscores = dscores + dm * lmat
                gm = dm * m
                dxh_pair.append(_dot_tn(m.astype(BF16), dys_b))
                h3 = _split3(gm)
                colsum = _dot_tn(h3[0], ones_b) + _dot_tn(h3[1], ones_b) + _dot_tn(h3[2], ones_b)
                qv_pair.append(jnp.sum(gm, axis=-1, keepdims=True) - colsum)
            dxdt_slabs.append(jnp.where(lane < HEAD_DIM, dxh_pair[0], dxh_pair[1]))
            q_slabs.append(jnp.where(lane == 0, qv_pair[0], 0.0) + jnp.where(lane == HEAD_DIM, qv_pair[1], 0.0))
        dxdt = jnp.concatenate(dxdt_slabs, axis=1)
        q = q + jnp.concatenate(q_slabs, axis=1)

        dh_out = dh_sc[slab, :]
        dh_out_b = dh_out.astype(BF16)
        e_l = jnp.exp(p['acs_last_e'])
        dh_sc[slab, :] = dh_in + e_l * dh_out
        dlast = jnp.sum(dh_out * h_in, axis=0, keepdims=True) * e_l
        dxd = _dot(bm_b, dh_out_b)
        xd = xdt * decay_e
        dxdt = dxdt + dxd * decay_e
        dd = dxd * xd
        q = q - dd
        dlast = dlast + jnp.sum(dd, axis=0, keepdims=True)
        d_b = _dot_nt(xd.astype(BF16), dh_out_b)
        dsc_b = dscores.astype(BF16)
        d_c = d_c + _dot(dsc_b, bm_b)
        d_b = d_b + _dot_tn(dsc_b, cm_b)

        dxs = dxs + dxdt * dt_e
        rid = lax.broadcasted_iota(jnp.int32, (CHUNK, GW), 0)
        q = q + jnp.where(rid == CHUNK - 1, dlast, 0.0)
        dacs = _x01_nt(q, eh)
        ddt = _x01_nt(dxdt * xs, eh)
        dadt = _e01x_tn(p['tri_b'], dacs)
        ddt = ddt + dadt * p['a']
        d_a = jnp.sum(dadt * p['dt'], axis=0, keepdims=True)
        ddt_raw = ddt * _sigmoid(p['dt_in'])
        acc_head[...] += _rows8([jnp.sum(ddt_raw, axis=0, keepdims=True), d_a * p['a']], HPAD) + d_skip8

        @pl.when(g == 0)
        def _():
            ddt_ref[...] = ddt_raw

        @pl.when(g > 0)
        def _():
            ddt_ref[...] += ddt_raw

        for dv, dsil, ext, dext, car, acc, w_ref, o_ref in (
                (dxs, p['dsx'], ext_x, dext_x, car_x, acc_x, cwx, dxs_ref),
                (d_b, p['dsb'], ext_b, dext_b, car_b, acc_b, cwb, db_ref),
                (d_c, p['dsc'], ext_c, dext_c, car_c, acc_c, cwc, dc_ref)):
            dp = dv * dsil
            width = dp.shape[1]
            rows = [jnp.sum(ext[pl.ds(5 + k, CHUNK), :] * dp, axis=0, keepdims=True) for k in range(SSM_CONV)]
            rows.append(jnp.sum(dp, axis=0, keepdims=True))
            acc[g] += _rows8(rows, width)
            dext[0:CHUNK, :] = dp
            dext[CHUNK:CHUNK + 8, :] = car[g]
            car[g] = dext[0:8, :]
            dx = w_ref[SSM_CONV - 1:SSM_CONV, :] * dext[pl.ds(0, CHUNK), :]
            for k in range(SSM_CONV - 1):
                dx = dx + w_ref[k:k + 1, :] * dext[pl.ds(SSM_CONV - 1 - k, CHUNK), :]
            o_ref[...] = dx.astype(BF16)

    def body(*refs):
        ins = refs[:n_in]
        dy_all, ys_all, hs_all, duv_ref = refs[n_in:n_in + 4]
        hins = refs[n_in + 4:n_in + 4 + nh_in]
        o0 = n_in + 4 + nh_in
        dproj_ref, ddt_ref = refs[o0:o0 + 2]
        accs = refs[o0 + 2:o0 + 7]
        houts = refs[o0 + 7:o0 + 7 + nh_out]
        scratch = refs[o0 + 7 + nh_out:-2]
        sems = refs[-2:]
        cc = pl.program_id(0)
        _host_phase(hosted, 'start', cc == 0, hins, houts, sems)
        dproj_ref[:, 0:2 * D_A] = duv_ref[...]

        @pl.when(cc == 0)
        def _():
            for a in tuple(accs) + tuple(scratch[:4]):
                a[...] = jnp.zeros_like(a)

        shared = tuple(scratch[:4])
        scr_a, scr_b = _ssd_two_sets(tuple(scratch[7:14]) + tuple(scratch[17:17 + N_SSD_OWN]))
        set_a = shared + tuple(scratch[4:7]) + scr_a
        set_b = shared + tuple(scratch[14:17]) + scr_b

        def group(g, own):
            slab = pl.ds(pl.multiple_of(g * D_STATE, D_STATE), D_STATE)
            outs = (_lanes(dproj_ref, OFF_Z + g * GW, GW), _lanes(dproj_ref, OFF_XS + g * GW, GW),
                    _lanes(dproj_ref, OFF_B + g * D_STATE, D_STATE), _lanes(dproj_ref, OFF_C + g * D_STATE, D_STATE),
                    ddt_ref) + tuple(accs)
            per_group(g, nc - 1 - cc, _ssd_group_refs(ins, g), _lanes(dy_all, g * GW, GW),
                      _lanes(ys_all, g * GW, GW), hs_all.at[slab, :], outs, own)

        def pair(j, carry):
            group(2 * j, set_a)
            group(2 * j + 1, set_b)
            return carry

        _ssd_chunk(ins, scr_a)
        lax.fori_loop(0, ng // 2, pair, 0)
        _host_phase(hosted, 'finish', cc == nc - 1, hins, houts, sems)

    def cmap(i):
        return nc - 1 - i

    in_specs = _ssd_in_specs(cmap) + [
        pl.BlockSpec((CHUNK, D_SSM), lambda i: (cmap(i), D_A // D_SSM)),
        pl.BlockSpec((CHUNK, D_SSM), lambda i: (cmap(i), 0)),
        pl.BlockSpec((None, ng * D_STATE, GW), lambda i: (cmap(i), 0, 0)),
        pl.BlockSpec((CHUNK, 2 * D_A), lambda i: (cmap(i), 0)),
    ]

    def full(shape):
        return pl.BlockSpec(shape, lambda i: (0,) * len(shape))

    out_specs = [
        pl.BlockSpec((CHUNK, D_MAIN), lambda i: (cmap(i), 0)),
        pl.BlockSpec((CHUNK, HPAD), lambda i: (cmap(i), 0)),
        full((ng, 8, GW)), full((ng, 8, D_STATE)), full((ng, 8, D_STATE)),
        full((ng, 8, GW)), full((8, HPAD)),
    ]
    out_shape = [
        jax.ShapeDtypeStruct((t, D_MAIN), BF16),
        jax.ShapeDtypeStruct((t, HPAD), F32),
        jax.ShapeDtypeStruct((ng, 8, GW), F32), jax.ShapeDtypeStruct((ng, 8, D_STATE), F32),
        jax.ShapeDtypeStruct((ng, 8, D_STATE), F32), jax.ShapeDtypeStruct((ng, 8, GW), F32),
        jax.ShapeDtypeStruct((8, HPAD), F32),
    ]
    scratch = [pltpu.VMEM((ng * D_STATE, GW), F32),
               pltpu.VMEM((ng, 8, GW), F32), pltpu.VMEM((ng, 8, D_STATE), F32), pltpu.VMEM((ng, 8, D_STATE), F32),
               pltpu.VMEM((CHUNK + 8, GW), F32), pltpu.VMEM((CHUNK + 8, D_STATE), F32),
               pltpu.VMEM((CHUNK + 8, D_STATE), F32)] + _ssd_scratch()
    scratch += scratch[4:7] + _ssd_scratch()[:N_SSD_OWN]
    h_in, h_out, h_shape, h_scratch, h_alias = _host_plumbing(hosted, n_in + 4, len(out_shape))
    return pl.pallas_call(
        body, name="ssd_bwd", grid=(nc,), in_specs=in_specs + h_in, out_specs=out_specs + h_out,
        out_shape=out_shape + h_shape, scratch_shapes=scratch + h_scratch, input_output_aliases=h_alias,
        compiler_params=_cparams(("arbitrary",)),
    )(proj, proj, proj, proj, proj, dt_raw, cw, cb, dtb, alog, de, gain,
      tri_c, eh_c, e128_c, dyab, ysave, hs, duv, *hosted['arrays'])


ANY = pl.BlockSpec(memory_space=pl.ANY)


def _place():
    x, y, c = lax.axis_index("x"), lax.axis_index("y"), lax.axis_index("c")
    chips = [(1 - x, y), (x, 1 - y), (1 - x, 1 - y)]
    return x, y, c, chips


def _rcopy(src, dst, send_sems, recv_sems, k, dev):
    return pltpu.make_async_remote_copy(src_ref=src, dst_ref=dst, send_sem=send_sems.at[k],
                                        recv_sem=recv_sems.at[k], device_id=dev, device_id_type=MESH)


def _my_chip():
    return 2 * lax.axis_index("x") + lax.axis_index("y")


def _cast_into_slot(w, name):
    r, c = w.shape
    tr = _row_tile(r, c, 2)

    def body(w_ref, o_ref):
        o_ref[...] = w_ref[...].astype(BF16)

    return pl.pallas_call(
        body, name=name, grid=(r // tr,), in_specs=[pl.BlockSpec((tr, c), lambda i: (i, 0))],
        out_specs=pl.BlockSpec((None, tr, c), lambda i: (_my_chip(), i, 0)),
        out_shape=jax.ShapeDtypeStruct((N_CHIPS, r, c), BF16), compiler_params=_cparams(("parallel",)),
    )(w)


def _hosted_gather(bigs):
    nb = len(bigs)

    def rows(a, c):
        half = bigs[a].shape[1] // 2
        return pl.ds(c * half, half)

    def start(ins, outs, send_sems, recv_sems):
        x, y, c, chips = _place()
        q = 2 * x + y
        for a in range(nb):
            for k, chip in enumerate(chips):
                _rcopy(outs[a].at[q, rows(a, c)], outs[a].at[q, rows(a, c)], send_sems, recv_sems, 6 * a + k,
                       (chip[0], chip[1], c)).start()

    def mid(ins, outs, send_sems, recv_sems):
        x, y, c, chips = _place()
        sib = (x, y, 1 - c)
        for a in range(nb):
            for k, chip in enumerate(chips):
                slab = outs[a].at[2 * chip[0] + chip[1], rows(a, c)]
                _rcopy(slab, slab, send_sems, recv_sems, 6 * a + k, sib).wait_recv()
                _rcopy(slab, slab, send_sems, recv_sems, 6 * a + 3 + k, sib).start()

    def finish(ins, outs, send_sems, recv_sems):
        x, y, c, chips = _place()
        q = 2 * x + y
        sib = (x, y, 1 - c)
        for a in range(nb):
            for k, chip in enumerate(chips):
                qk = 2 * chip[0] + chip[1]
                other = outs[a].at[qk, rows(a, 1 - c)]
                _rcopy(other, other, send_sems, recv_sems, 6 * a + 3 + k, sib).wait_recv()
                mine = outs[a].at[q, rows(a, c)]
                _rcopy(mine, mine, send_sems, recv_sems, 6 * a + k, sib).wait_send()
                fwd = outs[a].at[qk, rows(a, c)]
                _rcopy(fwd, fwd, send_sems, recv_sems, 6 * a + 3 + k, sib).wait_send()

    return dict(arrays=list(bigs), out_shape=[jax.ShapeDtypeStruct(b.shape, b.dtype) for b in bigs],
                aliases={a: a for a in range(nb)}, nsem=6 * nb, start=start, mid=mid, finish=finish)


def _hosted_rs_chips(ps):
    na = len(ps)

    def copies(ins, outs, send_sems, recv_sems):
        x, y, c, chips = _place()
        return [_rcopy(ins[a].at[2 * chip[0] + chip[1]], outs[a].at[k], send_sems, recv_sems, 3 * a + k,
                       (chip[0], chip[1], c)) for a in range(na) for k, chip in enumerate(chips)]

    def start(ins, outs, send_sems, recv_sems):
        for cp in copies(ins, outs, send_sems, recv_sems):
            cp.start()

    def finish(ins, outs, send_sems, recv_sems):
        for cp in copies(ins, outs, send_sems, recv_sems):
            cp.wait()

    return dict(arrays=list(ps), out_shape=[jax.ShapeDtypeStruct((3,) + p.shape[1:], p.dtype) for p in ps],
                aliases={}, nsem=3 * na, start=start, mid=None, finish=finish)


def _hosted_rs_sibling(gs):
    na = len(gs)

    def copies(ins, outs, send_sems, recv_sems):
        x, y, c, _ = _place()
        halves = [g.shape[1] // 2 for g in gs]
        return [_rcopy(ins[a].at[:, pl.ds((1 - c) * halves[a], halves[a]), :], outs[a], send_sems, recv_sems, a,
                       (x, y, 1 - c)) for a in range(na)]

    def start(ins, outs, send_sems, recv_sems):
        for cp in copies(ins, outs, send_sems, recv_sems):
            cp.start()

    def finish(ins, outs, send_sems, recv_sems):
        for cp in copies(ins, outs, send_sems, recv_sems):
            cp.wait()

    return dict(arrays=list(gs), aliases={}, nsem=na, start=start, mid=None, finish=finish,
                out_shape=[jax.ShapeDtypeStruct((N_CHIPS, g.shape[1] // 2, g.shape[2]), g.dtype) for g in gs])


def _hosted_share_sibling(fs):
    na = len(fs)

    def rows(a, c):
        half = fs[a].shape[0] // 2
        return pl.ds(c * half, half)

    def start(ins, outs, send_sems, recv_sems):
        x, y, c, _ = _place()
        for a in range(na):
            _rcopy(outs[a].at[rows(a, c)], outs[a].at[rows(a, c)], send_sems, recv_sems, a, (x, y, 1 - c)).start()

    def finish(ins, outs, send_sems, recv_sems):
        x, y, c, _ = _place()
        for a in range(na):
            _rcopy(outs[a].at[rows(a, c)], outs[a].at[rows(a, c)], send_sems, recv_sems, a, (x, y, 1 - c)).wait_send()
            other = outs[a].at[rows(a, 1 - c)]
            _rcopy(other, other, send_sems, recv_sems, a, (x, y, 1 - c)).wait_recv()

    return dict(arrays=list(fs), out_shape=[jax.ShapeDtypeStruct(f.shape, f.dtype) for f in fs],
                aliases={a: a for a in range(na)}, nsem=na, start=start, mid=None, finish=finish)


def _hosted_allgather(buf):
    def copies(ins, outs, send_sems, recv_sems):
        x, y, c, _ = _place()
        me = 4 * x + 2 * y + c
        cps = []
        for k in range(1, 8):
            dev = (1 - x if k & 4 else x, 1 - y if k & 2 else y, 1 - c if k & 1 else c)
            cps.append(_rcopy(ins[0], outs[0].at[me], send_sems, recv_sems, k - 1, dev))
        return pltpu.make_async_copy(ins[0], outs[0].at[me], send_sems.at[7]), cps

    def start(ins, outs, send_sems, recv_sems):
        loc, cps = copies(ins, outs, send_sems, recv_sems)
        loc.start()
        for cp in cps:
            cp.start()

    def finish(ins, outs, send_sems, recv_sems):
        loc, cps = copies(ins, outs, send_sems, recv_sems)
        loc.wait()
        for cp in cps:
            cp.wait()

    return dict(arrays=[buf], out_shape=[jax.ShapeDtypeStruct((8,) + buf.shape, buf.dtype)], aliases={},
                nsem=8, start=start, mid=None, finish=finish)


class _SemWindow:
    def __init__(self, sems, off):
        self._sems, self._off = sems, off

    @property
    def at(self):
        return self

    def __getitem__(self, k):
        return self._sems.at[k + self._off]


def _hosted_join(parts):
    arrays, out_shape, aliases, spans, nsem = [], [], {}, [], 0
    for h in parts:
        spans.append((len(arrays), len(h['arrays']), len(out_shape), len(h['out_shape']), nsem))
        aliases.update({len(arrays) + a: len(out_shape) + b for a, b in h['aliases'].items()})
        arrays += h['arrays']
        out_shape += h['out_shape']
        nsem += h['nsem']

    def phase(name):
        if all(h[name] is None for h in parts):
            return None

        def run(ins, outs, send_sems, recv_sems):
            for h, (ia, na, io, no, s0) in zip(parts, spans):
                if h[name] is not None:
                    h[name](ins[ia:ia + na], outs[io:io + no], _SemWindow(send_sems, s0), _SemWindow(recv_sems, s0))

        return run

    return dict(arrays=arrays, out_shape=out_shape, aliases=aliases, nsem=nsem,
                start=phase('start'), mid=phase('mid'), finish=phase('finish'))


def _run_hosted(hosted, name):
    nh_in, nh_out = len(hosted['arrays']), len(hosted['out_shape'])

    def body(*refs):
        ins, outs, sems = refs[:nh_in], refs[nh_in:nh_in + nh_out], refs[-2:]
        for ph in ('start', 'mid', 'finish'):
            if hosted[ph] is not None:
                hosted[ph](ins, outs, sems[0], sems[1])

    h_in, h_out, h_shape, h_scratch, h_alias = _host_plumbing(hosted, 0, 0)
    return pl.pallas_call(body, name=name, in_specs=h_in, out_specs=h_out, out_shape=h_shape,
                          scratch_shapes=h_scratch, input_output_aliases=h_alias)(*hosted['arrays'])


def _host_plumbing(hosted, n_in, n_out):
    nh = len(hosted['arrays'])
    return ([ANY] * nh, [ANY] * len(hosted['out_shape']), list(hosted['out_shape']),
            [pltpu.SemaphoreType.DMA((hosted['nsem'],)), pltpu.SemaphoreType.DMA((hosted['nsem'],))],
            {n_in + a: n_out + b for a, b in hosted['aliases'].items()})


def _host_phase(hosted, phase, when, hins, houts, sems):
    fn = hosted[phase]
    if fn is None:
        return

    @pl.when(when)
    def _():
        fn(hins, houts, sems[0], sems[1])


def _gather_weights(bigs, smalls):
    nb, ns = len(bigs), len(smalls)
    na = nb + ns
    nsem = 6 * nb + 3 * ns
    big = _hosted_gather(bigs)

    def body(*refs):
        ins, outs = refs[:na], refs[na:2 * na]
        send_sems, recv_sems, loc_sems = refs[2 * na:]
        x, y, c, chips = _place()
        q = 2 * x + y
        sib = (x, y, 1 - c)
        locs, sends = [], []
        for s in range(ns):
            cp = pltpu.make_async_copy(ins[nb + s], outs[nb + s].at[q], loc_sems.at[s])
            cp.start()
            locs.append(cp)
        big['start'](ins[:nb], outs[:nb], send_sems, recv_sems)
        for s in range(ns):
            a = nb + s
            for k, chip in enumerate(chips):
                cp = _rcopy(ins[a], outs[a].at[q], send_sems, recv_sems, 6 * nb + 3 * s + k,
                            (chip[0], chip[1], c))
                cp.start()
                sends.append(cp)
        big['mid'](ins[:nb], outs[:nb], send_sems, recv_sems)
        big['finish'](ins[:nb], outs[:nb], send_sems, recv_sems)
        for s in range(ns):
            a = nb + s
            for k, chip in enumerate(chips):
                qk = 2 * chip[0] + chip[1]
                _rcopy(ins[a], outs[a].at[qk], send_sems, recv_sems, 6 * nb + 3 * s + k, sib).wait_recv()
        for cp in sends:
            cp.wait_send()
        for cp in locs:
            cp.wait()

    arrs = list(bigs) + list(smalls)
    out_shape = ([jax.ShapeDtypeStruct(a.shape, a.dtype) for a in bigs]
                 + [jax.ShapeDtypeStruct((N_CHIPS,) + a.shape, a.dtype) for a in smalls])
    return pl.pallas_call(
        body, name="gather_weights", in_specs=[ANY] * na, out_specs=[ANY] * na, out_shape=out_shape,
        input_output_aliases={a: a for a in range(nb)},
        scratch_shapes=[pltpu.SemaphoreType.DMA((nsem,)), pltpu.SemaphoreType.DMA((nsem,)),
                        pltpu.SemaphoreType.DMA((max(ns, 1),))],
    )(*arrs)


EW_VMEM_BUDGET = 24 * 1024 * 1024


def _row_tile(rows, cols, nbuf):
    budget = EW_VMEM_BUDGET // (nbuf * cols * 4 * 2)
    return _tile(rows, max(16, budget - budget % 16), 16) if rows % 16 == 0 else rows


def _add_pairs(g, rcv, name):
    s, half, c = rcv.shape
    tr = _row_tile(half, c, 3)
    nh = half // tr

    def body(a_ref, b_ref, o_ref):
        o_ref[...] = (a_ref[...].astype(F32) + b_ref[...].astype(F32)).astype(o_ref.dtype)

    blk = pl.BlockSpec((None, tr, c), lambda j, i: (j, i, 0))
    mine = pl.BlockSpec((None, tr, c), lambda j, i: (j, lax.axis_index("c") * nh + i, 0))
    return pl.pallas_call(
        body, name=name, grid=(s, nh), in_specs=[mine, blk], out_specs=blk,
        out_shape=jax.ShapeDtypeStruct(rcv.shape, rcv.dtype), compiler_params=_cparams(("parallel", "parallel")),
    )(g, rcv)


def _sum_chips(part, rcv, name):
    _, half, c = part.shape
    tr = _row_tile(half, c, 5)
    nh = half // tr

    def body(o_ref, r_ref, out_ref):
        acc = o_ref[...].astype(F32)
        for k in range(3):
            acc = acc + r_ref[k].astype(F32)
        out_ref[...] = acc

    return pl.pallas_call(
        body, name=name, grid=(nh,),
        in_specs=[pl.BlockSpec((None, tr, c), lambda i: (_my_chip(), i, 0)),
                  pl.BlockSpec((3, tr, c), lambda i: (0, i, 0))],
        out_specs=pl.BlockSpec((tr, c), lambda i: (lax.axis_index("c") * nh + i, 0)),
        out_shape=jax.ShapeDtypeStruct((2 * half, c), F32), compiler_params=_cparams(("parallel",)),
    )(part, rcv)


def _sum_devices(parts, name):
    _, n, _ = parts.shape
    tr = n if n <= 4096 else _tile(n, 512, 8)

    def body(p_ref, o_ref):
        acc = p_ref[0]
        for k in range(1, 8):
            acc = acc + p_ref[k]
        o_ref[...] = acc

    return pl.pallas_call(
        body, name=name, grid=(n // tr,),
        in_specs=[pl.BlockSpec((8, tr, 128), lambda i: (0, i, 0))],
        out_specs=pl.BlockSpec((tr, 128), lambda i: (i, 0)),
        out_shape=jax.ShapeDtypeStruct((n, 128), F32), compiler_params=_cparams(("parallel",)),
    )(parts)


def _adamw(w, g, m, v, name):
    r, c = w.shape
    tr = _row_tile(r, c, 7)
    c1 = 1.0 - ADAM_B1 ** ADAM_STEP
    c2 = 1.0 - ADAM_B2 ** ADAM_STEP

    def body(w_ref, g_ref, m_ref, v_ref, d_ref, mo_ref, vo_ref):
        gv = g_ref[...]
        mn = ADAM_B1 * m_ref[...] + (1.0 - ADAM_B1) * gv
        vn = ADAM_B2 * v_ref[...] + (1.0 - ADAM_B2) * (gv * gv)
        mo_ref[...] = mn
        vo_ref[...] = vn
        m_hat = mn / c1
        v_hat = vn / c2
        d_ref[...] = -ADAM_LR * (m_hat / (jnp.sqrt(v_hat) + ADAM_EPS) + ADAM_WD * w_ref[...])

    blk = pl.BlockSpec((tr, c), lambda i: (i, 0))
    sh = jax.ShapeDtypeStruct((r, c), F32)
    return pl.pallas_call(
        body, name=name, grid=(r // tr,), in_specs=[blk] * 4, out_specs=[blk] * 3, out_shape=[sh] * 3,
        compiler_params=_cparams(("parallel",)),
    )(w, g, m, v)


WEIGHTS = ['norm_mix_g', 'w_in', 'ln_a_g', 'ln_a_b', 'w_s', 'b_s', 'norm_a_g', 'conv_ssm_w', 'conv_ssm_b',
           'dt_bias', 'a_log', 'd_skip', 'ssm_norm_g', 'w_out', 'norm_ffn_g', 'w_up', 'conv_ffn_w',
           'conv_ffn_b', 'w_down', 'norm_ple_g', 'w_ple_gate', 'w_ple', 'norm_final_g']
BIG = ['w_in', 'w_out', 'w_up', 'w_down', 'w_ple_gate', 'w_ple']
SMALL = [n for n in WEIGHTS if n not in BIG]
PACK_ALIGN = 2048


def _pack(arrs):
    parts = []
    for a in arrs:
        f = a.reshape(-1).astype(F32)
        parts.append(jnp.pad(f, (0, (-f.shape[0]) % PACK_ALIGN)))
    return jnp.concatenate(parts).reshape(-1, 128)


def _unpack(buf, shapes):
    flat = buf.reshape(-1)
    out, off = [], 0
    for s in shapes:
        n = math.prod(s)
        out.append(flat[off:off + n].reshape(s))
        off += n + (-n) % PACK_ALIGN
    return out


def _pad_heads(v):
    return jnp.pad(v, ((0, 0), (0, HPAD - v.shape[1])))


def _col_sharded(full):
    r, c4 = full.shape
    return jnp.transpose(full.reshape(r, N_CHIPS, c4 // N_CHIPS), (1, 0, 2))


def _from_col_sharded(g):
    s, r, c = g.shape
    return jnp.transpose(g, (1, 0, 2)).reshape(r, s * c)


def kernel(x, p, norm_mix_g, w_in, ln_a_g, ln_a_b, w_s, b_s, norm_a_g, conv_ssm_w, conv_ssm_b, dt_bias, a_log, d_skip, ssm_norm_g, w_out, norm_ffn_g, w_up, conv_ffn_w, conv_ffn_b, w_down, norm_ple_g, w_ple_gate, w_ple, norm_final_g, loss_target, m_norm_mix_g, m_w_in, m_ln_a_g, m_ln_a_b, m_w_s, m_b_s, m_norm_a_g, m_conv_ssm_w, m_conv_ssm_b, m_dt_bias, m_a_log, m_d_skip, m_ssm_norm_g, m_w_out, m_norm_ffn_g, m_w_up, m_conv_ffn_w, m_conv_ffn_b, m_w_down, m_norm_ple_g, m_w_ple_gate, m_w_ple, m_norm_final_g, v_norm_mix_g, v_w_in, v_ln_a_g, v_ln_a_b, v_w_s, v_b_s, v_norm_a_g, v_conv_ssm_w, v_conv_ssm_b, v_dt_bias, v_a_log, v_d_skip, v_ssm_norm_g, v_w_out, v_norm_ffn_g, v_w_up, v_conv_ffn_w, v_conv_ffn_b, v_w_down, v_norm_ple_g, v_w_ple_gate, v_w_ple, v_norm_final_g):
    given = dict(locals())
    wts = {n: given[n] for n in WEIGHTS}
    mom = {n: given['m_' + n] for n in WEIGHTS}
    var = {n: given['v_' + n] for n in WEIGHTS}
    d = D_MODEL
    xt, pt, tgt = x[0], p[0, 0], loss_target[0]
    chip = 2 * lax.axis_index("x") + lax.axis_index("y")

    slots = {n: _cast_into_slot(wts[n][0], "cast_" + n) for n in BIG}
    g_in, g_cs, g_cf = _gather_weights([slots['w_in']], [conv_ssm_w[0], conv_ffn_w[0]])
    w_in_full = _from_col_sharded(g_in)
    w_main = w_in_full
    w_dt = _pad_heads(w_in_full[:, D_MAIN:])
    cs_w = _from_col_sharded(g_cs)
    cf_w = _from_col_sharded(g_cf)
    consts = _ssd_consts()
    dtb, alog = _pad_heads(dt_bias), _pad_heads(a_log)
    de = jnp.repeat(d_skip[0], HEAD_DIM)[None, :]
    b_exp = jnp.broadcast_to(b_s[0][:, :, None], (N_GROUPS_A, CHUNK, CHUNK))

    a1 = _rms_fwd(xt, norm_mix_g, "rms_mix")
    proj, g_out = _matmul(a1, w_main, mode='nn', name="mm_proj", tm=1024, tn=1024, b_cols=D_MAIN,
                          hosted=_hosted_gather([slots['w_out']]))
    dt_raw = _matmul(a1, w_dt, mode='nn', name="mm_dt", tm=1024, tn=128)
    yab = _gmlp_fwd(proj, ln_a_g, ln_a_b, w_s[0], b_exp, norm_a_g)
    yab, ysave, hs, g_up = _ssd_fwd(proj, dt_raw, yab, cs_w, conv_ssm_b, dtb, alog, de, ssm_norm_g, consts,
                                    _hosted_gather([slots['w_up']]))
    w_out_f = g_out.reshape(D_MIX, d)
    h1 = _matmul(yab, w_out_f, mode='nn', name="mm_out", res=xt, tm=512, tn=1024, tk=4096)
    f = _rms_fwd(h1, norm_ffn_g, "rms_ffn")
    hid, g_down, g_pg, g_ple = _matmul(
        f, g_up, mode='nn', name="mm_up", b_sharded=True, tm=1024, tn=1408,
        hosted=_hosted_gather([slots[n] for n in ('w_down', 'w_ple_gate', 'w_ple')]))
    w_down_f = g_down.reshape(D_FF, d)
    w_pg_f = g_pg.reshape(d, d)
    act, conv_g, conv_u = _ffn_act_fwd(hid, cf_w, conv_ffn_b)
    h2 = _matmul(act, w_down_f, mode='nn', name="mm_down", res=h1, tm=1024, tn=1024, tk=2816)
    n3 = _rms_fwd(h2, norm_ple_g, "rms_ple")
    gl = _matmul(n3, w_pg_f, mode='nn', name="mm_pg", tm=1024, tn=1024)
    pe = _matmul(pt, g_ple, mode='nn', name="mm_ple", b_sharded=True, tm=1024, tn=512)
    dh3, dgl, dpe, lossv, dgf = _tail(h2, gl, pe, tgt, norm_final_g[None, :])

    gs_ple = _matmul(pt, dpe, mode='tn', name="mm_dw_ple", out_dtype=BF16, out_shards=N_CHIPS,
                     tm=256, tn=512, tk=2048)
    gs_pg = _matmul(n3, dgl, mode='tn', name="mm_dw_pg", out_dtype=BF16, tm=1024, tn=1024, tk=4096)
    dn3 = _matmul(dgl, w_pg_f, mode='nt', name="mm_dn3", out_dtype=BF16, tm=1024, tn=1024)
    dh2, dg_ple, dh2_b = _rms_bwd(h2, norm_ple_g, dn3, dh3, "rms_ple_bwd", True)
    dact = _matmul(dh2_b, w_down_f, mode='nt', name="mm_dact", out_dtype=BF16, tm=1024, tn=1408)
    gs_down = _matmul(act, dh2_b, mode='tn', name="mm_dw_down", out_dtype=BF16, tm=1408, tn=1024, tk=2048)
    dpg, dpu, wg_acc, wu_acc = _ffn_act_bwd(hid, cf_w, conv_g, conv_u, dact)
    hc = N_CHIPS // 2
    gs_up = jnp.concatenate(
        [_matmul(f, dpg, mode='tn', name="mm_dw_up_g", out_dtype=BF16, out_shards=hc, tm=1024, tn=1408, tk=2048),
         _matmul(f, dpu, mode='tn', name="mm_dw_up_u", out_dtype=BF16, out_shards=hc, tm=1024, tn=1408, tk=2048)],
        axis=0)
    early = [gs_up, gs_down.reshape(N_CHIPS, D_FF // N_CHIPS, d), gs_pg.reshape(N_CHIPS, d // N_CHIPS, d), gs_ple]
    df, *sib_e = _matmul(dpg, g_up, mode='nt', name="mm_df_g", b_sharded=True, tm=1024, tn=1024, tk=2816,
                         hosted=_hosted_rs_sibling(early))
    part_e = [_add_pairs(a, b, "rs_add_e%d" % i) for i, (a, b) in enumerate(zip(early, sib_e))]
    df = _matmul(dpu, g_up, mode='nt', name="mm_df_u", b_sharded=True, b_shard_off=hc, res=df, out_dtype=BF16,
                 tm=1024, tn=1024, tk=2816)
    dh1, dg_ffn, dh1_b = _rms_bwd(h1, norm_ffn_g, df, dh2, "rms_ffn_bwd", True)
    dyab = _matmul(dh1_b, w_out_f, mode='nt', name="mm_dyab", out_dtype=BF16, tm=1024, tn=1024)
    gs_out = _matmul(yab, dh1_b, mode='tn', name="mm_dw_out", out_dtype=BF16, tm=1024, tn=1024, tk=4096)
    duv, dws, dbs, dlng, dlnb, dnag = _gmlp_bwd(proj, dyab, ln_a_g, ln_a_b, w_s[0], b_exp, norm_a_g)
    dproj, ddt_raw, acc_x, acc_b, acc_c, acc_gain, acc_head, *rcv_e = _ssd_bwd(
        proj, dt_raw, dyab, ysave, hs, duv, cs_w, conv_ssm_b, dtb, alog, de, ssm_norm_g, consts,
        _hosted_rs_chips(part_e))
    dw_main = _matmul(a1, dproj, mode='tn', name="mm_dw_main", out_dtype=BF16, tm=1024, tn=1024, tk=4096)
    dw_dt = _matmul(a1, ddt_raw, mode='tn', name="mm_dw_dt", out_dtype=BF16, tm=1024, tn=128, tk=2048)
    gs_in = _col_sharded(jnp.concatenate([dw_main, dw_dt[:, :N_HEADS]], axis=1))
    late = [gs_in, gs_out.reshape(N_CHIPS, D_MIX // N_CHIPS, d)]
    part_l = [_add_pairs(a, b, "rs_add_l%d" % i) for i, (a, b) in enumerate(
        zip(late, _run_hosted(_hosted_rs_sibling(late), "rs_sibling_late")))]

    def conv_rows(acc, k):
        return acc[:, k, :].reshape(1, -1)

    dcw = jnp.concatenate([jnp.concatenate([conv_rows(acc_x, k), conv_rows(acc_b, k), conv_rows(acc_c, k)], axis=1)
                           for k in range(SSM_CONV)], axis=0)
    dcb = jnp.concatenate([conv_rows(acc_x, SSM_CONV), conv_rows(acc_b, SSM_CONV), conv_rows(acc_c, SSM_CONV)], axis=1)
    part = {
        'ln_a_g': dlng, 'ln_a_b': dlnb, 'w_s': dws, 'b_s': dbs, 'norm_a_g': dnag,
        'conv_ssm_w': dcw, 'conv_ssm_b': dcb,
        'dt_bias': acc_head[0:1, :N_HEADS], 'a_log': acc_head[1:2, :N_HEADS], 'd_skip': acc_head[2:3, :N_HEADS],
        'ssm_norm_g': acc_gain[:, 0, :], 'norm_ffn_g': dg_ffn,
        'conv_ffn_w': jnp.concatenate([wg_acc[:FFN_CONV], wu_acc[:FFN_CONV]], axis=1),
        'conv_ffn_b': jnp.concatenate([wg_acc[FFN_CONV:FFN_CONV + 1], wu_acc[FFN_CONV:FFN_CONV + 1]], axis=1),
        'norm_ple_g': dg_ple, 'norm_final_g': dgf,
    }
    full_shapes = {n: wts[n].shape for n in SMALL}
    full_shapes['conv_ssm_w'] = (1, SSM_CONV, D_XBC)
    full_shapes['conv_ffn_w'] = (1, FFN_CONV, 2 * D_FF)
    small_e = [n for n in SMALL if n != 'norm_mix_g']
    packed = _pack([part[n] for n in small_e] + [lossv[:, 0:1]])

    halves_e = [_sum_chips(a, b, "rs_sum_e%d" % i) for i, (a, b) in enumerate(zip(part_e, rcv_e))]
    da_dt = _matmul(ddt_raw, w_dt, mode='nt', name="mm_da_dt", tm=1024, tn=1024)
    da, *moved = _matmul(dproj, w_main, mode='nt', name="mm_da", res=da_dt, out_dtype=BF16, tm=1024, tn=1024, tk=2560,
                         hosted=_hosted_join([_hosted_rs_chips(part_l), _hosted_share_sibling(halves_e),
                                              _hosted_allgather(packed)]))
    rcv_l, g_early, gathered = moved[:len(late)], moved[len(late):-1], moved[-1]
    dx, dg_mix = _rms_bwd(xt, norm_mix_g, da, dh1, "rms_mix_bwd", False)

    halves_l = [_sum_chips(a, b, "rs_sum_l%d" % i) for i, (a, b) in enumerate(zip(part_l, rcv_l))]
    g_big = dict(zip(['w_up', 'w_down', 'w_ple_gate', 'w_ple'], g_early))
    g_big.update(zip(['w_in', 'w_out'], _run_hosted(_hosted_share_sibling(halves_l), "share_sibling_late")))

    pieces = _unpack(_sum_devices(gathered, "sum_devices"), [full_shapes[n] for n in small_e] + [(1,)])
    g_small = dict(zip(small_e, pieces[:-1]))
    loss = pieces[-1][0]
    mix = _sum_devices(_run_hosted(_hosted_allgather(_pack([dg_mix])), "allgather_mix")[0], "sum_devices_mix")
    g_small['norm_mix_g'] = _unpack(mix, [full_shapes['norm_mix_g']])[0]
    for n in ('conv_ssm_w', 'conv_ffn_w'):
        width = wts[n].shape[2]
        g_small[n] = lax.dynamic_slice_in_dim(g_small[n], chip * width, width, axis=2)

    grads, delta, new_m, new_v = {}, {}, {}, {}
    for n in BIG:
        shp = wts[n].shape
        dl, mn, vn = _adamw(wts[n][0], g_big[n], mom[n][0], var[n][0], "adamw_" + n)
        grads[n], delta[n], new_m[n], new_v[n] = (g_big[n].reshape(shp), dl.reshape(shp), mn.reshape(shp),
                                                  vn.reshape(shp))
    shapes = [wts[n].shape for n in SMALL]
    dl, mn, vn = _adamw(_pack([wts[n] for n in SMALL]), _pack([g_small[n] for n in SMALL]),
                        _pack([mom[n] for n in SMALL]), _pack([var[n] for n in SMALL]), "adamw_small")
    for n, a, b, c in zip(SMALL, _unpack(dl, shapes), _unpack(mn, shapes), _unpack(vn, shapes)):
        grads[n], delta[n], new_m[n], new_v[n] = g_small[n], a, b, c

    return (loss, dx[None], *[grads[n] for n in WEIGHTS], *[delta[n] for n in WEIGHTS],
            *[new_m[n] for n in WEIGHTS], *[new_v[n] for n in WEIGHTS])
```

```python
import math

import jax
import jax.numpy as jnp
from jax import lax
from jax.experimental import pallas as pl
from jax.experimental.pallas import tpu as pltpu

D_MODEL = 2048
SEQ = 8192
D_MIX = 2 * D_MODEL
D_A = D_MIX // 2
CHUNK = 128
N_GROUPS_A = D_A // 128
D_SSM = D_MIX - D_A
HEAD_DIM = 64
N_HEADS = D_SSM // HEAD_DIM
HEADS_PER_GROUP = 4
N_SSM_GROUPS = N_HEADS // HEADS_PER_GROUP
GW = HEADS_PER_GROUP * HEAD_DIM
D_STATE = 128
SSM_CONV = 4
D_BC = N_SSM_GROUPS * D_STATE
D_XBC = D_SSM + 2 * D_BC
D_MAIN = 2 * D_A + D_SSM + D_XBC
D_IN = D_MAIN + N_HEADS
D_FF = (D_MODEL * 11) // 4
FFN_CONV = 3
D_PLE = 256
EPS = 1e-6
HPAD = 128
N_CHIPS = 4

ADAM_LR = 0.001
ADAM_B1 = 0.9
ADAM_B2 = 0.999
ADAM_EPS = 1e-08
ADAM_WD = 0.01
ADAM_STEP = 10

F32 = jnp.float32
BF16 = jnp.bfloat16
MESH = pl.DeviceIdType.MESH
VMEM_LIMIT = 56 * 1024 * 1024


def _cparams(sem):
    return pltpu.CompilerParams(dimension_semantics=sem, vmem_limit_bytes=VMEM_LIMIT)


def _forward_step(n_steps):
    return (7 * n_steps) // 8


def _tile(n, pref, mult):
    t = min(pref, n)
    t -= t % mult
    while n % t:
        t -= mult
    return t


def _dot(a, b):
    return jnp.dot(a, b, preferred_element_type=F32)


def _dot_nt(a, b):
    return lax.dot_general(a, b, (((1,), (1,)), ((), ())), preferred_element_type=F32)


def _dot_tn(a, b):
    return lax.dot_general(a, b, (((0,), (0,)), ((), ())), preferred_element_type=F32)


def _split3(x):
    hi = x.astype(BF16)
    r = x - hi.astype(F32)
    mid = r.astype(BF16)
    lo = (r - mid.astype(F32)).astype(BF16)
    return hi, mid, lo


def _x01(x, e):
    h, m, l = _split3(x)
    return _dot(h, e) + _dot(m, e) + _dot(l, e)


def _x01_nt(x, e):
    h, m, l = _split3(x)
    return _dot_nt(h, e) + _dot_nt(m, e) + _dot_nt(l, e)


def _e01x(e, x):
    h, m, l = _split3(x)
    return _dot(e, h) + _dot(e, m) + _dot(e, l)


def _e01x_tn(e, x):
    h, m, l = _split3(x)
    return _dot_tn(e, h) + _dot_tn(e, m) + _dot_tn(e, l)


def _sigmoid(x):
    return 1.0 / (1.0 + jnp.exp(-x))


_GELU_C = math.sqrt(2.0 / math.pi)


def _gelu_and_grad(x):
    x2 = x * x
    th = jnp.tanh(_GELU_C * (x + 0.044715 * x * x2))
    y = 0.5 * x * (1.0 + th)
    dy = 0.5 * (1.0 + th) + 0.5 * x * (1.0 - th * th) * (_GELU_C * (1.0 + 3.0 * 0.044715 * x2))
    return y, dy


def _silu_and_grad(x):
    s = _sigmoid(x)
    return x * s, s * (1.0 + x * (1.0 - s))


def _softplus(x):
    u = jnp.exp(-jnp.abs(x))
    w = 1.0 + u
    l1p = jnp.where(w == 1.0, u, jnp.log(w) * (u / (w - 1.0)))
    return jnp.maximum(x, 0.0) + l1p


def _matmul(a, b, *, mode, name, out_dtype=F32, res=None, tm=512, tn=512, tk=2048,
            b_sharded=False, b_shard_off=0, b_cols=None, out_shards=0, hosted=None):
    if mode == 'tn':
        kdim, m = a.shape
        n = b.shape[1]
    else:
        m, kdim = a.shape
        if b_sharded:
            s_b, d1, d2 = b.shape
            n = s_b * d2 if mode == 'nn' else d1
        else:
            n = b.shape[1] if mode == 'nn' else b.shape[0]
        if b_cols is not None:
            n = b_cols
    per = None
    if b_sharded:
        per = b.shape[2]
    if out_shards:
        per = n // out_shards
    tm = _tile(m, tm, 128 if mode == 'tn' else 8)
    if mode == 'nt' and b_sharded:
        tn = _tile(n, tn, 128)
        tk = _tile(per, tk, 128)
    elif per is not None:
        tn = _tile(per, tn, 128)
        tk = _tile(kdim, tk, 128 if mode != 'tn' else 8)
    else:
        tn = _tile(n, tn, 128)
        tk = _tile(kdim, tk, 128 if mode != 'tn' else 8)
    nm, nn_, nk = m // tm, n // tn, kdim // tk
    has_res = res is not None
    n_in = 2 + has_res
    nh_in = len(hosted['arrays']) if hosted else 0
    nh_out = len(hosted['out_shape']) if hosted else 0

    def body(*refs):
        a_ref, b_ref = refs[0], refs[1]
        res_ref = refs[2] if has_res else None
        o_ref = refs[n_in + nh_in]
        if hosted:
            hins = refs[n_in:n_in + nh_in]
            houts = refs[n_in + nh_in + 1:n_in + nh_in + 1 + nh_out]
            sems = refs[-2:]
            ids = [pl.program_id(d) for d in range(3)]
            step = (ids[0] * nm + ids[1]) * nk + ids[2]
            n_steps = nn_ * nm * nk
            at_first, at_mid, at_last = step == 0, step == _forward_step(n_steps), step == n_steps - 1
            _host_phase(hosted, 'start', at_first, hins, houts, sems)
        av = a_ref[...].astype(BF16)
        bv = b_ref[...].astype(BF16)
        if mode == 'nn':
            p = _dot(av, bv)
        elif mode == 'nt':
            p = _dot_nt(av, bv)
        else:
            p = _dot_tn(av, bv)

        def fin(v):
            if has_res:
                v = v + res_ref[...]
            o_ref[...] = v.astype(o_ref.dtype)

        if nk == 1:
            fin(p)
        else:
            acc_ref = refs[n_in + nh_in + 1 + nh_out]
            k = pl.program_id(2)

            @pl.when(k == 0)
            def _():
                acc_ref[...] = p

            @pl.when(k > 0)
            def _():
                acc_ref[...] += p

            @pl.when(k == nk - 1)
            def _():
                fin(acc_ref[...])
        if hosted:
            _host_phase(hosted, 'mid', at_mid, hins, houts, sems)
            _host_phase(hosted, 'finish', at_last, hins, houts, sems)

    if mode == 'nn':
        a_spec = pl.BlockSpec((tm, tk), lambda j, i, k: (i, k))
        if b_sharded:
            nps = per // tn
            b_spec = pl.BlockSpec((None, tk, tn), lambda j, i, k: (j // nps, k, j % nps))
        else:
            b_spec = pl.BlockSpec((tk, tn), lambda j, i, k: (k, j))
    elif mode == 'nt':
        a_spec = pl.BlockSpec((tm, tk), lambda j, i, k: (i, k))
        if b_sharded:
            kps = per // tk
            b_spec = pl.BlockSpec((None, tn, tk), lambda j, i, k: (k // kps + b_shard_off, j, k % kps))
        else:
            b_spec = pl.BlockSpec((tn, tk), lambda j, i, k: (j, k))
    else:
        a_spec = pl.BlockSpec((tk, tm), lambda j, i, k: (k, i))
        b_spec = pl.BlockSpec((tk, tn), lambda j, i, k: (k, j))
    in_specs = [a_spec, b_spec]
    args = [a, b]
    if has_res:
        in_specs.append(pl.BlockSpec((tm, tn), lambda j, i, k: (i, j)))
        args.append(res)
    if out_shards:
        nps_o = per // tn
        out_shape = jax.ShapeDtypeStruct((out_shards, m, per), out_dtype)
        out_spec = pl.BlockSpec((None, tm, tn), lambda j, i, k: (j // nps_o, i, j % nps_o))
    else:
        out_shape = jax.ShapeDtypeStruct((m, n), out_dtype)
        out_spec = pl.BlockSpec((tm, tn), lambda j, i, k: (i, j))
    scratch = [pltpu.VMEM((tm, tn), F32)] if nk > 1 else []
    if not hosted:
        return pl.pallas_call(
            body, name=name, grid=(nn_, nm, nk), in_specs=in_specs, out_specs=out_spec,
            out_shape=out_shape, scratch_shapes=scratch,
            compiler_params=_cparams(("parallel", "parallel", "arbitrary")),
        )(*args)
    h_in, h_out, h_shape, h_scratch, h_alias = _host_plumbing(hosted, n_in, 1)
    return pl.pallas_call(
        body, name=name, grid=(nn_, nm, nk), in_specs=in_specs + h_in, out_specs=[out_spec] + h_out,
        out_shape=[out_shape] + h_shape, scratch_shapes=scratch + h_scratch, input_output_aliases=h_alias,
        compiler_params=_cparams(("arbitrary", "arbitrary", "arbitrary")),
    )(*args, *hosted['arrays'])


def _rms_fwd(x, g, name):
    t, d = x.shape
    tt = _tile(t, 512, 8)

    def body(x_ref, g_ref, o_ref):
        xv = x_ref[...]
        r = lax.rsqrt(jnp.mean(xv * xv, axis=-1, keepdims=True) + EPS)
        o_ref[...] = (xv * r * g_ref[...]).astype(o_ref.dtype)

    return pl.pallas_call(
        body, name=name, grid=(t // tt,),
        in_specs=[pl.BlockSpec((tt, d), lambda i: (i, 0)), pl.BlockSpec((1, d), lambda i: (0, 0))],
        out_specs=pl.BlockSpec((tt, d), lambda i: (i, 0)),
        out_shape=jax.ShapeDtypeStruct((t, d), BF16),
        compiler_params=_cparams(("parallel",)),
    )(x, g)


def _rms_bwd(x, g, dy, dres, name, also_bf16):
    t, d = x.shape
    tt = _tile(t, 512, 16)

    def body(x_ref, g_ref, dy_ref, dres_ref, dx_ref, dg_ref, *dxb):
        i = pl.program_id(0)
        xv = x_ref[...]
        r = lax.rsqrt(jnp.mean(xv * xv, axis=-1, keepdims=True) + EPS)
        xh = xv * r
        dyv = dy_ref[...].astype(F32)
        dxh = dyv * g_ref[...]
        c = jnp.mean(dxh * xh, axis=-1, keepdims=True)
        dx = dres_ref[...] + r * (dxh - xh * c)
        dx_ref[...] = dx
        for dxb_ref in dxb:
            dxb_ref[...] = dx.astype(BF16)
        part = jnp.sum(dyv * xh, axis=0, keepdims=True)

        @pl.when(i == 0)
        def _():
            dg_ref[...] = part

        @pl.when(i > 0)
        def _():
            dg_ref[...] += part

    row = pl.BlockSpec((tt, d), lambda i: (i, 0))
    vec = pl.BlockSpec((1, d), lambda i: (0, 0))
    return pl.pallas_call(
        body, name=name, grid=(t // tt,),
        in_specs=[row, vec, row, row], out_specs=[row, vec] + [row] * also_bf16,
        out_shape=[jax.ShapeDtypeStruct((t, d), F32), jax.ShapeDtypeStruct((1, d), F32)]
        + [jax.ShapeDtypeStruct((t, d), BF16)] * also_bf16,
        compiler_params=_cparams(("arbitrary",)),
    )(x, g, dy, dres)


def _tail(h2, gl, pe, target, gfin):
    t, d = h2.shape
    tt = _tile(t, 256, 16)

    def body(h2_ref, gl_ref, pe_ref, tg_ref, gf_ref, dh3_ref, dgl_ref, dpe_ref, loss_ref, dgf_ref):
        i = pl.program_id(0)
        sig = _sigmoid(gl_ref[...])
        pev = pe_ref[...]
        h3 = h2_ref[...] + sig * pev
        r = lax.rsqrt(jnp.mean(h3 * h3, axis=-1, keepdims=True) + EPS)
        xh = h3 * r
        gf = gf_ref[...]
        e = xh * gf - tg_ref[...]
        dy = e * (1.0 / d)
        dxh = dy * gf
        c = jnp.mean(dxh * xh, axis=-1, keepdims=True)
        dh3 = r * (dxh - xh * c)
        dh3_ref[...] = dh3
        dgl_ref[...] = (dh3 * pev * sig * (1.0 - sig)).astype(BF16)
        dpe_ref[...] = (dh3 * sig).astype(BF16)
        lpart = jnp.sum(e * e, axis=0, keepdims=True) * (0.5 / d)
        gpart = jnp.sum(dy * xh, axis=0, keepdims=True)

        @pl.when(i == 0)
        def _():
            loss_ref[...] = lpart
            dgf_ref[...] = gpart

        @pl.when(i > 0)
        def _():
            loss_ref[...] += lpart
            dgf_ref[...] += gpart

        @pl.when(i == t // tt - 1)
        def _():
            loss_ref[...] = jnp.broadcast_to(jnp.sum(loss_ref[...], axis=-1, keepdims=True), (1, d))

    row = pl.BlockSpec((tt, d), lambda i: (i, 0))
    vec = pl.BlockSpec((1, d), lambda i: (0, 0))
    return pl.pallas_call(
        body, name="tail", grid=(t // tt,),
        in_specs=[row, row, row, row, vec], out_specs=[row, row, row, vec, vec],
        out_shape=[jax.ShapeDtypeStruct((t, d), F32), jax.ShapeDtypeStruct((t, d), BF16),
                   jax.ShapeDtypeStruct((t, d), BF16), jax.ShapeDtypeStruct((1, d), F32),
                   jax.ShapeDtypeStruct((1, d), F32)],
        compiler_params=_cparams(("arbitrary",)),
    )(h2, gl, pe, target, gfin)


def _conv(cur_ref, prev_ref, w_ref, b_ref, ext_ref, first, width):
    rows = cur_ref.shape[0]
    ext_ref[0:8, :] = jnp.where(first, 0.0, prev_ref[...])
    ext_ref[8:8 + rows, :] = cur_ref[...]
    acc = b_ref[...]
    for k in range(width):
        acc = acc + w_ref[k:k + 1, :] * ext_ref[pl.ds(9 - width + k, rows), :]
    return acc


def _ffn_specs(t, tt, tc, nf):
    hb = tt // 8
    cur_g = pl.BlockSpec((tt, tc), lambda j, i: (i, j))
    cur_u = pl.BlockSpec((tt, tc), lambda j, i: (i, j + nf))
    prev_g = pl.BlockSpec((8, tc), lambda j, i: (jnp.maximum(i * hb - 1, 0), j))
    prev_u = pl.BlockSpec((8, tc), lambda j, i: (jnp.maximum(i * hb - 1, 0), j + nf))
    w_g = pl.BlockSpec((FFN_CONV, tc), lambda j, i: (0, j))
    w_u = pl.BlockSpec((FFN_CONV, tc), lambda j, i: (0, j + nf))
    b_g = pl.BlockSpec((1, tc), lambda j, i: (0, j))
    b_u = pl.BlockSpec((1, tc), lambda j, i: (0, j + nf))
    return [cur_g, prev_g, cur_u, prev_u, w_g, w_u, b_g, b_u]


FFN_TC = 512


def _shift_down(prev, cur, n, rid):
    return jnp.where(rid < n, pltpu.roll(prev, n, 0), pltpu.roll(cur, n, 0))


def _shift_up(cur, nxt, n, rid):
    return jnp.where(rid < 8 - n, pltpu.roll(cur, 8 - n, 0), pltpu.roll(nxt, 8 - n, 0))


def _conv3_group(prev, cur, w_ref, b_ref, rid):
    x1 = _shift_down(prev, cur, 1, rid)
    x2 = _shift_down(prev, cur, 2, rid)
    return b_ref[...] + w_ref[2:3, :] * cur + w_ref[1:2, :] * x1 + w_ref[0:1, :] * x2


def _ffn_act_fwd(hid, cw, cb):
    t = hid.shape[0]
    tt = _tile(t, 2048, 16)
    tc = _tile(D_FF, FFN_TC, 128)
    nf = D_FF // tc

    def body(g_ref, gp_ref, u_ref, up_ref, wg_ref, wu_ref, bg_ref, bu_ref, o_ref, cg_ref, cu_ref):
        first = pl.program_id(1) == 0
        rid = lax.broadcasted_iota(jnp.int32, (8, tc), 0)

        def step(s, carry):
            pg, pu = carry
            r0 = pl.multiple_of(s * 16, 16)
            g0, g1 = g_ref[pl.ds(r0, 8), :], g_ref[pl.ds(r0 + 8, 8), :]
            u0, u1 = u_ref[pl.ds(r0, 8), :], u_ref[pl.ds(r0 + 8, 8), :]
            gate = jnp.concatenate([_conv3_group(pg, g0, wg_ref, bg_ref, rid),
                                    _conv3_group(g0, g1, wg_ref, bg_ref, rid)], axis=0)
            up = jnp.concatenate([_conv3_group(pu, u0, wu_ref, bu_ref, rid),
                                  _conv3_group(u0, u1, wu_ref, bu_ref, rid)], axis=0)
            cg_ref[pl.ds(r0, 16), :] = gate.astype(BF16)
            cu_ref[pl.ds(r0, 16), :] = up.astype(BF16)
            o_ref[pl.ds(r0, 16), :] = (gate * _sigmoid(gate) * up).astype(BF16)
            return g1, u1

        init = (jnp.where(first, 0.0, gp_ref[...]), jnp.where(first, 0.0, up_ref[...]))
        lax.fori_loop(0, tt // 16, step, init, unroll=2)

    blk = pl.BlockSpec((tt, tc), lambda j, i: (i, j))
    return pl.pallas_call(
        body, name="ffn_act_fwd", grid=(nf, t // tt), in_specs=_ffn_specs(t, tt, tc, nf),
        out_specs=[blk, blk, blk],
        out_shape=[jax.ShapeDtypeStruct((t, D_FF), BF16)] * 3,
        compiler_params=_cparams(("parallel", "arbitrary")),
    )(hid, hid, hid, hid, cw, cw, cb, cb)


def _ffn_act_bwd(hid, cw, conv_g, conv_u, dact):
    t = hid.shape[0]
    tt = _tile(t, 2048, 16)
    tc = _tile(D_FF, FFN_TC, 128)
    nf = D_FF // tc
    nt = t // tt
    n16 = tt // 16

    def body(g_ref, u_ref, wg_ref, wu_ref, cg_ref, cgn_ref, cu_ref, cun_ref, da_ref, dan_ref,
             og_ref, ou_ref, ag_ref, au_ref, accs):
        i = pl.program_id(1)
        first, last = i == 0, i == nt - 1
        rid = lax.broadcasted_iota(jnp.int32, (8, tc), 0)
        accs[...] = jnp.zeros_like(accs)

        def dgroup(gate, up, da):
            sv, sgr = _silu_and_grad(gate)
            return da * up * sgr, da * sv

        def finish(x, d0, d1, w_ref):
            s1 = _shift_up(d0, d1, 1, rid)
            s2 = _shift_up(d0, d1, 2, rid)
            dpre = w_ref[2:3, :] * d0 + w_ref[1:2, :] * s1 + w_ref[0:1, :] * s2
            return dpre, (x * s2, x * s1, x * d0, d0)

        def two_groups(it, carry, gate_blk, up_blk, da_blk, zero_ahead):
            d0g, d0u, gate1, up1, da1 = carry
            r0 = it * 16 if isinstance(it, int) else pl.multiple_of(it * 16, 16)
            x0g, x0u = g_ref[pl.ds(r0, 8), :], u_ref[pl.ds(r0, 8), :]
            x1g, x1u = g_ref[pl.ds(r0 + 8, 8), :], u_ref[pl.ds(r0 + 8, 8), :]
            d1g, d1u = dgroup(gate1, up1, da1)
            d2g, d2u = dgroup(gate_blk[0:8], up_blk[0:8], da_blk[0:8])
            d2g = jnp.where(zero_ahead, 0.0, d2g)
            d2u = jnp.where(zero_ahead, 0.0, d2u)
            for half, (xa, xb, da_, db_, dc_, w_ref, o_ref) in enumerate((
                    (x0g, x1g, d0g, d1g, d2g, wg_ref, og_ref), (x0u, x1u, d0u, d1u, d2u, wu_ref, ou_ref))):
                pa, prods_a = finish(xa, da_, db_, w_ref)
                pb, prods_b = finish(xb, db_, dc_, w_ref)
                o_ref[pl.ds(r0, 16), :] = jnp.concatenate([pa, pb], axis=0).astype(BF16)
                for k in range(4):
                    accs[4 * half + k] += prods_a[k] + prods_b[k]
            return d2g, d2u, gate_blk[8:16], up_blk[8:16], da_blk[8:16]

        def rows16(ref, r):
            return ref[pl.ds(r, 16), :].astype(F32)

        def step(it, carry):
            r1 = pl.multiple_of(it * 16 + 16, 16)
            return two_groups(it, carry, rows16(cg_ref, r1), rows16(cu_ref, r1), rows16(da_ref, r1), False)

        g0, u0, da0 = rows16(cg_ref, 0), rows16(cu_ref, 0), rows16(da_ref, 0)
        d0g, d0u = dgroup(g0[0:8], u0[0:8], da0[0:8])
        carry = lax.fori_loop(0, n16 - 1, step, (d0g, d0u, g0[8:16], u0[8:16], da0[8:16]))
        two_groups(n16 - 1, carry, cgn_ref[...].astype(F32), cun_ref[...].astype(F32), dan_ref[...].astype(F32), last)

        @pl.when(first)
        def _():
            ag_ref[...] = jnp.zeros_like(ag_ref)
            au_ref[...] = jnp.zeros_like(au_ref)

        for half, a_ref in enumerate((ag_ref, au_ref)):
            for k in range(4):
                a_ref[k:k + 1, :] += jnp.sum(accs[4 * half + k], axis=0, keepdims=True)

    def nxt16(i):
        return jnp.minimum((i + 1) * n16, t // 16 - 1)

    out_blk = pl.BlockSpec((tt, tc), lambda j, i: (i, j))
    acc_spec = pl.BlockSpec((8, tc), lambda j, i: (0, j))
    nxt_blk = pl.BlockSpec((16, tc), lambda j, i: (nxt16(i), j))
    in_specs = [out_blk, pl.BlockSpec((tt, tc), lambda j, i: (i, j + nf)),
                pl.BlockSpec((FFN_CONV, tc), lambda j, i: (0, j)), pl.BlockSpec((FFN_CONV, tc), lambda j, i: (0, j + nf)),
                out_blk, nxt_blk, out_blk, nxt_blk, out_blk, nxt_blk]
    return pl.pallas_call(
        body, name="ffn_act_bwd", grid=(nf, nt), in_specs=in_specs,
        out_specs=[out_blk, out_blk, acc_spec, acc_spec],
        out_shape=[jax.ShapeDtypeStruct((t, D_FF), BF16), jax.ShapeDtypeStruct((t, D_FF), BF16),
                   jax.ShapeDtypeStruct((8, D_FF), F32), jax.ShapeDtypeStruct((8, D_FF), F32)],
        scratch_shapes=[pltpu.VMEM((8, 8, tc), F32)],
        compiler_params=_cparams(("parallel", "arbitrary")),
    )(hid, hid, cw, cw, conv_g, conv_g, conv_u, conv_u, dact, dact)


def _tri_mask():
    r = lax.broadcasted_iota(jnp.int32, (CHUNK, CHUNK), 0)
    c = lax.broadcasted_iota(jnp.int32, (CHUNK, CHUNK), 1)
    return r >= c


def _gmlp_group_fwd(uv_ref, lng_ref, lnb_ref, ws_ref, bexp_ref, tri, g, want_grad):
    lo, hi = g * 128, (g + 1) * 128
    u_pre = uv_ref[:, lo:hi]
    v_pre = uv_ref[:, D_A + lo:D_A + hi]
    u, du = _gelu_and_grad(u_pre)
    v, dv = _gelu_and_grad(v_pre)
    mu = jnp.mean(v, axis=-1, keepdims=True)
    dc = v - mu
    rs = lax.rsqrt(jnp.mean(dc * dc, axis=-1, keepdims=True) + EPS)
    xh = dc * rs
    vn = (xh * lng_ref[:, lo:hi] + lnb_ref[:, lo:hi]).astype(BF16)
    w = jnp.where(tri, ws_ref[g], 0.0).astype(BF16)
    sg = _dot(w, vn) + bexp_ref[g]
    if want_grad:
        return u, du, dv, rs, xh, vn, w, sg
    return u * sg


def _gmlp_fwd(proj, ln_g, ln_b, w_s, b_exp, na_g):
    t = proj.shape[0]
    ng = N_GROUPS_A

    def body(uv_ref, lng_ref, lnb_ref, ws_ref, bexp_ref, nag_ref, o_ref):
        tri = _tri_mask()
        ys = [_gmlp_group_fwd(uv_ref, lng_ref, lnb_ref, ws_ref, bexp_ref, tri, g, False) for g in range(ng)]
        ssq = ys[0] * 0.0
        for y in ys:
            ssq = ssq + y * y
        r = lax.rsqrt(jnp.sum(ssq, axis=-1, keepdims=True) * (1.0 / D_A) + EPS)
        for g, y in enumerate(ys):
            o_ref[:, g * 128:(g + 1) * 128] = (y * r * nag_ref[:, g * 128:(g + 1) * 128]).astype(BF16)

    vec = pl.BlockSpec((1, D_A), lambda i: (0, 0))
    cube = pl.BlockSpec((ng, CHUNK, CHUNK), lambda i: (0, 0, 0))
    return pl.pallas_call(
        body, name="gmlp_fwd", grid=(t // CHUNK,),
        in_specs=[pl.BlockSpec((CHUNK, 2 * D_A), lambda i: (i, 0)), vec, vec, cube, cube, vec],
        out_specs=pl.BlockSpec((CHUNK, D_A), lambda i: (i, 0)),
        out_shape=jax.ShapeDtypeStruct((t, D_MIX), BF16),
        compiler_params=_cparams(("parallel",)),
    )(proj, ln_g, ln_b, w_s, b_exp, na_g)


def _gmlp_bwd(proj, dyab, ln_g, ln_b, w_s, b_exp, na_g):
    t = proj.shape[0]
    ng = N_GROUPS_A
    nsteps = t // CHUNK

    def body(uv_ref, dy_ref, lng_ref, lnb_ref, ws_ref, bexp_ref, nag_ref,
             duv_ref, dws_ref, dbs_ref, dlng_ref, dlnb_ref, dnag_ref, dbacc):
        i = pl.program_id(0)
        tri = _tri_mask()

        @pl.when(i == 0)
        def _():
            dws_ref[...] = jnp.zeros_like(dws_ref)
            dbacc[...] = jnp.zeros_like(dbacc)
            dlng_ref[...] = jnp.zeros_like(dlng_ref)
            dlnb_ref[...] = jnp.zeros_like(dlnb_ref)
            dnag_ref[...] = jnp.zeros_like(dnag_ref)

        st = [_gmlp_group_fwd(uv_ref, lng_ref, lnb_ref, ws_ref, bexp_ref, tri, g, True) for g in range(ng)]
        ssq = st[0][0] * 0.0
        for s in st:
            y = s[0] * s[7]
            ssq = ssq + y * y
        r = lax.rsqrt(jnp.sum(ssq, axis=-1, keepdims=True) * (1.0 / D_A) + EPS)
        csum = st[0][0] * 0.0
        for g, s in enumerate(st):
            sl = slice(g * 128, (g + 1) * 128)
            xhy = s[0] * s[7] * r
            dya = dy_ref[:, sl].astype(F32)
            dnag_ref[:, sl] += jnp.sum(dya * xhy, axis=0, keepdims=True)
            csum = csum + dya * nag_ref[:, sl] * xhy
        c1 = jnp.sum(csum, axis=-1, keepdims=True) * (1.0 / D_A)
        for g, s in enumerate(st):
            u, du, dv, rs, xh, vn, w, sg = s
            sl = slice(g * 128, (g + 1) * 128)
            dy = r * (dy_ref[:, sl].astype(F32) * nag_ref[:, sl] - u * sg * r * c1)
            dsg = dy * u
            dsg_b = dsg.astype(BF16)
            dws_ref[g] += _dot_nt(dsg_b, vn)
            dbacc[g] += dsg
            dvn = _dot_tn(w, dsg_b)
            dlnb_ref[:, sl] += jnp.sum(dvn, axis=0, keepdims=True)
            dlng_ref[:, sl] += jnp.sum(dvn * xh, axis=0, keepdims=True)
            dxh = dvn * lng_ref[:, sl]
            dvv = rs * (dxh - jnp.mean(dxh, axis=-1, keepdims=True)
                        - xh * jnp.mean(dxh * xh, axis=-1, keepdims=True))
            duv_ref[:, sl] = (dy * sg * du).astype(BF16)
            duv_ref[:, D_A + g * 128:D_A + (g + 1) * 128] = (dvv * dv).astype(BF16)

        @pl.when(i == nsteps - 1)
        def _():
            for g in range(ng):
                dws_ref[g] = jnp.where(tri, dws_ref[g], 0.0)
                dbs_ref[g] = jnp.sum(dbacc[g], axis=-1, keepdims=True)

    vec = pl.BlockSpec((1, D_A), lambda i: (0, 0))
    cube = pl.BlockSpec((ng, CHUNK, CHUNK), lambda i: (0, 0, 0))
    return pl.pallas_call(
        body, name="gmlp_bwd", grid=(nsteps,),
        in_specs=[pl.BlockSpec((CHUNK, 2 * D_A), lambda i: (i, 0)),
                  pl.BlockSpec((CHUNK, D_A), lambda i: (i, 0)), vec, vec, cube, cube, vec],
        out_specs=[pl.BlockSpec((CHUNK, 2 * D_A), lambda i: (i, 0)), cube,
                   pl.BlockSpec((ng, CHUNK, 1), lambda i: (0, 0, 0)), vec, vec, vec],
        out_shape=[jax.ShapeDtypeStruct((t, 2 * D_A), BF16), jax.ShapeDtypeStruct((ng, CHUNK, CHUNK), F32),
                   jax.ShapeDtypeStruct((ng, CHUNK, 1), F32), jax.ShapeDtypeStruct((1, D_A), F32),
                   jax.ShapeDtypeStruct((1, D_A), F32), jax.ShapeDtypeStruct((1, D_A), F32)],
        scratch_shapes=[pltpu.VMEM((ng, CHUNK, CHUNK), F32)],
        compiler_params=_cparams(("arbitrary",)),
    )(proj, dyab, ln_g, ln_b, w_s, b_exp, na_g)


OFF_Z = 2 * D_A
OFF_XS = OFF_Z + D_SSM
OFF_B = OFF_XS + D_SSM
OFF_C = OFF_B + D_BC


def _ssd_consts():
    tri = jnp.tril(jnp.ones((CHUNK, CHUNK), F32)).astype(BF16)
    h = jnp.arange(HPAD)[None, :, None]
    g = jnp.arange(N_SSM_GROUPS)[:, None, None]
    j1 = jnp.arange(GW)[None, None, :]
    eh = (h == g * HEADS_PER_GROUP + j1 // HEAD_DIM).astype(BF16)
    j2 = jnp.arange(HEADS_PER_GROUP * 128)[None, None, :]
    e128 = (h == g * HEADS_PER_GROUP + j2 // 128).astype(BF16)
    return tri, eh, e128


def _ssd_in_specs(cmap):
    def rows(i):
        return cmap(i)

    def prev8(i):
        return jnp.maximum(cmap(i) * (CHUNK // 8) - 1, 0)

    def whole(*shape):
        return pl.BlockSpec(shape, lambda i: (0,) * len(shape))

    bcw = 2 * D_BC
    specs = [
        pl.BlockSpec((CHUNK, D_SSM), lambda i: (rows(i), OFF_Z // D_SSM)),
        pl.BlockSpec((CHUNK, D_SSM), lambda i: (rows(i), OFF_XS // D_SSM)),
        pl.BlockSpec((8, D_SSM), lambda i: (prev8(i), OFF_XS // D_SSM)),
        pl.BlockSpec((CHUNK, bcw), lambda i: (rows(i), OFF_B // bcw)),
        pl.BlockSpec((8, bcw), lambda i: (prev8(i), OFF_B // bcw)),
        pl.BlockSpec((CHUNK, HPAD), lambda i: (rows(i), 0)),
        whole(SSM_CONV, D_XBC), whole(1, D_XBC),
        whole(1, HPAD), whole(1, HPAD),
        whole(1, D_SSM), whole(1, D_SSM),
        whole(CHUNK, CHUNK),
        whole(N_SSM_GROUPS, HPAD, GW), whole(N_SSM_GROUPS, HPAD, HEADS_PER_GROUP * 128),
    ]
    return specs


N_SSD_IN = 15


def _lanes(ref, start, width):
    return ref.at[:, pl.ds(pl.multiple_of(start, 128), width)]


def _ssd_group_refs(ins, g):
    (z_ref, xs_ref, xsp_ref, bc_ref, bcp_ref, dt_ref, cw_ref, cb_ref, dtb_ref, alog_ref, de_ref, gain_ref,
     tri_ref, eh_ref, e128_ref) = ins
    ox, ob, oc = g * GW, g * D_STATE, D_BC + g * D_STATE
    return (_lanes(z_ref, ox, GW), _lanes(xs_ref, ox, GW), _lanes(xsp_ref, ox, GW),
            _lanes(bc_ref, ob, D_STATE), _lanes(bcp_ref, ob, D_STATE),
            _lanes(bc_ref, oc, D_STATE), _lanes(bcp_ref, oc, D_STATE), dt_ref,
            _lanes(cw_ref, ox, GW), _lanes(cw_ref, D_SSM + ob, D_STATE), _lanes(cw_ref, D_SSM + oc, D_STATE),
            _lanes(cb_ref, ox, GW), _lanes(cb_ref, D_SSM + ob, D_STATE), _lanes(cb_ref, D_SSM + oc, D_STATE),
            dtb_ref, alog_ref, _lanes(de_ref, ox, GW), _lanes(gain_ref, ox, GW), tri_ref,
            eh_ref.at[g], e128_ref.at[g])


N_SSD_OWN = 4


def _ssd_scratch():
    return [pltpu.VMEM((CHUNK + 8, GW), F32), pltpu.VMEM((CHUNK + 8, D_STATE), F32),
            pltpu.VMEM((CHUNK + 8, D_STATE), F32), pltpu.VMEM((CHUNK, GW), F32),
            pltpu.VMEM((HPAD, CHUNK), F32), pltpu.VMEM((CHUNK, HPAD), F32), pltpu.VMEM((CHUNK, HPAD), F32)]


def _ssd_two_sets(alloc):
    a, b = alloc[:7], alloc[7:7 + N_SSD_OWN]
    return ((a[0], a[1], a[2], a[4], a[3], a[5], a[6]), (b[0], b[1], b[2], a[4], b[3], a[5], a[6]))


def _ssd_chunk(ins, scr):
    dt_ref, dtb_ref, alog_ref, tri_ref = ins[5], ins[8], ins[9], ins[12]
    acst_sc, dt_sc, acs_sc = scr[3], scr[5], scr[6]
    dt = _softplus(dt_ref[...] + dtb_ref[...])
    acs = _e01x(tri_ref[...], dt * (-jnp.exp(alog_ref[...])))
    dt_sc[...] = dt
    acs_sc[...] = acs
    acst_sc[...] = acs.T


def _ssd_pre(first, g, refs, scr):
    (z_ref, xs_ref, xsp_ref, b_ref, bp_ref, c_ref, cp_ref, dt_ref, cwx, cwb, cwc, cbx, cbb, cbc,
     dtb_ref, alog_ref, de_ref, gain_ref, tri_ref, eh_ref, e128_ref) = refs
    ext_x, ext_b, ext_c, acst_sc, acse_sc, dt_sc, acs_sc = scr
    p = {}
    px = _conv(xs_ref, xsp_ref, cwx, cbx, ext_x, first, SSM_CONV)
    pb = _conv(b_ref, bp_ref, cwb, cbb, ext_b, first, SSM_CONV)
    pc = _conv(c_ref, cp_ref, cwc, cbc, ext_c, first, SSM_CONV)
    p['xs'], p['dsx'] = _silu_and_grad(px)
    p['bm'], p['dsb'] = _silu_and_grad(pb)
    p['cm'], p['dsc'] = _silu_and_grad(pc)
    p['dt_in'] = dt_ref[...] + dtb_ref[...]
    dt = dt_sc[...]
    p['dt'] = dt
    p['a'] = -jnp.exp(alog_ref[...])
    tri_b = tri_ref[...]
    acs = acs_sc[...]
    eh = eh_ref[...]
    p['eh'] = eh
    p['dt_e'] = _x01(dt, eh)
    acs_e = _x01(acs, eh)
    acse_sc[...] = acs_e
    p['acs_e'] = acs_e
    p['acs_c'] = _x01(acs, e128_ref[...])
    p['acs_last_e'] = acse_sc[pl.ds(CHUNK - 1, 1), :]
    p['xdt'] = p['xs'] * p['dt_e']
    p['decay_e'] = jnp.exp(p['acs_last_e'] - acs_e)
    p['cm_b'] = p['cm'].astype(BF16)
    p['bm_b'] = p['bm'].astype(BF16)
    p['scores'] = _dot_nt(p['cm_b'], p['bm_b'])
    p['tri_b'] = tri_b
    return p


def _ssd_l(p, g, r, acst_sc, tri):
    col = p['acs_c'][:, r * 128:(r + 1) * 128]
    row = acst_sc[pl.ds(g * HEADS_PER_GROUP + r, 1), :]
    return jnp.where(tri, jnp.exp(jnp.minimum(col - row, 0.0)), 0.0)


def _ssd_fwd(proj, dt_raw, yab, cw, cb, dtb, alog, de, gain, consts, hosted):
    t = proj.shape[0]
    nc = t // CHUNK
    ng = N_SSM_GROUPS
    tri_c, eh_c, e128_c = consts
    n_in = N_SSD_IN
    nh_in, nh_out = len(hosted['arrays']), len(hosted['out_shape'])
    mid_chunk = _forward_step(nc)

    def body(*refs):
        ins = refs[:n_in]
        hins = refs[n_in + 1:n_in + 1 + nh_in]
        o0 = n_in + 1 + nh_in
        yb_all, ys_all, hs_all = refs[o0:o0 + 3]
        houts = refs[o0 + 3:o0 + 3 + nh_out]
        h_sc = refs[o0 + 3 + nh_out]
        scr_a, scr_b = _ssd_two_sets(refs[o0 + 4 + nh_out:-2])
        sems = refs[-2:]
        c = pl.program_id(0)
        _host_phase(hosted, 'start', c == 0, hins, houts, sems)
        _host_phase(hosted, 'mid', c == mid_chunk, hins, houts, sems)

        @pl.when(c == 0)
        def _():
            h_sc[...] = jnp.zeros_like(h_sc)

        def group(g, scr):
            grefs = _ssd_group_refs(ins, g)
            z_ref, de_ref, gain_ref = grefs[0], grefs[16], grefs[17]
            yb_ref, ys_ref = _lanes(yb_all, g * GW, GW), _lanes(ys_all, g * GW, GW)
            slab = pl.ds(pl.multiple_of(g * D_STATE, D_STATE), D_STATE)
            p = _ssd_pre(c == 0, g, grefs, scr)
            tri = _tri_mask()
            h_in = h_sc[slab, :]
            hs_all[slab, :] = h_in
            yoff = _dot(p['cm_b'], h_in.astype(BF16)) * jnp.exp(p['acs_e'])
            states = _dot_tn(p['bm_b'], (p['xdt'] * p['decay_e']).astype(BF16))
            lane = lax.broadcasted_iota(jnp.int32, (CHUNK, 128), 1)
            slabs = []
            for r2 in range(HEADS_PER_GROUP // 2):
                xb = p['xdt'][:, r2 * 128:(r2 + 1) * 128].astype(BF16)
                ya = _dot((p['scores'] * _ssd_l(p, g, 2 * r2, scr[3], tri)).astype(BF16), xb)
                yb = _dot((p['scores'] * _ssd_l(p, g, 2 * r2 + 1, scr[3], tri)).astype(BF16), xb)
                slabs.append(jnp.where(lane < HEAD_DIM, ya, yb))
            y = jnp.concatenate(slabs, axis=1) + yoff + de_ref[...] * p['xs']
            ys_ref[...] = y
            h_sc[slab, :] = jnp.exp(p['acs_last_e']) * h_in + states
            zv = z_ref[...]
            yg = y * zv * _sigmoid(zv)
            r = lax.rsqrt(jnp.mean(yg * yg, axis=-1, keepdims=True) + EPS)
            yb_ref[...] = (yg * r * gain_ref[...]).astype(BF16)

        def pair(j, carry):
            group(2 * j, scr_a)
            group(2 * j + 1, scr_b)
            return carry

        _ssd_chunk(ins, scr_a)
        lax.fori_loop(0, ng // 2, pair, 0)
        _host_phase(hosted, 'finish', c == nc - 1, hins, houts, sems)

    h_in, h_out, h_shape, h_scratch, h_alias = _host_plumbing(hosted, n_in + 1, 3)
    in_specs = _ssd_in_specs(lambda i: i) + [pl.BlockSpec(memory_space=pl.ANY)] + h_in
    out_specs = [pl.BlockSpec((CHUNK, D_SSM), lambda i: (i, D_A // D_SSM)),
                 pl.BlockSpec((CHUNK, D_SSM), lambda i: (i, 0)),
                 pl.BlockSpec((None, ng * D_STATE, GW), lambda i: (i, 0, 0))] + h_out
    out_shape = [jax.ShapeDtypeStruct((t, D_MIX), BF16), jax.ShapeDtypeStruct((t, D_SSM), F32),
                 jax.ShapeDtypeStruct((nc, ng * D_STATE, GW), F32)] + h_shape
    return pl.pallas_call(
        body, name="ssd_fwd", grid=(nc,), in_specs=in_specs, out_specs=out_specs, out_shape=out_shape,
        scratch_shapes=[pltpu.VMEM((ng * D_STATE, GW), F32)] + _ssd_scratch() + _ssd_scratch()[:N_SSD_OWN]
        + h_scratch,
        input_output_aliases={n_in: 0, **h_alias},
        compiler_params=_cparams(("arbitrary",)),
    )(proj, proj, proj, proj, proj, dt_raw, cw, cb, dtb, alog, de, gain,
      tri_c, eh_c, e128_c, yab, *hosted['arrays'])


def _rows8(vals, width):
    rid = lax.broadcasted_iota(jnp.int32, (8, width), 0)
    out = jnp.zeros((8, width), F32)
    for k, v in enumerate(vals):
        if v is not None:
            out = out + jnp.where(rid == k, v, 0.0)
    return out


def _ssd_bwd(proj, dt_raw, dyab, ysave, hs, duv, cw, cb, dtb, alog, de, gain, consts, hosted):
    t = proj.shape[0]
    nc = t // CHUNK
    ng = N_SSM_GROUPS
    tri_c, eh_c, e128_c = consts
    n_in = N_SSD_IN
    nh_in, nh_out = len(hosted['arrays']), len(hosted['out_shape'])

    def per_group(g, c, ins, dy_ref, ys_ref, hs_ref, outs, scratch):
        dz_ref, dxs_ref, db_ref, dc_ref, ddt_ref, acc_x, acc_b, acc_c, acc_gain, acc_head = outs
        dh_sc, car_x, car_b, car_c, dext_x, dext_b, dext_c = scratch[:7]
        scr = scratch[7:]
        ext_x, ext_b, ext_c, acst_sc = scr[:4]
        z_ref, cwx, cwb, cwc, de_ref, gain_ref = ins[0], ins[8], ins[9], ins[10], ins[16], ins[17]
        slab = pl.ds(pl.multiple_of(g * D_STATE, D_STATE), D_STATE)
        p = _ssd_pre(c == 0, g, ins, scr)
        tri = _tri_mask()
        xs, dt_e, acs_e, xdt, decay_e = p['xs'], p['dt_e'], p['acs_e'], p['xdt'], p['decay_e']
        cm_b, bm_b, scores, eh = p['cm_b'], p['bm_b'], p['scores'], p['eh']
        h_in = hs_ref[...]
        h_in_b = h_in.astype(BF16)
        e_a = jnp.exp(acs_e)
        raw = _dot(cm_b, h_in_b)

        y = ys_ref[...]
        zv = z_ref[...]
        sz, dsz = _silu_and_grad(zv)
        yg = y * sz
        r = lax.rsqrt(jnp.mean(yg * yg, axis=-1, keepdims=True) + EPS)
        xh = yg * r
        dout = dy_ref[...].astype(F32)
        gain = gain_ref[...]
        dxh = dout * gain
        dyg = r * (dxh - xh * jnp.mean(dxh * xh, axis=-1, keepdims=True))
        dy = dyg * sz
        dz_ref[...] = (dyg * y * dsz).astype(BF16)
        acc_gain[g] += _rows8([jnp.sum(dout * xh, axis=0, keepdims=True)], GW)
        d_skip8 = _x01_nt(_rows8([None, None, jnp.sum(dy * xs, axis=0, keepdims=True)], GW), eh)
        dxs = de_ref[...] * dy

        q = dy * raw * e_a
        draw = (dy * e_a).astype(BF16)
        d_c = _dot_nt(draw, h_in_b)
        dh_in = _dot_tn(cm_b, draw)

        lane = lax.broadcasted_iota(jnp.int32, (CHUNK, 128), 1)
        ones_b = jnp.ones((CHUNK, 128), BF16)
        dscores = jnp.zeros((CHUNK, CHUNK), F32)
        dxdt_slabs, q_slabs = [], []
        for r2 in range(HEADS_PER_GROUP // 2):
            sl = slice(r2 * 128, (r2 + 1) * 128)
            xb = xdt[:, sl].astype(BF16)
            dys = dy[:, sl]
            dys_b = dys.astype(BF16)
            dxh_pair, qv_pair = [], []
            for half in range(2):
                lmat = _ssd_l(p, g, 2 * r2 + half, acst_sc, tri)
                m = scores * lmat
                mine = (lane < HEAD_DIM) if half == 0 else (lane >= HEAD_DIM)
                dm = _dot_nt(jnp.where(mine, dys, 0.0).astype(BF16), xb)
                dscores = dscores + dm * lmat
                gm = dm * m
                dxh_pair.append(_dot_tn(m.astype(BF16), dys_b))
                h3 = _split3(gm)
                colsum = _dot_tn(h3[0], ones_b) + _dot_tn(h3[1], ones_b) + _dot_tn(h3[2], ones_b)
                qv_pair.append(jnp.sum(gm, axis=-1, keepdims=True) - colsum)
            dxdt_slabs.append(jnp.where(lane < HEAD_DIM, dxh_pair[0], dxh_pair[1]))
            q_slabs.append(jnp.where(lane == 0, qv_pair[0], 0.0) + jnp.where(lane == HEAD_DIM, qv_pair[1], 0.0))
        dxdt = jnp.concatenate(dxdt_slabs, axis=1)
        q = q + jnp.concatenate(q_slabs, axis=1)

        dh_out = dh_sc[slab, :]
        dh_out_b = dh_out.astype(BF16)
        e_l = jnp.exp(p['acs_last_e'])
        dh_sc[slab, :] = dh_in + e_l * dh_out
        dlast = jnp.sum(dh_out * h_in, axis=0, keepdims=True) * e_l
        dxd = _dot(bm_b, dh_out_b)
        xd = xdt * decay_e
        dxdt = dxdt + dxd * decay_e
        dd = dxd * xd
        q = q - dd
        dlast = dlast + jnp.sum(dd, axis=0, keepdims=True)
        d_b = _dot_nt(xd.astype(BF16), dh_out_b)
        dsc_b = dscores.astype(BF16)
        d_c = d_c + _dot(dsc_b, bm_b)
        d_b = d_b + _dot_tn(dsc_b, cm_b)

        dxs = dxs + dxdt * dt_e
        rid = lax.broadcasted_iota(jnp.int32, (CHUNK, GW), 0)
        q = q + jnp.where(rid == CHUNK - 1, dlast, 0.0)
        dacs = _x01_nt(q, eh)
        ddt = _x01_nt(dxdt * xs, eh)
        dadt = _e01x_tn(p['tri_b'], dacs)
        ddt = ddt + dadt * p['a']
        d_a = jnp.sum(dadt * p['dt'], axis=0, keepdims=True)
        ddt_raw = ddt * _sigmoid(p['dt_in'])
        acc_head[...] += _rows8([jnp.sum(ddt_raw, axis=0, keepdims=True), d_a * p['a']], HPAD) + d_skip8

        @pl.when(g == 0)
        def _():
            ddt_ref[...] = ddt_raw

        @pl.when(g > 0)
        def _():
            ddt_ref[...] += ddt_raw

        for dv, dsil, ext, dext, car, acc, w_ref, o_ref in (
                (dxs, p['dsx'], ext_x, dext_x, car_x, acc_x, cwx, dxs_ref),
                (d_b, p['dsb'], ext_b, dext_b, car_b, acc_b, cwb, db_ref),
                (d_c, p['dsc'], ext_c, dext_c, car_c, acc_c, cwc, dc_ref)):
            dp = dv * dsil
            width = dp.shape[1]
            rows = [jnp.sum(ext[pl.ds(5 + k, CHUNK), :] * dp, axis=0, keepdims=True) for k in range(SSM_CONV)]
            rows.append(jnp.sum(dp, axis=0, keepdims=True))
            acc[g] += _rows8(rows, width)
            dext[0:CHUNK, :] = dp
            dext[CHUNK:CHUNK + 8, :] = car[g]
            car[g] = dext[0:8, :]
            dx = w_ref[SSM_CONV - 1:SSM_CONV, :] * dext[pl.ds(0, CHUNK), :]
            for k in range(SSM_CONV - 1):
                dx = dx + w_ref[k:k + 1, :] * dext[pl.ds(SSM_CONV - 1 - k, CHUNK), :]
            o_ref[...] = dx.astype(BF16)

    def body(*refs):
        ins = refs[:n_in]
        dy_all, ys_all, hs_all, duv_ref = refs[n_in:n_in + 4]
        hins = refs[n_in + 4:n_in + 4 + nh_in]
        o0 = n_in + 4 + nh_in
        dproj_ref, ddt_ref = refs[o0:o0 + 2]
        accs = refs[o0 + 2:o0 + 7]
        houts = refs[o0 + 7:o0 + 7 + nh_out]
        scratch = refs[o0 + 7 + nh_out:-2]
        sems = refs[-2:]
        cc = pl.program_id(0)
        _host_phase(hosted, 'start', cc == 0, hins, houts, sems)
        dproj_ref[:, 0:2 * D_A] = duv_ref[...]

        @pl.when(cc == 0)
        def _():
            for a in tuple(accs) + tuple(scratch[:4]):
                a[...] = jnp.zeros_like(a)

        shared = tuple(scratch[:4])
        scr_a, scr_b = _ssd_two_sets(tuple(scratch[7:14]) + tuple(scratch[17:17 + N_SSD_OWN]))
        set_a = shared + tuple(scratch[4:7]) + scr_a
        set_b = shared + tuple(scratch[14:17]) + scr_b

        def group(g, own):
            slab = pl.ds(pl.multiple_of(g * D_STATE, D_STATE), D_STATE)
            outs = (_lanes(dproj_ref, OFF_Z + g * GW, GW), _lanes(dproj_ref, OFF_XS + g * GW, GW),
                    _lanes(dproj_ref, OFF_B + g * D_STATE, D_STATE), _lanes(dproj_ref, OFF_C + g * D_STATE, D_STATE),
                    ddt_ref) + tuple(accs)
            per_group(g, nc - 1 - cc, _ssd_group_refs(ins, g), _lanes(dy_all, g * GW, GW),
                      _lanes(ys_all, g * GW, GW), hs_all.at[slab, :], outs, own)

        def pair(j, carry):
            group(2 * j, set_a)
            group(2 * j + 1, set_b)
            return carry

        _ssd_chunk(ins, scr_a)
        lax.fori_loop(0, ng // 2, pair, 0)
        _host_phase(hosted, 'finish', cc == nc - 1, hins, houts, sems)

    def cmap(i):
        return nc - 1 - i

    in_specs = _ssd_in_specs(cmap) + [
        pl.BlockSpec((CHUNK, D_SSM), lambda i: (cmap(i), D_A // D_SSM)),
        pl.BlockSpec((CHUNK, D_SSM), lambda i: (cmap(i), 0)),
        pl.BlockSpec((None, ng * D_STATE, GW), lambda i: (cmap(i), 0, 0)),
        pl.BlockSpec((CHUNK, 2 * D_A), lambda i: (cmap(i), 0)),
    ]

    def full(shape):
        return pl.BlockSpec(shape, lambda i: (0,) * len(shape))

    out_specs = [
        pl.BlockSpec((CHUNK, D_MAIN), lambda i: (cmap(i), 0)),
        pl.BlockSpec((CHUNK, HPAD), lambda i: (cmap(i), 0)),
        full((ng, 8, GW)), full((ng, 8, D_STATE)), full((ng, 8, D_STATE)),
        full((ng, 8, GW)), full((8, HPAD)),
    ]
    out_shape = [
        jax.ShapeDtypeStruct((t, D_MAIN), BF16),
        jax.ShapeDtypeStruct((t, HPAD), F32),
        jax.ShapeDtypeStruct((ng, 8, GW), F32), jax.ShapeDtypeStruct((ng, 8, D_STATE), F32),
        jax.ShapeDtypeStruct((ng, 8, D_STATE), F32), jax.ShapeDtypeStruct((ng, 8, GW), F32),
        jax.ShapeDtypeStruct((8, HPAD), F32),
    ]
    scratch = [pltpu.VMEM((ng * D_STATE, GW), F32),
               pltpu.VMEM((ng, 8, GW), F32), pltpu.VMEM((ng, 8, D_STATE), F32), pltpu.VMEM((ng, 8, D_STATE), F32),
               pltpu.VMEM((CHUNK + 8, GW), F32), pltpu.VMEM((CHUNK + 8, D_STATE), F32),
               pltpu.VMEM((CHUNK + 8, D_STATE), F32)] + _ssd_scratch()
    scratch += scratch[4:7] + _ssd_scratch()[:N_SSD_OWN]
    h_in, h_out, h_shape, h_scratch, h_alias = _host_plumbing(hosted, n_in + 4, len(out_shape))
    return pl.pallas_call(
        body, name="ssd_bwd", grid=(nc,), in_specs=in_specs + h_in, out_specs=out_specs + h_out,
        out_shape=out_shape + h_shape, scratch_shapes=scratch + h_scratch, input_output_aliases=h_alias,
        compiler_params=_cparams(("arbitrary",)),
    )(proj, proj, proj, proj, proj, dt_raw, cw, cb, dtb, alog, de, gain,
      tri_c, eh_c, e128_c, dyab, ysave, hs, duv, *hosted['arrays'])


ANY = pl.BlockSpec(memory_space=pl.ANY)


def _place():
    x, y, c = lax.axis_index("x"), lax.axis_index("y"), lax.axis_index("c")
    chips = [(1 - x, y), (x, 1 - y), (1 - x, 1 - y)]
    return x, y, c, chips


def _rcopy(src, dst, send_sems, recv_sems, k, dev):
    return pltpu.make_async_remote_copy(src_ref=src, dst_ref=dst, send_sem=send_sems.at[k],
                                        recv_sem=recv_sems.at[k], device_id=dev, device_id_type=MESH)


def _my_chip():
    return 2 * lax.axis_index("x") + lax.axis_index("y")


def _cast_into_slot(w, name):
    r, c = w.shape
    tr = _row_tile(r, c, 2)

    def body(w_ref, o_ref):
        o_ref[...] = w_ref[...].astype(BF16)

    return pl.pallas_call(
        body, name=name, grid=(r // tr,), in_specs=[pl.BlockSpec((tr, c), lambda i: (i, 0))],
        out_specs=pl.BlockSpec((None, tr, c), lambda i: (_my_chip(), i, 0)),
        out_shape=jax.ShapeDtypeStruct((N_CHIPS, r, c), BF16), compiler_params=_cparams(("parallel",)),
    )(w)


def _hosted_gather(bigs):
    nb = len(bigs)

    def rows(a, c):
        half = bigs[a].shape[1] // 2
        return pl.ds(c * half, half)

    def start(ins, outs, send_sems, recv_sems):
        x, y, c, chips = _place()
        q = 2 * x + y
        for a in range(nb):
            for k, chip in enumerate(chips):
                _rcopy(outs[a].at[q, rows(a, c)], outs[a].at[q, rows(a, c)], send_sems, recv_sems, 6 * a + k,
                       (chip[0], chip[1], c)).start()

    def mid(ins, outs, send_sems, recv_sems):
        x, y, c, chips = _place()
        sib = (x, y, 1 - c)
        for a in range(nb):
            for k, chip in enumerate(chips):
                slab = outs[a].at[2 * chip[0] + chip[1], rows(a, c)]
                _rcopy(slab, slab, send_sems, recv_sems, 6 * a + k, sib).wait_recv()
                _rcopy(slab, slab, send_sems, recv_sems, 6 * a + 3 + k, sib).start()

    def finish(ins, outs, send_sems, recv_sems):
        x, y, c, chips = _place()
        q = 2 * x + y
        sib = (x, y, 1 - c)
        for a in range(nb):
            for k, chip in enumerate(chips):
                qk = 2 * chip[0] + chip[1]
                other = outs[a].at[qk, rows(a, 1 - c)]
                _rcopy(other, other, send_sems, recv_sems, 6 * a + 3 + k, sib).wait_recv()
                mine = outs[a].at[q, rows(a, c)]
                _rcopy(mine, mine, send_sems, recv_sems, 6 * a + k, sib).wait_send()
                fwd = outs[a].at[qk, rows(a, c)]
                _rcopy(fwd, fwd, send_sems, recv_sems, 6 * a + 3 + k, sib).wait_send()

    return dict(arrays=list(bigs), out_shape=[jax.ShapeDtypeStruct(b.shape, b.dtype) for b in bigs],
                aliases={a: a for a in range(nb)}, nsem=6 * nb, start=start, mid=mid, finish=finish)


def _hosted_rs_chips(ps):
    na = len(ps)

    def copies(ins, outs, send_sems, recv_sems):
        x, y, c, chips = _place()
        return [_rcopy(ins[a].at[2 * chip[0] + chip[1]], outs[a].at[k], send_sems, recv_sems, 3 * a + k,
                       (chip[0], chip[1], c)) for a in range(na) for k, chip in enumerate(chips)]

    def start(ins, outs, send_sems, recv_sems):
        for cp in copies(ins, outs, send_sems, recv_sems):
            cp.start()

    def finish(ins, outs, send_sems, recv_sems):
        for cp in copies(ins, outs, send_sems, recv_sems):
            cp.wait()

    return dict(arrays=list(ps), out_shape=[jax.ShapeDtypeStruct((3,) + p.shape[1:], p.dtype) for p in ps],
                aliases={}, nsem=3 * na, start=start, mid=None, finish=finish)


def _hosted_rs_sibling(gs):
    na = len(gs)

    def copies(ins, outs, send_sems, recv_sems):
        x, y, c, _ = _place()
        halves = [g.shape[1] // 2 for g in gs]
        return [_rcopy(ins[a].at[:, pl.ds((1 - c) * halves[a], halves[a]), :], outs[a], send_sems, recv_sems, a,
                       (x, y, 1 - c)) for a in range(na)]

    def start(ins, outs, send_sems, recv_sems):
        for cp in copies(ins, outs, send_sems, recv_sems):
            cp.start()

    def finish(ins, outs, send_sems, recv_sems):
        for cp in copies(ins, outs, send_sems, recv_sems):
            cp.wait()

    return dict(arrays=list(gs), aliases={}, nsem=na, start=start, mid=None, finish=finish,
                out_shape=[jax.ShapeDtypeStruct((N_CHIPS, g.shape[1] // 2, g.shape[2]), g.dtype) for g in gs])


def _hosted_share_sibling(fs):
    na = len(fs)

    def rows(a, c):
        half = fs[a].shape[0] // 2
        return pl.ds(c * half, half)

    def start(ins, outs, send_sems, recv_sems):
        x, y, c, _ = _place()
        for a in range(na):
            _rcopy(outs[a].at[rows(a, c)], outs[a].at[rows(a, c)], send_sems, recv_sems, a, (x, y, 1 - c)).start()

    def finish(ins, outs, send_sems, recv_sems):
        x, y, c, _ = _place()
        for a in range(na):
            _rcopy(outs[a].at[rows(a, c)], outs[a].at[rows(a, c)], send_sems, recv_sems, a, (x, y, 1 - c)).wait_send()
            other = outs[a].at[rows(a, 1 - c)]
            _rcopy(other, other, send_sems, recv_sems, a, (x, y, 1 - c)).wait_recv()

    return dict(arrays=list(fs), out_shape=[jax.ShapeDtypeStruct(f.shape, f.dtype) for f in fs],
                aliases={a: a for a in range(na)}, nsem=na, start=start, mid=None, finish=finish)


def _hosted_allgather(buf):
    def copies(ins, outs, send_sems, recv_sems):
        x, y, c, _ = _place()
        me = 4 * x + 2 * y + c
        cps = []
        for k in range(1, 8):
            dev = (1 - x if k & 4 else x, 1 - y if k & 2 else y, 1 - c if k & 1 else c)
            cps.append(_rcopy(ins[0], outs[0].at[me], send_sems, recv_sems, k - 1, dev))
        return pltpu.make_async_copy(ins[0], outs[0].at[me], send_sems.at[7]), cps

    def start(ins, outs, send_sems, recv_sems):
        loc, cps = copies(ins, outs, send_sems, recv_sems)
        loc.start()
        for cp in cps:
            cp.start()

    def finish(ins, outs, send_sems, recv_sems):
        loc, cps = copies(ins, outs, send_sems, recv_sems)
        loc.wait()
        for cp in cps:
            cp.wait()

    return dict(arrays=[buf], out_shape=[jax.ShapeDtypeStruct((8,) + buf.shape, buf.dtype)], aliases={},
                nsem=8, start=start, mid=None, finish=finish)


class _SemWindow:
    def __init__(self, sems, off):
        self._sems, self._off = sems, off

    @property
    def at(self):
        return self

    def __getitem__(self, k):
        return self._sems.at[k + self._off]


def _hosted_join(parts):
    arrays, out_shape, aliases, spans, nsem = [], [], {}, [], 0
    for h in parts:
        spans.append((len(arrays), len(h['arrays']), len(out_shape), len(h['out_shape']), nsem))
        aliases.update({len(arrays) + a: len(out_shape) + b for a, b in h['aliases'].items()})
        arrays += h['arrays']
        out_shape += h['out_shape']
        nsem += h['nsem']

    def phase(name):
        if all(h[name] is None for h in parts):
            return None

        def run(ins, outs, send_sems, recv_sems):
            for h, (ia, na, io, no, s0) in zip(parts, spans):
                if h[name] is not None:
                    h[name](ins[ia:ia + na], outs[io:io + no], _SemWindow(send_sems, s0), _SemWindow(recv_sems, s0))

        return run

    return dict(arrays=arrays, out_shape=out_shape, aliases=aliases, nsem=nsem,
                start=phase('start'), mid=phase('mid'), finish=phase('finish'))


def _run_hosted(hosted, name):
    nh_in, nh_out = len(hosted['arrays']), len(hosted['out_shape'])

    def body(*refs):
        ins, outs, sems = refs[:nh_in], refs[nh_in:nh_in + nh_out], refs[-2:]
        for ph in ('start', 'mid', 'finish'):
            if hosted[ph] is not None:
                hosted[ph](ins, outs, sems[0], sems[1])

    h_in, h_out, h_shape, h_scratch, h_alias = _host_plumbing(hosted, 0, 0)
    return pl.pallas_call(body, name=name, in_specs=h_in, out_specs=h_out, out_shape=h_shape,
                          scratch_shapes=h_scratch, input_output_aliases=h_alias)(*hosted['arrays'])


def _host_plumbing(hosted, n_in, n_out):
    nh = len(hosted['arrays'])
    return ([ANY] * nh, [ANY] * len(hosted['out_shape']), list(hosted['out_shape']),
            [pltpu.SemaphoreType.DMA((hosted['nsem'],)), pltpu.SemaphoreType.DMA((hosted['nsem'],))],
            {n_in + a: n_out + b for a, b in hosted['aliases'].items()})


def _host_phase(hosted, phase, when, hins, houts, sems):
    fn = hosted[phase]
    if fn is None:
        return

    @pl.when(when)
    def _():
        fn(hins, houts, sems[0], sems[1])


def _gather_weights(bigs, smalls):
    nb, ns = len(bigs), len(smalls)
    na = nb + ns
    nsem = 6 * nb + 3 * ns
    big = _hosted_gather(bigs)

    def body(*refs):
        ins, outs = refs[:na], refs[na:2 * na]
        send_sems, recv_sems, loc_sems = refs[2 * na:]
        x, y, c, chips = _place()
        q = 2 * x + y
        sib = (x, y, 1 - c)
        locs, sends = [], []
        for s in range(ns):
            cp = pltpu.make_async_copy(ins[nb + s], outs[nb + s].at[q], loc_sems.at[s])
            cp.start()
            locs.append(cp)
        big['start'](ins[:nb], outs[:nb], send_sems, recv_sems)
        for s in range(ns):
            a = nb + s
            for k, chip in enumerate(chips):
                cp = _rcopy(ins[a], outs[a].at[q], send_sems, recv_sems, 6 * nb + 3 * s + k,
                            (chip[0], chip[1], c))
                cp.start()
                sends.append(cp)
        big['mid'](ins[:nb], outs[:nb], send_sems, recv_sems)
        big['finish'](ins[:nb], outs[:nb], send_sems, recv_sems)
        for s in range(ns):
            a = nb + s
            for k, chip in enumerate(chips):
                qk = 2 * chip[0] + chip[1]
                _rcopy(ins[a], outs[a].at[qk], send_sems, recv_sems, 6 * nb + 3 * s + k, sib).wait_recv()
        for cp in sends:
            cp.wait_send()
        for cp in locs:
            cp.wait()

    arrs = list(bigs) + list(smalls)
    out_shape = ([jax.ShapeDtypeStruct(a.shape, a.dtype) for a in bigs]
                 + [jax.ShapeDtypeStruct((N_CHIPS,) + a.shape, a.dtype) for a in smalls])
    return pl.pallas_call(
        body, name="gather_weights", in_specs=[ANY] * na, out_specs=[ANY] * na, out_shape=out_shape,
        input_output_aliases={a: a for a in range(nb)},
        scratch_shapes=[pltpu.SemaphoreType.DMA((nsem,)), pltpu.SemaphoreType.DMA((nsem,)),
                        pltpu.SemaphoreType.DMA((max(ns, 1),))],
    )(*arrs)


EW_VMEM_BUDGET = 24 * 1024 * 1024


def _row_tile(rows, cols, nbuf):
    budget = EW_VMEM_BUDGET // (nbuf * cols * 4 * 2)
    return _tile(rows, max(16, budget - budget % 16), 16) if rows % 16 == 0 else rows


def _add_pairs(g, rcv, name):
    s, half, c = rcv.shape
    tr = _row_tile(half, c, 3)
    nh = half // tr

    def body(a_ref, b_ref, o_ref):
        o_ref[...] = (a_ref[...].astype(F32) + b_ref[...].astype(F32)).astype(o_ref.dtype)

    blk = pl.BlockSpec((None, tr, c), lambda j, i: (j, i, 0))
    mine = pl.BlockSpec((None, tr, c), lambda j, i: (j, lax.axis_index("c") * nh + i, 0))
    return pl.pallas_call(
        body, name=name, grid=(s, nh), in_specs=[mine, blk], out_specs=blk,
        out_shape=jax.ShapeDtypeStruct(rcv.shape, rcv.dtype), compiler_params=_cparams(("parallel", "parallel")),
    )(g, rcv)


def _sum_chips(part, rcv, name):
    _, half, c = part.shape
    tr = _row_tile(half, c, 5)
    nh = half // tr

    def body(o_ref, r_ref, out_ref):
        acc = o_ref[...].astype(F32)
        for k in range(3):
            acc = acc + r_ref[k].astype(F32)
        out_ref[...] = acc

    return pl.pallas_call(
        body, name=name, grid=(nh,),
        in_specs=[pl.BlockSpec((None, tr, c), lambda i: (_my_chip(), i, 0)),
                  pl.BlockSpec((3, tr, c), lambda i: (0, i, 0))],
        out_specs=pl.BlockSpec((tr, c), lambda i: (lax.axis_index("c") * nh + i, 0)),
        out_shape=jax.ShapeDtypeStruct((2 * half, c), F32), compiler_params=_cparams(("parallel",)),
    )(part, rcv)


def _sum_devices(parts, name):
    _, n, _ = parts.shape
    tr = n if n <= 4096 else _tile(n, 512, 8)

    def body(p_ref, o_ref):
        acc = p_ref[0]
        for k in range(1, 8):
            acc = acc + p_ref[k]
        o_ref[...] = acc

    return pl.pallas_call(
        body, name=name, grid=(n // tr,),
        in_specs=[pl.BlockSpec((8, tr, 128), lambda i: (0, i, 0))],
        out_specs=pl.BlockSpec((tr, 128), lambda i: (i, 0)),
        out_shape=jax.ShapeDtypeStruct((n, 128), F32), compiler_params=_cparams(("parallel",)),
    )(parts)


def _adamw(w, g, m, v, name):
    r, c = w.shape
    tr = _row_tile(r, c, 7)
    c1 = 1.0 - ADAM_B1 ** ADAM_STEP
    c2 = 1.0 - ADAM_B2 ** ADAM_STEP

    def body(w_ref, g_ref, m_ref, v_ref, d_ref, mo_ref, vo_ref):
        gv = g_ref[...]
        mn = ADAM_B1 * m_ref[...] + (1.0 - ADAM_B1) * gv
        vn = ADAM_B2 * v_ref[...] + (1.0 - ADAM_B2) * (gv * gv)
        mo_ref[...] = mn
        vo_ref[...] = vn
        m_hat = mn / c1
        v_hat = vn / c2
        d_ref[...] = -ADAM_LR * (m_hat / (jnp.sqrt(v_hat) + ADAM_EPS) + ADAM_WD * w_ref[...])

    blk = pl.BlockSpec((tr, c), lambda i: (i, 0))
    sh = jax.ShapeDtypeStruct((r, c), F32)
    return pl.pallas_call(
        body, name=name, grid=(r // tr,), in_specs=[blk] * 4, out_specs=[blk] * 3, out_shape=[sh] * 3,
        compiler_params=_cparams(("parallel",)),
    )(w, g, m, v)


WEIGHTS = ['norm_mix_g', 'w_in', 'ln_a_g', 'ln_a_b', 'w_s', 'b_s', 'norm_a_g', 'conv_ssm_w', 'conv_ssm_b',
           'dt_bias', 'a_log', 'd_skip', 'ssm_norm_g', 'w_out', 'norm_ffn_g', 'w_up', 'conv_ffn_w',
           'conv_ffn_b', 'w_down', 'norm_ple_g', 'w_ple_gate', 'w_ple', 'norm_final_g']
BIG = ['w_in', 'w_out', 'w_up', 'w_down', 'w_ple_gate', 'w_ple']
SMALL = [n for n in WEIGHTS if n not in BIG]
PACK_ALIGN = 2048


def _pack(arrs):
    parts = []
    for a in arrs:
        f = a.reshape(-1).astype(F32)
        parts.append(jnp.pad(f, (0, (-f.shape[0]) % PACK_ALIGN)))
    return jnp.concatenate(parts).reshape(-1, 128)


def _unpack(buf, shapes):
    flat = buf.reshape(-1)
    out, off = [], 0
    for s in shapes:
        n = math.prod(s)
        out.append(flat[off:off + n].reshape(s))
        off += n + (-n) % PACK_ALIGN
    return out


def _pad_heads(v):
    return jnp.pad(v, ((0, 0), (0, HPAD - v.shape[1])))


def _col_sharded(full):
    r, c4 = full.shape
    return jnp.transpose(full.reshape(r, N_CHIPS, c4 // N_CHIPS), (1, 0, 2))


def _from_col_sharded(g):
    s, r, c = g.shape
    return jnp.transpose(g, (1, 0, 2)).reshape(r, s * c)


def kernel(x, p, norm_mix_g, w_in, ln_a_g, ln_a_b, w_s, b_s, norm_a_g, conv_ssm_w, conv_ssm_b, dt_bias, a_log, d_skip, ssm_norm_g, w_out, norm_ffn_g, w_up, conv_ffn_w, conv_ffn_b, w_down, norm_ple_g, w_ple_gate, w_ple, norm_final_g, loss_target, m_norm_mix_g, m_w_in, m_ln_a_g, m_ln_a_b, m_w_s, m_b_s, m_norm_a_g, m_conv_ssm_w, m_conv_ssm_b, m_dt_bias, m_a_log, m_d_skip, m_ssm_norm_g, m_w_out, m_norm_ffn_g, m_w_up, m_conv_ffn_w, m_conv_ffn_b, m_w_down, m_norm_ple_g, m_w_ple_gate, m_w_ple, m_norm_final_g, v_norm_mix_g, v_w_in, v_ln_a_g, v_ln_a_b, v_w_s, v_b_s, v_norm_a_g, v_conv_ssm_w, v_conv_ssm_b, v_dt_bias, v_a_log, v_d_skip, v_ssm_norm_g, v_w_out, v_norm_ffn_g, v_w_up, v_conv_ffn_w, v_conv_ffn_b, v_w_down, v_norm_ple_g, v_w_ple_gate, v_w_ple, v_norm_final_g):
    given = dict(locals())
    wts = {n: given[n] for n in WEIGHTS}
    mom = {n: given['m_' + n] for n in WEIGHTS}
    var = {n: given['v_' + n] for n in WEIGHTS}
    d = D_MODEL
    xt, pt, tgt = x[0], p[0, 0], loss_target[0]
    chip = 2 * lax.axis_index("x") + lax.axis_index("y")

    slots = {n: _cast_into_slot(wts[n][0], "cast_" + n) for n in BIG}
    g_in, g_cs, g_cf = _gather_weights([slots['w_in']], [conv_ssm_w[0], conv_ffn_w[0]])
    w_in_full = _from_col_sharded(g_in)
    w_main = w_in_full
    w_dt = _pad_heads(w_in_full[:, D_MAIN:])
    cs_w = _from_col_sharded(g_cs)
    cf_w = _from_col_sharded(g_cf)
    consts = _ssd_consts()
    dtb, alog = _pad_heads(dt_bias), _pad_heads(a_log)
    de = jnp.repeat(d_skip[0], HEAD_DIM)[None, :]
    b_exp = jnp.broadcast_to(b_s[0][:, :, None], (N_GROUPS_A, CHUNK, CHUNK))

    a1 = _rms_fwd(xt, norm_mix_g, "rms_mix")
    proj, g_out = _matmul(a1, w_main, mode='nn', name="mm_proj", tm=1024, tn=1024, b_cols=D_MAIN,
                          hosted=_hosted_gather([slots['w_out']]))
    dt_raw = _matmul(a1, w_dt, mode='nn', name="mm_dt", tm=1024, tn=128)
    yab = _gmlp_fwd(proj, ln_a_g, ln_a_b, w_s[0], b_exp, norm_a_g)
    yab, ysave, hs, g_up = _ssd_fwd(proj, dt_raw, yab, cs_w, conv_ssm_b, dtb, alog, de, ssm_norm_g, consts,
                                    _hosted_gather([slots['w_up']]))
    w_out_f = g_out.reshape(D_MIX, d)
    h1 = _matmul(yab, w_out_f, mode='nn', name="mm_out", res=xt, tm=512, tn=1024, tk=4096)
    f = _rms_fwd(h1, norm_ffn_g, "rms_ffn")
    hid, g_down, g_pg, g_ple = _matmul(
        f, g_up, mode='nn', name="mm_up", b_sharded=True, tm=1024, tn=1408,
        hosted=_hosted_gather([slots[n] for n in ('w_down', 'w_ple_gate', 'w_ple')]))
    w_down_f = g_down.reshape(D_FF, d)
    w_pg_f = g_pg.reshape(d, d)
    act, conv_g, conv_u = _ffn_act_fwd(hid, cf_w, conv_ffn_b)
    h2 = _matmul(act, w_down_f, mode='nn', name="mm_down", res=h1, tm=1024, tn=1024, tk=2816)
    n3 = _rms_fwd(h2, norm_ple_g, "rms_ple")
    gl = _matmul(n3, w_pg_f, mode='nn', name="mm_pg", tm=1024, tn=1024)
    pe = _matmul(pt, g_ple, mode='nn', name="mm_ple", b_sharded=True, tm=1024, tn=512)
    dh3, dgl, dpe, lossv, dgf = _tail(h2, gl, pe, tgt, norm_final_g[None, :])

    gs_ple = _matmul(pt, dpe, mode='tn', name="mm_dw_ple", out_dtype=BF16, out_shards=N_CHIPS,
                     tm=256, tn=512, tk=2048)
    gs_pg = _matmul(n3, dgl, mode='tn', name="mm_dw_pg", out_dtype=BF16, tm=1024, tn=1024, tk=4096)
    dn3 = _matmul(dgl, w_pg_f, mode='nt', name="mm_dn3", out_dtype=BF16, tm=1024, tn=1024)
    dh2, dg_ple, dh2_b = _rms_bwd(h2, norm_ple_g, dn3, dh3, "rms_ple_bwd", True)
    dact = _matmul(dh2_b, w_down_f, mode='nt', name="mm_dact", out_dtype=BF16, tm=1024, tn=1408)
    gs_down = _matmul(act, dh2_b, mode='tn', name="mm_dw_down", out_dtype=BF16, tm=1408, tn=1024, tk=2048)
    dpg, dpu, wg_acc, wu_acc = _ffn_act_bwd(hid, cf_w, conv_g, conv_u, dact)
    hc = N_CHIPS // 2
    gs_up = jnp.concatenate(
        [_matmul(f, dpg, mode='tn', name="mm_dw_up_g", out_dtype=BF16, out_shards=hc, tm=1024, tn=1408, tk=2048),
         _matmul(f, dpu, mode='tn', name="mm_dw_up_u", out_dtype=BF16, out_shards=hc, tm=1024, tn=1408, tk=2048)],
        axis=0)
    early = [gs_up, gs_down.reshape(N_CHIPS, D_FF // N_CHIPS, d), gs_pg.reshape(N_CHIPS, d // N_CHIPS, d), gs_ple]
    df, *sib_e = _matmul(dpg, g_up, mode='nt', name="mm_df_g", b_sharded=True, tm=1024, tn=1024, tk=2816,
                         hosted=_hosted_rs_sibling(early))
    part_e = [_add_pairs(a, b, "rs_add_e%d" % i) for i, (a, b) in enumerate(zip(early, sib_e))]
    df = _matmul(dpu, g_up, mode='nt', name="mm_df_u", b_sharded=True, b_shard_off=hc, res=df, out_dtype=BF16,
                 tm=1024, tn=1024, tk=2816)
    dh1, dg_ffn, dh1_b = _rms_bwd(h1, norm_ffn_g, df, dh2, "rms_ffn_bwd", True)
    dyab = _matmul(dh1_b, w_out_f, mode='nt', name="mm_dyab", out_dtype=BF16, tm=1024, tn=1024)
    gs_out = _matmul(yab, dh1_b, mode='tn', name="mm_dw_out", out_dtype=BF16, tm=1024, tn=1024, tk=4096)
    duv, dws, dbs, dlng, dlnb, dnag = _gmlp_bwd(proj, dyab, ln_a_g, ln_a_b, w_s[0], b_exp, norm_a_g)
    dproj, ddt_raw, acc_x, acc_b, acc_c, acc_gain, acc_head, *rcv_e = _ssd_bwd(
        proj, dt_raw, dyab, ysave, hs, duv, cs_w, conv_ssm_b, dtb, alog, de, ssm_norm_g, consts,
        _hosted_rs_chips(part_e))
    dw_main = _matmul(a1, dproj, mode='tn', name="mm_dw_main", out_dtype=BF16, tm=1024, tn=1024, tk=4096)
    dw_dt = _matmul(a1, ddt_raw, mode='tn', name="mm_dw_dt", out_dtype=BF16, tm=1024, tn=128, tk=2048)
    gs_in = _col_sharded(jnp.concatenate([dw_main, dw_dt[:, :N_HEADS]], axis=1))
    late = [gs_in, gs_out.reshape(N_CHIPS, D_MIX // N_CHIPS, d)]
    da_dt, *sib_l = _matmul(ddt_raw, w_dt, mode='nt', name="mm_da_dt", tm=1024, tn=1024,
                            hosted=_hosted_rs_sibling(late))
    part_l = [_add_pairs(a, b, "rs_add_l%d" % i) for i, (a, b) in enumerate(zip(late, sib_l))]

    def conv_rows(acc, k):
        return acc[:, k, :].reshape(1, -1)

    dcw = jnp.concatenate([jnp.concatenate([conv_rows(acc_x, k), conv_rows(acc_b, k), conv_rows(acc_c, k)], axis=1)
                           for k in range(SSM_CONV)], axis=0)
    dcb = jnp.concatenate([conv_rows(acc_x, SSM_CONV), conv_rows(acc_b, SSM_CONV), conv_rows(acc_c, SSM_CONV)], axis=1)
    part = {
        'ln_a_g': dlng, 'ln_a_b': dlnb, 'w_s': dws, 'b_s': dbs, 'norm_a_g': dnag,
        'conv_ssm_w': dcw, 'conv_ssm_b': dcb,
        'dt_bias': acc_head[0:1, :N_HEADS], 'a_log': acc_head[1:2, :N_HEADS], 'd_skip': acc_head[2:3, :N_HEADS],
        'ssm_norm_g': acc_gain[:, 0, :], 'norm_ffn_g': dg_ffn,
        'conv_ffn_w': jnp.concatenate([wg_acc[:FFN_CONV], wu_acc[:FFN_CONV]], axis=1),
        'conv_ffn_b': jnp.concatenate([wg_acc[FFN_CONV:FFN_CONV + 1], wu_acc[FFN_CONV:FFN_CONV + 1]], axis=1),
        'norm_ple_g': dg_ple, 'norm_final_g': dgf,
    }
    full_shapes = {n: wts[n].shape for n in SMALL}
    full_shapes['conv_ssm_w'] = (1, SSM_CONV, D_XBC)
    full_shapes['conv_ffn_w'] = (1, FFN_CONV, 2 * D_FF)
    small_e = [n for n in SMALL if n != 'norm_mix_g']
    packed = _pack([part[n] for n in small_e] + [lossv[:, 0:1]])

    halves_e = [_sum_chips(a, b, "rs_sum_e%d" % i) for i, (a, b) in enumerate(zip(part_e, rcv_e))]
    da, *moved = _matmul(dproj, w_main, mode='nt', name="mm_da", res=da_dt, out_dtype=BF16, tm=1024, tn=1024, tk=2560,
                         hosted=_hosted_join([_hosted_rs_chips(part_l), _hosted_share_sibling(halves_e),
                                              _hosted_allgather(packed)]))
    rcv_l, g_early, gathered = moved[:len(late)], moved[len(late):-1], moved[-1]
    dx, dg_mix = _rms_bwd(xt, norm_mix_g, da, dh1, "rms_mix_bwd", False)

    halves_l = [_sum_chips(a, b, "rs_sum_l%d" % i) for i, (a, b) in enumerate(zip(part_l, rcv_l))]
    g_big = dict(zip(['w_up', 'w_down', 'w_ple_gate', 'w_ple'], g_early))
    g_big.update(zip(['w_in', 'w_out'], _run_hosted(_hosted_share_sibling(halves_l), "share_sibling_late")))

    pieces = _unpack(_sum_devices(gathered, "sum_devices"), [full_shapes[n] for n in small_e] + [(1,)])
    g_small = dict(zip(small_e, pieces[:-1]))
    loss = pieces[-1][0]
    mix = _sum_devices(_run_hosted(_hosted_allgather(_pack([dg_mix])), "allgather_mix")[0], "sum_devices_mix")
    g_small['norm_mix_g'] = _unpack(mix, [full_shapes['norm_mix_g']])[0]
    for n in ('conv_ssm_w', 'conv_ffn_w'):
        width = wts[n].shape[2]
        g_small[n] = lax.dynamic_slice_in_dim(g_small[n], chip * width, width, axis=2)

    grads, delta, new_m, new_v = {}, {}, {}, {}
    for n in BIG:
        shp = wts[n].shape
        dl, mn, vn = _adamw(wts[n][0], g_big[n], mom[n][0], var[n][0], "adamw_" + n)
        grads[n], delta[n], new_m[n], new_v[n] = (g_big[n].reshape(shp), dl.reshape(shp), mn.reshape(shp),
                                                  vn.reshape(shp))
    shapes = [wts[n].shape for n in SMALL]
    dl, mn, vn = _adamw(_pack([wts[n] for n in SMALL]), _pack([g_small[n] for n in SMALL]),
                        _pack([mom[n] for n in SMALL]), _pack([var[n] for n in SMALL]), "adamw_small")
    for n, a, b, c in zip(SMALL, _unpack(dl, shapes), _unpack(mn, shapes), _unpack(vn, shapes)):
        grads[n], delta[n], new_m[n], new_v[n] = g_small[n], a, b, c

    return (loss, dx[None], *[grads[n] for n in WEIGHTS], *[delta[n] for n in WEIGHTS],
            *[new_m[n] for n in WEIGHTS], *[new_v[n] for n in WEIGHTS])
```

```python
import math

import jax
import jax.numpy as jnp
from jax import lax
from jax.experimental import pallas as pl
from jax.experimental.pallas import tpu as pltpu

D_MODEL = 2048
SEQ = 8192
D_MIX = 2 * D_MODEL
D_A = D_MIX // 2
CHUNK = 128
N_GROUPS_A = D_A // 128
D_SSM = D_MIX - D_A
HEAD_DIM = 64
N_HEADS = D_SSM // HEAD_DIM
HEADS_PER_GROUP = 4
N_SSM_GROUPS = N_HEADS // HEADS_PER_GROUP
GW = HEADS_PER_GROUP * HEAD_DIM
D_STATE = 128
SSM_CONV = 4
D_BC = N_SSM_GROUPS * D_STATE
D_XBC = D_SSM + 2 * D_BC
D_MAIN = 2 * D_A + D_SSM + D_XBC
D_IN = D_MAIN + N_HEADS
D_FF = (D_MODEL * 11) // 4
FFN_CONV = 3
D_PLE = 256
EPS = 1e-6
HPAD = 128
N_CHIPS = 4

ADAM_LR = 0.001
ADAM_B1 = 0.9
ADAM_B2 = 0.999
ADAM_EPS = 1e-08
ADAM_WD = 0.01
ADAM_STEP = 10

F32 = jnp.float32
BF16 = jnp.bfloat16
MESH = pl.DeviceIdType.MESH
VMEM_LIMIT = 56 * 1024 * 1024


def _cparams(sem):
    return pltpu.CompilerParams(dimension_semantics=sem, vmem_limit_bytes=VMEM_LIMIT)


def _forward_step(n_steps):
    return (7 * n_steps) // 8


def _tile(n, pref, mult):
    t = min(pref, n)
    t -= t % mult
    while n % t:
        t -= mult
    return t


def _dot(a, b):
    return jnp.dot(a, b, preferred_element_type=F32)


def _dot_nt(a, b):
    return lax.dot_general(a, b, (((1,), (1,)), ((), ())), preferred_element_type=F32)


def _dot_tn(a, b):
    return lax.dot_general(a, b, (((0,), (0,)), ((), ())), preferred_element_type=F32)


def _split3(x):
    hi = x.astype(BF16)
    r = x - hi.astype(F32)
    mid = r.astype(BF16)
    lo = (r - mid.astype(F32)).astype(BF16)
    return hi, mid, lo


def _x01(x, e):
    h, m, l = _split3(x)
    return _dot(h, e) + _dot(m, e) + _dot(l, e)


def _x01_nt(x, e):
    h, m, l = _split3(x)
    return _dot_nt(h, e) + _dot_nt(m, e) + _dot_nt(l, e)


def _e01x(e, x):
    h, m, l = _split3(x)
    return _dot(e, h) + _dot(e, m) + _dot(e, l)


def _e01x_tn(e, x):
    h, m, l = _split3(x)
    return _dot_tn(e, h) + _dot_tn(e, m) + _dot_tn(e, l)


def _sigmoid(x):
    return 1.0 / (1.0 + jnp.exp(-x))


_GELU_C = math.sqrt(2.0 / math.pi)


def _gelu_and_grad(x):
    x2 = x * x
    th = jnp.tanh(_GELU_C * (x + 0.044715 * x * x2))
    y = 0.5 * x * (1.0 + th)
    dy = 0.5 * (1.0 + th) + 0.5 * x * (1.0 - th * th) * (_GELU_C * (1.0 + 3.0 * 0.044715 * x2))
    return y, dy


def _silu_and_grad(x):
    s = _sigmoid(x)
    return x * s, s * (1.0 + x * (1.0 - s))


def _softplus(x):
    u = jnp.exp(-jnp.abs(x))
    w = 1.0 + u
    l1p = jnp.where(w == 1.0, u, jnp.log(w) * (u / (w - 1.0)))
    return jnp.maximum(x, 0.0) + l1p


def _matmul(a, b, *, mode, name, out_dtype=F32, res=None, tm=512, tn=512, tk=2048,
            b_sharded=False, b_shard_off=0, b_cols=None, out_shards=0, out_total=0, out_into=None, hosted=None):
    if mode == 'tn':
        kdim, m = a.shape
        n = b.shape[1]
    else:
        m, kdim = a.shape
        if b_sharded:
            s_b, d1, d2 = b.shape
            n = s_b * d2 if mode == 'nn' else d1
        else:
            n = b.shape[1] if mode == 'nn' else b.shape[0]
        if b_cols is not None:
            n = b_cols
    per = None
    if b_sharded:
        per = b.shape[2]
    if out_shards:
        per = n // out_shards
    tm = _tile(m, tm, 128 if mode == 'tn' else 8)
    if mode == 'nt' and b_sharded:
        tn = _tile(n, tn, 128)
        tk = _tile(per, tk, 128)
    elif per is not None:
        tn = _tile(per, tn, 128)
        tk = _tile(kdim, tk, 128 if mode != 'tn' else 8)
    else:
        tn = _tile(n, tn, 128)
        tk = _tile(kdim, tk, 128 if mode != 'tn' else 8)
    nm, nn_, nk = m // tm, n // tn, kdim // tk
    has_res = res is not None
    n_in = 2 + has_res + (out_into is not None)
    nh_in = len(hosted['arrays']) if hosted else 0
    nh_out = len(hosted['out_shape']) if hosted else 0

    def body(*refs):
        a_ref, b_ref = refs[0], refs[1]
        res_ref = refs[2] if has_res else None
        o_ref = refs[n_in + nh_in]
        if hosted:
            hins = refs[n_in:n_in + nh_in]
            houts = refs[n_in + nh_in + 1:n_in + nh_in + 1 + nh_out]
            sems = refs[-2:]
            ids = [pl.program_id(d) for d in range(3)]
            step = (ids[0] * nm + ids[1]) * nk + ids[2]
            n_steps = nn_ * nm * nk
            at_first, at_mid, at_last = step == 0, step == _forward_step(n_steps), step == n_steps - 1
            _host_phase(hosted, 'start', at_first, hins, houts, sems)
        av = a_ref[...].astype(BF16)
        bv = b_ref[...].astype(BF16)
        if mode == 'nn':
            p = _dot(av, bv)
        elif mode == 'nt':
            p = _dot_nt(av, bv)
        else:
            p = _dot_tn(av, bv)

        def fin(v):
            if has_res:
                v = v + res_ref[...]
            o_ref[...] = v.astype(o_ref.dtype)

        if nk == 1:
            fin(p)
        else:
            acc_ref = refs[n_in + nh_in + 1 + nh_out]
            k = pl.program_id(2)

            @pl.when(k == 0)
            def _():
                acc_ref[...] = p

            @pl.when(k > 0)
            def _():
                acc_ref[...] += p

            @pl.when(k == nk - 1)
            def _():
                fin(acc_ref[...])
        if hosted:
            _host_phase(hosted, 'mid', at_mid, hins, houts, sems)
            _host_phase(hosted, 'finish', at_last, hins, houts, sems)

    if mode == 'nn':
        a_spec = pl.BlockSpec((tm, tk), lambda j, i, k: (i, k))
        if b_sharded:
            nps = per // tn
            b_spec = pl.BlockSpec((None, tk, tn), lambda j, i, k: (j // nps, k, j % nps))
        else:
            b_spec = pl.BlockSpec((tk, tn), lambda j, i, k: (k, j))
    elif mode == 'nt':
        a_spec = pl.BlockSpec((tm, tk), lambda j, i, k: (i, k))
        if b_sharded:
            kps = per // tk
            b_spec = pl.BlockSpec((None, tn, tk), lambda j, i, k: (k // kps + b_shard_off, j, k % kps))
        else:
            b_spec = pl.BlockSpec((tn, tk), lambda j, i, k: (j, k))
    else:
        a_spec = pl.BlockSpec((tk, tm), lambda j, i, k: (k, i))
        b_spec = pl.BlockSpec((tk, tn), lambda j, i, k: (k, j))
    in_specs = [a_spec, b_spec]
    args = [a, b]
    if has_res:
        in_specs.append(pl.BlockSpec((tm, tn), lambda j, i, k: (i, j)))
        args.append(res)
    if out_shards:
        nps_o = per // tn
        total, off = (out_into[0].shape[0], out_into[1]) if out_into is not None else (out_total or out_shards, 0)
        out_shape = jax.ShapeDtypeStruct((total, m, per), out_dtype)
        out_spec = pl.BlockSpec((None, tm, tn), lambda j, i, k: (j // nps_o + off, i, j % nps_o))
        if out_into is not None:
            in_specs.append(pl.BlockSpec(memory_space=pl.ANY))
            args.append(out_into[0])
    else:
        out_shape = jax.ShapeDtypeStruct((m, n), out_dtype)
        out_spec = pl.BlockSpec((tm, tn), lambda j, i, k: (i, j))
    scratch = [pltpu.VMEM((tm, tn), F32)] if nk > 1 else []
    if not hosted:
        return pl.pallas_call(
            body, name=name, grid=(nn_, nm, nk), in_specs=in_specs, out_specs=out_spec,
            out_shape=out_shape, scratch_shapes=scratch,
            input_output_aliases={n_in - 1: 0} if out_into is not None else {},
            compiler_params=_cparams(("parallel", "parallel", "arbitrary")),
        )(*args)
    h_in, h_out, h_shape, h_scratch, h_alias = _host_plumbing(hosted, n_in, 1)
    return pl.pallas_call(
        body, name=name, grid=(nn_, nm, nk), in_specs=in_specs + h_in, out_specs=[out_spec] + h_out,
        out_shape=[out_shape] + h_shape, scratch_shapes=scratch + h_scratch, input_output_aliases=h_alias,
        compiler_params=_cparams(("arbitrary", "arbitrary", "arbitrary")),
    )(*args, *hosted['arrays'])


def _rms_fwd(x, g, name):
    t, d = x.shape
    tt = _tile(t, 512, 8)

    def body(x_ref, g_ref, o_ref):
        xv = x_ref[...]
        r = lax.rsqrt(jnp.mean(xv * xv, axis=-1, keepdims=True) + EPS)
        o_ref[...] = (xv * r * g_ref[...]).astype(o_ref.dtype)

    return pl.pallas_call(
        body, name=name, grid=(t // tt,),
        in_specs=[pl.BlockSpec((tt, d), lambda i: (i, 0)), pl.BlockSpec((1, d), lambda i: (0, 0))],
        out_specs=pl.BlockSpec((tt, d), lambda i: (i, 0)),
        out_shape=jax.ShapeDtypeStruct((t, d), BF16),
        compiler_params=_cparams(("parallel",)),
    )(x, g)


def _rms_bwd(x, g, dy, dres, name, also_bf16):
    t, d = x.shape
    tt = _tile(t, 512, 16)

    def body(x_ref, g_ref, dy_ref, dres_ref, dx_ref, dg_ref, *dxb):
        i = pl.program_id(0)
        xv = x_ref[...]
        r = lax.rsqrt(jnp.mean(xv * xv, axis=-1, keepdims=True) + EPS)
        xh = xv * r
        dyv = dy_ref[...].astype(F32)
        dxh = dyv * g_ref[...]
        c = jnp.mean(dxh * xh, axis=-1, keepdims=True)
        dx = dres_ref[...] + r * (dxh - xh * c)
        dx_ref[...] = dx
        for dxb_ref in dxb:
            dxb_ref[...] = dx.astype(BF16)
        part = jnp.sum(dyv * xh, axis=0, keepdims=True)

        @pl.when(i == 0)
        def _():
            dg_ref[...] = part

        @pl.when(i > 0)
        def _():
            dg_ref[...] += part

    row = pl.BlockSpec((tt, d), lambda i: (i, 0))
    vec = pl.BlockSpec((1, d), lambda i: (0, 0))
    return pl.pallas_call(
        body, name=name, grid=(t // tt,),
        in_specs=[row, vec, row, row], out_specs=[row, vec] + [row] * also_bf16,
        out_shape=[jax.ShapeDtypeStruct((t, d), F32), jax.ShapeDtypeStruct((1, d), F32)]
        + [jax.ShapeDtypeStruct((t, d), BF16)] * also_bf16,
        compiler_params=_cparams(("arbitrary",)),
    )(x, g, dy, dres)


def _tail(h2, gl, pe, target, gfin):
    t, d = h2.shape
    tt = _tile(t, 256, 16)

    def body(h2_ref, gl_ref, pe_ref, tg_ref, gf_ref, dh3_ref, dgl_ref, dpe_ref, loss_ref, dgf_ref):
        i = pl.program_id(0)
        sig = _sigmoid(gl_ref[...])
        pev = pe_ref[...]
        h3 = h2_ref[...] + sig * pev
        r = lax.rsqrt(jnp.mean(h3 * h3, axis=-1, keepdims=True) + EPS)
        xh = h3 * r
        gf = gf_ref[...]
        e = xh * gf - tg_ref[...]
        dy = e * (1.0 / d)
        dxh = dy * gf
        c = jnp.mean(dxh * xh, axis=-1, keepdims=True)
        dh3 = r * (dxh - xh * c)
        dh3_ref[...] = dh3
        dgl_ref[...] = (dh3 * pev * sig * (1.0 - sig)).astype(BF16)
        dpe_ref[...] = (dh3 * sig).astype(BF16)
        lpart = jnp.sum(e * e, axis=0, keepdims=True) * (0.5 / d)
        gpart = jnp.sum(dy * xh, axis=0, keepdims=True)

        @pl.when(i == 0)
        def _():
            loss_ref[...] = lpart
            dgf_ref[...] = gpart

        @pl.when(i > 0)
        def _():
            loss_ref[...] += lpart
            dgf_ref[...] += gpart

        @pl.when(i == t // tt - 1)
        def _():
            loss_ref[...] = jnp.broadcast_to(jnp.sum(loss_ref[...], axis=-1, keepdims=True), (1, d))

    row = pl.BlockSpec((tt, d), lambda i: (i, 0))
    vec = pl.BlockSpec((1, d), lambda i: (0, 0))
    return pl.pallas_call(
        body, name="tail", grid=(t // tt,),
        in_specs=[row, row, row, row, vec], out_specs=[row, row, row, vec, vec],
        out_shape=[jax.ShapeDtypeStruct((t, d), F32), jax.ShapeDtypeStruct((t, d), BF16),
                   jax.ShapeDtypeStruct((t, d), BF16), jax.ShapeDtypeStruct((1, d), F32),
                   jax.ShapeDtypeStruct((1, d), F32)],
        compiler_params=_cparams(("arbitrary",)),
    )(h2, gl, pe, target, gfin)


def _conv(cur_ref, prev_ref, w_ref, b_ref, ext_ref, first, width):
    rows = cur_ref.shape[0]
    ext_ref[0:8, :] = jnp.where(first, 0.0, prev_ref[...])
    ext_ref[8:8 + rows, :] = cur_ref[...]
    acc = b_ref[...]
    for k in range(width):
        acc = acc + w_ref[k:k + 1, :] * ext_ref[pl.ds(9 - width + k, rows), :]
    return acc


def _ffn_specs(t, tt, tc, nf):
    hb = tt // 8
    cur_g = pl.BlockSpec((tt, tc), lambda j, i: (i, j))
    cur_u = pl.BlockSpec((tt, tc), lambda j, i: (i, j + nf))
    prev_g = pl.BlockSpec((8, tc), lambda j, i: (jnp.maximum(i * hb - 1, 0), j))
    prev_u = pl.BlockSpec((8, tc), lambda j, i: (jnp.maximum(i * hb - 1, 0), j + nf))
    w_g = pl.BlockSpec((FFN_CONV, tc), lambda j, i: (0, j))
    w_u = pl.BlockSpec((FFN_CONV, tc), lambda j, i: (0, j + nf))
    b_g = pl.BlockSpec((1, tc), lambda j, i: (0, j))
    b_u = pl.BlockSpec((1, tc), lambda j, i: (0, j + nf))
    return [cur_g, prev_g, cur_u, prev_u, w_g, w_u, b_g, b_u]


FFN_TC = 512


def _shift_down(prev, cur, n, rid):
    return jnp.where(rid < n, pltpu.roll(prev, n, 0), pltpu.roll(cur, n, 0))


def _shift_up(cur, nxt, n, rid):
    return jnp.where(rid < 8 - n, pltpu.roll(cur, 8 - n, 0), pltpu.roll(nxt, 8 - n, 0))


def _conv3_group(prev, cur, w_ref, b_ref, rid):
    x1 = _shift_down(prev, cur, 1, rid)
    x2 = _shift_down(prev, cur, 2, rid)
    return b_ref[...] + w_ref[2:3, :] * cur + w_ref[1:2, :] * x1 + w_ref[0:1, :] * x2


def _ffn_act_fwd(hid, cw, cb):
    t = hid.shape[0]
    tt = _tile(t, 2048, 16)
    tc = _tile(D_FF, FFN_TC, 128)
    nf = D_FF // tc

    def body(g_ref, gp_ref, u_ref, up_ref, wg_ref, wu_ref, bg_ref, bu_ref, o_ref, cg_ref, cu_ref):
        first = pl.program_id(1) == 0
        rid = lax.broadcasted_iota(jnp.int32, (8, tc), 0)

        def step(s, carry):
            pg, pu = carry
            r0 = pl.multiple_of(s * 16, 16)
            g0, g1 = g_ref[pl.ds(r0, 8), :], g_ref[pl.ds(r0 + 8, 8), :]
            u0, u1 = u_ref[pl.ds(r0, 8), :], u_ref[pl.ds(r0 + 8, 8), :]
            gate = jnp.concatenate([_conv3_group(pg, g0, wg_ref, bg_ref, rid),
                                    _conv3_group(g0, g1, wg_ref, bg_ref, rid)], axis=0)
            up = jnp.concatenate([_conv3_group(pu, u0, wu_ref, bu_ref, rid),
                                  _conv3_group(u0, u1, wu_ref, bu_ref, rid)], axis=0)
            cg_ref[pl.ds(r0, 16), :] = gate.astype(BF16)
            cu_ref[pl.ds(r0, 16), :] = up.astype(BF16)
            o_ref[pl.ds(r0, 16), :] = (gate * _sigmoid(gate) * up).astype(BF16)
            return g1, u1

        init = (jnp.where(first, 0.0, gp_ref[...]), jnp.where(first, 0.0, up_ref[...]))
        lax.fori_loop(0, tt // 16, step, init, unroll=2)

    blk = pl.BlockSpec((tt, tc), lambda j, i: (i, j))
    return pl.pallas_call(
        body, name="ffn_act_fwd", grid=(nf, t // tt), in_specs=_ffn_specs(t, tt, tc, nf),
        out_specs=[blk, blk, blk],
        out_shape=[jax.ShapeDtypeStruct((t, D_FF), BF16)] * 3,
        compiler_params=_cparams(("parallel", "arbitrary")),
    )(hid, hid, hid, hid, cw, cw, cb, cb)


def _ffn_act_bwd(hid, cw, conv_g, conv_u, dact):
    t = hid.shape[0]
    tt = _tile(t, 2048, 16)
    tc = _tile(D_FF, FFN_TC, 128)
    nf = D_FF // tc
    nt = t // tt
    n16 = tt // 16

    def body(g_ref, u_ref, wg_ref, wu_ref, cg_ref, cgn_ref, cu_ref, cun_ref, da_ref, dan_ref,
             og_ref, ou_ref, ag_ref, au_ref, accs):
        i = pl.program_id(1)
        first, last = i == 0, i == nt - 1
        rid = lax.broadcasted_iota(jnp.int32, (8, tc), 0)
        accs[...] = jnp.zeros_like(accs)

        def dgroup(gate, up, da):
            sv, sgr = _silu_and_grad(gate)
            return da * up * sgr, da * sv

        def finish(x, d0, d1, w_ref):
            s1 = _shift_up(d0, d1, 1, rid)
            s2 = _shift_up(d0, d1, 2, rid)
            dpre = w_ref[2:3, :] * d0 + w_ref[1:2, :] * s1 + w_ref[0:1, :] * s2
            return dpre, (x * s2, x * s1, x * d0, d0)

        def two_groups(it, carry, gate_blk, up_blk, da_blk, zero_ahead):
            d0g, d0u, gate1, up1, da1 = carry
            r0 = it * 16 if isinstance(it, int) else pl.multiple_of(it * 16, 16)
            x0g, x0u = g_ref[pl.ds(r0, 8), :], u_ref[pl.ds(r0, 8), :]
            x1g, x1u = g_ref[pl.ds(r0 + 8, 8), :], u_ref[pl.ds(r0 + 8, 8), :]
            d1g, d1u = dgroup(gate1, up1, da1)
            d2g, d2u = dgroup(gate_blk[0:8], up_blk[0:8], da_blk[0:8])
            d2g = jnp.where(zero_ahead, 0.0, d2g)
            d2u = jnp.where(zero_ahead, 0.0, d2u)
            for half, (xa, xb, da_, db_, dc_, w_ref, o_ref) in enumerate((
                    (x0g, x1g, d0g, d1g, d2g, wg_ref, og_ref), (x0u, x1u, d0u, d1u, d2u, wu_ref, ou_ref))):
                pa, prods_a = finish(xa, da_, db_, w_ref)
                pb, prods_b = finish(xb, db_, dc_, w_ref)
                o_ref[pl.ds(r0, 16), :] = jnp.concatenate([pa, pb], axis=0).astype(BF16)
                for k in range(4):
                    accs[4 * half + k] += prods_a[k] + prods_b[k]
            return d2g, d2u, gate_blk[8:16], up_blk[8:16], da_blk[8:16]

        def rows16(ref, r):
            return ref[pl.ds(r, 16), :].astype(F32)

        def step(it, carry):
            r1 = pl.multiple_of(it * 16 + 16, 16)
            return two_groups(it, carry, rows16(cg_ref, r1), rows16(cu_ref, r1), rows16(da_ref, r1), False)

        g0, u0, da0 = rows16(cg_ref, 0), rows16(cu_ref, 0), rows16(da_ref, 0)
        d0g, d0u = dgroup(g0[0:8], u0[0:8], da0[0:8])
        carry = lax.fori_loop(0, n16 - 1, step, (d0g, d0u, g0[8:16], u0[8:16], da0[8:16]))
        two_groups(n16 - 1, carry, cgn_ref[...].astype(F32), cun_ref[...].astype(F32), dan_ref[...].astype(F32), last)

        @pl.when(first)
        def _():
            ag_ref[...] = jnp.zeros_like(ag_ref)
            au_ref[...] = jnp.zeros_like(au_ref)

        for half, a_ref in enumerate((ag_ref, au_ref)):
            for k in range(4):
                a_ref[k:k + 1, :] += jnp.sum(accs[4 * half + k], axis=0, keepdims=True)

    def nxt16(i):
        return jnp.minimum((i + 1) * n16, t // 16 - 1)

    out_blk = pl.BlockSpec((tt, tc), lambda j, i: (i, j))
    acc_spec = pl.BlockSpec((8, tc), lambda j, i: (0, j))
    nxt_blk = pl.BlockSpec((16, tc), lambda j, i: (nxt16(i), j))
    in_specs = [out_blk, pl.BlockSpec((tt, tc), lambda j, i: (i, j + nf)),
                pl.BlockSpec((FFN_CONV, tc), lambda j, i: (0, j)), pl.BlockSpec((FFN_CONV, tc), lambda j, i: (0, j + nf)),
                out_blk, nxt_blk, out_blk, nxt_blk, out_blk, nxt_blk]
    return pl.pallas_call(
        body, name="ffn_act_bwd", grid=(nf, nt), in_specs=in_specs,
        out_specs=[out_blk, out_blk, acc_spec, acc_spec],
        out_shape=[jax.ShapeDtypeStruct((t, D_FF), BF16), jax.ShapeDtypeStruct((t, D_FF), BF16),
                   jax.ShapeDtypeStruct((8, D_FF), F32), jax.ShapeDtypeStruct((8, D_FF), F32)],
        scratch_shapes=[pltpu.VMEM((8, 8, tc), F32)],
        compiler_params=_cparams(("parallel", "arbitrary")),
    )(hid, hid, cw, cw, conv_g, conv_g, conv_u, conv_u, dact, dact)


def _tri_mask():
    r = lax.broadcasted_iota(jnp.int32, (CHUNK, CHUNK), 0)
    c = lax.broadcasted_iota(jnp.int32, (CHUNK, CHUNK), 1)
    return r >= c


def _gmlp_group_fwd(uv_ref, lng_ref, lnb_ref, ws_ref, bexp_ref, tri, g, want_grad):
    lo, hi = g * 128, (g + 1) * 128
    u_pre = uv_ref[:, lo:hi]
    v_pre = uv_ref[:, D_A + lo:D_A + hi]
    u, du = _gelu_and_grad(u_pre)
    v, dv = _gelu_and_grad(v_pre)
    mu = jnp.mean(v, axis=-1, keepdims=True)
    dc = v - mu
    rs = lax.rsqrt(jnp.mean(dc * dc, axis=-1, keepdims=True) + EPS)
    xh = dc * rs
    vn = (xh * lng_ref[:, lo:hi] + lnb_ref[:, lo:hi]).astype(BF16)
    w = jnp.where(tri, ws_ref[g], 0.0).astype(BF16)
    sg = _dot(w, vn) + bexp_ref[g]
    if want_grad:
        return u, du, dv, rs, xh, vn, w, sg
    return u * sg


def _gmlp_fwd(proj, ln_g, ln_b, w_s, b_exp, na_g):
    t = proj.shape[0]
    ng = N_GROUPS_A

    def body(uv_ref, lng_ref, lnb_ref, ws_ref, bexp_ref, nag_ref, o_ref):
        tri = _tri_mask()
        ys = [_gmlp_group_fwd(uv_ref, lng_ref, lnb_ref, ws_ref, bexp_ref, tri, g, False) for g in range(ng)]
        ssq = ys[0] * 0.0
        for y in ys:
            ssq = ssq + y * y
        r = lax.rsqrt(jnp.sum(ssq, axis=-1, keepdims=True) * (1.0 / D_A) + EPS)
        for g, y in enumerate(ys):
            o_ref[:, g * 128:(g + 1) * 128] = (y * r * nag_ref[:, g * 128:(g + 1) * 128]).astype(BF16)

    vec = pl.BlockSpec((1, D_A), lambda i: (0, 0))
    cube = pl.BlockSpec((ng, CHUNK, CHUNK), lambda i: (0, 0, 0))
    return pl.pallas_call(
        body, name="gmlp_fwd", grid=(t // CHUNK,),
        in_specs=[pl.BlockSpec((CHUNK, 2 * D_A), lambda i: (i, 0)), vec, vec, cube, cube, vec],
        out_specs=pl.BlockSpec((CHUNK, D_A), lambda i: (i, 0)),
        out_shape=jax.ShapeDtypeStruct((t, D_MIX), BF16),
        compiler_params=_cparams(("parallel",)),
    )(proj, ln_g, ln_b, w_s, b_exp, na_g)


def _gmlp_bwd(proj, dyab, ln_g, ln_b, w_s, b_exp, na_g):
    t = proj.shape[0]
    ng = N_GROUPS_A
    nsteps = t // CHUNK

    def body(uv_ref, dy_ref, lng_ref, lnb_ref, ws_ref, bexp_ref, nag_ref,
             duv_ref, dws_ref, dbs_ref, dlng_ref, dlnb_ref, dnag_ref, dbacc):
        i = pl.program_id(0)
        tri = _tri_mask()

        @pl.when(i == 0)
        def _():
            dws_ref[...] = jnp.zeros_like(dws_ref)
            dbacc[...] = jnp.zeros_like(dbacc)
            dlng_ref[...] = jnp.zeros_like(dlng_ref)
            dlnb_ref[...] = jnp.zeros_like(dlnb_ref)
            dnag_ref[...] = jnp.zeros_like(dnag_ref)

        st = [_gmlp_group_fwd(uv_ref, lng_ref, lnb_ref, ws_ref, bexp_ref, tri, g, True) for g in range(ng)]
        ssq = st[0][0] * 0.0
        for s in st:
            y = s[0] * s[7]
            ssq = ssq + y * y
        r = lax.rsqrt(jnp.sum(ssq, axis=-1, keepdims=True) * (1.0 / D_A) + EPS)
        csum = st[0][0] * 0.0
        for g, s in enumerate(st):
            sl = slice(g * 128, (g + 1) * 128)
            xhy = s[0] * s[7] * r
            dya = dy_ref[:, sl].astype(F32)
            dnag_ref[:, sl] += jnp.sum(dya * xhy, axis=0, keepdims=True)
            csum = csum + dya * nag_ref[:, sl] * xhy
        c1 = jnp.sum(csum, axis=-1, keepdims=True) * (1.0 / D_A)
        for g, s in enumerate(st):
            u, du, dv, rs, xh, vn, w, sg = s
            sl = slice(g * 128, (g + 1) * 128)
            dy = r * (dy_ref[:, sl].astype(F32) * nag_ref[:, sl] - u * sg * r * c1)
            dsg = dy * u
            dsg_b = dsg.astype(BF16)
            dws_ref[g] += _dot_nt(dsg_b, vn)
            dbacc[g] += dsg
            dvn = _dot_tn(w, dsg_b)
            dlnb_ref[:, sl] += jnp.sum(dvn, axis=0, keepdims=True)
            dlng_ref[:, sl] += jnp.sum(dvn * xh, axis=0, keepdims=True)
            dxh = dvn * lng_ref[:, sl]
            dvv = rs * (dxh - jnp.mean(dxh, axis=-1, keepdims=True)
                        - xh * jnp.mean(dxh * xh, axis=-1, keepdims=True))
            duv_ref[:, sl] = (dy * sg * du).astype(BF16)
            duv_ref[:, D_A + g * 128:D_A + (g + 1) * 128] = (dvv * dv).astype(BF16)

        @pl.when(i == nsteps - 1)
        def _():
            for g in range(ng):
                dws_ref[g] = jnp.where(tri, dws_ref[g], 0.0)
                dbs_ref[g] = jnp.sum(dbacc[g], axis=-1, keepdims=True)

    vec = pl.BlockSpec((1, D_A), lambda i: (0, 0))
    cube = pl.BlockSpec((ng, CHUNK, CHUNK), lambda i: (0, 0, 0))
    return pl.pallas_call(
        body, name="gmlp_bwd", grid=(nsteps,),
        in_specs=[pl.BlockSpec((CHUNK, 2 * D_A), lambda i: (i, 0)),
                  pl.BlockSpec((CHUNK, D_A), lambda i: (i, 0)), vec, vec, cube, cube, vec],
        out_specs=[pl.BlockSpec((CHUNK, 2 * D_A), lambda i: (i, 0)), cube,
                   pl.BlockSpec((ng, CHUNK, 1), lambda i: (0, 0, 0)), vec, vec, vec],
        out_shape=[jax.ShapeDtypeStruct((t, 2 * D_A), BF16), jax.ShapeDtypeStruct((ng, CHUNK, CHUNK), F32),
                   jax.ShapeDtypeStruct((ng, CHUNK, 1), F32), jax.ShapeDtypeStruct((1, D_A), F32),
                   jax.ShapeDtypeStruct((1, D_A), F32), jax.ShapeDtypeStruct((1, D_A), F32)],
        scratch_shapes=[pltpu.VMEM((ng, CHUNK, CHUNK), F32)],
        compiler_params=_cparams(("arbitrary",)),
    )(proj, dyab, ln_g, ln_b, w_s, b_exp, na_g)


OFF_Z = 2 * D_A
OFF_XS = OFF_Z + D_SSM
OFF_B = OFF_XS + D_SSM
OFF_C = OFF_B + D_BC


def _ssd_consts():
    tri = jnp.tril(jnp.ones((CHUNK, CHUNK), F32)).astype(BF16)
    h = jnp.arange(HPAD)[None, :, None]
    g = jnp.arange(N_SSM_GROUPS)[:, None, None]
    j1 = jnp.arange(GW)[None, None, :]
    eh = (h == g * HEADS_PER_GROUP + j1 // HEAD_DIM).astype(BF16)
    j2 = jnp.arange(HEADS_PER_GROUP * 128)[None, None, :]
    e128 = (h == g * HEADS_PER_GROUP + j2 // 128).astype(BF16)
    return tri, eh, e128


def _ssd_in_specs(cmap):
    def rows(i):
        return cmap(i)

    def prev8(i):
        return jnp.maximum(cmap(i) * (CHUNK // 8) - 1, 0)

    def whole(*shape):
        return pl.BlockSpec(shape, lambda i: (0,) * len(shape))

    bcw = 2 * D_BC
    specs = [
        pl.BlockSpec((CHUNK, D_SSM), lambda i: (rows(i), OFF_Z // D_SSM)),
        pl.BlockSpec((CHUNK, D_SSM), lambda i: (rows(i), OFF_XS // D_SSM)),
        pl.BlockSpec((8, D_SSM), lambda i: (prev8(i), OFF_XS // D_SSM)),
        pl.BlockSpec((CHUNK, bcw), lambda i: (rows(i), OFF_B // bcw)),
        pl.BlockSpec((8, bcw), lambda i: (prev8(i), OFF_B // bcw)),
        pl.BlockSpec((CHUNK, HPAD), lambda i: (rows(i), 0)),
        whole(SSM_CONV, D_XBC), whole(1, D_XBC),
        whole(1, HPAD), whole(1, HPAD),
        whole(1, D_SSM), whole(1, D_SSM),
        whole(CHUNK, CHUNK),
        whole(N_SSM_GROUPS, HPAD, GW), whole(N_SSM_GROUPS, HPAD, HEADS_PER_GROUP * 128),
    ]
    return specs


N_SSD_IN = 15


def _lanes(ref, start, width):
    return ref.at[:, pl.ds(pl.multiple_of(start, 128), width)]


def _ssd_group_refs(ins, g):
    (z_ref, xs_ref, xsp_ref, bc_ref, bcp_ref, dt_ref, cw_ref, cb_ref, dtb_ref, alog_ref, de_ref, gain_ref,
     tri_ref, eh_ref, e128_ref) = ins
    ox, ob, oc = g * GW, g * D_STATE, D_BC + g * D_STATE
    return (_lanes(z_ref, ox, GW), _lanes(xs_ref, ox, GW), _lanes(xsp_ref, ox, GW),
            _lanes(bc_ref, ob, D_STATE), _lanes(bcp_ref, ob, D_STATE),
            _lanes(bc_ref, oc, D_STATE), _lanes(bcp_ref, oc, D_STATE), dt_ref,
            _lanes(cw_ref, ox, GW), _lanes(cw_ref, D_SSM + ob, D_STATE), _lanes(cw_ref, D_SSM + oc, D_STATE),
            _lanes(cb_ref, ox, GW), _lanes(cb_ref, D_SSM + ob, D_STATE), _lanes(cb_ref, D_SSM + oc, D_STATE),
            dtb_ref, alog_ref, _lanes(de_ref, ox, GW), _lanes(gain_ref, ox, GW), tri_ref,
            eh_ref.at[g], e128_ref.at[g])


N_SSD_OWN = 4


def _ssd_scratch():
    return [pltpu.VMEM((CHUNK + 8, GW), F32), pltpu.VMEM((CHUNK + 8, D_STATE), F32),
            pltpu.VMEM((CHUNK + 8, D_STATE), F32), pltpu.VMEM((CHUNK, GW), F32),
            pltpu.VMEM((HPAD, CHUNK), F32), pltpu.VMEM((CHUNK, HPAD), F32), pltpu.VMEM((CHUNK, HPAD), F32)]


def _ssd_two_sets(alloc):
    a, b = alloc[:7], alloc[7:7 + N_SSD_OWN]
    return ((a[0], a[1], a[2], a[4], a[3], a[5], a[6]), (b[0], b[1], b[2], a[4], b[3], a[5], a[6]))


def _ssd_chunk(ins, scr):
    dt_ref, dtb_ref, alog_ref, tri_ref = ins[5], ins[8], ins[9], ins[12]
    acst_sc, dt_sc, acs_sc = scr[3], scr[5], scr[6]
    dt = _softplus(dt_ref[...] + dtb_ref[...])
    acs = _e01x(tri_ref[...], dt * (-jnp.exp(alog_ref[...])))
    dt_sc[...] = dt
    acs_sc[...] = acs
    acst_sc[...] = acs.T


def _ssd_pre(first, g, refs, scr):
    (z_ref, xs_ref, xsp_ref, b_ref, bp_ref, c_ref, cp_ref, dt_ref, cwx, cwb, cwc, cbx, cbb, cbc,
     dtb_ref, alog_ref, de_ref, gain_ref, tri_ref, eh_ref, e128_ref) = refs
    ext_x, ext_b, ext_c, acst_sc, acse_sc, dt_sc, acs_sc = scr
    p = {}
    px = _conv(xs_ref, xsp_ref, cwx, cbx, ext_x, first, SSM_CONV)
    pb = _conv(b_ref, bp_ref, cwb, cbb, ext_b, first, SSM_CONV)
    pc = _conv(c_ref, cp_ref, cwc, cbc, ext_c, first, SSM_CONV)
    p['xs'], p['dsx'] = _silu_and_grad(px)
    p['bm'], p['dsb'] = _silu_and_grad(pb)
    p['cm'], p['dsc'] = _silu_and_grad(pc)
    p['dt_in'] = dt_ref[...] + dtb_ref[...]
    dt = dt_sc[...]
    p['dt'] = dt
    p['a'] = -jnp.exp(alog_ref[...])
    tri_b = tri_ref[...]
    acs = acs_sc[...]
    eh = eh_ref[...]
    p['eh'] = eh
    p['dt_e'] = _x01(dt, eh)
    acs_e = _x01(acs, eh)
    acse_sc[...] = acs_e
    p['acs_e'] = acs_e
    p['acs_c'] = _x01(acs, e128_ref[...])
    p['acs_last_e'] = acse_sc[pl.ds(CHUNK - 1, 1), :]
    p['xdt'] = p['xs'] * p['dt_e']
    p['decay_e'] = jnp.exp(p['acs_last_e'] - acs_e)
    p['cm_b'] = p['cm'].astype(BF16)
    p['bm_b'] = p['bm'].astype(BF16)
    p['scores'] = _dot_nt(p['cm_b'], p['bm_b'])
    p['tri_b'] = tri_b
    return p


def _ssd_l(p, g, r, acst_sc, tri):
    col = p['acs_c'][:, r * 128:(r + 1) * 128]
    row = acst_sc[pl.ds(g * HEADS_PER_GROUP + r, 1), :]
    return jnp.where(tri, jnp.exp(jnp.minimum(col - row, 0.0)), 0.0)


def _ssd_fwd(proj, dt_raw, yab, cw, cb, dtb, alog, de, gain, consts, hosted):
    t = proj.shape[0]
    nc = t // CHUNK
    ng = N_SSM_GROUPS
    tri_c, eh_c, e128_c = consts
    n_in = N_SSD_IN
    nh_in, nh_out = len(hosted['arrays']), len(hosted['out_shape'])
    mid_chunk = _forward_step(nc)

    def body(*refs):
        ins = refs[:n_in]
        hins = refs[n_in + 1:n_in + 1 + nh_in]
        o0 = n_in + 1 + nh_in
        yb_all, ys_all, hs_all = refs[o0:o0 + 3]
        houts = refs[o0 + 3:o0 + 3 + nh_out]
        h_sc = refs[o0 + 3 + nh_out]
        scr_a, scr_b = _ssd_two_sets(refs[o0 + 4 + nh_out:-2])
        sems = refs[-2:]
        c = pl.program_id(0)
        _host_phase(hosted, 'start', c == 0, hins, houts, sems)
        _host_phase(hosted, 'mid', c == mid_chunk, hins, houts, sems)

        @pl.when(c == 0)
        def _():
            h_sc[...] = jnp.zeros_like(h_sc)

        def group(g, scr):
            grefs = _ssd_group_refs(ins, g)
            z_ref, de_ref, gain_ref = grefs[0], grefs[16], grefs[17]
            yb_ref, ys_ref = _lanes(yb_all, g * GW, GW), _lanes(ys_all, g * GW, GW)
            slab = pl.ds(pl.multiple_of(g * D_STATE, D_STATE), D_STATE)
            p = _ssd_pre(c == 0, g, grefs, scr)
            tri = _tri_mask()
            h_in = h_sc[slab, :]
            hs_all[slab, :] = h_in
            yoff = _dot(p['cm_b'], h_in.astype(BF16)) * jnp.exp(p['acs_e'])
            states = _dot_tn(p['bm_b'], (p['xdt'] * p['decay_e']).astype(BF16))
            lane = lax.broadcasted_iota(jnp.int32, (CHUNK, 128), 1)
            slabs = []
            for r2 in range(HEADS_PER_GROUP // 2):
                xb = p['xdt'][:, r2 * 128:(r2 + 1) * 128].astype(BF16)
                ya = _dot((p['scores'] * _ssd_l(p, g, 2 * r2, scr[3], tri)).astype(BF16), xb)
                yb = _dot((p['scores'] * _ssd_l(p, g, 2 * r2 + 1, scr[3], tri)).astype(BF16), xb)
                slabs.append(jnp.where(lane < HEAD_DIM, ya, yb))
            y = jnp.concatenate(slabs, axis=1) + yoff + de_ref[...] * p['xs']
            ys_ref[...] = y
            h_sc[slab, :] = jnp.exp(p['acs_last_e']) * h_in + states
            zv = z_ref[...]
            yg = y * zv * _sigmoid(zv)
            r = lax.rsqrt(jnp.mean(yg * yg, axis=-1, keepdims=True) + EPS)
            yb_ref[...] = (yg * r * gain_ref[...]).astype(BF16)

        def pair(j, carry):
            group(2 * j, scr_a)
            group(2 * j + 1, scr_b)
            return carry

        _ssd_chunk(ins, scr_a)
        lax.fori_loop(0, ng // 2, pair, 0)
        _host_phase(hosted, 'finish', c == nc - 1, hins, houts, sems)

    h_in, h_out, h_shape, h_scratch, h_alias = _host_plumbing(hosted, n_in + 1, 3)
    in_specs = _ssd_in_specs(lambda i: i) + [pl.BlockSpec(memory_space=pl.ANY)] + h_in
    out_specs = [pl.BlockSpec((CHUNK, D_SSM), lambda i: (i, D_A // D_SSM)),
                 pl.BlockSpec((CHUNK, D_SSM), lambda i: (i, 0)),
                 pl.BlockSpec((None, ng * D_STATE, GW), lambda i: (i, 0, 0))] + h_out
    out_shape = [jax.ShapeDtypeStruct((t, D_MIX), BF16), jax.ShapeDtypeStruct((t, D_SSM), F32),
                 jax.ShapeDtypeStruct((nc, ng * D_STATE, GW), F32)] + h_shape
    return pl.pallas_call(
        body, name="ssd_fwd", grid=(nc,), in_specs=in_specs, out_specs=out_specs, out_shape=out_shape,
        scratch_shapes=[pltpu.VMEM((ng * D_STATE, GW), F32)] + _ssd_scratch() + _ssd_scratch()[:N_SSD_OWN]
        + h_scratch,
        input_output_aliases={n_in: 0, **h_alias},
        compiler_params=_cparams(("arbitrary",)),
    )(proj, proj, proj, proj, proj, dt_raw, cw, cb, dtb, alog, de, gain,
      tri_c, eh_c, e128_c, yab, *hosted['arrays'])


def _rows8(vals, width):
    rid = lax.broadcasted_iota(jnp.int32, (8, width), 0)
    out = jnp.zeros((8, width), F32)
    for k, v in enumerate(vals):
        if v is not None:
            out = out + jnp.where(rid == k, v, 0.0)
    return out


def _ssd_bwd(proj, dt_raw, dyab, ysave, hs, duv, cw, cb, dtb, alog, de, gain, consts, hosted):
    t = proj.shape[0]
    nc = t // CHUNK
    ng = N_SSM_GROUPS
    tri_c, eh_c, e128_c = consts
    n_in = N_SSD_IN
    nh_in, nh_out = len(hosted['arrays']), len(hosted['out_shape'])

    def per_group(g, c, ins, dy_ref, ys_ref, hs_ref, outs, scratch):
        dz_ref, dxs_ref, db_ref, dc_ref, ddt_ref, acc_x, acc_b, acc_c, acc_gain, acc_head = outs
        dh_sc, car_x, car_b, car_c, dext_x, dext_b, dext_c = scratch[:7]
        scr = scratch[7:]
        ext_x, ext_b, ext_c, acst_sc = scr[:4]
        z_ref, cwx, cwb, cwc, de_ref, gain_ref = ins[0], ins[8], ins[9], ins[10], ins[16], ins[17]
        slab = pl.ds(pl.multiple_of(g * D_STATE, D_STATE), D_STATE)
        p = _ssd_pre(c == 0, g, ins, scr)
        tri = _tri_mask()
        xs, dt_e, acs_e, xdt, decay_e = p['xs'], p['dt_e'], p['acs_e'], p['xdt'], p['decay_e']
        cm_b, bm_b, scores, eh = p['cm_b'], p['bm_b'], p['scores'], p['eh']
        h_in = hs_ref[...]
        h_in_b = h_in.astype(BF16)
        e_a = jnp.exp(acs_e)
        raw = _dot(cm_b, h_in_b)

        y = ys_ref[...]
        zv = z_ref[...]
        sz, dsz = _silu_and_grad(zv)
        yg = y * sz
        r = lax.rsqrt(jnp.mean(yg * yg, axis=-1, keepdims=True) + EPS)
        xh = yg * r
        dout = dy_ref[...].astype(F32)
        gain = gain_ref[...]
        dxh = dout * gain
        dyg = r * (dxh - xh * jnp.mean(dxh * xh, axis=-1, keepdims=True))
        dy = dyg * sz
        dz_ref[...] = (dyg * y * dsz).astype(BF16)
        acc_gain[g] += _rows8([jnp.sum(dout * xh, axis=0, keepdims=True)], GW)
        d_skip8 = _x01_nt(_rows8([None, None, jnp.sum(dy * xs, axis=0, keepdims=True)], GW), eh)
        dxs = de_ref[...] * dy

        q = dy * raw * e_a
        draw = (dy * e_a).astype(BF16)
        d_c = _dot_nt(draw, h_in_b)
        dh_in = _dot_tn(cm_b, draw)

        lane = lax.broadcasted_iota(jnp.int32, (CHUNK, 128), 1)
        ones_b = jnp.ones((CHUNK, 128), BF16)
        dscores = jnp.zeros((CHUNK, CHUNK), F32)
        dxdt_slabs, q_slabs = [], []
        for r2 in range(HEADS_PER_GROUP // 2):
            sl = slice(r2 * 128, (r2 + 1) * 128)
            xb = xdt[:, sl].astype(BF16)
            dys = dy[:, sl]
            dys_b = dys.astype(BF16)
            dxh_pair, qv_pair = [], []
            for half in range(2):
                lmat = _ssd_l(p, g, 2 * r2 + half, acst_sc, tri)
                m = scores * lmat
                mine = (lane < HEAD_DIM) if half == 0 else (lane >= HEAD_DIM)
                dm = _dot_nt(jnp.where(mine, dys, 0.0).astype(BF16), xb)
                dscores = dscores + dm * lmat
                gm = dm * m
                dxh_pair.append(_dot_tn(m.astype(BF16), dys_b))
                h3 = _split3(gm)
                colsum = _dot_tn(h3[0], ones_b) + _dot_tn(h3[1], ones_b) + _dot_tn(h3[2], ones_b)
                qv_pair.append(jnp.sum(gm, axis=-1, keepdims=True) - colsum)
            dxdt_slabs.append(jnp.where(lane < HEAD_DIM, dxh_pair[0], dxh_pair[1]))
            q_slabs.append(jnp.where(lane == 0, qv_pair[0], 0.0) + jnp.where(lane == HEAD_DIM, qv_pair[1], 0.0))
        dxdt = jnp.concatenate(dxdt_slabs, axis=1)
        q = q + jnp.concatenate(q_slabs, axis=1)

        dh_out = dh_sc[slab, :]
        dh_out_b = dh_out.astype(BF16)
        e_l = jnp.exp(p['acs_last_e'])
        dh_sc[slab, :] = dh_in + e_l * dh_out
        dlast = jnp.sum(dh_out * h_in, axis=0, keepdims=True) * e_l
        dxd = _dot(bm_b, dh_out_b)
        xd = xdt * decay_e
        dxdt = dxdt + dxd * decay_e
        dd = dxd * xd
        q = q - dd
        dlast = dlast + jnp.sum(dd, axis=0, keepdims=True)
        d_b = _dot_nt(xd.astype(BF16), dh_out_b)
        dsc_b = dscores.astype(BF16)
        d_c = d_c + _dot(dsc_b, bm_b)
        d_b = d_b + _dot_tn(dsc_b, cm_b)

        dxs = dxs + dxdt * dt_e
        rid = lax.broadcasted_iota(jnp.int32, (CHUNK, GW), 0)
        q = q + jnp.where(rid == CHUNK - 1, dlast, 0.0)
        dacs = _x01_nt(q, eh)
        ddt = _x01_nt(dxdt * xs, eh)
        dadt = _e01x_tn(p['tri_b'], dacs)
        ddt = ddt + dadt * p['a']
        d_a = jnp.sum(dadt * p['dt'], axis=0, keepdims=True)
        ddt_raw = ddt * _sigmoid(p['dt_in'])
        acc_head[...] += _rows8([jnp.sum(ddt_raw, axis=0, keepdims=True), d_a * p['a']], HPAD) + d_skip8

        @pl.when(g == 0)
        def _():
            ddt_ref[...] = ddt_raw

        @pl.when(g > 0)
        def _():
            ddt_ref[...] += ddt_raw

        for dv, dsil, ext, dext, car, acc, w_ref, o_ref in (
                (dxs, p['dsx'], ext_x, dext_x, car_x, acc_x, cwx, dxs_ref),
                (d_b, p['dsb'], ext_b, dext_b, car_b, acc_b, cwb, db_ref),
                (d_c, p['dsc'], ext_c, dext_c, car_c, acc_c, cwc, dc_ref)):
            dp = dv * dsil
            width = dp.shape[1]
            rows = [jnp.sum(ext[pl.ds(5 + k, CHUNK), :] * dp, axis=0, keepdims=True) for k in range(SSM_CONV)]
            rows.append(jnp.sum(dp, axis=0, keepdims=True))
            acc[g] += _rows8(rows, width)
            dext[0:CHUNK, :] = dp
            dext[CHUNK:CHUNK + 8, :] = car[g]
            car[g] = dext[0:8, :]
            dx = w_ref[SSM_CONV - 1:SSM_CONV, :] * dext[pl.ds(0, CHUNK), :]
            for k in range(SSM_CONV - 1):
                dx = dx + w_ref[k:k + 1, :] * dext[pl.ds(SSM_CONV - 1 - k, CHUNK), :]
            o_ref[...] = dx.astype(BF16)

    def body(*refs):
        ins = refs[:n_in]
        dy_all, ys_all, hs_all, duv_ref = refs[n_in:n_in + 4]
        hins = refs[n_in + 4:n_in + 4 + nh_in]
        o0 = n_in + 4 + nh_in
        dproj_ref, ddt_ref = refs[o0:o0 + 2]
        accs = refs[o0 + 2:o0 + 7]
        houts = refs[o0 + 7:o0 + 7 + nh_out]
        scratch = refs[o0 + 7 + nh_out:-2]
        sems = refs[-2:]
        cc = pl.program_id(0)
        _host_phase(hosted, 'start', cc == 0, hins, houts, sems)
        dproj_ref[:, 0:2 * D_A] = duv_ref[...]

        @pl.when(cc == 0)
        def _():
            for a in tuple(accs) + tuple(scratch[:4]):
                a[...] = jnp.zeros_like(a)

        shared = tuple(scratch[:4])
        scr_a, scr_b = _ssd_two_sets(tuple(scratch[7:14]) + tuple(scratch[17:17 + N_SSD_OWN]))
        set_a = shared + tuple(scratch[4:7]) + scr_a
        set_b = shared + tuple(scratch[14:17]) + scr_b

        def group(g, own):
            slab = pl.ds(pl.multiple_of(g * D_STATE, D_STATE), D_STATE)
            outs = (_lanes(dproj_ref, OFF_Z + g * GW, GW), _lanes(dproj_ref, OFF_XS + g * GW, GW),
                    _lanes(dproj_ref, OFF_B + g * D_STATE, D_STATE), _lanes(dproj_ref, OFF_C + g * D_STATE, D_STATE),
                    ddt_ref) + tuple(accs)
            per_group(g, nc - 1 - cc, _ssd_group_refs(ins, g), _lanes(dy_all, g * GW, GW),
                      _lanes(ys_all, g * GW, GW), hs_all.at[slab, :], outs, own)

        def pair(j, carry):
            group(2 * j, set_a)
            group(2 * j + 1, set_b)
            return carry

        _ssd_chunk(ins, scr_a)
        lax.fori_loop(0, ng // 2, pair, 0)
        _host_phase(hosted, 'finish', cc == nc - 1, hins, houts, sems)

    def cmap(i):
        return nc - 1 - i

    in_specs = _ssd_in_specs(cmap) + [
        pl.BlockSpec((CHUNK, D_SSM), lambda i: (cmap(i), D_A // D_SSM)),
        pl.BlockSpec((CHUNK, D_SSM), lambda i: (cmap(i), 0)),
        pl.BlockSpec((None, ng * D_STATE, GW), lambda i: (cmap(i), 0, 0)),
        pl.BlockSpec((CHUNK, 2 * D_A), lambda i: (cmap(i), 0)),
    ]

    def full(shape):
        return pl.BlockSpec(shape, lambda i: (0,) * len(shape))

    out_specs = [
        pl.BlockSpec((CHUNK, D_MAIN), lambda i: (cmap(i), 0)),
        pl.BlockSpec((CHUNK, HPAD), lambda i: (cmap(i), 0)),
        full((ng, 8, GW)), full((ng, 8, D_STATE)), full((ng, 8, D_STATE)),
        full((ng, 8, GW)), full((8, HPAD)),
    ]
    out_shape = [
        jax.ShapeDtypeStruct((t, D_MAIN), BF16),
        jax.ShapeDtypeStruct((t, HPAD), F32),
        jax.ShapeDtypeStruct((ng, 8, GW), F32), jax.ShapeDtypeStruct((ng, 8, D_STATE), F32),
        jax.ShapeDtypeStruct((ng, 8, D_STATE), F32), jax.ShapeDtypeStruct((ng, 8, GW), F32),
        jax.ShapeDtypeStruct((8, HPAD), F32),
    ]
    scratch = [pltpu.VMEM((ng * D_STATE, GW), F32),
               pltpu.VMEM((ng, 8, GW), F32), pltpu.VMEM((ng, 8, D_STATE), F32), pltpu.VMEM((ng, 8, D_STATE), F32),
               pltpu.VMEM((CHUNK + 8, GW), F32), pltpu.VMEM((CHUNK + 8, D_STATE), F32),
               pltpu.VMEM((CHUNK + 8, D_STATE), F32)] + _ssd_scratch()
    scratch += scratch[4:7] + _ssd_scratch()[:N_SSD_OWN]
    h_in, h_out, h_shape, h_scratch, h_alias = _host_plumbing(hosted, n_in + 4, len(out_shape))
    return pl.pallas_call(
        body, name="ssd_bwd", grid=(nc,), in_specs=in_specs + h_in, out_specs=out_specs + h_out,
        out_shape=out_shape + h_shape, scratch_shapes=scratch + h_scratch, input_output_aliases=h_alias,
        compiler_params=_cparams(("arbitrary",)),
    )(proj, proj, proj, proj, proj, dt_raw, cw, cb, dtb, alog, de, gain,
      tri_c, eh_c, e128_c, dyab, ysave, hs, duv, *hosted['arrays'])


ANY = pl.BlockSpec(memory_space=pl.ANY)


def _place():
    x, y, c = lax.axis_index("x"), lax.axis_index("y"), lax.axis_index("c")
    chips = [(1 - x, y), (x, 1 - y), (1 - x, 1 - y)]
    return x, y, c, chips


def _rcopy(src, dst, send_sems, recv_sems, k, dev):
    return pltpu.make_async_remote_copy(src_ref=src, dst_ref=dst, send_sem=send_sems.at[k],
                                        recv_sem=recv_sems.at[k], device_id=dev, device_id_type=MESH)


def _my_chip():
    return 2 * lax.axis_index("x") + lax.axis_index("y")


def _cast_into_slot(w, name):
    r, c = w.shape
    tr = _row_tile(r, c, 2)

    def body(w_ref, o_ref):
        o_ref[...] = w_ref[...].astype(BF16)

    return pl.pallas_call(
        body, name=name, grid=(r // tr,), in_specs=[pl.BlockSpec((tr, c), lambda i: (i, 0))],
        out_specs=pl.BlockSpec((None, tr, c), lambda i: (_my_chip(), i, 0)),
        out_shape=jax.ShapeDtypeStruct((N_CHIPS, r, c), BF16), compiler_params=_cparams(("parallel",)),
    )(w)


def _hosted_gather(bigs):
    nb = len(bigs)

    def rows(a, c):
        half = bigs[a].shape[1] // 2
        return pl.ds(c * half, half)

    def start(ins, outs, send_sems, recv_sems):
        x, y, c, chips = _place()
        q = 2 * x + y
        for a in range(nb):
            for k, chip in enumerate(chips):
                _rcopy(outs[a].at[q, rows(a, c)], outs[a].at[q, rows(a, c)], send_sems, recv_sems, 6 * a + k,
                       (chip[0], chip[1], c)).start()

    def mid(ins, outs, send_sems, recv_sems):
        x, y, c, chips = _place()
        sib = (x, y, 1 - c)
        for a in range(nb):
            for k, chip in enumerate(chips):
                slab = outs[a].at[2 * chip[0] + chip[1], rows(a, c)]
                _rcopy(slab, slab, send_sems, recv_sems, 6 * a + k, sib).wait_recv()
                _rcopy(slab, slab, send_sems, recv_sems, 6 * a + 3 + k, sib).start()

    def finish(ins, outs, send_sems, recv_sems):
        x, y, c, chips = _place()
        q = 2 * x + y
        sib = (x, y, 1 - c)
        for a in range(nb):
            for k, chip in enumerate(chips):
                qk = 2 * chip[0] + chip[1]
                other = outs[a].at[qk, rows(a, 1 - c)]
                _rcopy(other, other, send_sems, recv_sems, 6 * a + 3 + k, sib).wait_recv()
                mine = outs[a].at[q, rows(a, c)]
                _rcopy(mine, mine, send_sems, recv_sems, 6 * a + k, sib).wait_send()
                fwd = outs[a].at[qk, rows(a, c)]
                _rcopy(fwd, fwd, send_sems, recv_sems, 6 * a + 3 + k, sib).wait_send()

    return dict(arrays=list(bigs), out_shape=[jax.ShapeDtypeStruct(b.shape, b.dtype) for b in bigs],
                aliases={a: a for a in range(nb)}, nsem=6 * nb, start=start, mid=mid, finish=finish)


def _hosted_rs_chips(ps):
    na = len(ps)

    def copies(ins, outs, send_sems, recv_sems):
        x, y, c, chips = _place()
        return [_rcopy(ins[a].at[2 * chip[0] + chip[1]], outs[a].at[k], send_sems, recv_sems, 3 * a + k,
                       (chip[0], chip[1], c)) for a in range(na) for k, chip in enumerate(chips)]

    def start(ins, outs, send_sems, recv_sems):
        for cp in copies(ins, outs, send_sems, recv_sems):
            cp.start()

    def finish(ins, outs, send_sems, recv_sems):
        for cp in copies(ins, outs, send_sems, recv_sems):
            cp.wait()

    return dict(arrays=list(ps), out_shape=[jax.ShapeDtypeStruct((3,) + p.shape[1:], p.dtype) for p in ps],
                aliases={}, nsem=3 * na, start=start, mid=None, finish=finish)


def _hosted_rs_sibling(gs):
    na = len(gs)

    def copies(ins, outs, send_sems, recv_sems):
        x, y, c, _ = _place()
        halves = [g.shape[1] // 2 for g in gs]
        return [_rcopy(ins[a].at[:, pl.ds((1 - c) * halves[a], halves[a]), :], outs[a], send_sems, recv_sems, a,
                       (x, y, 1 - c)) for a in range(na)]

    def start(ins, outs, send_sems, recv_sems):
        for cp in copies(ins, outs, send_sems, recv_sems):
            cp.start()

    def finish(ins, outs, send_sems, recv_sems):
        for cp in copies(ins, outs, send_sems, recv_sems):
            cp.wait()

    return dict(arrays=list(gs), aliases={}, nsem=na, start=start, mid=None, finish=finish,
                out_shape=[jax.ShapeDtypeStruct((N_CHIPS, g.shape[1] // 2, g.shape[2]), g.dtype) for g in gs])


def _hosted_share_sibling(fs):
    na = len(fs)

    def rows(a, c):
        half = fs[a].shape[0] // 2
        return pl.ds(c * half, half)

    def start(ins, outs, send_sems, recv_sems):
        x, y, c, _ = _place()
        for a in range(na):
            _rcopy(outs[a].at[rows(a, c)], outs[a].at[rows(a, c)], send_sems, recv_sems, a, (x, y, 1 - c)).start()

    def finish(ins, outs, send_sems, recv_sems):
        x, y, c, _ = _place()
        for a in range(na):
            _rcopy(outs[a].at[rows(a, c)], outs[a].at[rows(a, c)], send_sems, recv_sems, a, (x, y, 1 - c)).wait_send()
            other = outs[a].at[rows(a, 1 - c)]
            _rcopy(other, other, send_sems, recv_sems, a, (x, y, 1 - c)).wait_recv()

    return dict(arrays=list(fs), out_shape=[jax.ShapeDtypeStruct(f.shape, f.dtype) for f in fs],
                aliases={a: a for a in range(na)}, nsem=na, start=start, mid=None, finish=finish)


def _hosted_allgather(buf):
    def copies(ins, outs, send_sems, recv_sems):
        x, y, c, _ = _place()
        me = 4 * x + 2 * y + c
        cps = []
        for k in range(1, 8):
            dev = (1 - x if k & 4 else x, 1 - y if k & 2 else y, 1 - c if k & 1 else c)
            cps.append(_rcopy(ins[0], outs[0].at[me], send_sems, recv_sems, k - 1, dev))
        return pltpu.make_async_copy(ins[0], outs[0].at[me], send_sems.at[7]), cps

    def start(ins, outs, send_sems, recv_sems):
        loc, cps = copies(ins, outs, send_sems, recv_sems)
        loc.start()
        for cp in cps:
            cp.start()

    def finish(ins, outs, send_sems, recv_sems):
        loc, cps = copies(ins, outs, send_sems, recv_sems)
        loc.wait()
        for cp in cps:
            cp.wait()

    return dict(arrays=[buf], out_shape=[jax.ShapeDtypeStruct((8,) + buf.shape, buf.dtype)], aliases={},
                nsem=8, start=start, mid=None, finish=finish)


class _SemWindow:
    def __init__(self, sems, off):
        self._sems, self._off = sems, off

    @property
    def at(self):
        return self

    def __getitem__(self, k):
        return self._sems.at[k + self._off]


def _hosted_join(parts):
    arrays, out_shape, aliases, spans, nsem = [], [], {}, [], 0
    for h in parts:
        spans.append((len(arrays), len(h['arrays']), len(out_shape), len(h['out_shape']), nsem))
        aliases.update({len(arrays) + a: len(out_shape) + b for a, b in h['aliases'].items()})
        arrays += h['arrays']
        out_shape += h['out_shape']
        nsem += h['nsem']

    def phase(name):
        if all(h[name] is None for h in parts):
            return None

        def run(ins, outs, send_sems, recv_sems):
            for h, (ia, na, io, no, s0) in zip(parts, spans):
                if h[name] is not None:
                    h[name](ins[ia:ia + na], outs[io:io + no], _SemWindow(send_sems, s0), _SemWindow(recv_sems, s0))

        return run

    return dict(arrays=arrays, out_shape=out_shape, aliases=aliases, nsem=nsem,
                start=phase('start'), mid=phase('mid'), finish=phase('finish'))


def _run_hosted(hosted, name):
    nh_in, nh_out = len(hosted['arrays']), len(hosted['out_shape'])

    def body(*refs):
        ins, outs, sems = refs[:nh_in], refs[nh_in:nh_in + nh_out], refs[-2:]
        for ph in ('start', 'mid', 'finish'):
            if hosted[ph] is not None:
                hosted[ph](ins, outs, sems[0], sems[1])

    h_in, h_out, h_shape, h_scratch, h_alias = _host_plumbing(hosted, 0, 0)
    return pl.pallas_call(body, name=name, in_specs=h_in, out_specs=h_out, out_shape=h_shape,
                          scratch_shapes=h_scratch, input_output_aliases=h_alias)(*hosted['arrays'])


def _host_plumbing(hosted, n_in, n_out):
    nh = len(hosted['arrays'])
    return ([ANY] * nh, [ANY] * len(hosted['out_shape']), list(hosted['out_shape']),
            [pltpu.SemaphoreType.DMA((hosted['nsem'],)), pltpu.SemaphoreType.DMA((hosted['nsem'],))],
            {n_in + a: n_out + b for a, b in hosted['aliases'].items()})


def _host_phase(hosted, phase, when, hins, houts, sems):
    fn = hosted[phase]
    if fn is None:
        return

    @pl.when(when)
    def _():
        fn(hins, houts, sems[0], sems[1])


def _gather_weights(bigs, smalls):
    nb, ns = len(bigs), len(smalls)
    na = nb + ns
    nsem = 6 * nb + 3 * ns
    big = _hosted_gather(bigs)

    def body(*refs):
        ins, outs = refs[:na], refs[na:2 * na]
        send_sems, recv_sems, loc_sems = refs[2 * na:]
        x, y, c, chips = _place()
        q = 2 * x + y
        sib = (x, y, 1 - c)
        locs, sends = [], []
        for s in range(ns):
            cp = pltpu.make_async_copy(ins[nb + s], outs[nb + s].at[q], loc_sems.at[s])
            cp.start()
            locs.append(cp)
        big['start'](ins[:nb], outs[:nb], send_sems, recv_sems)
        for s in range(ns):
            a = nb + s
            for k, chip in enumerate(chips):
                cp = _rcopy(ins[a], outs[a].at[q], send_sems, recv_sems, 6 * nb + 3 * s + k,
                            (chip[0], chip[1], c))
                cp.start()
                sends.append(cp)
        big['mid'](ins[:nb], outs[:nb], send_sems, recv_sems)
        big['finish'](ins[:nb], outs[:nb], send_sems, recv_sems)
        for s in range(ns):
            a = nb + s
            for k, chip in enumerate(chips):
                qk = 2 * chip[0] + chip[1]
                _rcopy(ins[a], outs[a].at[qk], send_sems, recv_sems, 6 * nb + 3 * s + k, sib).wait_recv()
        for cp in sends:
            cp.wait_send()
        for cp in locs:
            cp.wait()

    arrs = list(bigs) + list(smalls)
    out_shape = ([jax.ShapeDtypeStruct(a.shape, a.dtype) for a in bigs]
                 + [jax.ShapeDtypeStruct((N_CHIPS,) + a.shape, a.dtype) for a in smalls])
    return pl.pallas_call(
        body, name="gather_weights", in_specs=[ANY] * na, out_specs=[ANY] * na, out_shape=out_shape,
        input_output_aliases={a: a for a in range(nb)},
        scratch_shapes=[pltpu.SemaphoreType.DMA((nsem,)), pltpu.SemaphoreType.DMA((nsem,)),
                        pltpu.SemaphoreType.DMA((max(ns, 1),))],
    )(*arrs)


EW_VMEM_BUDGET = 24 * 1024 * 1024


def _row_tile(rows, cols, nbuf):
    budget = EW_VMEM_BUDGET // (nbuf * cols * 4 * 2)
    return _tile(rows, max(16, budget - budget % 16), 16) if rows % 16 == 0 else rows


def _add_pairs(g, rcv, name):
    s, half, c = rcv.shape
    tr = _row_tile(half, c, 3)
    nh = half // tr

    def body(a_ref, b_ref, o_ref):
        o_ref[...] = (a_ref[...].astype(F32) + b_ref[...].astype(F32)).astype(o_ref.dtype)

    blk = pl.BlockSpec((None, tr, c), lambda j, i: (j, i, 0))
    mine = pl.BlockSpec((None, tr, c), lambda j, i: (j, lax.axis_index("c") * nh + i, 0))
    return pl.pallas_call(
        body, name=name, grid=(s, nh), in_specs=[mine, blk], out_specs=blk,
        out_shape=jax.ShapeDtypeStruct(rcv.shape, rcv.dtype), compiler_params=_cparams(("parallel", "parallel")),
    )(g, rcv)


def _sum_chips(part, rcv, name):
    _, half, c = part.shape
    tr = _row_tile(half, c, 5)
    nh = half // tr

    def body(o_ref, r_ref, out_ref):
        acc = o_ref[...].astype(F32)
        for k in range(3):
            acc = acc + r_ref[k].astype(F32)
        out_ref[...] = acc

    return pl.pallas_call(
        body, name=name, grid=(nh,),
        in_specs=[pl.BlockSpec((None, tr, c), lambda i: (_my_chip(), i, 0)),
                  pl.BlockSpec((3, tr, c), lambda i: (0, i, 0))],
        out_specs=pl.BlockSpec((tr, c), lambda i: (lax.axis_index("c") * nh + i, 0)),
        out_shape=jax.ShapeDtypeStruct((2 * half, c), F32), compiler_params=_cparams(("parallel",)),
    )(part, rcv)


def _sum_devices(parts, name):
    _, n, _ = parts.shape
    tr = n if n <= 4096 else _tile(n, 512, 8)

    def body(p_ref, o_ref):
        acc = p_ref[0]
        for k in range(1, 8):
            acc = acc + p_ref[k]
        o_ref[...] = acc

    return pl.pallas_call(
        body, name=name, grid=(n // tr,),
        in_specs=[pl.BlockSpec((8, tr, 128), lambda i: (0, i, 0))],
        out_specs=pl.BlockSpec((tr, 128), lambda i: (i, 0)),
        out_shape=jax.ShapeDtypeStruct((n, 128), F32), compiler_params=_cparams(("parallel",)),
    )(parts)


def _adamw(w, g, m, v, name):
    r, c = w.shape
    tr = _row_tile(r, c, 7)
    c1 = 1.0 - ADAM_B1 ** ADAM_STEP
    c2 = 1.0 - ADAM_B2 ** ADAM_STEP

    def body(w_ref, g_ref, m_ref, v_ref, d_ref, mo_ref, vo_ref):
        gv = g_ref[...]
        mn = ADAM_B1 * m_ref[...] + (1.0 - ADAM_B1) * gv
        vn = ADAM_B2 * v_ref[...] + (1.0 - ADAM_B2) * (gv * gv)
        mo_ref[...] = mn
        vo_ref[...] = vn
        m_hat = mn / c1
        v_hat = vn / c2
        d_ref[...] = -ADAM_LR * (m_hat / (jnp.sqrt(v_hat) + ADAM_EPS) + ADAM_WD * w_ref[...])

    blk = pl.BlockSpec((tr, c), lambda i: (i, 0))
    sh = jax.ShapeDtypeStruct((r, c), F32)
    return pl.pallas_call(
        body, name=name, grid=(r // tr,), in_specs=[blk] * 4, out_specs=[blk] * 3, out_shape=[sh] * 3,
        compiler_params=_cparams(("parallel",)),
    )(w, g, m, v)


WEIGHTS = ['norm_mix_g', 'w_in', 'ln_a_g', 'ln_a_b', 'w_s', 'b_s', 'norm_a_g', 'conv_ssm_w', 'conv_ssm_b',
           'dt_bias', 'a_log', 'd_skip', 'ssm_norm_g', 'w_out', 'norm_ffn_g', 'w_up', 'conv_ffn_w',
           'conv_ffn_b', 'w_down', 'norm_ple_g', 'w_ple_gate', 'w_ple', 'norm_final_g']
BIG = ['w_in', 'w_out', 'w_up', 'w_down', 'w_ple_gate', 'w_ple']
SMALL = [n for n in WEIGHTS if n not in BIG]
PACK_ALIGN = 2048


def _pack(arrs):
    parts = []
    for a in arrs:
        f = a.reshape(-1).astype(F32)
        parts.append(jnp.pad(f, (0, (-f.shape[0]) % PACK_ALIGN)))
    return jnp.concatenate(parts).reshape(-1, 128)


def _unpack(buf, shapes):
    flat = buf.reshape(-1)
    out, off = [], 0
    for s in shapes:
        n = math.prod(s)
        out.append(flat[off:off + n].reshape(s))
        off += n + (-n) % PACK_ALIGN
    return out


def _pad_heads(v):
    return jnp.pad(v, ((0, 0), (0, HPAD - v.shape[1])))


def _col_sharded(full):
    r, c4 = full.shape
    return jnp.transpose(full.reshape(r, N_CHIPS, c4 // N_CHIPS), (1, 0, 2))


def _from_col_sharded(g):
    s, r, c = g.shape
    return jnp.transpose(g, (1, 0, 2)).reshape(r, s * c)


def kernel(x, p, norm_mix_g, w_in, ln_a_g, ln_a_b, w_s, b_s, norm_a_g, conv_ssm_w, conv_ssm_b, dt_bias, a_log, d_skip, ssm_norm_g, w_out, norm_ffn_g, w_up, conv_ffn_w, conv_ffn_b, w_down, norm_ple_g, w_ple_gate, w_ple, norm_final_g, loss_target, m_norm_mix_g, m_w_in, m_ln_a_g, m_ln_a_b, m_w_s, m_b_s, m_norm_a_g, m_conv_ssm_w, m_conv_ssm_b, m_dt_bias, m_a_log, m_d_skip, m_ssm_norm_g, m_w_out, m_norm_ffn_g, m_w_up, m_conv_ffn_w, m_conv_ffn_b, m_w_down, m_norm_ple_g, m_w_ple_gate, m_w_ple, m_norm_final_g, v_norm_mix_g, v_w_in, v_ln_a_g, v_ln_a_b, v_w_s, v_b_s, v_norm_a_g, v_conv_ssm_w, v_conv_ssm_b, v_dt_bias, v_a_log, v_d_skip, v_ssm_norm_g, v_w_out, v_norm_ffn_g, v_w_up, v_conv_ffn_w, v_conv_ffn_b, v_w_down, v_norm_ple_g, v_w_ple_gate, v_w_ple, v_norm_final_g):
    given = dict(locals())
    wts = {n: given[n] for n in WEIGHTS}
    mom = {n: given['m_' + n] for n in WEIGHTS}
    var = {n: given['v_' + n] for n in WEIGHTS}
    d = D_MODEL
    xt, pt, tgt = x[0], p[0, 0], loss_target[0]
    chip = 2 * lax.axis_index("x") + lax.axis_index("y")

    slots = {n: _cast_into_slot(wts[n][0], "cast_" + n) for n in BIG}
    g_in, g_cs, g_cf = _gather_weights([slots['w_in']], [conv_ssm_w[0], conv_ffn_w[0]])
    w_in_full = _from_col_sharded(g_in)
    w_main = w_in_full
    w_dt = _pad_heads(w_in_full[:, D_MAIN:])
    cs_w = _from_col_sharded(g_cs)
    cf_w = _from_col_sharded(g_cf)
    consts = _ssd_consts()
    dtb, alog = _pad_heads(dt_bias), _pad_heads(a_log)
    de = jnp.repeat(d_skip[0], HEAD_DIM)[None, :]
    b_exp = jnp.broadcast_to(b_s[0][:, :, None], (N_GROUPS_A, CHUNK, CHUNK))

    a1 = _rms_fwd(xt, norm_mix_g, "rms_mix")
    proj, g_out = _matmul(a1, w_main, mode='nn', name="mm_proj", tm=1024, tn=1024, b_cols=D_MAIN,
                          hosted=_hosted_gather([slots['w_out']]))
    dt_raw = _matmul(a1, w_dt, mode='nn', name="mm_dt", tm=1024, tn=128)
    yab = _gmlp_fwd(proj, ln_a_g, ln_a_b, w_s[0], b_exp, norm_a_g)
    yab, ysave, hs, g_up = _ssd_fwd(proj, dt_raw, yab, cs_w, conv_ssm_b, dtb, alog, de, ssm_norm_g, consts,
                                    _hosted_gather([slots['w_up']]))
    w_out_f = g_out.reshape(D_MIX, d)
    h1 = _matmul(yab, w_out_f, mode='nn', name="mm_out", res=xt, tm=512, tn=1024, tk=4096)
    f = _rms_fwd(h1, norm_ffn_g, "rms_ffn")
    hid, g_down, g_pg, g_ple = _matmul(
        f, g_up, mode='nn', name="mm_up", b_sharded=True, tm=1024, tn=1408,
        hosted=_hosted_gather([slots[n] for n in ('w_down', 'w_ple_gate', 'w_ple')]))
    w_down_f = g_down.reshape(D_FF, d)
    w_pg_f = g_pg.reshape(d, d)
    act, conv_g, conv_u = _ffn_act_fwd(hid, cf_w, conv_ffn_b)
    h2 = _matmul(act, w_down_f, mode='nn', name="mm_down", res=h1, tm=1024, tn=1024, tk=2816)
    n3 = _rms_fwd(h2, norm_ple_g, "rms_ple")
    gl = _matmul(n3, w_pg_f, mode='nn', name="mm_pg", tm=1024, tn=1024)
    pe = _matmul(pt, g_ple, mode='nn', name="mm_ple", b_sharded=True, tm=1024, tn=512)
    dh3, dgl, dpe, lossv, dgf = _tail(h2, gl, pe, tgt, norm_final_g[None, :])

    gs_ple = _matmul(pt, dpe, mode='tn', name="mm_dw_ple", out_dtype=BF16, out_shards=N_CHIPS,
                     tm=256, tn=512, tk=2048)
    gs_pg = _matmul(n3, dgl, mode='tn', name="mm_dw_pg", out_dtype=BF16, tm=1024, tn=1024, tk=4096)
    dn3 = _matmul(dgl, w_pg_f, mode='nt', name="mm_dn3", out_dtype=BF16, tm=1024, tn=1024)
    dh2, dg_ple, dh2_b = _rms_bwd(h2, norm_ple_g, dn3, dh3, "rms_ple_bwd", True)
    dact = _matmul(dh2_b, w_down_f, mode='nt', name="mm_dact", out_dtype=BF16, tm=1024, tn=1408)
    gs_down = _matmul(act, dh2_b, mode='tn', name="mm_dw_down", out_dtype=BF16, tm=1408, tn=1024, tk=2048)
    dpg, dpu, wg_acc, wu_acc = _ffn_act_bwd(hid, cf_w, conv_g, conv_u, dact)
    hc = N_CHIPS // 2
    gs_up = _matmul(f, dpg, mode='tn', name="mm_dw_up_g", out_dtype=BF16, out_shards=hc, out_total=N_CHIPS,
                    tm=1024, tn=1408, tk=2048)
    gs_up = _matmul(f, dpu, mode='tn', name="mm_dw_up_u", out_dtype=BF16, out_shards=hc, tm=1024, tn=1408, tk=2048,
                    out_into=(gs_up, hc))
    early = [gs_up, gs_down.reshape(N_CHIPS, D_FF // N_CHIPS, d), gs_pg.reshape(N_CHIPS, d // N_CHIPS, d), gs_ple]
    df, *sib_e = _matmul(dpg, g_up, mode='nt', name="mm_df_g", b_sharded=True, tm=1024, tn=1024, tk=2816,
                         hosted=_hosted_rs_sibling(early))
    part_e = [_add_pairs(a, b, "rs_add_e%d" % i) for i, (a, b) in enumerate(zip(early, sib_e))]
    df = _matmul(dpu, g_up, mode='nt', name="mm_df_u", b_sharded=True, b_shard_off=hc, res=df, out_dtype=BF16,
                 tm=1024, tn=1024, tk=2816)
    dh1, dg_ffn, dh1_b = _rms_bwd(h1, norm_ffn_g, df, dh2, "rms_ffn_bwd", True)
    dyab = _matmul(dh1_b, w_out_f, mode='nt', name="mm_dyab", out_dtype=BF16, tm=1024, tn=1024)
    gs_out = _matmul(yab, dh1_b, mode='tn', name="mm_dw_out", out_dtype=BF16, tm=1024, tn=1024, tk=4096)
    duv, dws, dbs, dlng, dlnb, dnag = _gmlp_bwd(proj, dyab, ln_a_g, ln_a_b, w_s[0], b_exp, norm_a_g)
    dproj, ddt_raw, acc_x, acc_b, acc_c, acc_gain, acc_head, *rcv_e = _ssd_bwd(
        proj, dt_raw, dyab, ysave, hs, duv, cs_w, conv_ssm_b, dtb, alog, de, ssm_norm_g, consts,
        _hosted_rs_chips(part_e))
    dw_main = _matmul(a1, dproj, mode='tn', name="mm_dw_main", out_dtype=BF16, tm=1024, tn=1024, tk=4096)
    dw_dt = _matmul(a1, ddt_raw, mode='tn', name="mm_dw_dt", out_dtype=BF16, tm=1024, tn=128, tk=2048)
    gs_in = _col_sharded(jnp.concatenate([dw_main, dw_dt[:, :N_HEADS]], axis=1))
    late = [gs_in, gs_out.reshape(N_CHIPS, D_MIX // N_CHIPS, d)]
    da_dt, *sib_l = _matmul(ddt_raw, w_dt, mode='nt', name="mm_da_dt", tm=1024, tn=1024,
                            hosted=_hosted_rs_sibling(late))
    part_l = [_add_pairs(a, b, "rs_add_l%d" % i) for i, (a, b) in enumerate(zip(late, sib_l))]

    def conv_rows(acc, k):
        return acc[:, k, :].reshape(1, -1)

    dcw = jnp.concatenate([jnp.concatenate([conv_rows(acc_x, k), conv_rows(acc_b, k), conv_rows(acc_c, k)], axis=1)
                           for k in range(SSM_CONV)], axis=0)
    dcb = jnp.concatenate([conv_rows(acc_x, SSM_CONV), conv_rows(acc_b, SSM_CONV), conv_rows(acc_c, SSM_CONV)], axis=1)
    part = {
        'ln_a_g': dlng, 'ln_a_b': dlnb, 'w_s': dws, 'b_s': dbs, 'norm_a_g': dnag,
        'conv_ssm_w': dcw, 'conv_ssm_b': dcb,
        'dt_bias': acc_head[0:1, :N_HEADS], 'a_log': acc_head[1:2, :N_HEADS], 'd_skip': acc_head[2:3, :N_HEADS],
        'ssm_norm_g': acc_gain[:, 0, :], 'norm_ffn_g': dg_ffn,
        'conv_ffn_w': jnp.concatenate([wg_acc[:FFN_CONV], wu_acc[:FFN_CONV]], axis=1),
        'conv_ffn_b': jnp.concatenate([wg_acc[FFN_CONV:FFN_CONV + 1], wu_acc[FFN_CONV:FFN_CONV + 1]], axis=1),
        'norm_ple_g': dg_ple, 'norm_final_g': dgf,
    }
    full_shapes = {n: wts[n].shape for n in SMALL}
    full_shapes['conv_ssm_w'] = (1, SSM_CONV, D_XBC)
    full_shapes['conv_ffn_w'] = (1, FFN_CONV, 2 * D_FF)
    small_e = [n for n in SMALL if n != 'norm_mix_g']
    packed = _pack([part[n] for n in small_e] + [lossv[:, 0:1]])

    halves_e = [_sum_chips(a, b, "rs_sum_e%d" % i) for i, (a, b) in enumerate(zip(part_e, rcv_e))]
    da, *moved = _matmul(dproj, w_main, mode='nt', name="mm_da", res=da_dt, out_dtype=BF16, tm=1024, tn=1024, tk=2560,
                         hosted=_hosted_join([_hosted_rs_chips(part_l), _hosted_share_sibling(halves_e),
                                              _hosted_allgather(packed)]))
    rcv_l, g_early, gathered = moved[:len(late)], moved[len(late):-1], moved[-1]
    dx, dg_mix = _rms_bwd(xt, norm_mix_g, da, dh1, "rms_mix_bwd", False)

    halves_l = [_sum_chips(a, b, "rs_sum_l%d" % i) for i, (a, b) in enumerate(zip(part_l, rcv_l))]
    g_big = dict(zip(['w_up', 'w_down', 'w_ple_gate', 'w_ple'], g_early))
    g_big.update(zip(['w_in', 'w_out'], _run_hosted(_hosted_share_sibling(halves_l), "share_sibling_late")))

    pieces = _unpack(_sum_devices(gathered, "sum_devices"), [full_shapes[n] for n in small_e] + [(1,)])
    g_small = dict(zip(small_e, pieces[:-1]))
    loss = pieces[-1][0]
    mix = _sum_devices(_run_hosted(_hosted_allgather(_pack([dg_mix])), "allgather_mix")[0], "sum_devices_mix")
    g_small['norm_mix_g'] = _unpack(mix, [full_shapes['norm_mix_g']])[0]
    for n in ('conv_ssm_w', 'conv_ffn_w'):
        width = wts[n].shape[2]
        g_small[n] = lax.dynamic_slice_in_dim(g_small[n], chip * width, width, axis=2)

    grads, delta, new_m, new_v = {}, {}, {}, {}
    for n in BIG:
        shp = wts[n].shape
        dl, mn, vn = _adamw(wts[n][0], g_big[n], mom[n][0], var[n][0], "adamw_" + n)
        grads[n], delta[n], new_m[n], new_v[n] = (g_big[n].reshape(shp), dl.reshape(shp), mn.reshape(shp),
                                                  vn.reshape(shp))
    shapes = [wts[n].shape for n in SMALL]
    dl, mn, vn = _adamw(_pack([wts[n] for n in SMALL]), _pack([g_small[n] for n in SMALL]),
                        _pack([mom[n] for n in SMALL]), _pack([var[n] for n in SMALL]), "adamw_small")
    for n, a, b, c in zip(SMALL, _unpack(dl, shapes), _unpack(mn, shapes), _unpack(vn, shapes)):
        grads[n], delta[n], new_m[n], new_v[n] = g_small[n], a, b, c

    return (loss, dx[None], *[grads[n] for n in WEIGHTS], *[delta[n] for n in WEIGHTS],
            *[new_m[n] for n in WEIGHTS], *[new_v[n] for n in WEIGHTS])
```

```python
import math

import jax
import jax.numpy as jnp
from jax import lax
from jax.experimental import pallas as pl
from jax.experimental.pallas import tpu as pltpu

D_MODEL = 2048
SEQ = 8192
D_MIX = 2 * D_MODEL
D_A = D_MIX // 2
CHUNK = 128
N_GROUPS_A = D_A // 128
D_SSM = D_MIX - D_A
HEAD_DIM = 64
N_HEADS = D_SSM // HEAD_DIM
HEADS_PER_GROUP = 4
N_SSM_GROUPS = N_HEADS // HEADS_PER_GROUP
GW = HEADS_PER_GROUP * HEAD_DIM
D_STATE = 128
SSM_CONV = 4
D_BC = N_SSM_GROUPS * D_STATE
D_XBC = D_SSM + 2 * D_BC
D_MAIN = 2 * D_A + D_SSM + D_XBC
D_IN = D_MAIN + N_HEADS
D_FF = (D_MODEL * 11) // 4
FFN_CONV = 3
D_PLE = 256
EPS = 1e-6
HPAD = 128
N_CHIPS = 4

ADAM_LR = 0.001
ADAM_B1 = 0.9
ADAM_B2 = 0.999
ADAM_EPS = 1e-08
ADAM_WD = 0.01
ADAM_STEP = 10

F32 = jnp.float32
BF16 = jnp.bfloat16
MESH = pl.DeviceIdType.MESH
VMEM_LIMIT = 56 * 1024 * 1024


def _cparams(sem):
    return pltpu.CompilerParams(dimension_semantics=sem, vmem_limit_bytes=VMEM_LIMIT)


def _forward_step(n_steps):
    return (7 * n_steps) // 8


def _tile(n, pref, mult):
    t = min(pref, n)
    t -= t % mult
    while n % t:
        t -= mult
    return t


def _dot(a, b):
    return jnp.dot(a, b, preferred_element_type=F32)


def _dot_nt(a, b):
    return lax.dot_general(a, b, (((1,), (1,)), ((), ())), preferred_element_type=F32)


def _dot_tn(a, b):
    return lax.dot_general(a, b, (((0,), (0,)), ((), ())), preferred_element_type=F32)


def _split3(x):
    hi = x.astype(BF16)
    r = x - hi.astype(F32)
    mid = r.astype(BF16)
    lo = (r - mid.astype(F32)).astype(BF16)
    return hi, mid, lo


def _x01(x, e):
    h, m, l = _split3(x)
    return _dot(h, e) + _dot(m, e) + _dot(l, e)


def _x01_nt(x, e):
    h, m, l = _split3(x)
    return _dot_nt(h, e) + _dot_nt(m, e) + _dot_nt(l, e)


def _e01x(e, x):
    h, m, l = _split3(x)
    return _dot(e, h) + _dot(e, m) + _dot(e, l)


def _e01x_tn(e, x):
    h, m, l = _split3(x)
    return _dot_tn(e, h) + _dot_tn(e, m) + _dot_tn(e, l)


def _sigmoid(x):
    return 1.0 / (1.0 + jnp.exp(-x))


_GELU_C = math.sqrt(2.0 / math.pi)


def _gelu_and_grad(x):
    x2 = x * x
    th = jnp.tanh(_GELU_C * (x + 0.044715 * x * x2))
    y = 0.5 * x * (1.0 + th)
    dy = 0.5 * (1.0 + th) + 0.5 * x * (1.0 - th * th) * (_GELU_C * (1.0 + 3.0 * 0.044715 * x2))
    return y, dy


def _silu_and_grad(x):
    s = _sigmoid(x)
    return x * s, s * (1.0 + x * (1.0 - s))


def _softplus(x):
    u = jnp.exp(-jnp.abs(x))
    w = 1.0 + u
    l1p = jnp.where(w == 1.0, u, jnp.log(w) * (u / (w - 1.0)))
    return jnp.maximum(x, 0.0) + l1p


def _matmul(a, b, *, mode, name, out_dtype=F32, res=None, tm=512, tn=512, tk=2048,
            b_sharded=False, b_shard_off=0, b_cols=None, out_shards=0, out_total=0, out_into=None, hosted=None):
    if mode == 'tn':
        kdim, m = a.shape
        n = b.shape[1]
    else:
        m, kdim = a.shape
        if b_sharded:
            s_b, d1, d2 = b.shape
            n = s_b * d2 if mode == 'nn' else d1
        else:
            n = b.shape[1] if mode == 'nn' else b.shape[0]
        if b_cols is not None:
            n = b_cols
    per = None
    if b_sharded:
        per = b.shape[2]
    if out_shards:
        per = n // out_shards
    tm = _tile(m, tm, 128 if mode == 'tn' else 8)
    if mode == 'nt' and b_sharded:
        tn = _tile(n, tn, 128)
        tk = _tile(per, tk, 128)
    elif per is not None:
        tn = _tile(per, tn, 128)
        tk = _tile(kdim, tk, 128 if mode != 'tn' else 8)
    else:
        tn = _tile(n, tn, 128)
        tk = _tile(kdim, tk, 128 if mode != 'tn' else 8)
    nm, nn_, nk = m // tm, n // tn, kdim // tk
    has_res = res is not None
    n_in = 2 + has_res + (out_into is not None)
    nh_in = len(hosted['arrays']) if hosted else 0
    nh_out = len(hosted['out_shape']) if hosted else 0

    def body(*refs):
        a_ref, b_ref = refs[0], refs[1]
        res_ref = refs[2] if has_res else None
        o_ref = refs[n_in + nh_in]
        if hosted:
            hins = refs[n_in:n_in + nh_in]
            houts = refs[n_in + nh_in + 1:n_in + nh_in + 1 + nh_out]
            sems = refs[-2:]
            ids = [pl.program_id(d) for d in range(3)]
            step = (ids[0] * nm + ids[1]) * nk + ids[2]
            n_steps = nn_ * nm * nk
            at_first, at_mid, at_last = step == 0, step == _forward_step(n_steps), step == n_steps - 1
            _host_phase(hosted, 'start', at_first, hins, houts, sems)
        av = a_ref[...].astype(BF16)
        bv = b_ref[...].astype(BF16)
        if mode == 'nn':
            p = _dot(av, bv)
        elif mode == 'nt':
            p = _dot_nt(av, bv)
        else:
            p = _dot_tn(av, bv)

        def fin(v):
            if has_res:
                v = v + res_ref[...]
            o_ref[...] = v.astype(o_ref.dtype)

        if nk == 1:
            fin(p)
        else:
            acc_ref = refs[n_in + nh_in + 1 + nh_out]
            k = pl.program_id(2)

            @pl.when(k == 0)
            def _():
                acc_ref[...] = p

            @pl.when(k > 0)
            def _():
                acc_ref[...] += p

            @pl.when(k == nk - 1)
            def _():
                fin(acc_ref[...])
        if hosted:
            _host_phase(hosted, 'mid', at_mid, hins, houts, sems)
            _host_phase(hosted, 'finish', at_last, hins, houts, sems)

    if mode == 'nn':
        a_spec = pl.BlockSpec((tm, tk), lambda j, i, k: (i, k))
        if b_sharded:
            nps = per // tn
            b_spec = pl.BlockSpec((None, tk, tn), lambda j, i, k: (j // nps, k, j % nps))
        else:
            b_spec = pl.BlockSpec((tk, tn), lambda j, i, k: (k, j))
    elif mode == 'nt':
        a_spec = pl.BlockSpec((tm, tk), lambda j, i, k: (i, k))
        if b_sharded:
            kps = per // tk
            b_spec = pl.BlockSpec((None, tn, tk), lambda j, i, k: (k // kps + b_shard_off, j, k % kps))
        else:
            b_spec = pl.BlockSpec((tn, tk), lambda j, i, k: (j, k))
    else:
        a_spec = pl.BlockSpec((tk, tm), lambda j, i, k: (k, i))
        b_spec = pl.BlockSpec((tk, tn), lambda j, i, k: (k, j))
    in_specs = [a_spec, b_spec]
    args = [a, b]
    if has_res:
        in_specs.append(pl.BlockSpec((tm, tn), lambda j, i, k: (i, j)))
        args.append(res)
    if out_shards:
        nps_o = per // tn
        total, off = (out_into[0].shape[0], out_into[1]) if out_into is not None else (out_total or out_shards, 0)
        out_shape = jax.ShapeDtypeStruct((total, m, per), out_dtype)
        out_spec = pl.BlockSpec((None, tm, tn), lambda j, i, k: (j // nps_o + off, i, j % nps_o))
        if out_into is not None:
            in_specs.append(pl.BlockSpec(memory_space=pl.ANY))
            args.append(out_into[0])
    else:
        out_shape = jax.ShapeDtypeStruct((m, n), out_dtype)
        out_spec = pl.BlockSpec((tm, tn), lambda j, i, k: (i, j))
    scratch = [pltpu.VMEM((tm, tn), F32)] if nk > 1 else []
    if not hosted:
        return pl.pallas_call(
            body, name=name, grid=(nn_, nm, nk), in_specs=in_specs, out_specs=out_spec,
            out_shape=out_shape, scratch_shapes=scratch,
            input_output_aliases={n_in - 1: 0} if out_into is not None else {},
            compiler_params=_cparams(("parallel", "parallel", "arbitrary")),
        )(*args)
    h_in, h_out, h_shape, h_scratch, h_alias = _host_plumbing(hosted, n_in, 1)
    return pl.pallas_call(
        body, name=name, grid=(nn_, nm, nk), in_specs=in_specs + h_in, out_specs=[out_spec] + h_out,
        out_shape=[out_shape] + h_shape, scratch_shapes=scratch + h_scratch, input_output_aliases=h_alias,
        compiler_params=_cparams(("arbitrary", "arbitrary", "arbitrary")),
    )(*args, *hosted['arrays'])


def _rms_fwd(x, g, name):
    t, d = x.shape
    tt = _tile(t, 512, 8)

    def body(x_ref, g_ref, o_ref):
        xv = x_ref[...]
        r = lax.rsqrt(jnp.mean(xv * xv, axis=-1, keepdims=True) + EPS)
        o_ref[...] = (xv * r * g_ref[...]).astype(o_ref.dtype)

    return pl.pallas_call(
        body, name=name, grid=(t // tt,),
        in_specs=[pl.BlockSpec((tt, d), lambda i: (i, 0)), pl.BlockSpec((1, d), lambda i: (0, 0))],
        out_specs=pl.BlockSpec((tt, d), lambda i: (i, 0)),
        out_shape=jax.ShapeDtypeStruct((t, d), BF16),
        compiler_params=_cparams(("parallel",)),
    )(x, g)


def _rms_bwd(x, g, dy, dres, name, also_bf16):
    t, d = x.shape
    tt = _tile(t, 512, 16)

    def body(x_ref, g_ref, dy_ref, dres_ref, dx_ref, dg_ref, *dxb):
        i = pl.program_id(0)
        xv = x_ref[...]
        r = lax.rsqrt(jnp.mean(xv * xv, axis=-1, keepdims=True) + EPS)
        xh = xv * r
        dyv = dy_ref[...].astype(F32)
        dxh = dyv * g_ref[...]
        c = jnp.mean(dxh * xh, axis=-1, keepdims=True)
        dx = dres_ref[...] + r * (dxh - xh * c)
        dx_ref[...] = dx
        for dxb_ref in dxb:
            dxb_ref[...] = dx.astype(BF16)
        part = jnp.sum(dyv * xh, axis=0, keepdims=True)

        @pl.when(i == 0)
        def _():
            dg_ref[...] = part

        @pl.when(i > 0)
        def _():
            dg_ref[...] += part

    row = pl.BlockSpec((tt, d), lambda i: (i, 0))
    vec = pl.BlockSpec((1, d), lambda i: (0, 0))
    return pl.pallas_call(
        body, name=name, grid=(t // tt,),
        in_specs=[row, vec, row, row], out_specs=[row, vec] + [row] * also_bf16,
        out_shape=[jax.ShapeDtypeStruct((t, d), F32), jax.ShapeDtypeStruct((1, d), F32)]
        + [jax.ShapeDtypeStruct((t, d), BF16)] * also_bf16,
        compiler_params=_cparams(("arbitrary",)),
    )(x, g, dy, dres)


def _tail(h2, gl, pe, target, gfin):
    t, d = h2.shape
    tt = _tile(t, 256, 16)

    def body(h2_ref, gl_ref, pe_ref, tg_ref, gf_ref, dh3_ref, dgl_ref, dpe_ref, loss_ref, dgf_ref):
        i = pl.program_id(0)
        sig = _sigmoid(gl_ref[...])
        pev = pe_ref[...]
        h3 = h2_ref[...] + sig * pev
        r = lax.rsqrt(jnp.mean(h3 * h3, axis=-1, keepdims=True) + EPS)
        xh = h3 * r
        gf = gf_ref[...]
        e = xh * gf - tg_ref[...]
        dy = e * (1.0 / d)
        dxh = dy * gf
        c = jnp.mean(dxh * xh, axis=-1, keepdims=True)
        dh3 = r * (dxh - xh * c)
        dh3_ref[...] = dh3
        dgl_ref[...] = (dh3 * pev * sig * (1.0 - sig)).astype(BF16)
        dpe_ref[...] = (dh3 * sig).astype(BF16)
        lpart = jnp.sum(e * e, axis=0, keepdims=True) * (0.5 / d)
        gpart = jnp.sum(dy * xh, axis=0, keepdims=True)

        @pl.when(i == 0)
        def _():
            loss_ref[...] = lpart
            dgf_ref[...] = gpart

        @pl.when(i > 0)
        def _():
            loss_ref[...] += lpart
            dgf_ref[...] += gpart

        @pl.when(i == t // tt - 1)
        def _():
            loss_ref[...] = jnp.broadcast_to(jnp.sum(loss_ref[...], axis=-1, keepdims=True), (1, d))

    row = pl.BlockSpec((tt, d), lambda i: (i, 0))
    vec = pl.BlockSpec((1, d), lambda i: (0, 0))
    return pl.pallas_call(
        body, name="tail", grid=(t // tt,),
        in_specs=[row, row, row, row, vec], out_specs=[row, row, row, vec, vec],
        out_shape=[jax.ShapeDtypeStruct((t, d), F32), jax.ShapeDtypeStruct((t, d), BF16),
                   jax.ShapeDtypeStruct((t, d), BF16), jax.ShapeDtypeStruct((1, d), F32),
                   jax.ShapeDtypeStruct((1, d), F32)],
        compiler_params=_cparams(("arbitrary",)),
    )(h2, gl, pe, target, gfin)


def _conv(cur_ref, prev_ref, w_ref, b_ref, ext_ref, first, width):
    rows = cur_ref.shape[0]
    ext_ref[0:8, :] = jnp.where(first, 0.0, prev_ref[...])
    ext_ref[8:8 + rows, :] = cur_ref[...]
    acc = b_ref[...]
    for k in range(width):
        acc = acc + w_ref[k:k + 1, :] * ext_ref[pl.ds(9 - width + k, rows), :]
    return acc


def _ffn_specs(t, tt, tc, nf):
    hb = tt // 8
    cur_g = pl.BlockSpec((tt, tc), lambda j, i: (i, j))
    cur_u = pl.BlockSpec((tt, tc), lambda j, i: (i, j + nf))
    prev_g = pl.BlockSpec((8, tc), lambda j, i: (jnp.maximum(i * hb - 1, 0), j))
    prev_u = pl.BlockSpec((8, tc), lambda j, i: (jnp.maximum(i * hb - 1, 0), j + nf))
    w_g = pl.BlockSpec((FFN_CONV, tc), lambda j, i: (0, j))
    w_u = pl.BlockSpec((FFN_CONV, tc), lambda j, i: (0, j + nf))
    b_g = pl.BlockSpec((1, tc), lambda j, i: (0, j))
    b_u = pl.BlockSpec((1, tc), lambda j, i: (0, j + nf))
    return [cur_g, prev_g, cur_u, prev_u, w_g, w_u, b_g, b_u]


FFN_TC = 512


def _shift_down(prev, cur, n, rid):
    return jnp.where(rid < n, pltpu.roll(prev, n, 0), pltpu.roll(cur, n, 0))


def _shift_up(cur, nxt, n, rid):
    return jnp.where(rid < 8 - n, pltpu.roll(cur, 8 - n, 0), pltpu.roll(nxt, 8 - n, 0))


def _conv3_group(prev, cur, w_ref, b_ref, rid):
    x1 = _shift_down(prev, cur, 1, rid)
    x2 = _shift_down(prev, cur, 2, rid)
    return b_ref[...] + w_ref[2:3, :] * cur + w_ref[1:2, :] * x1 + w_ref[0:1, :] * x2


def _ffn_act_fwd(hid, cw, cb):
    t = hid.shape[0]
    tt = _tile(t, 2048, 16)
    tc = _tile(D_FF, FFN_TC, 128)
    nf = D_FF // tc

    def body(g_ref, gp_ref, u_ref, up_ref, wg_ref, wu_ref, bg_ref, bu_ref, o_ref, cg_ref, cu_ref):
        first = pl.program_id(1) == 0
        rid = lax.broadcasted_iota(jnp.int32, (8, tc), 0)

        def step(s, carry):
            pg, pu = carry
            r0 = pl.multiple_of(s * 16, 16)
            g0, g1 = g_ref[pl.ds(r0, 8), :], g_ref[pl.ds(r0 + 8, 8), :]
            u0, u1 = u_ref[pl.ds(r0, 8), :], u_ref[pl.ds(r0 + 8, 8), :]
            gate = jnp.concatenate([_conv3_group(pg, g0, wg_ref, bg_ref, rid),
                                    _conv3_group(g0, g1, wg_ref, bg_ref, rid)], axis=0)
            up = jnp.concatenate([_conv3_group(pu, u0, wu_ref, bu_ref, rid),
                                  _conv3_group(u0, u1, wu_ref, bu_ref, rid)], axis=0)
            cg_ref[pl.ds(r0, 16), :] = gate.astype(BF16)
            cu_ref[pl.ds(r0, 16), :] = up.astype(BF16)
            o_ref[pl.ds(r0, 16), :] = (gate * _sigmoid(gate) * up).astype(BF16)
            return g1, u1

        init = (jnp.where(first, 0.0, gp_ref[...]), jnp.where(first, 0.0, up_ref[...]))
        lax.fori_loop(0, tt // 16, step, init, unroll=2)

    blk = pl.BlockSpec((tt, tc), lambda j, i: (i, j))
    return pl.pallas_call(
        body, name="ffn_act_fwd", grid=(nf, t // tt), in_specs=_ffn_specs(t, tt, tc, nf),
        out_specs=[blk, blk, blk],
        out_shape=[jax.ShapeDtypeStruct((t, D_FF), BF16)] * 3,
        compiler_params=_cparams(("parallel", "arbitrary")),
    )(hid, hid, hid, hid, cw, cw, cb, cb)


def _ffn_act_bwd(hid, cw, conv_g, conv_u, dact):
    t = hid.shape[0]
    tt = _tile(t, 2048, 16)
    tc = _tile(D_FF, FFN_TC, 128)
    nf = D_FF // tc
    nt = t // tt
    n16 = tt // 16

    def body(g_ref, u_ref, wg_ref, wu_ref, cg_ref, cgn_ref, cu_ref, cun_ref, da_ref, dan_ref,
             og_ref, ou_ref, ag_ref, au_ref, accs):
        i = pl.program_id(1)
        first, last = i == 0, i == nt - 1
        rid = lax.broadcasted_iota(jnp.int32, (8, tc), 0)
        accs[...] = jnp.zeros_like(accs)

        def dgroup(gate, up, da):
            sv, sgr = _silu_and_grad(gate)
            return da * up * sgr, da * sv

        def finish(x, d0, d1, w_ref):
            s1 = _shift_up(d0, d1, 1, rid)
            s2 = _shift_up(d0, d1, 2, rid)
            dpre = w_ref[2:3, :] * d0 + w_ref[1:2, :] * s1 + w_ref[0:1, :] * s2
            return dpre, (x * s2, x * s1, x * d0, d0)

        def two_groups(it, carry, gate_blk, up_blk, da_blk, zero_ahead):
            d0g, d0u, gate1, up1, da1 = carry
            r0 = it * 16 if isinstance(it, int) else pl.multiple_of(it * 16, 16)
            x0g, x0u = g_ref[pl.ds(r0, 8), :], u_ref[pl.ds(r0, 8), :]
            x1g, x1u = g_ref[pl.ds(r0 + 8, 8), :], u_ref[pl.ds(r0 + 8, 8), :]
            d1g, d1u = dgroup(gate1, up1, da1)
            d2g, d2u = dgroup(gate_blk[0:8], up_blk[0:8], da_blk[0:8])
            d2g = jnp.where(zero_ahead, 0.0, d2g)
            d2u = jnp.where(zero_ahead, 0.0, d2u)
            for half, (xa, xb, da_, db_, dc_, w_ref, o_ref) in enumerate((
                    (x0g, x1g, d0g, d1g, d2g, wg_ref, og_ref), (x0u, x1u, d0u, d1u, d2u, wu_ref, ou_ref))):
                pa, prods_a = finish(xa, da_, db_, w_ref)
                pb, prods_b = finish(xb, db_, dc_, w_ref)
                o_ref[pl.ds(r0, 16), :] = jnp.concatenate([pa, pb], axis=0).astype(BF16)
                for k in range(4):
                    accs[4 * half + k] += prods_a[k] + prods_b[k]
            return d2g, d2u, gate_blk[8:16], up_blk[8:16], da_blk[8:16]

        def rows16(ref, r):
            return ref[pl.ds(r, 16), :].astype(F32)

        def step(it, carry):
            r1 = pl.multiple_of(it * 16 + 16, 16)
            return two_groups(it, carry, rows16(cg_ref, r1), rows16(cu_ref, r1), rows16(da_ref, r1), False)

        g0, u0, da0 = rows16(cg_ref, 0), rows16(cu_ref, 0), rows16(da_ref, 0)
        d0g, d0u = dgroup(g0[0:8], u0[0:8], da0[0:8])
        carry = lax.fori_loop(0, n16 - 1, step, (d0g, d0u, g0[8:16], u0[8:16], da0[8:16]))
        two_groups(n16 - 1, carry, cgn_ref[...].astype(F32), cun_ref[...].astype(F32), dan_ref[...].astype(F32), last)

        @pl.when(first)
        def _():
            ag_ref[...] = jnp.zeros_like(ag_ref)
            au_ref[...] = jnp.zeros_like(au_ref)

        for half, a_ref in enumerate((ag_ref, au_ref)):
            for k in range(4):
                a_ref[k:k + 1, :] += jnp.sum(accs[4 * half + k], axis=0, keepdims=True)

    def nxt16(i):
        return jnp.minimum((i + 1) * n16, t // 16 - 1)

    out_blk = pl.BlockSpec((tt, tc), lambda j, i: (i, j))
    acc_spec = pl.BlockSpec((8, tc), lambda j, i: (0, j))
    nxt_blk = pl.BlockSpec((16, tc), lambda j, i: (nxt16(i), j))
    in_specs = [out_blk, pl.BlockSpec((tt, tc), lambda j, i: (i, j + nf)),
                pl.BlockSpec((FFN_CONV, tc), lambda j, i: (0, j)), pl.BlockSpec((FFN_CONV, tc), lambda j, i: (0, j + nf)),
                out_blk, nxt_blk, out_blk, nxt_blk, out_blk, nxt_blk]
    return pl.pallas_call(
        body, name="ffn_act_bwd", grid=(nf, nt), in_specs=in_specs,
        out_specs=[out_blk, out_blk, acc_spec, acc_spec],
        out_shape=[jax.ShapeDtypeStruct((t, D_FF), BF16), jax.ShapeDtypeStruct((t, D_FF), BF16),
                   jax.ShapeDtypeStruct((8, D_FF), F32), jax.ShapeDtypeStruct((8, D_FF), F32)],
        scratch_shapes=[pltpu.VMEM((8, 8, tc), F32)],
        compiler_params=_cparams(("parallel", "arbitrary")),
    )(hid, hid, cw, cw, conv_g, conv_g, conv_u, conv_u, dact, dact)


def _tri_mask():
    r = lax.broadcasted_iota(jnp.int32, (CHUNK, CHUNK), 0)
    c = lax.broadcasted_iota(jnp.int32, (CHUNK, CHUNK), 1)
    return r >= c


def _gmlp_group_fwd(uv_ref, lng_ref, lnb_ref, ws_ref, bexp_ref, tri, g, want_grad):
    lo, hi = g * 128, (g + 1) * 128
    u_pre = uv_ref[:, lo:hi]
    v_pre = uv_ref[:, D_A + lo:D_A + hi]
    u, du = _gelu_and_grad(u_pre)
    v, dv = _gelu_and_grad(v_pre)
    mu = jnp.mean(v, axis=-1, keepdims=True)
    dc = v - mu
    rs = lax.rsqrt(jnp.mean(dc * dc, axis=-1, keepdims=True) + EPS)
    xh = dc * rs
    vn = (xh * lng_ref[:, lo:hi] + lnb_ref[:, lo:hi]).astype(BF16)
    w = jnp.where(tri, ws_ref[g], 0.0).astype(BF16)
    sg = _dot(w, vn) + bexp_ref[g]
    if want_grad:
        return u, du, dv, rs, xh, vn, w, sg
    return u * sg


def _gmlp_fwd(proj, ln_g, ln_b, w_s, b_exp, na_g):
    t = proj.shape[0]
    ng = N_GROUPS_A

    def body(uv_ref, lng_ref, lnb_ref, ws_ref, bexp_ref, nag_ref, o_ref):
        tri = _tri_mask()
        ys = [_gmlp_group_fwd(uv_ref, lng_ref, lnb_ref, ws_ref, bexp_ref, tri, g, False) for g in range(ng)]
        ssq = ys[0] * 0.0
        for y in ys:
            ssq = ssq + y * y
        r = lax.rsqrt(jnp.sum(ssq, axis=-1, keepdims=True) * (1.0 / D_A) + EPS)
        for g, y in enumerate(ys):
            o_ref[:, g * 128:(g + 1) * 128] = (y * r * nag_ref[:, g * 128:(g + 1) * 128]).astype(BF16)

    vec = pl.BlockSpec((1, D_A), lambda i: (0, 0))
    cube = pl.BlockSpec((ng, CHUNK, CHUNK), lambda i: (0, 0, 0))
    return pl.pallas_call(
        body, name="gmlp_fwd", grid=(t // CHUNK,),
        in_specs=[pl.BlockSpec((CHUNK, 2 * D_A), lambda i: (i, 0)), vec, vec, cube, cube, vec],
        out_specs=pl.BlockSpec((CHUNK, D_A), lambda i: (i, 0)),
        out_shape=jax.ShapeDtypeStruct((t, D_MIX), BF16),
        compiler_params=_cparams(("parallel",)),
    )(proj, ln_g, ln_b, w_s, b_exp, na_g)


def _gmlp_bwd(proj, dyab, ln_g, ln_b, w_s, b_exp, na_g):
    t = proj.shape[0]
    ng = N_GROUPS_A
    nsteps = t // CHUNK

    def body(uv_ref, dy_ref, lng_ref, lnb_ref, ws_ref, bexp_ref, nag_ref,
             duv_ref, dws_ref, dbs_ref, dlng_ref, dlnb_ref, dnag_ref, dbacc):
        i = pl.program_id(0)
        tri = _tri_mask()

        @pl.when(i == 0)
        def _():
            dws_ref[...] = jnp.zeros_like(dws_ref)
            dbacc[...] = jnp.zeros_like(dbacc)
            dlng_ref[...] = jnp.zeros_like(dlng_ref)
            dlnb_ref[...] = jnp.zeros_like(dlnb_ref)
            dnag_ref[...] = jnp.zeros_like(dnag_ref)

        st = [_gmlp_group_fwd(uv_ref, lng_ref, lnb_ref, ws_ref, bexp_ref, tri, g, True) for g in range(ng)]
        ssq = st[0][0] * 0.0
        for s in st:
            y = s[0] * s[7]
            ssq = ssq + y * y
        r = lax.rsqrt(jnp.sum(ssq, axis=-1, keepdims=True) * (1.0 / D_A) + EPS)
        csum = st[0][0] * 0.0
        for g, s in enumerate(st):
            sl = slice(g * 128, (g + 1) * 128)
            xhy = s[0] * s[7] * r
            dya = dy_ref[:, sl].astype(F32)
            dnag_ref[:, sl] += jnp.sum(dya * xhy, axis=0, keepdims=True)
            csum = csum + dya * nag_ref[:, sl] * xhy
        c1 = jnp.sum(csum, axis=-1, keepdims=True) * (1.0 / D_A)
        for g, s in enumerate(st):
            u, du, dv, rs, xh, vn, w, sg = s
            sl = slice(g * 128, (g + 1) * 128)
            dy = r * (dy_ref[:, sl].astype(F32) * nag_ref[:, sl] - u * sg * r * c1)
            dsg = dy * u
            dsg_b = dsg.astype(BF16)
            dws_ref[g] += _dot_nt(dsg_b, vn)
            dbacc[g] += dsg
            dvn = _dot_tn(w, dsg_b)
            dlnb_ref[:, sl] += jnp.sum(dvn, axis=0, keepdims=True)
            dlng_ref[:, sl] += jnp.sum(dvn * xh, axis=0, keepdims=True)
            dxh = dvn * lng_ref[:, sl]
            dvv = rs * (dxh - jnp.mean(dxh, axis=-1, keepdims=True)
                        - xh * jnp.mean(dxh * xh, axis=-1, keepdims=True))
            duv_ref[:, sl] = (dy * sg * du).astype(BF16)
            duv_ref[:, D_A + g * 128:D_A + (g + 1) * 128] = (dvv * dv).astype(BF16)

        @pl.when(i == nsteps - 1)
        def _():
            for g in range(ng):
                dws_ref[g] = jnp.where(tri, dws_ref[g], 0.0)
                dbs_ref[g] = jnp.sum(dbacc[g], axis=-1, keepdims=True)

    vec = pl.BlockSpec((1, D_A), lambda i: (0, 0))
    cube = pl.BlockSpec((ng, CHUNK, CHUNK), lambda i: (0, 0, 0))
    return pl.pallas_call(
        body, name="gmlp_bwd", grid=(nsteps,),
        in_specs=[pl.BlockSpec((CHUNK, 2 * D_A), lambda i: (i, 0)),
                  pl.BlockSpec((CHUNK, D_A), lambda i: (i, 0)), vec, vec, cube, cube, vec],
        out_specs=[pl.BlockSpec((CHUNK, 2 * D_A), lambda i: (i, 0)), cube,
                   pl.BlockSpec((ng, CHUNK, 1), lambda i: (0, 0, 0)), vec, vec, vec],
        out_shape=[jax.ShapeDtypeStruct((t, 2 * D_A), BF16), jax.ShapeDtypeStruct((ng, CHUNK, CHUNK), F32),
                   jax.ShapeDtypeStruct((ng, CHUNK, 1), F32), jax.ShapeDtypeStruct((1, D_A), F32),
                   jax.ShapeDtypeStruct((1, D_A), F32), jax.ShapeDtypeStruct((1, D_A), F32)],
        scratch_shapes=[pltpu.VMEM((ng, CHUNK, CHUNK), F32)],
        compiler_params=_cparams(("arbitrary",)),
    )(proj, dyab, ln_g, ln_b, w_s, b_exp, na_g)


OFF_Z = 2 * D_A
OFF_XS = OFF_Z + D_SSM
OFF_B = OFF_XS + D_SSM
OFF_C = OFF_B + D_BC


def _ssd_consts():
    tri = jnp.tril(jnp.ones((CHUNK, CHUNK), F32)).astype(BF16)
    h = jnp.arange(HPAD)[None, :, None]
    g = jnp.arange(N_SSM_GROUPS)[:, None, None]
    j1 = jnp.arange(GW)[None, None, :]
    eh = (h == g * HEADS_PER_GROUP + j1 // HEAD_DIM).astype(BF16)
    j2 = jnp.arange(HEADS_PER_GROUP * 128)[None, None, :]
    e128 = (h == g * HEADS_PER_GROUP + j2 // 128).astype(BF16)
    return tri, eh, e128


def _ssd_in_specs(cmap):
    def rows(i):
        return cmap(i)

    def prev8(i):
        return jnp.maximum(cmap(i) * (CHUNK // 8) - 1, 0)

    def whole(*shape):
        return pl.BlockSpec(shape, lambda i: (0,) * len(shape))

    bcw = 2 * D_BC
    specs = [
        pl.BlockSpec((CHUNK, D_SSM), lambda i: (rows(i), OFF_Z // D_SSM)),
        pl.BlockSpec((CHUNK, D_SSM), lambda i: (rows(i), OFF_XS // D_SSM)),
        pl.BlockSpec((8, D_SSM), lambda i: (prev8(i), OFF_XS // D_SSM)),
        pl.BlockSpec((CHUNK, bcw), lambda i: (rows(i), OFF_B // bcw)),
        pl.BlockSpec((8, bcw), lambda i: (prev8(i), OFF_B // bcw)),
        pl.BlockSpec((CHUNK, HPAD), lambda i: (rows(i), 0)),
        whole(SSM_CONV, D_XBC), whole(1, D_XBC),
        whole(1, HPAD), whole(1, HPAD),
        whole(1, D_SSM), whole(1, D_SSM),
        whole(CHUNK, CHUNK),
        whole(N_SSM_GROUPS, HPAD, GW), whole(N_SSM_GROUPS, HPAD, HEADS_PER_GROUP * 128),
    ]
    return specs


N_SSD_IN = 15


def _lanes(ref, start, width):
    return ref.at[:, pl.ds(pl.multiple_of(start, 128), width)]


def _ssd_group_refs(ins, g):
    (z_ref, xs_ref, xsp_ref, bc_ref, bcp_ref, dt_ref, cw_ref, cb_ref, dtb_ref, alog_ref, de_ref, gain_ref,
     tri_ref, eh_ref, e128_ref) = ins
    ox, ob, oc = g * GW, g * D_STATE, D_BC + g * D_STATE
    return (_lanes(z_ref, ox, GW), _lanes(xs_ref, ox, GW), _lanes(xsp_ref, ox, GW),
            _lanes(bc_ref, ob, D_STATE), _lanes(bcp_ref, ob, D_STATE),
            _lanes(bc_ref, oc, D_STATE), _lanes(bcp_ref, oc, D_STATE), dt_ref,
            _lanes(cw_ref, ox, GW), _lanes(cw_ref, D_SSM + ob, D_STATE), _lanes(cw_ref, D_SSM + oc, D_STATE),
            _lanes(cb_ref, ox, GW), _lanes(cb_ref, D_SSM + ob, D_STATE), _lanes(cb_ref, D_SSM + oc, D_STATE),
            dtb_ref, alog_ref, _lanes(de_ref, ox, GW), _lanes(gain_ref, ox, GW), tri_ref,
            eh_ref.at[g], e128_ref.at[g])


N_SSD_OWN = 4


def _ssd_scratch():
    return [pltpu.VMEM((CHUNK + 8, GW), F32), pltpu.VMEM((CHUNK + 8, D_STATE), F32),
            pltpu.VMEM((CHUNK + 8, D_STATE), F32), pltpu.VMEM((CHUNK, GW), F32),
            pltpu.VMEM((HPAD, CHUNK), F32), pltpu.VMEM((CHUNK, HPAD), F32), pltpu.VMEM((CHUNK, HPAD), F32)]


def _ssd_two_sets(alloc):
    a, b = alloc[:7], alloc[7:7 + N_SSD_OWN]
    return ((a[0], a[1], a[2], a[4], a[3], a[5], a[6]), (b[0], b[1], b[2], a[4], b[3], a[5], a[6]))


def _ssd_chunk(ins, scr):
    dt_ref, dtb_ref, alog_ref, tri_ref = ins[5], ins[8], ins[9], ins[12]
    acst_sc, dt_sc, acs_sc = scr[3], scr[5], scr[6]
    dt = _softplus(dt_ref[...] + dtb_ref[...])
    acs = _e01x(tri_ref[...], dt * (-jnp.exp(alog_ref[...])))
    dt_sc[...] = dt
    acs_sc[...] = acs
    acst_sc[...] = acs.T


def _ssd_pre(first, g, refs, scr):
    (z_ref, xs_ref, xsp_ref, b_ref, bp_ref, c_ref, cp_ref, dt_ref, cwx, cwb, cwc, cbx, cbb, cbc,
     dtb_ref, alog_ref, de_ref, gain_ref, tri_ref, eh_ref, e128_ref) = refs
    ext_x, ext_b, ext_c, acst_sc, acse_sc, dt_sc, acs_sc = scr
    p = {}
    px = _conv(xs_ref, xsp_ref, cwx, cbx, ext_x, first, SSM_CONV)
    pb = _conv(b_ref, bp_ref, cwb, cbb, ext_b, first, SSM_CONV)
    pc = _conv(c_ref, cp_ref, cwc, cbc, ext_c, first, SSM_CONV)
    p['xs'], p['dsx'] = _silu_and_grad(px)
    p['bm'], p['dsb'] = _silu_and_grad(pb)
    p['cm'], p['dsc'] = _silu_and_grad(pc)
    p['dt_in'] = dt_ref[...] + dtb_ref[...]
    dt = dt_sc[...]
    p['dt'] = dt
    p['a'] = -jnp.exp(alog_ref[...])
    tri_b = tri_ref[...]
    acs = acs_sc[...]
    eh = eh_ref[...]
    p['eh'] = eh
    p['dt_e'] = _x01(dt, eh)
    acs_e = _x01(acs, eh)
    acse_sc[...] = acs_e
    p['acs_e'] = acs_e
    p['acs_c'] = _x01(acs, e128_ref[...])
    p['acs_last_e'] = acse_sc[pl.ds(CHUNK - 1, 1), :]
    p['xdt'] = p['xs'] * p['dt_e']
    p['decay_e'] = jnp.exp(p['acs_last_e'] - acs_e)
    p['cm_b'] = p['cm'].astype(BF16)
    p['bm_b'] = p['bm'].astype(BF16)
    p['scores'] = _dot_nt(p['cm_b'], p['bm_b'])
    p['tri_b'] = tri_b
    return p


def _ssd_l(p, g, r, acst_sc, tri):
    col = p['acs_c'][:, r * 128:(r + 1) * 128]
    row = acst_sc[pl.ds(g * HEADS_PER_GROUP + r, 1), :]
    return jnp.where(tri, jnp.exp(jnp.minimum(col - row, 0.0)), 0.0)


def _ssd_fwd(proj, dt_raw, yab, cw, cb, dtb, alog, de, gain, consts, hosted):
    t = proj.shape[0]
    nc = t // CHUNK
    ng = N_SSM_GROUPS
    tri_c, eh_c, e128_c = consts
    n_in = N_SSD_IN
    nh_in, nh_out = len(hosted['arrays']), len(hosted['out_shape'])
    mid_chunk = _forward_step(nc)

    def body(*refs):
        ins = refs[:n_in]
        hins = refs[n_in + 1:n_in + 1 + nh_in]
        o0 = n_in + 1 + nh_in
        yb_all, ys_all, hs_all = refs[o0:o0 + 3]
        houts = refs[o0 + 3:o0 + 3 + nh_out]
        h_sc = refs[o0 + 3 + nh_out]
        scr_a, scr_b = _ssd_two_sets(refs[o0 + 4 + nh_out:-2])
        sems = refs[-2:]
        c = pl.program_id(0)
        _host_phase(hosted, 'start', c == 0, hins, houts, sems)
        _host_phase(hosted, 'mid', c == mid_chunk, hins, houts, sems)

        @pl.when(c == 0)
        def _():
            h_sc[...] = jnp.zeros_like(h_sc)

        def group(g, scr):
            grefs = _ssd_group_refs(ins, g)
            z_ref, de_ref, gain_ref = grefs[0], grefs[16], grefs[17]
            yb_ref, ys_ref = _lanes(yb_all, g * GW, GW), _lanes(ys_all, g * GW, GW)
            slab = pl.ds(pl.multiple_of(g * D_STATE, D_STATE), D_STATE)
            p = _ssd_pre(c == 0, g, grefs, scr)
            tri = _tri_mask()
            h_in = h_sc[slab, :]
            hs_all[slab, :] = h_in
            yoff = _dot(p['cm_b'], h_in.astype(BF16)) * jnp.exp(p['acs_e'])
            states = _dot_tn(p['bm_b'], (p['xdt'] * p['decay_e']).astype(BF16))
            lane = lax.broadcasted_iota(jnp.int32, (CHUNK, 128), 1)
            slabs = []
            for r2 in range(HEADS_PER_GROUP // 2):
                xb = p['xdt'][:, r2 * 128:(r2 + 1) * 128].astype(BF16)
                ya = _dot((p['scores'] * _ssd_l(p, g, 2 * r2, scr[3], tri)).astype(BF16), xb)
                yb = _dot((p['scores'] * _ssd_l(p, g, 2 * r2 + 1, scr[3], tri)).astype(BF16), xb)
                slabs.append(jnp.where(lane < HEAD_DIM, ya, yb))
            y = jnp.concatenate(slabs, axis=1) + yoff + de_ref[...] * p['xs']
            ys_ref[...] = y
            h_sc[slab, :] = jnp.exp(p['acs_last_e']) * h_in + states
            zv = z_ref[...]
            yg = y * zv * _sigmoid(zv)
            r = lax.rsqrt(jnp.mean(yg * yg, axis=-1, keepdims=True) + EPS)
            yb_ref[...] = (yg * r * gain_ref[...]).astype(BF16)

        def pair(j, carry):
            group(2 * j, scr_a)
            group(2 * j + 1, scr_b)
            return carry

        _ssd_chunk(ins, scr_a)
        lax.fori_loop(0, ng // 2, pair, 0)
        _host_phase(hosted, 'finish', c == nc - 1, hins, houts, sems)

    h_in, h_out, h_shape, h_scratch, h_alias = _host_plumbing(hosted, n_in + 1, 3)
    in_specs = _ssd_in_specs(lambda i: i) + [pl.BlockSpec(memory_space=pl.ANY)] + h_in
    out_specs = [pl.BlockSpec((CHUNK, D_SSM), lambda i: (i, D_A // D_SSM)),
                 pl.BlockSpec((CHUNK, D_SSM), lambda i: (i, 0)),
                 pl.BlockSpec((None, ng * D_STATE, GW), lambda i: (i, 0, 0))] + h_out
    out_shape = [jax.ShapeDtypeStruct((t, D_MIX), BF16), jax.ShapeDtypeStruct((t, D_SSM), F32),
                 jax.ShapeDtypeStruct((nc, ng * D_STATE, GW), F32)] + h_shape
    return pl.pallas_call(
        body, name="ssd_fwd", grid=(nc,), in_specs=in_specs, out_specs=out_specs, out_shape=out_shape,
        scratch_shapes=[pltpu.VMEM((ng * D_STATE, GW), F32)] + _ssd_scratch() + _ssd_scratch()[:N_SSD_OWN]
        + h_scratch,
        input_output_aliases={n_in: 0, **h_alias},
        compiler_params=_cparams(("arbitrary",)),
    )(proj, proj, proj, proj, proj, dt_raw, cw, cb, dtb, alog, de, gain,
      tri_c, eh_c, e128_c, yab, *hosted['arrays'])


def _rows8(vals, width):
    rid = lax.broadcasted_iota(jnp.int32, (8, width), 0)
    out = jnp.zeros((8, width), F32)
    for k, v in enumerate(vals):
        if v is not None:
            out = out + jnp.where(rid == k, v, 0.0)
    return out


def _ssd_bwd(proj, dt_raw, dyab, ysave, hs, duv, cw, cb, dtb, alog, de, gain, consts, hosted):
    t = proj.shape[0]
    nc = t // CHUNK
    ng = N_SSM_GROUPS
    tri_c, eh_c, e128_c = consts
    n_in = N_SSD_IN
    nh_in, nh_out = len(hosted['arrays']), len(hosted['out_shape'])

    def per_group(g, c, ins, dy_ref, ys_ref, hs_ref, outs, scratch):
        dz_ref, dxs_ref, db_ref, dc_ref, ddt_ref, acc_x, acc_b, acc_c, acc_gain, acc_head = outs
        dh_sc, car_x, car_b, car_c, dext_x, dext_b, dext_c = scratch[:7]
        scr = scratch[7:]
        ext_x, ext_b, ext_c, acst_sc = scr[:4]
        z_ref, cwx, cwb, cwc, de_ref, gain_ref = ins[0], ins[8], ins[9], ins[10], ins[16], ins[17]
        slab = pl.ds(pl.multiple_of(g * D_STATE, D_STATE), D_STATE)
        p = _ssd_pre(c == 0, g, ins, scr)
        tri = _tri_mask()
        xs, dt_e, acs_e, xdt, decay_e = p['xs'], p['dt_e'], p['acs_e'], p['xdt'], p['decay_e']
        cm_b, bm_b, scores, eh = p['cm_b'], p['bm_b'], p['scores'], p['eh']
        h_in = hs_ref[...]
        h_in_b = h_in.astype(BF16)
        e_a = jnp.exp(acs_e)
        raw = _dot(cm_b, h_in_b)

        y = ys_ref[...]
        zv = z_ref[...]
        sz, dsz = _silu_and_grad(zv)
        yg = y * sz
        r = lax.rsqrt(jnp.mean(yg * yg, axis=-1, keepdims=True) + EPS)
        xh = yg * r
        dout = dy_ref[...].astype(F32)
        gain = gain_ref[...]
        dxh = dout * gain
        dyg = r * (dxh - xh * jnp.mean(dxh * xh, axis=-1, keepdims=True))
        dy = dyg * sz
        dz_ref[...] = (dyg * y * dsz).astype(BF16)
        acc_gain[g] += _rows8([jnp.sum(dout * xh, axis=0, keepdims=True)], GW)
        d_skip8 = _x01_nt(_rows8([None, None, jnp.sum(dy * xs, axis=0, keepdims=True)], GW), eh)
        dxs = de_ref[...] * dy

        q = dy * raw * e_a
        draw = (dy * e_a).astype(BF16)
        d_c = _dot_nt(draw, h_in_b)
        dh_in = _dot_tn(cm_b, draw)

        lane = lax.broadcasted_iota(jnp.int32, (CHUNK, 128), 1)
        ones_b = jnp.ones((CHUNK, 128), BF16)
        dscores = jnp.zeros((CHUNK, CHUNK), F32)
        dxdt_slabs, q_slabs = [], []
        for r2 in range(HEADS_PER_GROUP // 2):
            sl = slice(r2 * 128, (r2 + 1) * 128)
            xb = xdt[:, sl].astype(BF16)
            dys = dy[:, sl]
            dys_b = dys.astype(BF16)
            dxh_pair, qv_pair = [], []
            for half in range(2):
                lmat = _ssd_l(p, g, 2 * r2 + half, acst_sc, tri)
                m = scores * lmat
                mine = (lane < HEAD_DIM) if half == 0 else (lane >= HEAD_DIM)
                dm = _dot_nt(jnp.where(mine, dys, 0.0).astype(BF16), xb)
                dscores = dscores + dm * lmat
                gm = dm * m
                dxh_pair.append(_dot_tn(m.astype(BF16), dys_b))
                h3 = _split3(gm)
                colsum = _dot_tn(h3[0], ones_b) + _dot_tn(h3[1], ones_b) + _dot_tn(h3[2], ones_b)
                qv_pair.append(jnp.sum(gm, axis=-1, keepdims=True) - colsum)
            dxdt_slabs.append(jnp.where(lane < HEAD_DIM, dxh_pair[0], dxh_pair[1]))
            q_slabs.append(jnp.where(lane == 0, qv_pair[0], 0.0) + jnp.where(lane == HEAD_DIM, qv_pair[1], 0.0))
        dxdt = jnp.concatenate(dxdt_slabs, axis=1)
        q = q + jnp.concatenate(q_slabs, axis=1)

        dh_out = dh_sc[slab, :]
        dh_out_b = dh_out.astype(BF16)
        e_l = jnp.exp(p['acs_last_e'])
        dh_sc[slab, :] = dh_in + e_l * dh_out
        dlast = jnp.sum(dh_out * h_in, axis=0, keepdims=True) * e_l
        dxd = _dot(bm_b, dh_out_b)
        xd = xdt * decay_e
        dxdt = dxdt + dxd * decay_e
        dd = dxd * xd
        q = q - dd
        dlast = dlast + jnp.sum(dd, axis=0, keepdims=True)
        d_b = _dot_nt(xd.astype(BF16), dh_out_b)
        dsc_b = dscores.astype(BF16)
        d_c = d_c + _dot(dsc_b, bm_b)
        d_b = d_b + _dot_tn(dsc_b, cm_b)

        dxs = dxs + dxdt * dt_e
        rid = lax.broadcasted_iota(jnp.int32, (CHUNK, GW), 0)
        q = q + jnp.where(rid == CHUNK - 1, dlast, 0.0)
        dacs = _x01_nt(q, eh)
        ddt = _x01_nt(dxdt * xs, eh)
        dadt = _e01x_tn(p['tri_b'], dacs)
        ddt = ddt + dadt * p['a']
        d_a = jnp.sum(dadt * p['dt'], axis=0, keepdims=True)
        ddt_raw = ddt * _sigmoid(p['dt_in'])
        acc_head[...] += _rows8([jnp.sum(ddt_raw, axis=0, keepdims=True), d_a * p['a']], HPAD) + d_skip8

        @pl.when(g == 0)
        def _():
            ddt_ref[...] = ddt_raw

        @pl.when(g > 0)
        def _():
            ddt_ref[...] += ddt_raw

        for dv, dsil, ext, dext, car, acc, w_ref, o_ref in (
                (dxs, p['dsx'], ext_x, dext_x, car_x, acc_x, cwx, dxs_ref),
                (d_b, p['dsb'], ext_b, dext_b, car_b, acc_b, cwb, db_ref),
                (d_c, p['dsc'], ext_c, dext_c, car_c, acc_c, cwc, dc_ref)):
            dp = dv * dsil
            width = dp.shape[1]
            rows = [jnp.sum(ext[pl.ds(5 + k, CHUNK), :] * dp, axis=0, keepdims=True) for k in range(SSM_CONV)]
            rows.append(jnp.sum(dp, axis=0, keepdims=True))
            acc[g] += _rows8(rows, width)
            dext[0:CHUNK, :] = dp
            dext[CHUNK:CHUNK + 8, :] = car[g]
            car[g] = dext[0:8, :]
            dx = w_ref[SSM_CONV - 1:SSM_CONV, :] * dext[pl.ds(0, CHUNK), :]
            for k in range(SSM_CONV - 1):
                dx = dx + w_ref[k:k + 1, :] * dext[pl.ds(SSM_CONV - 1 - k, CHUNK), :]
            o_ref[...] = dx.astype(BF16)

    def body(*refs):
        ins = refs[:n_in]
        dy_all, ys_all, hs_all, duv_ref = refs[n_in:n_in + 4]
        hins = refs[n_in + 4:n_in + 4 + nh_in]
        o0 = n_in + 4 + nh_in
        dproj_ref, ddt_ref = refs[o0:o0 + 2]
        accs = refs[o0 + 2:o0 + 7]
        houts = refs[o0 + 7:o0 + 7 + nh_out]
        scratch = refs[o0 + 7 + nh_out:-2]
        sems = refs[-2:]
        cc = pl.program_id(0)
        _host_phase(hosted, 'start', cc == 0, hins, houts, sems)
        dproj_ref[:, 0:2 * D_A] = duv_ref[...]

        @pl.when(cc == 0)
        def _():
            for a in tuple(accs) + tuple(scratch[:4]):
                a[...] = jnp.zeros_like(a)

        shared = tuple(scratch[:4])
        scr_a, scr_b = _ssd_two_sets(tuple(scratch[7:14]) + tuple(scratch[17:17 + N_SSD_OWN]))
        set_a = shared + tuple(scratch[4:7]) + scr_a
        set_b = shared + tuple(scratch[14:17]) + scr_b

        def group(g, own):
            slab = pl.ds(pl.multiple_of(g * D_STATE, D_STATE), D_STATE)
            outs = (_lanes(dproj_ref, OFF_Z + g * GW, GW), _lanes(dproj_ref, OFF_XS + g * GW, GW),
                    _lanes(dproj_ref, OFF_B + g * D_STATE, D_STATE), _lanes(dproj_ref, OFF_C + g * D_STATE, D_STATE),
                    ddt_ref) + tuple(accs)
            per_group(g, nc - 1 - cc, _ssd_group_refs(ins, g), _lanes(dy_all, g * GW, GW),
                      _lanes(ys_all, g * GW, GW), hs_all.at[slab, :], outs, own)

        def one(g, carry):
            group(g, set_a)
            return carry

        _ssd_chunk(ins, scr_a)
        lax.fori_loop(0, ng, one, 0)
        _host_phase(hosted, 'finish', cc == nc - 1, hins, houts, sems)

    def cmap(i):
        return nc - 1 - i

    in_specs = _ssd_in_specs(cmap) + [
        pl.BlockSpec((CHUNK, D_SSM), lambda i: (cmap(i), D_A // D_SSM)),
        pl.BlockSpec((CHUNK, D_SSM), lambda i: (cmap(i), 0)),
        pl.BlockSpec((None, ng * D_STATE, GW), lambda i: (cmap(i), 0, 0)),
        pl.BlockSpec((CHUNK, 2 * D_A), lambda i: (cmap(i), 0)),
    ]

    def full(shape):
        return pl.BlockSpec(shape, lambda i: (0,) * len(shape))

    out_specs = [
        pl.BlockSpec((CHUNK, D_MAIN), lambda i: (cmap(i), 0)),
        pl.BlockSpec((CHUNK, HPAD), lambda i: (cmap(i), 0)),
        full((ng, 8, GW)), full((ng, 8, D_STATE)), full((ng, 8, D_STATE)),
        full((ng, 8, GW)), full((8, HPAD)),
    ]
    out_shape = [
        jax.ShapeDtypeStruct((t, D_MAIN), BF16),
        jax.ShapeDtypeStruct((t, HPAD), F32),
        jax.ShapeDtypeStruct((ng, 8, GW), F32), jax.ShapeDtypeStruct((ng, 8, D_STATE), F32),
        jax.ShapeDtypeStruct((ng, 8, D_STATE), F32), jax.ShapeDtypeStruct((ng, 8, GW), F32),
        jax.ShapeDtypeStruct((8, HPAD), F32),
    ]
    scratch = [pltpu.VMEM((ng * D_STATE, GW), F32),
               pltpu.VMEM((ng, 8, GW), F32), pltpu.VMEM((ng, 8, D_STATE), F32), pltpu.VMEM((ng, 8, D_STATE), F32),
               pltpu.VMEM((CHUNK + 8, GW), F32), pltpu.VMEM((CHUNK + 8, D_STATE), F32),
               pltpu.VMEM((CHUNK + 8, D_STATE), F32)] + _ssd_scratch()
    scratch += scratch[4:7] + _ssd_scratch()[:N_SSD_OWN]
    h_in, h_out, h_shape, h_scratch, h_alias = _host_plumbing(hosted, n_in + 4, len(out_shape))
    return pl.pallas_call(
        body, name="ssd_bwd", grid=(nc,), in_specs=in_specs + h_in, out_specs=out_specs + h_out,
        out_shape=out_shape + h_shape, scratch_shapes=scratch + h_scratch, input_output_aliases=h_alias,
        compiler_params=_cparams(("arbitrary",)),
    )(proj, proj, proj, proj, proj, dt_raw, cw, cb, dtb, alog, de, gain,
      tri_c, eh_c, e128_c, dyab, ysave, hs, duv, *hosted['arrays'])


ANY = pl.BlockSpec(memory_space=pl.ANY)


def _place():
    x, y, c = lax.axis_index("x"), lax.axis_index("y"), lax.axis_index("c")
    chips = [(1 - x, y), (x, 1 - y), (1 - x, 1 - y)]
    return x, y, c, chips


def _rcopy(src, dst, send_sems, recv_sems, k, dev):
    return pltpu.make_async_remote_copy(src_ref=src, dst_ref=dst, send_sem=send_sems.at[k],
                                        recv_sem=recv_sems.at[k], device_id=dev, device_id_type=MESH)


def _my_chip():
    return 2 * lax.axis_index("x") + lax.axis_index("y")


def _cast_into_slot(w, name):
    r, c = w.shape
    tr = _row_tile(r, c, 2)

    def body(w_ref, o_ref):
        o_ref[...] = w_ref[...].astype(BF16)

    return pl.pallas_call(
        body, name=name, grid=(r // tr,), in_specs=[pl.BlockSpec((tr, c), lambda i: (i, 0))],
        out_specs=pl.BlockSpec((None, tr, c), lambda i: (_my_chip(), i, 0)),
        out_shape=jax.ShapeDtypeStruct((N_CHIPS, r, c), BF16), compiler_params=_cparams(("parallel",)),
    )(w)


def _hosted_gather(bigs):
    nb = len(bigs)

    def rows(a, c):
        half = bigs[a].shape[1] // 2
        return pl.ds(c * half, half)

    def start(ins, outs, send_sems, recv_sems):
        x, y, c, chips = _place()
        q = 2 * x + y
        for a in range(nb):
            for k, chip in enumerate(chips):
                _rcopy(outs[a].at[q, rows(a, c)], outs[a].at[q, rows(a, c)], send_sems, recv_sems, 6 * a + k,
                       (chip[0], chip[1], c)).start()

    def mid(ins, outs, send_sems, recv_sems):
        x, y, c, chips = _place()
        sib = (x, y, 1 - c)
        for a in range(nb):
            for k, chip in enumerate(chips):
                slab = outs[a].at[2 * chip[0] + chip[1], rows(a, c)]
                _rcopy(slab, slab, send_sems, recv_sems, 6 * a + k, sib).wait_recv()
                _rcopy(slab, slab, send_sems, recv_sems, 6 * a + 3 + k, sib).start()

    def finish(ins, outs, send_sems, recv_sems):
        x, y, c, chips = _place()
        q = 2 * x + y
        sib = (x, y, 1 - c)
        for a in range(nb):
            for k, chip in enumerate(chips):
                qk = 2 * chip[0] + chip[1]
                other = outs[a].at[qk, rows(a, 1 - c)]
                _rcopy(other, other, send_sems, recv_sems, 6 * a + 3 + k, sib).wait_recv()
                mine = outs[a].at[q, rows(a, c)]
                _rcopy(mine, mine, send_sems, recv_sems, 6 * a + k, sib).wait_send()
                fwd = outs[a].at[qk, rows(a, c)]
                _rcopy(fwd, fwd, send_sems, recv_sems, 6 * a + 3 + k, sib).wait_send()

    return dict(arrays=list(bigs), out_shape=[jax.ShapeDtypeStruct(b.shape, b.dtype) for b in bigs],
                aliases={a: a for a in range(nb)}, nsem=6 * nb, start=start, mid=mid, finish=finish)


def _hosted_rs_chips(ps):
    na = len(ps)

    def copies(ins, outs, send_sems, recv_sems):
        x, y, c, chips = _place()
        return [_rcopy(ins[a].at[2 * chip[0] + chip[1]], outs[a].at[k], send_sems, recv_sems, 3 * a + k,
                       (chip[0], chip[1], c)) for a in range(na) for k, chip in enumerate(chips)]

    def start(ins, outs, send_sems, recv_sems):
        for cp in copies(ins, outs, send_sems, recv_sems):
            cp.start()

    def finish(ins, outs, send_sems, recv_sems):
        for cp in copies(ins, outs, send_sems, recv_sems):
            cp.wait()

    return dict(arrays=list(ps), out_shape=[jax.ShapeDtypeStruct((3,) + p.shape[1:], p.dtype) for p in ps],
                aliases={}, nsem=3 * na, start=start, mid=None, finish=finish)


def _hosted_rs_sibling(gs):
    na = len(gs)

    def copies(ins, outs, send_sems, recv_sems):
        x, y, c, _ = _place()
        halves = [g.shape[1] // 2 for g in gs]
        return [_rcopy(ins[a].at[:, pl.ds((1 - c) * halves[a], halves[a]), :], outs[a], send_sems, recv_sems, a,
                       (x, y, 1 - c)) for a in range(na)]

    def start(ins, outs, send_sems, recv_sems):
        for cp in copies(ins, outs, send_sems, recv_sems):
            cp.start()

    def finish(ins, outs, send_sems, recv_sems):
        for cp in copies(ins, outs, send_sems, recv_sems):
            cp.wait()

    return dict(arrays=list(gs), aliases={}, nsem=na, start=start, mid=None, finish=finish,
                out_shape=[jax.ShapeDtypeStruct((N_CHIPS, g.shape[1] // 2, g.shape[2]), g.dtype) for g in gs])


def _hosted_share_sibling(fs):
    na = len(fs)

    def rows(a, c):
        half = fs[a].shape[0] // 2
        return pl.ds(c * half, half)

    def start(ins, outs, send_sems, recv_sems):
        x, y, c, _ = _place()
        for a in range(na):
            _rcopy(outs[a].at[rows(a, c)], outs[a].at[rows(a, c)], send_sems, recv_sems, a, (x, y, 1 - c)).start()

    def finish(ins, outs, send_sems, recv_sems):
        x, y, c, _ = _place()
        for a in range(na):
            _rcopy(outs[a].at[rows(a, c)], outs[a].at[rows(a, c)], send_sems, recv_sems, a, (x, y, 1 - c)).wait_send()
            other = outs[a].at[rows(a, 1 - c)]
            _rcopy(other, other, send_sems, recv_sems, a, (x, y, 1 - c)).wait_recv()

    return dict(arrays=list(fs), out_shape=[jax.ShapeDtypeStruct(f.shape, f.dtype) for f in fs],
                aliases={a: a for a in range(na)}, nsem=na, start=start, mid=None, finish=finish)


def _hosted_allgather(buf):
    def copies(ins, outs, send_sems, recv_sems):
        x, y, c, _ = _place()
        me = 4 * x + 2 * y + c
        cps = []
        for k in range(1, 8):
            dev = (1 - x if k & 4 else x, 1 - y if k & 2 else y, 1 - c if k & 1 else c)
            cps.append(_rcopy(ins[0], outs[0].at[me], send_sems, recv_sems, k - 1, dev))
        return pltpu.make_async_copy(ins[0], outs[0].at[me], send_sems.at[7]), cps

    def start(ins, outs, send_sems, recv_sems):
        loc, cps = copies(ins, outs, send_sems, recv_sems)
        loc.start()
        for cp in cps:
            cp.start()

    def finish(ins, outs, send_sems, recv_sems):
        loc, cps = copies(ins, outs, send_sems, recv_sems)
        loc.wait()
        for cp in cps:
            cp.wait()

    return dict(arrays=[buf], out_shape=[jax.ShapeDtypeStruct((8,) + buf.shape, buf.dtype)], aliases={},
                nsem=8, start=start, mid=None, finish=finish)


class _SemWindow:
    def __init__(self, sems, off):
        self._sems, self._off = sems, off

    @property
    def at(self):
        return self

    def __getitem__(self, k):
        return self._sems.at[k + self._off]


def _hosted_join(parts):
    arrays, out_shape, aliases, spans, nsem = [], [], {}, [], 0
    for h in parts:
        spans.append((len(arrays), len(h['arrays']), len(out_shape), len(h['out_shape']), nsem))
        aliases.update({len(arrays) + a: len(out_shape) + b for a, b in h['aliases'].items()})
        arrays += h['arrays']
        out_shape += h['out_shape']
        nsem += h['nsem']

    def phase(name):
        if all(h[name] is None for h in parts):
            return None

        def run(ins, outs, send_sems, recv_sems):
            for h, (ia, na, io, no, s0) in zip(parts, spans):
                if h[name] is not None:
                    h[name](ins[ia:ia + na], outs[io:io + no], _SemWindow(send_sems, s0), _SemWindow(recv_sems, s0))

        return run

    return dict(arrays=arrays, out_shape=out_shape, aliases=aliases, nsem=nsem,
                start=phase('start'), mid=phase('mid'), finish=phase('finish'))


def _run_hosted(hosted, name):
    nh_in, nh_out = len(hosted['arrays']), len(hosted['out_shape'])

    def body(*refs):
        ins, outs, sems = refs[:nh_in], refs[nh_in:nh_in + nh_out], refs[-2:]
        for ph in ('start', 'mid', 'finish'):
            if hosted[ph] is not None:
                hosted[ph](ins, outs, sems[0], sems[1])

    h_in, h_out, h_shape, h_scratch, h_alias = _host_plumbing(hosted, 0, 0)
    return pl.pallas_call(body, name=name, in_specs=h_in, out_specs=h_out, out_shape=h_shape,
                          scratch_shapes=h_scratch, input_output_aliases=h_alias)(*hosted['arrays'])


def _host_plumbing(hosted, n_in, n_out):
    nh = len(hosted['arrays'])
    return ([ANY] * nh, [ANY] * len(hosted['out_shape']), list(hosted['out_shape']),
            [pltpu.SemaphoreType.DMA((hosted['nsem'],)), pltpu.SemaphoreType.DMA((hosted['nsem'],))],
            {n_in + a: n_out + b for a, b in hosted['aliases'].items()})


def _host_phase(hosted, phase, when, hins, houts, sems):
    fn = hosted[phase]
    if fn is None:
        return

    @pl.when(when)
    def _():
        fn(hins, houts, sems[0], sems[1])


def _gather_weights(bigs, smalls):
    nb, ns = len(bigs), len(smalls)
    na = nb + ns
    nsem = 6 * nb + 3 * ns
    big = _hosted_gather(bigs)

    def body(*refs):
        ins, outs = refs[:na], refs[na:2 * na]
        send_sems, recv_sems, loc_sems = refs[2 * na:]
        x, y, c, chips = _place()
        q = 2 * x + y
        sib = (x, y, 1 - c)
        locs, sends = [], []
        for s in range(ns):
            cp = pltpu.make_async_copy(ins[nb + s], outs[nb + s].at[q], loc_sems.at[s])
            cp.start()
            locs.append(cp)
        big['start'](ins[:nb], outs[:nb], send_sems, recv_sems)
        for s in range(ns):
            a = nb + s
            for k, chip in enumerate(chips):
                cp = _rcopy(ins[a], outs[a].at[q], send_sems, recv_sems, 6 * nb + 3 * s + k,
                            (chip[0], chip[1], c))
                cp.start()
                sends.append(cp)
        big['mid'](ins[:nb], outs[:nb], send_sems, recv_sems)
        big['finish'](ins[:nb], outs[:nb], send_sems, recv_sems)
        for s in range(ns):
            a = nb + s
            for k, chip in enumerate(chips):
                qk = 2 * chip[0] + chip[1]
                _rcopy(ins[a], outs[a].at[qk], send_sems, recv_sems, 6 * nb + 3 * s + k, sib).wait_recv()
        for cp in sends:
            cp.wait_send()
        for cp in locs:
            cp.wait()

    arrs = list(bigs) + list(smalls)
    out_shape = ([jax.ShapeDtypeStruct(a.shape, a.dtype) for a in bigs]
                 + [jax.ShapeDtypeStruct((N_CHIPS,) + a.shape, a.dtype) for a in smalls])
    return pl.pallas_call(
        body, name="gather_weights", in_specs=[ANY] * na, out_specs=[ANY] * na, out_shape=out_shape,
        input_output_aliases={a: a for a in range(nb)},
        scratch_shapes=[pltpu.SemaphoreType.DMA((nsem,)), pltpu.SemaphoreType.DMA((nsem,)),
                        pltpu.SemaphoreType.DMA((max(ns, 1),))],
    )(*arrs)


EW_VMEM_BUDGET = 24 * 1024 * 1024


def _row_tile(rows, cols, nbuf):
    budget = EW_VMEM_BUDGET // (nbuf * cols * 4 * 2)
    return _tile(rows, max(16, budget - budget % 16), 16) if rows % 16 == 0 else rows


def _add_pairs(g, rcv, name):
    s, half, c = rcv.shape
    tr = _row_tile(half, c, 3)
    nh = half // tr

    def body(a_ref, b_ref, o_ref):
        o_ref[...] = (a_ref[...].astype(F32) + b_ref[...].astype(F32)).astype(o_ref.dtype)

    blk = pl.BlockSpec((None, tr, c), lambda j, i: (j, i, 0))
    mine = pl.BlockSpec((None, tr, c), lambda j, i: (j, lax.axis_index("c") * nh + i, 0))
    return pl.pallas_call(
        body, name=name, grid=(s, nh), in_specs=[mine, blk], out_specs=blk,
        out_shape=jax.ShapeDtypeStruct(rcv.shape, rcv.dtype), compiler_params=_cparams(("parallel", "parallel")),
    )(g, rcv)


def _sum_chips(part, rcv, name):
    _, half, c = part.shape
    tr = _row_tile(half, c, 5)
    nh = half // tr

    def body(o_ref, r_ref, out_ref):
        acc = o_ref[...].astype(F32)
        for k in range(3):
            acc = acc + r_ref[k].astype(F32)
        out_ref[...] = acc

    return pl.pallas_call(
        body, name=name, grid=(nh,),
        in_specs=[pl.BlockSpec((None, tr, c), lambda i: (_my_chip(), i, 0)),
                  pl.BlockSpec((3, tr, c), lambda i: (0, i, 0))],
        out_specs=pl.BlockSpec((tr, c), lambda i: (lax.axis_index("c") * nh + i, 0)),
        out_shape=jax.ShapeDtypeStruct((2 * half, c), F32), compiler_params=_cparams(("parallel",)),
    )(part, rcv)


def _sum_devices(parts, name):
    _, n, _ = parts.shape
    tr = n if n <= 4096 else _tile(n, 512, 8)

    def body(p_ref, o_ref):
        acc = p_ref[0]
        for k in range(1, 8):
            acc = acc + p_ref[k]
        o_ref[...] = acc

    return pl.pallas_call(
        body, name=name, grid=(n // tr,),
        in_specs=[pl.BlockSpec((8, tr, 128), lambda i: (0, i, 0))],
        out_specs=pl.BlockSpec((tr, 128), lambda i: (i, 0)),
        out_shape=jax.ShapeDtypeStruct((n, 128), F32), compiler_params=_cparams(("parallel",)),
    )(parts)


def _adamw(w, g, m, v, name):
    r, c = w.shape
    tr = _row_tile(r, c, 7)
    c1 = 1.0 - ADAM_B1 ** ADAM_STEP
    c2 = 1.0 - ADAM_B2 ** ADAM_STEP

    def body(w_ref, g_ref, m_ref, v_ref, d_ref, mo_ref, vo_ref):
        gv = g_ref[...]
        mn = ADAM_B1 * m_ref[...] + (1.0 - ADAM_B1) * gv
        vn = ADAM_B2 * v_ref[...] + (1.0 - ADAM_B2) * (gv * gv)
        mo_ref[...] = mn
        vo_ref[...] = vn
        m_hat = mn / c1
        v_hat = vn / c2
        d_ref[...] = -ADAM_LR * (m_hat / (jnp.sqrt(v_hat) + ADAM_EPS) + ADAM_WD * w_ref[...])

    blk = pl.BlockSpec((tr, c), lambda i: (i, 0))
    sh = jax.ShapeDtypeStruct((r, c), F32)
    return pl.pallas_call(
        body, name=name, grid=(r // tr,), in_specs=[blk] * 4, out_specs=[blk] * 3, out_shape=[sh] * 3,
        compiler_params=_cparams(("parallel",)),
    )(w, g, m, v)


WEIGHTS = ['norm_mix_g', 'w_in', 'ln_a_g', 'ln_a_b', 'w_s', 'b_s', 'norm_a_g', 'conv_ssm_w', 'conv_ssm_b',
           'dt_bias', 'a_log', 'd_skip', 'ssm_norm_g', 'w_out', 'norm_ffn_g', 'w_up', 'conv_ffn_w',
           'conv_ffn_b', 'w_down', 'norm_ple_g', 'w_ple_gate', 'w_ple', 'norm_final_g']
BIG = ['w_in', 'w_out', 'w_up', 'w_down', 'w_ple_gate', 'w_ple']
SMALL = [n for n in WEIGHTS if n not in BIG]
PACK_ALIGN = 2048


def _pack(arrs):
    parts = []
    for a in arrs:
        f = a.reshape(-1).astype(F32)
        parts.append(jnp.pad(f, (0, (-f.shape[0]) % PACK_ALIGN)))
    return jnp.concatenate(parts).reshape(-1, 128)


def _unpack(buf, shapes):
    flat = buf.reshape(-1)
    out, off = [], 0
    for s in shapes:
        n = math.prod(s)
        out.append(flat[off:off + n].reshape(s))
        off += n + (-n) % PACK_ALIGN
    return out


def _pad_heads(v):
    return jnp.pad(v, ((0, 0), (0, HPAD - v.shape[1])))


def _col_sharded(full):
    r, c4 = full.shape
    return jnp.transpose(full.reshape(r, N_CHIPS, c4 // N_CHIPS), (1, 0, 2))


def _from_col_sharded(g):
    s, r, c = g.shape
    return jnp.transpose(g, (1, 0, 2)).reshape(r, s * c)


def kernel(x, p, norm_mix_g, w_in, ln_a_g, ln_a_b, w_s, b_s, norm_a_g, conv_ssm_w, conv_ssm_b, dt_bias, a_log, d_skip, ssm_norm_g, w_out, norm_ffn_g, w_up, conv_ffn_w, conv_ffn_b, w_down, norm_ple_g, w_ple_gate, w_ple, norm_final_g, loss_target, m_norm_mix_g, m_w_in, m_ln_a_g, m_ln_a_b, m_w_s, m_b_s, m_norm_a_g, m_conv_ssm_w, m_conv_ssm_b, m_dt_bias, m_a_log, m_d_skip, m_ssm_norm_g, m_w_out, m_norm_ffn_g, m_w_up, m_conv_ffn_w, m_conv_ffn_b, m_w_down, m_norm_ple_g, m_w_ple_gate, m_w_ple, m_norm_final_g, v_norm_mix_g, v_w_in, v_ln_a_g, v_ln_a_b, v_w_s, v_b_s, v_norm_a_g, v_conv_ssm_w, v_conv_ssm_b, v_dt_bias, v_a_log, v_d_skip, v_ssm_norm_g, v_w_out, v_norm_ffn_g, v_w_up, v_conv_ffn_w, v_conv_ffn_b, v_w_down, v_norm_ple_g, v_w_ple_gate, v_w_ple, v_norm_final_g):
    given = dict(locals())
    wts = {n: given[n] for n in WEIGHTS}
    mom = {n: given['m_' + n] for n in WEIGHTS}
    var = {n: given['v_' + n] for n in WEIGHTS}
    d = D_MODEL
    xt, pt, tgt = x[0], p[0, 0], loss_target[0]
    chip = 2 * lax.axis_index("x") + lax.axis_index("y")

    slots = {n: _cast_into_slot(wts[n][0], "cast_" + n) for n in BIG}
    g_in, g_cs, g_cf = _gather_weights([slots['w_in']], [conv_ssm_w[0], conv_ffn_w[0]])
    w_in_full = _from_col_sharded(g_in)
    w_main = w_in_full
    w_dt = _pad_heads(w_in_full[:, D_MAIN:])
    cs_w = _from_col_sharded(g_cs)
    cf_w = _from_col_sharded(g_cf)
    consts = _ssd_consts()
    dtb, alog = _pad_heads(dt_bias), _pad_heads(a_log)
    de = jnp.repeat(d_skip[0], HEAD_DIM)[None, :]
    b_exp = jnp.broadcast_to(b_s[0][:, :, None], (N_GROUPS_A, CHUNK, CHUNK))

    a1 = _rms_fwd(xt, norm_mix_g, "rms_mix")
    proj, g_out = _matmul(a1, w_main, mode='nn', name="mm_proj", tm=1024, tn=1024, b_cols=D_MAIN,
                          hosted=_hosted_gather([slots['w_out']]))
    dt_raw = _matmul(a1, w_dt, mode='nn', name="mm_dt", tm=1024, tn=128)
    yab = _gmlp_fwd(proj, ln_a_g, ln_a_b, w_s[0], b_exp, norm_a_g)
    yab, ysave, hs, g_up = _ssd_fwd(proj, dt_raw, yab, cs_w, conv_ssm_b, dtb, alog, de, ssm_norm_g, consts,
                                    _hosted_gather([slots['w_up']]))
    w_out_f = g_out.reshape(D_MIX, d)
    h1 = _matmul(yab, w_out_f, mode='nn', name="mm_out", res=xt, tm=512, tn=1024, tk=4096)
    f = _rms_fwd(h1, norm_ffn_g, "rms_ffn")
    hid, g_down, g_pg, g_ple = _matmul(
        f, g_up, mode='nn', name="mm_up", b_sharded=True, tm=1024, tn=1408,
        hosted=_hosted_gather([slots[n] for n in ('w_down', 'w_ple_gate', 'w_ple')]))
    w_down_f = g_down.reshape(D_FF, d)
    w_pg_f = g_pg.reshape(d, d)
    act, conv_g, conv_u = _ffn_act_fwd(hid, cf_w, conv_ffn_b)
    h2 = _matmul(act, w_down_f, mode='nn', name="mm_down", res=h1, tm=1024, tn=1024, tk=2816)
    n3 = _rms_fwd(h2, norm_ple_g, "rms_ple")
    gl = _matmul(n3, w_pg_f, mode='nn', name="mm_pg", tm=1024, tn=1024)
    pe = _matmul(pt, g_ple, mode='nn', name="mm_ple", b_sharded=True, tm=1024, tn=512)
    dh3, dgl, dpe, lossv, dgf = _tail(h2, gl, pe, tgt, norm_final_g[None, :])

    gs_ple = _matmul(pt, dpe, mode='tn', name="mm_dw_ple", out_dtype=BF16, out_shards=N_CHIPS,
                     tm=256, tn=512, tk=2048)
    gs_pg = _matmul(n3, dgl, mode='tn', name="mm_dw_pg", out_dtype=BF16, tm=1024, tn=1024, tk=4096)
    dn3 = _matmul(dgl, w_pg_f, mode='nt', name="mm_dn3", out_dtype=BF16, tm=1024, tn=1024)
    dh2, dg_ple, dh2_b = _rms_bwd(h2, norm_ple_g, dn3, dh3, "rms_ple_bwd", True)
    dact = _matmul(dh2_b, w_down_f, mode='nt', name="mm_dact", out_dtype=BF16, tm=1024, tn=1408)
    gs_down = _matmul(act, dh2_b, mode='tn', name="mm_dw_down", out_dtype=BF16, tm=1408, tn=1024, tk=2048)
    dpg, dpu, wg_acc, wu_acc = _ffn_act_bwd(hid, cf_w, conv_g, conv_u, dact)
    hc = N_CHIPS // 2
    gs_up = _matmul(f, dpg, mode='tn', name="mm_dw_up_g", out_dtype=BF16, out_shards=hc, out_total=N_CHIPS,
                    tm=1024, tn=1408, tk=2048)
    gs_up = _matmul(f, dpu, mode='tn', name="mm_dw_up_u", out_dtype=BF16, out_shards=hc, tm=1024, tn=1408, tk=2048,
                    out_into=(gs_up, hc))
    early = [gs_up, gs_down.reshape(N_CHIPS, D_FF // N_CHIPS, d), gs_pg.reshape(N_CHIPS, d // N_CHIPS, d), gs_ple]
    df, *sib_e = _matmul(dpg, g_up, mode='nt', name="mm_df_g", b_sharded=True, tm=1024, tn=1024, tk=2816,
                         hosted=_hosted_rs_sibling(early))
    part_e = [_add_pairs(a, b, "rs_add_e%d" % i) for i, (a, b) in enumerate(zip(early, sib_e))]
    df = _matmul(dpu, g_up, mode='nt', name="mm_df_u", b_sharded=True, b_shard_off=hc, res=df, out_dtype=BF16,
                 tm=1024, tn=1024, tk=2816)
    dh1, dg_ffn, dh1_b = _rms_bwd(h1, norm_ffn_g, df, dh2, "rms_ffn_bwd", True)
    dyab = _matmul(dh1_b, w_out_f, mode='nt', name="mm_dyab", out_dtype=BF16, tm=1024, tn=1024)
    gs_out = _matmul(yab, dh1_b, mode='tn', name="mm_dw_out", out_dtype=BF16, tm=1024, tn=1024, tk=4096)
    duv, dws, dbs, dlng, dlnb, dnag = _gmlp_bwd(proj, dyab, ln_a_g, ln_a_b, w_s[0], b_exp, norm_a_g)
    dproj, ddt_raw, acc_x, acc_b, acc_c, acc_gain, acc_head, *rcv_e = _ssd_bwd(
        proj, dt_raw, dyab, ysave, hs, duv, cs_w, conv_ssm_b, dtb, alog, de, ssm_norm_g, consts,
        _hosted_rs_chips(part_e))
    dw_main = _matmul(a1, dproj, mode='tn', name="mm_dw_main", out_dtype=BF16, tm=1024, tn=1024, tk=4096)
    dw_dt = _matmul(a1, ddt_raw, mode='tn', name="mm_dw_dt", out_dtype=BF16, tm=1024, tn=128, tk=2048)
    gs_in = _col_sharded(jnp.concatenate([dw_main, dw_dt[:, :N_HEADS]], axis=1))
    late = [gs_in, gs_out.reshape(N_CHIPS, D_MIX // N_CHIPS, d)]
    da_dt, *sib_l = _matmul(ddt_raw, w_dt, mode='nt', name="mm_da_dt", tm=1024, tn=1024,
                            hosted=_hosted_rs_sibling(late))
    part_l = [_add_pairs(a, b, "rs_add_l%d" % i) for i, (a, b) in enumerate(zip(late, sib_l))]

    def conv_rows(acc, k):
        return acc[:, k, :].reshape(1, -1)

    dcw = jnp.concatenate([jnp.concatenate([conv_rows(acc_x, k), conv_rows(acc_b, k), conv_rows(acc_c, k)], axis=1)
                           for k in range(SSM_CONV)], axis=0)
    dcb = jnp.concatenate([conv_rows(acc_x, SSM_CONV), conv_rows(acc_b, SSM_CONV), conv_rows(acc_c, SSM_CONV)], axis=1)
    part = {
        'ln_a_g': dlng, 'ln_a_b': dlnb, 'w_s': dws, 'b_s': dbs, 'norm_a_g': dnag,
        'conv_ssm_w': dcw, 'conv_ssm_b': dcb,
        'dt_bias': acc_head[0:1, :N_HEADS], 'a_log': acc_head[1:2, :N_HEADS], 'd_skip': acc_head[2:3, :N_HEADS],
        'ssm_norm_g': acc_gain[:, 0, :], 'norm_ffn_g': dg_ffn,
        'conv_ffn_w': jnp.concatenate([wg_acc[:FFN_CONV], wu_acc[:FFN_CONV]], axis=1),
        'conv_ffn_b': jnp.concatenate([wg_acc[FFN_CONV:FFN_CONV + 1], wu_acc[FFN_CONV:FFN_CONV + 1]], axis=1),
        'norm_ple_g': dg_ple, 'norm_final_g': dgf,
    }
    full_shapes = {n: wts[n].shape for n in SMALL}
    full_shapes['conv_ssm_w'] = (1, SSM_CONV, D_XBC)
    full_shapes['conv_ffn_w'] = (1, FFN_CONV, 2 * D_FF)
    small_e = [n for n in SMALL if n != 'norm_mix_g']
    packed = _pack([part[n] for n in small_e] + [lossv[:, 0:1]])

    halves_e = [_sum_chips(a, b, "rs_sum_e%d" % i) for i, (a, b) in enumerate(zip(part_e, rcv_e))]
    da, *moved = _matmul(dproj, w_main, mode='nt', name="mm_da", res=da_dt, out_dtype=BF16, tm=1024, tn=1024, tk=2560,
                         hosted=_hosted_join([_hosted_rs_chips(part_l), _hosted_share_sibling(halves_e),
                                              _hosted_allgather(packed)]))
    rcv_l, g_early, gathered = moved[:len(late)], moved[len(late):-1], moved[-1]
    dx, dg_mix = _rms_bwd(xt, norm_mix_g, da, dh1, "rms_mix_bwd", False)

    halves_l = [_sum_chips(a, b, "rs_sum_l%d" % i) for i, (a, b) in enumerate(zip(part_l, rcv_l))]
    g_big = dict(zip(['w_up', 'w_down', 'w_ple_gate', 'w_ple'], g_early))
    g_big.update(zip(['w_in', 'w_out'], _run_hosted(_hosted_share_sibling(halves_l), "share_sibling_late")))

    pieces = _unpack(_sum_devices(gathered, "sum_devices"), [full_shapes[n] for n in small_e] + [(1,)])
    g_small = dict(zip(small_e, pieces[:-1]))
    loss = pieces[-1][0]
    mix = _sum_devices(_run_hosted(_hosted_allgather(_pack([dg_mix])), "allgather_mix")[0], "sum_devices_mix")
    g_small['norm_mix_g'] = _unpack(mix, [full_shapes['norm_mix_g']])[0]
    for n in ('conv_ssm_w', 'conv_ffn_w'):
        width = wts[n].shape[2]
        g_small[n] = lax.dynamic_slice_in_dim(g_small[n], chip * width, width, axis=2)

    grads, delta, new_m, new_v = {}, {}, {}, {}
    for n in BIG:
        shp = wts[n].shape
        dl, mn, vn = _adamw(wts[n][0], g_big[n], mom[n][0], var[n][0], "adamw_" + n)
        grads[n], delta[n], new_m[n], new_v[n] = (g_big[n].reshape(shp), dl.reshape(shp), mn.reshape(shp),
                                                  vn.reshape(shp))
    shapes = [wts[n].shape for n in SMALL]
    dl, mn, vn = _adamw(_pack([wts[n] for n in SMALL]), _pack([g_small[n] for n in SMALL]),
                        _pack([mom[n] for n in SMALL]), _pack([var[n] for n in SMALL]), "adamw_small")
    for n, a, b, c in zip(SMALL, _unpack(dl, shapes), _unpack(mn, shapes), _unpack(vn, shapes)):
        grads[n], delta[n], new_m[n], new_v[n] = g_small[n], a, b, c

    return (loss, dx[None], *[grads[n] for n in WEIGHTS], *[delta[n] for n in WEIGHTS],
            *[new_m[n] for n in WEIGHTS], *[new_v[n] for n in WEIGHTS])
```
